```python
import jax, jax.numpy as jnp
from jax import lax
import numpy as np

D_MODEL = 2048
BATCH = 8
SEQ = 8192
DEPTH = 2

CHUNK = 64
N_MEM = 256
EXPAND = 2
MIX_WIDTH = EXPAND * D_MODEL
W_A = MIX_WIDTH // 2
HEAD_DIM_A = 128
N_HEADS_A = W_A // HEAD_DIM_A
N_PAST_CHUNKS = 8
MAX_REL = 128
W_B = MIX_WIDTH - W_A
CONV_WIDTH = 31
GMLP_CHUNK = 128
N_GROUPS_C = 8
N_HEADS_X = 4
HEAD_DIM_X = D_MODEL // N_HEADS_X
EPS = 1e-6
N_EVEN = (DEPTH + 1) // 2
N_ODD = DEPTH // 2
AB_IN_COLS = 3 * W_A + 2 * W_B + MIX_WIDTH
C_IN_COLS = 3 * MIX_WIDTH

kernel_name = "hybrid_streaming_band_conv_sgu_encoder"


def rmsnorm(x, g):
    xf = x.astype(jnp.float32)
    y = xf * lax.rsqrt(jnp.mean(xf * xf, axis=-1, keepdims=True) + EPS)
    return (y * g.astype(jnp.float32)).astype(x.dtype)


def layernorm(x, g, b):
    xf = x.astype(jnp.float32)
    mu = jnp.mean(xf, axis=-1, keepdims=True)
    var = jnp.mean(jnp.square(xf - mu), axis=-1, keepdims=True)
    y = (xf - mu) * lax.rsqrt(var + EPS)
    return (y * g.astype(jnp.float32) + b.astype(jnp.float32)).astype(x.dtype)


def chunk_band_attention(q, k, v, rel_bias):
    b, s, h, dh = q.shape
    n_chunks = s // CHUNK
    pad = N_PAST_CHUNKS * CHUNK
    band = (N_PAST_CHUNKS + 1) * CHUNK
    k_pad = jnp.pad(k, ((0, 0), (pad, 0), (0, 0), (0, 0)))
    v_pad = jnp.pad(v, ((0, 0), (pad, 0), (0, 0), (0, 0)))
    q_off = np.arange(CHUNK)
    k_off = np.arange(band) - pad
    rel_idx = np.clip(q_off[:, None] - k_off[None, :], -MAX_REL, MAX_REL) + MAX_REL
    bias = jnp.take(rel_bias.astype(jnp.float32), jnp.asarray(rel_idx), axis=1)
    scale = dh ** -0.5
    k_off_j = jnp.asarray(k_off)

    def one_chunk(c):
        start = c * CHUNK
        qc = lax.dynamic_slice_in_dim(q, start, CHUNK, axis=1)
        kb = lax.dynamic_slice_in_dim(k_pad, start, band, axis=1)
        vb = lax.dynamic_slice_in_dim(v_pad, start, band, axis=1)
        sc = jnp.einsum('bqhd,bkhd->bhqk', qc, kb).astype(jnp.float32) * scale + bias[None]
        valid = (start + k_off_j) >= 0
        sc = jnp.where(valid[None, None, None, :], sc, jnp.float32(-1e30))
        p = jax.nn.softmax(sc, axis=-1).astype(vb.dtype)
        return jnp.einsum('bhqk,bkhd->bqhd', p, vb)

    out = lax.map(one_chunk, jnp.arange(n_chunks))
    return jnp.transpose(out, (1, 0, 2, 3, 4)).reshape(b, s, h * dh)


def causal_depthwise_conv(x, w, bias):
    c = x.shape[-1]
    xp = jnp.pad(x, ((0, 0), (w.shape[0] - 1, 0), (0, 0)))
    y = lax.conv_general_dilated(xp, w[:, None, :], window_strides=(1,), padding='VALID',
                                 dimension_numbers=('NWC', 'WIO', 'NWC'),
                                 feature_group_count=c)
    return y + bias


def ab_mixer(hn, w_in, rel_bias, conv_w, conv_b, ln_g, ln_b, w_out):
    b, s, _ = hn.shape
    proj = hn @ w_in
    splits = [W_A, 2 * W_A, 3 * W_A, 3 * W_A + W_B, 3 * W_A + 2 * W_B]
    q, k, v, glu_a, glu_b, gate = jnp.split(proj, splits, axis=-1)
    shp = (b, s, N_HEADS_A, HEAD_DIM_A)
    ya = chunk_band_attention(q.reshape(shp), k.reshape(shp), v.reshape(shp), rel_bias)
    yb = glu_a * jax.nn.sigmoid(glu_b)
    yb = jax.nn.silu(layernorm(causal_depthwise_conv(yb, conv_w, conv_b), ln_g, ln_b))
    y = jnp.concatenate([ya, yb], axis=-1) * jax.nn.silu(gate)
    return y @ w_out


def c_mixer(hn, w_in, ln_g, ln_b, w_s, b_s, w_out):
    b, s, _ = hn.shape
    u, v, gate = jnp.split(hn @ w_in, [MIX_WIDTH, 2 * MIX_WIDTH], axis=-1)
    v = layernorm(v, ln_g, ln_b)
    n_blk = s // GMLP_CHUNK
    vr = v.reshape(b, n_blk, GMLP_CHUNK, N_GROUPS_C, MIX_WIDTH // N_GROUPS_C)
    pos_chunk = np.arange(GMLP_CHUNK) // CHUNK
    mask = jnp.asarray(pos_chunk[:, None] >= pos_chunk[None, :], dtype=w_s.dtype)
    ws = w_s * mask[None]
    sg = jnp.einsum('gij,bnjgc->bnigc', ws, vr) + jnp.transpose(b_s)[None, None, :, :, None]
    y = u * sg.reshape(b, s, MIX_WIDTH) * jax.nn.silu(gate)
    return y @ w_out


def memory_cross_attention(hn, mem_n, wq, wk, wv, wo):
    b, s, _ = hn.shape
    q = (hn @ wq).reshape(b, s, N_HEADS_X, HEAD_DIM_X)
    k = (mem_n @ wk).reshape(b, N_MEM, N_HEADS_X, HEAD_DIM_X)
    v = (mem_n @ wv).reshape(b, N_MEM, N_HEADS_X, HEAD_DIM_X)
    sc = jnp.einsum('bqhd,bkhd->bhqk', q, k).astype(jnp.float32) * (HEAD_DIM_X ** -0.5)
    p = jax.nn.softmax(sc, axis=-1).astype(v.dtype)
    o = jnp.einsum('bhqk,bkhd->bqhd', p, v).reshape(b, s, D_MODEL)
    return o @ wo


def _fwd_setup_inputs(seed: int = 0) -> dict:
    key = jax.random.key(seed)
    ks = iter(jax.random.split(key, 32))
    nrm = lambda shape, scale: jax.random.normal(next(ks), shape, jnp.float32) * scale
    gain = lambda shape: 1.0 + nrm(shape, 0.01)
    d = D_MODEL
    return {
        "x": nrm((BATCH, SEQ, d), 1.0),
        "mem": nrm((BATCH, N_MEM, d), 1.0),
        "norm_mix_g": gain((DEPTH, d)),
        "norm_x_g": gain((DEPTH, d)),
        "norm_mem_g": gain((DEPTH, d)),
        "final_norm_g": gain((d,)),
        "w_in_ab": nrm((N_EVEN, d, AB_IN_COLS), d ** -0.5),
        "rel_bias": nrm((N_EVEN, N_HEADS_A, 2 * MAX_REL + 1), 0.2),
        "conv_w": nrm((N_EVEN, CONV_WIDTH, W_B), CONV_WIDTH ** -0.5),
        "conv_b": nrm((N_EVEN, W_B), 0.01),
        "conv_ln_g": gain((N_EVEN, W_B)),
        "conv_ln_b": nrm((N_EVEN, W_B), 0.01),
        "w_out_ab": nrm((N_EVEN, MIX_WIDTH, d), MIX_WIDTH ** -0.5),
        "w_in_c": nrm((N_ODD, d, C_IN_COLS), d ** -0.5),
        "sgu_ln_g": gain((N_ODD, MIX_WIDTH)),
        "sgu_ln_b": nrm((N_ODD, MIX_WIDTH), 0.01),
        "w_s": nrm((N_ODD, N_GROUPS_C, GMLP_CHUNK, GMLP_CHUNK), GMLP_CHUNK ** -0.5),
        "b_s": gain((N_ODD, N_GROUPS_C, GMLP_CHUNK)),
        "w_out_c": nrm((N_ODD, MIX_WIDTH, d), MIX_WIDTH ** -0.5),
        "w_xq": nrm((DEPTH, d, d), d ** -0.5),
        "w_xk": nrm((DEPTH, d, d), d ** -0.5),
        "w_xv": nrm((DEPTH, d, d), d ** -0.5),
        "w_xo": nrm((DEPTH, d, d), d ** -0.5),
    }


def _fwd_reference(x, mem, norm_mix_g, norm_x_g, norm_mem_g, final_norm_g, w_in_ab, rel_bias,
              conv_w, conv_b, conv_ln_g, conv_ln_b, w_out_ab, w_in_c, sgu_ln_g, sgu_ln_b,
              w_s, b_s, w_out_c, w_xq, w_xk, w_xv, w_xo):
    h = x
    for layer in range(DEPTH):
        i = layer // 2
        hn = rmsnorm(h, norm_mix_g[layer])
        if layer % 2 == 0:
            y = ab_mixer(hn, w_in_ab[i], rel_bias[i], conv_w[i], conv_b[i],
                         conv_ln_g[i], conv_ln_b[i], w_out_ab[i])
        else:
            y = c_mixer(hn, w_in_c[i], sgu_ln_g[i], sgu_ln_b[i], w_s[i], b_s[i], w_out_c[i])
        h = h + y
        h = h + memory_cross_attention(rmsnorm(h, norm_x_g[layer]), rmsnorm(mem, norm_mem_g[layer]),
                                       w_xq[layer], w_xk[layer], w_xv[layer], w_xo[layer])
    return rmsnorm(h, final_norm_g)


import jax as _jax
import jax.numpy as _jnp

TWIN_FORMAT = 'train_step'
FWD_PARAMS = ['x', 'mem', 'norm_mix_g', 'norm_x_g', 'norm_mem_g', 'final_norm_g', 'w_in_ab', 'rel_bias', 'conv_w', 'conv_b', 'conv_ln_g', 'conv_ln_b', 'w_out_ab', 'w_in_c', 'sgu_ln_g', 'sgu_ln_b', 'w_s', 'b_s', 'w_out_c', 'w_xq', 'w_xk', 'w_xv', 'w_xo']
TWIN_WEIGHTS = ['norm_mix_g', 'norm_x_g', 'norm_mem_g', 'final_norm_g', 'w_in_ab', 'rel_bias', 'conv_w', 'conv_b', 'conv_ln_g', 'conv_ln_b', 'w_out_ab', 'w_in_c', 'sgu_ln_g', 'sgu_ln_b', 'w_s', 'b_s', 'w_out_c', 'w_xq', 'w_xk', 'w_xv', 'w_xo']
TWIN_DIFF_INPUT = 'x'
TWIN_INPUTS = ['x', 'mem', 'norm_mix_g', 'norm_x_g', 'norm_mem_g', 'final_norm_g', 'w_in_ab', 'rel_bias', 'conv_w', 'conv_b', 'conv_ln_g', 'conv_ln_b', 'w_out_ab', 'w_in_c', 'sgu_ln_g', 'sgu_ln_b', 'w_s', 'b_s', 'w_out_c', 'w_xq', 'w_xk', 'w_xv', 'w_xo', 'loss_target', 'm_norm_mix_g', 'm_norm_x_g', 'm_norm_mem_g', 'm_final_norm_g', 'm_w_in_ab', 'm_rel_bias', 'm_conv_w', 'm_conv_b', 'm_conv_ln_g', 'm_conv_ln_b', 'm_w_out_ab', 'm_w_in_c', 'm_sgu_ln_g', 'm_sgu_ln_b', 'm_w_s', 'm_b_s', 'm_w_out_c', 'm_w_xq', 'm_w_xk', 'm_w_xv', 'm_w_xo', 'v_norm_mix_g', 'v_norm_x_g', 'v_norm_mem_g', 'v_final_norm_g', 'v_w_in_ab', 'v_rel_bias', 'v_conv_w', 'v_conv_b', 'v_conv_ln_g', 'v_conv_ln_b', 'v_w_out_ab', 'v_w_in_c', 'v_sgu_ln_g', 'v_sgu_ln_b', 'v_w_s', 'v_b_s', 'v_w_out_c', 'v_w_xq', 'v_w_xk', 'v_w_xv', 'v_w_xo']
TWIN_OUTPUTS = ['loss', 'grad_x', 'grad_norm_mix_g', 'grad_norm_x_g', 'grad_norm_mem_g', 'grad_final_norm_g', 'grad_w_in_ab', 'grad_rel_bias', 'grad_conv_w', 'grad_conv_b', 'grad_conv_ln_g', 'grad_conv_ln_b', 'grad_w_out_ab', 'grad_w_in_c', 'grad_sgu_ln_g', 'grad_sgu_ln_b', 'grad_w_s', 'grad_b_s', 'grad_w_out_c', 'grad_w_xq', 'grad_w_xk', 'grad_w_xv', 'grad_w_xo', 'delta_norm_mix_g', 'delta_norm_x_g', 'delta_norm_mem_g', 'delta_final_norm_g', 'delta_w_in_ab', 'delta_rel_bias', 'delta_conv_w', 'delta_conv_b', 'delta_conv_ln_g', 'delta_conv_ln_b', 'delta_w_out_ab', 'delta_w_in_c', 'delta_sgu_ln_g', 'delta_sgu_ln_b', 'delta_w_s', 'delta_b_s', 'delta_w_out_c', 'delta_w_xq', 'delta_w_xk', 'delta_w_xv', 'delta_w_xo', 'new_m_norm_mix_g', 'new_m_norm_x_g', 'new_m_norm_mem_g', 'new_m_final_norm_g', 'new_m_w_in_ab', 'new_m_rel_bias', 'new_m_conv_w', 'new_m_conv_b', 'new_m_conv_ln_g', 'new_m_conv_ln_b', 'new_m_w_out_ab', 'new_m_w_in_c', 'new_m_sgu_ln_g', 'new_m_sgu_ln_b', 'new_m_w_s', 'new_m_b_s', 'new_m_w_out_c', 'new_m_w_xq', 'new_m_w_xk', 'new_m_w_xv', 'new_m_w_xo', 'new_v_norm_mix_g', 'new_v_norm_x_g', 'new_v_norm_mem_g', 'new_v_final_norm_g', 'new_v_w_in_ab', 'new_v_rel_bias', 'new_v_conv_w', 'new_v_conv_b', 'new_v_conv_ln_g', 'new_v_conv_ln_b', 'new_v_w_out_ab', 'new_v_w_in_c', 'new_v_sgu_ln_g', 'new_v_sgu_ln_b', 'new_v_w_s', 'new_v_b_s', 'new_v_w_out_c', 'new_v_w_xq', 'new_v_w_xk', 'new_v_w_xv', 'new_v_w_xo']
TWIN_LEAF_KINDS = {'loss': 'loss', 'grad_x': 'grad_x', 'grad_norm_mix_g': 'grad_w', 'grad_norm_x_g': 'grad_w', 'grad_norm_mem_g': 'grad_w', 'grad_final_norm_g': 'grad_w', 'grad_w_in_ab': 'grad_w', 'grad_rel_bias': 'grad_w', 'grad_conv_w': 'grad_w', 'grad_conv_b': 'grad_w', 'grad_conv_ln_g': 'grad_w', 'grad_conv_ln_b': 'grad_w', 'grad_w_out_ab': 'grad_w', 'grad_w_in_c': 'grad_w', 'grad_sgu_ln_g': 'grad_w', 'grad_sgu_ln_b': 'grad_w', 'grad_w_s': 'grad_w', 'grad_b_s': 'grad_w', 'grad_w_out_c': 'grad_w', 'grad_w_xq': 'grad_w', 'grad_w_xk': 'grad_w', 'grad_w_xv': 'grad_w', 'grad_w_xo': 'grad_w', 'delta_norm_mix_g': 'delta_w', 'delta_norm_x_g': 'delta_w', 'delta_norm_mem_g': 'delta_w', 'delta_final_norm_g': 'delta_w', 'delta_w_in_ab': 'delta_w', 'delta_rel_bias': 'delta_w', 'delta_conv_w': 'delta_w', 'delta_conv_b': 'delta_w', 'delta_conv_ln_g': 'delta_w', 'delta_conv_ln_b': 'delta_w', 'delta_w_out_ab': 'delta_w', 'delta_w_in_c': 'delta_w', 'delta_sgu_ln_g': 'delta_w', 'delta_sgu_ln_b': 'delta_w', 'delta_w_s': 'delta_w', 'delta_b_s': 'delta_w', 'delta_w_out_c': 'delta_w', 'delta_w_xq': 'delta_w', 'delta_w_xk': 'delta_w', 'delta_w_xv': 'delta_w', 'delta_w_xo': 'delta_w', 'new_m_norm_mix_g': 'new_m', 'new_m_norm_x_g': 'new_m', 'new_m_norm_mem_g': 'new_m', 'new_m_final_norm_g': 'new_m', 'new_m_w_in_ab': 'new_m', 'new_m_rel_bias': 'new_m', 'new_m_conv_w': 'new_m', 'new_m_conv_b': 'new_m', 'new_m_conv_ln_g': 'new_m', 'new_m_conv_ln_b': 'new_m', 'new_m_w_out_ab': 'new_m', 'new_m_w_in_c': 'new_m', 'new_m_sgu_ln_g': 'new_m', 'new_m_sgu_ln_b': 'new_m', 'new_m_w_s': 'new_m', 'new_m_b_s': 'new_m', 'new_m_w_out_c': 'new_m', 'new_m_w_xq': 'new_m', 'new_m_w_xk': 'new_m', 'new_m_w_xv': 'new_m', 'new_m_w_xo': 'new_m', 'new_v_norm_mix_g': 'new_v', 'new_v_norm_x_g': 'new_v', 'new_v_norm_mem_g': 'new_v', 'new_v_final_norm_g': 'new_v', 'new_v_w_in_ab': 'new_v', 'new_v_rel_bias': 'new_v', 'new_v_conv_w': 'new_v', 'new_v_conv_b': 'new_v', 'new_v_conv_ln_g': 'new_v', 'new_v_conv_ln_b': 'new_v', 'new_v_w_out_ab': 'new_v', 'new_v_w_in_c': 'new_v', 'new_v_sgu_ln_g': 'new_v', 'new_v_sgu_ln_b': 'new_v', 'new_v_w_s': 'new_v', 'new_v_b_s': 'new_v', 'new_v_w_out_c': 'new_v', 'new_v_w_xq': 'new_v', 'new_v_w_xk': 'new_v', 'new_v_w_xv': 'new_v', 'new_v_w_xo': 'new_v'}


def _forward(args):
    return _fwd_reference(*[args[k] for k in FWD_PARAMS])


def _output_shape():
    def fwd():
        inp = _fwd_setup_inputs(0)
        return _fwd_reference(*[inp[k] for k in FWD_PARAMS])
    out = _jax.eval_shape(fwd)
    return out.shape, out.dtype

N_MICROBATCH = 1
ADAM_LR = 0.001
ADAM_B1 = 0.9
ADAM_B2 = 0.999
ADAM_EPS = 1e-08
ADAM_WD = 0.01
ADAM_STEP = 10
PER_EXAMPLE_BATCH_AXIS = {'x': 0, 'mem': 0, 'loss_target': 0}
SHARED_INPUTS = []
_WEIGHT_DTYPES = {'norm_mix_g': _jnp.float32, 'norm_x_g': _jnp.float32, 'norm_mem_g': _jnp.float32, 'final_norm_g': _jnp.float32, 'w_in_ab': _jnp.float32, 'rel_bias': _jnp.float32, 'conv_w': _jnp.float32, 'conv_b': _jnp.float32, 'conv_ln_g': _jnp.float32, 'conv_ln_b': _jnp.float32, 'w_out_ab': _jnp.float32, 'w_in_c': _jnp.float32, 'sgu_ln_g': _jnp.float32, 'sgu_ln_b': _jnp.float32, 'w_s': _jnp.float32, 'b_s': _jnp.float32, 'w_out_c': _jnp.float32, 'w_xq': _jnp.float32, 'w_xk': _jnp.float32, 'w_xv': _jnp.float32, 'w_xo': _jnp.float32}
MOMENT_SCALE = {'norm_mix_g': 9.474001e-02, 'norm_x_g': 1.285731e-02, 'norm_mem_g': 1.869901e-02, 'final_norm_g': 3.196988e+01, 'w_in_ab': 2.291476e-02, 'rel_bias': 6.175574e-03, 'conv_w': 3.933932e-02, 'conv_b': 8.219835e-02, 'conv_ln_g': 4.881079e-02, 'conv_ln_b': 4.273737e-02, 'w_out_ab': 3.910203e-02, 'w_in_c': 4.877277e-02, 'sgu_ln_g': 3.487229e-02, 'sgu_ln_b': 3.494556e-02, 'w_s': 7.051922e-02, 'b_s': 8.174977e-02, 'w_out_c': 7.567465e-02, 'w_xq': 1.282404e-02, 'w_xk': 1.283166e-02, 'w_xv': 1.303985e-02, 'w_xo': 1.305811e-02}


def _to_microbatches(a, axis):
    t = _jnp.moveaxis(a, axis, 0)
    t = t.reshape((N_MICROBATCH, t.shape[0] // N_MICROBATCH) + t.shape[1:])
    return _jnp.moveaxis(t, 1, axis + 1)


def setup_inputs(seed: int = 0) -> dict:
    inp = _fwd_setup_inputs(seed)
    key = _jax.random.fold_in(_jax.random.key(seed), 7919)
    shape, _ = _output_shape()
    out = dict(inp)
    out["loss_target"] = _jax.random.normal(_jax.random.fold_in(key, 0), shape, _jnp.float32)
    for i, name in enumerate(TWIN_WEIGHTS):
        w = inp[name].astype(_jnp.float32)
        if MOMENT_SCALE is None:
            s = _jnp.sqrt(_jnp.mean(_jnp.square(w)) + 1e-30)
        else:
            s = MOMENT_SCALE[name]
        km, kv = _jax.random.split(_jax.random.fold_in(key, i + 1))
        out[name] = w
        out["m_" + name] = s * _jax.random.normal(km, w.shape, _jnp.float32)
        out["v_" + name] = (s * s) * _jax.random.uniform(kv, w.shape, _jnp.float32, 0.5, 1.5)
    if N_MICROBATCH > 1:
        for name, axis in PER_EXAMPLE_BATCH_AXIS.items():
            out[name] = _to_microbatches(out[name], axis)
    return {'x': out['x'], 'mem': out['mem'], 'norm_mix_g': out['norm_mix_g'], 'norm_x_g': out['norm_x_g'], 'norm_mem_g': out['norm_mem_g'], 'final_norm_g': out['final_norm_g'], 'w_in_ab': out['w_in_ab'], 'rel_bias': out['rel_bias'], 'conv_w': out['conv_w'], 'conv_b': out['conv_b'], 'conv_ln_g': out['conv_ln_g'], 'conv_ln_b': out['conv_ln_b'], 'w_out_ab': out['w_out_ab'], 'w_in_c': out['w_in_c'], 'sgu_ln_g': out['sgu_ln_g'], 'sgu_ln_b': out['sgu_ln_b'], 'w_s': out['w_s'], 'b_s': out['b_s'], 'w_out_c': out['w_out_c'], 'w_xq': out['w_xq'], 'w_xk': out['w_xk'], 'w_xv': out['w_xv'], 'w_xo': out['w_xo'], 'loss_target': out['loss_target'], 'm_norm_mix_g': out['m_norm_mix_g'], 'm_norm_x_g': out['m_norm_x_g'], 'm_norm_mem_g': out['m_norm_mem_g'], 'm_final_norm_g': out['m_final_norm_g'], 'm_w_in_ab': out['m_w_in_ab'], 'm_rel_bias': out['m_rel_bias'], 'm_conv_w': out['m_conv_w'], 'm_conv_b': out['m_conv_b'], 'm_conv_ln_g': out['m_conv_ln_g'], 'm_conv_ln_b': out['m_conv_ln_b'], 'm_w_out_ab': out['m_w_out_ab'], 'm_w_in_c': out['m_w_in_c'], 'm_sgu_ln_g': out['m_sgu_ln_g'], 'm_sgu_ln_b': out['m_sgu_ln_b'], 'm_w_s': out['m_w_s'], 'm_b_s': out['m_b_s'], 'm_w_out_c': out['m_w_out_c'], 'm_w_xq': out['m_w_xq'], 'm_w_xk': out['m_w_xk'], 'm_w_xv': out['m_w_xv'], 'm_w_xo': out['m_w_xo'], 'v_norm_mix_g': out['v_norm_mix_g'], 'v_norm_x_g': out['v_norm_x_g'], 'v_norm_mem_g': out['v_norm_mem_g'], 'v_final_norm_g': out['v_final_norm_g'], 'v_w_in_ab': out['v_w_in_ab'], 'v_rel_bias': out['v_rel_bias'], 'v_conv_w': out['v_conv_w'], 'v_conv_b': out['v_conv_b'], 'v_conv_ln_g': out['v_conv_ln_g'], 'v_conv_ln_b': out['v_conv_ln_b'], 'v_w_out_ab': out['v_w_out_ab'], 'v_w_in_c': out['v_w_in_c'], 'v_sgu_ln_g': out['v_sgu_ln_g'], 'v_sgu_ln_b': out['v_sgu_ln_b'], 'v_w_s': out['v_w_s'], 'v_b_s': out['v_b_s'], 'v_w_out_c': out['v_w_out_c'], 'v_w_xq': out['v_w_xq'], 'v_w_xk': out['v_w_xk'], 'v_w_xv': out['v_w_xv'], 'v_w_xo': out['v_w_xo']}


def _loss(weights, diff, rest, loss_target):
    with _jax.named_scope("forward"):
        args = {**rest, TWIN_DIFF_INPUT: diff, **{k: w.astype(_WEIGHT_DTYPES[k]) for k, w in weights.items()}}
        y = _forward(args)
    with _jax.named_scope("loss_head"):
        err = _jnp.square(y.astype(_jnp.float32) - loss_target)
        return 0.5 * _jnp.sum(_jnp.mean(err, axis=-1)) if err.ndim else 0.5 * err


def _adamw(w, g, m, v):
    m = ADAM_B1 * m + (1.0 - ADAM_B1) * g
    v = ADAM_B2 * v + (1.0 - ADAM_B2) * _jnp.square(g)
    m_hat = m / (1.0 - ADAM_B1 ** ADAM_STEP)
    v_hat = v / (1.0 - ADAM_B2 ** ADAM_STEP)
    delta = -ADAM_LR * (m_hat / (_jnp.sqrt(v_hat) + ADAM_EPS) + ADAM_WD * w)
    return delta, m, v


def reference(x, mem, norm_mix_g, norm_x_g, norm_mem_g, final_norm_g, w_in_ab, rel_bias, conv_w, conv_b, conv_ln_g, conv_ln_b, w_out_ab, w_in_c, sgu_ln_g, sgu_ln_b, w_s, b_s, w_out_c, w_xq, w_xk, w_xv, w_xo, loss_target, m_norm_mix_g, m_norm_x_g, m_norm_mem_g, m_final_norm_g, m_w_in_ab, m_rel_bias, m_conv_w, m_conv_b, m_conv_ln_g, m_conv_ln_b, m_w_out_ab, m_w_in_c, m_sgu_ln_g, m_sgu_ln_b, m_w_s, m_b_s, m_w_out_c, m_w_xq, m_w_xk, m_w_xv, m_w_xo, v_norm_mix_g, v_norm_x_g, v_norm_mem_g, v_final_norm_g, v_w_in_ab, v_rel_bias, v_conv_w, v_conv_b, v_conv_ln_g, v_conv_ln_b, v_w_out_ab, v_w_in_c, v_sgu_ln_g, v_sgu_ln_b, v_w_s, v_b_s, v_w_out_c, v_w_xq, v_w_xk, v_w_xv, v_w_xo):
    given = dict(x=x, mem=mem, norm_mix_g=norm_mix_g, norm_x_g=norm_x_g, norm_mem_g=norm_mem_g, final_norm_g=final_norm_g, w_in_ab=w_in_ab, rel_bias=rel_bias, conv_w=conv_w, conv_b=conv_b, conv_ln_g=conv_ln_g, conv_ln_b=conv_ln_b, w_out_ab=w_out_ab, w_in_c=w_in_c, sgu_ln_g=sgu_ln_g, sgu_ln_b=sgu_ln_b, w_s=w_s, b_s=b_s, w_out_c=w_out_c, w_xq=w_xq, w_xk=w_xk, w_xv=w_xv, w_xo=w_xo, loss_target=loss_target, m_norm_mix_g=m_norm_mix_g, m_norm_x_g=m_norm_x_g, m_norm_mem_g=m_norm_mem_g, m_final_norm_g=m_final_norm_g, m_w_in_ab=m_w_in_ab, m_rel_bias=m_rel_bias, m_conv_w=m_conv_w, m_conv_b=m_conv_b, m_conv_ln_g=m_conv_ln_g, m_conv_ln_b=m_conv_ln_b, m_w_out_ab=m_w_out_ab, m_w_in_c=m_w_in_c, m_sgu_ln_g=m_sgu_ln_g, m_sgu_ln_b=m_sgu_ln_b, m_w_s=m_w_s, m_b_s=m_b_s, m_w_out_c=m_w_out_c, m_w_xq=m_w_xq, m_w_xk=m_w_xk, m_w_xv=m_w_xv, m_w_xo=m_w_xo, v_norm_mix_g=v_norm_mix_g, v_norm_x_g=v_norm_x_g, v_norm_mem_g=v_norm_mem_g, v_final_norm_g=v_final_norm_g, v_w_in_ab=v_w_in_ab, v_rel_bias=v_rel_bias, v_conv_w=v_conv_w, v_conv_b=v_conv_b, v_conv_ln_g=v_conv_ln_g, v_conv_ln_b=v_conv_ln_b, v_w_out_ab=v_w_out_ab, v_w_in_c=v_w_in_c, v_sgu_ln_g=v_sgu_ln_g, v_sgu_ln_b=v_sgu_ln_b, v_w_s=v_w_s, v_b_s=v_b_s, v_w_out_c=v_w_out_c, v_w_xq=v_w_xq, v_w_xk=v_w_xk, v_w_xv=v_w_xv, v_w_xo=v_w_xo)
    weights = {n: given[n] for n in TWIN_WEIGHTS}
    shared = {n: given[n] for n in SHARED_INPUTS}
    per_example = {n: given[n] for n in ['x', 'mem']}
    grad_fn = _jax.value_and_grad(_loss, argnums=(0, 1))

    def one_microbatch(ex, loss_target):
        ex = dict(ex)
        diff = ex.pop(TWIN_DIFF_INPUT)
        return grad_fn(weights, diff, {**shared, **ex}, loss_target)

    if N_MICROBATCH == 1:
        loss, (grad_w, grad_x) = one_microbatch(per_example, given["loss_target"])
    else:
        def body(carry, xs):
            loss_sum, grad_sum = carry
            l_k, (gw_k, gx_k) = one_microbatch(xs[0], xs[1])
            with _jax.named_scope("update"):
                return (loss_sum + l_k, _jax.tree.map(_jnp.add, grad_sum, gw_k)), gx_k

        init = (_jnp.zeros((), _jnp.float32), _jax.tree.map(_jnp.zeros_like, weights))
        (loss, grad_w), grad_x = _jax.lax.scan(body, init, (per_example, given["loss_target"]))
    with _jax.named_scope("update"):
        delta_w, new_m, new_v = {}, {}, {}
        for n in TWIN_WEIGHTS:
            delta_w[n], new_m[n], new_v[n] = _adamw(weights[n], grad_w[n], given["m_" + n], given["v_" + n])
    return (loss, grad_x, *[grad_w[n] for n in TWIN_WEIGHTS], *[delta_w[n] for n in TWIN_WEIGHTS],
            *[new_m[n] for n in TWIN_WEIGHTS], *[new_v[n] for n in TWIN_WEIGHTS])
```

```python
import functools

import numpy as np
import jax
import jax.numpy as jnp
from jax import lax
from jax.experimental import pallas as pl
from jax.experimental.pallas import tpu as pltpu

F32 = jnp.float32
BF16 = jnp.bfloat16
MESH = pl.DeviceIdType.MESH

EPS = 1e-6
CHUNK = 64
N_PAST_CHUNKS = 8
MAX_REL = 128
HEAD_DIM_A = 128
CONV_WIDTH = 31
CONV_HALO = 32
GMLP_CHUNK = 128
N_GROUPS_C = 8
N_HEADS_X = 4
NEG = -1e30

ADAM_LR = 0.001
ADAM_B1 = 0.9
ADAM_B2 = 0.999
ADAM_EPS = 1e-08
ADAM_WD = 0.01
ADAM_STEP = 10

N_CHIPS = 4
N_DEV = 8
V7X_VMEM_LIMIT = 56 * 1024 * 1024
LANES = 128
MXU = 256


def _pick(n, cands):
    for c in cands:
        if c <= n and n % c == 0:
            return c
    return n


def _cparams(*sem):
    return pltpu.CompilerParams(dimension_semantics=sem, vmem_limit_bytes=V7X_VMEM_LIMIT)


def _sigmoid(x):
    return 1.0 / (1.0 + jnp.exp(-x))


def _dot(a, b, contract):
    return lax.dot_general(a, b, (contract, ((), ())), preferred_element_type=F32)


NN = ((1,), (0,))
NT = ((1,), (1,))
TN = ((0,), (0,))


def _mm(name, a, b, *, contract, grid, a_spec, b_spec, o_spec, out_shape, res=None):
    nk = grid[2]

    def body(*refs):
        if res is not None:
            a_ref, b_ref, r_ref, o_ref = refs[:4]
        else:
            a_ref, b_ref, o_ref = refs[:3]
            r_ref = None
        p = _dot(a_ref[...].astype(BF16), b_ref[...].astype(BF16), contract)

        def finish(acc):
            if r_ref is not None:
                acc = acc + r_ref[...]
            o_ref[...] = acc.astype(o_ref.dtype)

        if nk == 1:
            finish(p)
        else:
            acc_ref = refs[-1]
            k = pl.program_id(2)

            @pl.when(k == 0)
            def _():
                acc_ref[...] = p

            @pl.when(k > 0)
            def _():
                acc_ref[...] += p

            @pl.when(k == nk - 1)
            def _():
                finish(acc_ref[...])

    in_specs = [a_spec, b_spec]
    args = [a, b]
    if res is not None:
        in_specs.append(o_spec)
        args.append(res)
    blk = tuple(d for d in o_spec.block_shape if d is not None)
    scratch = [] if nk == 1 else [pltpu.VMEM(blk, F32)]
    return pl.pallas_call(
        body, grid=grid, in_specs=in_specs, out_specs=o_spec, out_shape=out_shape,
        scratch_shapes=scratch, name=name,
        compiler_params=_cparams("parallel", "parallel", "arbitrary"))(*args)


def mm_nn_cols(name, a, w4, out_dtype):
    M, K = a.shape
    _, _, C = w4.shape
    tm = _pick(M, (1024, 512, 256))
    tn = _pick(C, (512, 256, 128))
    nps = C // tn
    return _mm(name, a, w4, contract=NN, grid=(M // tm, 4 * nps, 1),
               a_spec=pl.BlockSpec((tm, K), lambda i, j, k: (i, 0)),
               b_spec=pl.BlockSpec((None, K, tn), lambda i, j, k: (j // nps, 0, j % nps)),
               o_spec=pl.BlockSpec((tm, tn), lambda i, j, k: (i, j)),
               out_shape=jax.ShapeDtypeStruct((M, 4 * C), out_dtype))


def mm_nn_rows(name, a, w4, row_off, R, out_dtype, res=None):
    M, K = a.shape
    C = w4.shape[2]
    tm = _pick(M, (1024, 512, 256))
    tn = _pick(C, (1024, 512, 256, 128))
    tk = _pick(R, (1024, 512, 256, 128))
    kps = R // tk
    ro = row_off // tk
    return _mm(name, a, w4, contract=NN, grid=(M // tm, C // tn, 4 * kps),
               a_spec=pl.BlockSpec((tm, tk), lambda i, j, k: (i, k)),
               b_spec=pl.BlockSpec((None, tk, tn), lambda i, j, k: (k // kps, ro + k % kps, j)),
               o_spec=pl.BlockSpec((tm, tn), lambda i, j, k: (i, j)),
               out_shape=jax.ShapeDtypeStruct((M, C), out_dtype), res=res)


def mm_nt_cols(name, a, w4, out_dtype):
    M = a.shape[0]
    _, K, C = w4.shape
    tm = _pick(M, (1024, 512, 256))
    tn = _pick(K, (1024, 512, 256, 128))
    tk = _pick(C, (512, 256, 128))
    kps = C // tk
    return _mm(name, a, w4, contract=NT, grid=(M // tm, K // tn, 4 * kps),
               a_spec=pl.BlockSpec((tm, tk), lambda i, j, k: (i, k)),
               b_spec=pl.BlockSpec((None, tn, tk), lambda i, j, k: (k // kps, j, k % kps)),
               o_spec=pl.BlockSpec((tm, tn), lambda i, j, k: (i, j)),
               out_shape=jax.ShapeDtypeStruct((M, K), out_dtype))


def mm_nt_rows(name, a, w4, row_off, R, out_dtype):
    M, C = a.shape
    tm = _pick(M, (1024, 512, 256))
    tn = _pick(R, (512, 256, 128))
    nps = R // tn
    ro = row_off // tn
    return _mm(name, a, w4, contract=NT, grid=(M // tm, 4 * nps, 1),
               a_spec=pl.BlockSpec((tm, C), lambda i, j, k: (i, 0)),
               b_spec=pl.BlockSpec((None, tn, C), lambda i, j, k: (j // nps, ro + j % nps, 0)),
               o_spec=pl.BlockSpec((tm, tn), lambda i, j, k: (i, j)),
               out_shape=jax.ShapeDtypeStruct((M, 4 * R), out_dtype))


def mm_tn_cols(name, a, b):
    S, K = a.shape
    C = b.shape[1] // 4
    ts = _pick(S, (1024, 512, 256))
    tko = _pick(K, (2048, 1024, 512, 256, 128))
    tn = _pick(C, (1024, 512, 256, 128))
    nps = C // tn
    return _mm(name, a, b, contract=TN, grid=(K // tko, 4 * nps, S // ts),
               a_spec=pl.BlockSpec((ts, tko), lambda i, j, k: (k, i)),
               b_spec=pl.BlockSpec((ts, tn), lambda i, j, k: (k, j)),
               o_spec=pl.BlockSpec((None, tko, tn), lambda i, j, k: (j // nps, i, j % nps)),
               out_shape=jax.ShapeDtypeStruct((4, K, C), BF16))


def mm_tn_rows(name, a, b):
    S, K = a.shape
    R = K // 4
    C = b.shape[1]
    ts = _pick(S, (1024, 512, 256))
    tko = _pick(R, (1024, 512, 256, 128))
    tn = _pick(C, (1024, 512, 256, 128))
    kps = R // tko
    return _mm(name, a, b, contract=TN, grid=(K // tko, C // tn, S // ts),
               a_spec=pl.BlockSpec((ts, tko), lambda i, j, k: (k, i)),
               b_spec=pl.BlockSpec((ts, tn), lambda i, j, k: (k, j)),
               o_spec=pl.BlockSpec((None, tko, tn), lambda i, j, k: (i // kps, i % kps, j)),
               out_shape=jax.ShapeDtypeStruct((4, R, C), BF16))


def rms_fwd(name, x, g):
    S, D = x.shape
    T = _pick(S, (512, 256))

    def body(x_ref, g_ref, o_ref):
        xf = x_ref[...]
        r = lax.rsqrt(jnp.mean(xf * xf, axis=-1, keepdims=True) + EPS)
        o_ref[...] = (xf * r * g_ref[...]).astype(o_ref.dtype)

    return pl.pallas_call(
        body, grid=(S // T,),
        in_specs=[pl.BlockSpec((T, D), lambda i: (i, 0)), pl.BlockSpec((1, D), lambda i: (0, 0))],
        out_specs=pl.BlockSpec((T, D), lambda i: (i, 0)),
        out_shape=jax.ShapeDtypeStruct((S, D), BF16), name=name,
        compiler_params=_cparams("parallel"))(x, g.reshape(1, D))


def rms_bwd(name, x, g, dys, dres):
    S, D = x.shape
    T = _pick(S, (256,))
    ndy = len(dys)
    has_res = dres is not None

    def body(*refs):
        x_ref, g_ref = refs[0], refs[1]
        dy_refs = refs[2:2 + ndy]
        r_ref = refs[2 + ndy] if has_res else None
        dx_ref, dg_ref = refs[-2], refs[-1]
        i = pl.program_id(0)
        xf = x_ref[...]
        r = lax.rsqrt(jnp.mean(xf * xf, axis=-1, keepdims=True) + EPS)
        xhat = xf * r
        dy = dy_refs[0][...].astype(F32)
        for d in dy_refs[1:]:
            dy = dy + d[...].astype(F32)
        dxhat = dy * g_ref[...]
        dx = r * (dxhat - xhat * jnp.mean(dxhat * xhat, axis=-1, keepdims=True))
        if has_res:
            dx = dx + r_ref[...]
        dx_ref[...] = dx
        dg = jnp.sum(dy * xhat, axis=0, keepdims=True)

        @pl.when(i == 0)
        def _():
            dg_ref[...] = dg

        @pl.when(i > 0)
        def _():
            dg_ref[...] += dg

    row = pl.BlockSpec((T, D), lambda i: (i, 0))
    vec = pl.BlockSpec((1, D), lambda i: (0, 0))
    args = [x, g.reshape(1, D), *dys] + ([dres] if has_res else [])
    return pl.pallas_call(
        body, grid=(S // T,),
        in_specs=[row, vec] + [row] * (ndy + int(has_res)),
        out_specs=[row, vec],
        out_shape=[jax.ShapeDtypeStruct((S, D), F32), jax.ShapeDtypeStruct((1, D), F32)],
        name=name, compiler_params=_cparams("arbitrary"))(*args)


def loss_head(name, h, g, target):
    S, D = h.shape
    T = _pick(S, (256,))

    def body(h_ref, g_ref, t_ref, loss_ref, dg_ref, dh_ref):
        i = pl.program_id(0)
        xf = h_ref[...]
        gv = g_ref[...]
        r = lax.rsqrt(jnp.mean(xf * xf, axis=-1, keepdims=True) + EPS)
        xhat = xf * r
        err = xhat * gv - t_ref[...]
        part = 0.5 * jnp.sum(jnp.sum(err * err, axis=-1, keepdims=True), axis=0, keepdims=True) / D
        dout = err / D
        dxhat = dout * gv
        dh_ref[...] = r * (dxhat - xhat * jnp.mean(dxhat * xhat, axis=-1, keepdims=True))
        dg = jnp.sum(dout * xhat, axis=0, keepdims=True)
        lrow = jnp.broadcast_to(part, (1, LANES))

        @pl.when(i == 0)
        def _():
            dg_ref[...] = dg
            loss_ref[...] = lrow

        @pl.when(i > 0)
        def _():
            dg_ref[...] += dg
            loss_ref[...] += lrow

    row = pl.BlockSpec((T, D), lambda i: (i, 0))
    vec = pl.BlockSpec((1, D), lambda i: (0, 0))
    return pl.pallas_call(
        body, grid=(S // T,), in_specs=[row, vec, row],
        out_specs=[pl.BlockSpec((1, LANES), lambda i: (0, 0)), vec, row],
        out_shape=[jax.ShapeDtypeStruct((1, LANES), F32), jax.ShapeDtypeStruct((1, D), F32),
                   jax.ShapeDtypeStruct((S, D), F32)],
        name=name, compiler_params=_cparams("arbitrary"))(h, g.reshape(1, D), target)


def _attn_tq(S):
    return _pick(S, (512,))


def band_bias_table(rel_bias, tq):
    r = np.arange(tq)[:, None]
    kpos = np.arange(2 * tq)[None, :] - tq
    idx = np.clip(r - kpos, -MAX_REL, MAX_REL) + MAX_REL
    qc = r // CHUNK
    kc = np.floor_divide(kpos, CHUNK)
    valid = (kc <= qc) & (kc >= qc - N_PAST_CHUNKS)
    tab = jnp.take(rel_bias.astype(F32), jnp.asarray(idx.reshape(-1)), axis=1).reshape(-1, tq, 2 * tq)
    return jnp.where(jnp.asarray(valid)[None], tab, NEG)


def attn_fwd(proj, bm, D):
    S = proj.shape[0]
    H = D // HEAD_DIM_A
    tq = _attn_tq(S)
    nb = S // tq
    scale = HEAD_DIM_A ** -0.5

    def body(q_ref, kp_ref, kc_ref, vp_ref, vc_ref, bm_ref, o_ref, lse_ref):
        i = pl.program_id(1)
        q = q_ref[...]
        sp = _dot(q, kp_ref[...], NT) * scale + bm_ref[:, :tq]
        sp = jnp.where(i == 0, NEG, sp)
        sc = _dot(q, kc_ref[...], NT) * scale + bm_ref[:, tq:]
        m = jnp.maximum(jnp.max(sp, axis=-1, keepdims=True), jnp.max(sc, axis=-1, keepdims=True))
        pp = jnp.exp(sp - m)
        pc = jnp.exp(sc - m)
        l = jnp.sum(pp, axis=-1, keepdims=True) + jnp.sum(pc, axis=-1, keepdims=True)
        o = _dot(pp.astype(BF16), vp_ref[...], NN) + _dot(pc.astype(BF16), vc_ref[...], NN)
        o_ref[...] = (o / l).astype(o_ref.dtype)
        lse_ref[...] = m + jnp.log(l)

    def col(base):
        return (pl.BlockSpec((tq, HEAD_DIM_A), lambda h, i: (jnp.maximum(i - 1, 0), base + h)),
                pl.BlockSpec((tq, HEAD_DIM_A), lambda h, i: (i, base + h)))

    kp, kc = col(H)
    vp, vc = col(2 * H)
    return pl.pallas_call(
        body, grid=(H, nb),
        in_specs=[pl.BlockSpec((tq, HEAD_DIM_A), lambda h, i: (i, h)), kp, kc, vp, vc,
                  pl.BlockSpec((None, tq, 2 * tq), lambda h, i: (h, 0, 0))],
        out_specs=[pl.BlockSpec((tq, HEAD_DIM_A), lambda h, i: (i, h)),
                   pl.BlockSpec((None, tq, 1), lambda h, i: (h, i, 0))],
        out_shape=[jax.ShapeDtypeStruct((S, D), BF16), jax.ShapeDtypeStruct((H, S, 1), F32)],
        name="attn_fwd", compiler_params=_cparams("parallel", "arbitrary"))(
            proj, proj, proj, proj, proj, bm)


def attn_bwd(proj, ya, dya, lse, bm, D):
    S = proj.shape[0]
    H = D // HEAD_DIM_A
    tq = _attn_tq(S)
    nb = S // tq
    scale = HEAD_DIM_A ** -0.5

    def body(q_ref, kp_ref, kc_ref, vp_ref, vc_ref, o_ref, do_ref, lse_ref, bm_ref,
             dq_ref, dkc_ref, dkp_ref, dvc_ref, dvp_ref, ds_ref):
        i = pl.program_id(1)
        q = q_ref[...]
        do = do_ref[...]
        delta = jnp.sum(do.astype(F32) * o_ref[...].astype(F32), axis=-1, keepdims=True)
        lse_v = lse_ref[...]

        def half(k_ref, v_ref, bias, first):
            k = k_ref[...]
            s = _dot(q, k, NT) * scale + bias
            if first:
                s = jnp.where(i == 0, NEG, s)
            p = jnp.exp(s - lse_v)
            dv = _dot(p.astype(BF16), do, TN)
            dp = _dot(do, v_ref[...], NT)
            ds = p * (dp - delta)
            dsb = ds.astype(BF16)
            dq = _dot(dsb, k, NN)
            dk = _dot(dsb, q, TN) * scale
            return ds, dq, dk, dv

        dsp, dqp, dkp, dvp = half(kp_ref, vp_ref, bm_ref[:, :tq], True)
        dsc, dqc, dkc, dvc = half(kc_ref, vc_ref, bm_ref[:, tq:], False)
        dq_ref[...] = ((dqp + dqc) * scale).astype(dq_ref.dtype)
        dkp_ref[...] = dkp.astype(dkp_ref.dtype)
        dkc_ref[...] = dkc.astype(dkc_ref.dtype)
        dvp_ref[...] = dvp.astype(dvp_ref.dtype)
        dvc_ref[...] = dvc.astype(dvc_ref.dtype)

        @pl.when(i == 0)
        def _():
            ds_ref[:, :tq] = dsp
            ds_ref[:, tq:] = dsc

        @pl.when(i > 0)
        def _():
            ds_ref[:, :tq] += dsp
            ds_ref[:, tq:] += dsc

    def col(base):
        return (pl.BlockSpec((tq, HEAD_DIM_A), lambda h, i: (jnp.maximum(i - 1, 0), base + h)),
                pl.BlockSpec((tq, HEAD_DIM_A), lambda h, i: (i, base + h)))

    kp, kc = col(H)
    vp, vc = col(2 * H)
    blk = pl.BlockSpec((tq, HEAD_DIM_A), lambda h, i: (i, h))
    sd = jax.ShapeDtypeStruct((S, D), BF16)
    return pl.pallas_call(
        body, grid=(H, nb),
        in_specs=[blk, kp, kc, vp, vc, blk, blk,
                  pl.BlockSpec((None, tq, 1), lambda h, i: (h, i, 0)),
                  pl.BlockSpec((None, tq, 2 * tq), lambda h, i: (h, 0, 0))],
        out_specs=[blk, blk, blk, blk, blk, pl.BlockSpec((None, tq, 2 * tq), lambda h, i: (h, 0, 0))],
        out_shape=[sd, sd, sd, sd, sd, jax.ShapeDtypeStruct((H, tq, 2 * tq), F32)],
        name="attn_bwd", compiler_params=_cparams("parallel", "arbitrary"))(
            proj, proj, proj, proj, proj, ya, dya, lse, bm)


def rel_bias_grad(ds_sum):
    H, tq, w = ds_sum.shape
    nbin = 2 * MAX_REL + 1
    nbin_pad = 3 * LANES
    d_lo, d_hi = -(CHUNK - 1), (N_PAST_CHUNKS + 1) * CHUNK - 1
    assert d_hi - d_lo + 1 <= w
    onehot = np.zeros((w, nbin_pad), np.float32)
    for d in range(d_lo, d_hi + 1):
        onehot[(tq - d) % w, int(np.clip(d, -MAX_REL, MAX_REL)) + MAX_REL] = 1.0
    nbits = int(np.log2(tq))
    assert (1 << nbits) == tq

    def body(ds_ref, m_ref, o_ref):
        x = ds_ref[...]
        row = lax.broadcasted_iota(jnp.int32, x.shape, 0)
        for b in range(nbits):
            rolled = pltpu.roll(x, w - (1 << b), 1)
            x = jnp.where(((row >> b) & 1) == 1, rolled, x)
        t = jnp.sum(x, axis=0, keepdims=True)
        o_ref[...] = lax.dot_general(t, m_ref[...], (NN, ((), ())), precision=lax.Precision.HIGHEST,
                                     preferred_element_type=F32)

    out = pl.pallas_call(
        body, grid=(H,),
        in_specs=[pl.BlockSpec((None, tq, w), lambda h: (h, 0, 0)),
                  pl.BlockSpec((w, nbin_pad), lambda h: (0, 0))],
        out_specs=pl.BlockSpec((None, 1, nbin_pad), lambda h: (h, 0, 0)),
        out_shape=jax.ShapeDtypeStruct((H, 1, nbin_pad), F32),
        name="rel_bias_grad", compiler_params=_cparams("parallel"))(ds_sum, jnp.asarray(onehot))
    return out[:, 0, :nbin]


def _conv_t(S):
    return _pick(S, (256,))


def _fill_zbuf(zbuf, ap_ref, bp_ref, a_ref, b_ref, i):
    zp = ap_ref[...].astype(F32) * _sigmoid(bp_ref[...].astype(F32))
    zbuf[0:CONV_HALO, :] = jnp.where(i == 0, 0.0, zp)
    zbuf[CONV_HALO:, :] = a_ref[...].astype(F32) * _sigmoid(b_ref[...].astype(F32))


def conv_gate_fwd(proj, ya, cw, cb, lng, lnb, D):
    S = proj.shape[0]
    T = _conv_t(S)
    hb = T // CONV_HALO
    nlb = D // LANES

    def body(ap_ref, bp_ref, a_ref, b_ref, ga_ref, gb_ref, ya_ref, cw_ref, cb_ref, lng_ref, lnb_ref,
             y_ref, c_ref, zbuf):
        i = pl.program_id(0)
        _fill_zbuf(zbuf, ap_ref, bp_ref, a_ref, b_ref, i)

        def lane_block(lb, carry):
            lanes = pl.ds(pl.multiple_of(lb * LANES, LANES), LANES)
            acc = jnp.zeros((T, LANES), F32)
            for k in range(CONV_WIDTH):
                acc = acc + cw_ref[k:k + 1, lanes] * zbuf[pl.ds(CONV_HALO - CONV_WIDTH + 1 + k, T), lanes]
            c_ref[:, lanes] = acc + cb_ref[:, lanes]
            return carry

        lax.fori_loop(0, nlb, lane_block, 0)
        c = c_ref[...]
        mu = jnp.mean(c, axis=-1, keepdims=True)
        xc = c - mu
        rstd = lax.rsqrt(jnp.mean(xc * xc, axis=-1, keepdims=True) + EPS)
        ln = xc * rstd * lng_ref[...] + lnb_ref[...]
        yb = ln * _sigmoid(ln)
        ga = ga_ref[...].astype(F32)
        gb = gb_ref[...].astype(F32)
        y_ref[:, :D] = (ya_ref[...].astype(F32) * (ga * _sigmoid(ga))).astype(y_ref.dtype)
        y_ref[:, D:] = (yb * (gb * _sigmoid(gb))).astype(y_ref.dtype)

    def cur(cidx):
        return pl.BlockSpec((T, D), lambda i: (i, cidx))

    def prev(cidx):
        return pl.BlockSpec((CONV_HALO, D), lambda i: (jnp.maximum(i * hb - 1, 0), cidx))

    vec = pl.BlockSpec((1, D), lambda i: (0, 0))
    return pl.pallas_call(
        body, grid=(S // T,),
        in_specs=[prev(3), prev(4), cur(3), cur(4), cur(5), cur(6), pl.BlockSpec((T, D), lambda i: (i, 0)),
                  pl.BlockSpec((CONV_HALO, D), lambda i: (0, 0)), vec, vec, vec],
        out_specs=[pl.BlockSpec((T, 2 * D), lambda i: (i, 0)), pl.BlockSpec((T, D), lambda i: (i, 0))],
        out_shape=[jax.ShapeDtypeStruct((S, 2 * D), BF16), jax.ShapeDtypeStruct((S, D), F32)],
        scratch_shapes=[pltpu.VMEM((T + CONV_HALO, D), F32)],
        name="conv_gate_fwd", compiler_params=_cparams("parallel"))(
            proj, proj, proj, proj, proj, proj, ya, cw, cb, lng, lnb)


def conv_gate_bwd_a(dy0, proj, ya, cpre, lng, lnb, D):
    S = proj.shape[0]
    T = _conv_t(S)

    def body(dy_ref, ga_ref, gb_ref, ya_ref, c_ref, lng_ref, lnb_ref,
             dya_ref, dg_ref, dc_ref, dlng_ref, dlnb_ref):
        i = pl.program_id(0)
        c = c_ref[...]
        gv = lng_ref[...]
        mu = jnp.mean(c, axis=-1, keepdims=True)
        xc = c - mu
        rstd = lax.rsqrt(jnp.mean(xc * xc, axis=-1, keepdims=True) + EPS)
        xhat = xc * rstd
        ln = xhat * gv + lnb_ref[...]
        sl = _sigmoid(ln)
        yb = ln * sl
        ga = ga_ref[...].astype(F32)
        gb = gb_ref[...].astype(F32)
        sa = _sigmoid(ga)
        sb = _sigmoid(gb)
        dy_a = dy_ref[:, :D].astype(F32)
        dy_b = dy_ref[:, D:].astype(F32)
        dya_ref[...] = (dy_a * (ga * sa)).astype(dya_ref.dtype)
        dg_ref[:, :D] = (dy_a * ya_ref[...].astype(F32) * (sa * (1.0 + ga * (1.0 - sa)))).astype(dg_ref.dtype)
        dg_ref[:, D:] = (dy_b * yb * (sb * (1.0 + gb * (1.0 - sb)))).astype(dg_ref.dtype)
        dln = dy_b * (gb * sb) * (sl * (1.0 + ln * (1.0 - sl)))
        dxhat = dln * gv
        dc_ref[...] = rstd * (dxhat - jnp.mean(dxhat, axis=-1, keepdims=True)
                              - xhat * jnp.mean(dxhat * xhat, axis=-1, keepdims=True))
        dlng = jnp.sum(dln * xhat, axis=0, keepdims=True)
        dlnb = jnp.sum(dln, axis=0, keepdims=True)

        @pl.when(i == 0)
        def _():
            dlng_ref[...] = dlng
            dlnb_ref[...] = dlnb

        @pl.when(i > 0)
        def _():
            dlng_ref[...] += dlng
            dlnb_ref[...] += dlnb

    row = pl.BlockSpec((T, D), lambda i: (i, 0))
    vec = pl.BlockSpec((1, D), lambda i: (0, 0))
    return pl.pallas_call(
        body, grid=(S // T,),
        in_specs=[pl.BlockSpec((T, 2 * D), lambda i: (i, 0)),
                  pl.BlockSpec((T, D), lambda i: (i, 5)), pl.BlockSpec((T, D), lambda i: (i, 6)),
                  row, row, vec, vec],
        out_specs=[row, pl.BlockSpec((T, 2 * D), lambda i: (i, 0)), row, vec, vec],
        out_shape=[jax.ShapeDtypeStruct((S, D), BF16), jax.ShapeDtypeStruct((S, 2 * D), BF16),
                   jax.ShapeDtypeStruct((S, D), F32), jax.ShapeDtypeStruct((1, D), F32),
                   jax.ShapeDtypeStruct((1, D), F32)],
        name="conv_gate_bwd_a", compiler_params=_cparams("arbitrary"))(
            dy0, proj, proj, ya, cpre, lng, lnb)


def conv_gate_bwd_b(dc, proj, cw, D):
    S = proj.shape[0]
    T = _conv_t(S)
    hb = T // CONV_HALO
    nt = S // T
    nlb = D // LANES

    def body(dc_ref, dn_ref, ap_ref, bp_ref, a_ref, b_ref, cw_ref, da_ref, db_ref, dcw_ref, dcb_ref,
             zbuf, dcbuf):
        i = pl.program_id(0)
        _fill_zbuf(zbuf, ap_ref, bp_ref, a_ref, b_ref, i)
        dcv = dc_ref[...]
        dcbuf[0:T, :] = dcv
        dcbuf[T:, :] = jnp.where(i == nt - 1, 0.0, dn_ref[...])

        @pl.when(i == 0)
        def _():
            dcw_ref[...] = jnp.zeros_like(dcw_ref)
            dcb_ref[...] = jnp.zeros_like(dcb_ref)

        dcb_ref[...] += jnp.sum(dcv, axis=0, keepdims=True)

        def lane_block(lb, carry):
            lanes = pl.ds(pl.multiple_of(lb * LANES, LANES), LANES)
            d0 = dcbuf[0:T, lanes]
            dz = jnp.zeros((T, LANES), F32)
            for k in range(CONV_WIDTH):
                dz = dz + cw_ref[k:k + 1, lanes] * dcbuf[pl.ds(CONV_WIDTH - 1 - k, T), lanes]
                zs = zbuf[pl.ds(CONV_HALO - CONV_WIDTH + 1 + k, T), lanes]
                dcw_ref[k:k + 1, lanes] += jnp.sum(d0 * zs, axis=0, keepdims=True)
            av = a_ref[:, lanes].astype(F32)
            sg = _sigmoid(b_ref[:, lanes].astype(F32))
            da_ref[:, lanes] = (dz * sg).astype(da_ref.dtype)
            db_ref[:, lanes] = (dz * av * sg * (1.0 - sg)).astype(db_ref.dtype)
            return carry

        lax.fori_loop(0, nlb, lane_block, 0)

    def cur(cidx):
        return pl.BlockSpec((T, D), lambda i: (i, cidx))

    def prev(cidx):
        return pl.BlockSpec((CONV_HALO, D), lambda i: (jnp.maximum(i * hb - 1, 0), cidx))

    row = pl.BlockSpec((T, D), lambda i: (i, 0))
    nxt = pl.BlockSpec((CONV_HALO, D), lambda i: (jnp.minimum((i + 1) * hb, nt * hb - 1), 0))
    return pl.pallas_call(
        body, grid=(nt,),
        in_specs=[row, nxt, prev(3), prev(4), cur(3), cur(4), pl.BlockSpec((CONV_HALO, D), lambda i: (0, 0))],
        out_specs=[row, row, pl.BlockSpec((CONV_HALO, D), lambda i: (0, 0)),
                   pl.BlockSpec((1, D), lambda i: (0, 0))],
        out_shape=[jax.ShapeDtypeStruct((S, D), BF16), jax.ShapeDtypeStruct((S, D), BF16),
                   jax.ShapeDtypeStruct((CONV_HALO, D), F32), jax.ShapeDtypeStruct((1, D), F32)],
        scratch_shapes=[pltpu.VMEM((T + CONV_HALO, D), F32), pltpu.VMEM((T + CONV_HALO, D), F32)],
        name="conv_gate_bwd_b", compiler_params=_cparams("arbitrary"))(
            dc, dc, proj, proj, proj, proj, cw)


def assemble_dproj0(dq, dkc, dkp, dvc, dvp, da, db, dgate, D):
    S = dq.shape[0]
    tq = _attn_tq(S)
    T = _pick(S, (256,))
    shift = tq // T
    nt = S // T

    def body(dq_ref, dkc_ref, dkp_ref, dvc_ref, dvp_ref, da_ref, db_ref, dg_ref, o_ref):
        i = pl.program_id(0)
        last = i + shift >= nt
        o_ref[:, 0:D] = dq_ref[...]
        dk = dkc_ref[...].astype(F32) + jnp.where(last, 0.0, dkp_ref[...].astype(F32))
        dv = dvc_ref[...].astype(F32) + jnp.where(last, 0.0, dvp_ref[...].astype(F32))
        o_ref[:, D:2 * D] = dk.astype(o_ref.dtype)
        o_ref[:, 2 * D:3 * D] = dv.astype(o_ref.dtype)
        o_ref[:, 3 * D:4 * D] = da_ref[...]
        o_ref[:, 4 * D:5 * D] = db_ref[...]
        o_ref[:, 5 * D:] = dg_ref[...]

    row = pl.BlockSpec((T, D), lambda i: (i, 0))
    nxt = pl.BlockSpec((T, D), lambda i: (jnp.minimum(i + shift, nt - 1), 0))
    return pl.pallas_call(
        body, grid=(nt,),
        in_specs=[row, row, nxt, row, nxt, row, row, pl.BlockSpec((T, 2 * D), lambda i: (i, 0))],
        out_specs=pl.BlockSpec((T, 7 * D), lambda i: (i, 0)),
        out_shape=jax.ShapeDtypeStruct((S, 7 * D), BF16),
        name="assemble_dproj0", compiler_params=_cparams("parallel"))(dq, dkc, dkp, dvc, dvp, da, db, dgate)


def _sgu_t(S):
    return _pick(S, (256, 128))


def _ws_masked(ws_ref, g):
    row = lax.broadcasted_iota(jnp.int32, (GMLP_CHUNK, GMLP_CHUNK), 0) // CHUNK
    col = lax.broadcasted_iota(jnp.int32, (GMLP_CHUNK, GMLP_CHUNK), 1) // CHUNK
    return jnp.where(row >= col, ws_ref[g], 0.0), row >= col


def sgu_fwd(proj, lng, lnb, ws, bst, MIX):
    S = proj.shape[0]
    T = _sgu_t(S)
    gw = MIX // N_GROUPS_C

    def body(u_ref, v_ref, g_ref, lng_ref, lnb_ref, ws_ref, bst_ref, y_ref):
        v = v_ref[...].astype(F32)
        mu = jnp.mean(v, axis=-1, keepdims=True)
        xc = v - mu
        rstd = lax.rsqrt(jnp.mean(xc * xc, axis=-1, keepdims=True) + EPS)
        for g in range(N_GROUPS_C):
            cols = slice(g * gw, (g + 1) * gw)
            wsm = _ws_masked(ws_ref, g)[0].astype(BF16)
            vn = (xc[:, cols] * rstd * lng_ref[:, cols] + lnb_ref[:, cols]).astype(BF16)
            for blk in range(T // GMLP_CHUNK):
                rows = slice(blk * GMLP_CHUNK, (blk + 1) * GMLP_CHUNK)
                sg = _dot(wsm, vn[rows], NN) + bst_ref[:, g:g + 1]
                gate = g_ref[rows, cols].astype(F32)
                y = u_ref[rows, cols].astype(F32) * sg * (gate * _sigmoid(gate))
                y_ref[rows, cols] = y.astype(y_ref.dtype)

    def part(cidx):
        return pl.BlockSpec((T, MIX), lambda i: (i, cidx))

    vec = pl.BlockSpec((1, MIX), lambda i: (0, 0))
    return pl.pallas_call(
        body, grid=(S // T,),
        in_specs=[part(0), part(1), part(2), vec, vec,
                  pl.BlockSpec((N_GROUPS_C, GMLP_CHUNK, GMLP_CHUNK), lambda i: (0, 0, 0)),
                  pl.BlockSpec((GMLP_CHUNK, N_GROUPS_C), lambda i: (0, 0))],
        out_specs=pl.BlockSpec((T, MIX), lambda i: (i, 0)),
        out_shape=jax.ShapeDtypeStruct((S, MIX), BF16),
        name="sgu_fwd", compiler_params=_cparams("parallel"))(proj, proj, proj, lng, lnb, ws, bst)


def sgu_bwd(dy1, proj, lng, lnb, ws, bst, MIX):
    S = proj.shape[0]
    T = _sgu_t(S)
    gw = MIX // N_GROUPS_C

    def body(dy_ref, u_ref, v_ref, g_ref, lng_ref, lnb_ref, ws_ref, bst_ref,
             dp_ref, dws_ref, dbst_ref, dlng_ref, dlnb_ref, dvn_buf):
        i = pl.program_id(0)

        @pl.when(i == 0)
        def _():
            dws_ref[...] = jnp.zeros_like(dws_ref)
            dbst_ref[...] = jnp.zeros_like(dbst_ref)
            dlng_ref[...] = jnp.zeros_like(dlng_ref)
            dlnb_ref[...] = jnp.zeros_like(dlnb_ref)

        v = v_ref[...].astype(F32)
        mu = jnp.mean(v, axis=-1, keepdims=True)
        xc = v - mu
        rstd = lax.rsqrt(jnp.mean(xc * xc, axis=-1, keepdims=True) + EPS)
        for g in range(N_GROUPS_C):
            cols = slice(g * gw, (g + 1) * gw)
            wsf, keep = _ws_masked(ws_ref, g)
            wsm = wsf.astype(BF16)
            vn = (xc[:, cols] * rstd * lng_ref[:, cols] + lnb_ref[:, cols]).astype(BF16)
            for blk in range(T // GMLP_CHUNK):
                rows = slice(blk * GMLP_CHUNK, (blk + 1) * GMLP_CHUNK)
                vnb = vn[rows]
                sg = _dot(wsm, vnb, NN) + bst_ref[:, g:g + 1]
                gate = g_ref[rows, cols].astype(F32)
                sig = _sigmoid(gate)
                sil = gate * sig
                u = u_ref[rows, cols].astype(F32)
                dy = dy_ref[rows, cols].astype(F32)
                dp_ref[rows, g * gw:(g + 1) * gw] = (dy * sg * sil).astype(dp_ref.dtype)
                dp_ref[rows, 2 * MIX + g * gw:2 * MIX + (g + 1) * gw] = (
                    dy * u * sg * (sig * (1.0 + gate * (1.0 - sig)))).astype(dp_ref.dtype)
                dsg = dy * u * sil
                dsgb = dsg.astype(BF16)
                dvn_buf[rows, cols] = _dot(wsm, dsgb, TN)
                dws_ref[g] += jnp.where(keep, _dot(dsgb, vnb, NT), 0.0)
                dbst_ref[:, g:g + 1] += jnp.sum(dsg, axis=-1, keepdims=True)
        dvn = dvn_buf[...]
        xhat = xc * rstd
        dxhat = dvn * lng_ref[...]
        dv = rstd * (dxhat - jnp.mean(dxhat, axis=-1, keepdims=True)
                     - xhat * jnp.mean(dxhat * xhat, axis=-1, keepdims=True))
        dp_ref[:, MIX:2 * MIX] = dv.astype(dp_ref.dtype)
        dlng_ref[...] += jnp.sum(dvn * xhat, axis=0, keepdims=True)
        dlnb_ref[...] += jnp.sum(dvn, axis=0, keepdims=True)

    def part(cidx):
        return pl.BlockSpec((T, MIX), lambda i: (i, cidx))

    vec = pl.BlockSpec((1, MIX), lambda i: (0, 0))
    wspec = pl.BlockSpec((N_GROUPS_C, GMLP_CHUNK, GMLP_CHUNK), lambda i: (0, 0, 0))
    bspec = pl.BlockSpec((GMLP_CHUNK, N_GROUPS_C), lambda i: (0, 0))
    return pl.pallas_call(
        body, grid=(S // T,),
        in_specs=[pl.BlockSpec((T, MIX), lambda i: (i, 0)), part(0), part(1), part(2), vec, vec, wspec, bspec],
        out_specs=[pl.BlockSpec((T, 3 * MIX), lambda i: (i, 0)), wspec, bspec, vec, vec],
        out_shape=[jax.ShapeDtypeStruct((S, 3 * MIX), BF16),
                   jax.ShapeDtypeStruct((N_GROUPS_C, GMLP_CHUNK, GMLP_CHUNK), F32),
                   jax.ShapeDtypeStruct((GMLP_CHUNK, N_GROUPS_C), F32),
                   jax.ShapeDtypeStruct((1, MIX), F32), jax.ShapeDtypeStruct((1, MIX), F32)],
        scratch_shapes=[pltpu.VMEM((T, MIX), F32)],
        name="sgu_bwd", compiler_params=_cparams("arbitrary"))(dy1, proj, proj, proj, lng, lnb, ws, bst)


def xattn_fwd(name, q, k, v):
    S, D = q.shape
    nm = k.shape[0]
    dh = D // N_HEADS_X
    tq = _pick(S, (512, 256))
    scale = dh ** -0.5

    def body(q_ref, k_ref, v_ref, o_ref, lse_ref):
        s = _dot(q_ref[...], k_ref[...], NT) * scale
        m = jnp.max(s, axis=-1, keepdims=True)
        p = jnp.exp(s - m)
        l = jnp.sum(p, axis=-1, keepdims=True)
        o_ref[...] = (_dot(p.astype(BF16), v_ref[...], NN) / l).astype(o_ref.dtype)
        lse_ref[...] = m + jnp.log(l)

    return pl.pallas_call(
        body, grid=(N_HEADS_X, S // tq),
        in_specs=[pl.BlockSpec((tq, dh), lambda h, i: (i, h)),
                  pl.BlockSpec((nm, dh), lambda h, i: (0, h)), pl.BlockSpec((nm, dh), lambda h, i: (0, h))],
        out_specs=[pl.BlockSpec((tq, dh), lambda h, i: (i, h)),
                   pl.BlockSpec((None, tq, 1), lambda h, i: (h, i, 0))],
        out_shape=[jax.ShapeDtypeStruct((S, D), BF16), jax.ShapeDtypeStruct((N_HEADS_X, S, 1), F32)],
        name=name, compiler_params=_cparams("parallel", "parallel"))(q, k, v)


def xattn_bwd(name, q, k, v, o, do, lse):
    S, D = q.shape
    nm = k.shape[0]
    dh = D // N_HEADS_X
    tq = _pick(S, (512, 256))
    scale = dh ** -0.5

    def body(q_ref, k_ref, v_ref, o_ref, do_ref, lse_ref, dq_ref, dk_ref, dv_ref):
        i = pl.program_id(1)
        q_v = q_ref[...]
        k_v = k_ref[...]
        do_v = do_ref[...]
        p = jnp.exp(_dot(q_v, k_v, NT) * scale - lse_ref[...])
        delta = jnp.sum(do_v.astype(F32) * o_ref[...].astype(F32), axis=-1, keepdims=True)
        dv = _dot(p.astype(BF16), do_v, TN)
        ds = (p * (_dot(do_v, v_ref[...], NT) - delta)).astype(BF16)
        dq_ref[...] = (_dot(ds, k_v, NN) * scale).astype(dq_ref.dtype)
        dk = _dot(ds, q_v, TN) * scale

        @pl.when(i == 0)
        def _():
            dk_ref[...] = dk
            dv_ref[...] = dv

        @pl.when(i > 0)
        def _():
            dk_ref[...] += dk
            dv_ref[...] += dv

    qs = pl.BlockSpec((tq, dh), lambda h, i: (i, h))
    ks = pl.BlockSpec((nm, dh), lambda h, i: (0, h))
    return pl.pallas_call(
        body, grid=(N_HEADS_X, S // tq),
        in_specs=[qs, ks, ks, qs, qs, pl.BlockSpec((None, tq, 1), lambda h, i: (h, i, 0))],
        out_specs=[qs, ks, ks],
        out_shape=[jax.ShapeDtypeStruct((S, D), BF16), jax.ShapeDtypeStruct((nm, D), F32),
                   jax.ShapeDtypeStruct((nm, D), F32)],
        name=name, compiler_params=_cparams("parallel", "arbitrary"))(q, k, v, o, do, lse)


def adamw(name, w, g, m, v):
    R, C = w.shape
    tr = _pick(R, tuple(t for t in (512, 256, 128, 64, 32, 16, 8) if t * C * 4 <= (1 << 20)) or (8,))
    c1 = 1.0 - ADAM_B1 ** ADAM_STEP
    c2 = 1.0 - ADAM_B2 ** ADAM_STEP

    def body(w_ref, g_ref, m_ref, v_ref, d_ref, nm_ref, nv_ref):
        gv = g_ref[...]
        nm = ADAM_B1 * m_ref[...] + (1.0 - ADAM_B1) * gv
        nv = ADAM_B2 * v_ref[...] + (1.0 - ADAM_B2) * (gv * gv)
        d_ref[...] = -ADAM_LR * ((nm / c1) / (jnp.sqrt(nv / c2) + ADAM_EPS) + ADAM_WD * w_ref[...])
        nm_ref[...] = nm
        nv_ref[...] = nv

    blk = pl.BlockSpec((tr, C), lambda i: (i, 0))
    sd = jax.ShapeDtypeStruct((R, C), F32)
    return pl.pallas_call(body, grid=(R // tr,), in_specs=[blk] * 4, out_specs=[blk] * 3,
                          out_shape=[sd, sd, sd], name=name, compiler_params=_cparams("parallel"))(w, g, m, v)


def add_halves(name, g4, recv, cidx):
    _, R, C = g4.shape
    rh = R // 2
    tr = _pick(rh, (256, 128, 64, 32, 16))
    nrb = rh // tr

    def body(c_ref, a_ref, b_ref, o_ref):
        o_ref[...] = (a_ref[...].astype(F32) + b_ref[...].astype(F32)).astype(o_ref.dtype)

    grid_spec = pltpu.PrefetchScalarGridSpec(
        num_scalar_prefetch=1, grid=(4, nrb),
        in_specs=[pl.BlockSpec((None, tr, C), lambda j, r, c_ref: (j, c_ref[0] * nrb + r, 0)),
                  pl.BlockSpec((None, tr, C), lambda j, r, c_ref: (j, r, 0))],
        out_specs=pl.BlockSpec((None, tr, C), lambda j, r, c_ref: (j, r, 0)))
    return pl.pallas_call(body, grid_spec=grid_spec, out_shape=jax.ShapeDtypeStruct((4, rh, C), BF16),
                          name=name, compiler_params=_cparams("parallel", "parallel"))(cidx, g4, recv)


def sum_chips(name, parts):
    _, R, C = parts.shape
    tr = _pick(R, (256, 128, 64, 32, 16))

    def body(p_ref, o_ref):
        acc = p_ref[0].astype(F32)
        for j in range(1, N_CHIPS):
            acc = acc + p_ref[j].astype(F32)
        o_ref[...] = acc

    return pl.pallas_call(body, grid=(R // tr,),
                          in_specs=[pl.BlockSpec((N_CHIPS, tr, C), lambda r: (0, r, 0))],
                          out_specs=pl.BlockSpec((tr, C), lambda r: (r, 0)),
                          out_shape=jax.ShapeDtypeStruct((R, C), F32), name=name,
                          compiler_params=_cparams("parallel"))(parts)


def _place():
    return lax.axis_index("x"), lax.axis_index("y"), lax.axis_index("c")


_CHIP_FLIPS = ((1, 0), (0, 1), (1, 1))


def _flip(v, bit):
    return 1 - v if bit else v


HBM_SPEC = pl.BlockSpec(memory_space=pl.ANY)


def exchange_small(name, buf, reduce):
    R = buf.shape[0]

    def body(x_ref, *refs):
        if reduce:
            sum_ref, all_ref, send_sems, recv_sems, local_sem = refs
        else:
            all_ref, send_sems, recv_sems, local_sem = refs
        x, y, c = _place()
        me = 4 * x + 2 * y + c
        mine = pltpu.make_async_copy(x_ref, all_ref.at[me], local_sem)
        mine.start()
        sends = []
        for k in range(1, N_DEV):
            peer = (_flip(x, k & 4), _flip(y, k & 2), _flip(c, k & 1))
            cp = pltpu.make_async_remote_copy(src_ref=x_ref, dst_ref=all_ref.at[me], send_sem=send_sems.at[k - 1],
                                              recv_sem=recv_sems.at[k - 1], device_id=peer, device_id_type=MESH)
            cp.start()
            sends.append(cp)
        for k in range(1, N_DEV):
            peer = (_flip(x, k & 4), _flip(y, k & 2), _flip(c, k & 1))
            src = 4 * peer[0] + 2 * peer[1] + peer[2]
            pltpu.make_async_remote_copy(src_ref=x_ref, dst_ref=all_ref.at[src], send_sem=send_sems.at[k - 1],
                                         recv_sem=recv_sems.at[k - 1], device_id=peer,
                                         device_id_type=MESH).wait_recv()
        for cp in sends:
            cp.wait_send()
        mine.wait()
        if reduce:
            acc = all_ref[0]
            for d in range(1, N_DEV):
                acc = acc + all_ref[d]
            sum_ref[...] = acc

    vm = pl.BlockSpec(memory_space=pltpu.VMEM)
    sems = [pltpu.SemaphoreType.DMA((N_DEV - 1,)), pltpu.SemaphoreType.DMA((N_DEV - 1,)), pltpu.SemaphoreType.DMA]
    if reduce:
        return pl.pallas_call(
            body, in_specs=[vm], out_specs=vm, out_shape=jax.ShapeDtypeStruct((R, LANES), F32),
            scratch_shapes=[pltpu.VMEM((N_DEV, R, LANES), F32)] + sems, name=name,
            compiler_params=pltpu.CompilerParams(vmem_limit_bytes=V7X_VMEM_LIMIT))(buf)
    return pl.pallas_call(
        body, in_specs=[vm], out_specs=vm, out_shape=jax.ShapeDtypeStruct((N_DEV, R, LANES), F32),
        scratch_shapes=sems, name=name,
        compiler_params=pltpu.CompilerParams(vmem_limit_bytes=V7X_VMEM_LIMIT))(buf)


def gather_weights(shards):
    n = len(shards)

    def body(*refs):
        s_refs, o_refs = refs[:n], refs[n:2 * n]
        send_sems, recv_sems, local_sems = refs[2 * n:]
        x, y, c = _place()
        me = 2 * x + y
        sib = (x, y, 1 - c)
        chips = [(_flip(x, fx), _flip(y, fy)) for fx, fy in _CHIP_FLIPS]
        locals_, sends = [], []
        for t in range(n):
            lc = pltpu.make_async_copy(s_refs[t], o_refs[t].at[me], local_sems.at[t])
            lc.start()
            locals_.append(lc)
        for t in range(n):
            rh = s_refs[t].shape[0] // 2
            mine = pl.ds(c * rh, rh)
            for k, (px, py) in enumerate(chips):
                cp = pltpu.make_async_remote_copy(
                    src_ref=s_refs[t].at[mine], dst_ref=o_refs[t].at[me, mine],
                    send_sem=send_sems.at[t, k], recv_sem=recv_sems.at[t, k],
                    device_id=(px, py, c), device_id_type=MESH)
                cp.start()
                sends.append(cp)
        for t in range(n):
            rh = s_refs[t].shape[0] // 2
            mine = pl.ds(c * rh, rh)
            for k, (px, py) in enumerate(chips):
                landed = o_refs[t].at[2 * px + py, mine]
                pltpu.make_async_remote_copy(
                    src_ref=landed, dst_ref=landed, send_sem=send_sems.at[t, k], recv_sem=recv_sems.at[t, k],
                    device_id=(px, py, c), device_id_type=MESH).wait_recv()
                fw = pltpu.make_async_remote_copy(
                    src_ref=landed, dst_ref=landed, send_sem=send_sems.at[t, 3 + k],
                    recv_sem=recv_sems.at[t, 3 + k], device_id=sib, device_id_type=MESH)
                fw.start()
                sends.append(fw)
        for t in range(n):
            rh = s_refs[t].shape[0] // 2
            theirs = pl.ds((1 - c) * rh, rh)
            for k, (px, py) in enumerate(chips):
                landed = o_refs[t].at[2 * px + py, theirs]
                pltpu.make_async_remote_copy(
                    src_ref=landed, dst_ref=landed, send_sem=send_sems.at[t, 3 + k],
                    recv_sem=recv_sems.at[t, 3 + k], device_id=sib, device_id_type=MESH).wait_recv()
        for cp in sends:
            cp.wait_send()
        for lc in locals_:
            lc.wait()

    return pl.pallas_call(
        body, in_specs=[HBM_SPEC] * n, out_specs=[HBM_SPEC] * n,
        out_shape=[jax.ShapeDtypeStruct((N_CHIPS,) + s.shape, s.dtype) for s in shards],
        scratch_shapes=[pltpu.SemaphoreType.DMA((n, 6)), pltpu.SemaphoreType.DMA((n, 6)),
                        pltpu.SemaphoreType.DMA((n,))],
        name="gather_weights")(*shards)


def send_sibling_halves(grads):
    n = len(grads)

    def body(*refs):
        g_refs, o_refs = refs[:n], refs[n:2 * n]
        send_sems, recv_sems = refs[2 * n:]
        x, y, c = _place()
        sib = (x, y, 1 - c)
        cps = []
        for t in range(n):
            rh = g_refs[t].shape[1] // 2
            cp = pltpu.make_async_remote_copy(
                src_ref=g_refs[t].at[:, pl.ds((1 - c) * rh, rh), :], dst_ref=o_refs[t],
                send_sem=send_sems.at[t], recv_sem=recv_sems.at[t], device_id=sib, device_id_type=MESH)
            cp.start()
            cps.append(cp)
        for cp in cps:
            cp.wait_recv()
        for cp in cps:
            cp.wait_send()

    return pl.pallas_call(
        body, in_specs=[HBM_SPEC] * n, out_specs=[HBM_SPEC] * n,
        out_shape=[jax.ShapeDtypeStruct((4, g.shape[1] // 2, g.shape[2]), g.dtype) for g in grads],
        scratch_shapes=[pltpu.SemaphoreType.DMA((n,)), pltpu.SemaphoreType.DMA((n,))],
        name="send_sibling_halves")(*grads)


def scatter_chip_partials(parts):
    n = len(parts)

    def body(*refs):
        p_refs, o_refs = refs[:n], refs[n:2 * n]
        send_sems, recv_sems, local_sems = refs[2 * n:]
        x, y, c = _place()
        me = 2 * x + y
        chips = [(_flip(x, fx), _flip(y, fy)) for fx, fy in _CHIP_FLIPS]
        locals_, sends = [], []
        for t in range(n):
            lc = pltpu.make_async_copy(p_refs[t].at[me], o_refs[t].at[me], local_sems.at[t])
            lc.start()
            locals_.append(lc)
            for k, (px, py) in enumerate(chips):
                cp = pltpu.make_async_remote_copy(
                    src_ref=p_refs[t].at[2 * px + py], dst_ref=o_refs[t].at[me],
                    send_sem=send_sems.at[t, k], recv_sem=recv_sems.at[t, k],
                    device_id=(px, py, c), device_id_type=MESH)
                cp.start()
                sends.append(cp)
        for t in range(n):
            for k, (px, py) in enumerate(chips):
                landed = o_refs[t].at[2 * px + py]
                pltpu.make_async_remote_copy(
                    src_ref=landed, dst_ref=landed, send_sem=send_sems.at[t, k], recv_sem=recv_sems.at[t, k],
                    device_id=(px, py, c), device_id_type=MESH).wait_recv()
        for cp in sends:
            cp.wait_send()
        for lc in locals_:
            lc.wait()

    return pl.pallas_call(
        body, in_specs=[HBM_SPEC] * n, out_specs=[HBM_SPEC] * n,
        out_shape=[jax.ShapeDtypeStruct(p.shape, p.dtype) for p in parts],
        scratch_shapes=[pltpu.SemaphoreType.DMA((n, 3)), pltpu.SemaphoreType.DMA((n, 3)),
                        pltpu.SemaphoreType.DMA((n,))],
        name="scatter_chip_partials")(*parts)


def share_reduced_halves(halves):
    n = len(halves)

    def body(*refs):
        h_refs, o_refs = refs[:n], refs[n:2 * n]
        send_sems, recv_sems, local_sems = refs[2 * n:]
        x, y, c = _place()
        sib = (x, y, 1 - c)
        cps, locals_ = [], []
        for t in range(n):
            rh = h_refs[t].shape[0]
            mine = pl.ds(c * rh, rh)
            lc = pltpu.make_async_copy(h_refs[t], o_refs[t].at[mine], local_sems.at[t])
            lc.start()
            locals_.append(lc)
            cp = pltpu.make_async_remote_copy(
                src_ref=h_refs[t], dst_ref=o_refs[t].at[mine], send_sem=send_sems.at[t],
                recv_sem=recv_sems.at[t], device_id=sib, device_id_type=MESH)
            cp.start()
            cps.append(cp)
        for t in range(n):
            rh = h_refs[t].shape[0]
            theirs = o_refs[t].at[pl.ds((1 - c) * rh, rh)]
            pltpu.make_async_remote_copy(
                src_ref=theirs, dst_ref=theirs, send_sem=send_sems.at[t], recv_sem=recv_sems.at[t],
                device_id=sib, device_id_type=MESH).wait_recv()
        for cp in cps:
            cp.wait_send()
        for lc in locals_:
            lc.wait()

    return pl.pallas_call(
        body, in_specs=[HBM_SPEC] * n, out_specs=[HBM_SPEC] * n,
        out_shape=[jax.ShapeDtypeStruct((2 * h.shape[0], h.shape[1]), h.dtype) for h in halves],
        scratch_shapes=[pltpu.SemaphoreType.DMA((n,)), pltpu.SemaphoreType.DMA((n,)),
                        pltpu.SemaphoreType.DMA((n,))],
        name="share_reduced_halves")(*halves)


def _pack(arrs):
    flat = []
    for a in arrs:
        v = a.reshape(-1).astype(F32)
        pad = (-v.shape[0]) % (8 * LANES)
        flat.append(jnp.pad(v, (0, pad)))
    return jnp.concatenate(flat).reshape(-1, LANES)


def _unpack(buf, shapes):
    out, off = [], 0
    flat = buf.reshape(-1)
    for s in shapes:
        n = int(np.prod(s))
        out.append(flat[off:off + n].reshape(s))
        off += n + ((-n) % (8 * LANES))
    return out


def _xattn_layer_fwd(tag, h, mem, gx, gmem, wr4, offs, D):
    hx = rms_fwd(f"rms_x{tag}", h, gx)
    memn = rms_fwd(f"rms_mem{tag}", mem, gmem)
    rq = D // 4
    q = mm_nn_rows(f"xq{tag}", hx, wr4, offs["q"], rq, BF16)
    k = mm_nn_rows(f"xk{tag}", memn, wr4, offs["k"], rq, BF16)
    v = mm_nn_rows(f"xv{tag}", memn, wr4, offs["v"], rq, BF16)
    o, lse = xattn_fwd(f"xattn_fwd{tag}", q, k, v)
    h_out = mm_nn_rows(f"xo{tag}", o, wr4, offs["o"], rq, F32, res=h)
    return h_out, dict(hx=hx, memn=memn, q=q, k=k, v=v, o=o, lse=lse)


def _xattn_layer_bwd(tag, dh_out, h_in, mem, gx, gmem, wr4, offs, sv, D):
    rq = D // 4
    do = mm_nt_rows(f"d_xo{tag}", dh_out, wr4, offs["o"], rq, BF16)
    dwo = mm_tn_rows(f"dw_xo{tag}", sv["o"], dh_out)
    dq, dk, dv = xattn_bwd(f"xattn_bwd{tag}", sv["q"], sv["k"], sv["v"], sv["o"], do, sv["lse"])
    dwq = mm_tn_rows(f"dw_xq{tag}", sv["hx"], dq)
    dhx = mm_nt_rows(f"d_xq{tag}", dq, wr4, offs["q"], rq, F32)
    dwk = mm_tn_rows(f"dw_xk{tag}", sv["memn"], dk)
    dwv = mm_tn_rows(f"dw_xv{tag}", sv["memn"], dv)
    dmk = mm_nt_rows(f"d_xk{tag}", dk, wr4, offs["k"], rq, F32)
    dmv = mm_nt_rows(f"d_xv{tag}", dv, wr4, offs["v"], rq, F32)
    dh_in, dgx = rms_bwd(f"rms_x_bwd{tag}", h_in, gx, [dhx], dh_out)
    _, dgmem = rms_bwd(f"rms_mem_bwd{tag}", mem, gmem, [dmk, dmv], None)
    return dh_in, dgx, dgmem, dict(q=dwq, k=dwk, v=dwv, o=dwo)


def kernel(x, mem, norm_mix_g, norm_x_g, norm_mem_g, final_norm_g, w_in_ab, rel_bias, conv_w, conv_b, conv_ln_g, conv_ln_b, w_out_ab, w_in_c, sgu_ln_g, sgu_ln_b, w_s, b_s, w_out_c, w_xq, w_xk, w_xv, w_xo, loss_target, m_norm_mix_g, m_norm_x_g, m_norm_mem_g, m_final_norm_g, m_w_in_ab, m_rel_bias, m_conv_w, m_conv_b, m_conv_ln_g, m_conv_ln_b, m_w_out_ab, m_w_in_c, m_sgu_ln_g, m_sgu_ln_b, m_w_s, m_b_s, m_w_out_c, m_w_xq, m_w_xk, m_w_xv, m_w_xo, v_norm_mix_g, v_norm_x_g, v_norm_mem_g, v_final_norm_g, v_w_in_ab, v_rel_bias, v_conv_w, v_conv_b, v_conv_ln_g, v_conv_ln_b, v_w_out_ab, v_w_in_c, v_sgu_ln_g, v_sgu_ln_b, v_w_s, v_b_s, v_w_out_c, v_w_xq, v_w_xk, v_w_xv, v_w_xo):
    S, D = x.shape[1], x.shape[2]
    MIX = 2 * D
    xs, mems, tgt = x[0], mem[0], loss_target[0]
    cx, cy, cc = _place()
    chip = 2 * cx + cy
    cidx = jnp.reshape(cc, (1,)).astype(jnp.int32)

    ro, rq = MIX // 4, D // 4
    stack = [w_out_ab[0], w_out_c[0]]
    offs = [dict(), dict()]
    off = 2 * ro
    for layer in range(2):
        for nm_, w in (("q", w_xq), ("k", w_xk), ("v", w_xv), ("o", w_xo)):
            stack.append(w[layer])
            offs[layer][nm_] = off
            off += rq
    wr_shard = jnp.concatenate([w.astype(BF16) for w in stack], axis=0)
    wab4, wc4, wr4 = gather_weights([w_in_ab[0].astype(BF16), w_in_c[0].astype(BF16), wr_shard])

    small_sh = [conv_w[0], sgu_ln_g[0], sgu_ln_b[0]]
    gathered = exchange_small("gather_small", _pack(small_sh), reduce=False)
    per_chip = [_unpack(gathered[2 * j], [a.shape for a in small_sh]) for j in range(N_CHIPS)]
    conv_w_full = jnp.concatenate([p[0] for p in per_chip], axis=1)
    sgu_g_full = jnp.concatenate([p[1] for p in per_chip], axis=0).reshape(1, MIX)
    sgu_b_full = jnp.concatenate([p[2] for p in per_chip], axis=0).reshape(1, MIX)
    cw_pad = jnp.pad(conv_w_full, ((0, CONV_HALO - CONV_WIDTH), (0, 0)))
    cb = conv_b.reshape(1, D)
    clg, clb = conv_ln_g.reshape(1, D), conv_ln_b.reshape(1, D)
    ws = w_s[0]
    bst = jnp.transpose(b_s[0])
    tq = _attn_tq(S)
    bm = band_bias_table(rel_bias[0], tq)

    hn0 = rms_fwd("rms_mix0", xs, norm_mix_g[0])
    proj0 = mm_nn_cols("proj_ab", hn0, wab4, BF16)
    ya, lse_a = attn_fwd(proj0, bm, D)
    y0, cpre = conv_gate_fwd(proj0, ya, cw_pad, cb, clg, clb, D)
    h1 = mm_nn_rows("out_ab", y0, wr4, 0, ro, F32, res=xs)
    h2, sx0 = _xattn_layer_fwd("0", h1, mems, norm_x_g[0], norm_mem_g[0], wr4, offs[0], D)
    hn1 = rms_fwd("rms_mix1", h2, norm_mix_g[1])
    proj1 = mm_nn_cols("proj_c", hn1, wc4, BF16)
    y1 = sgu_fwd(proj1, sgu_g_full, sgu_b_full, ws, bst, MIX)
    h3 = mm_nn_rows("out_c", y1, wr4, ro, ro, F32, res=h2)
    h4, sx1 = _xattn_layer_fwd("1", h3, mems, norm_x_g[1], norm_mem_g[1], wr4, offs[1], D)
    loss_row, dg_final, dh4 = loss_head("loss_head", h4, final_norm_g, tgt)

    dh3, dgx1, dgmem1, dwx1 = _xattn_layer_bwd("1", dh4, h3, mems, norm_x_g[1], norm_mem_g[1], wr4, offs[1], sx1, D)
    dy1 = mm_nt_rows("d_out_c", dh3, wr4, ro, ro, BF16)
    dw_out_c = mm_tn_rows("dw_out_c", y1, dh3)
    dproj1, dws, dbst, dsgu_g, dsgu_b = sgu_bwd(dy1, proj1, sgu_g_full, sgu_b_full, ws, bst, MIX)
    dw_in_c = mm_tn_cols("dw_in_c", hn1, dproj1)
    dhn1 = mm_nt_cols("d_proj_c", dproj1, wc4, F32)
    dh2, dgmix1 = rms_bwd("rms_mix1_bwd", h2, norm_mix_g[1], [dhn1], dh3)
    dh1, dgx0, dgmem0, dwx0 = _xattn_layer_bwd("0", dh2, h1, mems, norm_x_g[0], norm_mem_g[0], wr4, offs[0], sx0, D)
    dy0 = mm_nt_rows("d_out_ab", dh1, wr4, 0, ro, BF16)
    dw_out_ab = mm_tn_rows("dw_out_ab", y0, dh1)
    dya, dgate, dc, dclg, dclb = conv_gate_bwd_a(dy0, proj0, ya, cpre, clg, clb, D)
    da, db, dcw, dcb = conv_gate_bwd_b(dc, proj0, cw_pad, D)
    dq, dkc, dkp, dvc, dvp, ds_sum = attn_bwd(proj0, ya, dya, lse_a, bm, D)
    drel = rel_bias_grad(ds_sum)
    dproj0 = assemble_dproj0(dq, dkc, dkp, dvc, dvp, da, db, dgate, D)
    dw_in_ab = mm_tn_cols("dw_in_ab", hn0, dproj0)
    dhn0 = mm_nt_cols("d_proj_ab", dproj0, wab4, F32)
    dx, dgmix0 = rms_bwd("rms_mix0_bwd", xs, norm_mix_g[0], [dhn0], dh1)

    dwr = jnp.concatenate([dw_out_ab, dw_out_c] + [dwx[nm_] for dwx in (dwx0, dwx1) for nm_ in "qkvo"], axis=1)
    big = [dw_in_ab, dw_in_c, dwr]
    recv1 = send_sibling_halves(big)
    chip_parts = [add_halves(f"add_halves{t}", g, r, cidx) for t, (g, r) in enumerate(zip(big, recv1))]
    recv2 = scatter_chip_partials(chip_parts)
    halves = [sum_chips(f"sum_chips{t}", p) for t, p in enumerate(recv2)]
    g_ab, g_c, g_r = share_reduced_halves(halves)

    small_full = [
        jnp.concatenate([dgmix0, dgmix1], axis=0), jnp.concatenate([dgx0, dgx1], axis=0),
        jnp.concatenate([dgmem0, dgmem1], axis=0), dg_final.reshape(D), drel[None],
        dcb, dclg, dclb, dws[None], jnp.transpose(dbst)[None],
        dcw[:CONV_WIDTH][None], dsgu_g, dsgu_b]
    summed = _unpack(exchange_small("reduce_small", _pack(small_full), reduce=True), [a.shape for a in small_full])
    (g_norm_mix, g_norm_x, g_norm_mem, g_final, g_rel, g_conv_b, g_clg, g_clb, g_ws, g_bs,
     g_conv_w_full, g_sgu_g_full, g_sgu_b_full) = summed
    cws = conv_w.shape[2]
    g_conv_w = lax.dynamic_slice_in_dim(g_conv_w_full, chip * cws, cws, axis=2)
    sgs = sgu_ln_g.shape[1]
    g_sgu_g = lax.dynamic_slice_in_dim(g_sgu_g_full, chip * sgs, sgs, axis=1)
    g_sgu_b = lax.dynamic_slice_in_dim(g_sgu_b_full, chip * sgs, sgs, axis=1)

    loss = lax.psum(loss_row[0, 0], ("x", "y", "c"))

    g_rows = {}
    g_rows["w_out_ab"] = g_r[0:ro][None]
    g_rows["w_out_c"] = g_r[ro:2 * ro][None]
    for nm_ in "qkvo":
        g_rows["w_x" + nm_] = jnp.stack([g_r[offs[0][nm_]:offs[0][nm_] + rq], g_r[offs[1][nm_]:offs[1][nm_] + rq]])
    grads = dict(
        norm_mix_g=g_norm_mix, norm_x_g=g_norm_x, norm_mem_g=g_norm_mem, final_norm_g=g_final,
        w_in_ab=g_ab[None], rel_bias=g_rel, conv_w=g_conv_w, conv_b=g_conv_b, conv_ln_g=g_clg, conv_ln_b=g_clb,
        w_out_ab=g_rows["w_out_ab"], w_in_c=g_c[None], sgu_ln_g=g_sgu_g, sgu_ln_b=g_sgu_b, w_s=g_ws, b_s=g_bs,
        w_out_c=g_rows["w_out_c"], w_xq=g_rows["w_xq"], w_xk=g_rows["w_xk"], w_xv=g_rows["w_xv"],
        w_xo=g_rows["w_xo"])
    weights = dict(
        norm_mix_g=(norm_mix_g, m_norm_mix_g, v_norm_mix_g), norm_x_g=(norm_x_g, m_norm_x_g, v_norm_x_g),
        norm_mem_g=(norm_mem_g, m_norm_mem_g, v_norm_mem_g), final_norm_g=(final_norm_g, m_final_norm_g, v_final_norm_g),
        w_in_ab=(w_in_ab, m_w_in_ab, v_w_in_ab), rel_bias=(rel_bias, m_rel_bias, v_rel_bias),
        conv_w=(conv_w, m_conv_w, v_conv_w), conv_b=(conv_b, m_conv_b, v_conv_b),
        conv_ln_g=(conv_ln_g, m_conv_ln_g, v_conv_ln_g), conv_ln_b=(conv_ln_b, m_conv_ln_b, v_conv_ln_b),
        w_out_ab=(w_out_ab, m_w_out_ab, v_w_out_ab), w_in_c=(w_in_c, m_w_in_c, v_w_in_c),
        sgu_ln_g=(sgu_ln_g, m_sgu_ln_g, v_sgu_ln_g), sgu_ln_b=(sgu_ln_b, m_sgu_ln_b, v_sgu_ln_b),
        w_s=(w_s, m_w_s, v_w_s), b_s=(b_s, m_b_s, v_b_s), w_out_c=(w_out_c, m_w_out_c, v_w_out_c),
        w_xq=(w_xq, m_w_xq, v_w_xq), w_xk=(w_xk, m_w_xk, v_w_xk), w_xv=(w_xv, m_w_xv, v_w_xv),
        w_xo=(w_xo, m_w_xo, v_w_xo))
    names = list(weights)
    big_names = ("w_in_ab", "w_out_ab", "w_in_c", "w_out_c", "w_xq", "w_xk", "w_xv", "w_xo")
    delta, new_m, new_v = {}, {}, {}
    for nm_ in big_names:
        w, m, v = weights[nm_]
        C = w.shape[-1]
        d2, m2, v2 = adamw("adamw_" + nm_, w.reshape(-1, C), grads[nm_].reshape(-1, C), m.reshape(-1, C),
                           v.reshape(-1, C))
        delta[nm_], new_m[nm_], new_v[nm_] = d2.reshape(w.shape), m2.reshape(w.shape), v2.reshape(w.shape)
    small_names = [n for n in names if n not in big_names]
    shapes = [weights[n][0].shape for n in small_names]
    d_s, m_s, v_s = adamw("adamw_small", _pack([weights[n][0] for n in small_names]),
                          _pack([grads[n] for n in small_names]), _pack([weights[n][1] for n in small_names]),
                          _pack([weights[n][2] for n in small_names]))
    for n, d_, m_, v_ in zip(small_names, _unpack(d_s, shapes), _unpack(m_s, shapes), _unpack(v_s, shapes)):
        delta[n], new_m[n], new_v[n] = d_, m_, v_

    return (loss, dx[None], *[grads[n].reshape(weights[n][0].shape) for n in names], *[delta[n] for n in names],
            *[new_m[n] for n in names], *[new_v[n] for n in names])
```

```python
import functools

import numpy as np
import jax
import jax.numpy as jnp
from jax import lax
from jax.experimental import pallas as pl
from jax.experimental.pallas import tpu as pltpu

F32 = jnp.float32
BF16 = jnp.bfloat16
MESH = pl.DeviceIdType.MESH

EPS = 1e-6
CHUNK = 64
N_PAST_CHUNKS = 8
MAX_REL = 128
HEAD_DIM_A = 128
CONV_WIDTH = 31
CONV_HALO = 32
GMLP_CHUNK = 128
N_GROUPS_C = 8
N_HEADS_X = 4
NEG = -1e30

ADAM_LR = 0.001
ADAM_B1 = 0.9
ADAM_B2 = 0.999
ADAM_EPS = 1e-08
ADAM_WD = 0.01
ADAM_STEP = 10

N_CHIPS = 4
N_DEV = 8
V7X_VMEM_LIMIT = 56 * 1024 * 1024
LANES = 128
MXU = 256


def _pick(n, cands):
    for c in cands:
        if c <= n and n % c == 0:
            return c
    return n


def _cparams(*sem):
    return pltpu.CompilerParams(dimension_semantics=sem, vmem_limit_bytes=V7X_VMEM_LIMIT)


def _sigmoid(x):
    return 1.0 / (1.0 + jnp.exp(-x))


def _dot(a, b, contract):
    return lax.dot_general(a, b, (contract, ((), ())), preferred_element_type=F32)


NN = ((1,), (0,))
NT = ((1,), (1,))
TN = ((0,), (0,))


def _mm(name, a, b, *, contract, grid, a_spec, b_spec, o_spec, out_shape, res=None):
    nk = grid[2]

    def body(*refs):
        if res is not None:
            a_ref, b_ref, r_ref, o_ref = refs[:4]
        else:
            a_ref, b_ref, o_ref = refs[:3]
            r_ref = None
        p = _dot(a_ref[...].astype(BF16), b_ref[...].astype(BF16), contract)

        def finish(acc):
            if r_ref is not None:
                acc = acc + r_ref[...]
            o_ref[...] = acc.astype(o_ref.dtype)

        if nk == 1:
            finish(p)
        else:
            acc_ref = refs[-1]
            k = pl.program_id(2)

            @pl.when(k == 0)
            def _():
                acc_ref[...] = p

            @pl.when(k > 0)
            def _():
                acc_ref[...] += p

            @pl.when(k == nk - 1)
            def _():
                finish(acc_ref[...])

    in_specs = [a_spec, b_spec]
    args = [a, b]
    if res is not None:
        in_specs.append(o_spec)
        args.append(res)
    blk = tuple(d for d in o_spec.block_shape if d is not None)
    scratch = [] if nk == 1 else [pltpu.VMEM(blk, F32)]
    return pl.pallas_call(
        body, grid=grid, in_specs=in_specs, out_specs=o_spec, out_shape=out_shape,
        scratch_shapes=scratch, name=name,
        compiler_params=_cparams("parallel", "parallel", "arbitrary"))(*args)


def mm_nn_cols(name, a, w4, out_dtype):
    M, K = a.shape
    _, _, C = w4.shape
    tm = _pick(M, (1024, 512, 256))
    tn = _pick(C, (512, 256, 128))
    nps = C // tn
    return _mm(name, a, w4, contract=NN, grid=(M // tm, 4 * nps, 1),
               a_spec=pl.BlockSpec((tm, K), lambda i, j, k: (i, 0)),
               b_spec=pl.BlockSpec((None, K, tn), lambda i, j, k: (j // nps, 0, j % nps)),
               o_spec=pl.BlockSpec((tm, tn), lambda i, j, k: (i, j)),
               out_shape=jax.ShapeDtypeStruct((M, 4 * C), out_dtype))


def mm_nn(name, a, w, out_dtype, res=None):
    M, K = a.shape
    N = w.shape[1]
    tm = _pick(M, (1024, 512, 256))
    tn = _pick(N, (512, 256, 128))
    return _mm(name, a, w, contract=NN, grid=(M // tm, N // tn, 1),
               a_spec=pl.BlockSpec((tm, K), lambda i, j, k: (i, 0)),
               b_spec=pl.BlockSpec((K, tn), lambda i, j, k: (0, j)),
               o_spec=pl.BlockSpec((tm, tn), lambda i, j, k: (i, j)),
               out_shape=jax.ShapeDtypeStruct((M, N), out_dtype), res=res)


def mm_nt_cols(name, a, w4, out_dtype):
    M = a.shape[0]
    _, K, C = w4.shape
    tm = _pick(M, (1024, 512, 256))
    tn = _pick(K, (1024, 512, 256, 128))
    tk = _pick(C, (1792, 1536, 1024, 512, 256, 128))
    kps = C // tk
    return _mm(name, a, w4, contract=NT, grid=(M // tm, K // tn, 4 * kps),
               a_spec=pl.BlockSpec((tm, tk), lambda i, j, k: (i, k)),
               b_spec=pl.BlockSpec((None, tn, tk), lambda i, j, k: (k // kps, j, k % kps)),
               o_spec=pl.BlockSpec((tm, tn), lambda i, j, k: (i, j)),
               out_shape=jax.ShapeDtypeStruct((M, K), out_dtype))


def mm_nt(name, a, w, out_dtype):
    M, C = a.shape
    N = w.shape[0]
    tm = _pick(M, (1024, 512, 256))
    tn = _pick(N, (512, 256, 128))
    return _mm(name, a, w, contract=NT, grid=(M // tm, N // tn, 1),
               a_spec=pl.BlockSpec((tm, C), lambda i, j, k: (i, 0)),
               b_spec=pl.BlockSpec((tn, C), lambda i, j, k: (j, 0)),
               o_spec=pl.BlockSpec((tm, tn), lambda i, j, k: (i, j)),
               out_shape=jax.ShapeDtypeStruct((M, N), out_dtype))


def mm_tn_cols(name, a, b):
    S, K = a.shape
    C = b.shape[1] // 4
    ts = _pick(S, (1024, 512, 256))
    tko = _pick(K, (2048, 1024, 512, 256, 128))
    tn = _pick(C, (1024, 512, 256, 128))
    nps = C // tn
    return _mm(name, a, b, contract=TN, grid=(K // tko, 4 * nps, S // ts),
               a_spec=pl.BlockSpec((ts, tko), lambda i, j, k: (k, i)),
               b_spec=pl.BlockSpec((ts, tn), lambda i, j, k: (k, j)),
               o_spec=pl.BlockSpec((None, tko, tn), lambda i, j, k: (j // nps, i, j % nps)),
               out_shape=jax.ShapeDtypeStruct((4, K, C), BF16))


def mm_tn(name, a, b):
    S, K = a.shape
    N = b.shape[1]
    ts = _pick(S, (1024, 512, 256))
    tko = _pick(K, (2048, 1024, 512, 256, 128))
    tn = _pick(N, (1024, 512, 256, 128))
    return _mm(name, a, b, contract=TN, grid=(K // tko, N // tn, S // ts),
               a_spec=pl.BlockSpec((ts, tko), lambda i, j, k: (k, i)),
               b_spec=pl.BlockSpec((ts, tn), lambda i, j, k: (k, j)),
               o_spec=pl.BlockSpec((tko, tn), lambda i, j, k: (i, j)),
               out_shape=jax.ShapeDtypeStruct((K, N), BF16))


def rms_fwd(name, x, g):
    S, D = x.shape
    T = _pick(S, (512, 256))

    def body(x_ref, g_ref, o_ref):
        xf = x_ref[...]
        r = lax.rsqrt(jnp.mean(xf * xf, axis=-1, keepdims=True) + EPS)
        o_ref[...] = (xf * r * g_ref[...]).astype(o_ref.dtype)

    return pl.pallas_call(
        body, grid=(S // T,),
        in_specs=[pl.BlockSpec((T, D), lambda i: (i, 0)), pl.BlockSpec((1, D), lambda i: (0, 0))],
        out_specs=pl.BlockSpec((T, D), lambda i: (i, 0)),
        out_shape=jax.ShapeDtypeStruct((S, D), BF16), name=name,
        compiler_params=_cparams("parallel"))(x, g.reshape(1, D))


def rms_bwd(name, x, g, dys, dres):
    S, D = x.shape
    T = _pick(S, (256,))
    ndy = len(dys)
    has_res = dres is not None

    def body(*refs):
        x_ref, g_ref = refs[0], refs[1]
        dy_refs = refs[2:2 + ndy]
        r_ref = refs[2 + ndy] if has_res else None
        dx_ref, dxb_ref, dg_ref = refs[-3], refs[-2], refs[-1]
        i = pl.program_id(0)
        xf = x_ref[...]
        r = lax.rsqrt(jnp.mean(xf * xf, axis=-1, keepdims=True) + EPS)
        xhat = xf * r
        dy = dy_refs[0][...].astype(F32)
        for d in dy_refs[1:]:
            dy = dy + d[...].astype(F32)
        dxhat = dy * g_ref[...]
        dx = r * (dxhat - xhat * jnp.mean(dxhat * xhat, axis=-1, keepdims=True))
        if has_res:
            dx = dx + r_ref[...]
        dx_ref[...] = dx
        dxb_ref[...] = dx.astype(dxb_ref.dtype)
        dg = jnp.sum(dy * xhat, axis=0, keepdims=True)

        @pl.when(i == 0)
        def _():
            dg_ref[...] = dg

        @pl.when(i > 0)
        def _():
            dg_ref[...] += dg

    row = pl.BlockSpec((T, D), lambda i: (i, 0))
    vec = pl.BlockSpec((1, D), lambda i: (0, 0))
    args = [x, g.reshape(1, D), *dys] + ([dres] if has_res else [])
    return pl.pallas_call(
        body, grid=(S // T,),
        in_specs=[row, vec] + [row] * (ndy + int(has_res)),
        out_specs=[row, row, vec],
        out_shape=[jax.ShapeDtypeStruct((S, D), F32), jax.ShapeDtypeStruct((S, D), BF16),
                   jax.ShapeDtypeStruct((1, D), F32)],
        name=name, compiler_params=_cparams("arbitrary"))(*args)


def loss_head(name, h, g, target):
    S, D = h.shape
    T = _pick(S, (256,))

    def body(h_ref, g_ref, t_ref, loss_ref, dg_ref, dh_ref, dhb_ref):
        i = pl.program_id(0)
        xf = h_ref[...]
        gv = g_ref[...]
        r = lax.rsqrt(jnp.mean(xf * xf, axis=-1, keepdims=True) + EPS)
        xhat = xf * r
        err = xhat * gv - t_ref[...]
        part = 0.5 * jnp.sum(jnp.sum(err * err, axis=-1, keepdims=True), axis=0, keepdims=True) / D
        dout = err / D
        dxhat = dout * gv
        dh = r * (dxhat - xhat * jnp.mean(dxhat * xhat, axis=-1, keepdims=True))
        dh_ref[...] = dh
        dhb_ref[...] = dh.astype(dhb_ref.dtype)
        dg = jnp.sum(dout * xhat, axis=0, keepdims=True)
        lrow = jnp.broadcast_to(part, (1, LANES))

        @pl.when(i == 0)
        def _():
            dg_ref[...] = dg
            loss_ref[...] = lrow

        @pl.when(i > 0)
        def _():
            dg_ref[...] += dg
            loss_ref[...] += lrow

    row = pl.BlockSpec((T, D), lambda i: (i, 0))
    vec = pl.BlockSpec((1, D), lambda i: (0, 0))
    return pl.pallas_call(
        body, grid=(S // T,), in_specs=[row, vec, row],
        out_specs=[pl.BlockSpec((1, LANES), lambda i: (0, 0)), vec, row, row],
        out_shape=[jax.ShapeDtypeStruct((1, LANES), F32), jax.ShapeDtypeStruct((1, D), F32),
                   jax.ShapeDtypeStruct((S, D), F32), jax.ShapeDtypeStruct((S, D), BF16)],
        name=name, compiler_params=_cparams("arbitrary"))(h, g.reshape(1, D), target)


def _attn_tq(S):
    return _pick(S, (512,))


def band_bias_table(rel_bias, tq):
    H = rel_bias.shape[0]
    w = 2 * tq
    nbits = int(np.log2(tq))
    assert (1 << nbits) == tq and (N_PAST_CHUNKS + 2) * CHUNK - 1 <= w
    c = np.arange(w)
    d0 = np.where(c <= tq + CHUNK - 1, tq - c, tq + w - c)
    base = jnp.take(rel_bias.astype(F32), jnp.asarray(np.clip(d0, -MAX_REL, MAX_REL) + MAX_REL), axis=1)

    def body(b_ref, o_ref):
        x = jnp.broadcast_to(b_ref[...], (tq, w))
        row = lax.broadcasted_iota(jnp.int32, (tq, w), 0)
        col = lax.broadcasted_iota(jnp.int32, (tq, w), 1)
        for b in range(nbits):
            x = jnp.where(((row >> b) & 1) == 1, pltpu.roll(x, 1 << b, 1), x)
        qc = row // CHUNK
        kc = col // CHUNK - tq // CHUNK
        o_ref[...] = jnp.where((kc <= qc) & (kc >= qc - N_PAST_CHUNKS), x, NEG)

    return pl.pallas_call(
        body, grid=(H,), in_specs=[pl.BlockSpec((None, 1, w), lambda h: (h, 0, 0))],
        out_specs=pl.BlockSpec((None, tq, w), lambda h: (h, 0, 0)),
        out_shape=jax.ShapeDtypeStruct((H, tq, w), F32), name="band_bias_table",
        compiler_params=_cparams("parallel"))(base.reshape(H, 1, w))


def attn_fwd(proj, bm, D):
    S = proj.shape[0]
    H = D // HEAD_DIM_A
    tq = _attn_tq(S)
    nb = S // tq
    scale = HEAD_DIM_A ** -0.5

    def body(q_ref, kp_ref, kc_ref, vp_ref, vc_ref, bm_ref, o_ref, lse_ref):
        i = pl.program_id(1)
        q = q_ref[...]
        sp = _dot(q, kp_ref[...], NT) * scale + bm_ref[:, :tq]
        sp = jnp.where(i == 0, NEG, sp)
        sc = _dot(q, kc_ref[...], NT) * scale + bm_ref[:, tq:]
        m = jnp.maximum(jnp.max(sp, axis=-1, keepdims=True), jnp.max(sc, axis=-1, keepdims=True))
        pp = jnp.exp(sp - m)
        pc = jnp.exp(sc - m)
        l = jnp.sum(pp, axis=-1, keepdims=True) + jnp.sum(pc, axis=-1, keepdims=True)
        o = _dot(pp.astype(BF16), vp_ref[...], NN) + _dot(pc.astype(BF16), vc_ref[...], NN)
        o_ref[...] = (o / l).astype(o_ref.dtype)
        lse_ref[...] = m + jnp.log(l)

    def col(base):
        return (pl.BlockSpec((tq, HEAD_DIM_A), lambda h, i: (jnp.maximum(i - 1, 0), base + h)),
                pl.BlockSpec((tq, HEAD_DIM_A), lambda h, i: (i, base + h)))

    kp, kc = col(H)
    vp, vc = col(2 * H)
    return pl.pallas_call(
        body, grid=(H, nb),
        in_specs=[pl.BlockSpec((tq, HEAD_DIM_A), lambda h, i: (i, h)), kp, kc, vp, vc,
                  pl.BlockSpec((None, tq, 2 * tq), lambda h, i: (h, 0, 0))],
        out_specs=[pl.BlockSpec((tq, HEAD_DIM_A), lambda h, i: (i, h)),
                   pl.BlockSpec((None, tq, 1), lambda h, i: (h, i, 0))],
        out_shape=[jax.ShapeDtypeStruct((S, D), BF16), jax.ShapeDtypeStruct((H, S, 1), F32)],
        name="attn_fwd", compiler_params=_cparams("parallel", "arbitrary"))(
            proj, proj, proj, proj, proj, bm)


def attn_bwd(proj, ya, dya, lse, bm, D):
    S = proj.shape[0]
    H = D // HEAD_DIM_A
    tq = _attn_tq(S)
    nb = S // tq
    scale = HEAD_DIM_A ** -0.5

    def body(q_ref, kp_ref, kc_ref, vp_ref, vc_ref, o_ref, do_ref, lse_ref, bm_ref,
             dq_ref, dkc_ref, dkp_ref, dvc_ref, dvp_ref, ds_ref):
        i = pl.program_id(1)
        q = q_ref[...]
        do = do_ref[...]
        delta = jnp.sum(do.astype(F32) * o_ref[...].astype(F32), axis=-1, keepdims=True)
        lse_v = lse_ref[...]

        def half(k_ref, v_ref, bias, first):
            k = k_ref[...]
            s = _dot(q, k, NT) * scale + bias
            if first:
                s = jnp.where(i == 0, NEG, s)
            p = jnp.exp(s - lse_v)
            dv = _dot(p.astype(BF16), do, TN)
            dp = _dot(do, v_ref[...], NT)
            ds = p * (dp - delta)
            dsb = ds.astype(BF16)
            dq = _dot(dsb, k, NN)
            dk = _dot(dsb, q, TN) * scale
            return ds, dq, dk, dv

        dsp, dqp, dkp, dvp = half(kp_ref, vp_ref, bm_ref[:, :tq], True)
        dsc, dqc, dkc, dvc = half(kc_ref, vc_ref, bm_ref[:, tq:], False)
        dq_ref[...] = ((dqp + dqc) * scale).astype(dq_ref.dtype)
        dkp_ref[...] = dkp.astype(dkp_ref.dtype)
        dkc_ref[...] = dkc.astype(dkc_ref.dtype)
        dvp_ref[...] = dvp.astype(dvp_ref.dtype)
        dvc_ref[...] = dvc.astype(dvc_ref.dtype)

        @pl.when(i == 0)
        def _():
            ds_ref[:, :tq] = dsp
            ds_ref[:, tq:] = dsc

        @pl.when(i > 0)
        def _():
            ds_ref[:, :tq] += dsp
            ds_ref[:, tq:] += dsc

    def col(base):
        return (pl.BlockSpec((tq, HEAD_DIM_A), lambda h, i: (jnp.maximum(i - 1, 0), base + h)),
                pl.BlockSpec((tq, HEAD_DIM_A), lambda h, i: (i, base + h)))

    kp, kc = col(H)
    vp, vc = col(2 * H)
    blk = pl.BlockSpec((tq, HEAD_DIM_A), lambda h, i: (i, h))
    sd = jax.ShapeDtypeStruct((S, D), BF16)
    return pl.pallas_call(
        body, grid=(H, nb),
        in_specs=[blk, kp, kc, vp, vc, blk, blk,
                  pl.BlockSpec((None, tq, 1), lambda h, i: (h, i, 0)),
                  pl.BlockSpec((None, tq, 2 * tq), lambda h, i: (h, 0, 0))],
        out_specs=[blk, blk, blk, blk, blk, pl.BlockSpec((None, tq, 2 * tq), lambda h, i: (h, 0, 0))],
        out_shape=[sd, sd, sd, sd, sd, jax.ShapeDtypeStruct((H, tq, 2 * tq), F32)],
        name="attn_bwd", compiler_params=_cparams("parallel", "arbitrary"))(
            proj, proj, proj, proj, proj, ya, dya, lse, bm)


def rel_bias_grad(ds_sum):
    H, tq, w = ds_sum.shape
    nbin = 2 * MAX_REL + 1
    nbin_pad = 3 * LANES
    d_lo, d_hi = -(CHUNK - 1), (N_PAST_CHUNKS + 1) * CHUNK - 1
    assert d_hi - d_lo + 1 <= w
    onehot = np.zeros((w, nbin_pad), np.float32)
    for d in range(d_lo, d_hi + 1):
        onehot[(tq - d) % w, int(np.clip(d, -MAX_REL, MAX_REL)) + MAX_REL] = 1.0
    nbits = int(np.log2(tq))
    assert (1 << nbits) == tq

    def body(ds_ref, m_ref, o_ref):
        x = ds_ref[...]
        row = lax.broadcasted_iota(jnp.int32, x.shape, 0)
        for b in range(nbits):
            rolled = pltpu.roll(x, w - (1 << b), 1)
            x = jnp.where(((row >> b) & 1) == 1, rolled, x)
        t = jnp.sum(x, axis=0, keepdims=True)
        o_ref[...] = lax.dot_general(t, m_ref[...], (NN, ((), ())), precision=lax.Precision.HIGHEST,
                                     preferred_element_type=F32)

    out = pl.pallas_call(
        body, grid=(H,),
        in_specs=[pl.BlockSpec((None, tq, w), lambda h: (h, 0, 0)),
                  pl.BlockSpec((w, nbin_pad), lambda h: (0, 0))],
        out_specs=pl.BlockSpec((None, 1, nbin_pad), lambda h: (h, 0, 0)),
        out_shape=jax.ShapeDtypeStruct((H, 1, nbin_pad), F32),
        name="rel_bias_grad", compiler_params=_cparams("parallel"))(ds_sum, jnp.asarray(onehot))
    return out[:, 0, :nbin]


def _conv_t(S):
    return _pick(S, (256,))


def _fill_zbuf(zbuf, ap_ref, bp_ref, a_ref, b_ref, i):
    zp = ap_ref[...].astype(F32) * _sigmoid(bp_ref[...].astype(F32))
    zbuf[0:CONV_HALO, :] = jnp.where(i == 0, 0.0, zp)
    zbuf[CONV_HALO:, :] = a_ref[...].astype(F32) * _sigmoid(b_ref[...].astype(F32))


def conv_gate_fwd(proj, ya, cw, cb, lng, lnb, D):
    S = proj.shape[0]
    T = _conv_t(S)
    hb = T // CONV_HALO
    nlb = D // LANES

    def body(ap_ref, bp_ref, a_ref, b_ref, ga_ref, gb_ref, ya_ref, cw_ref, cb_ref, lng_ref, lnb_ref,
             y_ref, c_ref, zbuf):
        i = pl.program_id(0)
        _fill_zbuf(zbuf, ap_ref, bp_ref, a_ref, b_ref, i)

        def lane_block(lb, carry):
            lanes = pl.ds(pl.multiple_of(lb * LANES, LANES), LANES)
            acc = jnp.zeros((T, LANES), F32)
            for k in range(CONV_WIDTH):
                acc = acc + cw_ref[k:k + 1, lanes] * zbuf[pl.ds(CONV_HALO - CONV_WIDTH + 1 + k, T), lanes]
            c_ref[:, lanes] = acc + cb_ref[:, lanes]
            return carry

        lax.fori_loop(0, nlb, lane_block, 0)
        c = c_ref[...]
        mu = jnp.mean(c, axis=-1, keepdims=True)
        xc = c - mu
        rstd = lax.rsqrt(jnp.mean(xc * xc, axis=-1, keepdims=True) + EPS)
        ln = xc * rstd * lng_ref[...] + lnb_ref[...]
        yb = ln * _sigmoid(ln)
        ga = ga_ref[...].astype(F32)
        gb = gb_ref[...].astype(F32)
        y_ref[:, :D] = (ya_ref[...].astype(F32) * (ga * _sigmoid(ga))).astype(y_ref.dtype)
        y_ref[:, D:] = (yb * (gb * _sigmoid(gb))).astype(y_ref.dtype)

    def cur(cidx):
        return pl.BlockSpec((T, D), lambda i: (i, cidx))

    def prev(cidx):
        return pl.BlockSpec((CONV_HALO, D), lambda i: (jnp.maximum(i * hb - 1, 0), cidx))

    vec = pl.BlockSpec((1, D), lambda i: (0, 0))
    return pl.pallas_call(
        body, grid=(S // T,),
        in_specs=[prev(3), prev(4), cur(3), cur(4), cur(5), cur(6), pl.BlockSpec((T, D), lambda i: (i, 0)),
                  pl.BlockSpec((CONV_HALO, D), lambda i: (0, 0)), vec, vec, vec],
        out_specs=[pl.BlockSpec((T, 2 * D), lambda i: (i, 0)), pl.BlockSpec((T, D), lambda i: (i, 0))],
        out_shape=[jax.ShapeDtypeStruct((S, 2 * D), BF16), jax.ShapeDtypeStruct((S, D), F32)],
        scratch_shapes=[pltpu.VMEM((T + CONV_HALO, D), F32)],
        name="conv_gate_fwd", compiler_params=_cparams("parallel"))(
            proj, proj, proj, proj, proj, proj, ya, cw, cb, lng, lnb)


def conv_gate_bwd_a(dy0, proj, ya, cpre, lng, lnb, D):
    S = proj.shape[0]
    T = _conv_t(S)

    def body(dy_ref, ga_ref, gb_ref, ya_ref, c_ref, lng_ref, lnb_ref,
             dya_ref, dg_ref, dc_ref, dlng_ref, dlnb_ref):
        i = pl.program_id(0)
        c = c_ref[...]
        gv = lng_ref[...]
        mu = jnp.mean(c, axis=-1, keepdims=True)
        xc = c - mu
        rstd = lax.rsqrt(jnp.mean(xc * xc, axis=-1, keepdims=True) + EPS)
        xhat = xc * rstd
        ln = xhat * gv + lnb_ref[...]
        sl = _sigmoid(ln)
        yb = ln * sl
        ga = ga_ref[...].astype(F32)
        gb = gb_ref[...].astype(F32)
        sa = _sigmoid(ga)
        sb = _sigmoid(gb)
        dy_a = dy_ref[:, :D].astype(F32)
        dy_b = dy_ref[:, D:].astype(F32)
        dya_ref[...] = (dy_a * (ga * sa)).astype(dya_ref.dtype)
        dg_ref[:, :D] = (dy_a * ya_ref[...].astype(F32) * (sa * (1.0 + ga * (1.0 - sa)))).astype(dg_ref.dtype)
        dg_ref[:, D:] = (dy_b * yb * (sb * (1.0 + gb * (1.0 - sb)))).astype(dg_ref.dtype)
        dln = dy_b * (gb * sb) * (sl * (1.0 + ln * (1.0 - sl)))
        dxhat = dln * gv
        dc_ref[...] = rstd * (dxhat - jnp.mean(dxhat, axis=-1, keepdims=True)
                              - xhat * jnp.mean(dxhat * xhat, axis=-1, keepdims=True))
        dlng = jnp.sum(dln * xhat, axis=0, keepdims=True)
        dlnb = jnp.sum(dln, axis=0, keepdims=True)

        @pl.when(i == 0)
        def _():
            dlng_ref[...] = dlng
            dlnb_ref[...] = dlnb

        @pl.when(i > 0)
        def _():
            dlng_ref[...] += dlng
            dlnb_ref[...] += dlnb

    row = pl.BlockSpec((T, D), lambda i: (i, 0))
    vec = pl.BlockSpec((1, D), lambda i: (0, 0))
    return pl.pallas_call(
        body, grid=(S // T,),
        in_specs=[pl.BlockSpec((T, 2 * D), lambda i: (i, 0)),
                  pl.BlockSpec((T, D), lambda i: (i, 5)), pl.BlockSpec((T, D), lambda i: (i, 6)),
                  row, row, vec, vec],
        out_specs=[row, pl.BlockSpec((T, 2 * D), lambda i: (i, 0)), row, vec, vec],
        out_shape=[jax.ShapeDtypeStruct((S, D), BF16), jax.ShapeDtypeStruct((S, 2 * D), BF16),
                   jax.ShapeDtypeStruct((S, D), F32), jax.ShapeDtypeStruct((1, D), F32),
                   jax.ShapeDtypeStruct((1, D), F32)],
        name="conv_gate_bwd_a", compiler_params=_cparams("arbitrary"))(
            dy0, proj, proj, ya, cpre, lng, lnb)


def conv_gate_bwd_b(dc, proj, cw, D):
    S = proj.shape[0]
    T = _conv_t(S)
    hb = T // CONV_HALO
    nt = S // T
    nlb = D // LANES

    def body(dc_ref, dn_ref, ap_ref, bp_ref, a_ref, b_ref, cw_ref, da_ref, db_ref, dcw_ref, dcb_ref,
             zbuf, dcbuf):
        i = pl.program_id(0)
        _fill_zbuf(zbuf, ap_ref, bp_ref, a_ref, b_ref, i)
        dcv = dc_ref[...]
        dcbuf[0:T, :] = dcv
        dcbuf[T:, :] = jnp.where(i == nt - 1, 0.0, dn_ref[...])

        @pl.when(i == 0)
        def _():
            dcw_ref[...] = jnp.zeros_like(dcw_ref)
            dcb_ref[...] = jnp.zeros_like(dcb_ref)

        dcb_ref[...] += jnp.sum(dcv, axis=0, keepdims=True)

        def lane_block(lb, carry):
            lanes = pl.ds(pl.multiple_of(lb * LANES, LANES), LANES)
            d0 = dcbuf[0:T, lanes]
            dz = jnp.zeros((T, LANES), F32)
            for k in range(CONV_WIDTH):
                dz = dz + cw_ref[k:k + 1, lanes] * dcbuf[pl.ds(CONV_WIDTH - 1 - k, T), lanes]
                zs = zbuf[pl.ds(CONV_HALO - CONV_WIDTH + 1 + k, T), lanes]
                dcw_ref[k:k + 1, lanes] += jnp.sum(d0 * zs, axis=0, keepdims=True)
            av = a_ref[:, lanes].astype(F32)
            sg = _sigmoid(b_ref[:, lanes].astype(F32))
            da_ref[:, lanes] = (dz * sg).astype(da_ref.dtype)
            db_ref[:, lanes] = (dz * av * sg * (1.0 - sg)).astype(db_ref.dtype)
            return carry

        lax.fori_loop(0, nlb, lane_block, 0)

    def cur(cidx):
        return pl.BlockSpec((T, D), lambda i: (i, cidx))

    def prev(cidx):
        return pl.BlockSpec((CONV_HALO, D), lambda i: (jnp.maximum(i * hb - 1, 0), cidx))

    row = pl.BlockSpec((T, D), lambda i: (i, 0))
    nxt = pl.BlockSpec((CONV_HALO, D), lambda i: (jnp.minimum((i + 1) * hb, nt * hb - 1), 0))
    return pl.pallas_call(
        body, grid=(nt,),
        in_specs=[row, nxt, prev(3), prev(4), cur(3), cur(4), pl.BlockSpec((CONV_HALO, D), lambda i: (0, 0))],
        out_specs=[row, row, pl.BlockSpec((CONV_HALO, D), lambda i: (0, 0)),
                   pl.BlockSpec((1, D), lambda i: (0, 0))],
        out_shape=[jax.ShapeDtypeStruct((S, D), BF16), jax.ShapeDtypeStruct((S, D), BF16),
                   jax.ShapeDtypeStruct((CONV_HALO, D), F32), jax.ShapeDtypeStruct((1, D), F32)],
        scratch_shapes=[pltpu.VMEM((T + CONV_HALO, D), F32), pltpu.VMEM((T + CONV_HALO, D), F32)],
        name="conv_gate_bwd_b", compiler_params=_cparams("arbitrary"))(
            dc, dc, proj, proj, proj, proj, cw)


def assemble_dproj0(dq, dkc, dkp, dvc, dvp, da, db, dgate, D):
    S = dq.shape[0]
    tq = _attn_tq(S)
    T = _pick(S, (256,))
    shift = tq // T
    nt = S // T

    def body(dq_ref, dkc_ref, dkp_ref, dvc_ref, dvp_ref, da_ref, db_ref, dg_ref, o_ref):
        i = pl.program_id(0)
        last = i + shift >= nt
        o_ref[:, 0:D] = dq_ref[...]
        dk = dkc_ref[...].astype(F32) + jnp.where(last, 0.0, dkp_ref[...].astype(F32))
        dv = dvc_ref[...].astype(F32) + jnp.where(last, 0.0, dvp_ref[...].astype(F32))
        o_ref[:, D:2 * D] = dk.astype(o_ref.dtype)
        o_ref[:, 2 * D:3 * D] = dv.astype(o_ref.dtype)
        o_ref[:, 3 * D:4 * D] = da_ref[...]
        o_ref[:, 4 * D:5 * D] = db_ref[...]
        o_ref[:, 5 * D:] = dg_ref[...]

    row = pl.BlockSpec((T, D), lambda i: (i, 0))
    nxt = pl.BlockSpec((T, D), lambda i: (jnp.minimum(i + shift, nt - 1), 0))
    return pl.pallas_call(
        body, grid=(nt,),
        in_specs=[row, row, nxt, row, nxt, row, row, pl.BlockSpec((T, 2 * D), lambda i: (i, 0))],
        out_specs=pl.BlockSpec((T, 7 * D), lambda i: (i, 0)),
        out_shape=jax.ShapeDtypeStruct((S, 7 * D), BF16),
        name="assemble_dproj0", compiler_params=_cparams("parallel"))(dq, dkc, dkp, dvc, dvp, da, db, dgate)


def _sgu_t(S):
    return _pick(S, (256, 128))


def _ws_masked(ws_ref, g):
    row = lax.broadcasted_iota(jnp.int32, (GMLP_CHUNK, GMLP_CHUNK), 0) // CHUNK
    col = lax.broadcasted_iota(jnp.int32, (GMLP_CHUNK, GMLP_CHUNK), 1) // CHUNK
    return jnp.where(row >= col, ws_ref[g], 0.0), row >= col


def sgu_fwd(proj, lng, lnb, ws, bst, MIX):
    S = proj.shape[0]
    T = _sgu_t(S)
    gw = MIX // N_GROUPS_C

    def body(u_ref, v_ref, g_ref, lng_ref, lnb_ref, ws_ref, bst_ref, y_ref):
        v = v_ref[...].astype(F32)
        mu = jnp.mean(v, axis=-1, keepdims=True)
        xc = v - mu
        rstd = lax.rsqrt(jnp.mean(xc * xc, axis=-1, keepdims=True) + EPS)
        for g in range(N_GROUPS_C):
            cols = slice(g * gw, (g + 1) * gw)
            wsm = _ws_masked(ws_ref, g)[0].astype(BF16)
            vn = (xc[:, cols] * rstd * lng_ref[:, cols] + lnb_ref[:, cols]).astype(BF16)
            for blk in range(T // GMLP_CHUNK):
                rows = slice(blk * GMLP_CHUNK, (blk + 1) * GMLP_CHUNK)
                sg = _dot(wsm, vn[rows], NN) + bst_ref[:, g:g + 1]
                gate = g_ref[rows, cols].astype(F32)
                y = u_ref[rows, cols].astype(F32) * sg * (gate * _sigmoid(gate))
                y_ref[rows, cols] = y.astype(y_ref.dtype)

    def part(cidx):
        return pl.BlockSpec((T, MIX), lambda i: (i, cidx))

    vec = pl.BlockSpec((1, MIX), lambda i: (0, 0))
    return pl.pallas_call(
        body, grid=(S // T,),
        in_specs=[part(0), part(1), part(2), vec, vec,
                  pl.BlockSpec((N_GROUPS_C, GMLP_CHUNK, GMLP_CHUNK), lambda i: (0, 0, 0)),
                  pl.BlockSpec((GMLP_CHUNK, N_GROUPS_C), lambda i: (0, 0))],
        out_specs=pl.BlockSpec((T, MIX), lambda i: (i, 0)),
        out_shape=jax.ShapeDtypeStruct((S, MIX), BF16),
        name="sgu_fwd", compiler_params=_cparams("parallel"))(proj, proj, proj, lng, lnb, ws, bst)


def sgu_bwd(dy1, proj, lng, lnb, ws, bst, MIX):
    S = proj.shape[0]
    T = _sgu_t(S)
    gw = MIX // N_GROUPS_C

    def body(dy_ref, u_ref, v_ref, g_ref, lng_ref, lnb_ref, ws_ref, bst_ref,
             dp_ref, dws_ref, dbst_ref, dlng_ref, dlnb_ref, dvn_buf):
        i = pl.program_id(0)

        @pl.when(i == 0)
        def _():
            dws_ref[...] = jnp.zeros_like(dws_ref)
            dbst_ref[...] = jnp.zeros_like(dbst_ref)
            dlng_ref[...] = jnp.zeros_like(dlng_ref)
            dlnb_ref[...] = jnp.zeros_like(dlnb_ref)

        v = v_ref[...].astype(F32)
        mu = jnp.mean(v, axis=-1, keepdims=True)
        xc = v - mu
        rstd = lax.rsqrt(jnp.mean(xc * xc, axis=-1, keepdims=True) + EPS)
        for g in range(N_GROUPS_C):
            cols = slice(g * gw, (g + 1) * gw)
            wsf, keep = _ws_masked(ws_ref, g)
            wsm = wsf.astype(BF16)
            vn = (xc[:, cols] * rstd * lng_ref[:, cols] + lnb_ref[:, cols]).astype(BF16)
            for blk in range(T // GMLP_CHUNK):
                rows = slice(blk * GMLP_CHUNK, (blk + 1) * GMLP_CHUNK)
                vnb = vn[rows]
                sg = _dot(wsm, vnb, NN) + bst_ref[:, g:g + 1]
                gate = g_ref[rows, cols].astype(F32)
                sig = _sigmoid(gate)
                sil = gate * sig
                u = u_ref[rows, cols].astype(F32)
                dy = dy_ref[rows, cols].astype(F32)
                dp_ref[rows, g * gw:(g + 1) * gw] = (dy * sg * sil).astype(dp_ref.dtype)
                dp_ref[rows, 2 * MIX + g * gw:2 * MIX + (g + 1) * gw] = (
                    dy * u * sg * (sig * (1.0 + gate * (1.0 - sig)))).astype(dp_ref.dtype)
                dsg = dy * u * sil
                dsgb = dsg.astype(BF16)
                dvn_buf[rows, cols] = _dot(wsm, dsgb, TN)
                dws_ref[g] += jnp.where(keep, _dot(dsgb, vnb, NT), 0.0)
                dbst_ref[:, g:g + 1] += jnp.sum(dsg, axis=-1, keepdims=True)
        dvn = dvn_buf[...]
        xhat = xc * rstd
        dxhat = dvn * lng_ref[...]
        dv = rstd * (dxhat - jnp.mean(dxhat, axis=-1, keepdims=True)
                     - xhat * jnp.mean(dxhat * xhat, axis=-1, keepdims=True))
        dp_ref[:, MIX:2 * MIX] = dv.astype(dp_ref.dtype)
        dlng_ref[...] += jnp.sum(dvn * xhat, axis=0, keepdims=True)
        dlnb_ref[...] += jnp.sum(dvn, axis=0, keepdims=True)

    def part(cidx):
        return pl.BlockSpec((T, MIX), lambda i: (i, cidx))

    vec = pl.BlockSpec((1, MIX), lambda i: (0, 0))
    wspec = pl.BlockSpec((N_GROUPS_C, GMLP_CHUNK, GMLP_CHUNK), lambda i: (0, 0, 0))
    bspec = pl.BlockSpec((GMLP_CHUNK, N_GROUPS_C), lambda i: (0, 0))
    return pl.pallas_call(
        body, grid=(S // T,),
        in_specs=[pl.BlockSpec((T, MIX), lambda i: (i, 0)), part(0), part(1), part(2), vec, vec, wspec, bspec],
        out_specs=[pl.BlockSpec((T, 3 * MIX), lambda i: (i, 0)), wspec, bspec, vec, vec],
        out_shape=[jax.ShapeDtypeStruct((S, 3 * MIX), BF16),
                   jax.ShapeDtypeStruct((N_GROUPS_C, GMLP_CHUNK, GMLP_CHUNK), F32),
                   jax.ShapeDtypeStruct((GMLP_CHUNK, N_GROUPS_C), F32),
                   jax.ShapeDtypeStruct((1, MIX), F32), jax.ShapeDtypeStruct((1, MIX), F32)],
        scratch_shapes=[pltpu.VMEM((T, MIX), F32)],
        name="sgu_bwd", compiler_params=_cparams("arbitrary"))(dy1, proj, proj, proj, lng, lnb, ws, bst)


def xattn_fwd(name, q, k, v):
    S, D = q.shape
    nm = k.shape[0]
    dh = D // N_HEADS_X
    tq = _pick(S, (512, 256))
    scale = dh ** -0.5

    def body(q_ref, k_ref, v_ref, o_ref, lse_ref):
        s = _dot(q_ref[...], k_ref[...], NT) * scale
        m = jnp.max(s, axis=-1, keepdims=True)
        p = jnp.exp(s - m)
        l = jnp.sum(p, axis=-1, keepdims=True)
        o_ref[...] = (_dot(p.astype(BF16), v_ref[...], NN) / l).astype(o_ref.dtype)
        lse_ref[...] = m + jnp.log(l)

    return pl.pallas_call(
        body, grid=(N_HEADS_X, S // tq),
        in_specs=[pl.BlockSpec((tq, dh), lambda h, i: (i, h)),
                  pl.BlockSpec((nm, dh), lambda h, i: (0, h)), pl.BlockSpec((nm, dh), lambda h, i: (0, h))],
        out_specs=[pl.BlockSpec((tq, dh), lambda h, i: (i, h)),
                   pl.BlockSpec((None, tq, 1), lambda h, i: (h, i, 0))],
        out_shape=[jax.ShapeDtypeStruct((S, D), BF16), jax.ShapeDtypeStruct((N_HEADS_X, S, 1), F32)],
        name=name, compiler_params=_cparams("parallel", "parallel"))(q, k, v)


def xattn_bwd(name, q, k, v, o, do, lse):
    S, D = q.shape
    nm = k.shape[0]
    dh = D // N_HEADS_X
    tq = _pick(S, (512, 256))
    scale = dh ** -0.5

    def body(q_ref, k_ref, v_ref, o_ref, do_ref, lse_ref, dq_ref, dk_ref, dv_ref):
        i = pl.program_id(1)
        q_v = q_ref[...]
        k_v = k_ref[...]
        do_v = do_ref[...]
        p = jnp.exp(_dot(q_v, k_v, NT) * scale - lse_ref[...])
        delta = jnp.sum(do_v.astype(F32) * o_ref[...].astype(F32), axis=-1, keepdims=True)
        dv = _dot(p.astype(BF16), do_v, TN)
        ds = (p * (_dot(do_v, v_ref[...], NT) - delta)).astype(BF16)
        dq_ref[...] = (_dot(ds, k_v, NN) * scale).astype(dq_ref.dtype)
        dk = _dot(ds, q_v, TN) * scale

        @pl.when(i == 0)
        def _():
            dk_ref[...] = dk
            dv_ref[...] = dv

        @pl.when(i > 0)
        def _():
            dk_ref[...] += dk
            dv_ref[...] += dv

    qs = pl.BlockSpec((tq, dh), lambda h, i: (i, h))
    ks = pl.BlockSpec((nm, dh), lambda h, i: (0, h))
    return pl.pallas_call(
        body, grid=(N_HEADS_X, S // tq),
        in_specs=[qs, ks, ks, qs, qs, pl.BlockSpec((None, tq, 1), lambda h, i: (h, i, 0))],
        out_specs=[qs, ks, ks],
        out_shape=[jax.ShapeDtypeStruct((S, D), BF16), jax.ShapeDtypeStruct((nm, D), F32),
                   jax.ShapeDtypeStruct((nm, D), F32)],
        name=name, compiler_params=_cparams("parallel", "arbitrary"))(q, k, v, o, do, lse)


def adamw(name, w, g, m, v):
    R, C = w.shape
    tr = _pick(R, tuple(t for t in (512, 256, 128, 64, 32, 16, 8) if t * C * 4 <= (1 << 20)) or (8,))
    c1 = 1.0 - ADAM_B1 ** ADAM_STEP
    c2 = 1.0 - ADAM_B2 ** ADAM_STEP

    def body(w_ref, g_ref, m_ref, v_ref, d_ref, nm_ref, nv_ref):
        gv = g_ref[...]
        nm = ADAM_B1 * m_ref[...] + (1.0 - ADAM_B1) * gv
        nv = ADAM_B2 * v_ref[...] + (1.0 - ADAM_B2) * (gv * gv)
        d_ref[...] = -ADAM_LR * ((nm / c1) / (jnp.sqrt(nv / c2) + ADAM_EPS) + ADAM_WD * w_ref[...])
        nm_ref[...] = nm
        nv_ref[...] = nv

    blk = pl.BlockSpec((tr, C), lambda i: (i, 0))
    sd = jax.ShapeDtypeStruct((R, C), F32)
    return pl.pallas_call(body, grid=(R // tr,), in_specs=[blk] * 4, out_specs=[blk] * 3,
                          out_shape=[sd, sd, sd], name=name, compiler_params=_cparams("parallel"))(w, g, m, v)


def add_halves(name, g4, recv, cidx):
    _, R, C = g4.shape
    rh = R // 2
    tr = _pick(rh, (256, 128, 64, 32, 16))
    nrb = rh // tr

    def body(c_ref, a_ref, b_ref, o_ref):
        o_ref[...] = (a_ref[...].astype(F32) + b_ref[...].astype(F32)).astype(o_ref.dtype)

    grid_spec = pltpu.PrefetchScalarGridSpec(
        num_scalar_prefetch=1, grid=(4, nrb),
        in_specs=[pl.BlockSpec((None, tr, C), lambda j, r, c_ref: (j, c_ref[0] * nrb + r, 0)),
                  pl.BlockSpec((None, tr, C), lambda j, r, c_ref: (j, r, 0))],
        out_specs=pl.BlockSpec((None, tr, C), lambda j, r, c_ref: (j, r, 0)))
    return pl.pallas_call(body, grid_spec=grid_spec, out_shape=jax.ShapeDtypeStruct((4, rh, C), BF16),
                          name=name, compiler_params=_cparams("parallel", "parallel"))(cidx, g4, recv)


def sum_chips(name, own, recv, place):
    _, rh, C = own.shape
    tr = _pick(rh, (256, 128, 64, 32, 16))
    nrb = rh // tr

    def body(s_ref, own_ref, recv_ref, o_ref):
        acc = own_ref[...].astype(F32)
        for k in range(N_CHIPS - 1):
            acc = acc + recv_ref[k].astype(F32)
        o_ref[...] = acc

    grid_spec = pltpu.PrefetchScalarGridSpec(
        num_scalar_prefetch=1, grid=(nrb,),
        in_specs=[pl.BlockSpec((None, tr, C), lambda r, s: (s[0], r, 0)),
                  pl.BlockSpec((N_CHIPS - 1, tr, C), lambda r, s: (0, r, 0))],
        out_specs=pl.BlockSpec((tr, C), lambda r, s: (s[1] * nrb + r, 0)))
    return pl.pallas_call(body, grid_spec=grid_spec, out_shape=jax.ShapeDtypeStruct((2 * rh, C), F32),
                          name=name, compiler_params=_cparams("parallel"))(place, own, recv)


def cast_into_slot(name, w, place):
    R, C = w.shape
    tr = _pick(R, (256, 128, 64, 32, 16))

    def body(s_ref, w_ref, o_ref):
        o_ref[...] = w_ref[...].astype(o_ref.dtype)

    grid_spec = pltpu.PrefetchScalarGridSpec(
        num_scalar_prefetch=1, grid=(R // tr,),
        in_specs=[pl.BlockSpec((tr, C), lambda r, s: (r, 0))],
        out_specs=pl.BlockSpec((None, tr, C), lambda r, s: (s[0], r, 0)))
    return pl.pallas_call(body, grid_spec=grid_spec, out_shape=jax.ShapeDtypeStruct((N_CHIPS, R, C), BF16),
                          name=name, compiler_params=_cparams("parallel"))(place, w)


def _place():
    return lax.axis_index("x"), lax.axis_index("y"), lax.axis_index("c")


_CHIP_FLIPS = ((1, 0), (0, 1), (1, 1))


def _flip(v, bit):
    return 1 - v if bit else v


HBM_SPEC = pl.BlockSpec(memory_space=pl.ANY)


def exchange_small(name, buf, reduce):
    R = buf.shape[0]

    def body(x_ref, *refs):
        if reduce:
            sum_ref, all_ref, send_sems, recv_sems, local_sem = refs
        else:
            all_ref, send_sems, recv_sems, local_sem = refs
        x, y, c = _place()
        me = 4 * x + 2 * y + c
        mine = pltpu.make_async_copy(x_ref, all_ref.at[me], local_sem)
        mine.start()
        sends = []
        for k in range(1, N_DEV):
            peer = (_flip(x, k & 4), _flip(y, k & 2), _flip(c, k & 1))
            cp = pltpu.make_async_remote_copy(src_ref=x_ref, dst_ref=all_ref.at[me], send_sem=send_sems.at[k - 1],
                                              recv_sem=recv_sems.at[k - 1], device_id=peer, device_id_type=MESH)
            cp.start()
            sends.append(cp)
        for k in range(1, N_DEV):
            peer = (_flip(x, k & 4), _flip(y, k & 2), _flip(c, k & 1))
            src = 4 * peer[0] + 2 * peer[1] + peer[2]
            pltpu.make_async_remote_copy(src_ref=x_ref, dst_ref=all_ref.at[src], send_sem=send_sems.at[k - 1],
                                         recv_sem=recv_sems.at[k - 1], device_id=peer,
                                         device_id_type=MESH).wait_recv()
        for cp in sends:
            cp.wait_send()
        mine.wait()
        if reduce:
            acc = all_ref[0]
            for d in range(1, N_DEV):
                acc = acc + all_ref[d]
            sum_ref[...] = acc

    vm = pl.BlockSpec(memory_space=pltpu.VMEM)
    sems = [pltpu.SemaphoreType.DMA((N_DEV - 1,)), pltpu.SemaphoreType.DMA((N_DEV - 1,)), pltpu.SemaphoreType.DMA]
    if reduce:
        return pl.pallas_call(
            body, in_specs=[vm], out_specs=vm, out_shape=jax.ShapeDtypeStruct((R, LANES), F32),
            scratch_shapes=[pltpu.VMEM((N_DEV, R, LANES), F32)] + sems, name=name,
            compiler_params=pltpu.CompilerParams(vmem_limit_bytes=V7X_VMEM_LIMIT))(buf)
    return pl.pallas_call(
        body, in_specs=[vm], out_specs=vm, out_shape=jax.ShapeDtypeStruct((N_DEV, R, LANES), F32),
        scratch_shapes=sems, name=name,
        compiler_params=pltpu.CompilerParams(vmem_limit_bytes=V7X_VMEM_LIMIT))(buf)


def gather_weights(slots):
    n = len(slots)

    def body(*refs):
        o_refs = refs[n:2 * n]
        send_sems, recv_sems = refs[2 * n:]
        x, y, c = _place()
        me = 2 * x + y
        sib = (x, y, 1 - c)
        chips = [(_flip(x, fx), _flip(y, fy)) for fx, fy in _CHIP_FLIPS]
        sends = []
        for t in range(n):
            rh = o_refs[t].shape[1] // 2
            mine = pl.ds(c * rh, rh)
            for k, (px, py) in enumerate(chips):
                own = o_refs[t].at[me, mine]
                cp = pltpu.make_async_remote_copy(
                    src_ref=own, dst_ref=own, send_sem=send_sems.at[t, k], recv_sem=recv_sems.at[t, k],
                    device_id=(px, py, c), device_id_type=MESH)
                cp.start()
                sends.append(cp)
        for t in range(n):
            rh = o_refs[t].shape[1] // 2
            mine = pl.ds(c * rh, rh)
            for k, (px, py) in enumerate(chips):
                landed = o_refs[t].at[2 * px + py, mine]
                pltpu.make_async_remote_copy(
                    src_ref=landed, dst_ref=landed, send_sem=send_sems.at[t, k], recv_sem=recv_sems.at[t, k],
                    device_id=(px, py, c), device_id_type=MESH).wait_recv()
                fw = pltpu.make_async_remote_copy(
                    src_ref=landed, dst_ref=landed, send_sem=send_sems.at[t, 3 + k],
                    recv_sem=recv_sems.at[t, 3 + k], device_id=sib, device_id_type=MESH)
                fw.start()
                sends.append(fw)
        for t in range(n):
            rh = o_refs[t].shape[1] // 2
            theirs = pl.ds((1 - c) * rh, rh)
            for k, (px, py) in enumerate(chips):
                landed = o_refs[t].at[2 * px + py, theirs]
                pltpu.make_async_remote_copy(
                    src_ref=landed, dst_ref=landed, send_sem=send_sems.at[t, 3 + k],
                    recv_sem=recv_sems.at[t, 3 + k], device_id=sib, device_id_type=MESH).wait_recv()
        for cp in sends:
            cp.wait_send()

    return pl.pallas_call(
        body, in_specs=[HBM_SPEC] * n, out_specs=[HBM_SPEC] * n,
        out_shape=[jax.ShapeDtypeStruct(s.shape, s.dtype) for s in slots],
        input_output_aliases={t: t for t in range(n)},
        scratch_shapes=[pltpu.SemaphoreType.DMA((n, 6)), pltpu.SemaphoreType.DMA((n, 6))],
        name="gather_weights")(*slots)


def send_sibling_halves(grads):
    n = len(grads)

    def body(*refs):
        g_refs, o_refs = refs[:n], refs[n:2 * n]
        send_sems, recv_sems = refs[2 * n:]
        x, y, c = _place()
        sib = (x, y, 1 - c)
        cps = []
        for t in range(n):
            rh = g_refs[t].shape[1] // 2
            cp = pltpu.make_async_remote_copy(
                src_ref=g_refs[t].at[:, pl.ds((1 - c) * rh, rh), :], dst_ref=o_refs[t],
                send_sem=send_sems.at[t], recv_sem=recv_sems.at[t], device_id=sib, device_id_type=MESH)
            cp.start()
            cps.append(cp)
        for cp in cps:
            cp.wait_recv()
        for cp in cps:
            cp.wait_send()

    return pl.pallas_call(
        body, in_specs=[HBM_SPEC] * n, out_specs=[HBM_SPEC] * n,
        out_shape=[jax.ShapeDtypeStruct((4, g.shape[1] // 2, g.shape[2]), g.dtype) for g in grads],
        scratch_shapes=[pltpu.SemaphoreType.DMA((n,)), pltpu.SemaphoreType.DMA((n,))],
        name="send_sibling_halves")(*grads)


def scatter_chip_partials(parts):
    n = len(parts)

    def body(*refs):
        p_refs, o_refs = refs[:n], refs[n:2 * n]
        send_sems, recv_sems = refs[2 * n:]
        x, y, c = _place()
        chips = [(_flip(x, fx), _flip(y, fy)) for fx, fy in _CHIP_FLIPS]
        sends = []
        for t in range(n):
            for k, (px, py) in enumerate(chips):
                cp = pltpu.make_async_remote_copy(
                    src_ref=p_refs[t].at[2 * px + py], dst_ref=o_refs[t].at[k],
                    send_sem=send_sems.at[t, k], recv_sem=recv_sems.at[t, k],
                    device_id=(px, py, c), device_id_type=MESH)
                cp.start()
                sends.append(cp)
        for cp in sends:
            cp.wait_recv()
        for cp in sends:
            cp.wait_send()

    return pl.pallas_call(
        body, in_specs=[HBM_SPEC] * n, out_specs=[HBM_SPEC] * n,
        out_shape=[jax.ShapeDtypeStruct((N_CHIPS - 1,) + p.shape[1:], p.dtype) for p in parts],
        scratch_shapes=[pltpu.SemaphoreType.DMA((n, 3)), pltpu.SemaphoreType.DMA((n, 3))],
        name="scatter_chip_partials")(*parts)


def share_reduced_halves(halves):
    n = len(halves)

    def body(*refs):
        o_refs = refs[n:2 * n]
        send_sems, recv_sems = refs[2 * n:]
        x, y, c = _place()
        sib = (x, y, 1 - c)
        cps = []
        for t in range(n):
            rh = o_refs[t].shape[0] // 2
            mine = o_refs[t].at[pl.ds(c * rh, rh)]
            cp = pltpu.make_async_remote_copy(
                src_ref=mine, dst_ref=mine, send_sem=send_sems.at[t], recv_sem=recv_sems.at[t],
                device_id=sib, device_id_type=MESH)
            cp.start()
            cps.append(cp)
        for t in range(n):
            rh = o_refs[t].shape[0] // 2
            theirs = o_refs[t].at[pl.ds((1 - c) * rh, rh)]
            pltpu.make_async_remote_copy(
                src_ref=theirs, dst_ref=theirs, send_sem=send_sems.at[t], recv_sem=recv_sems.at[t],
                device_id=sib, device_id_type=MESH).wait_recv()
        for cp in cps:
            cp.wait_send()

    return pl.pallas_call(
        body, in_specs=[HBM_SPEC] * n, out_specs=[HBM_SPEC] * n,
        out_shape=[jax.ShapeDtypeStruct(h.shape, h.dtype) for h in halves],
        input_output_aliases={t: t for t in range(n)},
        scratch_shapes=[pltpu.SemaphoreType.DMA((n,)), pltpu.SemaphoreType.DMA((n,))],
        name="share_reduced_halves")(*halves)


def _pack(arrs):
    flat = []
    for a in arrs:
        v = a.reshape(-1).astype(F32)
        pad = (-v.shape[0]) % (8 * LANES)
        flat.append(jnp.pad(v, (0, pad)))
    return jnp.concatenate(flat).reshape(-1, LANES)


def _unpack(buf, shapes):
    out, off = [], 0
    flat = buf.reshape(-1)
    for s in shapes:
        n = int(np.prod(s))
        out.append(flat[off:off + n].reshape(s))
        off += n + ((-n) % (8 * LANES))
    return out


def _xattn_layer_fwd(tag, h, mem, gx, gmem, w):
    hx = rms_fwd(f"rms_x{tag}", h, gx)
    memn = rms_fwd(f"rms_mem{tag}", mem, gmem)
    q = mm_nn(f"xq{tag}", hx, w["q"], BF16)
    k = mm_nn(f"xk{tag}", memn, w["k"], BF16)
    v = mm_nn(f"xv{tag}", memn, w["v"], BF16)
    o, lse = xattn_fwd(f"xattn_fwd{tag}", q, k, v)
    h_out = mm_nn(f"xo{tag}", o, w["o"], F32, res=h)
    return h_out, dict(hx=hx, memn=memn, q=q, k=k, v=v, o=o, lse=lse)


def _xattn_layer_bwd(tag, dh_out, dh_out_b, h_in, mem, gx, gmem, w, sv):
    do = mm_nt(f"d_xo{tag}", dh_out_b, w["o"], BF16)
    dwo = mm_tn(f"dw_xo{tag}", sv["o"], dh_out_b)
    dq, dk, dv = xattn_bwd(f"xattn_bwd{tag}", sv["q"], sv["k"], sv["v"], sv["o"], do, sv["lse"])
    dwq = mm_tn(f"dw_xq{tag}", sv["hx"], dq)
    dhx = mm_nt(f"d_xq{tag}", dq, w["q"], F32)
    dwk = mm_tn(f"dw_xk{tag}", sv["memn"], dk)
    dwv = mm_tn(f"dw_xv{tag}", sv["memn"], dv)
    dmk = mm_nt(f"d_xk{tag}", dk, w["k"], F32)
    dmv = mm_nt(f"d_xv{tag}", dv, w["v"], F32)
    dh_in, dh_in_b, dgx = rms_bwd(f"rms_x_bwd{tag}", h_in, gx, [dhx], dh_out)
    _, _, dgmem = rms_bwd(f"rms_mem_bwd{tag}", mem, gmem, [dmk, dmv], None)
    return dh_in, dh_in_b, dgx, dgmem, dict(q=dwq, k=dwk, v=dwv, o=dwo)


def kernel(x, mem, norm_mix_g, norm_x_g, norm_mem_g, final_norm_g, w_in_ab, rel_bias, conv_w, conv_b, conv_ln_g, conv_ln_b, w_out_ab, w_in_c, sgu_ln_g, sgu_ln_b, w_s, b_s, w_out_c, w_xq, w_xk, w_xv, w_xo, loss_target, m_norm_mix_g, m_norm_x_g, m_norm_mem_g, m_final_norm_g, m_w_in_ab, m_rel_bias, m_conv_w, m_conv_b, m_conv_ln_g, m_conv_ln_b, m_w_out_ab, m_w_in_c, m_sgu_ln_g, m_sgu_ln_b, m_w_s, m_b_s, m_w_out_c, m_w_xq, m_w_xk, m_w_xv, m_w_xo, v_norm_mix_g, v_norm_x_g, v_norm_mem_g, v_final_norm_g, v_w_in_ab, v_rel_bias, v_conv_w, v_conv_b, v_conv_ln_g, v_conv_ln_b, v_w_out_ab, v_w_in_c, v_sgu_ln_g, v_sgu_ln_b, v_w_s, v_b_s, v_w_out_c, v_w_xq, v_w_xk, v_w_xv, v_w_xo):
    S, D = x.shape[1], x.shape[2]
    MIX = 2 * D
    xs, mems, tgt = x[0], mem[0], loss_target[0]
    cx, cy, cc = _place()
    chip = 2 * cx + cy
    cidx = jnp.reshape(cc, (1,)).astype(jnp.int32)
    place = jnp.stack([chip, cc]).astype(jnp.int32)

    ro, rq = MIX // 4, D // 4
    row_sharded = [("out_ab", w_out_ab[0]), ("out_c", w_out_c[0])]
    offs = [dict(), dict()]
    off = 2 * ro
    for layer in range(2):
        for nm_, w in (("q", w_xq), ("k", w_xk), ("v", w_xv), ("o", w_xo)):
            row_sharded.append((f"x{nm_}{layer}", w[layer]))
            offs[layer][nm_] = off
            off += rq
    slots = [cast_into_slot("cast_in_ab", w_in_ab[0], place), cast_into_slot("cast_in_c", w_in_c[0], place)]
    slots += [cast_into_slot("cast_" + nm_, w, place) for nm_, w in row_sharded]
    gathered_w = gather_weights(slots)
    wab4, wc4 = gathered_w[0], gathered_w[1]
    wrow = {nm_: g.reshape(-1, g.shape[2]) for (nm_, _), g in zip(row_sharded, gathered_w[2:])}
    wx = [{k: wrow[f"x{k}{layer}"] for k in "qkvo"} for layer in range(2)]

    small_sh = [conv_w[0], sgu_ln_g[0], sgu_ln_b[0]]
    gathered = exchange_small("gather_small", _pack(small_sh), reduce=False)
    per_chip = [_unpack(gathered[2 * j], [a.shape for a in small_sh]) for j in range(N_CHIPS)]
    conv_w_full = jnp.concatenate([p[0] for p in per_chip], axis=1)
    sgu_g_full = jnp.concatenate([p[1] for p in per_chip], axis=0).reshape(1, MIX)
    sgu_b_full = jnp.concatenate([p[2] for p in per_chip], axis=0).reshape(1, MIX)
    cw_pad = jnp.pad(conv_w_full, ((0, CONV_HALO - CONV_WIDTH), (0, 0)))
    cb = conv_b.reshape(1, D)
    clg, clb = conv_ln_g.reshape(1, D), conv_ln_b.reshape(1, D)
    ws = w_s[0]
    bst = jnp.transpose(b_s[0])
    tq = _attn_tq(S)
    bm = band_bias_table(rel_bias[0], tq)

    hn0 = rms_fwd("rms_mix0", xs, norm_mix_g[0])
    proj0 = mm_nn_cols("proj_ab", hn0, wab4, BF16)
    ya, lse_a = attn_fwd(proj0, bm, D)
    y0, cpre = conv_gate_fwd(proj0, ya, cw_pad, cb, clg, clb, D)
    h1 = mm_nn("out_ab", y0, wrow["out_ab"], F32, res=xs)
    h2, sx0 = _xattn_layer_fwd("0", h1, mems, norm_x_g[0], norm_mem_g[0], wx[0])
    hn1 = rms_fwd("rms_mix1", h2, norm_mix_g[1])
    proj1 = mm_nn_cols("proj_c", hn1, wc4, BF16)
    y1 = sgu_fwd(proj1, sgu_g_full, sgu_b_full, ws, bst, MIX)
    h3 = mm_nn("out_c", y1, wrow["out_c"], F32, res=h2)
    h4, sx1 = _xattn_layer_fwd("1", h3, mems, norm_x_g[1], norm_mem_g[1], wx[1])
    loss_row, dg_final, dh4, dh4b = loss_head("loss_head", h4, final_norm_g, tgt)

    dh3, dh3b, dgx1, dgmem1, dwx1 = _xattn_layer_bwd("1", dh4, dh4b, h3, mems, norm_x_g[1], norm_mem_g[1], wx[1], sx1)
    dy1 = mm_nt("d_out_c", dh3b, wrow["out_c"], BF16)
    dw_out_c = mm_tn("dw_out_c", y1, dh3b)
    dproj1, dws, dbst, dsgu_g, dsgu_b = sgu_bwd(dy1, proj1, sgu_g_full, sgu_b_full, ws, bst, MIX)
    dw_in_c = mm_tn_cols("dw_in_c", hn1, dproj1)
    dhn1 = mm_nt_cols("d_proj_c", dproj1, wc4, F32)
    dh2, dh2b, dgmix1 = rms_bwd("rms_mix1_bwd", h2, norm_mix_g[1], [dhn1], dh3)
    dh1, dh1b, dgx0, dgmem0, dwx0 = _xattn_layer_bwd("0", dh2, dh2b, h1, mems, norm_x_g[0], norm_mem_g[0], wx[0], sx0)
    dy0 = mm_nt("d_out_ab", dh1b, wrow["out_ab"], BF16)
    dw_out_ab = mm_tn("dw_out_ab", y0, dh1b)
    dya, dgate, dc, dclg, dclb = conv_gate_bwd_a(dy0, proj0, ya, cpre, clg, clb, D)
    da, db, dcw, dcb = conv_gate_bwd_b(dc, proj0, cw_pad, D)
    dq, dkc, dkp, dvc, dvp, ds_sum = attn_bwd(proj0, ya, dya, lse_a, bm, D)
    drel = rel_bias_grad(ds_sum)
    dproj0 = assemble_dproj0(dq, dkc, dkp, dvc, dvp, da, db, dgate, D)
    dw_in_ab = mm_tn_cols("dw_in_ab", hn0, dproj0)
    dhn0 = mm_nt_cols("d_proj_ab", dproj0, wab4, F32)
    dx, _, dgmix0 = rms_bwd("rms_mix0_bwd", xs, norm_mix_g[0], [dhn0], dh1)

    dw_rows = [dw_out_ab, dw_out_c] + [dwx[nm_] for dwx in (dwx0, dwx1) for nm_ in "qkvo"]
    dwr = jnp.concatenate([g.reshape(N_CHIPS, -1, g.shape[1]) for g in dw_rows], axis=1)
    big = [dw_in_ab, dw_in_c, dwr]
    recv1 = send_sibling_halves(big)
    chip_parts = [add_halves(f"add_halves{t}", g, r, cidx) for t, (g, r) in enumerate(zip(big, recv1))]
    recv2 = scatter_chip_partials(chip_parts)
    halves = [sum_chips(f"sum_chips{t}", p, r, place) for t, (p, r) in enumerate(zip(chip_parts, recv2))]
    g_ab, g_c, g_r = share_reduced_halves(halves)

    small_full = [
        jnp.concatenate([dgmix0, dgmix1], axis=0), jnp.concatenate([dgx0, dgx1], axis=0),
        jnp.concatenate([dgmem0, dgmem1], axis=0), dg_final.reshape(D), drel[None],
        dcb, dclg, dclb, dws[None], jnp.transpose(dbst)[None],
        dcw[:CONV_WIDTH][None], dsgu_g, dsgu_b]
    summed = _unpack(exchange_small("reduce_small", _pack(small_full), reduce=True), [a.shape for a in small_full])
    (g_norm_mix, g_norm_x, g_norm_mem, g_final, g_rel, g_conv_b, g_clg, g_clb, g_ws, g_bs,
     g_conv_w_full, g_sgu_g_full, g_sgu_b_full) = summed
    cws = conv_w.shape[2]
    g_conv_w = lax.dynamic_slice_in_dim(g_conv_w_full, chip * cws, cws, axis=2)
    sgs = sgu_ln_g.shape[1]
    g_sgu_g = lax.dynamic_slice_in_dim(g_sgu_g_full, chip * sgs, sgs, axis=1)
    g_sgu_b = lax.dynamic_slice_in_dim(g_sgu_b_full, chip * sgs, sgs, axis=1)

    loss = lax.psum(loss_row[0, 0], ("x", "y", "c"))

    g_rows = {}
    g_rows["w_out_ab"] = g_r[0:ro][None]
    g_rows["w_out_c"] = g_r[ro:2 * ro][None]
    for nm_ in "qkvo":
        g_rows["w_x" + nm_] = jnp.stack([g_r[offs[0][nm_]:offs[0][nm_] + rq], g_r[offs[1][nm_]:offs[1][nm_] + rq]])
    grads = dict(
        norm_mix_g=g_norm_mix, norm_x_g=g_norm_x, norm_mem_g=g_norm_mem, final_norm_g=g_final,
        w_in_ab=g_ab[None], rel_bias=g_rel, conv_w=g_conv_w, conv_b=g_conv_b, conv_ln_g=g_clg, conv_ln_b=g_clb,
        w_out_ab=g_rows["w_out_ab"], w_in_c=g_c[None], sgu_ln_g=g_sgu_g, sgu_ln_b=g_sgu_b, w_s=g_ws, b_s=g_bs,
        w_out_c=g_rows["w_out_c"], w_xq=g_rows["w_xq"], w_xk=g_rows["w_xk"], w_xv=g_rows["w_xv"],
        w_xo=g_rows["w_xo"])
    weights = dict(
        norm_mix_g=(norm_mix_g, m_norm_mix_g, v_norm_mix_g), norm_x_g=(norm_x_g, m_norm_x_g, v_norm_x_g),
        norm_mem_g=(norm_mem_g, m_norm_mem_g, v_norm_mem_g), final_norm_g=(final_norm_g, m_final_norm_g, v_final_norm_g),
        w_in_ab=(w_in_ab, m_w_in_ab, v_w_in_ab), rel_bias=(rel_bias, m_rel_bias, v_rel_bias),
        conv_w=(conv_w, m_conv_w, v_conv_w), conv_b=(conv_b, m_conv_b, v_conv_b),
        conv_ln_g=(conv_ln_g, m_conv_ln_g, v_conv_ln_g), conv_ln_b=(conv_ln_b, m_conv_ln_b, v_conv_ln_b),
        w_out_ab=(w_out_ab, m_w_out_ab, v_w_out_ab), w_in_c=(w_in_c, m_w_in_c, v_w_in_c),
        sgu_ln_g=(sgu_ln_g, m_sgu_ln_g, v_sgu_ln_g), sgu_ln_b=(sgu_ln_b, m_sgu_ln_b, v_sgu_ln_b),
        w_s=(w_s, m_w_s, v_w_s), b_s=(b_s, m_b_s, v_b_s), w_out_c=(w_out_c, m_w_out_c, v_w_out_c),
        w_xq=(w_xq, m_w_xq, v_w_xq), w_xk=(w_xk, m_w_xk, v_w_xk), w_xv=(w_xv, m_w_xv, v_w_xv),
        w_xo=(w_xo, m_w_xo, v_w_xo))
    names = list(weights)
    big_names = ("w_in_ab", "w_out_ab", "w_in_c", "w_out_c", "w_xq", "w_xk", "w_xv", "w_xo")
    delta, new_m, new_v = {}, {}, {}
    for nm_ in big_names:
        w, m, v = weights[nm_]
        C = w.shape[-1]
        d2, m2, v2 = adamw("adamw_" + nm_, w.reshape(-1, C), grads[nm_].reshape(-1, C), m.reshape(-1, C),
                           v.reshape(-1, C))
        delta[nm_], new_m[nm_], new_v[nm_] = d2.reshape(w.shape), m2.reshape(w.shape), v2.reshape(w.shape)
    small_names = [n for n in names if n not in big_names]
    shapes = [weights[n][0].shape for n in small_names]
    d_s, m_s, v_s = adamw("adamw_small", _pack([weights[n][0] for n in small_names]),
                          _pack([grads[n] for n in small_names]), _pack([weights[n][1] for n in small_names]),
                          _pack([weights[n][2] for n in small_names]))
    for n, d_, m_, v_ in zip(small_names, _unpack(d_s, shapes), _unpack(m_s, shapes), _unpack(v_s, shapes)):
        delta[n], new_m[n], new_v[n] = d_, m_, v_

    return (loss, dx[None], *[grads[n].reshape(weights[n][0].shape) for n in names], *[delta[n] for n in names],
            *[new_m[n] for n in names], *[new_v[n] for n in names])
```

```python
import functools

import numpy as np
import jax
import jax.numpy as jnp
from jax import lax
from jax.experimental import pallas as pl
from jax.experimental.pallas import tpu as pltpu

F32 = jnp.float32
BF16 = jnp.bfloat16
MESH = pl.DeviceIdType.MESH

EPS = 1e-6
CHUNK = 64
N_PAST_CHUNKS = 8
MAX_REL = 128
HEAD_DIM_A = 128
CONV_WIDTH = 31
CONV_HALO = 32
GMLP_CHUNK = 128
N_GROUPS_C = 8
N_HEADS_X = 4
NEG = -1e30

ADAM_LR = 0.001
ADAM_B1 = 0.9
ADAM_B2 = 0.999
ADAM_EPS = 1e-08
ADAM_WD = 0.01
ADAM_STEP = 10

N_CHIPS = 4
N_DEV = 8
V7X_VMEM_LIMIT = 56 * 1024 * 1024
LANES = 128
MXU = 256


def _pick(n, cands):
    for c in cands:
        if c <= n and n % c == 0:
            return c
    return n


def _cparams(*sem):
    return pltpu.CompilerParams(dimension_semantics=sem, vmem_limit_bytes=V7X_VMEM_LIMIT)


def _sigmoid(x):
    return 1.0 / (1.0 + jnp.exp(-x))


def _dot(a, b, contract):
    return lax.dot_general(a, b, (contract, ((), ())), preferred_element_type=F32)


NN = ((1,), (0,))
NT = ((1,), (1,))
TN = ((0,), (0,))


class _Comm:
    def __init__(self, arrays, out_shapes, aliases, sems, start, finish):
        self.arrays, self.out_shapes, self.aliases, self.sems = list(arrays), list(out_shapes), dict(aliases), list(sems)
        self.start, self.finish = start, finish


def _join(*jobs):
    arrays, outs, sems, aliases, spans = [], [], [], {}, []
    for j in jobs:
        spans.append((len(arrays), len(outs), len(sems)))
        aliases.update({len(arrays) + i: len(outs) + o for i, o in j.aliases.items()})
        arrays += j.arrays
        outs += j.out_shapes
        sems += j.sems

    def part(j, span, ins, os_, ss):
        a0, o0, s0 = span
        return (ins[a0:a0 + len(j.arrays)], os_[o0:o0 + len(j.out_shapes)], ss[s0:s0 + len(j.sems)])

    def start(ins, os_, ss):
        for j, span in zip(jobs, spans):
            j.start(*part(j, span, ins, os_, ss))

    def finish(ins, os_, ss):
        for j, span in zip(jobs, spans):
            j.finish(*part(j, span, ins, os_, ss))

    return _Comm(arrays, outs, aliases, sems, start, finish)


def _call(body, *, name, grid, in_specs, out_specs, out_shape, args, scratch_shapes=(), sem=None, comm=None):
    multi = isinstance(out_shape, (list, tuple))
    o_shapes = list(out_shape) if multi else [out_shape]
    o_specs = list(out_specs) if multi else [out_specs]
    if comm is None:
        return pl.pallas_call(body, grid=grid, in_specs=in_specs, out_specs=out_specs, out_shape=out_shape,
                              scratch_shapes=list(scratch_shapes), name=name,
                              compiler_params=_cparams(*sem))(*args)
    n_in, n_out, n_scr = len(in_specs), len(o_shapes), len(scratch_shapes)
    n_ci, n_co = len(comm.arrays), len(comm.out_shapes)

    def carrier(*refs):
        ins, rest = refs[:n_in], refs[n_in:]
        cins, rest = rest[:n_ci], rest[n_ci:]
        outs, rest = rest[:n_out], rest[n_out:]
        couts, rest = rest[:n_co], rest[n_co:]
        scr, csems = rest[:n_scr], rest[n_scr:]
        ids = [pl.program_id(a) for a in range(len(grid))]
        first = functools.reduce(jnp.logical_and, [i == 0 for i in ids])
        last = functools.reduce(jnp.logical_and, [i == g - 1 for i, g in zip(ids, grid)])

        @pl.when(first)
        def _():
            comm.start(cins, couts, csems)

        body(*ins, *outs, *scr)

        @pl.when(last)
        def _():
            comm.finish(cins, couts, csems)

    res = pl.pallas_call(
        carrier, grid=grid, in_specs=list(in_specs) + [HBM_SPEC] * n_ci,
        out_specs=o_specs + [HBM_SPEC] * n_co, out_shape=o_shapes + comm.out_shapes,
        input_output_aliases={n_in + i: n_out + o for i, o in comm.aliases.items()},
        scratch_shapes=list(scratch_shapes) + comm.sems, name=name,
        compiler_params=_cparams(*(["arbitrary"] * len(grid))))(*args, *comm.arrays)
    mine = list(res[:n_out]) if multi else res[0]
    return mine, list(res[n_out:])


def run_comm(name, comm):
    def body(*refs):
        n_ci, n_co = len(comm.arrays), len(comm.out_shapes)
        cins, couts, csems = refs[:n_ci], refs[n_ci:n_ci + n_co], refs[n_ci + n_co:]
        comm.start(cins, couts, csems)
        comm.finish(cins, couts, csems)

    return pl.pallas_call(
        body, in_specs=[HBM_SPEC] * len(comm.arrays), out_specs=[HBM_SPEC] * len(comm.out_shapes),
        out_shape=comm.out_shapes, input_output_aliases=comm.aliases, scratch_shapes=comm.sems,
        name=name)(*comm.arrays)


def _mm(name, a, b, *, contract, grid, a_spec, b_spec, o_spec, out_shape, res=None, comm=None):
    nk = grid[2]

    def body(*refs):
        if res is not None:
            a_ref, b_ref, r_ref, o_ref = refs[:4]
        else:
            a_ref, b_ref, o_ref = refs[:3]
            r_ref = None
        p = _dot(a_ref[...].astype(BF16), b_ref[...].astype(BF16), contract)

        def finish(acc):
            if r_ref is not None:
                acc = acc + r_ref[...]
            o_ref[...] = acc.astype(o_ref.dtype)

        if nk == 1:
            finish(p)
        else:
            acc_ref = refs[-1]
            k = pl.program_id(2)

            @pl.when(k == 0)
            def _():
                acc_ref[...] = p

            @pl.when(k > 0)
            def _():
                acc_ref[...] += p

            @pl.when(k == nk - 1)
            def _():
                finish(acc_ref[...])

    in_specs = [a_spec, b_spec]
    args = [a, b]
    if res is not None:
        in_specs.append(o_spec)
        args.append(res)
    blk = tuple(d for d in o_spec.block_shape if d is not None)
    scratch = [] if nk == 1 else [pltpu.VMEM(blk, F32)]
    return _call(body, name=name, grid=grid, in_specs=in_specs, out_specs=o_spec, out_shape=out_shape,
                 args=args, scratch_shapes=scratch, sem=("parallel", "parallel", "arbitrary"), comm=comm)


def mm_nn_cols(name, a, w4, out_dtype, comm=None):
    M, K = a.shape
    _, _, C = w4.shape
    tm = _pick(M, (1024, 512, 256))
    tn = _pick(C, (512, 256, 128))
    nps = C // tn
    return _mm(name, a, w4, contract=NN, grid=(M // tm, 4 * nps, 1),
               a_spec=pl.BlockSpec((tm, K), lambda i, j, k: (i, 0)),
               b_spec=pl.BlockSpec((None, K, tn), lambda i, j, k: (j // nps, 0, j % nps)),
               o_spec=pl.BlockSpec((tm, tn), lambda i, j, k: (i, j)),
               out_shape=jax.ShapeDtypeStruct((M, 4 * C), out_dtype), comm=comm)


def mm_nn(name, a, w, out_dtype, res=None):
    M, K = a.shape
    N = w.shape[1]
    tm = _pick(M, (1024, 512, 256))
    tn = _pick(N, (512, 256, 128))
    return _mm(name, a, w, contract=NN, grid=(M // tm, N // tn, 1),
               a_spec=pl.BlockSpec((tm, K), lambda i, j, k: (i, 0)),
               b_spec=pl.BlockSpec((K, tn), lambda i, j, k: (0, j)),
               o_spec=pl.BlockSpec((tm, tn), lambda i, j, k: (i, j)),
               out_shape=jax.ShapeDtypeStruct((M, N), out_dtype), res=res)


def mm_nt_cols(name, a, w4, out_dtype, comm=None):
    M = a.shape[0]
    _, K, C = w4.shape
    tm = _pick(M, (1024, 512, 256))
    tn = _pick(K, (1024, 512, 256, 128))
    tk = _pick(C, (1792, 1536, 1024, 512, 256, 128))
    kps = C // tk
    return _mm(name, a, w4, contract=NT, grid=(M // tm, K // tn, 4 * kps),
               a_spec=pl.BlockSpec((tm, tk), lambda i, j, k: (i, k)),
               b_spec=pl.BlockSpec((None, tn, tk), lambda i, j, k: (k // kps, j, k % kps)),
               o_spec=pl.BlockSpec((tm, tn), lambda i, j, k: (i, j)),
               out_shape=jax.ShapeDtypeStruct((M, K), out_dtype), comm=comm)


def mm_nt(name, a, w, out_dtype):
    M, C = a.shape
    N = w.shape[0]
    tm = _pick(M, (1024, 512, 256))
    tn = _pick(N, (512, 256, 128))
    return _mm(name, a, w, contract=NT, grid=(M // tm, N // tn, 1),
               a_spec=pl.BlockSpec((tm, C), lambda i, j, k: (i, 0)),
               b_spec=pl.BlockSpec((tn, C), lambda i, j, k: (j, 0)),
               o_spec=pl.BlockSpec((tm, tn), lambda i, j, k: (i, j)),
               out_shape=jax.ShapeDtypeStruct((M, N), out_dtype))


def mm_tn_cols(name, a, b, comm=None):
    S, K = a.shape
    C = b.shape[1] // 4
    ts = _pick(S, (1024, 512, 256))
    tko = _pick(K, (2048, 1024, 512, 256, 128))
    tn = _pick(C, (1024, 512, 256, 128))
    nps = C // tn
    return _mm(name, a, b, contract=TN, grid=(K // tko, 4 * nps, S // ts),
               a_spec=pl.BlockSpec((ts, tko), lambda i, j, k: (k, i)),
               b_spec=pl.BlockSpec((ts, tn), lambda i, j, k: (k, j)),
               o_spec=pl.BlockSpec((None, tko, tn), lambda i, j, k: (j // nps, i, j % nps)),
               out_shape=jax.ShapeDtypeStruct((4, K, C), BF16), comm=comm)


def mm_tn(name, a, b):
    S, K = a.shape
    N = b.shape[1]
    ts = _pick(S, (1024, 512, 256))
    tko = _pick(K, (2048, 1024, 512, 256, 128))
    tn = _pick(N, (1024, 512, 256, 128))
    return _mm(name, a, b, contract=TN, grid=(K // tko, N // tn, S // ts),
               a_spec=pl.BlockSpec((ts, tko), lambda i, j, k: (k, i)),
               b_spec=pl.BlockSpec((ts, tn), lambda i, j, k: (k, j)),
               o_spec=pl.BlockSpec((tko, tn), lambda i, j, k: (i, j)),
               out_shape=jax.ShapeDtypeStruct((K, N), BF16))


def rms_fwd(name, x, g):
    S, D = x.shape
    T = _pick(S, (512, 256))

    def body(x_ref, g_ref, o_ref):
        xf = x_ref[...]
        r = lax.rsqrt(jnp.mean(xf * xf, axis=-1, keepdims=True) + EPS)
        o_ref[...] = (xf * r * g_ref[...]).astype(o_ref.dtype)

    return pl.pallas_call(
        body, grid=(S // T,),
        in_specs=[pl.BlockSpec((T, D), lambda i: (i, 0)), pl.BlockSpec((1, D), lambda i: (0, 0))],
        out_specs=pl.BlockSpec((T, D), lambda i: (i, 0)),
        out_shape=jax.ShapeDtypeStruct((S, D), BF16), name=name,
        compiler_params=_cparams("parallel"))(x, g.reshape(1, D))


def rms_bwd(name, x, g, dys, dres):
    S, D = x.shape
    T = _pick(S, (256,))
    ndy = len(dys)
    has_res = dres is not None

    def body(*refs):
        x_ref, g_ref = refs[0], refs[1]
        dy_refs = refs[2:2 + ndy]
        r_ref = refs[2 + ndy] if has_res else None
        dx_ref, dxb_ref, dg_ref = refs[-3], refs[-2], refs[-1]
        i = pl.program_id(0)
        xf = x_ref[...]
        r = lax.rsqrt(jnp.mean(xf * xf, axis=-1, keepdims=True) + EPS)
        xhat = xf * r
        dy = dy_refs[0][...].astype(F32)
        for d in dy_refs[1:]:
            dy = dy + d[...].astype(F32)
        dxhat = dy * g_ref[...]
        dx = r * (dxhat - xhat * jnp.mean(dxhat * xhat, axis=-1, keepdims=True))
        if has_res:
            dx = dx + r_ref[...]
        dx_ref[...] = dx
        dxb_ref[...] = dx.astype(dxb_ref.dtype)
        dg = jnp.sum(dy * xhat, axis=0, keepdims=True)

        @pl.when(i == 0)
        def _():
            dg_ref[...] = dg

        @pl.when(i > 0)
        def _():
            dg_ref[...] += dg

    row = pl.BlockSpec((T, D), lambda i: (i, 0))
    vec = pl.BlockSpec((1, D), lambda i: (0, 0))
    args = [x, g.reshape(1, D), *dys] + ([dres] if has_res else [])
    return pl.pallas_call(
        body, grid=(S // T,),
        in_specs=[row, vec] + [row] * (ndy + int(has_res)),
        out_specs=[row, row, vec],
        out_shape=[jax.ShapeDtypeStruct((S, D), F32), jax.ShapeDtypeStruct((S, D), BF16),
                   jax.ShapeDtypeStruct((1, D), F32)],
        name=name, compiler_params=_cparams("arbitrary"))(*args)


def loss_head(name, h, g, target):
    S, D = h.shape
    T = _pick(S, (256,))

    def body(h_ref, g_ref, t_ref, loss_ref, dg_ref, dh_ref, dhb_ref):
        i = pl.program_id(0)
        xf = h_ref[...]
        gv = g_ref[...]
        r = lax.rsqrt(jnp.mean(xf * xf, axis=-1, keepdims=True) + EPS)
        xhat = xf * r
        err = xhat * gv - t_ref[...]
        part = 0.5 * jnp.sum(jnp.sum(err * err, axis=-1, keepdims=True), axis=0, keepdims=True) / D
        dout = err / D
        dxhat = dout * gv
        dh = r * (dxhat - xhat * jnp.mean(dxhat * xhat, axis=-1, keepdims=True))
        dh_ref[...] = dh
        dhb_ref[...] = dh.astype(dhb_ref.dtype)
        dg = jnp.sum(dout * xhat, axis=0, keepdims=True)
        lrow = jnp.broadcast_to(part, (1, LANES))

        @pl.when(i == 0)
        def _():
            dg_ref[...] = dg
            loss_ref[...] = lrow

        @pl.when(i > 0)
        def _():
            dg_ref[...] += dg
            loss_ref[...] += lrow

    row = pl.BlockSpec((T, D), lambda i: (i, 0))
    vec = pl.BlockSpec((1, D), lambda i: (0, 0))
    return pl.pallas_call(
        body, grid=(S // T,), in_specs=[row, vec, row],
        out_specs=[pl.BlockSpec((1, LANES), lambda i: (0, 0)), vec, row, row],
        out_shape=[jax.ShapeDtypeStruct((1, LANES), F32), jax.ShapeDtypeStruct((1, D), F32),
                   jax.ShapeDtypeStruct((S, D), F32), jax.ShapeDtypeStruct((S, D), BF16)],
        name=name, compiler_params=_cparams("arbitrary"))(h, g.reshape(1, D), target)


def _attn_tq(S):
    return _pick(S, (512,))


def band_bias_table(rel_bias, tq):
    H = rel_bias.shape[0]
    w = 2 * tq
    nbits = int(np.log2(tq))
    assert (1 << nbits) == tq and (N_PAST_CHUNKS + 2) * CHUNK - 1 <= w
    c = np.arange(w)
    d0 = np.where(c <= tq + CHUNK - 1, tq - c, tq + w - c)
    base = jnp.take(rel_bias.astype(F32), jnp.asarray(np.clip(d0, -MAX_REL, MAX_REL) + MAX_REL), axis=1)

    def body(b_ref, o_ref):
        x = jnp.broadcast_to(b_ref[...], (tq, w))
        row = lax.broadcasted_iota(jnp.int32, (tq, w), 0)
        col = lax.broadcasted_iota(jnp.int32, (tq, w), 1)
        for b in range(nbits):
            x = jnp.where(((row >> b) & 1) == 1, pltpu.roll(x, 1 << b, 1), x)
        qc = row // CHUNK
        kc = col // CHUNK - tq // CHUNK
        o_ref[...] = jnp.where((kc <= qc) & (kc >= qc - N_PAST_CHUNKS), x, NEG)

    return pl.pallas_call(
        body, grid=(H,), in_specs=[pl.BlockSpec((None, 1, w), lambda h: (h, 0, 0))],
        out_specs=pl.BlockSpec((None, tq, w), lambda h: (h, 0, 0)),
        out_shape=jax.ShapeDtypeStruct((H, tq, w), F32), name="band_bias_table",
        compiler_params=_cparams("parallel"))(base.reshape(H, 1, w))


def attn_fwd(proj, bm, D, comm=None):
    S = proj.shape[0]
    H = D // HEAD_DIM_A
    tq = _attn_tq(S)
    nb = S // tq
    scale = HEAD_DIM_A ** -0.5

    def body(q_ref, kp_ref, kc_ref, vp_ref, vc_ref, bm_ref, o_ref, lse_ref):
        i = pl.program_id(1)
        q = q_ref[...]
        sp = _dot(q, kp_ref[...], NT) * scale + bm_ref[:, :tq]
        sp = jnp.where(i == 0, NEG, sp)
        sc = _dot(q, kc_ref[...], NT) * scale + bm_ref[:, tq:]
        m = jnp.maximum(jnp.max(sp, axis=-1, keepdims=True), jnp.max(sc, axis=-1, keepdims=True))
        pp = jnp.exp(sp - m)
        pc = jnp.exp(sc - m)
        l = jnp.sum(pp, axis=-1, keepdims=True) + jnp.sum(pc, axis=-1, keepdims=True)
        o = _dot(pp.astype(BF16), vp_ref[...], NN) + _dot(pc.astype(BF16), vc_ref[...], NN)
        o_ref[...] = (o / l).astype(o_ref.dtype)
        lse_ref[...] = m + jnp.log(l)

    def col(base):
        return (pl.BlockSpec((tq, HEAD_DIM_A), lambda h, i: (jnp.maximum(i - 1, 0), base + h)),
                pl.BlockSpec((tq, HEAD_DIM_A), lambda h, i: (i, base + h)))

    kp, kc = col(H)
    vp, vc = col(2 * H)
    return _call(
        body, name="attn_fwd", grid=(H, nb),
        in_specs=[pl.BlockSpec((tq, HEAD_DIM_A), lambda h, i: (i, h)), kp, kc, vp, vc,
                  pl.BlockSpec((None, tq, 2 * tq), lambda h, i: (h, 0, 0))],
        out_specs=[pl.BlockSpec((tq, HEAD_DIM_A), lambda h, i: (i, h)),
                   pl.BlockSpec((None, tq, 1), lambda h, i: (h, i, 0))],
        out_shape=[jax.ShapeDtypeStruct((S, D), BF16), jax.ShapeDtypeStruct((H, S, 1), F32)],
        args=[proj, proj, proj, proj, proj, bm], sem=("parallel", "arbitrary"), comm=comm)


def attn_bwd(proj, ya, dya, lse, bm, D, comm=None):
    S = proj.shape[0]
    H = D // HEAD_DIM_A
    tq = _attn_tq(S)
    nb = S // tq
    scale = HEAD_DIM_A ** -0.5

    def body(q_ref, kp_ref, kc_ref, vp_ref, vc_ref, o_ref, do_ref, lse_ref, bm_ref,
             dq_ref, dkc_ref, dkp_ref, dvc_ref, dvp_ref, ds_ref):
        i = pl.program_id(1)
        q = q_ref[...]
        do = do_ref[...]
        delta = jnp.sum(do.astype(F32) * o_ref[...].astype(F32), axis=-1, keepdims=True)
        lse_v = lse_ref[...]

        def half(k_ref, v_ref, bias, first):
            k = k_ref[...]
            s = _dot(q, k, NT) * scale + bias
            if first:
                s = jnp.where(i == 0, NEG, s)
            p = jnp.exp(s - lse_v)
            dv = _dot(p.astype(BF16), do, TN)
            dp = _dot(do, v_ref[...], NT)
            ds = p * (dp - delta)
            dsb = ds.astype(BF16)
            dq = _dot(dsb, k, NN)
            dk = _dot(dsb, q, TN) * scale
            return ds, dq, dk, dv

        dsp, dqp, dkp, dvp = half(kp_ref, vp_ref, bm_ref[:, :tq], True)
        dsc, dqc, dkc, dvc = half(kc_ref, vc_ref, bm_ref[:, tq:], False)
        dq_ref[...] = ((dqp + dqc) * scale).astype(dq_ref.dtype)
        dkp_ref[...] = dkp.astype(dkp_ref.dtype)
        dkc_ref[...] = dkc.astype(dkc_ref.dtype)
        dvp_ref[...] = dvp.astype(dvp_ref.dtype)
        dvc_ref[...] = dvc.astype(dvc_ref.dtype)

        @pl.when(i == 0)
        def _():
            ds_ref[:, :tq] = dsp
            ds_ref[:, tq:] = dsc

        @pl.when(i > 0)
        def _():
            ds_ref[:, :tq] += dsp
            ds_ref[:, tq:] += dsc

    def col(base):
        return (pl.BlockSpec((tq, HEAD_DIM_A), lambda h, i: (jnp.maximum(i - 1, 0), base + h)),
                pl.BlockSpec((tq, HEAD_DIM_A), lambda h, i: (i, base + h)))

    kp, kc = col(H)
    vp, vc = col(2 * H)
    blk = pl.BlockSpec((tq, HEAD_DIM_A), lambda h, i: (i, h))
    sd = jax.ShapeDtypeStruct((S, D), BF16)
    return _call(
        body, name="attn_bwd", grid=(H, nb),
        in_specs=[blk, kp, kc, vp, vc, blk, blk,
                  pl.BlockSpec((None, tq, 1), lambda h, i: (h, i, 0)),
                  pl.BlockSpec((None, tq, 2 * tq), lambda h, i: (h, 0, 0))],
        out_specs=[blk, blk, blk, blk, blk, pl.BlockSpec((None, tq, 2 * tq), lambda h, i: (h, 0, 0))],
        out_shape=[sd, sd, sd, sd, sd, jax.ShapeDtypeStruct((H, tq, 2 * tq), F32)],
        args=[proj, proj, proj, proj, proj, ya, dya, lse, bm], sem=("parallel", "arbitrary"), comm=comm)


def rel_bias_grad(ds_sum):
    H, tq, w = ds_sum.shape
    nbin = 2 * MAX_REL + 1
    nbin_pad = 3 * LANES
    d_lo, d_hi = -(CHUNK - 1), (N_PAST_CHUNKS + 1) * CHUNK - 1
    assert d_hi - d_lo + 1 <= w
    onehot = np.zeros((w, nbin_pad), np.float32)
    for d in range(d_lo, d_hi + 1):
        onehot[(tq - d) % w, int(np.clip(d, -MAX_REL, MAX_REL)) + MAX_REL] = 1.0
    nbits = int(np.log2(tq))
    assert (1 << nbits) == tq

    def body(ds_ref, m_ref, o_ref):
        x = ds_ref[...]
        row = lax.broadcasted_iota(jnp.int32, x.shape, 0)
        for b in range(nbits):
            rolled = pltpu.roll(x, w - (1 << b), 1)
            x = jnp.where(((row >> b) & 1) == 1, rolled, x)
        t = jnp.sum(x, axis=0, keepdims=True)
        o_ref[...] = lax.dot_general(t, m_ref[...], (NN, ((), ())), precision=lax.Precision.HIGHEST,
                                     preferred_element_type=F32)

    out = pl.pallas_call(
        body, grid=(H,),
        in_specs=[pl.BlockSpec((None, tq, w), lambda h: (h, 0, 0)),
                  pl.BlockSpec((w, nbin_pad), lambda h: (0, 0))],
        out_specs=pl.BlockSpec((None, 1, nbin_pad), lambda h: (h, 0, 0)),
        out_shape=jax.ShapeDtypeStruct((H, 1, nbin_pad), F32),
        name="rel_bias_grad", compiler_params=_cparams("parallel"))(ds_sum, jnp.asarray(onehot))
    return out[:, 0, :nbin]


def _conv_t(S):
    return _pick(S, (256,))


def _fill_zbuf(zbuf, ap_ref, bp_ref, a_ref, b_ref, i):
    zp = ap_ref[...].astype(F32) * _sigmoid(bp_ref[...].astype(F32))
    zbuf[0:CONV_HALO, :] = jnp.where(i == 0, 0.0, zp)
    zbuf[CONV_HALO:, :] = a_ref[...].astype(F32) * _sigmoid(b_ref[...].astype(F32))


def conv_gate_fwd(proj, ya, cw, cb, lng, lnb, D, comm=None):
    S = proj.shape[0]
    T = _conv_t(S)
    hb = T // CONV_HALO
    nlb = D // LANES

    def body(ap_ref, bp_ref, a_ref, b_ref, ga_ref, gb_ref, ya_ref, cw_ref, cb_ref, lng_ref, lnb_ref,
             y_ref, c_ref, zbuf):
        i = pl.program_id(0)
        _fill_zbuf(zbuf, ap_ref, bp_ref, a_ref, b_ref, i)

        def lane_block(lb, carry):
            lanes = pl.ds(pl.multiple_of(lb * LANES, LANES), LANES)
            acc = jnp.zeros((T, LANES), F32)
            for k in range(CONV_WIDTH):
                acc = acc + cw_ref[k:k + 1, lanes] * zbuf[pl.ds(CONV_HALO - CONV_WIDTH + 1 + k, T), lanes]
            c_ref[:, lanes] = acc + cb_ref[:, lanes]
            return carry

        lax.fori_loop(0, nlb, lane_block, 0)
        c = c_ref[...]
        mu = jnp.mean(c, axis=-1, keepdims=True)
        xc = c - mu
        rstd = lax.rsqrt(jnp.mean(xc * xc, axis=-1, keepdims=True) + EPS)
        ln = xc * rstd * lng_ref[...] + lnb_ref[...]
        yb = ln * _sigmoid(ln)
        ga = ga_ref[...].astype(F32)
        gb = gb_ref[...].astype(F32)
        y_ref[:, :D] = (ya_ref[...].astype(F32) * (ga * _sigmoid(ga))).astype(y_ref.dtype)
        y_ref[:, D:] = (yb * (gb * _sigmoid(gb))).astype(y_ref.dtype)

    def cur(cidx):
        return pl.BlockSpec((T, D), lambda i: (i, cidx))

    def prev(cidx):
        return pl.BlockSpec((CONV_HALO, D), lambda i: (jnp.maximum(i * hb - 1, 0), cidx))

    vec = pl.BlockSpec((1, D), lambda i: (0, 0))
    return _call(
        body, name="conv_gate_fwd", grid=(S // T,),
        in_specs=[prev(3), prev(4), cur(3), cur(4), cur(5), cur(6), pl.BlockSpec((T, D), lambda i: (i, 0)),
                  pl.BlockSpec((CONV_HALO, D), lambda i: (0, 0)), vec, vec, vec],
        out_specs=[pl.BlockSpec((T, 2 * D), lambda i: (i, 0)), pl.BlockSpec((T, D), lambda i: (i, 0))],
        out_shape=[jax.ShapeDtypeStruct((S, 2 * D), BF16), jax.ShapeDtypeStruct((S, D), F32)],
        scratch_shapes=[pltpu.VMEM((T + CONV_HALO, D), F32)],
        args=[proj, proj, proj, proj, proj, proj, ya, cw, cb, lng, lnb], sem=("parallel",), comm=comm)


def conv_gate_bwd_a(dy0, proj, ya, cpre, lng, lnb, D):
    S = proj.shape[0]
    T = _conv_t(S)

    def body(dy_ref, ga_ref, gb_ref, ya_ref, c_ref, lng_ref, lnb_ref,
             dya_ref, dg_ref, dc_ref, dlng_ref, dlnb_ref):
        i = pl.program_id(0)
        c = c_ref[...]
        gv = lng_ref[...]
        mu = jnp.mean(c, axis=-1, keepdims=True)
        xc = c - mu
        rstd = lax.rsqrt(jnp.mean(xc * xc, axis=-1, keepdims=True) + EPS)
        xhat = xc * rstd
        ln = xhat * gv + lnb_ref[...]
        sl = _sigmoid(ln)
        yb = ln * sl
        ga = ga_ref[...].astype(F32)
        gb = gb_ref[...].astype(F32)
        sa = _sigmoid(ga)
        sb = _sigmoid(gb)
        dy_a = dy_ref[:, :D].astype(F32)
        dy_b = dy_ref[:, D:].astype(F32)
        dya_ref[...] = (dy_a * (ga * sa)).astype(dya_ref.dtype)
        dg_ref[:, :D] = (dy_a * ya_ref[...].astype(F32) * (sa * (1.0 + ga * (1.0 - sa)))).astype(dg_ref.dtype)
        dg_ref[:, D:] = (dy_b * yb * (sb * (1.0 + gb * (1.0 - sb)))).astype(dg_ref.dtype)
        dln = dy_b * (gb * sb) * (sl * (1.0 + ln * (1.0 - sl)))
        dxhat = dln * gv
        dc_ref[...] = rstd * (dxhat - jnp.mean(dxhat, axis=-1, keepdims=True)
                              - xhat * jnp.mean(dxhat * xhat, axis=-1, keepdims=True))
        dlng = jnp.sum(dln * xhat, axis=0, keepdims=True)
        dlnb = jnp.sum(dln, axis=0, keepdims=True)

        @pl.when(i == 0)
        def _():
            dlng_ref[...] = dlng
            dlnb_ref[...] = dlnb

        @pl.when(i > 0)
        def _():
            dlng_ref[...] += dlng
            dlnb_ref[...] += dlnb

    row = pl.BlockSpec((T, D), lambda i: (i, 0))
    vec = pl.BlockSpec((1, D), lambda i: (0, 0))
    return pl.pallas_call(
        body, grid=(S // T,),
        in_specs=[pl.BlockSpec((T, 2 * D), lambda i: (i, 0)),
                  pl.BlockSpec((T, D), lambda i: (i, 5)), pl.BlockSpec((T, D), lambda i: (i, 6)),
                  row, row, vec, vec],
        out_specs=[row, pl.BlockSpec((T, 2 * D), lambda i: (i, 0)), row, vec, vec],
        out_shape=[jax.ShapeDtypeStruct((S, D), BF16), jax.ShapeDtypeStruct((S, 2 * D), BF16),
                   jax.ShapeDtypeStruct((S, D), F32), jax.ShapeDtypeStruct((1, D), F32),
                   jax.ShapeDtypeStruct((1, D), F32)],
        name="conv_gate_bwd_a", compiler_params=_cparams("arbitrary"))(
            dy0, proj, proj, ya, cpre, lng, lnb)


def conv_gate_bwd_b(dc, proj, cw, D, comm=None):
    S = proj.shape[0]
    T = _conv_t(S)
    hb = T // CONV_HALO
    nt = S // T
    nlb = D // LANES

    def body(dc_ref, dn_ref, ap_ref, bp_ref, a_ref, b_ref, cw_ref, da_ref, db_ref, dcw_ref, dcb_ref,
             zbuf, dcbuf):
        i = pl.program_id(0)
        _fill_zbuf(zbuf, ap_ref, bp_ref, a_ref, b_ref, i)
        dcv = dc_ref[...]
        dcbuf[0:T, :] = dcv
        dcbuf[T:, :] = jnp.where(i == nt - 1, 0.0, dn_ref[...])

        @pl.when(i == 0)
        def _():
            dcw_ref[...] = jnp.zeros_like(dcw_ref)
            dcb_ref[...] = jnp.zeros_like(dcb_ref)

        dcb_ref[...] += jnp.sum(dcv, axis=0, keepdims=True)

        def lane_block(lb, carry):
            lanes = pl.ds(pl.multiple_of(lb * LANES, LANES), LANES)
            d0 = dcbuf[0:T, lanes]
            dz = jnp.zeros((T, LANES), F32)
            for k in range(CONV_WIDTH):
                dz = dz + cw_ref[k:k + 1, lanes] * dcbuf[pl.ds(CONV_WIDTH - 1 - k, T), lanes]
                zs = zbuf[pl.ds(CONV_HALO - CONV_WIDTH + 1 + k, T), lanes]
                dcw_ref[k:k + 1, lanes] += jnp.sum(d0 * zs, axis=0, keepdims=True)
            av = a_ref[:, lanes].astype(F32)
            sg = _sigmoid(b_ref[:, lanes].astype(F32))
            da_ref[:, lanes] = (dz * sg).astype(da_ref.dtype)
            db_ref[:, lanes] = (dz * av * sg * (1.0 - sg)).astype(db_ref.dtype)
            return carry

        lax.fori_loop(0, nlb, lane_block, 0)

    def cur(cidx):
        return pl.BlockSpec((T, D), lambda i: (i, cidx))

    def prev(cidx):
        return pl.BlockSpec((CONV_HALO, D), lambda i: (jnp.maximum(i * hb - 1, 0), cidx))

    row = pl.BlockSpec((T, D), lambda i: (i, 0))
    nxt = pl.BlockSpec((CONV_HALO, D), lambda i: (jnp.minimum((i + 1) * hb, nt * hb - 1), 0))
    return _call(
        body, name="conv_gate_bwd_b", grid=(nt,),
        in_specs=[row, nxt, prev(3), prev(4), cur(3), cur(4), pl.BlockSpec((CONV_HALO, D), lambda i: (0, 0))],
        out_specs=[row, row, pl.BlockSpec((CONV_HALO, D), lambda i: (0, 0)),
                   pl.BlockSpec((1, D), lambda i: (0, 0))],
        out_shape=[jax.ShapeDtypeStruct((S, D), BF16), jax.ShapeDtypeStruct((S, D), BF16),
                   jax.ShapeDtypeStruct((CONV_HALO, D), F32), jax.ShapeDtypeStruct((1, D), F32)],
        scratch_shapes=[pltpu.VMEM((T + CONV_HALO, D), F32), pltpu.VMEM((T + CONV_HALO, D), F32)],
        args=[dc, dc, proj, proj, proj, proj, cw], sem=("arbitrary",), comm=comm)


def assemble_dproj0(dq, dkc, dkp, dvc, dvp, da, db, dgate, D):
    S = dq.shape[0]
    tq = _attn_tq(S)
    T = _pick(S, (256,))
    shift = tq // T
    nt = S // T

    def body(dq_ref, dkc_ref, dkp_ref, dvc_ref, dvp_ref, da_ref, db_ref, dg_ref, o_ref):
        i = pl.program_id(0)
        last = i + shift >= nt
        o_ref[:, 0:D] = dq_ref[...]
        dk = dkc_ref[...].astype(F32) + jnp.where(last, 0.0, dkp_ref[...].astype(F32))
        dv = dvc_ref[...].astype(F32) + jnp.where(last, 0.0, dvp_ref[...].astype(F32))
        o_ref[:, D:2 * D] = dk.astype(o_ref.dtype)
        o_ref[:, 2 * D:3 * D] = dv.astype(o_ref.dtype)
        o_ref[:, 3 * D:4 * D] = da_ref[...]
        o_ref[:, 4 * D:5 * D] = db_ref[...]
        o_ref[:, 5 * D:] = dg_ref[...]

    row = pl.BlockSpec((T, D), lambda i: (i, 0))
    nxt = pl.BlockSpec((T, D), lambda i: (jnp.minimum(i + shift, nt - 1), 0))
    return pl.pallas_call(
        body, grid=(nt,),
        in_specs=[row, row, nxt, row, nxt, row, row, pl.BlockSpec((T, 2 * D), lambda i: (i, 0))],
        out_specs=pl.BlockSpec((T, 7 * D), lambda i: (i, 0)),
        out_shape=jax.ShapeDtypeStruct((S, 7 * D), BF16),
        name="assemble_dproj0", compiler_params=_cparams("parallel"))(dq, dkc, dkp, dvc, dvp, da, db, dgate)


def _sgu_t(S):
    return _pick(S, (256, 128))


def _ws_masked(ws_ref, g):
    row = lax.broadcasted_iota(jnp.int32, (GMLP_CHUNK, GMLP_CHUNK), 0) // CHUNK
    col = lax.broadcasted_iota(jnp.int32, (GMLP_CHUNK, GMLP_CHUNK), 1) // CHUNK
    return jnp.where(row >= col, ws_ref[g], 0.0), row >= col


def sgu_fwd(proj, lng, lnb, ws, bst, MIX):
    S = proj.shape[0]
    T = _sgu_t(S)
    gw = MIX // N_GROUPS_C

    def body(u_ref, v_ref, g_ref, lng_ref, lnb_ref, ws_ref, bst_ref, y_ref):
        v = v_ref[...].astype(F32)
        mu = jnp.mean(v, axis=-1, keepdims=True)
        xc = v - mu
        rstd = lax.rsqrt(jnp.mean(xc * xc, axis=-1, keepdims=True) + EPS)
        for g in range(N_GROUPS_C):
            cols = slice(g * gw, (g + 1) * gw)
            wsm = _ws_masked(ws_ref, g)[0].astype(BF16)
            vn = (xc[:, cols] * rstd * lng_ref[:, cols] + lnb_ref[:, cols]).astype(BF16)
            for blk in range(T // GMLP_CHUNK):
                rows = slice(blk * GMLP_CHUNK, (blk + 1) * GMLP_CHUNK)
                sg = _dot(wsm, vn[rows], NN) + bst_ref[:, g:g + 1]
                gate = g_ref[rows, cols].astype(F32)
                y = u_ref[rows, cols].astype(F32) * sg * (gate * _sigmoid(gate))
                y_ref[rows, cols] = y.astype(y_ref.dtype)

    def part(cidx):
        return pl.BlockSpec((T, MIX), lambda i: (i, cidx))

    vec = pl.BlockSpec((1, MIX), lambda i: (0, 0))
    return pl.pallas_call(
        body, grid=(S // T,),
        in_specs=[part(0), part(1), part(2), vec, vec,
                  pl.BlockSpec((N_GROUPS_C, GMLP_CHUNK, GMLP_CHUNK), lambda i: (0, 0, 0)),
                  pl.BlockSpec((GMLP_CHUNK, N_GROUPS_C), lambda i: (0, 0))],
        out_specs=pl.BlockSpec((T, MIX), lambda i: (i, 0)),
        out_shape=jax.ShapeDtypeStruct((S, MIX), BF16),
        name="sgu_fwd", compiler_params=_cparams("parallel"))(proj, proj, proj, lng, lnb, ws, bst)


def sgu_bwd(dy1, proj, lng, lnb, ws, bst, MIX):
    S = proj.shape[0]
    T = _sgu_t(S)
    gw = MIX // N_GROUPS_C

    def body(dy_ref, u_ref, v_ref, g_ref, lng_ref, lnb_ref, ws_ref, bst_ref,
             dp_ref, dws_ref, dbst_ref, dlng_ref, dlnb_ref, dvn_buf):
        i = pl.program_id(0)

        @pl.when(i == 0)
        def _():
            dws_ref[...] = jnp.zeros_like(dws_ref)
            dbst_ref[...] = jnp.zeros_like(dbst_ref)
            dlng_ref[...] = jnp.zeros_like(dlng_ref)
            dlnb_ref[...] = jnp.zeros_like(dlnb_ref)

        v = v_ref[...].astype(F32)
        mu = jnp.mean(v, axis=-1, keepdims=True)
        xc = v - mu
        rstd = lax.rsqrt(jnp.mean(xc * xc, axis=-1, keepdims=True) + EPS)
        for g in range(N_GROUPS_C):
            cols = slice(g * gw, (g + 1) * gw)
            wsf, keep = _ws_masked(ws_ref, g)
            wsm = wsf.astype(BF16)
            vn = (xc[:, cols] * rstd * lng_ref[:, cols] + lnb_ref[:, cols]).astype(BF16)
            for blk in range(T // GMLP_CHUNK):
                rows = slice(blk * GMLP_CHUNK, (blk + 1) * GMLP_CHUNK)
                vnb = vn[rows]
                sg = _dot(wsm, vnb, NN) + bst_ref[:, g:g + 1]
                gate = g_ref[rows, cols].astype(F32)
                sig = _sigmoid(gate)
                sil = gate * sig
                u = u_ref[rows, cols].astype(F32)
                dy = dy_ref[rows, cols].astype(F32)
                dp_ref[rows, g * gw:(g + 1) * gw] = (dy * sg * sil).astype(dp_ref.dtype)
                dp_ref[rows, 2 * MIX + g * gw:2 * MIX + (g + 1) * gw] = (
                    dy * u * sg * (sig * (1.0 + gate * (1.0 - sig)))).astype(dp_ref.dtype)
                dsg = dy * u * sil
                dsgb = dsg.astype(BF16)
                dvn_buf[rows, cols] = _dot(wsm, dsgb, TN)
                dws_ref[g] += jnp.where(keep, _dot(dsgb, vnb, NT), 0.0)
                dbst_ref[:, g:g + 1] += jnp.sum(dsg, axis=-1, keepdims=True)
        dvn = dvn_buf[...]
        xhat = xc * rstd
        dxhat = dvn * lng_ref[...]
        dv = rstd * (dxhat - jnp.mean(dxhat, axis=-1, keepdims=True)
                     - xhat * jnp.mean(dxhat * xhat, axis=-1, keepdims=True))
        dp_ref[:, MIX:2 * MIX] = dv.astype(dp_ref.dtype)
        dlng_ref[...] += jnp.sum(dvn * xhat, axis=0, keepdims=True)
        dlnb_ref[...] += jnp.sum(dvn, axis=0, keepdims=True)

    def part(cidx):
        return pl.BlockSpec((T, MIX), lambda i: (i, cidx))

    vec = pl.BlockSpec((1, MIX), lambda i: (0, 0))
    wspec = pl.BlockSpec((N_GROUPS_C, GMLP_CHUNK, GMLP_CHUNK), lambda i: (0, 0, 0))
    bspec = pl.BlockSpec((GMLP_CHUNK, N_GROUPS_C), lambda i: (0, 0))
    return pl.pallas_call(
        body, grid=(S // T,),
        in_specs=[pl.BlockSpec((T, MIX), lambda i: (i, 0)), part(0), part(1), part(2), vec, vec, wspec, bspec],
        out_specs=[pl.BlockSpec((T, 3 * MIX), lambda i: (i, 0)), wspec, bspec, vec, vec],
        out_shape=[jax.ShapeDtypeStruct((S, 3 * MIX), BF16),
                   jax.ShapeDtypeStruct((N_GROUPS_C, GMLP_CHUNK, GMLP_CHUNK), F32),
                   jax.ShapeDtypeStruct((GMLP_CHUNK, N_GROUPS_C), F32),
                   jax.ShapeDtypeStruct((1, MIX), F32), jax.ShapeDtypeStruct((1, MIX), F32)],
        scratch_shapes=[pltpu.VMEM((T, MIX), F32)],
        name="sgu_bwd", compiler_params=_cparams("arbitrary"))(dy1, proj, proj, proj, lng, lnb, ws, bst)


def xattn_fwd(name, q, k, v):
    S, D = q.shape
    nm = k.shape[0]
    dh = D // N_HEADS_X
    tq = _pick(S, (512, 256))
    scale = dh ** -0.5

    def body(q_ref, k_ref, v_ref, o_ref, lse_ref):
        s = _dot(q_ref[...], k_ref[...], NT) * scale
        m = jnp.max(s, axis=-1, keepdims=True)
        p = jnp.exp(s - m)
        l = jnp.sum(p, axis=-1, keepdims=True)
        o_ref[...] = (_dot(p.astype(BF16), v_ref[...], NN) / l).astype(o_ref.dtype)
        lse_ref[...] = m + jnp.log(l)

    return pl.pallas_call(
        body, grid=(N_HEADS_X, S // tq),
        in_specs=[pl.BlockSpec((tq, dh), lambda h, i: (i, h)),
                  pl.BlockSpec((nm, dh), lambda h, i: (0, h)), pl.BlockSpec((nm, dh), lambda h, i: (0, h))],
        out_specs=[pl.BlockSpec((tq, dh), lambda h, i: (i, h)),
                   pl.BlockSpec((None, tq, 1), lambda h, i: (h, i, 0))],
        out_shape=[jax.ShapeDtypeStruct((S, D), BF16), jax.ShapeDtypeStruct((N_HEADS_X, S, 1), F32)],
        name=name, compiler_params=_cparams("parallel", "parallel"))(q, k, v)


def xattn_bwd(name, q, k, v, o, do, lse):
    S, D = q.shape
    nm = k.shape[0]
    dh = D // N_HEADS_X
    tq = _pick(S, (512, 256))
    scale = dh ** -0.5

    def body(q_ref, k_ref, v_ref, o_ref, do_ref, lse_ref, dq_ref, dk_ref, dv_ref):
        i = pl.program_id(1)
        q_v = q_ref[...]
        k_v = k_ref[...]
        do_v = do_ref[...]
        p = jnp.exp(_dot(q_v, k_v, NT) * scale - lse_ref[...])
        delta = jnp.sum(do_v.astype(F32) * o_ref[...].astype(F32), axis=-1, keepdims=True)
        dv = _dot(p.astype(BF16), do_v, TN)
        ds = (p * (_dot(do_v, v_ref[...], NT) - delta)).astype(BF16)
        dq_ref[...] = (_dot(ds, k_v, NN) * scale).astype(dq_ref.dtype)
        dk = _dot(ds, q_v, TN) * scale

        @pl.when(i == 0)
        def _():
            dk_ref[...] = dk
            dv_ref[...] = dv

        @pl.when(i > 0)
        def _():
            dk_ref[...] += dk
            dv_ref[...] += dv

    qs = pl.BlockSpec((tq, dh), lambda h, i: (i, h))
    ks = pl.BlockSpec((nm, dh), lambda h, i: (0, h))
    return pl.pallas_call(
        body, grid=(N_HEADS_X, S // tq),
        in_specs=[qs, ks, ks, qs, qs, pl.BlockSpec((None, tq, 1), lambda h, i: (h, i, 0))],
        out_specs=[qs, ks, ks],
        out_shape=[jax.ShapeDtypeStruct((S, D), BF16), jax.ShapeDtypeStruct((nm, D), F32),
                   jax.ShapeDtypeStruct((nm, D), F32)],
        name=name, compiler_params=_cparams("parallel", "arbitrary"))(q, k, v, o, do, lse)


def adamw(name, w, g, m, v):
    R, C = w.shape
    tr = _pick(R, tuple(t for t in (512, 256, 128, 64, 32, 16, 8) if t * C * 4 <= (1 << 20)) or (8,))
    c1 = 1.0 - ADAM_B1 ** ADAM_STEP
    c2 = 1.0 - ADAM_B2 ** ADAM_STEP

    def body(w_ref, g_ref, m_ref, v_ref, d_ref, nm_ref, nv_ref):
        gv = g_ref[...]
        nm = ADAM_B1 * m_ref[...] + (1.0 - ADAM_B1) * gv
        nv = ADAM_B2 * v_ref[...] + (1.0 - ADAM_B2) * (gv * gv)
        d_ref[...] = -ADAM_LR * ((nm / c1) / (jnp.sqrt(nv / c2) + ADAM_EPS) + ADAM_WD * w_ref[...])
        nm_ref[...] = nm
        nv_ref[...] = nv

    blk = pl.BlockSpec((tr, C), lambda i: (i, 0))
    sd = jax.ShapeDtypeStruct((R, C), F32)
    return pl.pallas_call(body, grid=(R // tr,), in_specs=[blk] * 4, out_specs=[blk] * 3,
                          out_shape=[sd, sd, sd], name=name, compiler_params=_cparams("parallel"))(w, g, m, v)


def add_halves(name, g4, recv, cidx):
    _, R, C = g4.shape
    rh = R // 2
    tr = _pick(rh, (256, 128, 64, 32, 16))
    nrb = rh // tr

    def body(c_ref, a_ref, b_ref, o_ref):
        o_ref[...] = (a_ref[...].astype(F32) + b_ref[...].astype(F32)).astype(o_ref.dtype)

    grid_spec = pltpu.PrefetchScalarGridSpec(
        num_scalar_prefetch=1, grid=(4, nrb),
        in_specs=[pl.BlockSpec((None, tr, C), lambda j, r, c_ref: (j, c_ref[0] * nrb + r, 0)),
                  pl.BlockSpec((None, tr, C), lambda j, r, c_ref: (j, r, 0))],
        out_specs=pl.BlockSpec((None, tr, C), lambda j, r, c_ref: (j, r, 0)))
    return pl.pallas_call(body, grid_spec=grid_spec, out_shape=jax.ShapeDtypeStruct((4, rh, C), BF16),
                          name=name, compiler_params=_cparams("parallel", "parallel"))(cidx, g4, recv)


def sum_chips(name, own, recv, place):
    _, rh, C = own.shape
    tr = _pick(rh, (256, 128, 64, 32, 16))
    nrb = rh // tr

    def body(s_ref, own_ref, recv_ref, o_ref):
        acc = own_ref[...].astype(F32)
        for k in range(N_CHIPS - 1):
            acc = acc + recv_ref[k].astype(F32)
        o_ref[...] = acc

    grid_spec = pltpu.PrefetchScalarGridSpec(
        num_scalar_prefetch=1, grid=(nrb,),
        in_specs=[pl.BlockSpec((None, tr, C), lambda r, s: (s[0], r, 0)),
                  pl.BlockSpec((N_CHIPS - 1, tr, C), lambda r, s: (0, r, 0))],
        out_specs=pl.BlockSpec((tr, C), lambda r, s: (s[1] * nrb + r, 0)))
    return pl.pallas_call(body, grid_spec=grid_spec, out_shape=jax.ShapeDtypeStruct((2 * rh, C), F32),
                          name=name, compiler_params=_cparams("parallel"))(place, own, recv)


def cast_into_slot(name, w, place):
    R, C = w.shape
    tr = _pick(R, (256, 128, 64, 32, 16))

    def body(s_ref, w_ref, o_ref):
        o_ref[...] = w_ref[...].astype(o_ref.dtype)

    grid_spec = pltpu.PrefetchScalarGridSpec(
        num_scalar_prefetch=1, grid=(R // tr,),
        in_specs=[pl.BlockSpec((tr, C), lambda r, s: (r, 0))],
        out_specs=pl.BlockSpec((None, tr, C), lambda r, s: (s[0], r, 0)))
    return pl.pallas_call(body, grid_spec=grid_spec, out_shape=jax.ShapeDtypeStruct((N_CHIPS, R, C), BF16),
                          name=name, compiler_params=_cparams("parallel"))(place, w)


def _place():
    return lax.axis_index("x"), lax.axis_index("y"), lax.axis_index("c")


_CHIP_FLIPS = ((1, 0), (0, 1), (1, 1))


def _flip(v, bit):
    return 1 - v if bit else v


HBM_SPEC = pl.BlockSpec(memory_space=pl.ANY)


def exchange_small(name, buf, reduce):
    R = buf.shape[0]

    def body(x_ref, *refs):
        if reduce:
            sum_ref, all_ref, send_sems, recv_sems, local_sem = refs
        else:
            all_ref, send_sems, recv_sems, local_sem = refs
        x, y, c = _place()
        me = 4 * x + 2 * y + c
        mine = pltpu.make_async_copy(x_ref, all_ref.at[me], local_sem)
        mine.start()
        sends = []
        for k in range(1, N_DEV):
            peer = (_flip(x, k & 4), _flip(y, k & 2), _flip(c, k & 1))
            cp = pltpu.make_async_remote_copy(src_ref=x_ref, dst_ref=all_ref.at[me], send_sem=send_sems.at[k - 1],
                                              recv_sem=recv_sems.at[k - 1], device_id=peer, device_id_type=MESH)
            cp.start()
            sends.append(cp)
        for k in range(1, N_DEV):
            peer = (_flip(x, k & 4), _flip(y, k & 2), _flip(c, k & 1))
            src = 4 * peer[0] + 2 * peer[1] + peer[2]
            pltpu.make_async_remote_copy(src_ref=x_ref, dst_ref=all_ref.at[src], send_sem=send_sems.at[k - 1],
                                         recv_sem=recv_sems.at[k - 1], device_id=peer,
                                         device_id_type=MESH).wait_recv()
        for cp in sends:
            cp.wait_send()
        mine.wait()
        if reduce:
            acc = all_ref[0]
            for d in range(1, N_DEV):
                acc = acc + all_ref[d]
            sum_ref[...] = acc

    vm = pl.BlockSpec(memory_space=pltpu.VMEM)
    sems = [pltpu.SemaphoreType.DMA((N_DEV - 1,)), pltpu.SemaphoreType.DMA((N_DEV - 1,)), pltpu.SemaphoreType.DMA]
    if reduce:
        return pl.pallas_call(
            body, in_specs=[vm], out_specs=vm, out_shape=jax.ShapeDtypeStruct((R, LANES), F32),
            scratch_shapes=[pltpu.VMEM((N_DEV, R, LANES), F32)] + sems, name=name,
            compiler_params=pltpu.CompilerParams(vmem_limit_bytes=V7X_VMEM_LIMIT))(buf)
    return pl.pallas_call(
        body, in_specs=[vm], out_specs=vm, out_shape=jax.ShapeDtypeStruct((N_DEV, R, LANES), F32),
        scratch_shapes=sems, name=name,
        compiler_params=pltpu.CompilerParams(vmem_limit_bytes=V7X_VMEM_LIMIT))(buf)


def gather_job(slots):
    n = len(slots)

    def copies(o_refs, send_sems, recv_sems):
        x, y, c = _place()
        me = 2 * x + y
        sib = (x, y, 1 - c)
        chips = [(_flip(x, fx), _flip(y, fy)) for fx, fy in _CHIP_FLIPS]
        ici, fwd, from_sib = [], [], []
        for t in range(n):
            rh = o_refs[t].shape[1] // 2
            mine, theirs = pl.ds(c * rh, rh), pl.ds((1 - c) * rh, rh)
            for k, (px, py) in enumerate(chips):
                own = o_refs[t].at[me, mine]
                ici.append(pltpu.make_async_remote_copy(
                    src_ref=own, dst_ref=own, send_sem=send_sems.at[t, k], recv_sem=recv_sems.at[t, k],
                    device_id=(px, py, c), device_id_type=MESH))
                landed = o_refs[t].at[2 * px + py, mine]
                arrival = pltpu.make_async_remote_copy(
                    src_ref=landed, dst_ref=landed, send_sem=send_sems.at[t, k], recv_sem=recv_sems.at[t, k],
                    device_id=(px, py, c), device_id_type=MESH)
                fwd.append((arrival, pltpu.make_async_remote_copy(
                    src_ref=landed, dst_ref=landed, send_sem=send_sems.at[t, 3 + k],
                    recv_sem=recv_sems.at[t, 3 + k], device_id=sib, device_id_type=MESH)))
                passed = o_refs[t].at[2 * px + py, theirs]
                from_sib.append(pltpu.make_async_remote_copy(
                    src_ref=passed, dst_ref=passed, send_sem=send_sems.at[t, 3 + k],
                    recv_sem=recv_sems.at[t, 3 + k], device_id=sib, device_id_type=MESH))
        return ici, fwd, from_sib

    def start(ins, o_refs, sems):
        for cp in copies(o_refs, *sems)[0]:
            cp.start()

    def finish(ins, o_refs, sems):
        ici, fwd, from_sib = copies(o_refs, *sems)
        for arrival, forward in fwd:
            arrival.wait_recv()
            forward.start()
        for cp in from_sib:
            cp.wait_recv()
        for cp in ici:
            cp.wait_send()
        for _, forward in fwd:
            forward.wait_send()

    return _Comm(slots, [jax.ShapeDtypeStruct(s.shape, s.dtype) for s in slots], {t: t for t in range(n)},
                 [pltpu.SemaphoreType.DMA((n, 6)), pltpu.SemaphoreType.DMA((n, 6))], start, finish)


def sibling_halves_job(grads):
    n = len(grads)

    def copies(g_refs, o_refs, send_sems, recv_sems):
        x, y, c = _place()
        out = []
        for t in range(n):
            rh = g_refs[t].shape[1] // 2
            out.append(pltpu.make_async_remote_copy(
                src_ref=g_refs[t].at[:, pl.ds((1 - c) * rh, rh), :], dst_ref=o_refs[t],
                send_sem=send_sems.at[t], recv_sem=recv_sems.at[t], device_id=(x, y, 1 - c),
                device_id_type=MESH))
        return out

    def start(g_refs, o_refs, sems):
        for cp in copies(g_refs, o_refs, *sems):
            cp.start()

    def finish(g_refs, o_refs, sems):
        cps = copies(g_refs, o_refs, *sems)
        for cp in cps:
            cp.wait_recv()
        for cp in cps:
            cp.wait_send()

    return _Comm(grads, [jax.ShapeDtypeStruct((4, g.shape[1] // 2, g.shape[2]), g.dtype) for g in grads], {},
                 [pltpu.SemaphoreType.DMA((n,)), pltpu.SemaphoreType.DMA((n,))], start, finish)


def scatter_job(parts):
    n = len(parts)

    def copies(p_refs, o_refs, send_sems, recv_sems):
        x, y, c = _place()
        out = []
        for t in range(n):
            for k, (fx, fy) in enumerate(_CHIP_FLIPS):
                px, py = _flip(x, fx), _flip(y, fy)
                out.append(pltpu.make_async_remote_copy(
                    src_ref=p_refs[t].at[2 * px + py], dst_ref=o_refs[t].at[k],
                    send_sem=send_sems.at[t, k], recv_sem=recv_sems.at[t, k],
                    device_id=(px, py, c), device_id_type=MESH))
        return out

    def start(p_refs, o_refs, sems):
        for cp in copies(p_refs, o_refs, *sems):
            cp.start()

    def finish(p_refs, o_refs, sems):
        cps = copies(p_refs, o_refs, *sems)
        for cp in cps:
            cp.wait_recv()
        for cp in cps:
            cp.wait_send()

    return _Comm(parts, [jax.ShapeDtypeStruct((N_CHIPS - 1,) + p.shape[1:], p.dtype) for p in parts], {},
                 [pltpu.SemaphoreType.DMA((n, 3)), pltpu.SemaphoreType.DMA((n, 3))], start, finish)


def share_halves_job(halves):
    n = len(halves)

    def copies(o_refs, send_sems, recv_sems):
        x, y, c = _place()
        sends, arrivals = [], []
        for t in range(n):
            rh = o_refs[t].shape[0] // 2
            mine = o_refs[t].at[pl.ds(c * rh, rh)]
            theirs = o_refs[t].at[pl.ds((1 - c) * rh, rh)]
            sends.append(pltpu.make_async_remote_copy(
                src_ref=mine, dst_ref=mine, send_sem=send_sems.at[t], recv_sem=recv_sems.at[t],
                device_id=(x, y, 1 - c), device_id_type=MESH))
            arrivals.append(pltpu.make_async_remote_copy(
                src_ref=theirs, dst_ref=theirs, send_sem=send_sems.at[t], recv_sem=recv_sems.at[t],
                device_id=(x, y, 1 - c), device_id_type=MESH))
        return sends, arrivals

    def start(ins, o_refs, sems):
        for cp in copies(o_refs, *sems)[0]:
            cp.start()

    def finish(ins, o_refs, sems):
        sends, arrivals = copies(o_refs, *sems)
        for cp in arrivals:
            cp.wait_recv()
        for cp in sends:
            cp.wait_send()

    return _Comm(halves, [jax.ShapeDtypeStruct(h.shape, h.dtype) for h in halves], {t: t for t in range(n)},
                 [pltpu.SemaphoreType.DMA((n,)), pltpu.SemaphoreType.DMA((n,))], start, finish)


def _pack(arrs):
    flat = []
    for a in arrs:
        v = a.reshape(-1).astype(F32)
        pad = (-v.shape[0]) % (8 * LANES)
        flat.append(jnp.pad(v, (0, pad)))
    return jnp.concatenate(flat).reshape(-1, LANES)


def _unpack(buf, shapes):
    out, off = [], 0
    flat = buf.reshape(-1)
    for s in shapes:
        n = int(np.prod(s))
        out.append(flat[off:off + n].reshape(s))
        off += n + ((-n) % (8 * LANES))
    return out


def _xattn_layer_fwd(tag, h, mem, gx, gmem, w):
    hx = rms_fwd(f"rms_x{tag}", h, gx)
    memn = rms_fwd(f"rms_mem{tag}", mem, gmem)
    q = mm_nn(f"xq{tag}", hx, w["q"], BF16)
    k = mm_nn(f"xk{tag}", memn, w["k"], BF16)
    v = mm_nn(f"xv{tag}", memn, w["v"], BF16)
    o, lse = xattn_fwd(f"xattn_fwd{tag}", q, k, v)
    h_out = mm_nn(f"xo{tag}", o, w["o"], F32, res=h)
    return h_out, dict(hx=hx, memn=memn, q=q, k=k, v=v, o=o, lse=lse)


def _xattn_layer_bwd(tag, dh_out, dh_out_b, h_in, mem, gx, gmem, w, sv):
    do = mm_nt(f"d_xo{tag}", dh_out_b, w["o"], BF16)
    dwo = mm_tn(f"dw_xo{tag}", sv["o"], dh_out_b)
    dq, dk, dv = xattn_bwd(f"xattn_bwd{tag}", sv["q"], sv["k"], sv["v"], sv["o"], do, sv["lse"])
    dwq = mm_tn(f"dw_xq{tag}", sv["hx"], dq)
    dhx = mm_nt(f"d_xq{tag}", dq, w["q"], F32)
    dwk = mm_tn(f"dw_xk{tag}", sv["memn"], dk)
    dwv = mm_tn(f"dw_xv{tag}", sv["memn"], dv)
    dmk = mm_nt(f"d_xk{tag}", dk, w["k"], F32)
    dmv = mm_nt(f"d_xv{tag}", dv, w["v"], F32)
    dh_in, dh_in_b, dgx = rms_bwd(f"rms_x_bwd{tag}", h_in, gx, [dhx], dh_out)
    _, _, dgmem = rms_bwd(f"rms_mem_bwd{tag}", mem, gmem, [dmk, dmv], None)
    return dh_in, dh_in_b, dgx, dgmem, dict(q=dwq, k=dwk, v=dwv, o=dwo)


def kernel(x, mem, norm_mix_g, norm_x_g, norm_mem_g, final_norm_g, w_in_ab, rel_bias, conv_w, conv_b, conv_ln_g, conv_ln_b, w_out_ab, w_in_c, sgu_ln_g, sgu_ln_b, w_s, b_s, w_out_c, w_xq, w_xk, w_xv, w_xo, loss_target, m_norm_mix_g, m_norm_x_g, m_norm_mem_g, m_final_norm_g, m_w_in_ab, m_rel_bias, m_conv_w, m_conv_b, m_conv_ln_g, m_conv_ln_b, m_w_out_ab, m_w_in_c, m_sgu_ln_g, m_sgu_ln_b, m_w_s, m_b_s, m_w_out_c, m_w_xq, m_w_xk, m_w_xv, m_w_xo, v_norm_mix_g, v_norm_x_g, v_norm_mem_g, v_final_norm_g, v_w_in_ab, v_rel_bias, v_conv_w, v_conv_b, v_conv_ln_g, v_conv_ln_b, v_w_out_ab, v_w_in_c, v_sgu_ln_g, v_sgu_ln_b, v_w_s, v_b_s, v_w_out_c, v_w_xq, v_w_xk, v_w_xv, v_w_xo):
    S, D = x.shape[1], x.shape[2]
    MIX = 2 * D
    xs, mems, tgt = x[0], mem[0], loss_target[0]
    cx, cy, cc = _place()
    chip = 2 * cx + cy
    cidx = jnp.reshape(cc, (1,)).astype(jnp.int32)
    place = jnp.stack([chip, cc]).astype(jnp.int32)

    ro, rq = MIX // 4, D // 4
    row_sharded = [("out_ab", w_out_ab[0]), ("out_c", w_out_c[0])]
    for layer in range(2):
        for nm_, w in (("q", w_xq), ("k", w_xk), ("v", w_xv), ("o", w_xo)):
            row_sharded.append((f"x{nm_}{layer}", w[layer]))
    slots = {"in_ab": cast_into_slot("cast_in_ab", w_in_ab[0], place),
             "in_c": cast_into_slot("cast_in_c", w_in_c[0], place)}
    slots.update({nm_: cast_into_slot("cast_" + nm_, w, place) for nm_, w in row_sharded})

    small_sh = [conv_w[0], sgu_ln_g[0], sgu_ln_b[0]]
    gathered = exchange_small("gather_small", _pack(small_sh), reduce=False)
    per_chip = [_unpack(gathered[2 * j], [a.shape for a in small_sh]) for j in range(N_CHIPS)]
    conv_w_full = jnp.concatenate([p[0] for p in per_chip], axis=1)
    sgu_g_full = jnp.concatenate([p[1] for p in per_chip], axis=0).reshape(1, MIX)
    sgu_b_full = jnp.concatenate([p[2] for p in per_chip], axis=0).reshape(1, MIX)
    cw_pad = jnp.pad(conv_w_full, ((0, CONV_HALO - CONV_WIDTH), (0, 0)))
    cb = conv_b.reshape(1, D)
    clg, clb = conv_ln_g.reshape(1, D), conv_ln_b.reshape(1, D)
    ws = w_s[0]
    bst = jnp.transpose(b_s[0])
    tq = _attn_tq(S)
    bm = band_bias_table(rel_bias[0], tq)

    hn0 = rms_fwd("rms_mix0", xs, norm_mix_g[0])
    (wab4,) = run_comm("gather_in_ab", gather_job([slots["in_ab"]]))
    layer0 = ["out_ab", "xq0", "xk0", "xv0", "xo0"]
    layer1 = ["out_c", "xq1", "xk1", "xv1", "xo1"]
    proj0, got0 = mm_nn_cols("proj_ab", hn0, wab4, BF16, comm=gather_job([slots[n] for n in layer0]))
    (ya, lse_a), (wc4,) = attn_fwd(proj0, bm, D, comm=gather_job([slots["in_c"]]))
    (y0, cpre), got1 = conv_gate_fwd(proj0, ya, cw_pad, cb, clg, clb, D,
                                     comm=gather_job([slots[n] for n in layer1]))
    wrow = {n: g.reshape(-1, g.shape[2]) for n, g in zip(layer0 + layer1, got0 + got1)}
    wx = [{k: wrow[f"x{k}{layer}"] for k in "qkvo"} for layer in range(2)]
    h1 = mm_nn("out_ab", y0, wrow["out_ab"], F32, res=xs)
    h2, sx0 = _xattn_layer_fwd("0", h1, mems, norm_x_g[0], norm_mem_g[0], wx[0])
    hn1 = rms_fwd("rms_mix1", h2, norm_mix_g[1])
    proj1 = mm_nn_cols("proj_c", hn1, wc4, BF16)
    y1 = sgu_fwd(proj1, sgu_g_full, sgu_b_full, ws, bst, MIX)
    h3 = mm_nn("out_c", y1, wrow["out_c"], F32, res=h2)
    h4, sx1 = _xattn_layer_fwd("1", h3, mems, norm_x_g[1], norm_mem_g[1], wx[1])
    loss_row, dg_final, dh4, dh4b = loss_head("loss_head", h4, final_norm_g, tgt)

    dh3, dh3b, dgx1, dgmem1, dwx1 = _xattn_layer_bwd("1", dh4, dh4b, h3, mems, norm_x_g[1], norm_mem_g[1], wx[1], sx1)
    def stack_rows(dw_out, dwx):
        return jnp.concatenate([g.reshape(N_CHIPS, -1, g.shape[1]) for g in [dw_out] + [dwx[k] for k in "qkvo"]],
                               axis=1)

    dy1 = mm_nt("d_out_c", dh3b, wrow["out_c"], BF16)
    dw_out_c = mm_tn("dw_out_c", y1, dh3b)
    dproj1, dws, dbst, dsgu_g, dsgu_b = sgu_bwd(dy1, proj1, sgu_g_full, sgu_b_full, ws, bst, MIX)
    grp1 = stack_rows(dw_out_c, dwx1)
    dw_in_c, (sib1,) = mm_tn_cols("dw_in_c", hn1, dproj1, comm=sibling_halves_job([grp1]))
    part1 = add_halves("add_halves1", grp1, sib1, cidx)
    dhn1, (recv1, sib2) = mm_nt_cols("d_proj_c", dproj1, wc4, F32,
                                     comm=_join(scatter_job([part1]), sibling_halves_job([dw_in_c])))
    part2 = add_halves("add_halves2", dw_in_c, sib2, cidx)
    dh2, dh2b, dgmix1 = rms_bwd("rms_mix1_bwd", h2, norm_mix_g[1], [dhn1], dh3)
    dh1, dh1b, dgx0, dgmem0, dwx0 = _xattn_layer_bwd("0", dh2, dh2b, h1, mems, norm_x_g[0], norm_mem_g[0], wx[0], sx0)
    dy0 = mm_nt("d_out_ab", dh1b, wrow["out_ab"], BF16)
    dw_out_ab = mm_tn("dw_out_ab", y0, dh1b)
    grp3 = stack_rows(dw_out_ab, dwx0)
    dya, dgate, dc, dclg, dclb = conv_gate_bwd_a(dy0, proj0, ya, cpre, clg, clb, D)
    (da, db, dcw, dcb), (recv2, sib3) = conv_gate_bwd_b(
        dc, proj0, cw_pad, D, comm=_join(scatter_job([part2]), sibling_halves_job([grp3])))
    part3 = add_halves("add_halves3", grp3, sib3, cidx)
    (dq, dkc, dkp, dvc, dvp, ds_sum), (recv3,) = attn_bwd(proj0, ya, dya, lse_a, bm, D, comm=scatter_job([part3]))
    drel = rel_bias_grad(ds_sum)
    dproj0 = assemble_dproj0(dq, dkc, dkp, dvc, dvp, da, db, dgate, D)
    dw_in_ab = mm_tn_cols("dw_in_ab", hn0, dproj0)
    (sib4,) = run_comm("sibling_halves4", sibling_halves_job([dw_in_ab]))
    part4 = add_halves("add_halves4", dw_in_ab, sib4, cidx)
    dhn0, (recv4,) = mm_nt_cols("d_proj_ab", dproj0, wab4, F32, comm=scatter_job([part4]))
    dx, _, dgmix0 = rms_bwd("rms_mix0_bwd", xs, norm_mix_g[0], [dhn0], dh1)
    halves = [sum_chips(f"sum_chips{t + 1}", p, r, place)
              for t, (p, r) in enumerate(((part1, recv1), (part2, recv2), (part3, recv3), (part4, recv4)))]
    g_r1, g_c, g_r0, g_ab = run_comm("share_reduced_halves", share_halves_job(halves))

    small_full = [
        jnp.concatenate([dgmix0, dgmix1], axis=0), jnp.concatenate([dgx0, dgx1], axis=0),
        jnp.concatenate([dgmem0, dgmem1], axis=0), dg_final.reshape(D), drel[None],
        dcb, dclg, dclb, dws[None], jnp.transpose(dbst)[None],
        dcw[:CONV_WIDTH][None], dsgu_g, dsgu_b]
    summed = _unpack(exchange_small("reduce_small", _pack(small_full), reduce=True), [a.shape for a in small_full])
    (g_norm_mix, g_norm_x, g_norm_mem, g_final, g_rel, g_conv_b, g_clg, g_clb, g_ws, g_bs,
     g_conv_w_full, g_sgu_g_full, g_sgu_b_full) = summed
    cws = conv_w.shape[2]
    g_conv_w = lax.dynamic_slice_in_dim(g_conv_w_full, chip * cws, cws, axis=2)
    sgs = sgu_ln_g.shape[1]
    g_sgu_g = lax.dynamic_slice_in_dim(g_sgu_g_full, chip * sgs, sgs, axis=1)
    g_sgu_b = lax.dynamic_slice_in_dim(g_sgu_b_full, chip * sgs, sgs, axis=1)

    loss = lax.psum(loss_row[0, 0], ("x", "y", "c"))

    g_rows = {"w_out_ab": g_r0[0:ro][None], "w_out_c": g_r1[0:ro][None]}
    for i, nm_ in enumerate("qkvo"):
        lo = ro + i * rq
        g_rows["w_x" + nm_] = jnp.stack([g_r0[lo:lo + rq], g_r1[lo:lo + rq]])
    grads = dict(
        norm_mix_g=g_norm_mix, norm_x_g=g_norm_x, norm_mem_g=g_norm_mem, final_norm_g=g_final,
        w_in_ab=g_ab[None], rel_bias=g_rel, conv_w=g_conv_w, conv_b=g_conv_b, conv_ln_g=g_clg, conv_ln_b=g_clb,
        w_out_ab=g_rows["w_out_ab"], w_in_c=g_c[None], sgu_ln_g=g_sgu_g, sgu_ln_b=g_sgu_b, w_s=g_ws, b_s=g_bs,
        w_out_c=g_rows["w_out_c"], w_xq=g_rows["w_xq"], w_xk=g_rows["w_xk"], w_xv=g_rows["w_xv"],
        w_xo=g_rows["w_xo"])
    weights = dict(
        norm_mix_g=(norm_mix_g, m_norm_mix_g, v_norm_mix_g), norm_x_g=(norm_x_g, m_norm_x_g, v_norm_x_g),
        norm_mem_g=(norm_mem_g, m_norm_mem_g, v_norm_mem_g), final_norm_g=(final_norm_g, m_final_norm_g, v_final_norm_g),
        w_in_ab=(w_in_ab, m_w_in_ab, v_w_in_ab), rel_bias=(rel_bias, m_rel_bias, v_rel_bias),
        conv_w=(conv_w, m_conv_w, v_conv_w), conv_b=(conv_b, m_conv_b, v_conv_b),
        conv_ln_g=(conv_ln_g, m_conv_ln_g, v_conv_ln_g), conv_ln_b=(conv_ln_b, m_conv_ln_b, v_conv_ln_b),
        w_out_ab=(w_out_ab, m_w_out_ab, v_w_out_ab), w_in_c=(w_in_c, m_w_in_c, v_w_in_c),
        sgu_ln_g=(sgu_ln_g, m_sgu_ln_g, v_sgu_ln_g), sgu_ln_b=(sgu_ln_b, m_sgu_ln_b, v_sgu_ln_b),
        w_s=(w_s, m_w_s, v_w_s), b_s=(b_s, m_b_s, v_b_s), w_out_c=(w_out_c, m_w_out_c, v_w_out_c),
        w_xq=(w_xq, m_w_xq, v_w_xq), w_xk=(w_xk, m_w_xk, v_w_xk), w_xv=(w_xv, m_w_xv, v_w_xv),
        w_xo=(w_xo, m_w_xo, v_w_xo))
    names = list(weights)
    big_names = ("w_in_ab", "w_out_ab", "w_in_c", "w_out_c", "w_xq", "w_xk", "w_xv", "w_xo")
    delta, new_m, new_v = {}, {}, {}
    for nm_ in big_names:
        w, m, v = weights[nm_]
        C = w.shape[-1]
        d2, m2, v2 = adamw("adamw_" + nm_, w.reshape(-1, C), grads[nm_].reshape(-1, C), m.reshape(-1, C),
                           v.reshape(-1, C))
        delta[nm_], new_m[nm_], new_v[nm_] = d2.reshape(w.shape), m2.reshape(w.shape), v2.reshape(w.shape)
    small_names = [n for n in names if n not in big_names]
    shapes = [weights[n][0].shape for n in small_names]
    d_s, m_s, v_s = adamw("adamw_small", _pack([weights[n][0] for n in small_names]),
                          _pack([grads[n] for n in small_names]), _pack([weights[n][1] for n in small_names]),
                          _pack([weights[n][2] for n in small_names]))
    for n, d_, m_, v_ in zip(small_names, _unpack(d_s, shapes), _unpack(m_s, shapes), _unpack(v_s, shapes)):
        delta[n], new_m[n], new_v[n] = d_, m_, v_

    return (loss, dx[None], *[grads[n].reshape(weights[n][0].shape) for n in names], *[delta[n] for n in names],
            *[new_m[n] for n in names], *[new_v[n] for n in names])
```

```python
import functools

import numpy as np
import jax
import jax.numpy as jnp
from jax import lax
from jax.experimental import pallas as pl
from jax.experimental.pallas import tpu as pltpu

F32 = jnp.float32
BF16 = jnp.bfloat16
MESH = pl.DeviceIdType.MESH

EPS = 1e-6
CHUNK = 64
N_PAST_CHUNKS = 8
MAX_REL = 128
HEAD_DIM_A = 128
CONV_WIDTH = 31
CONV_HALO = 32
GMLP_CHUNK = 128
N_GROUPS_C = 8
N_HEADS_X = 4
NEG = -1e30

ADAM_LR = 0.001
ADAM_B1 = 0.9
ADAM_B2 = 0.999
ADAM_EPS = 1e-08
ADAM_WD = 0.01
ADAM_STEP = 10

N_CHIPS = 4
N_DEV = 8
V7X_VMEM_LIMIT = 56 * 1024 * 1024
LANES = 128
SUBLANES = 8


def _pick(n, cands):
    for c in cands:
        if c <= n and n % c == 0:
            return c
    return n


def _cparams(*sem):
    return pltpu.CompilerParams(dimension_semantics=sem, vmem_limit_bytes=V7X_VMEM_LIMIT)


def _sigmoid(x):
    return 0.5 * jnp.tanh(0.5 * x) + 0.5


def _dot(a, b, contract):
    return lax.dot_general(a, b, (contract, ((), ())), preferred_element_type=F32)


NN = ((1,), (0,))
NT = ((1,), (1,))
TN = ((0,), (0,))


class _Comm:
    def __init__(self, arrays, out_shapes, aliases, sems, start, finish):
        self.arrays, self.out_shapes, self.aliases, self.sems = list(arrays), list(out_shapes), dict(aliases), list(sems)
        self.start, self.finish = start, finish


def _join(*jobs):
    arrays, outs, sems, aliases, spans = [], [], [], {}, []
    for j in jobs:
        spans.append((len(arrays), len(outs), len(sems)))
        aliases.update({len(arrays) + i: len(outs) + o for i, o in j.aliases.items()})
        arrays += j.arrays
        outs += j.out_shapes
        sems += j.sems

    def part(j, span, ins, os_, ss):
        a0, o0, s0 = span
        return (ins[a0:a0 + len(j.arrays)], os_[o0:o0 + len(j.out_shapes)], ss[s0:s0 + len(j.sems)])

    def start(ins, os_, ss):
        for j, span in zip(jobs, spans):
            j.start(*part(j, span, ins, os_, ss))

    def finish(ins, os_, ss):
        for j, span in zip(jobs, spans):
            j.finish(*part(j, span, ins, os_, ss))

    return _Comm(arrays, outs, aliases, sems, start, finish)


def _call(body, *, name, grid, in_specs, out_specs, out_shape, args, scratch_shapes=(), sem=None, comm=None):
    multi = isinstance(out_shape, (list, tuple))
    o_shapes = list(out_shape) if multi else [out_shape]
    o_specs = list(out_specs) if multi else [out_specs]
    if comm is None:
        return pl.pallas_call(body, grid=grid, in_specs=in_specs, out_specs=out_specs, out_shape=out_shape,
                              scratch_shapes=list(scratch_shapes), name=name,
                              compiler_params=_cparams(*sem))(*args)
    n_in, n_out, n_scr = len(in_specs), len(o_shapes), len(scratch_shapes)
    n_ci, n_co = len(comm.arrays), len(comm.out_shapes)

    def carrier(*refs):
        ins, rest = refs[:n_in], refs[n_in:]
        cins, rest = rest[:n_ci], rest[n_ci:]
        outs, rest = rest[:n_out], rest[n_out:]
        couts, rest = rest[:n_co], rest[n_co:]
        scr, csems = rest[:n_scr], rest[n_scr:]
        ids = [pl.program_id(a) for a in range(len(grid))]
        first = functools.reduce(jnp.logical_and, [i == 0 for i in ids])
        last = functools.reduce(jnp.logical_and, [i == g - 1 for i, g in zip(ids, grid)])

        @pl.when(first)
        def _():
            comm.start(cins, couts, csems)

        body(*ins, *outs, *scr)

        @pl.when(last)
        def _():
            comm.finish(cins, couts, csems)

    res = pl.pallas_call(
        carrier, grid=grid, in_specs=list(in_specs) + [HBM_SPEC] * n_ci,
        out_specs=o_specs + [HBM_SPEC] * n_co, out_shape=o_shapes + comm.out_shapes,
        input_output_aliases={n_in + i: n_out + o for i, o in comm.aliases.items()},
        scratch_shapes=list(scratch_shapes) + comm.sems, name=name,
        compiler_params=_cparams(*(["arbitrary"] * len(grid))))(*args, *comm.arrays)
    mine = list(res[:n_out]) if multi else res[0]
    return mine, list(res[n_out:])


def run_comm(name, comm):
    def body(*refs):
        n_ci, n_co = len(comm.arrays), len(comm.out_shapes)
        cins, couts, csems = refs[:n_ci], refs[n_ci:n_ci + n_co], refs[n_ci + n_co:]
        comm.start(cins, couts, csems)
        comm.finish(cins, couts, csems)

    return pl.pallas_call(
        body, in_specs=[HBM_SPEC] * len(comm.arrays), out_specs=[HBM_SPEC] * len(comm.out_shapes),
        out_shape=comm.out_shapes, input_output_aliases=comm.aliases, scratch_shapes=comm.sems,
        name=name)(*comm.arrays)


def _mm(name, a, b, *, contract, grid, a_spec, b_spec, o_spec, out_shape, res=None, comm=None):
    nk = grid[2]

    def body(*refs):
        if res is not None:
            a_ref, b_ref, r_ref, o_ref = refs[:4]
        else:
            a_ref, b_ref, o_ref = refs[:3]
            r_ref = None
        p = _dot(a_ref[...].astype(BF16), b_ref[...].astype(BF16), contract)

        def finish(acc):
            if r_ref is not None:
                acc = acc + r_ref[...]
            o_ref[...] = acc.astype(o_ref.dtype)

        if nk == 1:
            finish(p)
        else:
            acc_ref = refs[-1]
            k = pl.program_id(2)

            @pl.when(k == 0)
            def _():
                acc_ref[...] = p

            @pl.when(k > 0)
            def _():
                acc_ref[...] += p

            @pl.when(k == nk - 1)
            def _():
                finish(acc_ref[...])

    in_specs = [a_spec, b_spec]
    args = [a, b]
    if res is not None:
        in_specs.append(o_spec)
        args.append(res)
    blk = tuple(d for d in o_spec.block_shape if d is not None)
    scratch = [] if nk == 1 else [pltpu.VMEM(blk, F32)]
    return _call(body, name=name, grid=grid, in_specs=in_specs, out_specs=o_spec, out_shape=out_shape,
                 args=args, scratch_shapes=scratch, sem=("parallel", "parallel", "arbitrary"), comm=comm)


def mm_nn_cols(name, a, w4, out_dtype, comm=None):
    M, K = a.shape
    _, _, C = w4.shape
    tm = _pick(M, (1024, 512, 256))
    tn = _pick(C, (512, 256, 128))
    nps = C // tn
    return _mm(name, a, w4, contract=NN, grid=(M // tm, 4 * nps, 1),
               a_spec=pl.BlockSpec((tm, K), lambda i, j, k: (i, 0)),
               b_spec=pl.BlockSpec((None, K, tn), lambda i, j, k: (j // nps, 0, j % nps)),
               o_spec=pl.BlockSpec((tm, tn), lambda i, j, k: (i, j)),
               out_shape=jax.ShapeDtypeStruct((M, 4 * C), out_dtype), comm=comm)


def mm_nn(name, a, w, out_dtype, res=None):
    M, K = a.shape
    N = w.shape[1]
    tm = _pick(M, (1024, 512, 256))
    tn = _pick(N, (512, 256, 128))
    return _mm(name, a, w, contract=NN, grid=(M // tm, N // tn, 1),
               a_spec=pl.BlockSpec((tm, K), lambda i, j, k: (i, 0)),
               b_spec=pl.BlockSpec((K, tn), lambda i, j, k: (0, j)),
               o_spec=pl.BlockSpec((tm, tn), lambda i, j, k: (i, j)),
               out_shape=jax.ShapeDtypeStruct((M, N), out_dtype), res=res)


def mm_nt_cols(name, a, w4, out_dtype, comm=None):
    M = a.shape[0]
    _, K, C = w4.shape
    tm = _pick(M, (1024, 512, 256))
    tn = _pick(K, (1024, 512, 256, 128))
    tk = _pick(C, (1792, 1536, 1024, 512, 256, 128))
    kps = C // tk
    return _mm(name, a, w4, contract=NT, grid=(M // tm, K // tn, 4 * kps),
               a_spec=pl.BlockSpec((tm, tk), lambda i, j, k: (i, k)),
               b_spec=pl.BlockSpec((None, tn, tk), lambda i, j, k: (k // kps, j, k % kps)),
               o_spec=pl.BlockSpec((tm, tn), lambda i, j, k: (i, j)),
               out_shape=jax.ShapeDtypeStruct((M, K), out_dtype), comm=comm)


def mm_nt(name, a, w, out_dtype):
    M, C = a.shape
    N = w.shape[0]
    tm = _pick(M, (1024, 512, 256))
    tn = _pick(N, (512, 256, 128))
    return _mm(name, a, w, contract=NT, grid=(M // tm, N // tn, 1),
               a_spec=pl.BlockSpec((tm, C), lambda i, j, k: (i, 0)),
               b_spec=pl.BlockSpec((tn, C), lambda i, j, k: (j, 0)),
               o_spec=pl.BlockSpec((tm, tn), lambda i, j, k: (i, j)),
               out_shape=jax.ShapeDtypeStruct((M, N), out_dtype))


def mm_tn_cols(name, a, b, comm=None):
    S, K = a.shape
    C = b.shape[1] // 4
    ts = _pick(S, (2048, 1024, 512, 256))
    tko = _pick(K, (1024, 512, 256, 128))
    tn = _pick(C, (1792, 1536, 1024, 512, 256, 128))
    nps = C // tn
    return _mm(name, a, b, contract=TN, grid=(K // tko, 4 * nps, S // ts),
               a_spec=pl.BlockSpec((ts, tko), lambda i, j, k: (k, i)),
               b_spec=pl.BlockSpec((ts, tn), lambda i, j, k: (k, j)),
               o_spec=pl.BlockSpec((None, tko, tn), lambda i, j, k: (j // nps, i, j % nps)),
               out_shape=jax.ShapeDtypeStruct((4, K, C), BF16), comm=comm)


def mm_tn(name, a, b):
    S, K = a.shape
    N = b.shape[1]
    ts = _pick(S, (1024, 512, 256))
    tko = _pick(K, (2048, 1024, 512, 256, 128))
    tn = _pick(N, (1024, 512, 256, 128))
    return _mm(name, a, b, contract=TN, grid=(K // tko, N // tn, S // ts),
               a_spec=pl.BlockSpec((ts, tko), lambda i, j, k: (k, i)),
               b_spec=pl.BlockSpec((ts, tn), lambda i, j, k: (k, j)),
               o_spec=pl.BlockSpec((tko, tn), lambda i, j, k: (i, j)),
               out_shape=jax.ShapeDtypeStruct((K, N), BF16))


def rms_fwd(name, x, g):
    S, D = x.shape
    T = _pick(S, (512, 256))

    def body(x_ref, g_ref, o_ref):
        xf = x_ref[...]
        r = lax.rsqrt(jnp.mean(xf * xf, axis=-1, keepdims=True) + EPS)
        o_ref[...] = (xf * r * g_ref[...]).astype(o_ref.dtype)

    return pl.pallas_call(
        body, grid=(S // T,),
        in_specs=[pl.BlockSpec((T, D), lambda i: (i, 0)), pl.BlockSpec((1, D), lambda i: (0, 0))],
        out_specs=pl.BlockSpec((T, D), lambda i: (i, 0)),
        out_shape=jax.ShapeDtypeStruct((S, D), BF16), name=name,
        compiler_params=_cparams("parallel"))(x, g.reshape(1, D))


def rms_bwd(name, x, g, dys, dres):
    S, D = x.shape
    T = _pick(S, (256,))
    ndy = len(dys)
    has_res = dres is not None

    def body(*refs):
        x_ref, g_ref = refs[0], refs[1]
        dy_refs = refs[2:2 + ndy]
        r_ref = refs[2 + ndy] if has_res else None
        dx_ref, dxb_ref, dg_ref = refs[-3], refs[-2], refs[-1]
        i = pl.program_id(0)
        xf = x_ref[...]
        r = lax.rsqrt(jnp.mean(xf * xf, axis=-1, keepdims=True) + EPS)
        xhat = xf * r
        dy = dy_refs[0][...].astype(F32)
        for d in dy_refs[1:]:
            dy = dy + d[...].astype(F32)
        dxhat = dy * g_ref[...]
        dx = r * (dxhat - xhat * jnp.mean(dxhat * xhat, axis=-1, keepdims=True))
        if has_res:
            dx = dx + r_ref[...]
        dx_ref[...] = dx
        dxb_ref[...] = dx.astype(dxb_ref.dtype)
        dg = jnp.sum(dy * xhat, axis=0, keepdims=True)

        @pl.when(i == 0)
        def _():
            dg_ref[...] = dg

        @pl.when(i > 0)
        def _():
            dg_ref[...] += dg

    row = pl.BlockSpec((T, D), lambda i: (i, 0))
    vec = pl.BlockSpec((1, D), lambda i: (0, 0))
    args = [x, g.reshape(1, D), *dys] + ([dres] if has_res else [])
    return pl.pallas_call(
        body, grid=(S // T,),
        in_specs=[row, vec] + [row] * (ndy + int(has_res)),
        out_specs=[row, row, vec],
        out_shape=[jax.ShapeDtypeStruct((S, D), F32), jax.ShapeDtypeStruct((S, D), BF16),
                   jax.ShapeDtypeStruct((1, D), F32)],
        name=name, compiler_params=_cparams("arbitrary"))(*args)


def loss_head(name, h, g, target):
    S, D = h.shape
    T = _pick(S, (256,))

    def body(h_ref, g_ref, t_ref, loss_ref, dg_ref, dh_ref, dhb_ref):
        i = pl.program_id(0)
        xf = h_ref[...]
        gv = g_ref[...]
        r = lax.rsqrt(jnp.mean(xf * xf, axis=-1, keepdims=True) + EPS)
        xhat = xf * r
        err = xhat * gv - t_ref[...]
        part = 0.5 * jnp.sum(jnp.sum(err * err, axis=-1, keepdims=True), axis=0, keepdims=True) / D
        dout = err / D
        dxhat = dout * gv
        dh = r * (dxhat - xhat * jnp.mean(dxhat * xhat, axis=-1, keepdims=True))
        dh_ref[...] = dh
        dhb_ref[...] = dh.astype(dhb_ref.dtype)
        dg = jnp.sum(dout * xhat, axis=0, keepdims=True)
        lrow = jnp.broadcast_to(part, (1, LANES))

        @pl.when(i == 0)
        def _():
            dg_ref[...] = dg
            loss_ref[...] = lrow

        @pl.when(i > 0)
        def _():
            dg_ref[...] += dg
            loss_ref[...] += lrow

    row = pl.BlockSpec((T, D), lambda i: (i, 0))
    vec = pl.BlockSpec((1, D), lambda i: (0, 0))
    return pl.pallas_call(
        body, grid=(S // T,), in_specs=[row, vec, row],
        out_specs=[pl.BlockSpec((1, LANES), lambda i: (0, 0)), vec, row, row],
        out_shape=[jax.ShapeDtypeStruct((1, LANES), F32), jax.ShapeDtypeStruct((1, D), F32),
                   jax.ShapeDtypeStruct((S, D), F32), jax.ShapeDtypeStruct((S, D), BF16)],
        name=name, compiler_params=_cparams("arbitrary"))(h, g.reshape(1, D), target)


def _attn_tq(S):
    return _pick(S, (512,))


def band_bias_table(rel_bias, tq):
    H = rel_bias.shape[0]
    w = 2 * tq
    nbits = int(np.log2(tq))
    assert (1 << nbits) == tq and (N_PAST_CHUNKS + 2) * CHUNK - 1 <= w
    c = np.arange(w)
    d0 = np.where(c <= tq + CHUNK - 1, tq - c, tq + w - c)
    base = jnp.take(rel_bias.astype(F32), jnp.asarray(np.clip(d0, -MAX_REL, MAX_REL) + MAX_REL), axis=1)

    def body(b_ref, o_ref):
        x = jnp.broadcast_to(b_ref[...], (tq, w))
        row = lax.broadcasted_iota(jnp.int32, (tq, w), 0)
        col = lax.broadcasted_iota(jnp.int32, (tq, w), 1)
        for b in range(nbits):
            x = jnp.where(((row >> b) & 1) == 1, pltpu.roll(x, 1 << b, 1), x)
        qc = row // CHUNK
        kc = col // CHUNK - tq // CHUNK
        o_ref[...] = jnp.where((kc <= qc) & (kc >= qc - N_PAST_CHUNKS), x, NEG)

    return pl.pallas_call(
        body, grid=(H,), in_specs=[pl.BlockSpec((None, 1, w), lambda h: (h, 0, 0))],
        out_specs=pl.BlockSpec((None, tq, w), lambda h: (h, 0, 0)),
        out_shape=jax.ShapeDtypeStruct((H, tq, w), F32), name="band_bias_table",
        compiler_params=_cparams("parallel"))(base.reshape(H, 1, w))


def attn_fwd(proj, bm, D, comm=None):
    S = proj.shape[0]
    H = D // HEAD_DIM_A
    tq = _attn_tq(S)
    nb = S // tq
    scale = HEAD_DIM_A ** -0.5

    def body(q_ref, kp_ref, kc_ref, vp_ref, vc_ref, bm_ref, o_ref, lse_ref):
        i = pl.program_id(1)
        q = q_ref[...]
        sp = _dot(q, kp_ref[...], NT) * scale + bm_ref[:, :tq]
        sp = jnp.where(i == 0, NEG, sp)
        sc = _dot(q, kc_ref[...], NT) * scale + bm_ref[:, tq:]
        m = jnp.maximum(jnp.max(sp, axis=-1, keepdims=True), jnp.max(sc, axis=-1, keepdims=True))
        pp = jnp.exp(sp - m)
        pc = jnp.exp(sc - m)
        l = jnp.sum(pp, axis=-1, keepdims=True) + jnp.sum(pc, axis=-1, keepdims=True)
        o = _dot(pp.astype(BF16), vp_ref[...], NN) + _dot(pc.astype(BF16), vc_ref[...], NN)
        o_ref[...] = (o / l).astype(o_ref.dtype)
        lse_ref[...] = m + jnp.log(l)

    def col(base):
        return (pl.BlockSpec((tq, HEAD_DIM_A), lambda h, i: (jnp.maximum(i - 1, 0), base + h)),
                pl.BlockSpec((tq, HEAD_DIM_A), lambda h, i: (i, base + h)))

    kp, kc = col(H)
    vp, vc = col(2 * H)
    return _call(
        body, name="attn_fwd", grid=(H, nb),
        in_specs=[pl.BlockSpec((tq, HEAD_DIM_A), lambda h, i: (i, h)), kp, kc, vp, vc,
                  pl.BlockSpec((None, tq, 2 * tq), lambda h, i: (h, 0, 0))],
        out_specs=[pl.BlockSpec((tq, HEAD_DIM_A), lambda h, i: (i, h)),
                   pl.BlockSpec((None, tq, 1), lambda h, i: (h, i, 0))],
        out_shape=[jax.ShapeDtypeStruct((S, D), BF16), jax.ShapeDtypeStruct((H, S, 1), F32)],
        args=[proj, proj, proj, proj, proj, bm], sem=("parallel", "arbitrary"), comm=comm)


def attn_bwd(proj, ya, dya, lse, bm, D, comm=None):
    S = proj.shape[0]
    H = D // HEAD_DIM_A
    tq = _attn_tq(S)
    nb = S // tq
    scale = HEAD_DIM_A ** -0.5

    def body(q_ref, kp_ref, kc_ref, vp_ref, vc_ref, o_ref, do_ref, lse_ref, bm_ref,
             dq_ref, dkc_ref, dkp_ref, dvc_ref, dvp_ref, ds_ref):
        i = pl.program_id(1)
        q = q_ref[...]
        do = do_ref[...]
        delta = jnp.sum(do.astype(F32) * o_ref[...].astype(F32), axis=-1, keepdims=True)
        lse_v = lse_ref[...]

        def half(k_ref, v_ref, bias, first):
            k = k_ref[...]
            s = _dot(q, k, NT) * scale + bias
            if first:
                s = jnp.where(i == 0, NEG, s)
            p = jnp.exp(s - lse_v)
            dv = _dot(p.astype(BF16), do, TN)
            dp = _dot(do, v_ref[...], NT)
            ds = p * (dp - delta)
            dsb = ds.astype(BF16)
            dq = _dot(dsb, k, NN)
            dk = _dot(dsb, q, TN) * scale
            return ds, dq, dk, dv

        dsp, dqp, dkp, dvp = half(kp_ref, vp_ref, bm_ref[:, :tq], True)
        dsc, dqc, dkc, dvc = half(kc_ref, vc_ref, bm_ref[:, tq:], False)
        dq_ref[...] = ((dqp + dqc) * scale).astype(dq_ref.dtype)
        dkp_ref[...] = dkp.astype(dkp_ref.dtype)
        dkc_ref[...] = dkc.astype(dkc_ref.dtype)
        dvp_ref[...] = dvp.astype(dvp_ref.dtype)
        dvc_ref[...] = dvc.astype(dvc_ref.dtype)

        @pl.when(i == 0)
        def _():
            ds_ref[:, :tq] = dsp
            ds_ref[:, tq:] = dsc

        @pl.when(i > 0)
        def _():
            ds_ref[:, :tq] += dsp
            ds_ref[:, tq:] += dsc

    def col(base):
        return (pl.BlockSpec((tq, HEAD_DIM_A), lambda h, i: (jnp.maximum(i - 1, 0), base + h)),
                pl.BlockSpec((tq, HEAD_DIM_A), lambda h, i: (i, base + h)))

    kp, kc = col(H)
    vp, vc = col(2 * H)
    blk = pl.BlockSpec((tq, HEAD_DIM_A), lambda h, i: (i, h))
    sd = jax.ShapeDtypeStruct((S, D), BF16)
    return _call(
        body, name="attn_bwd", grid=(H, nb),
        in_specs=[blk, kp, kc, vp, vc, blk, blk,
                  pl.BlockSpec((None, tq, 1), lambda h, i: (h, i, 0)),
                  pl.BlockSpec((None, tq, 2 * tq), lambda h, i: (h, 0, 0))],
        out_specs=[blk, blk, blk, blk, blk, pl.BlockSpec((None, tq, 2 * tq), lambda h, i: (h, 0, 0))],
        out_shape=[sd, sd, sd, sd, sd, jax.ShapeDtypeStruct((H, tq, 2 * tq), F32)],
        args=[proj, proj, proj, proj, proj, ya, dya, lse, bm], sem=("parallel", "arbitrary"), comm=comm)


def rel_bias_grad(ds_sum):
    H, tq, w = ds_sum.shape
    nbin = 2 * MAX_REL + 1
    nbin_pad = 3 * LANES
    d_lo, d_hi = -(CHUNK - 1), (N_PAST_CHUNKS + 1) * CHUNK - 1
    assert d_hi - d_lo + 1 <= w
    onehot = np.zeros((w, nbin_pad), np.float32)
    for d in range(d_lo, d_hi + 1):
        onehot[(tq - d) % w, int(np.clip(d, -MAX_REL, MAX_REL)) + MAX_REL] = 1.0
    nbits = int(np.log2(tq))
    assert (1 << nbits) == tq

    def body(ds_ref, m_ref, o_ref):
        x = ds_ref[...]
        row = lax.broadcasted_iota(jnp.int32, x.shape, 0)
        for b in range(nbits):
            rolled = pltpu.roll(x, w - (1 << b), 1)
            x = jnp.where(((row >> b) & 1) == 1, rolled, x)
        t = jnp.sum(x, axis=0, keepdims=True)
        o_ref[...] = lax.dot_general(t, m_ref[...], (NN, ((), ())), precision=lax.Precision.HIGHEST,
                                     preferred_element_type=F32)

    out = pl.pallas_call(
        body, grid=(H,),
        in_specs=[pl.BlockSpec((None, tq, w), lambda h: (h, 0, 0)),
                  pl.BlockSpec((w, nbin_pad), lambda h: (0, 0))],
        out_specs=pl.BlockSpec((None, 1, nbin_pad), lambda h: (h, 0, 0)),
        out_shape=jax.ShapeDtypeStruct((H, 1, nbin_pad), F32),
        name="rel_bias_grad", compiler_params=_cparams("parallel"))(ds_sum, jnp.asarray(onehot))
    return out[:, 0, :nbin]


def _conv_t(S):
    return _pick(S, (256,))


ROW_CHUNK = 16


def _row_loop(n_rows, step):
    def one(r, carry):
        step(pl.ds(pl.multiple_of(r * ROW_CHUNK, ROW_CHUNK), ROW_CHUNK))
        return carry

    lax.fori_loop(0, n_rows // ROW_CHUNK, one, 0)


def _fill_zbuf(zbuf, ap_ref, bp_ref, a_ref, b_ref, i):
    zp = ap_ref[...].astype(F32) * _sigmoid(bp_ref[...].astype(F32))
    zbuf[0:CONV_HALO, :] = jnp.where(i == 0, 0.0, zp)

    def step(rows):
        below = pl.ds(pl.multiple_of(rows.start + CONV_HALO, ROW_CHUNK), ROW_CHUNK)
        zbuf[below, :] = a_ref[rows, :].astype(F32) * _sigmoid(b_ref[rows, :].astype(F32))

    _row_loop(a_ref.shape[0], step)


def _shifted_windows(buf, shifted, lanes, T):
    rows = T + CONV_HALO - SUBLANES
    for b in range(1, SUBLANES):
        shifted[b - 1] = buf[pl.ds(b, rows), lanes]

    def window(off, r0=0, n=T):
        a, b = divmod(off, SUBLANES)
        if b == 0:
            return buf[pl.ds(r0 + off, n), lanes]
        return shifted[b - 1, pl.ds(r0 + a * SUBLANES, n), :]

    return window


def _shifted_scratch(T):
    return pltpu.VMEM((SUBLANES - 1, T + CONV_HALO - SUBLANES, LANES), F32)


def conv_gate_fwd(proj, ya, cw, cb, lng, lnb, D, comm=None):
    S = proj.shape[0]
    T = _conv_t(S)
    hb = T // CONV_HALO
    nlb = D // LANES

    def body(ap_ref, bp_ref, a_ref, b_ref, ga_ref, gb_ref, ya_ref, cw_ref, cb_ref, lng_ref, lnb_ref,
             y_ref, c_ref, zbuf, zsh):
        i = pl.program_id(0)
        _fill_zbuf(zbuf, ap_ref, bp_ref, a_ref, b_ref, i)

        def lane_block(lb, carry):
            lanes = pl.ds(pl.multiple_of(lb * LANES, LANES), LANES)
            z_at = _shifted_windows(zbuf, zsh, lanes, T)
            acc = jnp.zeros((T, LANES), F32)
            for k in range(CONV_WIDTH):
                acc = acc + cw_ref[k:k + 1, lanes] * z_at(CONV_HALO - CONV_WIDTH + 1 + k)
            c_ref[:, lanes] = acc + cb_ref[:, lanes]
            return carry

        lax.fori_loop(0, nlb, lane_block, 0)

        def norm_and_gate(rows):
            c = c_ref[rows, :]
            mu = jnp.mean(c, axis=-1, keepdims=True)
            xc = c - mu
            rstd = lax.rsqrt(jnp.mean(xc * xc, axis=-1, keepdims=True) + EPS)
            ln = xc * rstd * lng_ref[...] + lnb_ref[...]
            yb = ln * _sigmoid(ln)
            ga = ga_ref[rows, :].astype(F32)
            gb = gb_ref[rows, :].astype(F32)
            y_ref[rows, :D] = (ya_ref[rows, :].astype(F32) * (ga * _sigmoid(ga))).astype(y_ref.dtype)
            y_ref[rows, D:] = (yb * (gb * _sigmoid(gb))).astype(y_ref.dtype)

        _row_loop(T, norm_and_gate)

    def cur(cidx):
        return pl.BlockSpec((T, D), lambda i: (i, cidx))

    def prev(cidx):
        return pl.BlockSpec((CONV_HALO, D), lambda i: (jnp.maximum(i * hb - 1, 0), cidx))

    vec = pl.BlockSpec((1, D), lambda i: (0, 0))
    return _call(
        body, name="conv_gate_fwd", grid=(S // T,),
        in_specs=[prev(3), prev(4), cur(3), cur(4), cur(5), cur(6), pl.BlockSpec((T, D), lambda i: (i, 0)),
                  pl.BlockSpec((CONV_HALO, D), lambda i: (0, 0)), vec, vec, vec],
        out_specs=[pl.BlockSpec((T, 2 * D), lambda i: (i, 0)), pl.BlockSpec((T, D), lambda i: (i, 0))],
        out_shape=[jax.ShapeDtypeStruct((S, 2 * D), BF16), jax.ShapeDtypeStruct((S, D), F32)],
        scratch_shapes=[pltpu.VMEM((T + CONV_HALO, D), F32), _shifted_scratch(T)],
        args=[proj, proj, proj, proj, proj, proj, ya, cw, cb, lng, lnb], sem=("parallel",), comm=comm)


def conv_gate_bwd_a(dy0, proj, ya, cpre, lng, lnb, D):
    S = proj.shape[0]
    T = _conv_t(S)

    def body(dy_ref, ga_ref, gb_ref, ya_ref, c_ref, lng_ref, lnb_ref,
             dya_ref, dg_ref, dc_ref, dlng_ref, dlnb_ref):
        i = pl.program_id(0)

        @pl.when(i == 0)
        def _():
            dlng_ref[...] = jnp.zeros_like(dlng_ref)
            dlnb_ref[...] = jnp.zeros_like(dlnb_ref)

        def step(rows):
            c = c_ref[rows, :]
            gv = lng_ref[...]
            mu = jnp.mean(c, axis=-1, keepdims=True)
            xc = c - mu
            rstd = lax.rsqrt(jnp.mean(xc * xc, axis=-1, keepdims=True) + EPS)
            xhat = xc * rstd
            ln = xhat * gv + lnb_ref[...]
            sl = _sigmoid(ln)
            yb = ln * sl
            ga = ga_ref[rows, :].astype(F32)
            gb = gb_ref[rows, :].astype(F32)
            sa = _sigmoid(ga)
            sb = _sigmoid(gb)
            dy_a = dy_ref[rows, :D].astype(F32)
            dy_b = dy_ref[rows, D:].astype(F32)
            dya_ref[rows, :] = (dy_a * (ga * sa)).astype(dya_ref.dtype)
            dg_ref[rows, :D] = (dy_a * ya_ref[rows, :].astype(F32)
                                * (sa * (1.0 + ga * (1.0 - sa)))).astype(dg_ref.dtype)
            dg_ref[rows, D:] = (dy_b * yb * (sb * (1.0 + gb * (1.0 - sb)))).astype(dg_ref.dtype)
            dln = dy_b * (gb * sb) * (sl * (1.0 + ln * (1.0 - sl)))
            dxhat = dln * gv
            dc_ref[rows, :] = rstd * (dxhat - jnp.mean(dxhat, axis=-1, keepdims=True)
                                      - xhat * jnp.mean(dxhat * xhat, axis=-1, keepdims=True))
            dlng_ref[...] += jnp.sum(dln * xhat, axis=0, keepdims=True)
            dlnb_ref[...] += jnp.sum(dln, axis=0, keepdims=True)

        _row_loop(T, step)

    row = pl.BlockSpec((T, D), lambda i: (i, 0))
    vec = pl.BlockSpec((1, D), lambda i: (0, 0))
    return pl.pallas_call(
        body, grid=(S // T,),
        in_specs=[pl.BlockSpec((T, 2 * D), lambda i: (i, 0)),
                  pl.BlockSpec((T, D), lambda i: (i, 5)), pl.BlockSpec((T, D), lambda i: (i, 6)),
                  row, row, vec, vec],
        out_specs=[row, pl.BlockSpec((T, 2 * D), lambda i: (i, 0)), row, vec, vec],
        out_shape=[jax.ShapeDtypeStruct((S, D), BF16), jax.ShapeDtypeStruct((S, 2 * D), BF16),
                   jax.ShapeDtypeStruct((S, D), F32), jax.ShapeDtypeStruct((1, D), F32),
                   jax.ShapeDtypeStruct((1, D), F32)],
        name="conv_gate_bwd_a", compiler_params=_cparams("arbitrary"))(
            dy0, proj, proj, ya, cpre, lng, lnb)


def conv_gate_bwd_b(dc, proj, cw, D, comm=None):
    S = proj.shape[0]
    T = _conv_t(S)
    hb = T // CONV_HALO
    nt = S // T
    nlb = D // LANES
    half = T // 2

    def body(dc_ref, dn_ref, ap_ref, bp_ref, a_ref, b_ref, cw_ref, da_ref, db_ref, dcw_ref, dcb_ref,
             zbuf, dcbuf, zsh, dcsh, dcw8):
        i = pl.program_id(0)
        _fill_zbuf(zbuf, ap_ref, bp_ref, a_ref, b_ref, i)
        dcv = dc_ref[...]
        dcbuf[0:T, :] = dcv
        dcbuf[T:, :] = jnp.where(i == nt - 1, 0.0, dn_ref[...])

        @pl.when(i == 0)
        def _():
            dcw8[...] = jnp.zeros_like(dcw8)
            dcb_ref[...] = jnp.zeros_like(dcb_ref)

        dcb_ref[...] += jnp.sum(dcv, axis=0, keepdims=True)

        def lane_block(lb, carry):
            lanes = pl.ds(pl.multiple_of(lb * LANES, LANES), LANES)
            z_at = _shifted_windows(zbuf, zsh, lanes, T)
            dc_at = _shifted_windows(dcbuf, dcsh, lanes, T)
            for r0 in range(0, T, half):
                d0 = dcbuf[r0:r0 + half, lanes]
                dz = jnp.zeros((half, LANES), F32)
                for k in range(CONV_WIDTH):
                    dz = dz + cw_ref[k:k + 1, lanes] * dc_at(CONV_WIDTH - 1 - k, r0, half)
                    prod = d0 * z_at(CONV_HALO - CONV_WIDTH + 1 + k, r0, half)
                    dcw8[pl.ds(k * SUBLANES, SUBLANES), lanes] += jnp.sum(
                        prod.reshape(half // SUBLANES, SUBLANES, LANES), axis=0)
                av = a_ref[r0:r0 + half, lanes].astype(F32)
                sg = _sigmoid(b_ref[r0:r0 + half, lanes].astype(F32))
                da_ref[r0:r0 + half, lanes] = (dz * sg).astype(da_ref.dtype)
                db_ref[r0:r0 + half, lanes] = (dz * av * sg * (1.0 - sg)).astype(db_ref.dtype)
            return carry

        lax.fori_loop(0, nlb, lane_block, 0)

        @pl.when(i == nt - 1)
        def _():
            dcw_ref[...] = jnp.sum(dcw8[...].reshape(CONV_HALO, SUBLANES, D), axis=1)

    def cur(cidx):
        return pl.BlockSpec((T, D), lambda i: (i, cidx))

    def prev(cidx):
        return pl.BlockSpec((CONV_HALO, D), lambda i: (jnp.maximum(i * hb - 1, 0), cidx))

    row = pl.BlockSpec((T, D), lambda i: (i, 0))
    nxt = pl.BlockSpec((CONV_HALO, D), lambda i: (jnp.minimum((i + 1) * hb, nt * hb - 1), 0))
    return _call(
        body, name="conv_gate_bwd_b", grid=(nt,),
        in_specs=[row, nxt, prev(3), prev(4), cur(3), cur(4), pl.BlockSpec((CONV_HALO, D), lambda i: (0, 0))],
        out_specs=[row, row, pl.BlockSpec((CONV_HALO, D), lambda i: (0, 0)),
                   pl.BlockSpec((1, D), lambda i: (0, 0))],
        out_shape=[jax.ShapeDtypeStruct((S, D), BF16), jax.ShapeDtypeStruct((S, D), BF16),
                   jax.ShapeDtypeStruct((CONV_HALO, D), F32), jax.ShapeDtypeStruct((1, D), F32)],
        scratch_shapes=[pltpu.VMEM((T + CONV_HALO, D), F32), pltpu.VMEM((T + CONV_HALO, D), F32),
                        _shifted_scratch(T), _shifted_scratch(T), pltpu.VMEM((CONV_HALO * SUBLANES, D), F32)],
        args=[dc, dc, proj, proj, proj, proj, cw], sem=("arbitrary",), comm=comm)


def assemble_dproj0(dq, dkc, dkp, dvc, dvp, da, db, dgate, D):
    S = dq.shape[0]
    tq = _attn_tq(S)
    T = _pick(S, (256,))
    shift = tq // T
    nt = S // T

    def body(dq_ref, dkc_ref, dkp_ref, dvc_ref, dvp_ref, da_ref, db_ref, dg_ref, o_ref):
        i = pl.program_id(0)
        last = i + shift >= nt
        o_ref[:, 0:D] = dq_ref[...]
        dk = dkc_ref[...].astype(F32) + jnp.where(last, 0.0, dkp_ref[...].astype(F32))
        dv = dvc_ref[...].astype(F32) + jnp.where(last, 0.0, dvp_ref[...].astype(F32))
        o_ref[:, D:2 * D] = dk.astype(o_ref.dtype)
        o_ref[:, 2 * D:3 * D] = dv.astype(o_ref.dtype)
        o_ref[:, 3 * D:4 * D] = da_ref[...]
        o_ref[:, 4 * D:5 * D] = db_ref[...]
        o_ref[:, 5 * D:] = dg_ref[...]

    row = pl.BlockSpec((T, D), lambda i: (i, 0))
    nxt = pl.BlockSpec((T, D), lambda i: (jnp.minimum(i + shift, nt - 1), 0))
    return pl.pallas_call(
        body, grid=(nt,),
        in_specs=[row, row, nxt, row, nxt, row, row, pl.BlockSpec((T, 2 * D), lambda i: (i, 0))],
        out_specs=pl.BlockSpec((T, 7 * D), lambda i: (i, 0)),
        out_shape=jax.ShapeDtypeStruct((S, 7 * D), BF16),
        name="assemble_dproj0", compiler_params=_cparams("parallel"))(dq, dkc, dkp, dvc, dvp, da, db, dgate)


def _sgu_t(S):
    return _pick(S, (256, 128))


def _ws_masked(ws_ref, g):
    row = lax.broadcasted_iota(jnp.int32, (GMLP_CHUNK, GMLP_CHUNK), 0) // CHUNK
    col = lax.broadcasted_iota(jnp.int32, (GMLP_CHUNK, GMLP_CHUNK), 1) // CHUNK
    return jnp.where(row >= col, ws_ref[g], 0.0), row >= col


def sgu_fwd(proj, lng, lnb, ws, bst, MIX):
    S = proj.shape[0]
    T = _sgu_t(S)
    gw = MIX // N_GROUPS_C

    def body(u_ref, v_ref, g_ref, lng_ref, lnb_ref, ws_ref, bst_ref, y_ref):
        v = v_ref[...].astype(F32)
        mu = jnp.mean(v, axis=-1, keepdims=True)
        xc = v - mu
        rstd = lax.rsqrt(jnp.mean(xc * xc, axis=-1, keepdims=True) + EPS)
        for g in range(N_GROUPS_C):
            cols = slice(g * gw, (g + 1) * gw)
            wsm = _ws_masked(ws_ref, g)[0].astype(BF16)
            vn = (xc[:, cols] * rstd * lng_ref[:, cols] + lnb_ref[:, cols]).astype(BF16)
            for blk in range(T // GMLP_CHUNK):
                rows = slice(blk * GMLP_CHUNK, (blk + 1) * GMLP_CHUNK)
                sg = _dot(wsm, vn[rows], NN) + bst_ref[:, g:g + 1]
                gate = g_ref[rows, cols].astype(F32)
                y = u_ref[rows, cols].astype(F32) * sg * (gate * _sigmoid(gate))
                y_ref[rows, cols] = y.astype(y_ref.dtype)

    def part(cidx):
        return pl.BlockSpec((T, MIX), lambda i: (i, cidx))

    vec = pl.BlockSpec((1, MIX), lambda i: (0, 0))
    return pl.pallas_call(
        body, grid=(S // T,),
        in_specs=[part(0), part(1), part(2), vec, vec,
                  pl.BlockSpec((N_GROUPS_C, GMLP_CHUNK, GMLP_CHUNK), lambda i: (0, 0, 0)),
                  pl.BlockSpec((GMLP_CHUNK, N_GROUPS_C), lambda i: (0, 0))],
        out_specs=pl.BlockSpec((T, MIX), lambda i: (i, 0)),
        out_shape=jax.ShapeDtypeStruct((S, MIX), BF16),
        name="sgu_fwd", compiler_params=_cparams("parallel"))(proj, proj, proj, lng, lnb, ws, bst)


def sgu_bwd(dy1, proj, lng, lnb, ws, bst, MIX):
    S = proj.shape[0]
    T = _sgu_t(S)
    gw = MIX // N_GROUPS_C

    def body(dy_ref, u_ref, v_ref, g_ref, lng_ref, lnb_ref, ws_ref, bst_ref,
             dp_ref, dws_ref, dbst_ref, dlng_ref, dlnb_ref, dvn_buf):
        i = pl.program_id(0)

        @pl.when(i == 0)
        def _():
            dws_ref[...] = jnp.zeros_like(dws_ref)
            dbst_ref[...] = jnp.zeros_like(dbst_ref)
            dlng_ref[...] = jnp.zeros_like(dlng_ref)
            dlnb_ref[...] = jnp.zeros_like(dlnb_ref)

        v = v_ref[...].astype(F32)
        mu = jnp.mean(v, axis=-1, keepdims=True)
        xc = v - mu
        rstd = lax.rsqrt(jnp.mean(xc * xc, axis=-1, keepdims=True) + EPS)
        for g in range(N_GROUPS_C):
            cols = slice(g * gw, (g + 1) * gw)
            wsf, keep = _ws_masked(ws_ref, g)
            wsm = wsf.astype(BF16)
            vn = (xc[:, cols] * rstd * lng_ref[:, cols] + lnb_ref[:, cols]).astype(BF16)
            for blk in range(T // GMLP_CHUNK):
                rows = slice(blk * GMLP_CHUNK, (blk + 1) * GMLP_CHUNK)
                vnb = vn[rows]
                sg = _dot(wsm, vnb, NN) + bst_ref[:, g:g + 1]
                gate = g_ref[rows, cols].astype(F32)
                sig = _sigmoid(gate)
                sil = gate * sig
                u = u_ref[rows, cols].astype(F32)
                dy = dy_ref[rows, cols].astype(F32)
                dp_ref[rows, g * gw:(g + 1) * gw] = (dy * sg * sil).astype(dp_ref.dtype)
                dp_ref[rows, 2 * MIX + g * gw:2 * MIX + (g + 1) * gw] = (
                    dy * u * sg * (sig * (1.0 + gate * (1.0 - sig)))).astype(dp_ref.dtype)
                dsg = dy * u * sil
                dsgb = dsg.astype(BF16)
                dvn_buf[rows, cols] = _dot(wsm, dsgb, TN)
                dws_ref[g] += jnp.where(keep, _dot(dsgb, vnb, NT), 0.0)
                dbst_ref[:, g:g + 1] += jnp.sum(dsg, axis=-1, keepdims=True)
        dvn = dvn_buf[...]
        xhat = xc * rstd
        dxhat = dvn * lng_ref[...]
        dv = rstd * (dxhat - jnp.mean(dxhat, axis=-1, keepdims=True)
                     - xhat * jnp.mean(dxhat * xhat, axis=-1, keepdims=True))
        dp_ref[:, MIX:2 * MIX] = dv.astype(dp_ref.dtype)
        dlng_ref[...] += jnp.sum(dvn * xhat, axis=0, keepdims=True)
        dlnb_ref[...] += jnp.sum(dvn, axis=0, keepdims=True)

    def part(cidx):
        return pl.BlockSpec((T, MIX), lambda i: (i, cidx))

    vec = pl.BlockSpec((1, MIX), lambda i: (0, 0))
    wspec = pl.BlockSpec((N_GROUPS_C, GMLP_CHUNK, GMLP_CHUNK), lambda i: (0, 0, 0))
    bspec = pl.BlockSpec((GMLP_CHUNK, N_GROUPS_C), lambda i: (0, 0))
    return pl.pallas_call(
        body, grid=(S // T,),
        in_specs=[pl.BlockSpec((T, MIX), lambda i: (i, 0)), part(0), part(1), part(2), vec, vec, wspec, bspec],
        out_specs=[pl.BlockSpec((T, 3 * MIX), lambda i: (i, 0)), wspec, bspec, vec, vec],
        out_shape=[jax.ShapeDtypeStruct((S, 3 * MIX), BF16),
                   jax.ShapeDtypeStruct((N_GROUPS_C, GMLP_CHUNK, GMLP_CHUNK), F32),
                   jax.ShapeDtypeStruct((GMLP_CHUNK, N_GROUPS_C), F32),
                   jax.ShapeDtypeStruct((1, MIX), F32), jax.ShapeDtypeStruct((1, MIX), F32)],
        scratch_shapes=[pltpu.VMEM((T, MIX), F32)],
        name="sgu_bwd", compiler_params=_cparams("arbitrary"))(dy1, proj, proj, proj, lng, lnb, ws, bst)


def xattn_fwd(name, q, k, v):
    S, D = q.shape
    nm = k.shape[0]
    dh = D // N_HEADS_X
    tq = _pick(S, (512, 256))
    scale = dh ** -0.5

    def body(q_ref, k_ref, v_ref, o_ref, lse_ref):
        s = _dot(q_ref[...], k_ref[...], NT) * scale
        m = jnp.max(s, axis=-1, keepdims=True)
        p = jnp.exp(s - m)
        l = jnp.sum(p, axis=-1, keepdims=True)
        o_ref[...] = (_dot(p.astype(BF16), v_ref[...], NN) / l).astype(o_ref.dtype)
        lse_ref[...] = m + jnp.log(l)

    return pl.pallas_call(
        body, grid=(N_HEADS_X, S // tq),
        in_specs=[pl.BlockSpec((tq, dh), lambda h, i: (i, h)),
                  pl.BlockSpec((nm, dh), lambda h, i: (0, h)), pl.BlockSpec((nm, dh), lambda h, i: (0, h))],
        out_specs=[pl.BlockSpec((tq, dh), lambda h, i: (i, h)),
                   pl.BlockSpec((None, tq, 1), lambda h, i: (h, i, 0))],
        out_shape=[jax.ShapeDtypeStruct((S, D), BF16), jax.ShapeDtypeStruct((N_HEADS_X, S, 1), F32)],
        name=name, compiler_params=_cparams("parallel", "parallel"))(q, k, v)


def xattn_bwd(name, q, k, v, o, do, lse):
    S, D = q.shape
    nm = k.shape[0]
    dh = D // N_HEADS_X
    tq = _pick(S, (512, 256))
    scale = dh ** -0.5

    def body(q_ref, k_ref, v_ref, o_ref, do_ref, lse_ref, dq_ref, dk_ref, dv_ref):
        i = pl.program_id(1)
        q_v = q_ref[...]
        k_v = k_ref[...]
        do_v = do_ref[...]
        p = jnp.exp(_dot(q_v, k_v, NT) * scale - lse_ref[...])
        delta = jnp.sum(do_v.astype(F32) * o_ref[...].astype(F32), axis=-1, keepdims=True)
        dv = _dot(p.astype(BF16), do_v, TN)
        ds = (p * (_dot(do_v, v_ref[...], NT) - delta)).astype(BF16)
        dq_ref[...] = (_dot(ds, k_v, NN) * scale).astype(dq_ref.dtype)
        dk = _dot(ds, q_v, TN) * scale

        @pl.when(i == 0)
        def _():
            dk_ref[...] = dk
            dv_ref[...] = dv

        @pl.when(i > 0)
        def _():
            dk_ref[...] += dk
            dv_ref[...] += dv

    qs = pl.BlockSpec((tq, dh), lambda h, i: (i, h))
    ks = pl.BlockSpec((nm, dh), lambda h, i: (0, h))
    return pl.pallas_call(
        body, grid=(N_HEADS_X, S // tq),
        in_specs=[qs, ks, ks, qs, qs, pl.BlockSpec((None, tq, 1), lambda h, i: (h, i, 0))],
        out_specs=[qs, ks, ks],
        out_shape=[jax.ShapeDtypeStruct((S, D), BF16), jax.ShapeDtypeStruct((nm, D), F32),
                   jax.ShapeDtypeStruct((nm, D), F32)],
        name=name, compiler_params=_cparams("parallel", "arbitrary"))(q, k, v, o, do, lse)


def adamw(name, w, g, m, v):
    R, C = w.shape
    tr = _pick(R, tuple(t for t in (512, 256, 128, 64, 32, 16, 8) if t * C * 4 <= (1 << 20)) or (8,))
    c1 = 1.0 - ADAM_B1 ** ADAM_STEP
    c2 = 1.0 - ADAM_B2 ** ADAM_STEP

    def body(w_ref, g_ref, m_ref, v_ref, d_ref, nm_ref, nv_ref):
        gv = g_ref[...]
        nm = ADAM_B1 * m_ref[...] + (1.0 - ADAM_B1) * gv
        nv = ADAM_B2 * v_ref[...] + (1.0 - ADAM_B2) * (gv * gv)
        d_ref[...] = -ADAM_LR * ((nm / c1) / (jnp.sqrt(nv / c2) + ADAM_EPS) + ADAM_WD * w_ref[...])
        nm_ref[...] = nm
        nv_ref[...] = nv

    blk = pl.BlockSpec((tr, C), lambda i: (i, 0))
    sd = jax.ShapeDtypeStruct((R, C), F32)
    return pl.pallas_call(body, grid=(R // tr,), in_specs=[blk] * 4, out_specs=[blk] * 3,
                          out_shape=[sd, sd, sd], name=name, compiler_params=_cparams("parallel"))(w, g, m, v)


def add_halves(name, g4, recv, cidx):
    _, R, C = g4.shape
    rh = R // 2
    tr = _pick(rh, (256, 128, 64, 32, 16))
    nrb = rh // tr

    def body(c_ref, a_ref, b_ref, o_ref):
        o_ref[...] = (a_ref[...].astype(F32) + b_ref[...].astype(F32)).astype(o_ref.dtype)

    grid_spec = pltpu.PrefetchScalarGridSpec(
        num_scalar_prefetch=1, grid=(4, nrb),
        in_specs=[pl.BlockSpec((None, tr, C), lambda j, r, c_ref: (j, c_ref[0] * nrb + r, 0)),
                  pl.BlockSpec((None, tr, C), lambda j, r, c_ref: (j, r, 0))],
        out_specs=pl.BlockSpec((None, tr, C), lambda j, r, c_ref: (j, r, 0)))
    return pl.pallas_call(body, grid_spec=grid_spec, out_shape=jax.ShapeDtypeStruct((4, rh, C), BF16),
                          name=name, compiler_params=_cparams("parallel", "parallel"))(cidx, g4, recv)


def sum_chips(name, own, recv, place):
    _, rh, C = own.shape
    tr = _pick(rh, (256, 128, 64, 32, 16))
    nrb = rh // tr

    def body(s_ref, own_ref, recv_ref, o_ref):
        acc = own_ref[...].astype(F32)
        for k in range(N_CHIPS - 1):
            acc = acc + recv_ref[k].astype(F32)
        o_ref[...] = acc

    grid_spec = pltpu.PrefetchScalarGridSpec(
        num_scalar_prefetch=1, grid=(nrb,),
        in_specs=[pl.BlockSpec((None, tr, C), lambda r, s: (s[0], r, 0)),
                  pl.BlockSpec((N_CHIPS - 1, tr, C), lambda r, s: (0, r, 0))],
        out_specs=pl.BlockSpec((tr, C), lambda r, s: (s[1] * nrb + r, 0)))
    return pl.pallas_call(body, grid_spec=grid_spec, out_shape=jax.ShapeDtypeStruct((2 * rh, C), F32),
                          name=name, compiler_params=_cparams("parallel"))(place, own, recv)


def cast_into_slot(name, w, place):
    R, C = w.shape
    tr = _pick(R, (256, 128, 64, 32, 16))

    def body(s_ref, w_ref, o_ref):
        o_ref[...] = w_ref[...].astype(o_ref.dtype)

    grid_spec = pltpu.PrefetchScalarGridSpec(
        num_scalar_prefetch=1, grid=(R // tr,),
        in_specs=[pl.BlockSpec((tr, C), lambda r, s: (r, 0))],
        out_specs=pl.BlockSpec((None, tr, C), lambda r, s: (s[0], r, 0)))
    return pl.pallas_call(body, grid_spec=grid_spec, out_shape=jax.ShapeDtypeStruct((N_CHIPS, R, C), BF16),
                          name=name, compiler_params=_cparams("parallel"))(place, w)


def _place():
    return lax.axis_index("x"), lax.axis_index("y"), lax.axis_index("c")


_CHIP_FLIPS = ((1, 0), (0, 1), (1, 1))


def _flip(v, bit):
    return 1 - v if bit else v


HBM_SPEC = pl.BlockSpec(memory_space=pl.ANY)


def exchange_small(name, buf, reduce):
    R = buf.shape[0]

    def body(x_ref, *refs):
        if reduce:
            sum_ref, all_ref, send_sems, recv_sems, local_sem = refs
        else:
            all_ref, send_sems, recv_sems, local_sem = refs
        x, y, c = _place()
        me = 4 * x + 2 * y + c
        mine = pltpu.make_async_copy(x_ref, all_ref.at[me], local_sem)
        mine.start()
        sends = []
        for k in range(1, N_DEV):
            peer = (_flip(x, k & 4), _flip(y, k & 2), _flip(c, k & 1))
            cp = pltpu.make_async_remote_copy(src_ref=x_ref, dst_ref=all_ref.at[me], send_sem=send_sems.at[k - 1],
                                              recv_sem=recv_sems.at[k - 1], device_id=peer, device_id_type=MESH)
            cp.start()
            sends.append(cp)
        for k in range(1, N_DEV):
            peer = (_flip(x, k & 4), _flip(y, k & 2), _flip(c, k & 1))
            src = 4 * peer[0] + 2 * peer[1] + peer[2]
            pltpu.make_async_remote_copy(src_ref=x_ref, dst_ref=all_ref.at[src], send_sem=send_sems.at[k - 1],
                                         recv_sem=recv_sems.at[k - 1], device_id=peer,
                                         device_id_type=MESH).wait_recv()
        for cp in sends:
            cp.wait_send()
        mine.wait()
        if reduce:
            acc = all_ref[0]
            for d in range(1, N_DEV):
                acc = acc + all_ref[d]
            sum_ref[...] = acc

    vm = pl.BlockSpec(memory_space=pltpu.VMEM)
    sems = [pltpu.SemaphoreType.DMA((N_DEV - 1,)), pltpu.SemaphoreType.DMA((N_DEV - 1,)), pltpu.SemaphoreType.DMA]
    if reduce:
        return pl.pallas_call(
            body, in_specs=[vm], out_specs=vm, out_shape=jax.ShapeDtypeStruct((R, LANES), F32),
            scratch_shapes=[pltpu.VMEM((N_DEV, R, LANES), F32)] + sems, name=name,
            compiler_params=pltpu.CompilerParams(vmem_limit_bytes=V7X_VMEM_LIMIT))(buf)
    return pl.pallas_call(
        body, in_specs=[vm], out_specs=vm, out_shape=jax.ShapeDtypeStruct((N_DEV, R, LANES), F32),
        scratch_shapes=sems, name=name,
        compiler_params=pltpu.CompilerParams(vmem_limit_bytes=V7X_VMEM_LIMIT))(buf)


def gather_job(slots):
    n = len(slots)

    def copies(o_refs, send_sems, recv_sems):
        x, y, c = _place()
        me = 2 * x + y
        sib = (x, y, 1 - c)
        chips = [(_flip(x, fx), _flip(y, fy)) for fx, fy in _CHIP_FLIPS]
        ici, fwd, from_sib = [], [], []
        for t in range(n):
            rh = o_refs[t].shape[1] // 2
            mine, theirs = pl.ds(c * rh, rh), pl.ds((1 - c) * rh, rh)
            for k, (px, py) in enumerate(chips):
                own = o_refs[t].at[me, mine]
                ici.append(pltpu.make_async_remote_copy(
                    src_ref=own, dst_ref=own, send_sem=send_sems.at[t, k], recv_sem=recv_sems.at[t, k],
                    device_id=(px, py, c), device_id_type=MESH))
                landed = o_refs[t].at[2 * px + py, mine]
                arrival = pltpu.make_async_remote_copy(
                    src_ref=landed, dst_ref=landed, send_sem=send_sems.at[t, k], recv_sem=recv_sems.at[t, k],
                    device_id=(px, py, c), device_id_type=MESH)
                fwd.append((arrival, pltpu.make_async_remote_copy(
                    src_ref=landed, dst_ref=landed, send_sem=send_sems.at[t, 3 + k],
                    recv_sem=recv_sems.at[t, 3 + k], device_id=sib, device_id_type=MESH)))
                passed = o_refs[t].at[2 * px + py, theirs]
                from_sib.append(pltpu.make_async_remote_copy(
                    src_ref=passed, dst_ref=passed, send_sem=send_sems.at[t, 3 + k],
                    recv_sem=recv_sems.at[t, 3 + k], device_id=sib, device_id_type=MESH))
        return ici, fwd, from_sib

    def start(ins, o_refs, sems):
        for cp in copies(o_refs, *sems)[0]:
            cp.start()

    def finish(ins, o_refs, sems):
        ici, fwd, from_sib = copies(o_refs, *sems)
        for arrival, forward in fwd:
            arrival.wait_recv()
            forward.start()
        for cp in from_sib:
            cp.wait_recv()
        for cp in ici:
            cp.wait_send()
        for _, forward in fwd:
            forward.wait_send()

    return _Comm(slots, [jax.ShapeDtypeStruct(s.shape, s.dtype) for s in slots], {t: t for t in range(n)},
                 [pltpu.SemaphoreType.DMA((n, 6)), pltpu.SemaphoreType.DMA((n, 6))], start, finish)


def sibling_halves_job(grads):
    n = len(grads)

    def copies(g_refs, o_refs, send_sems, recv_sems):
        x, y, c = _place()
        out = []
        for t in range(n):
            rh = g_refs[t].shape[1] // 2
            out.append(pltpu.make_async_remote_copy(
                src_ref=g_refs[t].at[:, pl.ds((1 - c) * rh, rh), :], dst_ref=o_refs[t],
                send_sem=send_sems.at[t], recv_sem=recv_sems.at[t], device_id=(x, y, 1 - c),
                device_id_type=MESH))
        return out

    def start(g_refs, o_refs, sems):
        for cp in copies(g_refs, o_refs, *sems):
            cp.start()

    def finish(g_refs, o_refs, sems):
        cps = copies(g_refs, o_refs, *sems)
        for cp in cps:
            cp.wait_recv()
        for cp in cps:
            cp.wait_send()

    return _Comm(grads, [jax.ShapeDtypeStruct((4, g.shape[1] // 2, g.shape[2]), g.dtype) for g in grads], {},
                 [pltpu.SemaphoreType.DMA((n,)), pltpu.SemaphoreType.DMA((n,))], start, finish)


def scatter_job(parts):
    n = len(parts)

    def copies(p_refs, o_refs, send_sems, recv_sems):
        x, y, c = _place()
        out = []
        for t in range(n):
            for k, (fx, fy) in enumerate(_CHIP_FLIPS):
                px, py = _flip(x, fx), _flip(y, fy)
                out.append(pltpu.make_async_remote_copy(
                    src_ref=p_refs[t].at[2 * px + py], dst_ref=o_refs[t].at[k],
                    send_sem=send_sems.at[t, k], recv_sem=recv_sems.at[t, k],
                    device_id=(px, py, c), device_id_type=MESH))
        return out

    def start(p_refs, o_refs, sems):
        for cp in copies(p_refs, o_refs, *sems):
            cp.start()

    def finish(p_refs, o_refs, sems):
        cps = copies(p_refs, o_refs, *sems)
        for cp in cps:
            cp.wait_recv()
        for cp in cps:
            cp.wait_send()

    return _Comm(parts, [jax.ShapeDtypeStruct((N_CHIPS - 1,) + p.shape[1:], p.dtype) for p in parts], {},
                 [pltpu.SemaphoreType.DMA((n, 3)), pltpu.SemaphoreType.DMA((n, 3))], start, finish)


def share_halves_job(halves):
    n = len(halves)

    def copies(o_refs, send_sems, recv_sems):
        x, y, c = _place()
        sends, arrivals = [], []
        for t in range(n):
            rh = o_refs[t].shape[0] // 2
            mine = o_refs[t].at[pl.ds(c * rh, rh)]
            theirs = o_refs[t].at[pl.ds((1 - c) * rh, rh)]
            sends.append(pltpu.make_async_remote_copy(
                src_ref=mine, dst_ref=mine, send_sem=send_sems.at[t], recv_sem=recv_sems.at[t],
                device_id=(x, y, 1 - c), device_id_type=MESH))
            arrivals.append(pltpu.make_async_remote_copy(
                src_ref=theirs, dst_ref=theirs, send_sem=send_sems.at[t], recv_sem=recv_sems.at[t],
                device_id=(x, y, 1 - c), device_id_type=MESH))
        return sends, arrivals

    def start(ins, o_refs, sems):
        for cp in copies(o_refs, *sems)[0]:
            cp.start()

    def finish(ins, o_refs, sems):
        sends, arrivals = copies(o_refs, *sems)
        for cp in arrivals:
            cp.wait_recv()
        for cp in sends:
            cp.wait_send()

    return _Comm(halves, [jax.ShapeDtypeStruct(h.shape, h.dtype) for h in halves], {t: t for t in range(n)},
                 [pltpu.SemaphoreType.DMA((n,)), pltpu.SemaphoreType.DMA((n,))], start, finish)


def _pack(arrs):
    flat = []
    for a in arrs:
        v = a.reshape(-1).astype(F32)
        pad = (-v.shape[0]) % (8 * LANES)
        flat.append(jnp.pad(v, (0, pad)))
    return jnp.concatenate(flat).reshape(-1, LANES)


def _unpack(buf, shapes):
    out, off = [], 0
    flat = buf.reshape(-1)
    for s in shapes:
        n = int(np.prod(s))
        out.append(flat[off:off + n].reshape(s))
        off += n + ((-n) % (8 * LANES))
    return out


def _xattn_layer_fwd(tag, h, mem, gx, gmem, w):
    hx = rms_fwd(f"rms_x{tag}", h, gx)
    memn = rms_fwd(f"rms_mem{tag}", mem, gmem)
    q = mm_nn(f"xq{tag}", hx, w["q"], BF16)
    k = mm_nn(f"xk{tag}", memn, w["k"], BF16)
    v = mm_nn(f"xv{tag}", memn, w["v"], BF16)
    o, lse = xattn_fwd(f"xattn_fwd{tag}", q, k, v)
    h_out = mm_nn(f"xo{tag}", o, w["o"], F32, res=h)
    return h_out, dict(hx=hx, memn=memn, q=q, k=k, v=v, o=o, lse=lse)


def _xattn_layer_bwd(tag, dh_out, dh_out_b, h_in, mem, gx, gmem, w, sv):
    do = mm_nt(f"d_xo{tag}", dh_out_b, w["o"], BF16)
    dwo = mm_tn(f"dw_xo{tag}", sv["o"], dh_out_b)
    dq, dk, dv = xattn_bwd(f"xattn_bwd{tag}", sv["q"], sv["k"], sv["v"], sv["o"], do, sv["lse"])
    dwq = mm_tn(f"dw_xq{tag}", sv["hx"], dq)
    dhx = mm_nt(f"d_xq{tag}", dq, w["q"], F32)
    dwk = mm_tn(f"dw_xk{tag}", sv["memn"], dk)
    dwv = mm_tn(f"dw_xv{tag}", sv["memn"], dv)
    dmk = mm_nt(f"d_xk{tag}", dk, w["k"], F32)
    dmv = mm_nt(f"d_xv{tag}", dv, w["v"], F32)
    dh_in, dh_in_b, dgx = rms_bwd(f"rms_x_bwd{tag}", h_in, gx, [dhx], dh_out)
    _, _, dgmem = rms_bwd(f"rms_mem_bwd{tag}", mem, gmem, [dmk, dmv], None)
    return dh_in, dh_in_b, dgx, dgmem, dict(q=dwq, k=dwk, v=dwv, o=dwo)


def kernel(x, mem, norm_mix_g, norm_x_g, norm_mem_g, final_norm_g, w_in_ab, rel_bias, conv_w, conv_b, conv_ln_g, conv_ln_b, w_out_ab, w_in_c, sgu_ln_g, sgu_ln_b, w_s, b_s, w_out_c, w_xq, w_xk, w_xv, w_xo, loss_target, m_norm_mix_g, m_norm_x_g, m_norm_mem_g, m_final_norm_g, m_w_in_ab, m_rel_bias, m_conv_w, m_conv_b, m_conv_ln_g, m_conv_ln_b, m_w_out_ab, m_w_in_c, m_sgu_ln_g, m_sgu_ln_b, m_w_s, m_b_s, m_w_out_c, m_w_xq, m_w_xk, m_w_xv, m_w_xo, v_norm_mix_g, v_norm_x_g, v_norm_mem_g, v_final_norm_g, v_w_in_ab, v_rel_bias, v_conv_w, v_conv_b, v_conv_ln_g, v_conv_ln_b, v_w_out_ab, v_w_in_c, v_sgu_ln_g, v_sgu_ln_b, v_w_s, v_b_s, v_w_out_c, v_w_xq, v_w_xk, v_w_xv, v_w_xo):
    S, D = x.shape[1], x.shape[2]
    MIX = 2 * D
    xs, mems, tgt = x[0], mem[0], loss_target[0]
    cx, cy, cc = _place()
    chip = 2 * cx + cy
    cidx = jnp.reshape(cc, (1,)).astype(jnp.int32)
    place = jnp.stack([chip, cc]).astype(jnp.int32)

    ro, rq = MIX // 4, D // 4
    row_sharded = [("out_ab", w_out_ab[0]), ("out_c", w_out_c[0])]
    for layer in range(2):
        for nm_, w in (("q", w_xq), ("k", w_xk), ("v", w_xv), ("o", w_xo)):
            row_sharded.append((f"x{nm_}{layer}", w[layer]))
    slots = {"in_ab": cast_into_slot("cast_in_ab", w_in_ab[0], place),
             "in_c": cast_into_slot("cast_in_c", w_in_c[0], place)}
    slots.update({nm_: cast_into_slot("cast_" + nm_, w, place) for nm_, w in row_sharded})

    small_sh = [conv_w[0], sgu_ln_g[0], sgu_ln_b[0]]
    gathered = exchange_small("gather_small", _pack(small_sh), reduce=False)
    per_chip = [_unpack(gathered[2 * j], [a.shape for a in small_sh]) for j in range(N_CHIPS)]
    conv_w_full = jnp.concatenate([p[0] for p in per_chip], axis=1)
    sgu_g_full = jnp.concatenate([p[1] for p in per_chip], axis=0).reshape(1, MIX)
    sgu_b_full = jnp.concatenate([p[2] for p in per_chip], axis=0).reshape(1, MIX)
    cw_pad = jnp.pad(conv_w_full, ((0, CONV_HALO - CONV_WIDTH), (0, 0)))
    cb = conv_b.reshape(1, D)
    clg, clb = conv_ln_g.reshape(1, D), conv_ln_b.reshape(1, D)
    ws = w_s[0]
    bst = jnp.transpose(b_s[0])
    tq = _attn_tq(S)
    bm = band_bias_table(rel_bias[0], tq)

    hn0 = rms_fwd("rms_mix0", xs, norm_mix_g[0])
    (wab4,) = run_comm("gather_in_ab", gather_job([slots["in_ab"]]))
    layer0 = ["out_ab", "xq0", "xk0", "xv0", "xo0"]
    layer1 = ["out_c", "xq1", "xk1", "xv1", "xo1"]
    proj0, got0 = mm_nn_cols("proj_ab", hn0, wab4, BF16, comm=gather_job([slots[n] for n in layer0]))
    (ya, lse_a), (wc4,) = attn_fwd(proj0, bm, D, comm=gather_job([slots["in_c"]]))
    (y0, cpre), got1 = conv_gate_fwd(proj0, ya, cw_pad, cb, clg, clb, D,
                                     comm=gather_job([slots[n] for n in layer1]))
    wrow = {n: g.reshape(-1, g.shape[2]) for n, g in zip(layer0 + layer1, got0 + got1)}
    wx = [{k: wrow[f"x{k}{layer}"] for k in "qkvo"} for layer in range(2)]
    h1 = mm_nn("out_ab", y0, wrow["out_ab"], F32, res=xs)
    h2, sx0 = _xattn_layer_fwd("0", h1, mems, norm_x_g[0], norm_mem_g[0], wx[0])
    hn1 = rms_fwd("rms_mix1", h2, norm_mix_g[1])
    proj1 = mm_nn_cols("proj_c", hn1, wc4, BF16)
    y1 = sgu_fwd(proj1, sgu_g_full, sgu_b_full, ws, bst, MIX)
    h3 = mm_nn("out_c", y1, wrow["out_c"], F32, res=h2)
    h4, sx1 = _xattn_layer_fwd("1", h3, mems, norm_x_g[1], norm_mem_g[1], wx[1])
    loss_row, dg_final, dh4, dh4b = loss_head("loss_head", h4, final_norm_g, tgt)

    dh3, dh3b, dgx1, dgmem1, dwx1 = _xattn_layer_bwd("1", dh4, dh4b, h3, mems, norm_x_g[1], norm_mem_g[1], wx[1], sx1)
    def stack_rows(dw_out, dwx):
        return jnp.concatenate([g.reshape(N_CHIPS, -1, g.shape[1]) for g in [dw_out] + [dwx[k] for k in "qkvo"]],
                               axis=1)

    dy1 = mm_nt("d_out_c", dh3b, wrow["out_c"], BF16)
    dw_out_c = mm_tn("dw_out_c", y1, dh3b)
    dproj1, dws, dbst, dsgu_g, dsgu_b = sgu_bwd(dy1, proj1, sgu_g_full, sgu_b_full, ws, bst, MIX)
    grp1 = stack_rows(dw_out_c, dwx1)
    dw_in_c, (sib1,) = mm_tn_cols("dw_in_c", hn1, dproj1, comm=sibling_halves_job([grp1]))
    part1 = add_halves("add_halves1", grp1, sib1, cidx)
    dhn1, (recv1, sib2) = mm_nt_cols("d_proj_c", dproj1, wc4, F32,
                                     comm=_join(scatter_job([part1]), sibling_halves_job([dw_in_c])))
    part2 = add_halves("add_halves2", dw_in_c, sib2, cidx)
    dh2, dh2b, dgmix1 = rms_bwd("rms_mix1_bwd", h2, norm_mix_g[1], [dhn1], dh3)
    dh1, dh1b, dgx0, dgmem0, dwx0 = _xattn_layer_bwd("0", dh2, dh2b, h1, mems, norm_x_g[0], norm_mem_g[0], wx[0], sx0)
    dy0 = mm_nt("d_out_ab", dh1b, wrow["out_ab"], BF16)
    dw_out_ab = mm_tn("dw_out_ab", y0, dh1b)
    grp3 = stack_rows(dw_out_ab, dwx0)
    dya, dgate, dc, dclg, dclb = conv_gate_bwd_a(dy0, proj0, ya, cpre, clg, clb, D)
    (da, db, dcw, dcb), (recv2, sib3) = conv_gate_bwd_b(
        dc, proj0, cw_pad, D, comm=_join(scatter_job([part2]), sibling_halves_job([grp3])))
    part3 = add_halves("add_halves3", grp3, sib3, cidx)
    (dq, dkc, dkp, dvc, dvp, ds_sum), (recv3,) = attn_bwd(proj0, ya, dya, lse_a, bm, D, comm=scatter_job([part3]))
    drel = rel_bias_grad(ds_sum)
    dproj0 = assemble_dproj0(dq, dkc, dkp, dvc, dvp, da, db, dgate, D)
    dw_in_ab = mm_tn_cols("dw_in_ab", hn0, dproj0)
    (sib4,) = run_comm("sibling_halves4", sibling_halves_job([dw_in_ab]))
    part4 = add_halves("add_halves4", dw_in_ab, sib4, cidx)
    dhn0, (recv4,) = mm_nt_cols("d_proj_ab", dproj0, wab4, F32, comm=scatter_job([part4]))
    dx, _, dgmix0 = rms_bwd("rms_mix0_bwd", xs, norm_mix_g[0], [dhn0], dh1)
    halves = [sum_chips(f"sum_chips{t + 1}", p, r, place)
              for t, (p, r) in enumerate(((part1, recv1), (part2, recv2), (part3, recv3), (part4, recv4)))]
    g_r1, g_c, g_r0, g_ab = run_comm("share_reduced_halves", share_halves_job(halves))

    small_full = [
        jnp.concatenate([dgmix0, dgmix1], axis=0), jnp.concatenate([dgx0, dgx1], axis=0),
        jnp.concatenate([dgmem0, dgmem1], axis=0), dg_final.reshape(D), drel[None],
        dcb, dclg, dclb, dws[None], jnp.transpose(dbst)[None],
        dcw[:CONV_WIDTH][None], dsgu_g, dsgu_b]
    summed = _unpack(exchange_small("reduce_small", _pack(small_full), reduce=True), [a.shape for a in small_full])
    (g_norm_mix, g_norm_x, g_norm_mem, g_final, g_rel, g_conv_b, g_clg, g_clb, g_ws, g_bs,
     g_conv_w_full, g_sgu_g_full, g_sgu_b_full) = summed
    cws = conv_w.shape[2]
    g_conv_w = lax.dynamic_slice_in_dim(g_conv_w_full, chip * cws, cws, axis=2)
    sgs = sgu_ln_g.shape[1]
    g_sgu_g = lax.dynamic_slice_in_dim(g_sgu_g_full, chip * sgs, sgs, axis=1)
    g_sgu_b = lax.dynamic_slice_in_dim(g_sgu_b_full, chip * sgs, sgs, axis=1)

    loss = lax.psum(loss_row[0, 0], ("x", "y", "c"))

    g_rows = {"w_out_ab": g_r0[0:ro][None], "w_out_c": g_r1[0:ro][None]}
    for i, nm_ in enumerate("qkvo"):
        lo = ro + i * rq
        g_rows["w_x" + nm_] = jnp.stack([g_r0[lo:lo + rq], g_r1[lo:lo + rq]])
    grads = dict(
        norm_mix_g=g_norm_mix, norm_x_g=g_norm_x, norm_mem_g=g_norm_mem, final_norm_g=g_final,
        w_in_ab=g_ab[None], rel_bias=g_rel, conv_w=g_conv_w, conv_b=g_conv_b, conv_ln_g=g_clg, conv_ln_b=g_clb,
        w_out_ab=g_rows["w_out_ab"], w_in_c=g_c[None], sgu_ln_g=g_sgu_g, sgu_ln_b=g_sgu_b, w_s=g_ws, b_s=g_bs,
        w_out_c=g_rows["w_out_c"], w_xq=g_rows["w_xq"], w_xk=g_rows["w_xk"], w_xv=g_rows["w_xv"],
        w_xo=g_rows["w_xo"])
    weights = dict(
        norm_mix_g=(norm_mix_g, m_norm_mix_g, v_norm_mix_g), norm_x_g=(norm_x_g, m_norm_x_g, v_norm_x_g),
        norm_mem_g=(norm_mem_g, m_norm_mem_g, v_norm_mem_g), final_norm_g=(final_norm_g, m_final_norm_g, v_final_norm_g),
        w_in_ab=(w_in_ab, m_w_in_ab, v_w_in_ab), rel_bias=(rel_bias, m_rel_bias, v_rel_bias),
        conv_w=(conv_w, m_conv_w, v_conv_w), conv_b=(conv_b, m_conv_b, v_conv_b),
        conv_ln_g=(conv_ln_g, m_conv_ln_g, v_conv_ln_g), conv_ln_b=(conv_ln_b, m_conv_ln_b, v_conv_ln_b),
        w_out_ab=(w_out_ab, m_w_out_ab, v_w_out_ab), w_in_c=(w_in_c, m_w_in_c, v_w_in_c),
        sgu_ln_g=(sgu_ln_g, m_sgu_ln_g, v_sgu_ln_g), sgu_ln_b=(sgu_ln_b, m_sgu_ln_b, v_sgu_ln_b),
        w_s=(w_s, m_w_s, v_w_s), b_s=(b_s, m_b_s, v_b_s), w_out_c=(w_out_c, m_w_out_c, v_w_out_c),
        w_xq=(w_xq, m_w_xq, v_w_xq), w_xk=(w_xk, m_w_xk, v_w_xk), w_xv=(w_xv, m_w_xv, v_w_xv),
        w_xo=(w_xo, m_w_xo, v_w_xo))
    names = list(weights)
    big_names = ("w_in_ab", "w_out_ab", "w_in_c", "w_out_c", "w_xq", "w_xk", "w_xv", "w_xo")
    delta, new_m, new_v = {}, {}, {}
    for nm_ in big_names:
        w, m, v = weights[nm_]
        C = w.shape[-1]
        d2, m2, v2 = adamw("adamw_" + nm_, w.reshape(-1, C), grads[nm_].reshape(-1, C), m.reshape(-1, C),
                           v.reshape(-1, C))
        delta[nm_], new_m[nm_], new_v[nm_] = d2.reshape(w.shape), m2.reshape(w.shape), v2.reshape(w.shape)
    small_names = [n for n in names if n not in big_names]
    shapes = [weights[n][0].shape for n in small_names]
    d_s, m_s, v_s = adamw("adamw_small", _pack([weights[n][0] for n in small_names]),
                          _pack([grads[n] for n in small_names]), _pack([weights[n][1] for n in small_names]),
                          _pack([weights[n][2] for n in small_names]))
    for n, d_, m_, v_ in zip(small_names, _unpack(d_s, shapes), _unpack(m_s, shapes), _unpack(v_s, shapes)):
        delta[n], new_m[n], new_v[n] = d_, m_, v_

    return (loss, dx[None], *[grads[n].reshape(weights[n][0].shape) for n in names], *[delta[n] for n in names],
            *[new_m[n] for n in names], *[new_v[n] for n in names])
```

```python
import functools

import numpy as np
import jax
import jax.numpy as jnp
from jax import lax
from jax.experimental import pallas as pl
from jax.experimental.pallas import tpu as pltpu

F32 = jnp.float32
BF16 = jnp.bfloat16
MESH = pl.DeviceIdType.MESH

EPS = 1e-6
CHUNK = 64
N_PAST_CHUNKS = 8
MAX_REL = 128
HEAD_DIM_A = 128
CONV_WIDTH = 31
CONV_HALO = 32
GMLP_CHUNK = 128
N_GROUPS_C = 8
N_HEADS_X = 4
NEG = -1e30

ADAM_LR = 0.001
ADAM_B1 = 0.9
ADAM_B2 = 0.999
ADAM_EPS = 1e-08
ADAM_WD = 0.01
ADAM_STEP = 10

N_CHIPS = 4
N_DEV = 8
V7X_VMEM_LIMIT = 56 * 1024 * 1024
LANES = 128
SUBLANES = 8


def _pick(n, cands):
    for c in cands:
        if c <= n and n % c == 0:
            return c
    return n


def _cparams(*sem):
    return pltpu.CompilerParams(dimension_semantics=sem, vmem_limit_bytes=V7X_VMEM_LIMIT)


def _sigmoid(x):
    return 0.5 * jnp.tanh(0.5 * x) + 0.5


def _dot(a, b, contract):
    return lax.dot_general(a, b, (contract, ((), ())), preferred_element_type=F32)


NN = ((1,), (0,))
NT = ((1,), (1,))
TN = ((0,), (0,))


class _Comm:
    def __init__(self, arrays, out_shapes, aliases, sems, start, finish, relay=None):
        self.arrays, self.out_shapes, self.aliases, self.sems = list(arrays), list(out_shapes), dict(aliases), list(sems)
        self.start, self.finish, self.relay = start, finish, relay


def _join(*jobs):
    assert all(j.relay is None for j in jobs)
    arrays, outs, sems, aliases, spans = [], [], [], {}, []
    for j in jobs:
        spans.append((len(arrays), len(outs), len(sems)))
        aliases.update({len(arrays) + i: len(outs) + o for i, o in j.aliases.items()})
        arrays += j.arrays
        outs += j.out_shapes
        sems += j.sems

    def part(j, span, ins, os_, ss):
        a0, o0, s0 = span
        return (ins[a0:a0 + len(j.arrays)], os_[o0:o0 + len(j.out_shapes)], ss[s0:s0 + len(j.sems)])

    def start(ins, os_, ss):
        for j, span in zip(jobs, spans):
            j.start(*part(j, span, ins, os_, ss))

    def finish(ins, os_, ss):
        for j, span in zip(jobs, spans):
            j.finish(*part(j, span, ins, os_, ss))

    return _Comm(arrays, outs, aliases, sems, start, finish)


def _call(body, *, name, grid, in_specs, out_specs, out_shape, args, scratch_shapes=(), sem=None, comm=None):
    multi = isinstance(out_shape, (list, tuple))
    o_shapes = list(out_shape) if multi else [out_shape]
    o_specs = list(out_specs) if multi else [out_specs]
    if comm is None:
        return pl.pallas_call(body, grid=grid, in_specs=in_specs, out_specs=out_specs, out_shape=out_shape,
                              scratch_shapes=list(scratch_shapes), name=name,
                              compiler_params=_cparams(*sem))(*args)
    n_in, n_out, n_scr = len(in_specs), len(o_shapes), len(scratch_shapes)
    n_ci, n_co = len(comm.arrays), len(comm.out_shapes)
    n_steps = int(np.prod(grid))

    def carrier(*refs):
        ins, rest = refs[:n_in], refs[n_in:]
        cins, rest = rest[:n_ci], rest[n_ci:]
        outs, rest = rest[:n_out], rest[n_out:]
        couts, rest = rest[:n_co], rest[n_co:]
        scr, csems = rest[:n_scr], rest[n_scr:]
        step = 0
        for a, g in enumerate(grid):
            step = step * g + pl.program_id(a)

        @pl.when(step == 0)
        def _():
            comm.start(cins, couts, csems)

        body(*ins, *outs, *scr)

        if comm.relay is not None and n_steps >= 4:
            @pl.when(step == (3 * n_steps) // 4)
            def _():
                comm.relay(cins, couts, csems)

        @pl.when(step == n_steps - 1)
        def _():
            if comm.relay is not None and n_steps < 4:
                comm.relay(cins, couts, csems)
            comm.finish(cins, couts, csems)

    res = pl.pallas_call(
        carrier, grid=grid, in_specs=list(in_specs) + [HBM_SPEC] * n_ci,
        out_specs=o_specs + [HBM_SPEC] * n_co, out_shape=o_shapes + comm.out_shapes,
        input_output_aliases={n_in + i: n_out + o for i, o in comm.aliases.items()},
        scratch_shapes=list(scratch_shapes) + comm.sems, name=name,
        compiler_params=_cparams(*(["arbitrary"] * len(grid))))(*args, *comm.arrays)
    mine = list(res[:n_out]) if multi else res[0]
    return mine, list(res[n_out:])


def run_comm(name, comm):
    def body(*refs):
        n_ci, n_co = len(comm.arrays), len(comm.out_shapes)
        cins, couts, csems = refs[:n_ci], refs[n_ci:n_ci + n_co], refs[n_ci + n_co:]
        comm.start(cins, couts, csems)
        if comm.relay is not None:
            comm.relay(cins, couts, csems)
        comm.finish(cins, couts, csems)

    return pl.pallas_call(
        body, in_specs=[HBM_SPEC] * len(comm.arrays), out_specs=[HBM_SPEC] * len(comm.out_shapes),
        out_shape=comm.out_shapes, input_output_aliases=comm.aliases, scratch_shapes=comm.sems,
        name=name)(*comm.arrays)


def _mm(name, a, b, *, contract, grid, a_spec, b_spec, o_spec, out_shape, res=None, comm=None):
    nk = grid[2]

    def body(*refs):
        if res is not None:
            a_ref, b_ref, r_ref, o_ref = refs[:4]
        else:
            a_ref, b_ref, o_ref = refs[:3]
            r_ref = None
        p = _dot(a_ref[...].astype(BF16), b_ref[...].astype(BF16), contract)

        def finish(acc):
            if r_ref is not None:
                acc = acc + r_ref[...]
            o_ref[...] = acc.astype(o_ref.dtype)

        if nk == 1:
            finish(p)
        else:
            acc_ref = refs[-1]
            k = pl.program_id(2)

            @pl.when(k == 0)
            def _():
                acc_ref[...] = p

            @pl.when(k > 0)
            def _():
                acc_ref[...] += p

            @pl.when(k == nk - 1)
            def _():
                finish(acc_ref[...])

    in_specs = [a_spec, b_spec]
    args = [a, b]
    if res is not None:
        in_specs.append(o_spec)
        args.append(res)
    blk = tuple(d for d in o_spec.block_shape if d is not None)
    scratch = [] if nk == 1 else [pltpu.VMEM(blk, F32)]
    return _call(body, name=name, grid=grid, in_specs=in_specs, out_specs=o_spec, out_shape=out_shape,
                 args=args, scratch_shapes=scratch, sem=("parallel", "parallel", "arbitrary"), comm=comm)


def mm_nn_cols(name, a, w4, out_dtype, comm=None):
    M, K = a.shape
    _, _, C = w4.shape
    tm = _pick(M, (1024, 512, 256))
    tn = _pick(C, (512, 256, 128))
    nps = C // tn
    return _mm(name, a, w4, contract=NN, grid=(M // tm, 4 * nps, 1),
               a_spec=pl.BlockSpec((tm, K), lambda i, j, k: (i, 0)),
               b_spec=pl.BlockSpec((None, K, tn), lambda i, j, k: (j // nps, 0, j % nps)),
               o_spec=pl.BlockSpec((tm, tn), lambda i, j, k: (i, j)),
               out_shape=jax.ShapeDtypeStruct((M, 4 * C), out_dtype), comm=comm)


def mm_nn(name, a, w, out_dtype, res=None):
    M, K = a.shape
    N = w.shape[1]
    tm = _pick(M, (1024, 512, 256))
    tn = _pick(N, (512, 256, 128))
    return _mm(name, a, w, contract=NN, grid=(M // tm, N // tn, 1),
               a_spec=pl.BlockSpec((tm, K), lambda i, j, k: (i, 0)),
               b_spec=pl.BlockSpec((K, tn), lambda i, j, k: (0, j)),
               o_spec=pl.BlockSpec((tm, tn), lambda i, j, k: (i, j)),
               out_shape=jax.ShapeDtypeStruct((M, N), out_dtype), res=res)


def mm_nt_cols(name, a, w4, out_dtype, comm=None):
    M = a.shape[0]
    _, K, C = w4.shape
    tm = _pick(M, (1024, 512, 256))
    tn = _pick(K, (1024, 512, 256, 128))
    tk = _pick(C, (3584, 3072, 1792, 1536, 1024, 512, 256, 128))
    kps = C // tk
    return _mm(name, a, w4, contract=NT, grid=(M // tm, K // tn, 4 * kps),
               a_spec=pl.BlockSpec((tm, tk), lambda i, j, k: (i, k)),
               b_spec=pl.BlockSpec((None, tn, tk), lambda i, j, k: (k // kps, j, k % kps)),
               o_spec=pl.BlockSpec((tm, tn), lambda i, j, k: (i, j)),
               out_shape=jax.ShapeDtypeStruct((M, K), out_dtype), comm=comm)


def mm_nt(name, a, w, out_dtype):
    M, C = a.shape
    N = w.shape[0]
    tm = _pick(M, (1024, 512, 256))
    tn = _pick(N, (512, 256, 128))
    return _mm(name, a, w, contract=NT, grid=(M // tm, N // tn, 1),
               a_spec=pl.BlockSpec((tm, C), lambda i, j, k: (i, 0)),
               b_spec=pl.BlockSpec((tn, C), lambda i, j, k: (j, 0)),
               o_spec=pl.BlockSpec((tm, tn), lambda i, j, k: (i, j)),
               out_shape=jax.ShapeDtypeStruct((M, N), out_dtype))


def mm_tn_cols(name, a, b, comm=None):
    S, K = a.shape
    C = b.shape[1] // 4
    ts = _pick(S, (2048, 1024, 512, 256))
    tko = _pick(K, (1024, 512, 256, 128))
    tn = _pick(C, (1792, 1536, 1024, 512, 256, 128))
    nps = C // tn
    return _mm(name, a, b, contract=TN, grid=(K // tko, 4 * nps, S // ts),
               a_spec=pl.BlockSpec((ts, tko), lambda i, j, k: (k, i)),
               b_spec=pl.BlockSpec((ts, tn), lambda i, j, k: (k, j)),
               o_spec=pl.BlockSpec((None, tko, tn), lambda i, j, k: (j // nps, i, j % nps)),
               out_shape=jax.ShapeDtypeStruct((4, K, C), BF16), comm=comm)


def mm_tn(name, a, b):
    S, K = a.shape
    N = b.shape[1]
    ts = _pick(S, (1024, 512, 256))
    tko = _pick(K, (2048, 1024, 512, 256, 128))
    tn = _pick(N, (1024, 512, 256, 128))
    return _mm(name, a, b, contract=TN, grid=(K // tko, N // tn, S // ts),
               a_spec=pl.BlockSpec((ts, tko), lambda i, j, k: (k, i)),
               b_spec=pl.BlockSpec((ts, tn), lambda i, j, k: (k, j)),
               o_spec=pl.BlockSpec((tko, tn), lambda i, j, k: (i, j)),
               out_shape=jax.ShapeDtypeStruct((K, N), BF16))


def rms_fwd(name, x, g):
    S, D = x.shape
    T = _pick(S, (512, 256))

    def body(x_ref, g_ref, o_ref):
        xf = x_ref[...]
        r = lax.rsqrt(jnp.mean(xf * xf, axis=-1, keepdims=True) + EPS)
        o_ref[...] = (xf * r * g_ref[...]).astype(o_ref.dtype)

    return pl.pallas_call(
        body, grid=(S // T,),
        in_specs=[pl.BlockSpec((T, D), lambda i: (i, 0)), pl.BlockSpec((1, D), lambda i: (0, 0))],
        out_specs=pl.BlockSpec((T, D), lambda i: (i, 0)),
        out_shape=jax.ShapeDtypeStruct((S, D), BF16), name=name,
        compiler_params=_cparams("parallel"))(x, g.reshape(1, D))


def rms_bwd(name, x, g, dys, dres):
    S, D = x.shape
    T = _pick(S, (256,))
    ndy = len(dys)
    has_res = dres is not None

    def body(*refs):
        x_ref, g_ref = refs[0], refs[1]
        dy_refs = refs[2:2 + ndy]
        r_ref = refs[2 + ndy] if has_res else None
        dx_ref, dxb_ref, dg_ref = refs[-3], refs[-2], refs[-1]
        i = pl.program_id(0)
        xf = x_ref[...]
        r = lax.rsqrt(jnp.mean(xf * xf, axis=-1, keepdims=True) + EPS)
        xhat = xf * r
        dy = dy_refs[0][...].astype(F32)
        for d in dy_refs[1:]:
            dy = dy + d[...].astype(F32)
        dxhat = dy * g_ref[...]
        dx = r * (dxhat - xhat * jnp.mean(dxhat * xhat, axis=-1, keepdims=True))
        if has_res:
            dx = dx + r_ref[...]
        dx_ref[...] = dx
        dxb_ref[...] = dx.astype(dxb_ref.dtype)
        dg = jnp.sum(dy * xhat, axis=0, keepdims=True)

        @pl.when(i == 0)
        def _():
            dg_ref[...] = dg

        @pl.when(i > 0)
        def _():
            dg_ref[...] += dg

    row = pl.BlockSpec((T, D), lambda i: (i, 0))
    vec = pl.BlockSpec((1, D), lambda i: (0, 0))
    args = [x, g.reshape(1, D), *dys] + ([dres] if has_res else [])
    return pl.pallas_call(
        body, grid=(S // T,),
        in_specs=[row, vec] + [row] * (ndy + int(has_res)),
        out_specs=[row, row, vec],
        out_shape=[jax.ShapeDtypeStruct((S, D), F32), jax.ShapeDtypeStruct((S, D), BF16),
                   jax.ShapeDtypeStruct((1, D), F32)],
        name=name, compiler_params=_cparams("arbitrary"))(*args)


def loss_head(name, h, g, target):
    S, D = h.shape
    T = _pick(S, (256,))

    def body(h_ref, g_ref, t_ref, loss_ref, dg_ref, dh_ref, dhb_ref):
        i = pl.program_id(0)
        xf = h_ref[...]
        gv = g_ref[...]
        r = lax.rsqrt(jnp.mean(xf * xf, axis=-1, keepdims=True) + EPS)
        xhat = xf * r
        err = xhat * gv - t_ref[...]
        part = 0.5 * jnp.sum(jnp.sum(err * err, axis=-1, keepdims=True), axis=0, keepdims=True) / D
        dout = err / D
        dxhat = dout * gv
        dh = r * (dxhat - xhat * jnp.mean(dxhat * xhat, axis=-1, keepdims=True))
        dh_ref[...] = dh
        dhb_ref[...] = dh.astype(dhb_ref.dtype)
        dg = jnp.sum(dout * xhat, axis=0, keepdims=True)
        lrow = jnp.broadcast_to(part, (1, LANES))

        @pl.when(i == 0)
        def _():
            dg_ref[...] = dg
            loss_ref[...] = lrow

        @pl.when(i > 0)
        def _():
            dg_ref[...] += dg
            loss_ref[...] += lrow

    row = pl.BlockSpec((T, D), lambda i: (i, 0))
    vec = pl.BlockSpec((1, D), lambda i: (0, 0))
    return pl.pallas_call(
        body, grid=(S // T,), in_specs=[row, vec, row],
        out_specs=[pl.BlockSpec((1, LANES), lambda i: (0, 0)), vec, row, row],
        out_shape=[jax.ShapeDtypeStruct((1, LANES), F32), jax.ShapeDtypeStruct((1, D), F32),
                   jax.ShapeDtypeStruct((S, D), F32), jax.ShapeDtypeStruct((S, D), BF16)],
        name=name, compiler_params=_cparams("arbitrary"))(h, g.reshape(1, D), target)


def _attn_tq(S):
    return _pick(S, (512,))


def band_bias_table(rel_bias, tq):
    H = rel_bias.shape[0]
    w = 2 * tq
    nbits = int(np.log2(tq))
    assert (1 << nbits) == tq and (N_PAST_CHUNKS + 2) * CHUNK - 1 <= w
    c = np.arange(w)
    d0 = np.where(c <= tq + CHUNK - 1, tq - c, tq + w - c)
    base = jnp.take(rel_bias.astype(F32), jnp.asarray(np.clip(d0, -MAX_REL, MAX_REL) + MAX_REL), axis=1)

    def body(b_ref, o_ref):
        x = jnp.broadcast_to(b_ref[...], (tq, w))
        row = lax.broadcasted_iota(jnp.int32, (tq, w), 0)
        col = lax.broadcasted_iota(jnp.int32, (tq, w), 1)
        for b in range(nbits):
            x = jnp.where(((row >> b) & 1) == 1, pltpu.roll(x, 1 << b, 1), x)
        qc = row // CHUNK
        kc = col // CHUNK - tq // CHUNK
        o_ref[...] = jnp.where((kc <= qc) & (kc >= qc - N_PAST_CHUNKS), x, NEG)

    return pl.pallas_call(
        body, grid=(H,), in_specs=[pl.BlockSpec((None, 1, w), lambda h: (h, 0, 0))],
        out_specs=pl.BlockSpec((None, tq, w), lambda h: (h, 0, 0)),
        out_shape=jax.ShapeDtypeStruct((H, tq, w), F32), name="band_bias_table",
        compiler_params=_cparams("parallel"))(base.reshape(H, 1, w))


def _attn_subblocks(tq):
    sub = tq // 2
    assert sub % CHUNK == 0 and N_PAST_CHUNKS * CHUNK == tq
    return sub, 3


def attn_fwd(proj, bm, D, comm=None):
    S = proj.shape[0]
    H = D // HEAD_DIM_A
    tq = _attn_tq(S)
    nb = S // tq
    scale = HEAD_DIM_A ** -0.5

    sub, n_sub = _attn_subblocks(tq)

    def body(q_ref, kp_ref, kc_ref, vp_ref, vc_ref, bm_ref, o_ref, lse_ref):
        i = pl.program_id(1)
        for qh in range(tq // sub):
            rows = slice(qh * sub, (qh + 1) * sub)
            q = q_ref[rows, :]
            ss = []
            for kb in range(qh, qh + n_sub):
                k_ref, krows = (kp_ref, kb) if kb < tq // sub else (kc_ref, kb - tq // sub)
                s = _dot(q, k_ref[krows * sub:(krows + 1) * sub, :], NT) * scale + bm_ref[rows, kb * sub:(kb + 1) * sub]
                if kb < tq // sub:
                    s = jnp.where(i == 0, NEG, s)
                ss.append(s)
            m = functools.reduce(jnp.maximum, [jnp.max(s, axis=-1, keepdims=True) for s in ss])
            ps = [jnp.exp(s - m) for s in ss]
            l = functools.reduce(jnp.add, [jnp.sum(p, axis=-1, keepdims=True) for p in ps])
            o = None
            for p, kb in zip(ps, range(qh, qh + n_sub)):
                v_ref, vrows = (vp_ref, kb) if kb < tq // sub else (vc_ref, kb - tq // sub)
                t = _dot(p.astype(BF16), v_ref[vrows * sub:(vrows + 1) * sub, :], NN)
                o = t if o is None else o + t
            o_ref[rows, :] = (o / l).astype(o_ref.dtype)
            lse_ref[rows, :] = m + jnp.log(l)

    def col(base):
        return (pl.BlockSpec((tq, HEAD_DIM_A), lambda h, i: (jnp.maximum(i - 1, 0), base + h)),
                pl.BlockSpec((tq, HEAD_DIM_A), lambda h, i: (i, base + h)))

    kp, kc = col(H)
    vp, vc = col(2 * H)
    return _call(
        body, name="attn_fwd", grid=(H, nb),
        in_specs=[pl.BlockSpec((tq, HEAD_DIM_A), lambda h, i: (i, h)), kp, kc, vp, vc,
                  pl.BlockSpec((None, tq, 2 * tq), lambda h, i: (h, 0, 0))],
        out_specs=[pl.BlockSpec((tq, HEAD_DIM_A), lambda h, i: (i, h)),
                   pl.BlockSpec((None, tq, 1), lambda h, i: (h, i, 0))],
        out_shape=[jax.ShapeDtypeStruct((S, D), BF16), jax.ShapeDtypeStruct((H, S, 1), F32)],
        args=[proj, proj, proj, proj, proj, bm], sem=("parallel", "arbitrary"), comm=comm)


def attn_bwd(proj, ya, dya, lse, bm, D, comm=None):
    S = proj.shape[0]
    H = D // HEAD_DIM_A
    tq = _attn_tq(S)
    nb = S // tq
    scale = HEAD_DIM_A ** -0.5
    sub, n_sub = _attn_subblocks(tq)

    def body(q_ref, kp_ref, kc_ref, vp_ref, vc_ref, o_ref, do_ref, lse_ref, bm_ref,
             dq_ref, dkc_ref, dkp_ref, dvc_ref, dvp_ref, ds_ref):
        i = pl.program_id(1)
        per = tq // sub

        @pl.when(i == 0)
        def _():
            ds_ref[...] = jnp.zeros_like(ds_ref)

        dk_acc = [None] * (2 * per)
        dv_acc = [None] * (2 * per)
        for qh in range(per):
            rows = slice(qh * sub, (qh + 1) * sub)
            q = q_ref[rows, :]
            do = do_ref[rows, :]
            delta = jnp.sum(do.astype(F32) * o_ref[rows, :].astype(F32), axis=-1, keepdims=True)
            lse_v = lse_ref[rows, :]
            dq = None
            for kb in range(qh, qh + n_sub):
                k_ref, v_ref, kr = (kp_ref, vp_ref, kb) if kb < per else (kc_ref, vc_ref, kb - per)
                k = k_ref[kr * sub:(kr + 1) * sub, :]
                cols = slice(kb * sub, (kb + 1) * sub)
                s = _dot(q, k, NT) * scale + bm_ref[rows, cols]
                if kb < per:
                    s = jnp.where(i == 0, NEG, s)
                p = jnp.exp(s - lse_v)
                dv = _dot(p.astype(BF16), do, TN)
                dp = _dot(do, v_ref[kr * sub:(kr + 1) * sub, :], NT)
                ds = p * (dp - delta)
                dsb = ds.astype(BF16)
                t = _dot(dsb, k, NN)
                dq = t if dq is None else dq + t
                dk = _dot(dsb, q, TN)
                dk_acc[kb] = dk if dk_acc[kb] is None else dk_acc[kb] + dk
                dv_acc[kb] = dv if dv_acc[kb] is None else dv_acc[kb] + dv
                ds_ref[rows, cols] += ds
            dq_ref[rows, :] = (dq * scale).astype(dq_ref.dtype)
        for kb in range(2 * per):
            dk_ref, dv_ref, kr = (dkp_ref, dvp_ref, kb) if kb < per else (dkc_ref, dvc_ref, kb - per)
            dk_ref[kr * sub:(kr + 1) * sub, :] = (dk_acc[kb] * scale).astype(dk_ref.dtype)
            dv_ref[kr * sub:(kr + 1) * sub, :] = dv_acc[kb].astype(dv_ref.dtype)

    def col(base):
        return (pl.BlockSpec((tq, HEAD_DIM_A), lambda h, i: (jnp.maximum(i - 1, 0), base + h)),
                pl.BlockSpec((tq, HEAD_DIM_A), lambda h, i: (i, base + h)))

    kp, kc = col(H)
    vp, vc = col(2 * H)
    blk = pl.BlockSpec((tq, HEAD_DIM_A), lambda h, i: (i, h))
    sd = jax.ShapeDtypeStruct((S, D), BF16)
    return _call(
        body, name="attn_bwd", grid=(H, nb),
        in_specs=[blk, kp, kc, vp, vc, blk, blk,
                  pl.BlockSpec((None, tq, 1), lambda h, i: (h, i, 0)),
                  pl.BlockSpec((None, tq, 2 * tq), lambda h, i: (h, 0, 0))],
        out_specs=[blk, blk, blk, blk, blk, pl.BlockSpec((None, tq, 2 * tq), lambda h, i: (h, 0, 0))],
        out_shape=[sd, sd, sd, sd, sd, jax.ShapeDtypeStruct((H, tq, 2 * tq), F32)],
        args=[proj, proj, proj, proj, proj, ya, dya, lse, bm], sem=("parallel", "arbitrary"), comm=comm)


def rel_bias_grad(ds_sum):
    H, tq, w = ds_sum.shape
    nbin = 2 * MAX_REL + 1
    nbin_pad = 3 * LANES
    d_lo, d_hi = -(CHUNK - 1), (N_PAST_CHUNKS + 1) * CHUNK - 1
    assert d_hi - d_lo + 1 <= w
    onehot = np.zeros((w, nbin_pad), np.float32)
    for d in range(d_lo, d_hi + 1):
        onehot[(tq - d) % w, int(np.clip(d, -MAX_REL, MAX_REL)) + MAX_REL] = 1.0
    nbits = int(np.log2(tq))
    assert (1 << nbits) == tq

    def body(ds_ref, m_ref, o_ref):
        x = ds_ref[...]
        row = lax.broadcasted_iota(jnp.int32, x.shape, 0)
        for b in range(nbits):
            rolled = pltpu.roll(x, w - (1 << b), 1)
            x = jnp.where(((row >> b) & 1) == 1, rolled, x)
        t = jnp.sum(x, axis=0, keepdims=True)
        o_ref[...] = lax.dot_general(t, m_ref[...], (NN, ((), ())), precision=lax.Precision.HIGHEST,
                                     preferred_element_type=F32)

    out = pl.pallas_call(
        body, grid=(H,),
        in_specs=[pl.BlockSpec((None, tq, w), lambda h: (h, 0, 0)),
                  pl.BlockSpec((w, nbin_pad), lambda h: (0, 0))],
        out_specs=pl.BlockSpec((None, 1, nbin_pad), lambda h: (h, 0, 0)),
        out_shape=jax.ShapeDtypeStruct((H, 1, nbin_pad), F32),
        name="rel_bias_grad", compiler_params=_cparams("parallel"))(ds_sum, jnp.asarray(onehot))
    return out[:, 0, :nbin]


def _conv_t(S):
    return _pick(S, (256,))


ROW_CHUNK = 16


def _row_loop(n_rows, step):
    def one(r, carry):
        step(pl.ds(pl.multiple_of(r * ROW_CHUNK, ROW_CHUNK), ROW_CHUNK))
        return carry

    lax.fori_loop(0, n_rows // ROW_CHUNK, one, 0)


def _fill_zbuf(zbuf, ap_ref, bp_ref, a_ref, b_ref, i):
    zp = ap_ref[...].astype(F32) * _sigmoid(bp_ref[...].astype(F32))
    zbuf[0:CONV_HALO, :] = jnp.where(i == 0, 0.0, zp)

    def step(rows):
        below = pl.ds(pl.multiple_of(rows.start + CONV_HALO, ROW_CHUNK), ROW_CHUNK)
        zbuf[below, :] = a_ref[rows, :].astype(F32) * _sigmoid(b_ref[rows, :].astype(F32))

    _row_loop(a_ref.shape[0], step)


def _shifted_windows(buf, shifted, lanes, T):
    rows = T + CONV_HALO - SUBLANES
    for b in range(1, SUBLANES):
        shifted[b - 1] = buf[pl.ds(b, rows), lanes]

    def window(off, r0=0, n=T):
        a, b = divmod(off, SUBLANES)
        if b == 0:
            return buf[pl.ds(r0 + off, n), lanes]
        return shifted[b - 1, pl.ds(r0 + a * SUBLANES, n), :]

    return window


def _shifted_scratch(T):
    return pltpu.VMEM((SUBLANES - 1, T + CONV_HALO - SUBLANES, LANES), F32)


def conv_gate_fwd(proj, ya, cw, cb, lng, lnb, D, comm=None):
    S = proj.shape[0]
    T = _conv_t(S)
    hb = T // CONV_HALO
    nlb = D // LANES

    def body(ap_ref, bp_ref, a_ref, b_ref, ga_ref, gb_ref, ya_ref, cw_ref, cb_ref, lng_ref, lnb_ref,
             y_ref, c_ref, zbuf, zsh):
        i = pl.program_id(0)
        _fill_zbuf(zbuf, ap_ref, bp_ref, a_ref, b_ref, i)

        def lane_block(lb, carry):
            lanes = pl.ds(pl.multiple_of(lb * LANES, LANES), LANES)
            z_at = _shifted_windows(zbuf, zsh, lanes, T)
            acc = jnp.zeros((T, LANES), F32)
            for k in range(CONV_WIDTH):
                acc = acc + cw_ref[k:k + 1, lanes] * z_at(CONV_HALO - CONV_WIDTH + 1 + k)
            c_ref[:, lanes] = acc + cb_ref[:, lanes]
            return carry

        lax.fori_loop(0, nlb, lane_block, 0)

        def norm_and_gate(rows):
            c = c_ref[rows, :]
            mu = jnp.mean(c, axis=-1, keepdims=True)
            xc = c - mu
            rstd = lax.rsqrt(jnp.mean(xc * xc, axis=-1, keepdims=True) + EPS)
            ln = xc * rstd * lng_ref[...] + lnb_ref[...]
            yb = ln * _sigmoid(ln)
            ga = ga_ref[rows, :].astype(F32)
            gb = gb_ref[rows, :].astype(F32)
            y_ref[rows, :D] = (ya_ref[rows, :].astype(F32) * (ga * _sigmoid(ga))).astype(y_ref.dtype)
            y_ref[rows, D:] = (yb * (gb * _sigmoid(gb))).astype(y_ref.dtype)

        _row_loop(T, norm_and_gate)

    def cur(cidx):
        return pl.BlockSpec((T, D), lambda i: (i, cidx))

    def prev(cidx):
        return pl.BlockSpec((CONV_HALO, D), lambda i: (jnp.maximum(i * hb - 1, 0), cidx))

    vec = pl.BlockSpec((1, D), lambda i: (0, 0))
    return _call(
        body, name="conv_gate_fwd", grid=(S // T,),
        in_specs=[prev(3), prev(4), cur(3), cur(4), cur(5), cur(6), pl.BlockSpec((T, D), lambda i: (i, 0)),
                  pl.BlockSpec((CONV_HALO, D), lambda i: (0, 0)), vec, vec, vec],
        out_specs=[pl.BlockSpec((T, 2 * D), lambda i: (i, 0)), pl.BlockSpec((T, D), lambda i: (i, 0))],
        out_shape=[jax.ShapeDtypeStruct((S, 2 * D), BF16), jax.ShapeDtypeStruct((S, D), F32)],
        scratch_shapes=[pltpu.VMEM((T + CONV_HALO, D), F32), _shifted_scratch(T)],
        args=[proj, proj, proj, proj, proj, proj, ya, cw, cb, lng, lnb], sem=("parallel",), comm=comm)


def conv_gate_bwd_a(dy0, proj, ya, cpre, lng, lnb, D):
    S = proj.shape[0]
    T = _conv_t(S)

    def body(dy_ref, ga_ref, gb_ref, ya_ref, c_ref, lng_ref, lnb_ref,
             dya_ref, dg_ref, dc_ref, dlng_ref, dlnb_ref):
        i = pl.program_id(0)

        c = c_ref[...]
        gv = lng_ref[...]
        mu = jnp.mean(c, axis=-1, keepdims=True)
        xc = c - mu
        rstd = lax.rsqrt(jnp.mean(xc * xc, axis=-1, keepdims=True) + EPS)
        xhat = xc * rstd
        ln = xhat * gv + lnb_ref[...]
        sl = _sigmoid(ln)
        yb = ln * sl
        ga = ga_ref[...].astype(F32)
        gb = gb_ref[...].astype(F32)
        sa = _sigmoid(ga)
        sb = _sigmoid(gb)
        dy_a = dy_ref[:, :D].astype(F32)
        dy_b = dy_ref[:, D:].astype(F32)
        dya_ref[...] = (dy_a * (ga * sa)).astype(dya_ref.dtype)
        dg_ref[:, :D] = (dy_a * ya_ref[...].astype(F32) * (sa * (1.0 + ga * (1.0 - sa)))).astype(dg_ref.dtype)
        dg_ref[:, D:] = (dy_b * yb * (sb * (1.0 + gb * (1.0 - sb)))).astype(dg_ref.dtype)
        dln = dy_b * (gb * sb) * (sl * (1.0 + ln * (1.0 - sl)))
        dxhat = dln * gv
        dc_ref[...] = rstd * (dxhat - jnp.mean(dxhat, axis=-1, keepdims=True)
                              - xhat * jnp.mean(dxhat * xhat, axis=-1, keepdims=True))
        dlng = jnp.sum(dln * xhat, axis=0, keepdims=True)
        dlnb = jnp.sum(dln, axis=0, keepdims=True)

        @pl.when(i == 0)
        def _():
            dlng_ref[...] = dlng
            dlnb_ref[...] = dlnb

        @pl.when(i > 0)
        def _():
            dlng_ref[...] += dlng
            dlnb_ref[...] += dlnb

    row = pl.BlockSpec((T, D), lambda i: (i, 0))
    vec = pl.BlockSpec((1, D), lambda i: (0, 0))
    return pl.pallas_call(
        body, grid=(S // T,),
        in_specs=[pl.BlockSpec((T, 2 * D), lambda i: (i, 0)),
                  pl.BlockSpec((T, D), lambda i: (i, 5)), pl.BlockSpec((T, D), lambda i: (i, 6)),
                  row, row, vec, vec],
        out_specs=[row, pl.BlockSpec((T, 2 * D), lambda i: (i, 0)), row, vec, vec],
        out_shape=[jax.ShapeDtypeStruct((S, D), BF16), jax.ShapeDtypeStruct((S, 2 * D), BF16),
                   jax.ShapeDtypeStruct((S, D), F32), jax.ShapeDtypeStruct((1, D), F32),
                   jax.ShapeDtypeStruct((1, D), F32)],
        name="conv_gate_bwd_a", compiler_params=_cparams("arbitrary"))(
            dy0, proj, proj, ya, cpre, lng, lnb)


def conv_gate_bwd_b(dc, proj, cw, D, comm=None):
    S = proj.shape[0]
    T = _conv_t(S)
    hb = T // CONV_HALO
    nt = S // T
    nlb = D // LANES
    half = T // 2

    def body(dc_ref, dn_ref, ap_ref, bp_ref, a_ref, b_ref, cw_ref, da_ref, db_ref, dcw_ref, dcb_ref,
             zbuf, dcbuf, zsh, dcsh, dcw8):
        i = pl.program_id(0)
        _fill_zbuf(zbuf, ap_ref, bp_ref, a_ref, b_ref, i)
        dcv = dc_ref[...]
        dcbuf[0:T, :] = dcv
        dcbuf[T:, :] = jnp.where(i == nt - 1, 0.0, dn_ref[...])

        @pl.when(i == 0)
        def _():
            dcw8[...] = jnp.zeros_like(dcw8)
            dcb_ref[...] = jnp.zeros_like(dcb_ref)

        dcb_ref[...] += jnp.sum(dcv, axis=0, keepdims=True)

        def lane_block(lb, carry):
            lanes = pl.ds(pl.multiple_of(lb * LANES, LANES), LANES)
            z_at = _shifted_windows(zbuf, zsh, lanes, T)
            dc_at = _shifted_windows(dcbuf, dcsh, lanes, T)
            for r0 in range(0, T, half):
                d0 = dcbuf[r0:r0 + half, lanes]
                dz = jnp.zeros((half, LANES), F32)
                for k in range(CONV_WIDTH):
                    dz = dz + cw_ref[k:k + 1, lanes] * dc_at(CONV_WIDTH - 1 - k, r0, half)
                    prod = d0 * z_at(CONV_HALO - CONV_WIDTH + 1 + k, r0, half)
                    dcw8[pl.ds(k * SUBLANES, SUBLANES), lanes] += jnp.sum(
                        prod.reshape(half // SUBLANES, SUBLANES, LANES), axis=0)
                av = a_ref[r0:r0 + half, lanes].astype(F32)
                sg = _sigmoid(b_ref[r0:r0 + half, lanes].astype(F32))
                da_ref[r0:r0 + half, lanes] = (dz * sg).astype(da_ref.dtype)
                db_ref[r0:r0 + half, lanes] = (dz * av * sg * (1.0 - sg)).astype(db_ref.dtype)
            return carry

        lax.fori_loop(0, nlb, lane_block, 0)

        @pl.when(i == nt - 1)
        def _():
            dcw_ref[...] = jnp.sum(dcw8[...].reshape(CONV_HALO, SUBLANES, D), axis=1)

    def cur(cidx):
        return pl.BlockSpec((T, D), lambda i: (i, cidx))

    def prev(cidx):
        return pl.BlockSpec((CONV_HALO, D), lambda i: (jnp.maximum(i * hb - 1, 0), cidx))

    row = pl.BlockSpec((T, D), lambda i: (i, 0))
    nxt = pl.BlockSpec((CONV_HALO, D), lambda i: (jnp.minimum((i + 1) * hb, nt * hb - 1), 0))
    return _call(
        body, name="conv_gate_bwd_b", grid=(nt,),
        in_specs=[row, nxt, prev(3), prev(4), cur(3), cur(4), pl.BlockSpec((CONV_HALO, D), lambda i: (0, 0))],
        out_specs=[row, row, pl.BlockSpec((CONV_HALO, D), lambda i: (0, 0)),
                   pl.BlockSpec((1, D), lambda i: (0, 0))],
        out_shape=[jax.ShapeDtypeStruct((S, D), BF16), jax.ShapeDtypeStruct((S, D), BF16),
                   jax.ShapeDtypeStruct((CONV_HALO, D), F32), jax.ShapeDtypeStruct((1, D), F32)],
        scratch_shapes=[pltpu.VMEM((T + CONV_HALO, D), F32), pltpu.VMEM((T + CONV_HALO, D), F32),
                        _shifted_scratch(T), _shifted_scratch(T), pltpu.VMEM((CONV_HALO * SUBLANES, D), F32)],
        args=[dc, dc, proj, proj, proj, proj, cw], sem=("arbitrary",), comm=comm)


def assemble_dproj0(dq, dkc, dkp, dvc, dvp, da, db, dgate, D):
    S = dq.shape[0]
    tq = _attn_tq(S)
    T = _pick(S, (256,))
    shift = tq // T
    nt = S // T

    def body(dq_ref, dkc_ref, dkp_ref, dvc_ref, dvp_ref, da_ref, db_ref, dg_ref, o_ref):
        i = pl.program_id(0)
        last = i + shift >= nt
        o_ref[:, 0:D] = dq_ref[...]
        dk = dkc_ref[...].astype(F32) + jnp.where(last, 0.0, dkp_ref[...].astype(F32))
        dv = dvc_ref[...].astype(F32) + jnp.where(last, 0.0, dvp_ref[...].astype(F32))
        o_ref[:, D:2 * D] = dk.astype(o_ref.dtype)
        o_ref[:, 2 * D:3 * D] = dv.astype(o_ref.dtype)
        o_ref[:, 3 * D:4 * D] = da_ref[...]
        o_ref[:, 4 * D:5 * D] = db_ref[...]
        o_ref[:, 5 * D:] = dg_ref[...]

    row = pl.BlockSpec((T, D), lambda i: (i, 0))
    nxt = pl.BlockSpec((T, D), lambda i: (jnp.minimum(i + shift, nt - 1), 0))
    return pl.pallas_call(
        body, grid=(nt,),
        in_specs=[row, row, nxt, row, nxt, row, row, pl.BlockSpec((T, 2 * D), lambda i: (i, 0))],
        out_specs=pl.BlockSpec((T, 7 * D), lambda i: (i, 0)),
        out_shape=jax.ShapeDtypeStruct((S, 7 * D), BF16),
        name="assemble_dproj0", compiler_params=_cparams("parallel"))(dq, dkc, dkp, dvc, dvp, da, db, dgate)


def _sgu_t(S):
    return _pick(S, (256, 128))


def _ws_masked(ws_ref, g):
    row = lax.broadcasted_iota(jnp.int32, (GMLP_CHUNK, GMLP_CHUNK), 0) // CHUNK
    col = lax.broadcasted_iota(jnp.int32, (GMLP_CHUNK, GMLP_CHUNK), 1) // CHUNK
    return jnp.where(row >= col, ws_ref[g], 0.0), row >= col


def sgu_fwd(proj, lng, lnb, ws, bst, MIX):
    S = proj.shape[0]
    T = _sgu_t(S)
    gw = MIX // N_GROUPS_C

    def body(u_ref, v_ref, g_ref, lng_ref, lnb_ref, ws_ref, bst_ref, y_ref):
        v = v_ref[...].astype(F32)
        mu = jnp.mean(v, axis=-1, keepdims=True)
        xc = v - mu
        rstd = lax.rsqrt(jnp.mean(xc * xc, axis=-1, keepdims=True) + EPS)
        for g in range(N_GROUPS_C):
            cols = slice(g * gw, (g + 1) * gw)
            wsm = _ws_masked(ws_ref, g)[0].astype(BF16)
            vn = (xc[:, cols] * rstd * lng_ref[:, cols] + lnb_ref[:, cols]).astype(BF16)
            for blk in range(T // GMLP_CHUNK):
                rows = slice(blk * GMLP_CHUNK, (blk + 1) * GMLP_CHUNK)
                sg = _dot(wsm, vn[rows], NN) + bst_ref[:, g:g + 1]
                gate = g_ref[rows, cols].astype(F32)
                y = u_ref[rows, cols].astype(F32) * sg * (gate * _sigmoid(gate))
                y_ref[rows, cols] = y.astype(y_ref.dtype)

    def part(cidx):
        return pl.BlockSpec((T, MIX), lambda i: (i, cidx))

    vec = pl.BlockSpec((1, MIX), lambda i: (0, 0))
    return pl.pallas_call(
        body, grid=(S // T,),
        in_specs=[part(0), part(1), part(2), vec, vec,
                  pl.BlockSpec((N_GROUPS_C, GMLP_CHUNK, GMLP_CHUNK), lambda i: (0, 0, 0)),
                  pl.BlockSpec((GMLP_CHUNK, N_GROUPS_C), lambda i: (0, 0))],
        out_specs=pl.BlockSpec((T, MIX), lambda i: (i, 0)),
        out_shape=jax.ShapeDtypeStruct((S, MIX), BF16),
        name="sgu_fwd", compiler_params=_cparams("parallel"))(proj, proj, proj, lng, lnb, ws, bst)


def sgu_bwd(dy1, proj, lng, lnb, ws, bst, MIX):
    S = proj.shape[0]
    T = _sgu_t(S)
    gw = MIX // N_GROUPS_C

    def body(dy_ref, u_ref, v_ref, g_ref, lng_ref, lnb_ref, ws_ref, bst_ref,
             dp_ref, dws_ref, dbst_ref, dlng_ref, dlnb_ref, dvn_buf):
        i = pl.program_id(0)

        @pl.when(i == 0)
        def _():
            dws_ref[...] = jnp.zeros_like(dws_ref)
            dbst_ref[...] = jnp.zeros_like(dbst_ref)
            dlng_ref[...] = jnp.zeros_like(dlng_ref)
            dlnb_ref[...] = jnp.zeros_like(dlnb_ref)

        v = v_ref[...].astype(F32)
        mu = jnp.mean(v, axis=-1, keepdims=True)
        xc = v - mu
        rstd = lax.rsqrt(jnp.mean(xc * xc, axis=-1, keepdims=True) + EPS)
        for g in range(N_GROUPS_C):
            cols = slice(g * gw, (g + 1) * gw)
            wsf, keep = _ws_masked(ws_ref, g)
            wsm = wsf.astype(BF16)
            vn = (xc[:, cols] * rstd * lng_ref[:, cols] + lnb_ref[:, cols]).astype(BF16)
            for blk in range(T // GMLP_CHUNK):
                rows = slice(blk * GMLP_CHUNK, (blk + 1) * GMLP_CHUNK)
                vnb = vn[rows]
                sg = _dot(wsm, vnb, NN) + bst_ref[:, g:g + 1]
                gate = g_ref[rows, cols].astype(F32)
                sig = _sigmoid(gate)
                sil = gate * sig
                u = u_ref[rows, cols].astype(F32)
                dy = dy_ref[rows, cols].astype(F32)
                dp_ref[rows, g * gw:(g + 1) * gw] = (dy * sg * sil).astype(dp_ref.dtype)
                dp_ref[rows, 2 * MIX + g * gw:2 * MIX + (g + 1) * gw] = (
                    dy * u * sg * (sig * (1.0 + gate * (1.0 - sig)))).astype(dp_ref.dtype)
                dsg = dy * u * sil
                dsgb = dsg.astype(BF16)
                dvn_buf[rows, cols] = _dot(wsm, dsgb, TN)
                dws_ref[g] += jnp.where(keep, _dot(dsgb, vnb, NT), 0.0)
                dbst_ref[:, g:g + 1] += jnp.sum(dsg, axis=-1, keepdims=True)
        dvn = dvn_buf[...]
        xhat = xc * rstd
        dxhat = dvn * lng_ref[...]
        dv = rstd * (dxhat - jnp.mean(dxhat, axis=-1, keepdims=True)
                     - xhat * jnp.mean(dxhat * xhat, axis=-1, keepdims=True))
        dp_ref[:, MIX:2 * MIX] = dv.astype(dp_ref.dtype)
        dlng_ref[...] += jnp.sum(dvn * xhat, axis=0, keepdims=True)
        dlnb_ref[...] += jnp.sum(dvn, axis=0, keepdims=True)

    def part(cidx):
        return pl.BlockSpec((T, MIX), lambda i: (i, cidx))

    vec = pl.BlockSpec((1, MIX), lambda i: (0, 0))
    wspec = pl.BlockSpec((N_GROUPS_C, GMLP_CHUNK, GMLP_CHUNK), lambda i: (0, 0, 0))
    bspec = pl.BlockSpec((GMLP_CHUNK, N_GROUPS_C), lambda i: (0, 0))
    return pl.pallas_call(
        body, grid=(S // T,),
        in_specs=[pl.BlockSpec((T, MIX), lambda i: (i, 0)), part(0), part(1), part(2), vec, vec, wspec, bspec],
        out_specs=[pl.BlockSpec((T, 3 * MIX), lambda i: (i, 0)), wspec, bspec, vec, vec],
        out_shape=[jax.ShapeDtypeStruct((S, 3 * MIX), BF16),
                   jax.ShapeDtypeStruct((N_GROUPS_C, GMLP_CHUNK, GMLP_CHUNK), F32),
                   jax.ShapeDtypeStruct((GMLP_CHUNK, N_GROUPS_C), F32),
                   jax.ShapeDtypeStruct((1, MIX), F32), jax.ShapeDtypeStruct((1, MIX), F32)],
        scratch_shapes=[pltpu.VMEM((T, MIX), F32)],
        name="sgu_bwd", compiler_params=_cparams("arbitrary"))(dy1, proj, proj, proj, lng, lnb, ws, bst)


def xattn_fwd(name, q, k, v):
    S, D = q.shape
    nm = k.shape[0]
    dh = D // N_HEADS_X
    tq = _pick(S, (512, 256))
    scale = dh ** -0.5

    def body(q_ref, k_ref, v_ref, o_ref, lse_ref):
        s = _dot(q_ref[...], k_ref[...], NT) * scale
        m = jnp.max(s, axis=-1, keepdims=True)
        p = jnp.exp(s - m)
        l = jnp.sum(p, axis=-1, keepdims=True)
        o_ref[...] = (_dot(p.astype(BF16), v_ref[...], NN) / l).astype(o_ref.dtype)
        lse_ref[...] = m + jnp.log(l)

    return pl.pallas_call(
        body, grid=(N_HEADS_X, S // tq),
        in_specs=[pl.BlockSpec((tq, dh), lambda h, i: (i, h)),
                  pl.BlockSpec((nm, dh), lambda h, i: (0, h)), pl.BlockSpec((nm, dh), lambda h, i: (0, h))],
        out_specs=[pl.BlockSpec((tq, dh), lambda h, i: (i, h)),
                   pl.BlockSpec((None, tq, 1), lambda h, i: (h, i, 0))],
        out_shape=[jax.ShapeDtypeStruct((S, D), BF16), jax.ShapeDtypeStruct((N_HEADS_X, S, 1), F32)],
        name=name, compiler_params=_cparams("parallel", "parallel"))(q, k, v)


def xattn_bwd(name, q, k, v, o, do, lse):
    S, D = q.shape
    nm = k.shape[0]
    dh = D // N_HEADS_X
    tq = _pick(S, (512, 256))
    scale = dh ** -0.5

    def body(q_ref, k_ref, v_ref, o_ref, do_ref, lse_ref, dq_ref, dk_ref, dv_ref):
        i = pl.program_id(1)
        q_v = q_ref[...]
        k_v = k_ref[...]
        do_v = do_ref[...]
        p = jnp.exp(_dot(q_v, k_v, NT) * scale - lse_ref[...])
        delta = jnp.sum(do_v.astype(F32) * o_ref[...].astype(F32), axis=-1, keepdims=True)
        dv = _dot(p.astype(BF16), do_v, TN)
        ds = (p * (_dot(do_v, v_ref[...], NT) - delta)).astype(BF16)
        dq_ref[...] = (_dot(ds, k_v, NN) * scale).astype(dq_ref.dtype)
        dk = _dot(ds, q_v, TN) * scale

        @pl.when(i == 0)
        def _():
            dk_ref[...] = dk
            dv_ref[...] = dv

        @pl.when(i > 0)
        def _():
            dk_ref[...] += dk
            dv_ref[...] += dv

    qs = pl.BlockSpec((tq, dh), lambda h, i: (i, h))
    ks = pl.BlockSpec((nm, dh), lambda h, i: (0, h))
    return pl.pallas_call(
        body, grid=(N_HEADS_X, S // tq),
        in_specs=[qs, ks, ks, qs, qs, pl.BlockSpec((None, tq, 1), lambda h, i: (h, i, 0))],
        out_specs=[qs, ks, ks],
        out_shape=[jax.ShapeDtypeStruct((S, D), BF16), jax.ShapeDtypeStruct((nm, D), F32),
                   jax.ShapeDtypeStruct((nm, D), F32)],
        name=name, compiler_params=_cparams("parallel", "arbitrary"))(q, k, v, o, do, lse)


def adamw(name, w, g, m, v):
    R, C = w.shape
    tr = _pick(R, tuple(t for t in (512, 256, 128, 64, 32, 16, 8) if t * C * 4 <= (1 << 20)) or (8,))
    c1 = 1.0 - ADAM_B1 ** ADAM_STEP
    c2 = 1.0 - ADAM_B2 ** ADAM_STEP

    def body(w_ref, g_ref, m_ref, v_ref, d_ref, nm_ref, nv_ref):
        gv = g_ref[...]
        nm = ADAM_B1 * m_ref[...] + (1.0 - ADAM_B1) * gv
        nv = ADAM_B2 * v_ref[...] + (1.0 - ADAM_B2) * (gv * gv)
        d_ref[...] = -ADAM_LR * ((nm / c1) / (jnp.sqrt(nv / c2) + ADAM_EPS) + ADAM_WD * w_ref[...])
        nm_ref[...] = nm
        nv_ref[...] = nv

    blk = pl.BlockSpec((tr, C), lambda i: (i, 0))
    sd = jax.ShapeDtypeStruct((R, C), F32)
    return pl.pallas_call(body, grid=(R // tr,), in_specs=[blk] * 4, out_specs=[blk] * 3,
                          out_shape=[sd, sd, sd], name=name, compiler_params=_cparams("parallel"))(w, g, m, v)


def add_halves(name, g4, recv, cidx):
    _, R, C = g4.shape
    rh = R // 2
    tr = _pick(rh, (256, 128, 64, 32, 16))
    nrb = rh // tr

    def body(c_ref, a_ref, b_ref, o_ref):
        o_ref[...] = (a_ref[...].astype(F32) + b_ref[...].astype(F32)).astype(o_ref.dtype)

    grid_spec = pltpu.PrefetchScalarGridSpec(
        num_scalar_prefetch=1, grid=(4, nrb),
        in_specs=[pl.BlockSpec((None, tr, C), lambda j, r, c_ref: (j, c_ref[0] * nrb + r, 0)),
                  pl.BlockSpec((None, tr, C), lambda j, r, c_ref: (j, r, 0))],
        out_specs=pl.BlockSpec((None, tr, C), lambda j, r, c_ref: (j, r, 0)))
    return pl.pallas_call(body, grid_spec=grid_spec, out_shape=jax.ShapeDtypeStruct((4, rh, C), BF16),
                          name=name, compiler_params=_cparams("parallel", "parallel"))(cidx, g4, recv)


def sum_chips(name, own, recv, place):
    _, rh, C = own.shape
    tr = _pick(rh, (256, 128, 64, 32, 16))
    nrb = rh // tr

    def body(s_ref, own_ref, recv_ref, o_ref):
        acc = own_ref[...].astype(F32)
        for k in range(N_CHIPS - 1):
            acc = acc + recv_ref[k].astype(F32)
        o_ref[...] = acc

    grid_spec = pltpu.PrefetchScalarGridSpec(
        num_scalar_prefetch=1, grid=(nrb,),
        in_specs=[pl.BlockSpec((None, tr, C), lambda r, s: (s[0], r, 0)),
                  pl.BlockSpec((N_CHIPS - 1, tr, C), lambda r, s: (0, r, 0))],
        out_specs=pl.BlockSpec((tr, C), lambda r, s: (s[1] * nrb + r, 0)))
    return pl.pallas_call(body, grid_spec=grid_spec, out_shape=jax.ShapeDtypeStruct((2 * rh, C), F32),
                          name=name, compiler_params=_cparams("parallel"))(place, own, recv)


def cast_into_slot(name, w, place):
    R, C = w.shape
    tr = _pick(R, (256, 128, 64, 32, 16))

    def body(s_ref, w_ref, o_ref):
        o_ref[...] = w_ref[...].astype(o_ref.dtype)

    grid_spec = pltpu.PrefetchScalarGridSpec(
        num_scalar_prefetch=1, grid=(R // tr,),
        in_specs=[pl.BlockSpec((tr, C), lambda r, s: (r, 0))],
        out_specs=pl.BlockSpec((None, tr, C), lambda r, s: (s[0], r, 0)))
    return pl.pallas_call(body, grid_spec=grid_spec, out_shape=jax.ShapeDtypeStruct((N_CHIPS, R, C), BF16),
                          name=name, compiler_params=_cparams("parallel"))(place, w)


def _place():
    return lax.axis_index("x"), lax.axis_index("y"), lax.axis_index("c")


_CHIP_FLIPS = ((1, 0), (0, 1), (1, 1))


def _flip(v, bit):
    return 1 - v if bit else v


HBM_SPEC = pl.BlockSpec(memory_space=pl.ANY)


def exchange_small(name, buf, reduce):
    R = buf.shape[0]

    def body(x_ref, *refs):
        if reduce:
            sum_ref, all_ref, send_sems, recv_sems, local_sem = refs
        else:
            all_ref, send_sems, recv_sems, local_sem = refs
        x, y, c = _place()
        me = 4 * x + 2 * y + c
        mine = pltpu.make_async_copy(x_ref, all_ref.at[me], local_sem)
        mine.start()
        sends = []
        for k in range(1, N_DEV):
            peer = (_flip(x, k & 4), _flip(y, k & 2), _flip(c, k & 1))
            cp = pltpu.make_async_remote_copy(src_ref=x_ref, dst_ref=all_ref.at[me], send_sem=send_sems.at[k - 1],
                                              recv_sem=recv_sems.at[k - 1], device_id=peer, device_id_type=MESH)
            cp.start()
            sends.append(cp)
        for k in range(1, N_DEV):
            peer = (_flip(x, k & 4), _flip(y, k & 2), _flip(c, k & 1))
            src = 4 * peer[0] + 2 * peer[1] + peer[2]
            pltpu.make_async_remote_copy(src_ref=x_ref, dst_ref=all_ref.at[src], send_sem=send_sems.at[k - 1],
                                         recv_sem=recv_sems.at[k - 1], device_id=peer,
                                         device_id_type=MESH).wait_recv()
        for cp in sends:
            cp.wait_send()
        mine.wait()
        if reduce:
            acc = all_ref[0]
            for d in range(1, N_DEV):
                acc = acc + all_ref[d]
            sum_ref[...] = acc

    vm = pl.BlockSpec(memory_space=pltpu.VMEM)
    sems = [pltpu.SemaphoreType.DMA((N_DEV - 1,)), pltpu.SemaphoreType.DMA((N_DEV - 1,)), pltpu.SemaphoreType.DMA]
    if reduce:
        return pl.pallas_call(
            body, in_specs=[vm], out_specs=vm, out_shape=jax.ShapeDtypeStruct((R, LANES), F32),
            scratch_shapes=[pltpu.VMEM((N_DEV, R, LANES), F32)] + sems, name=name,
            compiler_params=pltpu.CompilerParams(vmem_limit_bytes=V7X_VMEM_LIMIT))(buf)
    return pl.pallas_call(
        body, in_specs=[vm], out_specs=vm, out_shape=jax.ShapeDtypeStruct((N_DEV, R, LANES), F32),
        scratch_shapes=sems, name=name,
        compiler_params=pltpu.CompilerParams(vmem_limit_bytes=V7X_VMEM_LIMIT))(buf)


def gather_job(slots):
    n = len(slots)

    def copies(o_refs, send_sems, recv_sems):
        x, y, c = _place()
        me = 2 * x + y
        sib = (x, y, 1 - c)
        chips = [(_flip(x, fx), _flip(y, fy)) for fx, fy in _CHIP_FLIPS]
        ici, fwd, from_sib = [], [], []
        for t in range(n):
            rh = o_refs[t].shape[1] // 2
            mine, theirs = pl.ds(c * rh, rh), pl.ds((1 - c) * rh, rh)
            for k, (px, py) in enumerate(chips):
                own = o_refs[t].at[me, mine]
                ici.append(pltpu.make_async_remote_copy(
                    src_ref=own, dst_ref=own, send_sem=send_sems.at[t, k], recv_sem=recv_sems.at[t, k],
                    device_id=(px, py, c), device_id_type=MESH))
                landed = o_refs[t].at[2 * px + py, mine]
                arrival = pltpu.make_async_remote_copy(
                    src_ref=landed, dst_ref=landed, send_sem=send_sems.at[t, k], recv_sem=recv_sems.at[t, k],
                    device_id=(px, py, c), device_id_type=MESH)
                fwd.append((arrival, pltpu.make_async_remote_copy(
                    src_ref=landed, dst_ref=landed, send_sem=send_sems.at[t, 3 + k],
                    recv_sem=recv_sems.at[t, 3 + k], device_id=sib, device_id_type=MESH)))
                passed = o_refs[t].at[2 * px + py, theirs]
                from_sib.append(pltpu.make_async_remote_copy(
                    src_ref=passed, dst_ref=passed, send_sem=send_sems.at[t, 3 + k],
                    recv_sem=recv_sems.at[t, 3 + k], device_id=sib, device_id_type=MESH))
        return ici, fwd, from_sib

    def start(ins, o_refs, sems):
        for cp in copies(o_refs, *sems)[0]:
            cp.start()

    def relay(ins, o_refs, sems):
        for arrival, forward in copies(o_refs, *sems)[1]:
            arrival.wait_recv()
            forward.start()

    def finish(ins, o_refs, sems):
        ici, fwd, from_sib = copies(o_refs, *sems)
        for cp in from_sib:
            cp.wait_recv()
        for cp in ici:
            cp.wait_send()
        for _, forward in fwd:
            forward.wait_send()

    return _Comm(slots, [jax.ShapeDtypeStruct(s.shape, s.dtype) for s in slots], {t: t for t in range(n)},
                 [pltpu.SemaphoreType.DMA((n, 6)), pltpu.SemaphoreType.DMA((n, 6))], start, finish, relay)


def sibling_halves_job(grads):
    n = len(grads)

    def copies(g_refs, o_refs, send_sems, recv_sems):
        x, y, c = _place()
        out = []
        for t in range(n):
            rh = g_refs[t].shape[1] // 2
            out.append(pltpu.make_async_remote_copy(
                src_ref=g_refs[t].at[:, pl.ds((1 - c) * rh, rh), :], dst_ref=o_refs[t],
                send_sem=send_sems.at[t], recv_sem=recv_sems.at[t], device_id=(x, y, 1 - c),
                device_id_type=MESH))
        return out

    def start(g_refs, o_refs, sems):
        for cp in copies(g_refs, o_refs, *sems):
            cp.start()

    def finish(g_refs, o_refs, sems):
        cps = copies(g_refs, o_refs, *sems)
        for cp in cps:
            cp.wait_recv()
        for cp in cps:
            cp.wait_send()

    return _Comm(grads, [jax.ShapeDtypeStruct((4, g.shape[1] // 2, g.shape[2]), g.dtype) for g in grads], {},
                 [pltpu.SemaphoreType.DMA((n,)), pltpu.SemaphoreType.DMA((n,))], start, finish)


def scatter_job(parts):
    n = len(parts)

    def copies(p_refs, o_refs, send_sems, recv_sems):
        x, y, c = _place()
        out = []
        for t in range(n):
            for k, (fx, fy) in enumerate(_CHIP_FLIPS):
                px, py = _flip(x, fx), _flip(y, fy)
                out.append(pltpu.make_async_remote_copy(
                    src_ref=p_refs[t].at[2 * px + py], dst_ref=o_refs[t].at[k],
                    send_sem=send_sems.at[t, k], recv_sem=recv_sems.at[t, k],
                    device_id=(px, py, c), device_id_type=MESH))
        return out

    def start(p_refs, o_refs, sems):
        for cp in copies(p_refs, o_refs, *sems):
            cp.start()

    def finish(p_refs, o_refs, sems):
        cps = copies(p_refs, o_refs, *sems)
        for cp in cps:
            cp.wait_recv()
        for cp in cps:
            cp.wait_send()

    return _Comm(parts, [jax.ShapeDtypeStruct((N_CHIPS - 1,) + p.shape[1:], p.dtype) for p in parts], {},
                 [pltpu.SemaphoreType.DMA((n, 3)), pltpu.SemaphoreType.DMA((n, 3))], start, finish)


def share_halves_job(halves):
    n = len(halves)

    def copies(o_refs, send_sems, recv_sems):
        x, y, c = _place()
        sends, arrivals = [], []
        for t in range(n):
            rh = o_refs[t].shape[0] // 2
            mine = o_refs[t].at[pl.ds(c * rh, rh)]
            theirs = o_refs[t].at[pl.ds((1 - c) * rh, rh)]
            sends.append(pltpu.make_async_remote_copy(
                src_ref=mine, dst_ref=mine, send_sem=send_sems.at[t], recv_sem=recv_sems.at[t],
                device_id=(x, y, 1 - c), device_id_type=MESH))
            arrivals.append(pltpu.make_async_remote_copy(
                src_ref=theirs, dst_ref=theirs, send_sem=send_sems.at[t], recv_sem=recv_sems.at[t],
                device_id=(x, y, 1 - c), device_id_type=MESH))
        return sends, arrivals

    def start(ins, o_refs, sems):
        for cp in copies(o_refs, *sems)[0]:
            cp.start()

    def finish(ins, o_refs, sems):
        sends, arrivals = copies(o_refs, *sems)
        for cp in arrivals:
            cp.wait_recv()
        for cp in sends:
            cp.wait_send()

    return _Comm(halves, [jax.ShapeDtypeStruct(h.shape, h.dtype) for h in halves], {t: t for t in range(n)},
                 [pltpu.SemaphoreType.DMA((n,)), pltpu.SemaphoreType.DMA((n,))], start, finish)


def _pack(arrs):
    flat = []
    for a in arrs:
        v = a.reshape(-1).astype(F32)
        pad = (-v.shape[0]) % (8 * LANES)
        flat.append(jnp.pad(v, (0, pad)))
    return jnp.concatenate(flat).reshape(-1, LANES)


def _unpack(buf, shapes):
    out, off = [], 0
    flat = buf.reshape(-1)
    for s in shapes:
        n = int(np.prod(s))
        out.append(flat[off:off + n].reshape(s))
        off += n + ((-n) % (8 * LANES))
    return out


def _xattn_layer_fwd(tag, h, mem, gx, gmem, w):
    hx = rms_fwd(f"rms_x{tag}", h, gx)
    memn = rms_fwd(f"rms_mem{tag}", mem, gmem)
    q = mm_nn(f"xq{tag}", hx, w["q"], BF16)
    k = mm_nn(f"xk{tag}", memn, w["k"], BF16)
    v = mm_nn(f"xv{tag}", memn, w["v"], BF16)
    o, lse = xattn_fwd(f"xattn_fwd{tag}", q, k, v)
    h_out = mm_nn(f"xo{tag}", o, w["o"], F32, res=h)
    return h_out, dict(hx=hx, memn=memn, q=q, k=k, v=v, o=o, lse=lse)


def _xattn_layer_bwd(tag, dh_out, dh_out_b, h_in, mem, gx, gmem, w, sv):
    do = mm_nt(f"d_xo{tag}", dh_out_b, w["o"], BF16)
    dwo = mm_tn(f"dw_xo{tag}", sv["o"], dh_out_b)
    dq, dk, dv = xattn_bwd(f"xattn_bwd{tag}", sv["q"], sv["k"], sv["v"], sv["o"], do, sv["lse"])
    dwq = mm_tn(f"dw_xq{tag}", sv["hx"], dq)
    dhx = mm_nt(f"d_xq{tag}", dq, w["q"], F32)
    dwk = mm_tn(f"dw_xk{tag}", sv["memn"], dk)
    dwv = mm_tn(f"dw_xv{tag}", sv["memn"], dv)
    dmk = mm_nt(f"d_xk{tag}", dk, w["k"], F32)
    dmv = mm_nt(f"d_xv{tag}", dv, w["v"], F32)
    dh_in, dh_in_b, dgx = rms_bwd(f"rms_x_bwd{tag}", h_in, gx, [dhx], dh_out)
    _, _, dgmem = rms_bwd(f"rms_mem_bwd{tag}", mem, gmem, [dmk, dmv], None)
    return dh_in, dh_in_b, dgx, dgmem, dict(q=dwq, k=dwk, v=dwv, o=dwo)


def kernel(x, mem, norm_mix_g, norm_x_g, norm_mem_g, final_norm_g, w_in_ab, rel_bias, conv_w, conv_b, conv_ln_g, conv_ln_b, w_out_ab, w_in_c, sgu_ln_g, sgu_ln_b, w_s, b_s, w_out_c, w_xq, w_xk, w_xv, w_xo, loss_target, m_norm_mix_g, m_norm_x_g, m_norm_mem_g, m_final_norm_g, m_w_in_ab, m_rel_bias, m_conv_w, m_conv_b, m_conv_ln_g, m_conv_ln_b, m_w_out_ab, m_w_in_c, m_sgu_ln_g, m_sgu_ln_b, m_w_s, m_b_s, m_w_out_c, m_w_xq, m_w_xk, m_w_xv, m_w_xo, v_norm_mix_g, v_norm_x_g, v_norm_mem_g, v_final_norm_g, v_w_in_ab, v_rel_bias, v_conv_w, v_conv_b, v_conv_ln_g, v_conv_ln_b, v_w_out_ab, v_w_in_c, v_sgu_ln_g, v_sgu_ln_b, v_w_s, v_b_s, v_w_out_c, v_w_xq, v_w_xk, v_w_xv, v_w_xo):
    S, D = x.shape[1], x.shape[2]
    MIX = 2 * D
    xs, mems, tgt = x[0], mem[0], loss_target[0]
    cx, cy, cc = _place()
    chip = 2 * cx + cy
    cidx = jnp.reshape(cc, (1,)).astype(jnp.int32)
    place = jnp.stack([chip, cc]).astype(jnp.int32)

    ro, rq = MIX // 4, D // 4
    row_sharded = [("out_ab", w_out_ab[0]), ("out_c", w_out_c[0])]
    for layer in range(2):
        for nm_, w in (("q", w_xq), ("k", w_xk), ("v", w_xv), ("o", w_xo)):
            row_sharded.append((f"x{nm_}{layer}", w[layer]))
    slots = {"in_ab": cast_into_slot("cast_in_ab", w_in_ab[0], place),
             "in_c": cast_into_slot("cast_in_c", w_in_c[0], place)}
    slots.update({nm_: cast_into_slot("cast_" + nm_, w, place) for nm_, w in row_sharded})

    small_sh = [conv_w[0], sgu_ln_g[0], sgu_ln_b[0]]
    gathered = exchange_small("gather_small", _pack(small_sh), reduce=False)
    per_chip = [_unpack(gathered[2 * j], [a.shape for a in small_sh]) for j in range(N_CHIPS)]
    conv_w_full = jnp.concatenate([p[0] for p in per_chip], axis=1)
    sgu_g_full = jnp.concatenate([p[1] for p in per_chip], axis=0).reshape(1, MIX)
    sgu_b_full = jnp.concatenate([p[2] for p in per_chip], axis=0).reshape(1, MIX)
    cw_pad = jnp.pad(conv_w_full, ((0, CONV_HALO - CONV_WIDTH), (0, 0)))
    cb = conv_b.reshape(1, D)
    clg, clb = conv_ln_g.reshape(1, D), conv_ln_b.reshape(1, D)
    ws = w_s[0]
    bst = jnp.transpose(b_s[0])
    tq = _attn_tq(S)
    bm = band_bias_table(rel_bias[0], tq)

    hn0 = rms_fwd("rms_mix0", xs, norm_mix_g[0])
    (wab4,) = run_comm("gather_in_ab", gather_job([slots["in_ab"]]))
    layer0 = ["out_ab", "xq0", "xk0", "xv0", "xo0"]
    layer1 = ["out_c", "xq1", "xk1", "xv1", "xo1"]
    proj0, got0 = mm_nn_cols("proj_ab", hn0, wab4, BF16, comm=gather_job([slots[n] for n in layer0]))
    (ya, lse_a), (wc4,) = attn_fwd(proj0, bm, D, comm=gather_job([slots["in_c"]]))
    (y0, cpre), got1 = conv_gate_fwd(proj0, ya, cw_pad, cb, clg, clb, D,
                                     comm=gather_job([slots[n] for n in layer1]))
    wrow = {n: g.reshape(-1, g.shape[2]) for n, g in zip(layer0 + layer1, got0 + got1)}
    wx = [{k: wrow[f"x{k}{layer}"] for k in "qkvo"} for layer in range(2)]
    h1 = mm_nn("out_ab", y0, wrow["out_ab"], F32, res=xs)
    h2, sx0 = _xattn_layer_fwd("0", h1, mems, norm_x_g[0], norm_mem_g[0], wx[0])
    hn1 = rms_fwd("rms_mix1", h2, norm_mix_g[1])
    proj1 = mm_nn_cols("proj_c", hn1, wc4, BF16)
    y1 = sgu_fwd(proj1, sgu_g_full, sgu_b_full, ws, bst, MIX)
    h3 = mm_nn("out_c", y1, wrow["out_c"], F32, res=h2)
    h4, sx1 = _xattn_layer_fwd("1", h3, mems, norm_x_g[1], norm_mem_g[1], wx[1])
    loss_row, dg_final, dh4, dh4b = loss_head("loss_head", h4, final_norm_g, tgt)

    dh3, dh3b, dgx1, dgmem1, dwx1 = _xattn_layer_bwd("1", dh4, dh4b, h3, mems, norm_x_g[1], norm_mem_g[1], wx[1], sx1)
    def stack_rows(dw_out, dwx):
        return jnp.concatenate([g.reshape(N_CHIPS, -1, g.shape[1]) for g in [dw_out] + [dwx[k] for k in "qkvo"]],
                               axis=1)

    dy1 = mm_nt("d_out_c", dh3b, wrow["out_c"], BF16)
    dw_out_c = mm_tn("dw_out_c", y1, dh3b)
    dproj1, dws, dbst, dsgu_g, dsgu_b = sgu_bwd(dy1, proj1, sgu_g_full, sgu_b_full, ws, bst, MIX)
    grp1 = stack_rows(dw_out_c, dwx1)
    dw_in_c, (sib1,) = mm_tn_cols("dw_in_c", hn1, dproj1, comm=sibling_halves_job([grp1]))
    part1 = add_halves("add_halves1", grp1, sib1, cidx)
    dhn1, (recv1, sib2) = mm_nt_cols("d_proj_c", dproj1, wc4, F32,
                                     comm=_join(scatter_job([part1]), sibling_halves_job([dw_in_c])))
    part2 = add_halves("add_halves2", dw_in_c, sib2, cidx)
    dh2, dh2b, dgmix1 = rms_bwd("rms_mix1_bwd", h2, norm_mix_g[1], [dhn1], dh3)
    dh1, dh1b, dgx0, dgmem0, dwx0 = _xattn_layer_bwd("0", dh2, dh2b, h1, mems, norm_x_g[0], norm_mem_g[0], wx[0], sx0)
    dy0 = mm_nt("d_out_ab", dh1b, wrow["out_ab"], BF16)
    dw_out_ab = mm_tn("dw_out_ab", y0, dh1b)
    grp3 = stack_rows(dw_out_ab, dwx0)
    dya, dgate, dc, dclg, dclb = conv_gate_bwd_a(dy0, proj0, ya, cpre, clg, clb, D)
    (da, db, dcw, dcb), (recv2, sib3) = conv_gate_bwd_b(
        dc, proj0, cw_pad, D, comm=_join(scatter_job([part2]), sibling_halves_job([grp3])))
    part3 = add_halves("add_halves3", grp3, sib3, cidx)
    (dq, dkc, dkp, dvc, dvp, ds_sum), (recv3,) = attn_bwd(proj0, ya, dya, lse_a, bm, D, comm=scatter_job([part3]))
    drel = rel_bias_grad(ds_sum)
    dproj0 = assemble_dproj0(dq, dkc, dkp, dvc, dvp, da, db, dgate, D)
    dw_in_ab = mm_tn_cols("dw_in_ab", hn0, dproj0)
    (sib4,) = run_comm("sibling_halves4", sibling_halves_job([dw_in_ab]))
    part4 = add_halves("add_halves4", dw_in_ab, sib4, cidx)
    dhn0, (recv4,) = mm_nt_cols("d_proj_ab", dproj0, wab4, F32, comm=scatter_job([part4]))
    dx, _, dgmix0 = rms_bwd("rms_mix0_bwd", xs, norm_mix_g[0], [dhn0], dh1)
    halves = [sum_chips(f"sum_chips{t + 1}", p, r, place)
              for t, (p, r) in enumerate(((part1, recv1), (part2, recv2), (part3, recv3), (part4, recv4)))]
    g_r1, g_c, g_r0, g_ab = run_comm("share_reduced_halves", share_halves_job(halves))

    small_full = [
        jnp.concatenate([dgmix0, dgmix1], axis=0), jnp.concatenate([dgx0, dgx1], axis=0),
        jnp.concatenate([dgmem0, dgmem1], axis=0), dg_final.reshape(D), drel[None],
        dcb, dclg, dclb, dws[None], jnp.transpose(dbst)[None],
        dcw[:CONV_WIDTH][None], dsgu_g, dsgu_b]
    summed = _unpack(exchange_small("reduce_small", _pack(small_full), reduce=True), [a.shape for a in small_full])
    (g_norm_mix, g_norm_x, g_norm_mem, g_final, g_rel, g_conv_b, g_clg, g_clb, g_ws, g_bs,
     g_conv_w_full, g_sgu_g_full, g_sgu_b_full) = summed
    cws = conv_w.shape[2]
    g_conv_w = lax.dynamic_slice_in_dim(g_conv_w_full, chip * cws, cws, axis=2)
    sgs = sgu_ln_g.shape[1]
    g_sgu_g = lax.dynamic_slice_in_dim(g_sgu_g_full, chip * sgs, sgs, axis=1)
    g_sgu_b = lax.dynamic_slice_in_dim(g_sgu_b_full, chip * sgs, sgs, axis=1)

    loss = lax.psum(loss_row[0, 0], ("x", "y", "c"))

    g_rows = {"w_out_ab": g_r0[0:ro][None], "w_out_c": g_r1[0:ro][None]}
    for i, nm_ in enumerate("qkvo"):
        lo = ro + i * rq
        g_rows["w_x" + nm_] = jnp.stack([g_r0[lo:lo + rq], g_r1[lo:lo + rq]])
    grads = dict(
        norm_mix_g=g_norm_mix, norm_x_g=g_norm_x, norm_mem_g=g_norm_mem, final_norm_g=g_final,
        w_in_ab=g_ab[None], rel_bias=g_rel, conv_w=g_conv_w, conv_b=g_conv_b, conv_ln_g=g_clg, conv_ln_b=g_clb,
        w_out_ab=g_rows["w_out_ab"], w_in_c=g_c[None], sgu_ln_g=g_sgu_g, sgu_ln_b=g_sgu_b, w_s=g_ws, b_s=g_bs,
        w_out_c=g_rows["w_out_c"], w_xq=g_rows["w_xq"], w_xk=g_rows["w_xk"], w_xv=g_rows["w_xv"],
        w_xo=g_rows["w_xo"])
    weights = dict(
        norm_mix_g=(norm_mix_g, m_norm_mix_g, v_norm_mix_g), norm_x_g=(norm_x_g, m_norm_x_g, v_norm_x_g),
        norm_mem_g=(norm_mem_g, m_norm_mem_g, v_norm_mem_g), final_norm_g=(final_norm_g, m_final_norm_g, v_final_norm_g),
        w_in_ab=(w_in_ab, m_w_in_ab, v_w_in_ab), rel_bias=(rel_bias, m_rel_bias, v_rel_bias),
        conv_w=(conv_w, m_conv_w, v_conv_w), conv_b=(conv_b, m_conv_b, v_conv_b),
        conv_ln_g=(conv_ln_g, m_conv_ln_g, v_conv_ln_g), conv_ln_b=(conv_ln_b, m_conv_ln_b, v_conv_ln_b),
        w_out_ab=(w_out_ab, m_w_out_ab, v_w_out_ab), w_in_c=(w_in_c, m_w_in_c, v_w_in_c),
        sgu_ln_g=(sgu_ln_g, m_sgu_ln_g, v_sgu_ln_g), sgu_ln_b=(sgu_ln_b, m_sgu_ln_b, v_sgu_ln_b),
        w_s=(w_s, m_w_s, v_w_s), b_s=(b_s, m_b_s, v_b_s), w_out_c=(w_out_c, m_w_out_c, v_w_out_c),
        w_xq=(w_xq, m_w_xq, v_w_xq), w_xk=(w_xk, m_w_xk, v_w_xk), w_xv=(w_xv, m_w_xv, v_w_xv),
        w_xo=(w_xo, m_w_xo, v_w_xo))
    names = list(weights)
    big_names = ("w_in_ab", "w_out_ab", "w_in_c", "w_out_c", "w_xq", "w_xk", "w_xv", "w_xo")
    delta, new_m, new_v = {}, {}, {}
    for nm_ in big_names:
        w, m, v = weights[nm_]
        C = w.shape[-1]
        d2, m2, v2 = adamw("adamw_" + nm_, w.reshape(-1, C), grads[nm_].reshape(-1, C), m.reshape(-1, C),
                           v.reshape(-1, C))
        delta[nm_], new_m[nm_], new_v[nm_] = d2.reshape(w.shape), m2.reshape(w.shape), v2.reshape(w.shape)
    small_names = [n for n in names if n not in big_names]
    shapes = [weights[n][0].shape for n in small_names]
    d_s, m_s, v_s = adamw("adamw_small", _pack([weights[n][0] for n in small_names]),
                          _pack([grads[n] for n in small_names]), _pack([weights[n][1] for n in small_names]),
                          _pack([weights[n][2] for n in small_names]))
    for n, d_, m_, v_ in zip(small_names, _unpack(d_s, shapes), _unpack(m_s, shapes), _unpack(v_s, shapes)):
        delta[n], new_m[n], new_v[n] = d_, m_, v_

    return (loss, dx[None], *[grads[n].reshape(weights[n][0].shape) for n in names], *[delta[n] for n in names],
            *[new_m[n] for n in names], *[new_v[n] for n in names])
```

```python
import functools

import numpy as np
import jax
import jax.numpy as jnp
from jax import lax
from jax.experimental import pallas as pl
from jax.experimental.pallas import tpu as pltpu

F32 = jnp.float32
BF16 = jnp.bfloat16
MESH = pl.DeviceIdType.MESH

EPS = 1e-6
CHUNK = 64
N_PAST_CHUNKS = 8
MAX_REL = 128
HEAD_DIM_A = 128
CONV_WIDTH = 31
CONV_HALO = 32
GMLP_CHUNK = 128
N_GROUPS_C = 8
N_HEADS_X = 4
NEG = -1e30

ADAM_LR = 0.001
ADAM_B1 = 0.9
ADAM_B2 = 0.999
ADAM_EPS = 1e-08
ADAM_WD = 0.01
ADAM_STEP = 10

N_CHIPS = 4
N_DEV = 8
V7X_VMEM_LIMIT = 56 * 1024 * 1024
LANES = 128
SUBLANES = 8


def _pick(n, cands):
    for c in cands:
        if c <= n and n % c == 0:
            return c
    return n


def _cparams(*sem):
    return pltpu.CompilerParams(dimension_semantics=sem, vmem_limit_bytes=V7X_VMEM_LIMIT)


def _sigmoid(x):
    return 0.5 * jnp.tanh(0.5 * x) + 0.5


def _dot(a, b, contract):
    return lax.dot_general(a, b, (contract, ((), ())), preferred_element_type=F32)


NN = ((1,), (0,))
NT = ((1,), (1,))
TN = ((0,), (0,))


class _Comm:
    def __init__(self, arrays, out_shapes, aliases, sems, start, finish, relay=None, relay_frac=0.75):
        self.arrays, self.out_shapes, self.aliases, self.sems = list(arrays), list(out_shapes), dict(aliases), list(sems)
        self.start, self.finish, self.relay = start, finish, relay
        self.relay_frac = relay_frac


def _join(*jobs):
    assert all(j.relay is None for j in jobs)
    arrays, outs, sems, aliases, spans = [], [], [], {}, []
    for j in jobs:
        spans.append((len(arrays), len(outs), len(sems)))
        aliases.update({len(arrays) + i: len(outs) + o for i, o in j.aliases.items()})
        arrays += j.arrays
        outs += j.out_shapes
        sems += j.sems

    def part(j, span, ins, os_, ss):
        a0, o0, s0 = span
        return (ins[a0:a0 + len(j.arrays)], os_[o0:o0 + len(j.out_shapes)], ss[s0:s0 + len(j.sems)])

    def start(ins, os_, ss):
        for j, span in zip(jobs, spans):
            j.start(*part(j, span, ins, os_, ss))

    def finish(ins, os_, ss):
        for j, span in zip(jobs, spans):
            j.finish(*part(j, span, ins, os_, ss))

    return _Comm(arrays, outs, aliases, sems, start, finish)


def _call(body, *, name, grid, in_specs, out_specs, out_shape, args, scratch_shapes=(), sem=None, comm=None,
          prefetch=None, io_aliases=None):
    multi = isinstance(out_shape, (list, tuple))
    o_shapes = list(out_shape) if multi else [out_shape]
    o_specs = list(out_specs) if multi else [out_specs]
    if comm is None:
        assert prefetch is None and io_aliases is None
        return pl.pallas_call(body, grid=grid, in_specs=in_specs, out_specs=out_specs, out_shape=out_shape,
                              scratch_shapes=list(scratch_shapes), name=name,
                              compiler_params=_cparams(*sem))(*args)
    n_in, n_out, n_scr = len(in_specs), len(o_shapes), len(scratch_shapes)
    n_ci, n_co = len(comm.arrays), len(comm.out_shapes)
    n_steps = int(np.prod(grid))
    n_pre = 0 if prefetch is None else 1

    def carrier(*refs):
        pre, refs = refs[:n_pre], refs[n_pre:]
        ins, rest = refs[:n_in], refs[n_in:]
        cins, rest = rest[:n_ci], rest[n_ci:]
        outs, rest = rest[:n_out], rest[n_out:]
        couts, rest = rest[:n_co], rest[n_co:]
        scr, csems = rest[:n_scr], rest[n_scr:]
        step = 0
        for a, g in enumerate(grid):
            step = step * g + pl.program_id(a)

        @pl.when(step == 0)
        def _():
            comm.start(cins, couts, csems)

        body(*pre, *ins, *outs, *scr)

        relay_step = min(int(comm.relay_frac * n_steps), n_steps - 1)
        if comm.relay is not None and relay_step < n_steps - 1:
            @pl.when(step == relay_step)
            def _():
                comm.relay(cins, couts, csems)

        @pl.when(step == n_steps - 1)
        def _():
            if comm.relay is not None and relay_step == n_steps - 1:
                comm.relay(cins, couts, csems)
            comm.finish(cins, couts, csems)

    aliases = {n_pre + n_in + i: n_out + o for i, o in comm.aliases.items()}
    aliases.update({n_pre + i: o for i, o in (io_aliases or {}).items()})
    all_in = list(in_specs) + [HBM_SPEC] * n_ci
    all_out = o_specs + [HBM_SPEC] * n_co
    all_scratch = list(scratch_shapes) + comm.sems
    params = _cparams(*(["arbitrary"] * len(grid)))
    if prefetch is None:
        res = pl.pallas_call(
            carrier, grid=grid, in_specs=all_in, out_specs=all_out, out_shape=o_shapes + comm.out_shapes,
            input_output_aliases=aliases, scratch_shapes=all_scratch, name=name,
            compiler_params=params)(*args, *comm.arrays)
    else:
        grid_spec = pltpu.PrefetchScalarGridSpec(num_scalar_prefetch=1, grid=grid, in_specs=all_in,
                                                 out_specs=all_out, scratch_shapes=all_scratch)
        res = pl.pallas_call(
            carrier, grid_spec=grid_spec, out_shape=o_shapes + comm.out_shapes, input_output_aliases=aliases,
            name=name, compiler_params=params)(prefetch, *args, *comm.arrays)
    mine = list(res[:n_out]) if multi else res[0]
    return mine, list(res[n_out:])


def run_comm(name, comm):
    def body(*refs):
        n_ci, n_co = len(comm.arrays), len(comm.out_shapes)
        cins, couts, csems = refs[:n_ci], refs[n_ci:n_ci + n_co], refs[n_ci + n_co:]
        comm.start(cins, couts, csems)
        if comm.relay is not None:
            comm.relay(cins, couts, csems)
        comm.finish(cins, couts, csems)

    return pl.pallas_call(
        body, in_specs=[HBM_SPEC] * len(comm.arrays), out_specs=[HBM_SPEC] * len(comm.out_shapes),
        out_shape=comm.out_shapes, input_output_aliases=comm.aliases, scratch_shapes=comm.sems,
        name=name)(*comm.arrays)


def _mm(name, a, b, *, contract, grid, a_spec, b_spec, o_spec, out_shape, res=None, comm=None):
    nk = grid[2]

    def body(*refs):
        if res is not None:
            a_ref, b_ref, r_ref, o_ref = refs[:4]
        else:
            a_ref, b_ref, o_ref = refs[:3]
            r_ref = None
        p = _dot(a_ref[...].astype(BF16), b_ref[...].astype(BF16), contract)

        def finish(acc):
            if r_ref is not None:
                acc = acc + r_ref[...]
            o_ref[...] = acc.astype(o_ref.dtype)

        if nk == 1:
            finish(p)
        else:
            acc_ref = refs[-1]
            k = pl.program_id(2)

            @pl.when(k == 0)
            def _():
                acc_ref[...] = p

            @pl.when(k > 0)
            def _():
                acc_ref[...] += p

            @pl.when(k == nk - 1)
            def _():
                finish(acc_ref[...])

    in_specs = [a_spec, b_spec]
    args = [a, b]
    if res is not None:
        in_specs.append(o_spec)
        args.append(res)
    blk = tuple(d for d in o_spec.block_shape if d is not None)
    scratch = [] if nk == 1 else [pltpu.VMEM(blk, F32)]
    return _call(body, name=name, grid=grid, in_specs=in_specs, out_specs=o_spec, out_shape=out_shape,
                 args=args, scratch_shapes=scratch, sem=("parallel", "parallel", "arbitrary"), comm=comm)


def mm_nn_cols(name, a, w4, out_dtype, comm=None):
    M, K = a.shape
    _, _, C = w4.shape
    tm = _pick(M, (1024, 512, 256))
    tn = _pick(C, (512, 256, 128))
    nps = C // tn
    return _mm(name, a, w4, contract=NN, grid=(M // tm, 4 * nps, 1),
               a_spec=pl.BlockSpec((tm, K), lambda i, j, k: (i, 0)),
               b_spec=pl.BlockSpec((None, K, tn), lambda i, j, k: (j // nps, 0, j % nps)),
               o_spec=pl.BlockSpec((tm, tn), lambda i, j, k: (i, j)),
               out_shape=jax.ShapeDtypeStruct((M, 4 * C), out_dtype), comm=comm)


def proj_cols_own(name, a, w_own, place, comm):
    M, K = a.shape
    C = w_own.shape[1]
    tm = _pick(M, (1024, 512, 256))
    tn = _pick(C, (512, 256, 128))
    nps = C // tn

    def body(s_ref, a_ref, b_ref, o_ref):
        o_ref[...] = _dot(a_ref[...], b_ref[...].astype(BF16), NN).astype(o_ref.dtype)

    return _call(body, name=name, grid=(M // tm, nps),
                 in_specs=[pl.BlockSpec((tm, K), lambda i, j, s: (i, 0)),
                           pl.BlockSpec((K, tn), lambda i, j, s: (0, j))],
                 out_specs=pl.BlockSpec((tm, tn), lambda i, j, s: (i, s[0] * nps + j)),
                 out_shape=jax.ShapeDtypeStruct((M, N_CHIPS * C), BF16), args=[a, w_own], comm=comm,
                 prefetch=place)


def proj_cols_rest(name, a, w4, partial, place, comm):
    M, K = a.shape
    C = w4.shape[2]
    tm = _pick(M, (1024, 512, 256))
    tn = _pick(C, (512, 256, 128))
    nps = C // tn

    def slot(j, s):
        return (s[0] + 1 + j // nps) % N_CHIPS

    def body(s_ref, a_ref, b_ref, part_ref, o_ref):
        o_ref[...] = _dot(a_ref[...], b_ref[...], NN).astype(o_ref.dtype)

    return _call(body, name=name, grid=(M // tm, (N_CHIPS - 1) * nps),
                 in_specs=[pl.BlockSpec((tm, K), lambda i, j, s: (i, 0)),
                           pl.BlockSpec((None, K, tn), lambda i, j, s: (slot(j, s), 0, j % nps)),
                           HBM_SPEC],
                 out_specs=pl.BlockSpec((tm, tn), lambda i, j, s: (i, slot(j, s) * nps + j % nps)),
                 out_shape=jax.ShapeDtypeStruct(partial.shape, partial.dtype), args=[a, w4, partial],
                 comm=comm, prefetch=place, io_aliases={2: 0})


def mm_nn(name, a, w, out_dtype, res=None):
    M, K = a.shape
    N = w.shape[1]
    tm = _pick(M, (1024, 512, 256))
    tn = _pick(N, (512, 256, 128))
    return _mm(name, a, w, contract=NN, grid=(M // tm, N // tn, 1),
               a_spec=pl.BlockSpec((tm, K), lambda i, j, k: (i, 0)),
               b_spec=pl.BlockSpec((K, tn), lambda i, j, k: (0, j)),
               o_spec=pl.BlockSpec((tm, tn), lambda i, j, k: (i, j)),
               out_shape=jax.ShapeDtypeStruct((M, N), out_dtype), res=res)


def mm_nt_cols(name, a, w4, out_dtype, comm=None):
    M = a.shape[0]
    _, K, C = w4.shape
    tm = _pick(M, (1024, 512, 256))
    tn = _pick(K, (1024, 512, 256, 128))
    tk = _pick(C, (3584, 3072, 1792, 1536, 1024, 512, 256, 128))
    kps = C // tk
    return _mm(name, a, w4, contract=NT, grid=(M // tm, K // tn, 4 * kps),
               a_spec=pl.BlockSpec((tm, tk), lambda i, j, k: (i, k)),
               b_spec=pl.BlockSpec((None, tn, tk), lambda i, j, k: (k // kps, j, k % kps)),
               o_spec=pl.BlockSpec((tm, tn), lambda i, j, k: (i, j)),
               out_shape=jax.ShapeDtypeStruct((M, K), out_dtype), comm=comm)


def mm_nt(name, a, w, out_dtype):
    M, C = a.shape
    N = w.shape[0]
    tm = _pick(M, (1024, 512, 256))
    tn = _pick(N, (512, 256, 128))
    return _mm(name, a, w, contract=NT, grid=(M // tm, N // tn, 1),
               a_spec=pl.BlockSpec((tm, C), lambda i, j, k: (i, 0)),
               b_spec=pl.BlockSpec((tn, C), lambda i, j, k: (j, 0)),
               o_spec=pl.BlockSpec((tm, tn), lambda i, j, k: (i, j)),
               out_shape=jax.ShapeDtypeStruct((M, N), out_dtype))


def mm_tn_cols(name, a, b, comm=None):
    S, K = a.shape
    C = b.shape[1] // 4
    ts = _pick(S, (2048, 1024, 512, 256))
    tko = _pick(K, (1024, 512, 256, 128))
    tn = _pick(C, (1792, 1536, 1024, 512, 256, 128))
    nps = C // tn
    return _mm(name, a, b, contract=TN, grid=(K // tko, 4 * nps, S // ts),
               a_spec=pl.BlockSpec((ts, tko), lambda i, j, k: (k, i)),
               b_spec=pl.BlockSpec((ts, tn), lambda i, j, k: (k, j)),
               o_spec=pl.BlockSpec((None, tko, tn), lambda i, j, k: (j // nps, i, j % nps)),
               out_shape=jax.ShapeDtypeStruct((4, K, C), BF16), comm=comm)


def mm_tn(name, a, b):
    S, K = a.shape
    N = b.shape[1]
    ts = _pick(S, (1024, 512, 256))
    tko = _pick(K, (2048, 1024, 512, 256, 128))
    tn = _pick(N, (1024, 512, 256, 128))
    return _mm(name, a, b, contract=TN, grid=(K // tko, N // tn, S // ts),
               a_spec=pl.BlockSpec((ts, tko), lambda i, j, k: (k, i)),
               b_spec=pl.BlockSpec((ts, tn), lambda i, j, k: (k, j)),
               o_spec=pl.BlockSpec((tko, tn), lambda i, j, k: (i, j)),
               out_shape=jax.ShapeDtypeStruct((K, N), BF16))


def rms_fwd(name, x, g):
    S, D = x.shape
    T = _pick(S, (512, 256))

    def body(x_ref, g_ref, o_ref):
        xf = x_ref[...]
        r = lax.rsqrt(jnp.mean(xf * xf, axis=-1, keepdims=True) + EPS)
        o_ref[...] = (xf * r * g_ref[...]).astype(o_ref.dtype)

    return pl.pallas_call(
        body, grid=(S // T,),
        in_specs=[pl.BlockSpec((T, D), lambda i: (i, 0)), pl.BlockSpec((1, D), lambda i: (0, 0))],
        out_specs=pl.BlockSpec((T, D), lambda i: (i, 0)),
        out_shape=jax.ShapeDtypeStruct((S, D), BF16), name=name,
        compiler_params=_cparams("parallel"))(x, g.reshape(1, D))


def rms_bwd(name, x, g, dys, dres):
    S, D = x.shape
    T = _pick(S, (256,))
    ndy = len(dys)
    has_res = dres is not None

    def body(*refs):
        x_ref, g_ref = refs[0], refs[1]
        dy_refs = refs[2:2 + ndy]
        r_ref = refs[2 + ndy] if has_res else None
        dx_ref, dxb_ref, dg_ref = refs[-3], refs[-2], refs[-1]
        i = pl.program_id(0)
        xf = x_ref[...]
        r = lax.rsqrt(jnp.mean(xf * xf, axis=-1, keepdims=True) + EPS)
        xhat = xf * r
        dy = dy_refs[0][...].astype(F32)
        for d in dy_refs[1:]:
            dy = dy + d[...].astype(F32)
        dxhat = dy * g_ref[...]
        dx = r * (dxhat - xhat * jnp.mean(dxhat * xhat, axis=-1, keepdims=True))
        if has_res:
            dx = dx + r_ref[...]
        dx_ref[...] = dx
        dxb_ref[...] = dx.astype(dxb_ref.dtype)
        dg = jnp.sum(dy * xhat, axis=0, keepdims=True)

        @pl.when(i == 0)
        def _():
            dg_ref[...] = dg

        @pl.when(i > 0)
        def _():
            dg_ref[...] += dg

    row = pl.BlockSpec((T, D), lambda i: (i, 0))
    vec = pl.BlockSpec((1, D), lambda i: (0, 0))
    args = [x, g.reshape(1, D), *dys] + ([dres] if has_res else [])
    return pl.pallas_call(
        body, grid=(S // T,),
        in_specs=[row, vec] + [row] * (ndy + int(has_res)),
        out_specs=[row, row, vec],
        out_shape=[jax.ShapeDtypeStruct((S, D), F32), jax.ShapeDtypeStruct((S, D), BF16),
                   jax.ShapeDtypeStruct((1, D), F32)],
        name=name, compiler_params=_cparams("arbitrary"))(*args)


def loss_head(name, h, g, target):
    S, D = h.shape
    T = _pick(S, (256,))

    def body(h_ref, g_ref, t_ref, loss_ref, dg_ref, dh_ref, dhb_ref):
        i = pl.program_id(0)
        xf = h_ref[...]
        gv = g_ref[...]
        r = lax.rsqrt(jnp.mean(xf * xf, axis=-1, keepdims=True) + EPS)
        xhat = xf * r
        err = xhat * gv - t_ref[...]
        part = 0.5 * jnp.sum(jnp.sum(err * err, axis=-1, keepdims=True), axis=0, keepdims=True) / D
        dout = err / D
        dxhat = dout * gv
        dh = r * (dxhat - xhat * jnp.mean(dxhat * xhat, axis=-1, keepdims=True))
        dh_ref[...] = dh
        dhb_ref[...] = dh.astype(dhb_ref.dtype)
        dg = jnp.sum(dout * xhat, axis=0, keepdims=True)
        lrow = jnp.broadcast_to(part, (1, LANES))

        @pl.when(i == 0)
        def _():
            dg_ref[...] = dg
            loss_ref[...] = lrow

        @pl.when(i > 0)
        def _():
            dg_ref[...] += dg
            loss_ref[...] += lrow

    row = pl.BlockSpec((T, D), lambda i: (i, 0))
    vec = pl.BlockSpec((1, D), lambda i: (0, 0))
    return pl.pallas_call(
        body, grid=(S // T,), in_specs=[row, vec, row],
        out_specs=[pl.BlockSpec((1, LANES), lambda i: (0, 0)), vec, row, row],
        out_shape=[jax.ShapeDtypeStruct((1, LANES), F32), jax.ShapeDtypeStruct((1, D), F32),
                   jax.ShapeDtypeStruct((S, D), F32), jax.ShapeDtypeStruct((S, D), BF16)],
        name=name, compiler_params=_cparams("arbitrary"))(h, g.reshape(1, D), target)


def _attn_tq(S):
    return _pick(S, (512,))


def band_bias_table(rel_bias, tq):
    H = rel_bias.shape[0]
    w = 2 * tq
    nbits = int(np.log2(tq))
    assert (1 << nbits) == tq and (N_PAST_CHUNKS + 2) * CHUNK - 1 <= w
    c = np.arange(w)
    d0 = np.where(c <= tq + CHUNK - 1, tq - c, tq + w - c)
    base = jnp.take(rel_bias.astype(F32), jnp.asarray(np.clip(d0, -MAX_REL, MAX_REL) + MAX_REL), axis=1)

    def body(b_ref, o_ref):
        x = jnp.broadcast_to(b_ref[...], (tq, w))
        row = lax.broadcasted_iota(jnp.int32, (tq, w), 0)
        col = lax.broadcasted_iota(jnp.int32, (tq, w), 1)
        for b in range(nbits):
            x = jnp.where(((row >> b) & 1) == 1, pltpu.roll(x, 1 << b, 1), x)
        qc = row // CHUNK
        kc = col // CHUNK - tq // CHUNK
        o_ref[...] = jnp.where((kc <= qc) & (kc >= qc - N_PAST_CHUNKS), x, NEG)

    return pl.pallas_call(
        body, grid=(H,), in_specs=[pl.BlockSpec((None, 1, w), lambda h: (h, 0, 0))],
        out_specs=pl.BlockSpec((None, tq, w), lambda h: (h, 0, 0)),
        out_shape=jax.ShapeDtypeStruct((H, tq, w), F32), name="band_bias_table",
        compiler_params=_cparams("parallel"))(base.reshape(H, 1, w))


def _attn_subblocks(tq):
    sub = tq // 2
    assert sub % CHUNK == 0 and N_PAST_CHUNKS * CHUNK == tq
    return sub, 3


def attn_fwd(proj, bm, D, comm=None):
    S = proj.shape[0]
    H = D // HEAD_DIM_A
    tq = _attn_tq(S)
    nb = S // tq
    scale = HEAD_DIM_A ** -0.5

    sub, n_sub = _attn_subblocks(tq)

    def body(q_ref, kp_ref, kc_ref, vp_ref, vc_ref, bm_ref, o_ref, lse_ref):
        i = pl.program_id(1)
        for qh in range(tq // sub):
            rows = slice(qh * sub, (qh + 1) * sub)
            q = q_ref[rows, :]
            ss = []
            for kb in range(qh, qh + n_sub):
                k_ref, krows = (kp_ref, kb) if kb < tq // sub else (kc_ref, kb - tq // sub)
                s = _dot(q, k_ref[krows * sub:(krows + 1) * sub, :], NT) * scale + bm_ref[rows, kb * sub:(kb + 1) * sub]
                if kb < tq // sub:
                    s = jnp.where(i == 0, NEG, s)
                ss.append(s)
            m = functools.reduce(jnp.maximum, [jnp.max(s, axis=-1, keepdims=True) for s in ss])
            ps = [jnp.exp(s - m) for s in ss]
            l = functools.reduce(jnp.add, [jnp.sum(p, axis=-1, keepdims=True) for p in ps])
            o = None
            for p, kb in zip(ps, range(qh, qh + n_sub)):
                v_ref, vrows = (vp_ref, kb) if kb < tq // sub else (vc_ref, kb - tq // sub)
                t = _dot(p.astype(BF16), v_ref[vrows * sub:(vrows + 1) * sub, :], NN)
                o = t if o is None else o + t
            o_ref[rows, :] = (o / l).astype(o_ref.dtype)
            lse_ref[rows, :] = m + jnp.log(l)

    def col(base):
        return (pl.BlockSpec((tq, HEAD_DIM_A), lambda h, i: (jnp.maximum(i - 1, 0), base + h)),
                pl.BlockSpec((tq, HEAD_DIM_A), lambda h, i: (i, base + h)))

    kp, kc = col(H)
    vp, vc = col(2 * H)
    return _call(
        body, name="attn_fwd", grid=(H, nb),
        in_specs=[pl.BlockSpec((tq, HEAD_DIM_A), lambda h, i: (i, h)), kp, kc, vp, vc,
                  pl.BlockSpec((None, tq, 2 * tq), lambda h, i: (h, 0, 0))],
        out_specs=[pl.BlockSpec((tq, HEAD_DIM_A), lambda h, i: (i, h)),
                   pl.BlockSpec((None, tq, 1), lambda h, i: (h, i, 0))],
        out_shape=[jax.ShapeDtypeStruct((S, D), BF16), jax.ShapeDtypeStruct((H, S, 1), F32)],
        args=[proj, proj, proj, proj, proj, bm], sem=("parallel", "arbitrary"), comm=comm)


def attn_bwd(proj, ya, dya, lse, bm, D, comm=None):
    S = proj.shape[0]
    H = D // HEAD_DIM_A
    tq = _attn_tq(S)
    nb = S // tq
    scale = HEAD_DIM_A ** -0.5
    sub, n_sub = _attn_subblocks(tq)

    def body(q_ref, kp_ref, kc_ref, vp_ref, vc_ref, o_ref, do_ref, lse_ref, bm_ref,
             dq_ref, dkc_ref, dkp_ref, dvc_ref, dvp_ref, ds_ref):
        i = pl.program_id(1)
        per = tq // sub

        @pl.when(i == 0)
        def _():
            ds_ref[...] = jnp.zeros_like(ds_ref)

        dk_acc = [None] * (2 * per)
        dv_acc = [None] * (2 * per)
        for qh in range(per):
            rows = slice(qh * sub, (qh + 1) * sub)
            q = q_ref[rows, :]
            do = do_ref[rows, :]
            delta = jnp.sum(do.astype(F32) * o_ref[rows, :].astype(F32), axis=-1, keepdims=True)
            lse_v = lse_ref[rows, :]
            dq = None
            for kb in range(qh, qh + n_sub):
                k_ref, v_ref, kr = (kp_ref, vp_ref, kb) if kb < per else (kc_ref, vc_ref, kb - per)
                k = k_ref[kr * sub:(kr + 1) * sub, :]
                cols = slice(kb * sub, (kb + 1) * sub)
                s = _dot(q, k, NT) * scale + bm_ref[rows, cols]
                if kb < per:
                    s = jnp.where(i == 0, NEG, s)
                p = jnp.exp(s - lse_v)
                dv = _dot(p.astype(BF16), do, TN)
                dp = _dot(do, v_ref[kr * sub:(kr + 1) * sub, :], NT)
                ds = p * (dp - delta)
                dsb = ds.astype(BF16)
                t = _dot(dsb, k, NN)
                dq = t if dq is None else dq + t
                dk = _dot(dsb, q, TN)
                dk_acc[kb] = dk if dk_acc[kb] is None else dk_acc[kb] + dk
                dv_acc[kb] = dv if dv_acc[kb] is None else dv_acc[kb] + dv
                ds_ref[rows, cols] += ds
            dq_ref[rows, :] = (dq * scale).astype(dq_ref.dtype)
        for kb in range(2 * per):
            dk_ref, dv_ref, kr = (dkp_ref, dvp_ref, kb) if kb < per else (dkc_ref, dvc_ref, kb - per)
            dk_ref[kr * sub:(kr + 1) * sub, :] = (dk_acc[kb] * scale).astype(dk_ref.dtype)
            dv_ref[kr * sub:(kr + 1) * sub, :] = dv_acc[kb].astype(dv_ref.dtype)

    def col(base):
        return (pl.BlockSpec((tq, HEAD_DIM_A), lambda h, i: (jnp.maximum(i - 1, 0), base + h)),
                pl.BlockSpec((tq, HEAD_DIM_A), lambda h, i: (i, base + h)))

    kp, kc = col(H)
    vp, vc = col(2 * H)
    blk = pl.BlockSpec((tq, HEAD_DIM_A), lambda h, i: (i, h))
    sd = jax.ShapeDtypeStruct((S, D), BF16)
    return _call(
        body, name="attn_bwd", grid=(H, nb),
        in_specs=[blk, kp, kc, vp, vc, blk, blk,
                  pl.BlockSpec((None, tq, 1), lambda h, i: (h, i, 0)),
                  pl.BlockSpec((None, tq, 2 * tq), lambda h, i: (h, 0, 0))],
        out_specs=[blk, blk, blk, blk, blk, pl.BlockSpec((None, tq, 2 * tq), lambda h, i: (h, 0, 0))],
        out_shape=[sd, sd, sd, sd, sd, jax.ShapeDtypeStruct((H, tq, 2 * tq), F32)],
        args=[proj, proj, proj, proj, proj, ya, dya, lse, bm], sem=("parallel", "arbitrary"), comm=comm)


def rel_bias_grad(ds_sum):
    H, tq, w = ds_sum.shape
    nbin = 2 * MAX_REL + 1
    nbin_pad = 3 * LANES
    d_lo, d_hi = -(CHUNK - 1), (N_PAST_CHUNKS + 1) * CHUNK - 1
    assert d_hi - d_lo + 1 <= w
    onehot = np.zeros((w, nbin_pad), np.float32)
    for d in range(d_lo, d_hi + 1):
        onehot[(tq - d) % w, int(np.clip(d, -MAX_REL, MAX_REL)) + MAX_REL] = 1.0
    nbits = int(np.log2(tq))
    assert (1 << nbits) == tq

    def body(ds_ref, m_ref, o_ref):
        x = ds_ref[...]
        row = lax.broadcasted_iota(jnp.int32, x.shape, 0)
        for b in range(nbits):
            rolled = pltpu.roll(x, w - (1 << b), 1)
            x = jnp.where(((row >> b) & 1) == 1, rolled, x)
        t = jnp.sum(x, axis=0, keepdims=True)
        o_ref[...] = lax.dot_general(t, m_ref[...], (NN, ((), ())), precision=lax.Precision.HIGHEST,
                                     preferred_element_type=F32)

    out = pl.pallas_call(
        body, grid=(H,),
        in_specs=[pl.BlockSpec((None, tq, w), lambda h: (h, 0, 0)),
                  pl.BlockSpec((w, nbin_pad), lambda h: (0, 0))],
        out_specs=pl.BlockSpec((None, 1, nbin_pad), lambda h: (h, 0, 0)),
        out_shape=jax.ShapeDtypeStruct((H, 1, nbin_pad), F32),
        name="rel_bias_grad", compiler_params=_cparams("parallel"))(ds_sum, jnp.asarray(onehot))
    return out[:, 0, :nbin]


def _conv_t(S):
    return _pick(S, (256,))


ROW_CHUNK = 16


def _row_loop(n_rows, step):
    def one(r, carry):
        step(pl.ds(pl.multiple_of(r * ROW_CHUNK, ROW_CHUNK), ROW_CHUNK))
        return carry

    lax.fori_loop(0, n_rows // ROW_CHUNK, one, 0)


def _fill_zbuf(zbuf, ap_ref, bp_ref, a_ref, b_ref, i):
    zp = ap_ref[...].astype(F32) * _sigmoid(bp_ref[...].astype(F32))
    zbuf[0:CONV_HALO, :] = jnp.where(i == 0, 0.0, zp)

    def step(rows):
        below = pl.ds(pl.multiple_of(rows.start + CONV_HALO, ROW_CHUNK), ROW_CHUNK)
        zbuf[below, :] = a_ref[rows, :].astype(F32) * _sigmoid(b_ref[rows, :].astype(F32))

    _row_loop(a_ref.shape[0], step)


def _shifted_windows(buf, shifted, lanes, T):
    rows = T + CONV_HALO - SUBLANES
    for b in range(1, SUBLANES):
        shifted[b - 1] = buf[pl.ds(b, rows), lanes]

    def window(off, r0=0, n=T):
        a, b = divmod(off, SUBLANES)
        if b == 0:
            return buf[pl.ds(r0 + off, n), lanes]
        return shifted[b - 1, pl.ds(r0 + a * SUBLANES, n), :]

    return window


def _shifted_scratch(T):
    return pltpu.VMEM((SUBLANES - 1, T + CONV_HALO - SUBLANES, LANES), F32)


def conv_gate_fwd(proj, ya, cw, cb, lng, lnb, D, comm=None):
    S = proj.shape[0]
    T = _conv_t(S)
    hb = T // CONV_HALO
    nlb = D // LANES

    def body(ap_ref, bp_ref, a_ref, b_ref, ga_ref, gb_ref, ya_ref, cw_ref, cb_ref, lng_ref, lnb_ref,
             y_ref, c_ref, zbuf, zsh):
        i = pl.program_id(0)
        _fill_zbuf(zbuf, ap_ref, bp_ref, a_ref, b_ref, i)

        def lane_block(lb, carry):
            lanes = pl.ds(pl.multiple_of(lb * LANES, LANES), LANES)
            z_at = _shifted_windows(zbuf, zsh, lanes, T)
            acc = jnp.zeros((T, LANES), F32)
            for k in range(CONV_WIDTH):
                acc = acc + cw_ref[k:k + 1, lanes] * z_at(CONV_HALO - CONV_WIDTH + 1 + k)
            c_ref[:, lanes] = acc + cb_ref[:, lanes]
            return carry

        lax.fori_loop(0, nlb, lane_block, 0)

        def norm_and_gate(rows):
            c = c_ref[rows, :]
            mu = jnp.mean(c, axis=-1, keepdims=True)
            xc = c - mu
            rstd = lax.rsqrt(jnp.mean(xc * xc, axis=-1, keepdims=True) + EPS)
            ln = xc * rstd * lng_ref[...] + lnb_ref[...]
            yb = ln * _sigmoid(ln)
            ga = ga_ref[rows, :].astype(F32)
            gb = gb_ref[rows, :].astype(F32)
            y_ref[rows, :D] = (ya_ref[rows, :].astype(F32) * (ga * _sigmoid(ga))).astype(y_ref.dtype)
            y_ref[rows, D:] = (yb * (gb * _sigmoid(gb))).astype(y_ref.dtype)

        _row_loop(T, norm_and_gate)

    def cur(cidx):
        return pl.BlockSpec((T, D), lambda i: (i, cidx))

    def prev(cidx):
        return pl.BlockSpec((CONV_HALO, D), lambda i: (jnp.maximum(i * hb - 1, 0), cidx))

    vec = pl.BlockSpec((1, D), lambda i: (0, 0))
    return _call(
        body, name="conv_gate_fwd", grid=(S // T,),
        in_specs=[prev(3), prev(4), cur(3), cur(4), cur(5), cur(6), pl.BlockSpec((T, D), lambda i: (i, 0)),
                  pl.BlockSpec((CONV_HALO, D), lambda i: (0, 0)), vec, vec, vec],
        out_specs=[pl.BlockSpec((T, 2 * D), lambda i: (i, 0)), pl.BlockSpec((T, D), lambda i: (i, 0))],
        out_shape=[jax.ShapeDtypeStruct((S, 2 * D), BF16), jax.ShapeDtypeStruct((S, D), F32)],
        scratch_shapes=[pltpu.VMEM((T + CONV_HALO, D), F32), _shifted_scratch(T)],
        args=[proj, proj, proj, proj, proj, proj, ya, cw, cb, lng, lnb], sem=("parallel",), comm=comm)


def conv_gate_bwd_a(dy0, proj, ya, cpre, lng, lnb, D):
    S = proj.shape[0]
    T = _conv_t(S)

    def body(dy_ref, ga_ref, gb_ref, ya_ref, c_ref, lng_ref, lnb_ref,
             dya_ref, dg_ref, dc_ref, dlng_ref, dlnb_ref):
        i = pl.program_id(0)

        c = c_ref[...]
        gv = lng_ref[...]
        mu = jnp.mean(c, axis=-1, keepdims=True)
        xc = c - mu
        rstd = lax.rsqrt(jnp.mean(xc * xc, axis=-1, keepdims=True) + EPS)
        xhat = xc * rstd
        ln = xhat * gv + lnb_ref[...]
        sl = _sigmoid(ln)
        yb = ln * sl
        ga = ga_ref[...].astype(F32)
        gb = gb_ref[...].astype(F32)
        sa = _sigmoid(ga)
        sb = _sigmoid(gb)
        dy_a = dy_ref[:, :D].astype(F32)
        dy_b = dy_ref[:, D:].astype(F32)
        dya_ref[...] = (dy_a * (ga * sa)).astype(dya_ref.dtype)
        dg_ref[:, :D] = (dy_a * ya_ref[...].astype(F32) * (sa * (1.0 + ga * (1.0 - sa)))).astype(dg_ref.dtype)
        dg_ref[:, D:] = (dy_b * yb * (sb * (1.0 + gb * (1.0 - sb)))).astype(dg_ref.dtype)
        dln = dy_b * (gb * sb) * (sl * (1.0 + ln * (1.0 - sl)))
        dxhat = dln * gv
        dc_ref[...] = rstd * (dxhat - jnp.mean(dxhat, axis=-1, keepdims=True)
                              - xhat * jnp.mean(dxhat * xhat, axis=-1, keepdims=True))
        dlng = jnp.sum(dln * xhat, axis=0, keepdims=True)
        dlnb = jnp.sum(dln, axis=0, keepdims=True)

        @pl.when(i == 0)
        def _():
            dlng_ref[...] = dlng
            dlnb_ref[...] = dlnb

        @pl.when(i > 0)
        def _():
            dlng_ref[...] += dlng
            dlnb_ref[...] += dlnb

    row = pl.BlockSpec((T, D), lambda i: (i, 0))
    vec = pl.BlockSpec((1, D), lambda i: (0, 0))
    return pl.pallas_call(
        body, grid=(S // T,),
        in_specs=[pl.BlockSpec((T, 2 * D), lambda i: (i, 0)),
                  pl.BlockSpec((T, D), lambda i: (i, 5)), pl.BlockSpec((T, D), lambda i: (i, 6)),
                  row, row, vec, vec],
        out_specs=[row, pl.BlockSpec((T, 2 * D), lambda i: (i, 0)), row, vec, vec],
        out_shape=[jax.ShapeDtypeStruct((S, D), BF16), jax.ShapeDtypeStruct((S, 2 * D), BF16),
                   jax.ShapeDtypeStruct((S, D), F32), jax.ShapeDtypeStruct((1, D), F32),
                   jax.ShapeDtypeStruct((1, D), F32)],
        name="conv_gate_bwd_a", compiler_params=_cparams("arbitrary"))(
            dy0, proj, proj, ya, cpre, lng, lnb)


def conv_gate_bwd_b(dc, proj, cw, D, comm=None):
    S = proj.shape[0]
    T = _conv_t(S)
    hb = T // CONV_HALO
    nt = S // T
    nlb = D // LANES
    half = T // 2

    def body(dc_ref, dn_ref, ap_ref, bp_ref, a_ref, b_ref, cw_ref, da_ref, db_ref, dcw_ref, dcb_ref,
             zbuf, dcbuf, zsh, dcsh, dcw8):
        i = pl.program_id(0)
        _fill_zbuf(zbuf, ap_ref, bp_ref, a_ref, b_ref, i)
        dcv = dc_ref[...]
        dcbuf[0:T, :] = dcv
        dcbuf[T:, :] = jnp.where(i == nt - 1, 0.0, dn_ref[...])

        @pl.when(i == 0)
        def _():
            dcw8[...] = jnp.zeros_like(dcw8)
            dcb_ref[...] = jnp.zeros_like(dcb_ref)

        dcb_ref[...] += jnp.sum(dcv, axis=0, keepdims=True)

        def lane_block(lb, carry):
            lanes = pl.ds(pl.multiple_of(lb * LANES, LANES), LANES)
            z_at = _shifted_windows(zbuf, zsh, lanes, T)
            dc_at = _shifted_windows(dcbuf, dcsh, lanes, T)
            for r0 in range(0, T, half):
                d0 = dcbuf[r0:r0 + half, lanes]
                dz = jnp.zeros((half, LANES), F32)
                for k in range(CONV_WIDTH):
                    dz = dz + cw_ref[k:k + 1, lanes] * dc_at(CONV_WIDTH - 1 - k, r0, half)
                    prod = d0 * z_at(CONV_HALO - CONV_WIDTH + 1 + k, r0, half)
                    dcw8[pl.ds(k * SUBLANES, SUBLANES), lanes] += jnp.sum(
                        prod.reshape(half // SUBLANES, SUBLANES, LANES), axis=0)
                av = a_ref[r0:r0 + half, lanes].astype(F32)
                sg = _sigmoid(b_ref[r0:r0 + half, lanes].astype(F32))
                da_ref[r0:r0 + half, lanes] = (dz * sg).astype(da_ref.dtype)
                db_ref[r0:r0 + half, lanes] = (dz * av * sg * (1.0 - sg)).astype(db_ref.dtype)
            return carry

        lax.fori_loop(0, nlb, lane_block, 0)

        @pl.when(i == nt - 1)
        def _():
            dcw_ref[...] = jnp.sum(dcw8[...].reshape(CONV_HALO, SUBLANES, D), axis=1)

    def cur(cidx):
        return pl.BlockSpec((T, D), lambda i: (i, cidx))

    def prev(cidx):
        return pl.BlockSpec((CONV_HALO, D), lambda i: (jnp.maximum(i * hb - 1, 0), cidx))

    row = pl.BlockSpec((T, D), lambda i: (i, 0))
    nxt = pl.BlockSpec((CONV_HALO, D), lambda i: (jnp.minimum((i + 1) * hb, nt * hb - 1), 0))
    return _call(
        body, name="conv_gate_bwd_b", grid=(nt,),
        in_specs=[row, nxt, prev(3), prev(4), cur(3), cur(4), pl.BlockSpec((CONV_HALO, D), lambda i: (0, 0))],
        out_specs=[row, row, pl.BlockSpec((CONV_HALO, D), lambda i: (0, 0)),
                   pl.BlockSpec((1, D), lambda i: (0, 0))],
        out_shape=[jax.ShapeDtypeStruct((S, D), BF16), jax.ShapeDtypeStruct((S, D), BF16),
                   jax.ShapeDtypeStruct((CONV_HALO, D), F32), jax.ShapeDtypeStruct((1, D), F32)],
        scratch_shapes=[pltpu.VMEM((T + CONV_HALO, D), F32), pltpu.VMEM((T + CONV_HALO, D), F32),
                        _shifted_scratch(T), _shifted_scratch(T), pltpu.VMEM((CONV_HALO * SUBLANES, D), F32)],
        args=[dc, dc, proj, proj, proj, proj, cw], sem=("arbitrary",), comm=comm)


def assemble_dproj0(dq, dkc, dkp, dvc, dvp, da, db, dgate, D):
    S = dq.shape[0]
    tq = _attn_tq(S)
    T = _pick(S, (256,))
    shift = tq // T
    nt = S // T

    def body(dq_ref, dkc_ref, dkp_ref, dvc_ref, dvp_ref, da_ref, db_ref, dg_ref, o_ref):
        i = pl.program_id(0)
        last = i + shift >= nt
        o_ref[:, 0:D] = dq_ref[...]
        dk = dkc_ref[...].astype(F32) + jnp.where(last, 0.0, dkp_ref[...].astype(F32))
        dv = dvc_ref[...].astype(F32) + jnp.where(last, 0.0, dvp_ref[...].astype(F32))
        o_ref[:, D:2 * D] = dk.astype(o_ref.dtype)
        o_ref[:, 2 * D:3 * D] = dv.astype(o_ref.dtype)
        o_ref[:, 3 * D:4 * D] = da_ref[...]
        o_ref[:, 4 * D:5 * D] = db_ref[...]
        o_ref[:, 5 * D:] = dg_ref[...]

    row = pl.BlockSpec((T, D), lambda i: (i, 0))
    nxt = pl.BlockSpec((T, D), lambda i: (jnp.minimum(i + shift, nt - 1), 0))
    return pl.pallas_call(
        body, grid=(nt,),
        in_specs=[row, row, nxt, row, nxt, row, row, pl.BlockSpec((T, 2 * D), lambda i: (i, 0))],
        out_specs=pl.BlockSpec((T, 7 * D), lambda i: (i, 0)),
        out_shape=jax.ShapeDtypeStruct((S, 7 * D), BF16),
        name="assemble_dproj0", compiler_params=_cparams("parallel"))(dq, dkc, dkp, dvc, dvp, da, db, dgate)


def _sgu_t(S):
    return _pick(S, (256, 128))


def _ws_masked(ws_ref, g):
    row = lax.broadcasted_iota(jnp.int32, (GMLP_CHUNK, GMLP_CHUNK), 0) // CHUNK
    col = lax.broadcasted_iota(jnp.int32, (GMLP_CHUNK, GMLP_CHUNK), 1) // CHUNK
    return jnp.where(row >= col, ws_ref[g], 0.0), row >= col


def sgu_fwd(proj, lng, lnb, ws, bst, MIX):
    S = proj.shape[0]
    T = _sgu_t(S)
    gw = MIX // N_GROUPS_C

    def body(u_ref, v_ref, g_ref, lng_ref, lnb_ref, ws_ref, bst_ref, y_ref):
        v = v_ref[...].astype(F32)
        mu = jnp.mean(v, axis=-1, keepdims=True)
        xc = v - mu
        rstd = lax.rsqrt(jnp.mean(xc * xc, axis=-1, keepdims=True) + EPS)
        for g in range(N_GROUPS_C):
            cols = slice(g * gw, (g + 1) * gw)
            wsm = _ws_masked(ws_ref, g)[0].astype(BF16)
            vn = (xc[:, cols] * rstd * lng_ref[:, cols] + lnb_ref[:, cols]).astype(BF16)
            for blk in range(T // GMLP_CHUNK):
                rows = slice(blk * GMLP_CHUNK, (blk + 1) * GMLP_CHUNK)
                sg = _dot(wsm, vn[rows], NN) + bst_ref[:, g:g + 1]
                gate = g_ref[rows, cols].astype(F32)
                y = u_ref[rows, cols].astype(F32) * sg * (gate * _sigmoid(gate))
                y_ref[rows, cols] = y.astype(y_ref.dtype)

    def part(cidx):
        return pl.BlockSpec((T, MIX), lambda i: (i, cidx))

    vec = pl.BlockSpec((1, MIX), lambda i: (0, 0))
    return pl.pallas_call(
        body, grid=(S // T,),
        in_specs=[part(0), part(1), part(2), vec, vec,
                  pl.BlockSpec((N_GROUPS_C, GMLP_CHUNK, GMLP_CHUNK), lambda i: (0, 0, 0)),
                  pl.BlockSpec((GMLP_CHUNK, N_GROUPS_C), lambda i: (0, 0))],
        out_specs=pl.BlockSpec((T, MIX), lambda i: (i, 0)),
        out_shape=jax.ShapeDtypeStruct((S, MIX), BF16),
        name="sgu_fwd", compiler_params=_cparams("parallel"))(proj, proj, proj, lng, lnb, ws, bst)


def sgu_bwd(dy1, proj, lng, lnb, ws, bst, MIX):
    S = proj.shape[0]
    T = _sgu_t(S)
    gw = MIX // N_GROUPS_C

    def body(dy_ref, u_ref, v_ref, g_ref, lng_ref, lnb_ref, ws_ref, bst_ref,
             dp_ref, dws_ref, dbst_ref, dlng_ref, dlnb_ref, dvn_buf):
        i = pl.program_id(0)

        @pl.when(i == 0)
        def _():
            dws_ref[...] = jnp.zeros_like(dws_ref)
            dbst_ref[...] = jnp.zeros_like(dbst_ref)
            dlng_ref[...] = jnp.zeros_like(dlng_ref)
            dlnb_ref[...] = jnp.zeros_like(dlnb_ref)

        v = v_ref[...].astype(F32)
        mu = jnp.mean(v, axis=-1, keepdims=True)
        xc = v - mu
        rstd = lax.rsqrt(jnp.mean(xc * xc, axis=-1, keepdims=True) + EPS)
        for g in range(N_GROUPS_C):
            cols = slice(g * gw, (g + 1) * gw)
            wsf, keep = _ws_masked(ws_ref, g)
            wsm = wsf.astype(BF16)
            vn = (xc[:, cols] * rstd * lng_ref[:, cols] + lnb_ref[:, cols]).astype(BF16)
            for blk in range(T // GMLP_CHUNK):
                rows = slice(blk * GMLP_CHUNK, (blk + 1) * GMLP_CHUNK)
                vnb = vn[rows]
                sg = _dot(wsm, vnb, NN) + bst_ref[:, g:g + 1]
                gate = g_ref[rows, cols].astype(F32)
                sig = _sigmoid(gate)
                sil = gate * sig
                u = u_ref[rows, cols].astype(F32)
                dy = dy_ref[rows, cols].astype(F32)
                dp_ref[rows, g * gw:(g + 1) * gw] = (dy * sg * sil).astype(dp_ref.dtype)
                dp_ref[rows, 2 * MIX + g * gw:2 * MIX + (g + 1) * gw] = (
                    dy * u * sg * (sig * (1.0 + gate * (1.0 - sig)))).astype(dp_ref.dtype)
                dsg = dy * u * sil
                dsgb = dsg.astype(BF16)
                dvn_buf[rows, cols] = _dot(wsm, dsgb, TN)
                dws_ref[g] += jnp.where(keep, _dot(dsgb, vnb, NT), 0.0)
                dbst_ref[:, g:g + 1] += jnp.sum(dsg, axis=-1, keepdims=True)
        dvn = dvn_buf[...]
        xhat = xc * rstd
        dxhat = dvn * lng_ref[...]
        dv = rstd * (dxhat - jnp.mean(dxhat, axis=-1, keepdims=True)
                     - xhat * jnp.mean(dxhat * xhat, axis=-1, keepdims=True))
        dp_ref[:, MIX:2 * MIX] = dv.astype(dp_ref.dtype)
        dlng_ref[...] += jnp.sum(dvn * xhat, axis=0, keepdims=True)
        dlnb_ref[...] += jnp.sum(dvn, axis=0, keepdims=True)

    def part(cidx):
        return pl.BlockSpec((T, MIX), lambda i: (i, cidx))

    vec = pl.BlockSpec((1, MIX), lambda i: (0, 0))
    wspec = pl.BlockSpec((N_GROUPS_C, GMLP_CHUNK, GMLP_CHUNK), lambda i: (0, 0, 0))
    bspec = pl.BlockSpec((GMLP_CHUNK, N_GROUPS_C), lambda i: (0, 0))
    return pl.pallas_call(
        body, grid=(S // T,),
        in_specs=[pl.BlockSpec((T, MIX), lambda i: (i, 0)), part(0), part(1), part(2), vec, vec, wspec, bspec],
        out_specs=[pl.BlockSpec((T, 3 * MIX), lambda i: (i, 0)), wspec, bspec, vec, vec],
        out_shape=[jax.ShapeDtypeStruct((S, 3 * MIX), BF16),
                   jax.ShapeDtypeStruct((N_GROUPS_C, GMLP_CHUNK, GMLP_CHUNK), F32),
                   jax.ShapeDtypeStruct((GMLP_CHUNK, N_GROUPS_C), F32),
                   jax.ShapeDtypeStruct((1, MIX), F32), jax.ShapeDtypeStruct((1, MIX), F32)],
        scratch_shapes=[pltpu.VMEM((T, MIX), F32)],
        name="sgu_bwd", compiler_params=_cparams("arbitrary"))(dy1, proj, proj, proj, lng, lnb, ws, bst)


def xattn_fwd(name, q, k, v):
    S, D = q.shape
    nm = k.shape[0]
    dh = D // N_HEADS_X
    tq = _pick(S, (512, 256))
    scale = dh ** -0.5

    def body(q_ref, k_ref, v_ref, o_ref, lse_ref):
        s = _dot(q_ref[...], k_ref[...], NT) * scale
        m = jnp.max(s, axis=-1, keepdims=True)
        p = jnp.exp(s - m)
        l = jnp.sum(p, axis=-1, keepdims=True)
        o_ref[...] = (_dot(p.astype(BF16), v_ref[...], NN) / l).astype(o_ref.dtype)
        lse_ref[...] = m + jnp.log(l)

    return pl.pallas_call(
        body, grid=(N_HEADS_X, S // tq),
        in_specs=[pl.BlockSpec((tq, dh), lambda h, i: (i, h)),
                  pl.BlockSpec((nm, dh), lambda h, i: (0, h)), pl.BlockSpec((nm, dh), lambda h, i: (0, h))],
        out_specs=[pl.BlockSpec((tq, dh), lambda h, i: (i, h)),
                   pl.BlockSpec((None, tq, 1), lambda h, i: (h, i, 0))],
        out_shape=[jax.ShapeDtypeStruct((S, D), BF16), jax.ShapeDtypeStruct((N_HEADS_X, S, 1), F32)],
        name=name, compiler_params=_cparams("parallel", "parallel"))(q, k, v)


def xattn_bwd(name, q, k, v, o, do, lse):
    S, D = q.shape
    nm = k.shape[0]
    dh = D // N_HEADS_X
    tq = _pick(S, (512, 256))
    scale = dh ** -0.5

    def body(q_ref, k_ref, v_ref, o_ref, do_ref, lse_ref, dq_ref, dk_ref, dv_ref):
        i = pl.program_id(1)
        q_v = q_ref[...]
        k_v = k_ref[...]
        do_v = do_ref[...]
        p = jnp.exp(_dot(q_v, k_v, NT) * scale - lse_ref[...])
        delta = jnp.sum(do_v.astype(F32) * o_ref[...].astype(F32), axis=-1, keepdims=True)
        dv = _dot(p.astype(BF16), do_v, TN)
        ds = (p * (_dot(do_v, v_ref[...], NT) - delta)).astype(BF16)
        dq_ref[...] = (_dot(ds, k_v, NN) * scale).astype(dq_ref.dtype)
        dk = _dot(ds, q_v, TN) * scale

        @pl.when(i == 0)
        def _():
            dk_ref[...] = dk
            dv_ref[...] = dv

        @pl.when(i > 0)
        def _():
            dk_ref[...] += dk
            dv_ref[...] += dv

    qs = pl.BlockSpec((tq, dh), lambda h, i: (i, h))
    ks = pl.BlockSpec((nm, dh), lambda h, i: (0, h))
    return pl.pallas_call(
        body, grid=(N_HEADS_X, S // tq),
        in_specs=[qs, ks, ks, qs, qs, pl.BlockSpec((None, tq, 1), lambda h, i: (h, i, 0))],
        out_specs=[qs, ks, ks],
        out_shape=[jax.ShapeDtypeStruct((S, D), BF16), jax.ShapeDtypeStruct((nm, D), F32),
                   jax.ShapeDtypeStruct((nm, D), F32)],
        name=name, compiler_params=_cparams("parallel", "arbitrary"))(q, k, v, o, do, lse)


def adamw(name, w, g, m, v):
    R, C = w.shape
    tr = _pick(R, tuple(t for t in (512, 256, 128, 64, 32, 16, 8) if t * C * 4 <= (1 << 20)) or (8,))
    c1 = 1.0 - ADAM_B1 ** ADAM_STEP
    c2 = 1.0 - ADAM_B2 ** ADAM_STEP

    def body(w_ref, g_ref, m_ref, v_ref, d_ref, nm_ref, nv_ref):
        gv = g_ref[...]
        nm = ADAM_B1 * m_ref[...] + (1.0 - ADAM_B1) * gv
        nv = ADAM_B2 * v_ref[...] + (1.0 - ADAM_B2) * (gv * gv)
        d_ref[...] = -ADAM_LR * ((nm / c1) / (jnp.sqrt(nv / c2) + ADAM_EPS) + ADAM_WD * w_ref[...])
        nm_ref[...] = nm
        nv_ref[...] = nv

    blk = pl.BlockSpec((tr, C), lambda i: (i, 0))
    sd = jax.ShapeDtypeStruct((R, C), F32)
    return pl.pallas_call(body, grid=(R // tr,), in_specs=[blk] * 4, out_specs=[blk] * 3,
                          out_shape=[sd, sd, sd], name=name, compiler_params=_cparams("parallel"))(w, g, m, v)


def add_halves(name, g4, recv, cidx):
    _, R, C = g4.shape
    rh = R // 2
    tr = _pick(rh, (256, 128, 64, 32, 16))
    nrb = rh // tr

    def body(c_ref, a_ref, b_ref, o_ref):
        o_ref[...] = (a_ref[...].astype(F32) + b_ref[...].astype(F32)).astype(o_ref.dtype)

    grid_spec = pltpu.PrefetchScalarGridSpec(
        num_scalar_prefetch=1, grid=(4, nrb),
        in_specs=[pl.BlockSpec((None, tr, C), lambda j, r, c_ref: (j, c_ref[0] * nrb + r, 0)),
                  pl.BlockSpec((None, tr, C), lambda j, r, c_ref: (j, r, 0))],
        out_specs=pl.BlockSpec((None, tr, C), lambda j, r, c_ref: (j, r, 0)))
    return pl.pallas_call(body, grid_spec=grid_spec, out_shape=jax.ShapeDtypeStruct((4, rh, C), BF16),
                          name=name, compiler_params=_cparams("parallel", "parallel"))(cidx, g4, recv)


def sum_chips(name, own, recv, place):
    _, rh, C = own.shape
    tr = _pick(rh, (256, 128, 64, 32, 16))
    nrb = rh // tr

    def body(s_ref, own_ref, recv_ref, o_ref):
        acc = own_ref[...].astype(F32)
        for k in range(N_CHIPS - 1):
            acc = acc + recv_ref[k].astype(F32)
        o_ref[...] = acc

    grid_spec = pltpu.PrefetchScalarGridSpec(
        num_scalar_prefetch=1, grid=(nrb,),
        in_specs=[pl.BlockSpec((None, tr, C), lambda r, s: (s[0], r, 0)),
                  pl.BlockSpec((N_CHIPS - 1, tr, C), lambda r, s: (0, r, 0))],
        out_specs=pl.BlockSpec((tr, C), lambda r, s: (s[1] * nrb + r, 0)))
    return pl.pallas_call(body, grid_spec=grid_spec, out_shape=jax.ShapeDtypeStruct((2 * rh, C), F32),
                          name=name, compiler_params=_cparams("parallel"))(place, own, recv)


def cast_into_slot(name, w, place):
    R, C = w.shape
    tr = _pick(R, (256, 128, 64, 32, 16))

    def body(s_ref, w_ref, o_ref):
        o_ref[...] = w_ref[...].astype(o_ref.dtype)

    grid_spec = pltpu.PrefetchScalarGridSpec(
        num_scalar_prefetch=1, grid=(R // tr,),
        in_specs=[pl.BlockSpec((tr, C), lambda r, s: (r, 0))],
        out_specs=pl.BlockSpec((None, tr, C), lambda r, s: (s[0], r, 0)))
    return pl.pallas_call(body, grid_spec=grid_spec, out_shape=jax.ShapeDtypeStruct((N_CHIPS, R, C), BF16),
                          name=name, compiler_params=_cparams("parallel"))(place, w)


def _place():
    return lax.axis_index("x"), lax.axis_index("y"), lax.axis_index("c")


_CHIP_FLIPS = ((1, 0), (0, 1), (1, 1))


def _flip(v, bit):
    return 1 - v if bit else v


HBM_SPEC = pl.BlockSpec(memory_space=pl.ANY)


def exchange_small(name, buf, reduce):
    R = buf.shape[0]

    def body(x_ref, *refs):
        if reduce:
            sum_ref, all_ref, send_sems, recv_sems, local_sem = refs
        else:
            all_ref, send_sems, recv_sems, local_sem = refs
        x, y, c = _place()
        me = 4 * x + 2 * y + c
        mine = pltpu.make_async_copy(x_ref, all_ref.at[me], local_sem)
        mine.start()
        sends = []
        for k in range(1, N_DEV):
            peer = (_flip(x, k & 4), _flip(y, k & 2), _flip(c, k & 1))
            cp = pltpu.make_async_remote_copy(src_ref=x_ref, dst_ref=all_ref.at[me], send_sem=send_sems.at[k - 1],
                                              recv_sem=recv_sems.at[k - 1], device_id=peer, device_id_type=MESH)
            cp.start()
            sends.append(cp)
        for k in range(1, N_DEV):
            peer = (_flip(x, k & 4), _flip(y, k & 2), _flip(c, k & 1))
            src = 4 * peer[0] + 2 * peer[1] + peer[2]
            pltpu.make_async_remote_copy(src_ref=x_ref, dst_ref=all_ref.at[src], send_sem=send_sems.at[k - 1],
                                         recv_sem=recv_sems.at[k - 1], device_id=peer,
                                         device_id_type=MESH).wait_recv()
        for cp in sends:
            cp.wait_send()
        mine.wait()
        if reduce:
            acc = all_ref[0]
            for d in range(1, N_DEV):
                acc = acc + all_ref[d]
            sum_ref[...] = acc

    vm = pl.BlockSpec(memory_space=pltpu.VMEM)
    sems = [pltpu.SemaphoreType.DMA((N_DEV - 1,)), pltpu.SemaphoreType.DMA((N_DEV - 1,)), pltpu.SemaphoreType.DMA]
    if reduce:
        return pl.pallas_call(
            body, in_specs=[vm], out_specs=vm, out_shape=jax.ShapeDtypeStruct((R, LANES), F32),
            scratch_shapes=[pltpu.VMEM((N_DEV, R, LANES), F32)] + sems, name=name,
            compiler_params=pltpu.CompilerParams(vmem_limit_bytes=V7X_VMEM_LIMIT))(buf)
    return pl.pallas_call(
        body, in_specs=[vm], out_specs=vm, out_shape=jax.ShapeDtypeStruct((N_DEV, R, LANES), F32),
        scratch_shapes=sems, name=name,
        compiler_params=pltpu.CompilerParams(vmem_limit_bytes=V7X_VMEM_LIMIT))(buf)


def exchange_job(buf):
    R = buf.shape[0]

    def copies(x_ref, all_ref, send_sems, recv_sems):
        x, y, c = _place()
        me = 4 * x + 2 * y + c
        sends, arrivals = [], []
        for k in range(1, N_DEV):
            peer = (_flip(x, k & 4), _flip(y, k & 2), _flip(c, k & 1))
            src = 4 * peer[0] + 2 * peer[1] + peer[2]
            sends.append(pltpu.make_async_remote_copy(
                src_ref=x_ref, dst_ref=all_ref.at[me], send_sem=send_sems.at[k - 1], recv_sem=recv_sems.at[k - 1],
                device_id=peer, device_id_type=MESH))
            arrivals.append(pltpu.make_async_remote_copy(
                src_ref=x_ref, dst_ref=all_ref.at[src], send_sem=send_sems.at[k - 1], recv_sem=recv_sems.at[k - 1],
                device_id=peer, device_id_type=MESH))
        return sends, arrivals

    def start(ins, outs, sems):
        for cp in copies(ins[0], outs[0], *sems)[0]:
            cp.start()

    def finish(ins, outs, sems):
        sends, arrivals = copies(ins[0], outs[0], *sems)
        for cp in arrivals:
            cp.wait_recv()
        for cp in sends:
            cp.wait_send()

    return _Comm([buf], [jax.ShapeDtypeStruct((N_DEV, R, LANES), F32)], {},
                 [pltpu.SemaphoreType.DMA((N_DEV - 1,)), pltpu.SemaphoreType.DMA((N_DEV - 1,))], start, finish)


def sum_devices(name, slots):
    _, R, _ = slots.shape
    tr = _pick(R, (512, 256, 128, 64, 32, 16, 8))

    def body(s_ref, o_ref):
        acc = s_ref[0]
        for d in range(1, N_DEV):
            acc = acc + s_ref[d]
        o_ref[...] = acc

    return pl.pallas_call(body, grid=(R // tr,),
                          in_specs=[pl.BlockSpec((N_DEV, tr, LANES), lambda r: (0, r, 0))],
                          out_specs=pl.BlockSpec((tr, LANES), lambda r: (r, 0)),
                          out_shape=jax.ShapeDtypeStruct((R, LANES), F32), name=name,
                          compiler_params=_cparams("parallel"))(slots)


def gather_job(slots, relay_frac=0.75):
    n = len(slots)

    def copies(o_refs, send_sems, recv_sems):
        x, y, c = _place()
        me = 2 * x + y
        sib = (x, y, 1 - c)
        chips = [(_flip(x, fx), _flip(y, fy)) for fx, fy in _CHIP_FLIPS]
        ici, fwd, from_sib = [], [], []
        for t in range(n):
            rh = o_refs[t].shape[1] // 2
            mine, theirs = pl.ds(c * rh, rh), pl.ds((1 - c) * rh, rh)
            for k, (px, py) in enumerate(chips):
                own = o_refs[t].at[me, mine]
                ici.append(pltpu.make_async_remote_copy(
                    src_ref=own, dst_ref=own, send_sem=send_sems.at[t, k], recv_sem=recv_sems.at[t, k],
                    device_id=(px, py, c), device_id_type=MESH))
                landed = o_refs[t].at[2 * px + py, mine]
                arrival = pltpu.make_async_remote_copy(
                    src_ref=landed, dst_ref=landed, send_sem=send_sems.at[t, k], recv_sem=recv_sems.at[t, k],
                    device_id=(px, py, c), device_id_type=MESH)
                fwd.append((arrival, pltpu.make_async_remote_copy(
                    src_ref=landed, dst_ref=landed, send_sem=send_sems.at[t, 3 + k],
                    recv_sem=recv_sems.at[t, 3 + k], device_id=sib, device_id_type=MESH)))
                passed = o_refs[t].at[2 * px + py, theirs]
                from_sib.append(pltpu.make_async_remote_copy(
                    src_ref=passed, dst_ref=passed, send_sem=send_sems.at[t, 3 + k],
                    recv_sem=recv_sems.at[t, 3 + k], device_id=sib, device_id_type=MESH))
        return ici, fwd, from_sib

    def start(ins, o_refs, sems):
        for cp in copies(o_refs, *sems)[0]:
            cp.start()

    def relay(ins, o_refs, sems):
        for arrival, forward in copies(o_refs, *sems)[1]:
            arrival.wait_recv()
            forward.start()

    def finish(ins, o_refs, sems):
        ici, fwd, from_sib = copies(o_refs, *sems)
        for cp in from_sib:
            cp.wait_recv()
        for cp in ici:
            cp.wait_send()
        for _, forward in fwd:
            forward.wait_send()

    return _Comm(slots, [jax.ShapeDtypeStruct(s.shape, s.dtype) for s in slots], {t: t for t in range(n)},
                 [pltpu.SemaphoreType.DMA((n, 6)), pltpu.SemaphoreType.DMA((n, 6))], start, finish, relay,
                 relay_frac)


def sibling_halves_job(grads):
    n = len(grads)

    def copies(g_refs, o_refs, send_sems, recv_sems):
        x, y, c = _place()
        out = []
        for t in range(n):
            rh = g_refs[t].shape[1] // 2
            out.append(pltpu.make_async_remote_copy(
                src_ref=g_refs[t].at[:, pl.ds((1 - c) * rh, rh), :], dst_ref=o_refs[t],
                send_sem=send_sems.at[t], recv_sem=recv_sems.at[t], device_id=(x, y, 1 - c),
                device_id_type=MESH))
        return out

    def start(g_refs, o_refs, sems):
        for cp in copies(g_refs, o_refs, *sems):
            cp.start()

    def finish(g_refs, o_refs, sems):
        cps = copies(g_refs, o_refs, *sems)
        for cp in cps:
            cp.wait_recv()
        for cp in cps:
            cp.wait_send()

    return _Comm(grads, [jax.ShapeDtypeStruct((4, g.shape[1] // 2, g.shape[2]), g.dtype) for g in grads], {},
                 [pltpu.SemaphoreType.DMA((n,)), pltpu.SemaphoreType.DMA((n,))], start, finish)


def scatter_job(parts):
    n = len(parts)

    def copies(p_refs, o_refs, send_sems, recv_sems):
        x, y, c = _place()
        out = []
        for t in range(n):
            for k, (fx, fy) in enumerate(_CHIP_FLIPS):
                px, py = _flip(x, fx), _flip(y, fy)
                out.append(pltpu.make_async_remote_copy(
                    src_ref=p_refs[t].at[2 * px + py], dst_ref=o_refs[t].at[k],
                    send_sem=send_sems.at[t, k], recv_sem=recv_sems.at[t, k],
                    device_id=(px, py, c), device_id_type=MESH))
        return out

    def start(p_refs, o_refs, sems):
        for cp in copies(p_refs, o_refs, *sems):
            cp.start()

    def finish(p_refs, o_refs, sems):
        cps = copies(p_refs, o_refs, *sems)
        for cp in cps:
            cp.wait_recv()
        for cp in cps:
            cp.wait_send()

    return _Comm(parts, [jax.ShapeDtypeStruct((N_CHIPS - 1,) + p.shape[1:], p.dtype) for p in parts], {},
                 [pltpu.SemaphoreType.DMA((n, 3)), pltpu.SemaphoreType.DMA((n, 3))], start, finish)


def share_halves_job(halves):
    n = len(halves)

    def copies(o_refs, send_sems, recv_sems):
        x, y, c = _place()
        sends, arrivals = [], []
        for t in range(n):
            rh = o_refs[t].shape[0] // 2
            mine = o_refs[t].at[pl.ds(c * rh, rh)]
            theirs = o_refs[t].at[pl.ds((1 - c) * rh, rh)]
            sends.append(pltpu.make_async_remote_copy(
                src_ref=mine, dst_ref=mine, send_sem=send_sems.at[t], recv_sem=recv_sems.at[t],
                device_id=(x, y, 1 - c), device_id_type=MESH))
            arrivals.append(pltpu.make_async_remote_copy(
                src_ref=theirs, dst_ref=theirs, send_sem=send_sems.at[t], recv_sem=recv_sems.at[t],
                device_id=(x, y, 1 - c), device_id_type=MESH))
        return sends, arrivals

    def start(ins, o_refs, sems):
        for cp in copies(o_refs, *sems)[0]:
            cp.start()

    def finish(ins, o_refs, sems):
        sends, arrivals = copies(o_refs, *sems)
        for cp in arrivals:
            cp.wait_recv()
        for cp in sends:
            cp.wait_send()

    return _Comm(halves, [jax.ShapeDtypeStruct(h.shape, h.dtype) for h in halves], {t: t for t in range(n)},
                 [pltpu.SemaphoreType.DMA((n,)), pltpu.SemaphoreType.DMA((n,))], start, finish)


def _pack(arrs, row_multiple=SUBLANES):
    flat, total = [], 0
    for a in arrs:
        v = a.reshape(-1).astype(F32)
        pad = (-v.shape[0]) % (SUBLANES * LANES)
        flat.append(jnp.pad(v, (0, pad)))
        total += v.shape[0] + pad
    tail = (-total) % (row_multiple * LANES)
    if tail:
        flat.append(jnp.zeros((tail,), F32))
    return jnp.concatenate(flat).reshape(-1, LANES)


def _unpack(buf, shapes):
    out, off = [], 0
    flat = buf.reshape(-1)
    for s in shapes:
        n = int(np.prod(s))
        out.append(flat[off:off + n].reshape(s))
        off += n + ((-n) % (8 * LANES))
    return out


def _xattn_layer_fwd(tag, h, mem, gx, gmem, w):
    hx = rms_fwd(f"rms_x{tag}", h, gx)
    memn = rms_fwd(f"rms_mem{tag}", mem, gmem)
    q = mm_nn(f"xq{tag}", hx, w["q"], BF16)
    k = mm_nn(f"xk{tag}", memn, w["k"], BF16)
    v = mm_nn(f"xv{tag}", memn, w["v"], BF16)
    o, lse = xattn_fwd(f"xattn_fwd{tag}", q, k, v)
    h_out = mm_nn(f"xo{tag}", o, w["o"], F32, res=h)
    return h_out, dict(hx=hx, memn=memn, q=q, k=k, v=v, o=o, lse=lse)


def _xattn_layer_bwd(tag, dh_out, dh_out_b, h_in, mem, gx, gmem, w, sv):
    do = mm_nt(f"d_xo{tag}", dh_out_b, w["o"], BF16)
    dwo = mm_tn(f"dw_xo{tag}", sv["o"], dh_out_b)
    dq, dk, dv = xattn_bwd(f"xattn_bwd{tag}", sv["q"], sv["k"], sv["v"], sv["o"], do, sv["lse"])
    dwq = mm_tn(f"dw_xq{tag}", sv["hx"], dq)
    dhx = mm_nt(f"d_xq{tag}", dq, w["q"], F32)
    dwk = mm_tn(f"dw_xk{tag}", sv["memn"], dk)
    dwv = mm_tn(f"dw_xv{tag}", sv["memn"], dv)
    dmk = mm_nt(f"d_xk{tag}", dk, w["k"], F32)
    dmv = mm_nt(f"d_xv{tag}", dv, w["v"], F32)
    dh_in, dh_in_b, dgx = rms_bwd(f"rms_x_bwd{tag}", h_in, gx, [dhx], dh_out)
    _, _, dgmem = rms_bwd(f"rms_mem_bwd{tag}", mem, gmem, [dmk, dmv], None)
    return dh_in, dh_in_b, dgx, dgmem, dict(q=dwq, k=dwk, v=dwv, o=dwo)


def kernel(x, mem, norm_mix_g, norm_x_g, norm_mem_g, final_norm_g, w_in_ab, rel_bias, conv_w, conv_b, conv_ln_g, conv_ln_b, w_out_ab, w_in_c, sgu_ln_g, sgu_ln_b, w_s, b_s, w_out_c, w_xq, w_xk, w_xv, w_xo, loss_target, m_norm_mix_g, m_norm_x_g, m_norm_mem_g, m_final_norm_g, m_w_in_ab, m_rel_bias, m_conv_w, m_conv_b, m_conv_ln_g, m_conv_ln_b, m_w_out_ab, m_w_in_c, m_sgu_ln_g, m_sgu_ln_b, m_w_s, m_b_s, m_w_out_c, m_w_xq, m_w_xk, m_w_xv, m_w_xo, v_norm_mix_g, v_norm_x_g, v_norm_mem_g, v_final_norm_g, v_w_in_ab, v_rel_bias, v_conv_w, v_conv_b, v_conv_ln_g, v_conv_ln_b, v_w_out_ab, v_w_in_c, v_sgu_ln_g, v_sgu_ln_b, v_w_s, v_b_s, v_w_out_c, v_w_xq, v_w_xk, v_w_xv, v_w_xo):
    S, D = x.shape[1], x.shape[2]
    MIX = 2 * D
    xs, mems, tgt = x[0], mem[0], loss_target[0]
    cx, cy, cc = _place()
    chip = 2 * cx + cy
    cidx = jnp.reshape(cc, (1,)).astype(jnp.int32)
    place = jnp.stack([chip, cc]).astype(jnp.int32)

    ro, rq = MIX // 4, D // 4
    row_sharded = [("out_ab", w_out_ab[0]), ("out_c", w_out_c[0])]
    for layer in range(2):
        for nm_, w in (("q", w_xq), ("k", w_xk), ("v", w_xv), ("o", w_xo)):
            row_sharded.append((f"x{nm_}{layer}", w[layer]))
    slots = {"in_ab": cast_into_slot("cast_in_ab", w_in_ab[0], place),
             "in_c": cast_into_slot("cast_in_c", w_in_c[0], place)}
    slots.update({nm_: cast_into_slot("cast_" + nm_, w, place) for nm_, w in row_sharded})

    small_sh = [conv_w[0], sgu_ln_g[0], sgu_ln_b[0]]
    gathered = exchange_small("gather_small", _pack(small_sh), reduce=False)
    per_chip = [_unpack(gathered[2 * j], [a.shape for a in small_sh]) for j in range(N_CHIPS)]
    conv_w_full = jnp.concatenate([p[0] for p in per_chip], axis=1)
    sgu_g_full = jnp.concatenate([p[1] for p in per_chip], axis=0).reshape(1, MIX)
    sgu_b_full = jnp.concatenate([p[2] for p in per_chip], axis=0).reshape(1, MIX)
    cw_pad = jnp.pad(conv_w_full, ((0, CONV_HALO - CONV_WIDTH), (0, 0)))
    cb = conv_b.reshape(1, D)
    clg, clb = conv_ln_g.reshape(1, D), conv_ln_b.reshape(1, D)
    ws = w_s[0]
    bst = jnp.transpose(b_s[0])
    tq = _attn_tq(S)
    bm = band_bias_table(rel_bias[0], tq)

    hn0 = rms_fwd("rms_mix0", xs, norm_mix_g[0])
    layer0 = ["out_ab", "xq0", "xk0", "xv0", "xo0"]
    layer1 = ["out_c", "xq1", "xk1", "xv1", "xo1"]
    proj_own, (wab4,) = proj_cols_own("proj_ab_own", hn0, w_in_ab[0], place,
                                      comm=gather_job([slots["in_ab"]], relay_frac=1.0))
    proj0, got0 = proj_cols_rest("proj_ab_rest", hn0, wab4, proj_own, place,
                                 comm=gather_job([slots[n] for n in layer0]))
    (ya, lse_a), (wc4,) = attn_fwd(proj0, bm, D, comm=gather_job([slots["in_c"]]))
    (y0, cpre), got1 = conv_gate_fwd(proj0, ya, cw_pad, cb, clg, clb, D,
                                     comm=gather_job([slots[n] for n in layer1]))
    wrow = {n: g.reshape(-1, g.shape[2]) for n, g in zip(layer0 + layer1, got0 + got1)}
    wx = [{k: wrow[f"x{k}{layer}"] for k in "qkvo"} for layer in range(2)]
    h1 = mm_nn("out_ab", y0, wrow["out_ab"], F32, res=xs)
    h2, sx0 = _xattn_layer_fwd("0", h1, mems, norm_x_g[0], norm_mem_g[0], wx[0])
    hn1 = rms_fwd("rms_mix1", h2, norm_mix_g[1])
    proj1 = mm_nn_cols("proj_c", hn1, wc4, BF16)
    y1 = sgu_fwd(proj1, sgu_g_full, sgu_b_full, ws, bst, MIX)
    h3 = mm_nn("out_c", y1, wrow["out_c"], F32, res=h2)
    h4, sx1 = _xattn_layer_fwd("1", h3, mems, norm_x_g[1], norm_mem_g[1], wx[1])
    loss_row, dg_final, dh4, dh4b = loss_head("loss_head", h4, final_norm_g, tgt)

    dh3, dh3b, dgx1, dgmem1, dwx1 = _xattn_layer_bwd("1", dh4, dh4b, h3, mems, norm_x_g[1], norm_mem_g[1], wx[1], sx1)
    def stack_rows(dw_out, dwx):
        return jnp.concatenate([g.reshape(N_CHIPS, -1, g.shape[1]) for g in [dw_out] + [dwx[k] for k in "qkvo"]],
                               axis=1)

    dy1 = mm_nt("d_out_c", dh3b, wrow["out_c"], BF16)
    dw_out_c = mm_tn("dw_out_c", y1, dh3b)
    dproj1, dws, dbst, dsgu_g, dsgu_b = sgu_bwd(dy1, proj1, sgu_g_full, sgu_b_full, ws, bst, MIX)
    grp1 = stack_rows(dw_out_c, dwx1)
    dw_in_c, (sib1,) = mm_tn_cols("dw_in_c", hn1, dproj1, comm=sibling_halves_job([grp1]))
    part1 = add_halves("add_halves1", grp1, sib1, cidx)
    dhn1, (recv1, sib2) = mm_nt_cols("d_proj_c", dproj1, wc4, F32,
                                     comm=_join(scatter_job([part1]), sibling_halves_job([dw_in_c])))
    part2 = add_halves("add_halves2", dw_in_c, sib2, cidx)
    dh2, dh2b, dgmix1 = rms_bwd("rms_mix1_bwd", h2, norm_mix_g[1], [dhn1], dh3)
    dh1, dh1b, dgx0, dgmem0, dwx0 = _xattn_layer_bwd("0", dh2, dh2b, h1, mems, norm_x_g[0], norm_mem_g[0], wx[0], sx0)
    dy0 = mm_nt("d_out_ab", dh1b, wrow["out_ab"], BF16)
    dw_out_ab = mm_tn("dw_out_ab", y0, dh1b)
    grp3 = stack_rows(dw_out_ab, dwx0)
    dya, dgate, dc, dclg, dclb = conv_gate_bwd_a(dy0, proj0, ya, cpre, clg, clb, D)
    (da, db, dcw, dcb), (recv2, sib3) = conv_gate_bwd_b(
        dc, proj0, cw_pad, D, comm=_join(scatter_job([part2]), sibling_halves_job([grp3])))
    part3 = add_halves("add_halves3", grp3, sib3, cidx)
    (dq, dkc, dkp, dvc, dvp, ds_sum), (recv3,) = attn_bwd(proj0, ya, dya, lse_a, bm, D, comm=scatter_job([part3]))
    drel = rel_bias_grad(ds_sum)
    dproj0 = assemble_dproj0(dq, dkc, dkp, dvc, dvp, da, db, dgate, D)
    dw_in_ab = mm_tn_cols("dw_in_ab", hn0, dproj0)
    (sib4,) = run_comm("sibling_halves4", sibling_halves_job([dw_in_ab]))
    part4 = add_halves("add_halves4", dw_in_ab, sib4, cidx)
    halves = [sum_chips(f"sum_chips{t + 1}", p, r, place)
              for t, (p, r) in enumerate(((part1, recv1), (part2, recv2), (part3, recv3)))]
    small_early = [
        jnp.concatenate([dgx0, dgx1], axis=0), jnp.concatenate([dgmem0, dgmem1], axis=0), dg_final.reshape(D),
        drel[None], dcb, dclg, dclb, dws[None], jnp.transpose(dbst)[None],
        dcw[:CONV_WIDTH][None], dsgu_g, dsgu_b]
    early = _pack(small_early, row_multiple=512)
    dhn0, (recv4, small_slots, g_r1, g_c, g_r0) = mm_nt_cols(
        "d_proj_ab", dproj0, wab4, F32,
        comm=_join(scatter_job([part4]), exchange_job(early), share_halves_job(halves)))
    dx, _, dgmix0 = rms_bwd("rms_mix0_bwd", xs, norm_mix_g[0], [dhn0], dh1)
    (g_ab,) = run_comm("share_reduced_half4", share_halves_job([sum_chips("sum_chips4", part4, recv4, place)]))

    me = 4 * cx + 2 * cy + cc
    small_slots = lax.dynamic_update_slice(small_slots, early[None], (me, 0, 0))
    summed = _unpack(sum_devices("sum_small", small_slots), [a.shape for a in small_early])
    (g_norm_x, g_norm_mem, g_final, g_rel, g_conv_b, g_clg, g_clb, g_ws, g_bs,
     g_conv_w_full, g_sgu_g_full, g_sgu_b_full) = summed
    dgmix = jnp.concatenate([dgmix0, dgmix1], axis=0)
    (g_norm_mix,) = _unpack(exchange_small("reduce_late", _pack([dgmix]), reduce=True), [dgmix.shape])
    cws = conv_w.shape[2]
    g_conv_w = lax.dynamic_slice_in_dim(g_conv_w_full, chip * cws, cws, axis=2)
    sgs = sgu_ln_g.shape[1]
    g_sgu_g = lax.dynamic_slice_in_dim(g_sgu_g_full, chip * sgs, sgs, axis=1)
    g_sgu_b = lax.dynamic_slice_in_dim(g_sgu_b_full, chip * sgs, sgs, axis=1)

    loss = lax.psum(loss_row[0, 0], ("x", "y", "c"))

    g_rows = {"w_out_ab": g_r0[0:ro][None], "w_out_c": g_r1[0:ro][None]}
    for i, nm_ in enumerate("qkvo"):
        lo = ro + i * rq
        g_rows["w_x" + nm_] = jnp.stack([g_r0[lo:lo + rq], g_r1[lo:lo + rq]])
    grads = dict(
        norm_mix_g=g_norm_mix, norm_x_g=g_norm_x, norm_mem_g=g_norm_mem, final_norm_g=g_final,
        w_in_ab=g_ab[None], rel_bias=g_rel, conv_w=g_conv_w, conv_b=g_conv_b, conv_ln_g=g_clg, conv_ln_b=g_clb,
        w_out_ab=g_rows["w_out_ab"], w_in_c=g_c[None], sgu_ln_g=g_sgu_g, sgu_ln_b=g_sgu_b, w_s=g_ws, b_s=g_bs,
        w_out_c=g_rows["w_out_c"], w_xq=g_rows["w_xq"], w_xk=g_rows["w_xk"], w_xv=g_rows["w_xv"],
        w_xo=g_rows["w_xo"])
    weights = dict(
        norm_mix_g=(norm_mix_g, m_norm_mix_g, v_norm_mix_g), norm_x_g=(norm_x_g, m_norm_x_g, v_norm_x_g),
        norm_mem_g=(norm_mem_g, m_norm_mem_g, v_norm_mem_g), final_norm_g=(final_norm_g, m_final_norm_g, v_final_norm_g),
        w_in_ab=(w_in_ab, m_w_in_ab, v_w_in_ab), rel_bias=(rel_bias, m_rel_bias, v_rel_bias),
        conv_w=(conv_w, m_conv_w, v_conv_w), conv_b=(conv_b, m_conv_b, v_conv_b),
        conv_ln_g=(conv_ln_g, m_conv_ln_g, v_conv_ln_g), conv_ln_b=(conv_ln_b, m_conv_ln_b, v_conv_ln_b),
        w_out_ab=(w_out_ab, m_w_out_ab, v_w_out_ab), w_in_c=(w_in_c, m_w_in_c, v_w_in_c),
        sgu_ln_g=(sgu_ln_g, m_sgu_ln_g, v_sgu_ln_g), sgu_ln_b=(sgu_ln_b, m_sgu_ln_b, v_sgu_ln_b),
        w_s=(w_s, m_w_s, v_w_s), b_s=(b_s, m_b_s, v_b_s), w_out_c=(w_out_c, m_w_out_c, v_w_out_c),
        w_xq=(w_xq, m_w_xq, v_w_xq), w_xk=(w_xk, m_w_xk, v_w_xk), w_xv=(w_xv, m_w_xv, v_w_xv),
        w_xo=(w_xo, m_w_xo, v_w_xo))
    names = list(weights)
    big_names = ("w_in_ab", "w_out_ab", "w_in_c", "w_out_c", "w_xq", "w_xk", "w_xv", "w_xo")
    delta, new_m, new_v = {}, {}, {}
    for nm_ in big_names:
        w, m, v = weights[nm_]
        C = w.shape[-1]
        d2, m2, v2 = adamw("adamw_" + nm_, w.reshape(-1, C), grads[nm_].reshape(-1, C), m.reshape(-1, C),
                           v.reshape(-1, C))
        delta[nm_], new_m[nm_], new_v[nm_] = d2.reshape(w.shape), m2.reshape(w.shape), v2.reshape(w.shape)
    small_names = [n for n in names if n not in big_names]
    shapes = [weights[n][0].shape for n in small_names]
    d_s, m_s, v_s = adamw("adamw_small", _pack([weights[n][0] for n in small_names]),
                          _pack([grads[n] for n in small_names]), _pack([weights[n][1] for n in small_names]),
                          _pack([weights[n][2] for n in small_names]))
    for n, d_, m_, v_ in zip(small_names, _unpack(d_s, shapes), _unpack(m_s, shapes), _unpack(v_s, shapes)):
        delta[n], new_m[n], new_v[n] = d_, m_, v_

    return (loss, dx[None], *[grads[n].reshape(weights[n][0].shape) for n in names], *[delta[n] for n in names],
            *[new_m[n] for n in names], *[new_v[n] for n in names])
```

```python
import functools

import numpy as np
import jax
import jax.numpy as jnp
from jax import lax
from jax.experimental import pallas as pl
from jax.experimental.pallas import tpu as pltpu

F32 = jnp.float32
BF16 = jnp.bfloat16
MESH = pl.DeviceIdType.MESH

EPS = 1e-6
CHUNK = 64
N_PAST_CHUNKS = 8
MAX_REL = 128
HEAD_DIM_A = 128
CONV_WIDTH = 31
CONV_HALO = 32
GMLP_CHUNK = 128
N_GROUPS_C = 8
N_HEADS_X = 4
NEG = -1e30

ADAM_LR = 0.001
ADAM_B1 = 0.9
ADAM_B2 = 0.999
ADAM_EPS = 1e-08
ADAM_WD = 0.01
ADAM_STEP = 10

N_CHIPS = 4
N_DEV = 8
V7X_VMEM_LIMIT = 56 * 1024 * 1024
LANES = 128
SUBLANES = 8


def _pick(n, cands):
    for c in cands:
        if c <= n and n % c == 0:
            return c
    return n


def _cparams(*sem):
    return pltpu.CompilerParams(dimension_semantics=sem, vmem_limit_bytes=V7X_VMEM_LIMIT)


def _sigmoid(x):
    return 0.5 * jnp.tanh(0.5 * x) + 0.5


def _dot(a, b, contract):
    return lax.dot_general(a, b, (contract, ((), ())), preferred_element_type=F32)


NN = ((1,), (0,))
NT = ((1,), (1,))
TN = ((0,), (0,))


class _Comm:
    def __init__(self, arrays, out_shapes, aliases, sems, start, finish, relay=None, relay_frac=0.75):
        self.arrays, self.out_shapes, self.aliases, self.sems = list(arrays), list(out_shapes), dict(aliases), list(sems)
        self.start, self.finish, self.relay = start, finish, relay
        self.relay_frac = relay_frac


def _join(*jobs):
    assert all(j.relay is None for j in jobs)
    arrays, outs, sems, aliases, spans = [], [], [], {}, []
    for j in jobs:
        spans.append((len(arrays), len(outs), len(sems)))
        aliases.update({len(arrays) + i: len(outs) + o for i, o in j.aliases.items()})
        arrays += j.arrays
        outs += j.out_shapes
        sems += j.sems

    def part(j, span, ins, os_, ss):
        a0, o0, s0 = span
        return (ins[a0:a0 + len(j.arrays)], os_[o0:o0 + len(j.out_shapes)], ss[s0:s0 + len(j.sems)])

    def start(ins, os_, ss):
        for j, span in zip(jobs, spans):
            j.start(*part(j, span, ins, os_, ss))

    def finish(ins, os_, ss):
        for j, span in zip(jobs, spans):
            j.finish(*part(j, span, ins, os_, ss))

    return _Comm(arrays, outs, aliases, sems, start, finish)


def _call(body, *, name, grid, in_specs, out_specs, out_shape, args, scratch_shapes=(), sem=None, comm=None,
          prefetch=None, io_aliases=None):
    multi = isinstance(out_shape, (list, tuple))
    o_shapes = list(out_shape) if multi else [out_shape]
    o_specs = list(out_specs) if multi else [out_specs]
    if comm is None:
        assert prefetch is None and io_aliases is None
        return pl.pallas_call(body, grid=grid, in_specs=in_specs, out_specs=out_specs, out_shape=out_shape,
                              scratch_shapes=list(scratch_shapes), name=name,
                              compiler_params=_cparams(*sem))(*args)
    n_in, n_out, n_scr = len(in_specs), len(o_shapes), len(scratch_shapes)
    n_ci, n_co = len(comm.arrays), len(comm.out_shapes)
    n_steps = int(np.prod(grid))
    n_pre = 0 if prefetch is None else 1

    def carrier(*refs):
        pre, refs = refs[:n_pre], refs[n_pre:]
        ins, rest = refs[:n_in], refs[n_in:]
        cins, rest = rest[:n_ci], rest[n_ci:]
        outs, rest = rest[:n_out], rest[n_out:]
        couts, rest = rest[:n_co], rest[n_co:]
        scr, csems = rest[:n_scr], rest[n_scr:]
        step = 0
        for a, g in enumerate(grid):
            step = step * g + pl.program_id(a)

        @pl.when(step == 0)
        def _():
            comm.start(cins, couts, csems)

        body(*pre, *ins, *outs, *scr)

        relay_step = min(int(comm.relay_frac * n_steps), n_steps - 1)
        if comm.relay is not None and relay_step < n_steps - 1:
            @pl.when(step == relay_step)
            def _():
                comm.relay(cins, couts, csems)

        @pl.when(step == n_steps - 1)
        def _():
            if comm.relay is not None and relay_step == n_steps - 1:
                comm.relay(cins, couts, csems)
            comm.finish(cins, couts, csems)

    aliases = {n_pre + n_in + i: n_out + o for i, o in comm.aliases.items()}
    aliases.update({n_pre + i: o for i, o in (io_aliases or {}).items()})
    all_in = list(in_specs) + [HBM_SPEC] * n_ci
    all_out = o_specs + [HBM_SPEC] * n_co
    all_scratch = list(scratch_shapes) + comm.sems
    params = _cparams(*(["arbitrary"] * len(grid)))
    if prefetch is None:
        res = pl.pallas_call(
            carrier, grid=grid, in_specs=all_in, out_specs=all_out, out_shape=o_shapes + comm.out_shapes,
            input_output_aliases=aliases, scratch_shapes=all_scratch, name=name,
            compiler_params=params)(*args, *comm.arrays)
    else:
        grid_spec = pltpu.PrefetchScalarGridSpec(num_scalar_prefetch=1, grid=grid, in_specs=all_in,
                                                 out_specs=all_out, scratch_shapes=all_scratch)
        res = pl.pallas_call(
            carrier, grid_spec=grid_spec, out_shape=o_shapes + comm.out_shapes, input_output_aliases=aliases,
            name=name, compiler_params=params)(prefetch, *args, *comm.arrays)
    mine = list(res[:n_out]) if multi else res[0]
    return mine, list(res[n_out:])


def run_comm(name, comm):
    def body(*refs):
        n_ci, n_co = len(comm.arrays), len(comm.out_shapes)
        cins, couts, csems = refs[:n_ci], refs[n_ci:n_ci + n_co], refs[n_ci + n_co:]
        comm.start(cins, couts, csems)
        if comm.relay is not None:
            comm.relay(cins, couts, csems)
        comm.finish(cins, couts, csems)

    return pl.pallas_call(
        body, in_specs=[HBM_SPEC] * len(comm.arrays), out_specs=[HBM_SPEC] * len(comm.out_shapes),
        out_shape=comm.out_shapes, input_output_aliases=comm.aliases, scratch_shapes=comm.sems,
        name=name)(*comm.arrays)


def _mm(name, a, b, *, contract, grid, a_spec, b_spec, o_spec, out_shape, res=None, comm=None):
    nk = grid[2]

    def body(*refs):
        if res is not None:
            a_ref, b_ref, r_ref, o_ref = refs[:4]
        else:
            a_ref, b_ref, o_ref = refs[:3]
            r_ref = None
        p = _dot(a_ref[...].astype(BF16), b_ref[...].astype(BF16), contract)

        def finish(acc):
            if r_ref is not None:
                acc = acc + r_ref[...]
            o_ref[...] = acc.astype(o_ref.dtype)

        if nk == 1:
            finish(p)
        else:
            acc_ref = refs[-1]
            k = pl.program_id(2)

            @pl.when(k == 0)
            def _():
                acc_ref[...] = p

            @pl.when(k > 0)
            def _():
                acc_ref[...] += p

            @pl.when(k == nk - 1)
            def _():
                finish(acc_ref[...])

    in_specs = [a_spec, b_spec]
    args = [a, b]
    if res is not None:
        in_specs.append(o_spec)
        args.append(res)
    blk = tuple(d for d in o_spec.block_shape if d is not None)
    scratch = [] if nk == 1 else [pltpu.VMEM(blk, F32)]
    return _call(body, name=name, grid=grid, in_specs=in_specs, out_specs=o_spec, out_shape=out_shape,
                 args=args, scratch_shapes=scratch, sem=("parallel", "parallel", "arbitrary"), comm=comm)


def mm_nn_cols(name, a, w4, out_dtype, comm=None):
    M, K = a.shape
    _, _, C = w4.shape
    tm = _pick(M, (1024, 512, 256))
    tn = _pick(C, (1024, 512, 256, 128))
    nps = C // tn
    return _mm(name, a, w4, contract=NN, grid=(M // tm, 4 * nps, 1),
               a_spec=pl.BlockSpec((tm, K), lambda i, j, k: (i, 0)),
               b_spec=pl.BlockSpec((None, K, tn), lambda i, j, k: (j // nps, 0, j % nps)),
               o_spec=pl.BlockSpec((tm, tn), lambda i, j, k: (i, j)),
               out_shape=jax.ShapeDtypeStruct((M, 4 * C), out_dtype), comm=comm)


def proj_cols_own(name, a, w_own, place, comm):
    M, K = a.shape
    C = w_own.shape[1]
    tm = _pick(M, (1024, 512, 256))
    tn = _pick(C, (512, 256, 128))
    nps = C // tn

    def body(s_ref, a_ref, b_ref, o_ref):
        o_ref[...] = _dot(a_ref[...], b_ref[...].astype(BF16), NN).astype(o_ref.dtype)

    return _call(body, name=name, grid=(M // tm, nps),
                 in_specs=[pl.BlockSpec((tm, K), lambda i, j, s: (i, 0)),
                           pl.BlockSpec((K, tn), lambda i, j, s: (0, j))],
                 out_specs=pl.BlockSpec((tm, tn), lambda i, j, s: (i, s[0] * nps + j)),
                 out_shape=jax.ShapeDtypeStruct((M, N_CHIPS * C), BF16), args=[a, w_own], comm=comm,
                 prefetch=place)


def proj_cols_rest(name, a, w4, partial, place, masks, comm):
    M, K = a.shape
    C = w4.shape[2]
    tm = _pick(M, (1024, 512, 256))
    tn = _pick(C, (512, 256, 128))
    nps = C // tn
    assert len(masks) in (1, 2)
    step = masks[-1] - masks[0]

    def slot(j, s):
        return jnp.bitwise_xor(s[0], masks[0] + step * (j // nps))

    def body(s_ref, a_ref, b_ref, part_ref, o_ref):
        o_ref[...] = _dot(a_ref[...], b_ref[...], NN).astype(o_ref.dtype)

    return _call(body, name=name, grid=(M // tm, len(masks) * nps),
                 in_specs=[pl.BlockSpec((tm, K), lambda i, j, s: (i, 0)),
                           pl.BlockSpec((None, K, tn), lambda i, j, s: (slot(j, s), 0, j % nps)),
                           HBM_SPEC],
                 out_specs=pl.BlockSpec((tm, tn), lambda i, j, s: (i, slot(j, s) * nps + j % nps)),
                 out_shape=jax.ShapeDtypeStruct(partial.shape, partial.dtype), args=[a, w4, partial],
                 comm=comm, prefetch=place, io_aliases={2: 0})


def mm_nn(name, a, w, out_dtype, res=None):
    M, K = a.shape
    N = w.shape[1]
    tm = _pick(M, (1024, 512, 256))
    tn = _pick(N, (512, 256, 128))
    return _mm(name, a, w, contract=NN, grid=(M // tm, N // tn, 1),
               a_spec=pl.BlockSpec((tm, K), lambda i, j, k: (i, 0)),
               b_spec=pl.BlockSpec((K, tn), lambda i, j, k: (0, j)),
               o_spec=pl.BlockSpec((tm, tn), lambda i, j, k: (i, j)),
               out_shape=jax.ShapeDtypeStruct((M, N), out_dtype), res=res)


def mm_nt_cols(name, a, w4, out_dtype, comm=None):
    M = a.shape[0]
    _, K, C = w4.shape
    tm = _pick(M, (1024, 512, 256))
    tn = _pick(K, (1024, 512, 256, 128))
    tk = _pick(C, (3584, 3072, 1792, 1536, 1024, 512, 256, 128))
    kps = C // tk
    return _mm(name, a, w4, contract=NT, grid=(M // tm, K // tn, 4 * kps),
               a_spec=pl.BlockSpec((tm, tk), lambda i, j, k: (i, k)),
               b_spec=pl.BlockSpec((None, tn, tk), lambda i, j, k: (k // kps, j, k % kps)),
               o_spec=pl.BlockSpec((tm, tn), lambda i, j, k: (i, j)),
               out_shape=jax.ShapeDtypeStruct((M, K), out_dtype), comm=comm)


def mm_nt(name, a, w, out_dtype):
    M, C = a.shape
    N = w.shape[0]
    tm = _pick(M, (1024, 512, 256))
    tn = _pick(N, (512, 256, 128))
    return _mm(name, a, w, contract=NT, grid=(M // tm, N // tn, 1),
               a_spec=pl.BlockSpec((tm, C), lambda i, j, k: (i, 0)),
               b_spec=pl.BlockSpec((tn, C), lambda i, j, k: (j, 0)),
               o_spec=pl.BlockSpec((tm, tn), lambda i, j, k: (i, j)),
               out_shape=jax.ShapeDtypeStruct((M, N), out_dtype))


def mm_tn_cols(name, a, b, comm=None):
    S, K = a.shape
    C = b.shape[1] // 4
    ts = _pick(S, (2048, 1024, 512, 256))
    tko = _pick(K, (1024, 512, 256, 128))
    tn = _pick(C, (1792, 1536, 1024, 512, 256, 128))
    nps = C // tn
    return _mm(name, a, b, contract=TN, grid=(K // tko, 4 * nps, S // ts),
               a_spec=pl.BlockSpec((ts, tko), lambda i, j, k: (k, i)),
               b_spec=pl.BlockSpec((ts, tn), lambda i, j, k: (k, j)),
               o_spec=pl.BlockSpec((None, tko, tn), lambda i, j, k: (j // nps, i, j % nps)),
               out_shape=jax.ShapeDtypeStruct((4, K, C), BF16), comm=comm)


def mm_tn(name, a, b):
    S, K = a.shape
    N = b.shape[1]
    ts = _pick(S, (1024, 512, 256))
    tko = _pick(K, (2048, 1024, 512, 256, 128))
    tn = _pick(N, (1024, 512, 256, 128))
    return _mm(name, a, b, contract=TN, grid=(K // tko, N // tn, S // ts),
               a_spec=pl.BlockSpec((ts, tko), lambda i, j, k: (k, i)),
               b_spec=pl.BlockSpec((ts, tn), lambda i, j, k: (k, j)),
               o_spec=pl.BlockSpec((tko, tn), lambda i, j, k: (i, j)),
               out_shape=jax.ShapeDtypeStruct((K, N), BF16))


def rms_fwd(name, x, g):
    S, D = x.shape
    T = _pick(S, (512, 256))

    def body(x_ref, g_ref, o_ref):
        xf = x_ref[...]
        r = lax.rsqrt(jnp.mean(xf * xf, axis=-1, keepdims=True) + EPS)
        o_ref[...] = (xf * r * g_ref[...]).astype(o_ref.dtype)

    return pl.pallas_call(
        body, grid=(S // T,),
        in_specs=[pl.BlockSpec((T, D), lambda i: (i, 0)), pl.BlockSpec((1, D), lambda i: (0, 0))],
        out_specs=pl.BlockSpec((T, D), lambda i: (i, 0)),
        out_shape=jax.ShapeDtypeStruct((S, D), BF16), name=name,
        compiler_params=_cparams("parallel"))(x, g.reshape(1, D))


def rms_bwd(name, x, g, dys, dres):
    S, D = x.shape
    T = _pick(S, (256,))
    ndy = len(dys)
    has_res = dres is not None

    def body(*refs):
        x_ref, g_ref = refs[0], refs[1]
        dy_refs = refs[2:2 + ndy]
        r_ref = refs[2 + ndy] if has_res else None
        dx_ref, dxb_ref, dg_ref = refs[-3], refs[-2], refs[-1]
        i = pl.program_id(0)
        xf = x_ref[...]
        r = lax.rsqrt(jnp.mean(xf * xf, axis=-1, keepdims=True) + EPS)
        xhat = xf * r
        dy = dy_refs[0][...].astype(F32)
        for d in dy_refs[1:]:
            dy = dy + d[...].astype(F32)
        dxhat = dy * g_ref[...]
        dx = r * (dxhat - xhat * jnp.mean(dxhat * xhat, axis=-1, keepdims=True))
        if has_res:
            dx = dx + r_ref[...]
        dx_ref[...] = dx
        dxb_ref[...] = dx.astype(dxb_ref.dtype)
        dg = jnp.sum(dy * xhat, axis=0, keepdims=True)

        @pl.when(i == 0)
        def _():
            dg_ref[...] = dg

        @pl.when(i > 0)
        def _():
            dg_ref[...] += dg

    row = pl.BlockSpec((T, D), lambda i: (i, 0))
    vec = pl.BlockSpec((1, D), lambda i: (0, 0))
    args = [x, g.reshape(1, D), *dys] + ([dres] if has_res else [])
    return pl.pallas_call(
        body, grid=(S // T,),
        in_specs=[row, vec] + [row] * (ndy + int(has_res)),
        out_specs=[row, row, vec],
        out_shape=[jax.ShapeDtypeStruct((S, D), F32), jax.ShapeDtypeStruct((S, D), BF16),
                   jax.ShapeDtypeStruct((1, D), F32)],
        name=name, compiler_params=_cparams("arbitrary"))(*args)


def loss_head(name, h, g, target):
    S, D = h.shape
    T = _pick(S, (256,))

    def body(h_ref, g_ref, t_ref, loss_ref, dg_ref, dh_ref, dhb_ref):
        i = pl.program_id(0)
        xf = h_ref[...]
        gv = g_ref[...]
        r = lax.rsqrt(jnp.mean(xf * xf, axis=-1, keepdims=True) + EPS)
        xhat = xf * r
        err = xhat * gv - t_ref[...]
        part = 0.5 * jnp.sum(jnp.sum(err * err, axis=-1, keepdims=True), axis=0, keepdims=True) / D
        dout = err / D
        dxhat = dout * gv
        dh = r * (dxhat - xhat * jnp.mean(dxhat * xhat, axis=-1, keepdims=True))
        dh_ref[...] = dh
        dhb_ref[...] = dh.astype(dhb_ref.dtype)
        dg = jnp.sum(dout * xhat, axis=0, keepdims=True)
        lrow = jnp.broadcast_to(part, (1, LANES))

        @pl.when(i == 0)
        def _():
            dg_ref[...] = dg
            loss_ref[...] = lrow

        @pl.when(i > 0)
        def _():
            dg_ref[...] += dg
            loss_ref[...] += lrow

    row = pl.BlockSpec((T, D), lambda i: (i, 0))
    vec = pl.BlockSpec((1, D), lambda i: (0, 0))
    return pl.pallas_call(
        body, grid=(S // T,), in_specs=[row, vec, row],
        out_specs=[pl.BlockSpec((1, LANES), lambda i: (0, 0)), vec, row, row],
        out_shape=[jax.ShapeDtypeStruct((1, LANES), F32), jax.ShapeDtypeStruct((1, D), F32),
                   jax.ShapeDtypeStruct((S, D), F32), jax.ShapeDtypeStruct((S, D), BF16)],
        name=name, compiler_params=_cparams("arbitrary"))(h, g.reshape(1, D), target)


def _attn_tq(S):
    return _pick(S, (512,))


def band_bias_table(rel_bias, tq):
    H = rel_bias.shape[0]
    w = 2 * tq
    nbits = int(np.log2(tq))
    assert (1 << nbits) == tq and (N_PAST_CHUNKS + 2) * CHUNK - 1 <= w
    c = np.arange(w)
    d0 = np.where(c <= tq + CHUNK - 1, tq - c, tq + w - c)
    base = jnp.take(rel_bias.astype(F32), jnp.asarray(np.clip(d0, -MAX_REL, MAX_REL) + MAX_REL), axis=1)

    def body(b_ref, o_ref):
        x = jnp.broadcast_to(b_ref[...], (tq, w))
        row = lax.broadcasted_iota(jnp.int32, (tq, w), 0)
        col = lax.broadcasted_iota(jnp.int32, (tq, w), 1)
        for b in range(nbits):
            x = jnp.where(((row >> b) & 1) == 1, pltpu.roll(x, 1 << b, 1), x)
        qc = row // CHUNK
        kc = col // CHUNK - tq // CHUNK
        o_ref[...] = jnp.where((kc <= qc) & (kc >= qc - N_PAST_CHUNKS), x, NEG)

    return pl.pallas_call(
        body, grid=(H,), in_specs=[pl.BlockSpec((None, 1, w), lambda h: (h, 0, 0))],
        out_specs=pl.BlockSpec((None, tq, w), lambda h: (h, 0, 0)),
        out_shape=jax.ShapeDtypeStruct((H, tq, w), F32), name="band_bias_table",
        compiler_params=_cparams("parallel"))(base.reshape(H, 1, w))


def _attn_subblocks(tq):
    sub = tq // 2
    assert sub % CHUNK == 0 and N_PAST_CHUNKS * CHUNK == tq
    return sub, 3


def attn_fwd(proj, bm, D, comm=None):
    S = proj.shape[0]
    H = D // HEAD_DIM_A
    tq = _attn_tq(S)
    nb = S // tq
    scale = HEAD_DIM_A ** -0.5

    sub, n_sub = _attn_subblocks(tq)

    def body(q_ref, kp_ref, kc_ref, vp_ref, vc_ref, bm_ref, o_ref, lse_ref):
        i = pl.program_id(1)
        for qh in range(tq // sub):
            rows = slice(qh * sub, (qh + 1) * sub)
            q = q_ref[rows, :]
            ss = []
            for kb in range(qh, qh + n_sub):
                k_ref, krows = (kp_ref, kb) if kb < tq // sub else (kc_ref, kb - tq // sub)
                s = _dot(q, k_ref[krows * sub:(krows + 1) * sub, :], NT) * scale + bm_ref[rows, kb * sub:(kb + 1) * sub]
                if kb < tq // sub:
                    s = jnp.where(i == 0, NEG, s)
                ss.append(s)
            m = functools.reduce(jnp.maximum, [jnp.max(s, axis=-1, keepdims=True) for s in ss])
            ps = [jnp.exp(s - m) for s in ss]
            l = functools.reduce(jnp.add, [jnp.sum(p, axis=-1, keepdims=True) for p in ps])
            o = None
            for p, kb in zip(ps, range(qh, qh + n_sub)):
                v_ref, vrows = (vp_ref, kb) if kb < tq // sub else (vc_ref, kb - tq // sub)
                t = _dot(p.astype(BF16), v_ref[vrows * sub:(vrows + 1) * sub, :], NN)
                o = t if o is None else o + t
            o_ref[rows, :] = (o / l).astype(o_ref.dtype)
            lse_ref[rows, :] = m + jnp.log(l)

    def col(base):
        return (pl.BlockSpec((tq, HEAD_DIM_A), lambda h, i: (jnp.maximum(i - 1, 0), base + h)),
                pl.BlockSpec((tq, HEAD_DIM_A), lambda h, i: (i, base + h)))

    kp, kc = col(H)
    vp, vc = col(2 * H)
    return _call(
        body, name="attn_fwd", grid=(H, nb),
        in_specs=[pl.BlockSpec((tq, HEAD_DIM_A), lambda h, i: (i, h)), kp, kc, vp, vc,
                  pl.BlockSpec((None, tq, 2 * tq), lambda h, i: (h, 0, 0))],
        out_specs=[pl.BlockSpec((tq, HEAD_DIM_A), lambda h, i: (i, h)),
                   pl.BlockSpec((None, tq, 1), lambda h, i: (h, i, 0))],
        out_shape=[jax.ShapeDtypeStruct((S, D), BF16), jax.ShapeDtypeStruct((H, S, 1), F32)],
        args=[proj, proj, proj, proj, proj, bm], sem=("parallel", "arbitrary"), comm=comm)


def attn_bwd(proj, ya, dya, lse, bm, D, comm=None):
    S = proj.shape[0]
    H = D // HEAD_DIM_A
    tq = _attn_tq(S)
    nb = S // tq
    scale = HEAD_DIM_A ** -0.5
    sub, n_sub = _attn_subblocks(tq)

    def body(q_ref, kp_ref, kc_ref, vp_ref, vc_ref, o_ref, do_ref, lse_ref, bm_ref,
             dq_ref, dkc_ref, dkp_ref, dvc_ref, dvp_ref, ds_ref):
        i = pl.program_id(1)
        per = tq // sub

        @pl.when(i == 0)
        def _():
            ds_ref[...] = jnp.zeros_like(ds_ref)

        dk_acc = [None] * (2 * per)
        dv_acc = [None] * (2 * per)
        for qh in range(per):
            rows = slice(qh * sub, (qh + 1) * sub)
            q = q_ref[rows, :]
            do = do_ref[rows, :]
            delta = jnp.sum(do.astype(F32) * o_ref[rows, :].astype(F32), axis=-1, keepdims=True)
            lse_v = lse_ref[rows, :]
            dq = None
            for kb in range(qh, qh + n_sub):
                k_ref, v_ref, kr = (kp_ref, vp_ref, kb) if kb < per else (kc_ref, vc_ref, kb - per)
                k = k_ref[kr * sub:(kr + 1) * sub, :]
                cols = slice(kb * sub, (kb + 1) * sub)
                s = _dot(q, k, NT) * scale + bm_ref[rows, cols]
                if kb < per:
                    s = jnp.where(i == 0, NEG, s)
                p = jnp.exp(s - lse_v)
                dv = _dot(p.astype(BF16), do, TN)
                dp = _dot(do, v_ref[kr * sub:(kr + 1) * sub, :], NT)
                ds = p * (dp - delta)
                dsb = ds.astype(BF16)
                t = _dot(dsb, k, NN)
                dq = t if dq is None else dq + t
                dk = _dot(dsb, q, TN)
                dk_acc[kb] = dk if dk_acc[kb] is None else dk_acc[kb] + dk
                dv_acc[kb] = dv if dv_acc[kb] is None else dv_acc[kb] + dv
                ds_ref[rows, cols] += ds
            dq_ref[rows, :] = (dq * scale).astype(dq_ref.dtype)
        for kb in range(2 * per):
            dk_ref, dv_ref, kr = (dkp_ref, dvp_ref, kb) if kb < per else (dkc_ref, dvc_ref, kb - per)
            dk_ref[kr * sub:(kr + 1) * sub, :] = (dk_acc[kb] * scale).astype(dk_ref.dtype)
            dv_ref[kr * sub:(kr + 1) * sub, :] = dv_acc[kb].astype(dv_ref.dtype)

    def col(base):
        return (pl.BlockSpec((tq, HEAD_DIM_A), lambda h, i: (jnp.maximum(i - 1, 0), base + h)),
                pl.BlockSpec((tq, HEAD_DIM_A), lambda h, i: (i, base + h)))

    kp, kc = col(H)
    vp, vc = col(2 * H)
    blk = pl.BlockSpec((tq, HEAD_DIM_A), lambda h, i: (i, h))
    sd = jax.ShapeDtypeStruct((S, D), BF16)
    return _call(
        body, name="attn_bwd", grid=(H, nb),
        in_specs=[blk, kp, kc, vp, vc, blk, blk,
                  pl.BlockSpec((None, tq, 1), lambda h, i: (h, i, 0)),
                  pl.BlockSpec((None, tq, 2 * tq), lambda h, i: (h, 0, 0))],
        out_specs=[blk, blk, blk, blk, blk, pl.BlockSpec((None, tq, 2 * tq), lambda h, i: (h, 0, 0))],
        out_shape=[sd, sd, sd, sd, sd, jax.ShapeDtypeStruct((H, tq, 2 * tq), F32)],
        args=[proj, proj, proj, proj, proj, ya, dya, lse, bm], sem=("parallel", "arbitrary"), comm=comm)


def rel_bias_grad(ds_sum):
    H, tq, w = ds_sum.shape
    nbin = 2 * MAX_REL + 1
    nbin_pad = 3 * LANES
    d_lo, d_hi = -(CHUNK - 1), (N_PAST_CHUNKS + 1) * CHUNK - 1
    assert d_hi - d_lo + 1 <= w
    onehot = np.zeros((w, nbin_pad), np.float32)
    for d in range(d_lo, d_hi + 1):
        onehot[(tq - d) % w, int(np.clip(d, -MAX_REL, MAX_REL)) + MAX_REL] = 1.0
    nbits = int(np.log2(tq))
    assert (1 << nbits) == tq

    def body(ds_ref, m_ref, o_ref):
        x = ds_ref[...]
        row = lax.broadcasted_iota(jnp.int32, x.shape, 0)
        for b in range(nbits):
            rolled = pltpu.roll(x, w - (1 << b), 1)
            x = jnp.where(((row >> b) & 1) == 1, rolled, x)
        t = jnp.sum(x, axis=0, keepdims=True)
        o_ref[...] = lax.dot_general(t, m_ref[...], (NN, ((), ())), precision=lax.Precision.HIGHEST,
                                     preferred_element_type=F32)

    out = pl.pallas_call(
        body, grid=(H,),
        in_specs=[pl.BlockSpec((None, tq, w), lambda h: (h, 0, 0)),
                  pl.BlockSpec((w, nbin_pad), lambda h: (0, 0))],
        out_specs=pl.BlockSpec((None, 1, nbin_pad), lambda h: (h, 0, 0)),
        out_shape=jax.ShapeDtypeStruct((H, 1, nbin_pad), F32),
        name="rel_bias_grad", compiler_params=_cparams("parallel"))(ds_sum, jnp.asarray(onehot))
    return out[:, 0, :nbin]


def _conv_t(S):
    return _pick(S, (256,))


ROW_CHUNK = 16


def _row_loop(n_rows, step):
    def one(r, carry):
        step(pl.ds(pl.multiple_of(r * ROW_CHUNK, ROW_CHUNK), ROW_CHUNK))
        return carry

    lax.fori_loop(0, n_rows // ROW_CHUNK, one, 0)


def _fill_zbuf(zbuf, ap_ref, bp_ref, a_ref, b_ref, i):
    zp = ap_ref[...].astype(F32) * _sigmoid(bp_ref[...].astype(F32))
    zbuf[0:CONV_HALO, :] = jnp.where(i == 0, 0.0, zp)

    def step(rows):
        below = pl.ds(pl.multiple_of(rows.start + CONV_HALO, ROW_CHUNK), ROW_CHUNK)
        zbuf[below, :] = a_ref[rows, :].astype(F32) * _sigmoid(b_ref[rows, :].astype(F32))

    _row_loop(a_ref.shape[0], step)


def _shifted_windows(buf, shifted, lanes, T):
    rows = T + CONV_HALO - SUBLANES
    for b in range(1, SUBLANES):
        shifted[b - 1] = buf[pl.ds(b, rows), lanes]

    def window(off, r0=0, n=T):
        a, b = divmod(off, SUBLANES)
        if b == 0:
            return buf[pl.ds(r0 + off, n), lanes]
        return shifted[b - 1, pl.ds(r0 + a * SUBLANES, n), :]

    return window


def _shifted_scratch(T):
    return pltpu.VMEM((SUBLANES - 1, T + CONV_HALO - SUBLANES, LANES), F32)


def conv_gate_fwd(proj, ya, cw, cb, lng, lnb, D, comm=None):
    S = proj.shape[0]
    T = _conv_t(S)
    hb = T // CONV_HALO
    nlb = D // LANES

    def body(ap_ref, bp_ref, a_ref, b_ref, ga_ref, gb_ref, ya_ref, cw_ref, cb_ref, lng_ref, lnb_ref,
             y_ref, c_ref, zbuf, zsh):
        i = pl.program_id(0)
        _fill_zbuf(zbuf, ap_ref, bp_ref, a_ref, b_ref, i)

        def lane_block(lb, carry):
            lanes = pl.ds(pl.multiple_of(lb * LANES, LANES), LANES)
            z_at = _shifted_windows(zbuf, zsh, lanes, T)
            acc = jnp.zeros((T, LANES), F32)
            for k in range(CONV_WIDTH):
                acc = acc + cw_ref[k:k + 1, lanes] * z_at(CONV_HALO - CONV_WIDTH + 1 + k)
            c_ref[:, lanes] = acc + cb_ref[:, lanes]
            return carry

        lax.fori_loop(0, nlb, lane_block, 0)

        def norm_and_gate(rows):
            c = c_ref[rows, :]
            mu = jnp.mean(c, axis=-1, keepdims=True)
            xc = c - mu
            rstd = lax.rsqrt(jnp.mean(xc * xc, axis=-1, keepdims=True) + EPS)
            ln = xc * rstd * lng_ref[...] + lnb_ref[...]
            yb = ln * _sigmoid(ln)
            ga = ga_ref[rows, :].astype(F32)
            gb = gb_ref[rows, :].astype(F32)
            y_ref[rows, :D] = (ya_ref[rows, :].astype(F32) * (ga * _sigmoid(ga))).astype(y_ref.dtype)
            y_ref[rows, D:] = (yb * (gb * _sigmoid(gb))).astype(y_ref.dtype)

        _row_loop(T, norm_and_gate)

    def cur(cidx):
        return pl.BlockSpec((T, D), lambda i: (i, cidx))

    def prev(cidx):
        return pl.BlockSpec((CONV_HALO, D), lambda i: (jnp.maximum(i * hb - 1, 0), cidx))

    vec = pl.BlockSpec((1, D), lambda i: (0, 0))
    return _call(
        body, name="conv_gate_fwd", grid=(S // T,),
        in_specs=[prev(3), prev(4), cur(3), cur(4), cur(5), cur(6), pl.BlockSpec((T, D), lambda i: (i, 0)),
                  pl.BlockSpec((CONV_HALO, D), lambda i: (0, 0)), vec, vec, vec],
        out_specs=[pl.BlockSpec((T, 2 * D), lambda i: (i, 0)), pl.BlockSpec((T, D), lambda i: (i, 0))],
        out_shape=[jax.ShapeDtypeStruct((S, 2 * D), BF16), jax.ShapeDtypeStruct((S, D), F32)],
        scratch_shapes=[pltpu.VMEM((T + CONV_HALO, D), F32), _shifted_scratch(T)],
        args=[proj, proj, proj, proj, proj, proj, ya, cw, cb, lng, lnb], sem=("parallel",), comm=comm)


def conv_gate_bwd_a(dy0, proj, ya, cpre, lng, lnb, D):
    S = proj.shape[0]
    T = _conv_t(S)

    def body(dy_ref, ga_ref, gb_ref, ya_ref, c_ref, lng_ref, lnb_ref,
             dya_ref, dg_ref, dc_ref, dlng_ref, dlnb_ref):
        i = pl.program_id(0)

        c = c_ref[...]
        gv = lng_ref[...]
        mu = jnp.mean(c, axis=-1, keepdims=True)
        xc = c - mu
        rstd = lax.rsqrt(jnp.mean(xc * xc, axis=-1, keepdims=True) + EPS)
        xhat = xc * rstd
        ln = xhat * gv + lnb_ref[...]
        sl = _sigmoid(ln)
        yb = ln * sl
        ga = ga_ref[...].astype(F32)
        gb = gb_ref[...].astype(F32)
        sa = _sigmoid(ga)
        sb = _sigmoid(gb)
        dy_a = dy_ref[:, :D].astype(F32)
        dy_b = dy_ref[:, D:].astype(F32)
        dya_ref[...] = (dy_a * (ga * sa)).astype(dya_ref.dtype)
        dg_ref[:, :D] = (dy_a * ya_ref[...].astype(F32) * (sa * (1.0 + ga * (1.0 - sa)))).astype(dg_ref.dtype)
        dg_ref[:, D:] = (dy_b * yb * (sb * (1.0 + gb * (1.0 - sb)))).astype(dg_ref.dtype)
        dln = dy_b * (gb * sb) * (sl * (1.0 + ln * (1.0 - sl)))
        dxhat = dln * gv
        dc_ref[...] = rstd * (dxhat - jnp.mean(dxhat, axis=-1, keepdims=True)
                              - xhat * jnp.mean(dxhat * xhat, axis=-1, keepdims=True))
        dlng = jnp.sum(dln * xhat, axis=0, keepdims=True)
        dlnb = jnp.sum(dln, axis=0, keepdims=True)

        @pl.when(i == 0)
        def _():
            dlng_ref[...] = dlng
            dlnb_ref[...] = dlnb

        @pl.when(i > 0)
        def _():
            dlng_ref[...] += dlng
            dlnb_ref[...] += dlnb

    row = pl.BlockSpec((T, D), lambda i: (i, 0))
    vec = pl.BlockSpec((1, D), lambda i: (0, 0))
    return pl.pallas_call(
        body, grid=(S // T,),
        in_specs=[pl.BlockSpec((T, 2 * D), lambda i: (i, 0)),
                  pl.BlockSpec((T, D), lambda i: (i, 5)), pl.BlockSpec((T, D), lambda i: (i, 6)),
                  row, row, vec, vec],
        out_specs=[row, pl.BlockSpec((T, 2 * D), lambda i: (i, 0)), row, vec, vec],
        out_shape=[jax.ShapeDtypeStruct((S, D), BF16), jax.ShapeDtypeStruct((S, 2 * D), BF16),
                   jax.ShapeDtypeStruct((S, D), F32), jax.ShapeDtypeStruct((1, D), F32),
                   jax.ShapeDtypeStruct((1, D), F32)],
        name="conv_gate_bwd_a", compiler_params=_cparams("arbitrary"))(
            dy0, proj, proj, ya, cpre, lng, lnb)


def conv_gate_bwd_b(dc, proj, cw, D, comm=None):
    S = proj.shape[0]
    T = _conv_t(S)
    hb = T // CONV_HALO
    nt = S // T
    nlb = D // LANES
    half = T // 2

    def body(dc_ref, dn_ref, ap_ref, bp_ref, a_ref, b_ref, cw_ref, da_ref, db_ref, dcw_ref, dcb_ref,
             zbuf, dcbuf, zsh, dcsh, dcw8):
        i = pl.program_id(0)
        _fill_zbuf(zbuf, ap_ref, bp_ref, a_ref, b_ref, i)
        dcv = dc_ref[...]
        dcbuf[0:T, :] = dcv
        dcbuf[T:, :] = jnp.where(i == nt - 1, 0.0, dn_ref[...])

        @pl.when(i == 0)
        def _():
            dcw8[...] = jnp.zeros_like(dcw8)
            dcb_ref[...] = jnp.zeros_like(dcb_ref)

        dcb_ref[...] += jnp.sum(dcv, axis=0, keepdims=True)

        def lane_block(lb, carry):
            lanes = pl.ds(pl.multiple_of(lb * LANES, LANES), LANES)
            z_at = _shifted_windows(zbuf, zsh, lanes, T)
            dc_at = _shifted_windows(dcbuf, dcsh, lanes, T)
            for r0 in range(0, T, half):
                d0 = dcbuf[r0:r0 + half, lanes]
                dz = jnp.zeros((half, LANES), F32)
                for k in range(CONV_WIDTH):
                    dz = dz + cw_ref[k:k + 1, lanes] * dc_at(CONV_WIDTH - 1 - k, r0, half)
                    prod = d0 * z_at(CONV_HALO - CONV_WIDTH + 1 + k, r0, half)
                    dcw8[pl.ds(k * SUBLANES, SUBLANES), lanes] += jnp.sum(
                        prod.reshape(half // SUBLANES, SUBLANES, LANES), axis=0)
                av = a_ref[r0:r0 + half, lanes].astype(F32)
                sg = _sigmoid(b_ref[r0:r0 + half, lanes].astype(F32))
                da_ref[r0:r0 + half, lanes] = (dz * sg).astype(da_ref.dtype)
                db_ref[r0:r0 + half, lanes] = (dz * av * sg * (1.0 - sg)).astype(db_ref.dtype)
            return carry

        lax.fori_loop(0, nlb, lane_block, 0)

        @pl.when(i == nt - 1)
        def _():
            dcw_ref[...] = jnp.sum(dcw8[...].reshape(CONV_HALO, SUBLANES, D), axis=1)

    def cur(cidx):
        return pl.BlockSpec((T, D), lambda i: (i, cidx))

    def prev(cidx):
        return pl.BlockSpec((CONV_HALO, D), lambda i: (jnp.maximum(i * hb - 1, 0), cidx))

    row = pl.BlockSpec((T, D), lambda i: (i, 0))
    nxt = pl.BlockSpec((CONV_HALO, D), lambda i: (jnp.minimum((i + 1) * hb, nt * hb - 1), 0))
    return _call(
        body, name="conv_gate_bwd_b", grid=(nt,),
        in_specs=[row, nxt, prev(3), prev(4), cur(3), cur(4), pl.BlockSpec((CONV_HALO, D), lambda i: (0, 0))],
        out_specs=[row, row, pl.BlockSpec((CONV_HALO, D), lambda i: (0, 0)),
                   pl.BlockSpec((1, D), lambda i: (0, 0))],
        out_shape=[jax.ShapeDtypeStruct((S, D), BF16), jax.ShapeDtypeStruct((S, D), BF16),
                   jax.ShapeDtypeStruct((CONV_HALO, D), F32), jax.ShapeDtypeStruct((1, D), F32)],
        scratch_shapes=[pltpu.VMEM((T + CONV_HALO, D), F32), pltpu.VMEM((T + CONV_HALO, D), F32),
                        _shifted_scratch(T), _shifted_scratch(T), pltpu.VMEM((CONV_HALO * SUBLANES, D), F32)],
        args=[dc, dc, proj, proj, proj, proj, cw], sem=("arbitrary",), comm=comm)


def assemble_dproj0(dq, dkc, dkp, dvc, dvp, da, db, dgate, D):
    S = dq.shape[0]
    tq = _attn_tq(S)
    T = _pick(S, (256,))
    shift = tq // T
    nt = S // T

    def body(dq_ref, dkc_ref, dkp_ref, dvc_ref, dvp_ref, da_ref, db_ref, dg_ref, o_ref):
        i = pl.program_id(0)
        last = i + shift >= nt
        o_ref[:, 0:D] = dq_ref[...]
        dk = dkc_ref[...].astype(F32) + jnp.where(last, 0.0, dkp_ref[...].astype(F32))
        dv = dvc_ref[...].astype(F32) + jnp.where(last, 0.0, dvp_ref[...].astype(F32))
        o_ref[:, D:2 * D] = dk.astype(o_ref.dtype)
        o_ref[:, 2 * D:3 * D] = dv.astype(o_ref.dtype)
        o_ref[:, 3 * D:4 * D] = da_ref[...]
        o_ref[:, 4 * D:5 * D] = db_ref[...]
        o_ref[:, 5 * D:] = dg_ref[...]

    row = pl.BlockSpec((T, D), lambda i: (i, 0))
    nxt = pl.BlockSpec((T, D), lambda i: (jnp.minimum(i + shift, nt - 1), 0))
    return pl.pallas_call(
        body, grid=(nt,),
        in_specs=[row, row, nxt, row, nxt, row, row, pl.BlockSpec((T, 2 * D), lambda i: (i, 0))],
        out_specs=pl.BlockSpec((T, 7 * D), lambda i: (i, 0)),
        out_shape=jax.ShapeDtypeStruct((S, 7 * D), BF16),
        name="assemble_dproj0", compiler_params=_cparams("parallel"))(dq, dkc, dkp, dvc, dvp, da, db, dgate)


def _sgu_t(S):
    return _pick(S, (256, 128))


def _ws_masked(ws_ref, g):
    row = lax.broadcasted_iota(jnp.int32, (GMLP_CHUNK, GMLP_CHUNK), 0) // CHUNK
    col = lax.broadcasted_iota(jnp.int32, (GMLP_CHUNK, GMLP_CHUNK), 1) // CHUNK
    return jnp.where(row >= col, ws_ref[g], 0.0), row >= col


def sgu_fwd(proj, lng, lnb, ws, bst, MIX):
    S = proj.shape[0]
    T = _sgu_t(S)
    gw = MIX // N_GROUPS_C

    def body(u_ref, v_ref, g_ref, lng_ref, lnb_ref, ws_ref, bst_ref, y_ref):
        v = v_ref[...].astype(F32)
        mu = jnp.mean(v, axis=-1, keepdims=True)
        xc = v - mu
        rstd = lax.rsqrt(jnp.mean(xc * xc, axis=-1, keepdims=True) + EPS)
        for g in range(N_GROUPS_C):
            cols = slice(g * gw, (g + 1) * gw)
            wsm = _ws_masked(ws_ref, g)[0].astype(BF16)
            vn = (xc[:, cols] * rstd * lng_ref[:, cols] + lnb_ref[:, cols]).astype(BF16)
            for blk in range(T // GMLP_CHUNK):
                rows = slice(blk * GMLP_CHUNK, (blk + 1) * GMLP_CHUNK)
                sg = _dot(wsm, vn[rows], NN) + bst_ref[:, g:g + 1]
                gate = g_ref[rows, cols].astype(F32)
                y = u_ref[rows, cols].astype(F32) * sg * (gate * _sigmoid(gate))
                y_ref[rows, cols] = y.astype(y_ref.dtype)

    def part(cidx):
        return pl.BlockSpec((T, MIX), lambda i: (i, cidx))

    vec = pl.BlockSpec((1, MIX), lambda i: (0, 0))
    return pl.pallas_call(
        body, grid=(S // T,),
        in_specs=[part(0), part(1), part(2), vec, vec,
                  pl.BlockSpec((N_GROUPS_C, GMLP_CHUNK, GMLP_CHUNK), lambda i: (0, 0, 0)),
                  pl.BlockSpec((GMLP_CHUNK, N_GROUPS_C), lambda i: (0, 0))],
        out_specs=pl.BlockSpec((T, MIX), lambda i: (i, 0)),
        out_shape=jax.ShapeDtypeStruct((S, MIX), BF16),
        name="sgu_fwd", compiler_params=_cparams("parallel"))(proj, proj, proj, lng, lnb, ws, bst)


def sgu_bwd(dy1, proj, lng, lnb, ws, bst, MIX):
    S = proj.shape[0]
    T = _sgu_t(S)
    gw = MIX // N_GROUPS_C

    def body(dy_ref, u_ref, v_ref, g_ref, lng_ref, lnb_ref, ws_ref, bst_ref,
             dp_ref, dws_ref, dbst_ref, dlng_ref, dlnb_ref, dvn_buf):
        i = pl.program_id(0)

        @pl.when(i == 0)
        def _():
            dws_ref[...] = jnp.zeros_like(dws_ref)
            dbst_ref[...] = jnp.zeros_like(dbst_ref)
            dlng_ref[...] = jnp.zeros_like(dlng_ref)
            dlnb_ref[...] = jnp.zeros_like(dlnb_ref)

        v = v_ref[...].astype(F32)
        mu = jnp.mean(v, axis=-1, keepdims=True)
        xc = v - mu
        rstd = lax.rsqrt(jnp.mean(xc * xc, axis=-1, keepdims=True) + EPS)
        for g in range(N_GROUPS_C):
            cols = slice(g * gw, (g + 1) * gw)
            wsf, keep = _ws_masked(ws_ref, g)
            wsm = wsf.astype(BF16)
            vn = (xc[:, cols] * rstd * lng_ref[:, cols] + lnb_ref[:, cols]).astype(BF16)
            for blk in range(T // GMLP_CHUNK):
                rows = slice(blk * GMLP_CHUNK, (blk + 1) * GMLP_CHUNK)
                vnb = vn[rows]
                sg = _dot(wsm, vnb, NN) + bst_ref[:, g:g + 1]
                gate = g_ref[rows, cols].astype(F32)
                sig = _sigmoid(gate)
                sil = gate * sig
                u = u_ref[rows, cols].astype(F32)
                dy = dy_ref[rows, cols].astype(F32)
                dp_ref[rows, g * gw:(g + 1) * gw] = (dy * sg * sil).astype(dp_ref.dtype)
                dp_ref[rows, 2 * MIX + g * gw:2 * MIX + (g + 1) * gw] = (
                    dy * u * sg * (sig * (1.0 + gate * (1.0 - sig)))).astype(dp_ref.dtype)
                dsg = dy * u * sil
                dsgb = dsg.astype(BF16)
                dvn_buf[rows, cols] = _dot(wsm, dsgb, TN)
                dws_ref[g] += jnp.where(keep, _dot(dsgb, vnb, NT), 0.0)
                dbst_ref[:, g:g + 1] += jnp.sum(dsg, axis=-1, keepdims=True)
        dvn = dvn_buf[...]
        xhat = xc * rstd
        dxhat = dvn * lng_ref[...]
        dv = rstd * (dxhat - jnp.mean(dxhat, axis=-1, keepdims=True)
                     - xhat * jnp.mean(dxhat * xhat, axis=-1, keepdims=True))
        dp_ref[:, MIX:2 * MIX] = dv.astype(dp_ref.dtype)
        dlng_ref[...] += jnp.sum(dvn * xhat, axis=0, keepdims=True)
        dlnb_ref[...] += jnp.sum(dvn, axis=0, keepdims=True)

    def part(cidx):
        return pl.BlockSpec((T, MIX), lambda i: (i, cidx))

    vec = pl.BlockSpec((1, MIX), lambda i: (0, 0))
    wspec = pl.BlockSpec((N_GROUPS_C, GMLP_CHUNK, GMLP_CHUNK), lambda i: (0, 0, 0))
    bspec = pl.BlockSpec((GMLP_CHUNK, N_GROUPS_C), lambda i: (0, 0))
    return pl.pallas_call(
        body, grid=(S // T,),
        in_specs=[pl.BlockSpec((T, MIX), lambda i: (i, 0)), part(0), part(1), part(2), vec, vec, wspec, bspec],
        out_specs=[pl.BlockSpec((T, 3 * MIX), lambda i: (i, 0)), wspec, bspec, vec, vec],
        out_shape=[jax.ShapeDtypeStruct((S, 3 * MIX), BF16),
                   jax.ShapeDtypeStruct((N_GROUPS_C, GMLP_CHUNK, GMLP_CHUNK), F32),
                   jax.ShapeDtypeStruct((GMLP_CHUNK, N_GROUPS_C), F32),
                   jax.ShapeDtypeStruct((1, MIX), F32), jax.ShapeDtypeStruct((1, MIX), F32)],
        scratch_shapes=[pltpu.VMEM((T, MIX), F32)],
        name="sgu_bwd", compiler_params=_cparams("arbitrary"))(dy1, proj, proj, proj, lng, lnb, ws, bst)


def xattn_fwd(name, q, k, v):
    S, D = q.shape
    nm = k.shape[0]
    dh = D // N_HEADS_X
    tq = _pick(S, (512, 256))
    scale = dh ** -0.5

    def body(q_ref, k_ref, v_ref, o_ref, lse_ref):
        s = _dot(q_ref[...], k_ref[...], NT) * scale
        m = jnp.max(s, axis=-1, keepdims=True)
        p = jnp.exp(s - m)
        l = jnp.sum(p, axis=-1, keepdims=True)
        o_ref[...] = (_dot(p.astype(BF16), v_ref[...], NN) / l).astype(o_ref.dtype)
        lse_ref[...] = m + jnp.log(l)

    return pl.pallas_call(
        body, grid=(N_HEADS_X, S // tq),
        in_specs=[pl.BlockSpec((tq, dh), lambda h, i: (i, h)),
                  pl.BlockSpec((nm, dh), lambda h, i: (0, h)), pl.BlockSpec((nm, dh), lambda h, i: (0, h))],
        out_specs=[pl.BlockSpec((tq, dh), lambda h, i: (i, h)),
                   pl.BlockSpec((None, tq, 1), lambda h, i: (h, i, 0))],
        out_shape=[jax.ShapeDtypeStruct((S, D), BF16), jax.ShapeDtypeStruct((N_HEADS_X, S, 1), F32)],
        name=name, compiler_params=_cparams("parallel", "parallel"))(q, k, v)


def xattn_bwd(name, q, k, v, o, do, lse):
    S, D = q.shape
    nm = k.shape[0]
    dh = D // N_HEADS_X
    tq = _pick(S, (512, 256))
    scale = dh ** -0.5

    def body(q_ref, k_ref, v_ref, o_ref, do_ref, lse_ref, dq_ref, dk_ref, dv_ref):
        i = pl.program_id(1)
        q_v = q_ref[...]
        k_v = k_ref[...]
        do_v = do_ref[...]
        p = jnp.exp(_dot(q_v, k_v, NT) * scale - lse_ref[...])
        delta = jnp.sum(do_v.astype(F32) * o_ref[...].astype(F32), axis=-1, keepdims=True)
        dv = _dot(p.astype(BF16), do_v, TN)
        ds = (p * (_dot(do_v, v_ref[...], NT) - delta)).astype(BF16)
        dq_ref[...] = (_dot(ds, k_v, NN) * scale).astype(dq_ref.dtype)
        dk = _dot(ds, q_v, TN) * scale

        @pl.when(i == 0)
        def _():
            dk_ref[...] = dk
            dv_ref[...] = dv

        @pl.when(i > 0)
        def _():
            dk_ref[...] += dk
            dv_ref[...] += dv

    qs = pl.BlockSpec((tq, dh), lambda h, i: (i, h))
    ks = pl.BlockSpec((nm, dh), lambda h, i: (0, h))
    return pl.pallas_call(
        body, grid=(N_HEADS_X, S // tq),
        in_specs=[qs, ks, ks, qs, qs, pl.BlockSpec((None, tq, 1), lambda h, i: (h, i, 0))],
        out_specs=[qs, ks, ks],
        out_shape=[jax.ShapeDtypeStruct((S, D), BF16), jax.ShapeDtypeStruct((nm, D), F32),
                   jax.ShapeDtypeStruct((nm, D), F32)],
        name=name, compiler_params=_cparams("parallel", "arbitrary"))(q, k, v, o, do, lse)


def adamw(name, w, g, m, v):
    R, C = w.shape
    tr = _pick(R, tuple(t for t in (512, 256, 128, 64, 32, 16, 8) if t * C * 4 <= (1 << 20)) or (8,))
    c1 = 1.0 - ADAM_B1 ** ADAM_STEP
    c2 = 1.0 - ADAM_B2 ** ADAM_STEP

    def body(w_ref, g_ref, m_ref, v_ref, d_ref, nm_ref, nv_ref):
        gv = g_ref[...]
        nm = ADAM_B1 * m_ref[...] + (1.0 - ADAM_B1) * gv
        nv = ADAM_B2 * v_ref[...] + (1.0 - ADAM_B2) * (gv * gv)
        d_ref[...] = -ADAM_LR * ((nm / c1) / (jnp.sqrt(nv / c2) + ADAM_EPS) + ADAM_WD * w_ref[...])
        nm_ref[...] = nm
        nv_ref[...] = nv

    blk = pl.BlockSpec((tr, C), lambda i: (i, 0))
    sd = jax.ShapeDtypeStruct((R, C), F32)
    return pl.pallas_call(body, grid=(R // tr,), in_specs=[blk] * 4, out_specs=[blk] * 3,
                          out_shape=[sd, sd, sd], name=name, compiler_params=_cparams("parallel"))(w, g, m, v)


def add_halves(name, g4, recv, cidx):
    _, R, C = g4.shape
    rh = R // 2
    tr = _pick(rh, (256, 128, 64, 32, 16))
    nrb = rh // tr

    def body(c_ref, a_ref, b_ref, o_ref):
        o_ref[...] = (a_ref[...].astype(F32) + b_ref[...].astype(F32)).astype(o_ref.dtype)

    grid_spec = pltpu.PrefetchScalarGridSpec(
        num_scalar_prefetch=1, grid=(4, nrb),
        in_specs=[pl.BlockSpec((None, tr, C), lambda j, r, c_ref: (j, c_ref[0] * nrb + r, 0)),
                  pl.BlockSpec((None, tr, C), lambda j, r, c_ref: (j, r, 0))],
        out_specs=pl.BlockSpec((None, tr, C), lambda j, r, c_ref: (j, r, 0)))
    return pl.pallas_call(body, grid_spec=grid_spec, out_shape=jax.ShapeDtypeStruct((4, rh, C), BF16),
                          name=name, compiler_params=_cparams("parallel", "parallel"))(cidx, g4, recv)


def sum_chips(name, own, recv, place):
    _, rh, C = own.shape
    tr = _pick(rh, (256, 128, 64, 32, 16))
    nrb = rh // tr

    def body(s_ref, own_ref, recv_ref, o_ref):
        acc = own_ref[...].astype(F32)
        for k in range(N_CHIPS - 1):
            acc = acc + recv_ref[k].astype(F32)
        o_ref[...] = acc

    grid_spec = pltpu.PrefetchScalarGridSpec(
        num_scalar_prefetch=1, grid=(nrb,),
        in_specs=[pl.BlockSpec((None, tr, C), lambda r, s: (s[0], r, 0)),
                  pl.BlockSpec((N_CHIPS - 1, tr, C), lambda r, s: (0, r, 0))],
        out_specs=pl.BlockSpec((tr, C), lambda r, s: (s[1] * nrb + r, 0)))
    return pl.pallas_call(body, grid_spec=grid_spec, out_shape=jax.ShapeDtypeStruct((2 * rh, C), F32),
                          name=name, compiler_params=_cparams("parallel"))(place, own, recv)


def cast_into_slot(name, w, place):
    R, C = w.shape
    tr = _pick(R, (256, 128, 64, 32, 16))

    def body(s_ref, w_ref, o_ref):
        o_ref[...] = w_ref[...].astype(o_ref.dtype)

    grid_spec = pltpu.PrefetchScalarGridSpec(
        num_scalar_prefetch=1, grid=(R // tr,),
        in_specs=[pl.BlockSpec((tr, C), lambda r, s: (r, 0))],
        out_specs=pl.BlockSpec((None, tr, C), lambda r, s: (s[0], r, 0)))
    return pl.pallas_call(body, grid_spec=grid_spec, out_shape=jax.ShapeDtypeStruct((N_CHIPS, R, C), BF16),
                          name=name, compiler_params=_cparams("parallel"))(place, w)


def _place():
    return lax.axis_index("x"), lax.axis_index("y"), lax.axis_index("c")


_CHIP_FLIPS = ((1, 0), (0, 1), (1, 1))


def _flip(v, bit):
    return 1 - v if bit else v


HBM_SPEC = pl.BlockSpec(memory_space=pl.ANY)


def exchange_small(name, buf, reduce):
    R = buf.shape[0]

    def body(x_ref, *refs):
        if reduce:
            sum_ref, all_ref, send_sems, recv_sems, local_sem = refs
        else:
            all_ref, send_sems, recv_sems, local_sem = refs
        x, y, c = _place()
        me = 4 * x + 2 * y + c
        mine = pltpu.make_async_copy(x_ref, all_ref.at[me], local_sem)
        mine.start()
        sends = []
        for k in range(1, N_DEV):
            peer = (_flip(x, k & 4), _flip(y, k & 2), _flip(c, k & 1))
            cp = pltpu.make_async_remote_copy(src_ref=x_ref, dst_ref=all_ref.at[me], send_sem=send_sems.at[k - 1],
                                              recv_sem=recv_sems.at[k - 1], device_id=peer, device_id_type=MESH)
            cp.start()
            sends.append(cp)
        for k in range(1, N_DEV):
            peer = (_flip(x, k & 4), _flip(y, k & 2), _flip(c, k & 1))
            src = 4 * peer[0] + 2 * peer[1] + peer[2]
            pltpu.make_async_remote_copy(src_ref=x_ref, dst_ref=all_ref.at[src], send_sem=send_sems.at[k - 1],
                                         recv_sem=recv_sems.at[k - 1], device_id=peer,
                                         device_id_type=MESH).wait_recv()
        for cp in sends:
            cp.wait_send()
        mine.wait()
        if reduce:
            acc = all_ref[0]
            for d in range(1, N_DEV):
                acc = acc + all_ref[d]
            sum_ref[...] = acc

    vm = pl.BlockSpec(memory_space=pltpu.VMEM)
    sems = [pltpu.SemaphoreType.DMA((N_DEV - 1,)), pltpu.SemaphoreType.DMA((N_DEV - 1,)), pltpu.SemaphoreType.DMA]
    if reduce:
        return pl.pallas_call(
            body, in_specs=[vm], out_specs=vm, out_shape=jax.ShapeDtypeStruct((R, LANES), F32),
            scratch_shapes=[pltpu.VMEM((N_DEV, R, LANES), F32)] + sems, name=name,
            compiler_params=pltpu.CompilerParams(vmem_limit_bytes=V7X_VMEM_LIMIT))(buf)
    return pl.pallas_call(
        body, in_specs=[vm], out_specs=vm, out_shape=jax.ShapeDtypeStruct((N_DEV, R, LANES), F32),
        scratch_shapes=sems, name=name,
        compiler_params=pltpu.CompilerParams(vmem_limit_bytes=V7X_VMEM_LIMIT))(buf)


def exchange_job(buf):
    R = buf.shape[0]

    def copies(x_ref, all_ref, send_sems, recv_sems):
        x, y, c = _place()
        me = 4 * x + 2 * y + c
        sends, arrivals = [], []
        for k in range(1, N_DEV):
            peer = (_flip(x, k & 4), _flip(y, k & 2), _flip(c, k & 1))
            src = 4 * peer[0] + 2 * peer[1] + peer[2]
            sends.append(pltpu.make_async_remote_copy(
                src_ref=x_ref, dst_ref=all_ref.at[me], send_sem=send_sems.at[k - 1], recv_sem=recv_sems.at[k - 1],
                device_id=peer, device_id_type=MESH))
            arrivals.append(pltpu.make_async_remote_copy(
                src_ref=x_ref, dst_ref=all_ref.at[src], send_sem=send_sems.at[k - 1], recv_sem=recv_sems.at[k - 1],
                device_id=peer, device_id_type=MESH))
        return sends, arrivals

    def start(ins, outs, sems):
        for cp in copies(ins[0], outs[0], *sems)[0]:
            cp.start()

    def finish(ins, outs, sems):
        sends, arrivals = copies(ins[0], outs[0], *sems)
        for cp in arrivals:
            cp.wait_recv()
        for cp in sends:
            cp.wait_send()

    return _Comm([buf], [jax.ShapeDtypeStruct((N_DEV, R, LANES), F32)], {},
                 [pltpu.SemaphoreType.DMA((N_DEV - 1,)), pltpu.SemaphoreType.DMA((N_DEV - 1,))], start, finish)


def sum_devices(name, slots):
    _, R, _ = slots.shape
    tr = _pick(R, (512, 256, 128, 64, 32, 16, 8))

    def body(s_ref, o_ref):
        acc = s_ref[0]
        for d in range(1, N_DEV):
            acc = acc + s_ref[d]
        o_ref[...] = acc

    return pl.pallas_call(body, grid=(R // tr,),
                          in_specs=[pl.BlockSpec((N_DEV, tr, LANES), lambda r: (0, r, 0))],
                          out_specs=pl.BlockSpec((tr, LANES), lambda r: (r, 0)),
                          out_shape=jax.ShapeDtypeStruct((R, LANES), F32), name=name,
                          compiler_params=_cparams("parallel"))(slots)


def gather_job(slots, relay_frac=0.75, flips=None):
    n = len(slots)
    flips = flips or [tuple(range(len(_CHIP_FLIPS)))] * n

    def copies(o_refs, send_sems, recv_sems):
        x, y, c = _place()
        me = 2 * x + y
        sib = (x, y, 1 - c)
        chips = [(_flip(x, fx), _flip(y, fy)) for fx, fy in _CHIP_FLIPS]
        ici, fwd, from_sib = [], [], []
        for t in range(n):
            rh = o_refs[t].shape[1] // 2
            mine, theirs = pl.ds(c * rh, rh), pl.ds((1 - c) * rh, rh)
            for k, (px, py) in enumerate(chips):
                if k not in flips[t]:
                    continue
                own = o_refs[t].at[me, mine]
                ici.append(pltpu.make_async_remote_copy(
                    src_ref=own, dst_ref=own, send_sem=send_sems.at[t, k], recv_sem=recv_sems.at[t, k],
                    device_id=(px, py, c), device_id_type=MESH))
                landed = o_refs[t].at[2 * px + py, mine]
                arrival = pltpu.make_async_remote_copy(
                    src_ref=landed, dst_ref=landed, send_sem=send_sems.at[t, k], recv_sem=recv_sems.at[t, k],
                    device_id=(px, py, c), device_id_type=MESH)
                fwd.append((arrival, pltpu.make_async_remote_copy(
                    src_ref=landed, dst_ref=landed, send_sem=send_sems.at[t, 3 + k],
                    recv_sem=recv_sems.at[t, 3 + k], device_id=sib, device_id_type=MESH)))
                passed = o_refs[t].at[2 * px + py, theirs]
                from_sib.append(pltpu.make_async_remote_copy(
                    src_ref=passed, dst_ref=passed, send_sem=send_sems.at[t, 3 + k],
                    recv_sem=recv_sems.at[t, 3 + k], device_id=sib, device_id_type=MESH))
        return ici, fwd, from_sib

    def start(ins, o_refs, sems):
        for cp in copies(o_refs, *sems)[0]:
            cp.start()

    def relay(ins, o_refs, sems):
        for arrival, forward in copies(o_refs, *sems)[1]:
            arrival.wait_recv()
            forward.start()

    def finish(ins, o_refs, sems):
        ici, fwd, from_sib = copies(o_refs, *sems)
        for cp in from_sib:
            cp.wait_recv()
        for cp in ici:
            cp.wait_send()
        for _, forward in fwd:
            forward.wait_send()

    return _Comm(slots, [jax.ShapeDtypeStruct(s.shape, s.dtype) for s in slots], {t: t for t in range(n)},
                 [pltpu.SemaphoreType.DMA((n, 6)), pltpu.SemaphoreType.DMA((n, 6))], start, finish, relay,
                 relay_frac)


def sibling_halves_job(grads):
    n = len(grads)

    def copies(g_refs, o_refs, send_sems, recv_sems):
        x, y, c = _place()
        out = []
        for t in range(n):
            rh = g_refs[t].shape[1] // 2
            out.append(pltpu.make_async_remote_copy(
                src_ref=g_refs[t].at[:, pl.ds((1 - c) * rh, rh), :], dst_ref=o_refs[t],
                send_sem=send_sems.at[t], recv_sem=recv_sems.at[t], device_id=(x, y, 1 - c),
                device_id_type=MESH))
        return out

    def start(g_refs, o_refs, sems):
        for cp in copies(g_refs, o_refs, *sems):
            cp.start()

    def finish(g_refs, o_refs, sems):
        cps = copies(g_refs, o_refs, *sems)
        for cp in cps:
            cp.wait_recv()
        for cp in cps:
            cp.wait_send()

    return _Comm(grads, [jax.ShapeDtypeStruct((4, g.shape[1] // 2, g.shape[2]), g.dtype) for g in grads], {},
                 [pltpu.SemaphoreType.DMA((n,)), pltpu.SemaphoreType.DMA((n,))], start, finish)


def scatter_job(parts):
    n = len(parts)

    def copies(p_refs, o_refs, send_sems, recv_sems):
        x, y, c = _place()
        out = []
        for t in range(n):
            for k, (fx, fy) in enumerate(_CHIP_FLIPS):
                px, py = _flip(x, fx), _flip(y, fy)
                out.append(pltpu.make_async_remote_copy(
                    src_ref=p_refs[t].at[2 * px + py], dst_ref=o_refs[t].at[k],
                    send_sem=send_sems.at[t, k], recv_sem=recv_sems.at[t, k],
                    device_id=(px, py, c), device_id_type=MESH))
        return out

    def start(p_refs, o_refs, sems):
        for cp in copies(p_refs, o_refs, *sems):
            cp.start()

    def finish(p_refs, o_refs, sems):
        cps = copies(p_refs, o_refs, *sems)
        for cp in cps:
            cp.wait_recv()
        for cp in cps:
            cp.wait_send()

    return _Comm(parts, [jax.ShapeDtypeStruct((N_CHIPS - 1,) + p.shape[1:], p.dtype) for p in parts], {},
                 [pltpu.SemaphoreType.DMA((n, 3)), pltpu.SemaphoreType.DMA((n, 3))], start, finish)


def share_halves_job(halves):
    n = len(halves)

    def copies(o_refs, send_sems, recv_sems):
        x, y, c = _place()
        sends, arrivals = [], []
        for t in range(n):
            rh = o_refs[t].shape[0] // 2
            mine = o_refs[t].at[pl.ds(c * rh, rh)]
            theirs = o_refs[t].at[pl.ds((1 - c) * rh, rh)]
            sends.append(pltpu.make_async_remote_copy(
                src_ref=mine, dst_ref=mine, send_sem=send_sems.at[t], recv_sem=recv_sems.at[t],
                device_id=(x, y, 1 - c), device_id_type=MESH))
            arrivals.append(pltpu.make_async_remote_copy(
                src_ref=theirs, dst_ref=theirs, send_sem=send_sems.at[t], recv_sem=recv_sems.at[t],
                device_id=(x, y, 1 - c), device_id_type=MESH))
        return sends, arrivals

    def start(ins, o_refs, sems):
        for cp in copies(o_refs, *sems)[0]:
            cp.start()

    def finish(ins, o_refs, sems):
        sends, arrivals = copies(o_refs, *sems)
        for cp in arrivals:
            cp.wait_recv()
        for cp in sends:
            cp.wait_send()

    return _Comm(halves, [jax.ShapeDtypeStruct(h.shape, h.dtype) for h in halves], {t: t for t in range(n)},
                 [pltpu.SemaphoreType.DMA((n,)), pltpu.SemaphoreType.DMA((n,))], start, finish)


def _pack(arrs, row_multiple=SUBLANES):
    flat, total = [], 0
    for a in arrs:
        v = a.reshape(-1).astype(F32)
        pad = (-v.shape[0]) % (SUBLANES * LANES)
        flat.append(jnp.pad(v, (0, pad)))
        total += v.shape[0] + pad
    tail = (-total) % (row_multiple * LANES)
    if tail:
        flat.append(jnp.zeros((tail,), F32))
    return jnp.concatenate(flat).reshape(-1, LANES)


def _unpack(buf, shapes):
    out, off = [], 0
    flat = buf.reshape(-1)
    for s in shapes:
        n = int(np.prod(s))
        out.append(flat[off:off + n].reshape(s))
        off += n + ((-n) % (8 * LANES))
    return out


def _xattn_layer_fwd(tag, h, mem, gx, gmem, w):
    hx = rms_fwd(f"rms_x{tag}", h, gx)
    memn = rms_fwd(f"rms_mem{tag}", mem, gmem)
    q = mm_nn(f"xq{tag}", hx, w["q"], BF16)
    k = mm_nn(f"xk{tag}", memn, w["k"], BF16)
    v = mm_nn(f"xv{tag}", memn, w["v"], BF16)
    o, lse = xattn_fwd(f"xattn_fwd{tag}", q, k, v)
    h_out = mm_nn(f"xo{tag}", o, w["o"], F32, res=h)
    return h_out, dict(hx=hx, memn=memn, q=q, k=k, v=v, o=o, lse=lse)


def _xattn_layer_bwd(tag, dh_out, dh_out_b, h_in, mem, gx, gmem, w, sv):
    do = mm_nt(f"d_xo{tag}", dh_out_b, w["o"], BF16)
    dwo = mm_tn(f"dw_xo{tag}", sv["o"], dh_out_b)
    dq, dk, dv = xattn_bwd(f"xattn_bwd{tag}", sv["q"], sv["k"], sv["v"], sv["o"], do, sv["lse"])
    dwq = mm_tn(f"dw_xq{tag}", sv["hx"], dq)
    dhx = mm_nt(f"d_xq{tag}", dq, w["q"], F32)
    dwk = mm_tn(f"dw_xk{tag}", sv["memn"], dk)
    dwv = mm_tn(f"dw_xv{tag}", sv["memn"], dv)
    dmk = mm_nt(f"d_xk{tag}", dk, w["k"], F32)
    dmv = mm_nt(f"d_xv{tag}", dv, w["v"], F32)
    dh_in, dh_in_b, dgx = rms_bwd(f"rms_x_bwd{tag}", h_in, gx, [dhx], dh_out)
    _, _, dgmem = rms_bwd(f"rms_mem_bwd{tag}", mem, gmem, [dmk, dmv], None)
    return dh_in, dh_in_b, dgx, dgmem, dict(q=dwq, k=dwk, v=dwv, o=dwo)


def kernel(x, mem, norm_mix_g, norm_x_g, norm_mem_g, final_norm_g, w_in_ab, rel_bias, conv_w, conv_b, conv_ln_g, conv_ln_b, w_out_ab, w_in_c, sgu_ln_g, sgu_ln_b, w_s, b_s, w_out_c, w_xq, w_xk, w_xv, w_xo, loss_target, m_norm_mix_g, m_norm_x_g, m_norm_mem_g, m_final_norm_g, m_w_in_ab, m_rel_bias, m_conv_w, m_conv_b, m_conv_ln_g, m_conv_ln_b, m_w_out_ab, m_w_in_c, m_sgu_ln_g, m_sgu_ln_b, m_w_s, m_b_s, m_w_out_c, m_w_xq, m_w_xk, m_w_xv, m_w_xo, v_norm_mix_g, v_norm_x_g, v_norm_mem_g, v_final_norm_g, v_w_in_ab, v_rel_bias, v_conv_w, v_conv_b, v_conv_ln_g, v_conv_ln_b, v_w_out_ab, v_w_in_c, v_sgu_ln_g, v_sgu_ln_b, v_w_s, v_b_s, v_w_out_c, v_w_xq, v_w_xk, v_w_xv, v_w_xo):
    S, D = x.shape[1], x.shape[2]
    MIX = 2 * D
    xs, mems, tgt = x[0], mem[0], loss_target[0]
    cx, cy, cc = _place()
    chip = 2 * cx + cy
    cidx = jnp.reshape(cc, (1,)).astype(jnp.int32)
    place = jnp.stack([chip, cc]).astype(jnp.int32)

    ro, rq = MIX // 4, D // 4
    row_sharded = [("out_ab", w_out_ab[0]), ("out_c", w_out_c[0])]
    for layer in range(2):
        for nm_, w in (("q", w_xq), ("k", w_xk), ("v", w_xv), ("o", w_xo)):
            row_sharded.append((f"x{nm_}{layer}", w[layer]))
    slots = {"in_ab": cast_into_slot("cast_in_ab", w_in_ab[0], place),
             "in_c": cast_into_slot("cast_in_c", w_in_c[0], place)}
    slots.update({nm_: cast_into_slot("cast_" + nm_, w, place) for nm_, w in row_sharded})

    small_sh = [conv_w[0], sgu_ln_g[0], sgu_ln_b[0]]
    gathered = exchange_small("gather_small", _pack(small_sh), reduce=False)
    per_chip = [_unpack(gathered[2 * j], [a.shape for a in small_sh]) for j in range(N_CHIPS)]
    conv_w_full = jnp.concatenate([p[0] for p in per_chip], axis=1)
    sgu_g_full = jnp.concatenate([p[1] for p in per_chip], axis=0).reshape(1, MIX)
    sgu_b_full = jnp.concatenate([p[2] for p in per_chip], axis=0).reshape(1, MIX)
    cw_pad = jnp.pad(conv_w_full, ((0, CONV_HALO - CONV_WIDTH), (0, 0)))
    cb = conv_b.reshape(1, D)
    clg, clb = conv_ln_g.reshape(1, D), conv_ln_b.reshape(1, D)
    ws = w_s[0]
    bst = jnp.transpose(b_s[0])
    tq = _attn_tq(S)
    bm = band_bias_table(rel_bias[0], tq)

    hn0 = rms_fwd("rms_mix0", xs, norm_mix_g[0])
    layer0 = ["out_ab", "xq0", "xk0", "xv0", "xo0"]
    layer1 = ["out_c", "xq1", "xk1", "xv1", "xo1"]
    near, far, every = (0, 1), (2,), (0, 1, 2)
    proj0, (wab4,) = proj_cols_own("proj_ab_own", hn0, w_in_ab[0], place,
                                   comm=gather_job([slots["in_ab"]], relay_frac=1.0, flips=[near]))
    proj0, (wab4, w_out_ab4) = proj_cols_rest(
        "proj_ab_near", hn0, wab4, proj0, place, (2, 1),
        comm=gather_job([wab4, slots["out_ab"]], flips=[far, every]))
    proj0, got_qk = proj_cols_rest("proj_ab_far", hn0, wab4, proj0, place, (3,),
                                   comm=gather_job([slots["xq0"], slots["xk0"]]))
    (ya, lse_a), (w_xv0, w_xo0, wc4) = attn_fwd(
        proj0, bm, D, comm=gather_job([slots["xv0"], slots["xo0"], slots["in_c"]]))
    (y0, cpre), got1 = conv_gate_fwd(proj0, ya, cw_pad, cb, clg, clb, D,
                                     comm=gather_job([slots[n] for n in layer1]))
    got0 = [w_out_ab4] + got_qk + [w_xv0, w_xo0]
    wrow = {n: g.reshape(-1, g.shape[2]) for n, g in zip(layer0 + layer1, got0 + got1)}
    wx = [{k: wrow[f"x{k}{layer}"] for k in "qkvo"} for layer in range(2)]
    h1 = mm_nn("out_ab", y0, wrow["out_ab"], F32, res=xs)
    h2, sx0 = _xattn_layer_fwd("0", h1, mems, norm_x_g[0], norm_mem_g[0], wx[0])
    hn1 = rms_fwd("rms_mix1", h2, norm_mix_g[1])
    proj1 = mm_nn_cols("proj_c", hn1, wc4, BF16)
    y1 = sgu_fwd(proj1, sgu_g_full, sgu_b_full, ws, bst, MIX)
    h3 = mm_nn("out_c", y1, wrow["out_c"], F32, res=h2)
    h4, sx1 = _xattn_layer_fwd("1", h3, mems, norm_x_g[1], norm_mem_g[1], wx[1])
    loss_row, dg_final, dh4, dh4b = loss_head("loss_head", h4, final_norm_g, tgt)

    dh3, dh3b, dgx1, dgmem1, dwx1 = _xattn_layer_bwd("1", dh4, dh4b, h3, mems, norm_x_g[1], norm_mem_g[1], wx[1], sx1)
    def stack_rows(dw_out, dwx):
        return jnp.concatenate([g.reshape(N_CHIPS, -1, g.shape[1]) for g in [dw_out] + [dwx[k] for k in "qkvo"]],
                               axis=1)

    dy1 = mm_nt("d_out_c", dh3b, wrow["out_c"], BF16)
    dw_out_c = mm_tn("dw_out_c", y1, dh3b)
    dproj1, dws, dbst, dsgu_g, dsgu_b = sgu_bwd(dy1, proj1, sgu_g_full, sgu_b_full, ws, bst, MIX)
    grp1 = stack_rows(dw_out_c, dwx1)
    dw_in_c, (sib1,) = mm_tn_cols("dw_in_c", hn1, dproj1, comm=sibling_halves_job([grp1]))
    part1 = add_halves("add_halves1", grp1, sib1, cidx)
    dhn1, (recv1, sib2) = mm_nt_cols("d_proj_c", dproj1, wc4, F32,
                                     comm=_join(scatter_job([part1]), sibling_halves_job([dw_in_c])))
    part2 = add_halves("add_halves2", dw_in_c, sib2, cidx)
    dh2, dh2b, dgmix1 = rms_bwd("rms_mix1_bwd", h2, norm_mix_g[1], [dhn1], dh3)
    dh1, dh1b, dgx0, dgmem0, dwx0 = _xattn_layer_bwd("0", dh2, dh2b, h1, mems, norm_x_g[0], norm_mem_g[0], wx[0], sx0)
    dy0 = mm_nt("d_out_ab", dh1b, wrow["out_ab"], BF16)
    dw_out_ab = mm_tn("dw_out_ab", y0, dh1b)
    grp3 = stack_rows(dw_out_ab, dwx0)
    dya, dgate, dc, dclg, dclb = conv_gate_bwd_a(dy0, proj0, ya, cpre, clg, clb, D)
    (da, db, dcw, dcb), (recv2, sib3) = conv_gate_bwd_b(
        dc, proj0, cw_pad, D, comm=_join(scatter_job([part2]), sibling_halves_job([grp3])))
    part3 = add_halves("add_halves3", grp3, sib3, cidx)
    (dq, dkc, dkp, dvc, dvp, ds_sum), (recv3,) = attn_bwd(proj0, ya, dya, lse_a, bm, D, comm=scatter_job([part3]))
    drel = rel_bias_grad(ds_sum)
    dproj0 = assemble_dproj0(dq, dkc, dkp, dvc, dvp, da, db, dgate, D)
    dw_in_ab = mm_tn_cols("dw_in_ab", hn0, dproj0)
    (sib4,) = run_comm("sibling_halves4", sibling_halves_job([dw_in_ab]))
    part4 = add_halves("add_halves4", dw_in_ab, sib4, cidx)
    halves = [sum_chips(f"sum_chips{t + 1}", p, r, place)
              for t, (p, r) in enumerate(((part1, recv1), (part2, recv2), (part3, recv3)))]
    small_early = [
        jnp.concatenate([dgx0, dgx1], axis=0), jnp.concatenate([dgmem0, dgmem1], axis=0), dg_final.reshape(D),
        drel[None], dcb, dclg, dclb, dws[None], jnp.transpose(dbst)[None],
        dcw[:CONV_WIDTH][None], dsgu_g, dsgu_b]
    early = _pack(small_early, row_multiple=512)
    dhn0, (recv4, small_slots, g_r1, g_c, g_r0) = mm_nt_cols(
        "d_proj_ab", dproj0, wab4, F32,
        comm=_join(scatter_job([part4]), exchange_job(early), share_halves_job(halves)))
    dx, _, dgmix0 = rms_bwd("rms_mix0_bwd", xs, norm_mix_g[0], [dhn0], dh1)
    (g_ab,) = run_comm("share_reduced_half4", share_halves_job([sum_chips("sum_chips4", part4, recv4, place)]))

    me = 4 * cx + 2 * cy + cc
    small_slots = lax.dynamic_update_slice(small_slots, early[None], (me, 0, 0))
    summed = _unpack(sum_devices("sum_small", small_slots), [a.shape for a in small_early])
    (g_norm_x, g_norm_mem, g_final, g_rel, g_conv_b, g_clg, g_clb, g_ws, g_bs,
     g_conv_w_full, g_sgu_g_full, g_sgu_b_full) = summed
    dgmix = jnp.concatenate([dgmix0, dgmix1], axis=0)
    (g_norm_mix,) = _unpack(exchange_small("reduce_late", _pack([dgmix]), reduce=True), [dgmix.shape])
    cws = conv_w.shape[2]
    g_conv_w = lax.dynamic_slice_in_dim(g_conv_w_full, chip * cws, cws, axis=2)
    sgs = sgu_ln_g.shape[1]
    g_sgu_g = lax.dynamic_slice_in_dim(g_sgu_g_full, chip * sgs, sgs, axis=1)
    g_sgu_b = lax.dynamic_slice_in_dim(g_sgu_b_full, chip * sgs, sgs, axis=1)

    loss = lax.psum(loss_row[0, 0], ("x", "y", "c"))

    g_rows = {"w_out_ab": g_r0[0:ro][None], "w_out_c": g_r1[0:ro][None]}
    for i, nm_ in enumerate("qkvo"):
        lo = ro + i * rq
        g_rows["w_x" + nm_] = jnp.stack([g_r0[lo:lo + rq], g_r1[lo:lo + rq]])
    grads = dict(
        norm_mix_g=g_norm_mix, norm_x_g=g_norm_x, norm_mem_g=g_norm_mem, final_norm_g=g_final,
        w_in_ab=g_ab[None], rel_bias=g_rel, conv_w=g_conv_w, conv_b=g_conv_b, conv_ln_g=g_clg, conv_ln_b=g_clb,
        w_out_ab=g_rows["w_out_ab"], w_in_c=g_c[None], sgu_ln_g=g_sgu_g, sgu_ln_b=g_sgu_b, w_s=g_ws, b_s=g_bs,
        w_out_c=g_rows["w_out_c"], w_xq=g_rows["w_xq"], w_xk=g_rows["w_xk"], w_xv=g_rows["w_xv"],
        w_xo=g_rows["w_xo"])
    weights = dict(
        norm_mix_g=(norm_mix_g, m_norm_mix_g, v_norm_mix_g), norm_x_g=(norm_x_g, m_norm_x_g, v_norm_x_g),
        norm_mem_g=(norm_mem_g, m_norm_mem_g, v_norm_mem_g), final_norm_g=(final_norm_g, m_final_norm_g, v_final_norm_g),
        w_in_ab=(w_in_ab, m_w_in_ab, v_w_in_ab), rel_bias=(rel_bias, m_rel_bias, v_rel_bias),
        conv_w=(conv_w, m_conv_w, v_conv_w), conv_b=(conv_b, m_conv_b, v_conv_b),
        conv_ln_g=(conv_ln_g, m_conv_ln_g, v_conv_ln_g), conv_ln_b=(conv_ln_b, m_conv_ln_b, v_conv_ln_b),
        w_out_ab=(w_out_ab, m_w_out_ab, v_w_out_ab), w_in_c=(w_in_c, m_w_in_c, v_w_in_c),
        sgu_ln_g=(sgu_ln_g, m_sgu_ln_g, v_sgu_ln_g), sgu_ln_b=(sgu_ln_b, m_sgu_ln_b, v_sgu_ln_b),
        w_s=(w_s, m_w_s, v_w_s), b_s=(b_s, m_b_s, v_b_s), w_out_c=(w_out_c, m_w_out_c, v_w_out_c),
        w_xq=(w_xq, m_w_xq, v_w_xq), w_xk=(w_xk, m_w_xk, v_w_xk), w_xv=(w_xv, m_w_xv, v_w_xv),
        w_xo=(w_xo, m_w_xo, v_w_xo))
    names = list(weights)
    big_names = ("w_in_ab", "w_out_ab", "w_in_c", "w_out_c", "w_xq", "w_xk", "w_xv", "w_xo")
    delta, new_m, new_v = {}, {}, {}
    for nm_ in big_names:
        w, m, v = weights[nm_]
        C = w.shape[-1]
        d2, m2, v2 = adamw("adamw_" + nm_, w.reshape(-1, C), grads[nm_].reshape(-1, C), m.reshape(-1, C),
                           v.reshape(-1, C))
        delta[nm_], new_m[nm_], new_v[nm_] = d2.reshape(w.shape), m2.reshape(w.shape), v2.reshape(w.shape)
    small_names = [n for n in names if n not in big_names]
    shapes = [weights[n][0].shape for n in small_names]
    d_s, m_s, v_s = adamw("adamw_small", _pack([weights[n][0] for n in small_names]),
                          _pack([grads[n] for n in small_names]), _pack([weights[n][1] for n in small_names]),
                          _pack([weights[n][2] for n in small_names]))
    for n, d_, m_, v_ in zip(small_names, _unpack(d_s, shapes), _unpack(m_s, shapes), _unpack(v_s, shapes)):
        delta[n], new_m[n], new_v[n] = d_, m_, v_

    return (loss, dx[None], *[grads[n].reshape(weights[n][0].shape) for n in names], *[delta[n] for n in names],
            *[new_m[n] for n in names], *[new_v[n] for n in names])
```

```python
import functools

import numpy as np
import jax
import jax.numpy as jnp
from jax import lax
from jax.experimental import pallas as pl
from jax.experimental.pallas import tpu as pltpu

F32 = jnp.float32
BF16 = jnp.bfloat16
MESH = pl.DeviceIdType.MESH

EPS = 1e-6
CHUNK = 64
N_PAST_CHUNKS = 8
MAX_REL = 128
HEAD_DIM_A = 128
CONV_WIDTH = 31
CONV_HALO = 32
GMLP_CHUNK = 128
N_GROUPS_C = 8
N_HEADS_X = 4
NEG = -1e30

ADAM_LR = 0.001
ADAM_B1 = 0.9
ADAM_B2 = 0.999
ADAM_EPS = 1e-08
ADAM_WD = 0.01
ADAM_STEP = 10

N_CHIPS = 4
N_DEV = 8
V7X_VMEM_LIMIT = 56 * 1024 * 1024
LANES = 128
SUBLANES = 8


def _pick(n, cands):
    for c in cands:
        if c <= n and n % c == 0:
            return c
    return n


def _cparams(*sem):
    return pltpu.CompilerParams(dimension_semantics=sem, vmem_limit_bytes=V7X_VMEM_LIMIT)


def _sigmoid(x):
    return 0.5 * jnp.tanh(0.5 * x) + 0.5


def _dot(a, b, contract):
    return lax.dot_general(a, b, (contract, ((), ())), preferred_element_type=F32)


NN = ((1,), (0,))
NT = ((1,), (1,))
TN = ((0,), (0,))


class _Comm:
    def __init__(self, arrays, out_shapes, aliases, sems, start, finish, relay=None, relay_frac=0.75):
        self.arrays, self.out_shapes, self.aliases, self.sems = list(arrays), list(out_shapes), dict(aliases), list(sems)
        self.start, self.finish, self.relay = start, finish, relay
        self.relay_frac = relay_frac


def _join(*jobs):
    assert all(j.relay is None for j in jobs)
    arrays, outs, sems, aliases, spans = [], [], [], {}, []
    for j in jobs:
        spans.append((len(arrays), len(outs), len(sems)))
        aliases.update({len(arrays) + i: len(outs) + o for i, o in j.aliases.items()})
        arrays += j.arrays
        outs += j.out_shapes
        sems += j.sems

    def part(j, span, ins, os_, ss):
        a0, o0, s0 = span
        return (ins[a0:a0 + len(j.arrays)], os_[o0:o0 + len(j.out_shapes)], ss[s0:s0 + len(j.sems)])

    def start(ins, os_, ss):
        for j, span in zip(jobs, spans):
            j.start(*part(j, span, ins, os_, ss))

    def finish(ins, os_, ss):
        for j, span in zip(jobs, spans):
            j.finish(*part(j, span, ins, os_, ss))

    return _Comm(arrays, outs, aliases, sems, start, finish)


def _call(body, *, name, grid, in_specs, out_specs, out_shape, args, scratch_shapes=(), sem=None, comm=None,
          prefetch=None, io_aliases=None):
    multi = isinstance(out_shape, (list, tuple))
    o_shapes = list(out_shape) if multi else [out_shape]
    o_specs = list(out_specs) if multi else [out_specs]
    if comm is None:
        assert prefetch is None and io_aliases is None
        return pl.pallas_call(body, grid=grid, in_specs=in_specs, out_specs=out_specs, out_shape=out_shape,
                              scratch_shapes=list(scratch_shapes), name=name,
                              compiler_params=_cparams(*sem))(*args)
    n_in, n_out, n_scr = len(in_specs), len(o_shapes), len(scratch_shapes)
    n_ci, n_co = len(comm.arrays), len(comm.out_shapes)
    n_steps = int(np.prod(grid))
    n_pre = 0 if prefetch is None else 1

    def carrier(*refs):
        pre, refs = refs[:n_pre], refs[n_pre:]
        ins, rest = refs[:n_in], refs[n_in:]
        cins, rest = rest[:n_ci], rest[n_ci:]
        outs, rest = rest[:n_out], rest[n_out:]
        couts, rest = rest[:n_co], rest[n_co:]
        scr, csems = rest[:n_scr], rest[n_scr:]
        step = 0
        for a, g in enumerate(grid):
            step = step * g + pl.program_id(a)

        @pl.when(step == 0)
        def _():
            comm.start(cins, couts, csems)

        body(*pre, *ins, *outs, *scr)

        relay_step = min(int(comm.relay_frac * n_steps), n_steps - 1)
        if comm.relay is not None and relay_step < n_steps - 1:
            @pl.when(step == relay_step)
            def _():
                comm.relay(cins, couts, csems)

        @pl.when(step == n_steps - 1)
        def _():
            if comm.relay is not None and relay_step == n_steps - 1:
                comm.relay(cins, couts, csems)
            comm.finish(cins, couts, csems)

    aliases = {n_pre + n_in + i: n_out + o for i, o in comm.aliases.items()}
    aliases.update({n_pre + i: o for i, o in (io_aliases or {}).items()})
    all_in = list(in_specs) + [HBM_SPEC] * n_ci
    all_out = o_specs + [HBM_SPEC] * n_co
    all_scratch = list(scratch_shapes) + comm.sems
    params = _cparams(*(["arbitrary"] * len(grid)))
    if prefetch is None:
        res = pl.pallas_call(
            carrier, grid=grid, in_specs=all_in, out_specs=all_out, out_shape=o_shapes + comm.out_shapes,
            input_output_aliases=aliases, scratch_shapes=all_scratch, name=name,
            compiler_params=params)(*args, *comm.arrays)
    else:
        grid_spec = pltpu.PrefetchScalarGridSpec(num_scalar_prefetch=1, grid=grid, in_specs=all_in,
                                                 out_specs=all_out, scratch_shapes=all_scratch)
        res = pl.pallas_call(
            carrier, grid_spec=grid_spec, out_shape=o_shapes + comm.out_shapes, input_output_aliases=aliases,
            name=name, compiler_params=params)(prefetch, *args, *comm.arrays)
    mine = list(res[:n_out]) if multi else res[0]
    return mine, list(res[n_out:])


def run_comm(name, comm):
    def body(*refs):
        n_ci, n_co = len(comm.arrays), len(comm.out_shapes)
        cins, couts, csems = refs[:n_ci], refs[n_ci:n_ci + n_co], refs[n_ci + n_co:]
        comm.start(cins, couts, csems)
        if comm.relay is not None:
            comm.relay(cins, couts, csems)
        comm.finish(cins, couts, csems)

    return pl.pallas_call(
        body, in_specs=[HBM_SPEC] * len(comm.arrays), out_specs=[HBM_SPEC] * len(comm.out_shapes),
        out_shape=comm.out_shapes, input_output_aliases=comm.aliases, scratch_shapes=comm.sems,
        name=name)(*comm.arrays)


def _mm(name, a, b, *, contract, grid, a_spec, b_spec, o_spec, out_shape, res=None, comm=None):
    nk = grid[2]

    def body(*refs):
        if res is not None:
            a_ref, b_ref, r_ref, o_ref = refs[:4]
        else:
            a_ref, b_ref, o_ref = refs[:3]
            r_ref = None
        p = _dot(a_ref[...].astype(BF16), b_ref[...].astype(BF16), contract)

        def finish(acc):
            if r_ref is not None:
                acc = acc + r_ref[...]
            o_ref[...] = acc.astype(o_ref.dtype)

        if nk == 1:
            finish(p)
        else:
            acc_ref = refs[-1]
            k = pl.program_id(2)

            @pl.when(k == 0)
            def _():
                acc_ref[...] = p

            @pl.when(k > 0)
            def _():
                acc_ref[...] += p

            @pl.when(k == nk - 1)
            def _():
                finish(acc_ref[...])

    in_specs = [a_spec, b_spec]
    args = [a, b]
    if res is not None:
        in_specs.append(o_spec)
        args.append(res)
    blk = tuple(d for d in o_spec.block_shape if d is not None)
    scratch = [] if nk == 1 else [pltpu.VMEM(blk, F32)]
    return _call(body, name=name, grid=grid, in_specs=in_specs, out_specs=o_spec, out_shape=out_shape,
                 args=args, scratch_shapes=scratch, sem=("parallel", "parallel", "arbitrary"), comm=comm)


def mm_nn_cols(name, a, w4, out_dtype, comm=None):
    M, K = a.shape
    _, _, C = w4.shape
    tm = _pick(M, (1024, 512, 256))
    tn = _pick(C, (1024, 512, 256, 128))
    nps = C // tn
    return _mm(name, a, w4, contract=NN, grid=(M // tm, 4 * nps, 1),
               a_spec=pl.BlockSpec((tm, K), lambda i, j, k: (i, 0)),
               b_spec=pl.BlockSpec((None, K, tn), lambda i, j, k: (j // nps, 0, j % nps)),
               o_spec=pl.BlockSpec((tm, tn), lambda i, j, k: (i, j)),
               out_shape=jax.ShapeDtypeStruct((M, 4 * C), out_dtype), comm=comm)


def proj_cols_own(name, a, w_own, place, comm):
    M, K = a.shape
    C = w_own.shape[1]
    tm = _pick(M, (1024, 512, 256))
    tn = _pick(C, (512, 256, 128))
    nps = C // tn

    def body(s_ref, a_ref, b_ref, o_ref):
        o_ref[...] = _dot(a_ref[...], b_ref[...].astype(BF16), NN).astype(o_ref.dtype)

    return _call(body, name=name, grid=(M // tm, nps),
                 in_specs=[pl.BlockSpec((tm, K), lambda i, j, s: (i, 0)),
                           pl.BlockSpec((K, tn), lambda i, j, s: (0, j))],
                 out_specs=pl.BlockSpec((tm, tn), lambda i, j, s: (i, s[0] * nps + j)),
                 out_shape=jax.ShapeDtypeStruct((M, N_CHIPS * C), BF16), args=[a, w_own], comm=comm,
                 prefetch=place)


def proj_cols_rest(name, a, w4, partial, place, masks, comm):
    M, K = a.shape
    C = w4.shape[2]
    tm = _pick(M, (1024, 512, 256))
    tn = _pick(C, (512, 256, 128))
    nps = C // tn
    assert len(masks) in (1, 2)
    step = masks[-1] - masks[0]

    def slot(j, s):
        return jnp.bitwise_xor(s[0], masks[0] + step * (j // nps))

    def body(s_ref, a_ref, b_ref, part_ref, o_ref):
        o_ref[...] = _dot(a_ref[...], b_ref[...], NN).astype(o_ref.dtype)

    return _call(body, name=name, grid=(M // tm, len(masks) * nps),
                 in_specs=[pl.BlockSpec((tm, K), lambda i, j, s: (i, 0)),
                           pl.BlockSpec((None, K, tn), lambda i, j, s: (slot(j, s), 0, j % nps)),
                           HBM_SPEC],
                 out_specs=pl.BlockSpec((tm, tn), lambda i, j, s: (i, slot(j, s) * nps + j % nps)),
                 out_shape=jax.ShapeDtypeStruct(partial.shape, partial.dtype), args=[a, w4, partial],
                 comm=comm, prefetch=place, io_aliases={2: 0})


def mm_nn(name, a, w, out_dtype, res=None, comm=None):
    M, K = a.shape
    N = w.shape[1]
    tm = _pick(M, (1024, 512, 256))
    tn = _pick(N, (512, 256, 128))
    return _mm(name, a, w, contract=NN, grid=(M // tm, N // tn, 1),
               a_spec=pl.BlockSpec((tm, K), lambda i, j, k: (i, 0)),
               b_spec=pl.BlockSpec((K, tn), lambda i, j, k: (0, j)),
               o_spec=pl.BlockSpec((tm, tn), lambda i, j, k: (i, j)),
               out_shape=jax.ShapeDtypeStruct((M, N), out_dtype), res=res, comm=comm)


def mm_nt_cols(name, a, w4, out_dtype, comm=None):
    M = a.shape[0]
    _, K, C = w4.shape
    tm = _pick(M, (1024, 512, 256))
    tn = _pick(K, (1024, 512, 256, 128))
    tk = _pick(C, (3584, 3072, 1792, 1536, 1024, 512, 256, 128))
    kps = C // tk
    return _mm(name, a, w4, contract=NT, grid=(M // tm, K // tn, 4 * kps),
               a_spec=pl.BlockSpec((tm, tk), lambda i, j, k: (i, k)),
               b_spec=pl.BlockSpec((None, tn, tk), lambda i, j, k: (k // kps, j, k % kps)),
               o_spec=pl.BlockSpec((tm, tn), lambda i, j, k: (i, j)),
               out_shape=jax.ShapeDtypeStruct((M, K), out_dtype), comm=comm)


def mm_nt(name, a, w, out_dtype):
    M, C = a.shape
    N = w.shape[0]
    tm = _pick(M, (1024, 512, 256))
    tn = _pick(N, (512, 256, 128))
    return _mm(name, a, w, contract=NT, grid=(M // tm, N // tn, 1),
               a_spec=pl.BlockSpec((tm, C), lambda i, j, k: (i, 0)),
               b_spec=pl.BlockSpec((tn, C), lambda i, j, k: (j, 0)),
               o_spec=pl.BlockSpec((tm, tn), lambda i, j, k: (i, j)),
               out_shape=jax.ShapeDtypeStruct((M, N), out_dtype))


def mm_tn_cols(name, a, b, comm=None):
    S, K = a.shape
    C = b.shape[1] // 4
    ts = _pick(S, (2048, 1024, 512, 256))
    tko = _pick(K, (1024, 512, 256, 128))
    tn = _pick(C, (1792, 1536, 1024, 512, 256, 128))
    nps = C // tn
    return _mm(name, a, b, contract=TN, grid=(K // tko, 4 * nps, S // ts),
               a_spec=pl.BlockSpec((ts, tko), lambda i, j, k: (k, i)),
               b_spec=pl.BlockSpec((ts, tn), lambda i, j, k: (k, j)),
               o_spec=pl.BlockSpec((None, tko, tn), lambda i, j, k: (j // nps, i, j % nps)),
               out_shape=jax.ShapeDtypeStruct((4, K, C), BF16), comm=comm)


def mm_tn(name, a, b):
    S, K = a.shape
    N = b.shape[1]
    ts = _pick(S, (1024, 512, 256))
    tko = _pick(K, (2048, 1024, 512, 256, 128))
    tn = _pick(N, (1024, 512, 256, 128))
    return _mm(name, a, b, contract=TN, grid=(K // tko, N // tn, S // ts),
               a_spec=pl.BlockSpec((ts, tko), lambda i, j, k: (k, i)),
               b_spec=pl.BlockSpec((ts, tn), lambda i, j, k: (k, j)),
               o_spec=pl.BlockSpec((tko, tn), lambda i, j, k: (i, j)),
               out_shape=jax.ShapeDtypeStruct((K, N), BF16))


def rms_fwd(name, x, g):
    S, D = x.shape
    T = _pick(S, (512, 256))

    def body(x_ref, g_ref, o_ref):
        xf = x_ref[...]
        r = lax.rsqrt(jnp.mean(xf * xf, axis=-1, keepdims=True) + EPS)
        o_ref[...] = (xf * r * g_ref[...]).astype(o_ref.dtype)

    return pl.pallas_call(
        body, grid=(S // T,),
        in_specs=[pl.BlockSpec((T, D), lambda i: (i, 0)), pl.BlockSpec((1, D), lambda i: (0, 0))],
        out_specs=pl.BlockSpec((T, D), lambda i: (i, 0)),
        out_shape=jax.ShapeDtypeStruct((S, D), BF16), name=name,
        compiler_params=_cparams("parallel"))(x, g.reshape(1, D))


def rms_bwd(name, x, g, dys, dres):
    S, D = x.shape
    T = _pick(S, (256,))
    ndy = len(dys)
    has_res = dres is not None

    def body(*refs):
        x_ref, g_ref = refs[0], refs[1]
        dy_refs = refs[2:2 + ndy]
        r_ref = refs[2 + ndy] if has_res else None
        dx_ref, dxb_ref, dg_ref = refs[-3], refs[-2], refs[-1]
        i = pl.program_id(0)
        xf = x_ref[...]
        r = lax.rsqrt(jnp.mean(xf * xf, axis=-1, keepdims=True) + EPS)
        xhat = xf * r
        dy = dy_refs[0][...].astype(F32)
        for d in dy_refs[1:]:
            dy = dy + d[...].astype(F32)
        dxhat = dy * g_ref[...]
        dx = r * (dxhat - xhat * jnp.mean(dxhat * xhat, axis=-1, keepdims=True))
        if has_res:
            dx = dx + r_ref[...]
        dx_ref[...] = dx
        dxb_ref[...] = dx.astype(dxb_ref.dtype)
        dg = jnp.sum(dy * xhat, axis=0, keepdims=True)

        @pl.when(i == 0)
        def _():
            dg_ref[...] = dg

        @pl.when(i > 0)
        def _():
            dg_ref[...] += dg

    row = pl.BlockSpec((T, D), lambda i: (i, 0))
    vec = pl.BlockSpec((1, D), lambda i: (0, 0))
    args = [x, g.reshape(1, D), *dys] + ([dres] if has_res else [])
    return pl.pallas_call(
        body, grid=(S // T,),
        in_specs=[row, vec] + [row] * (ndy + int(has_res)),
        out_specs=[row, row, vec],
        out_shape=[jax.ShapeDtypeStruct((S, D), F32), jax.ShapeDtypeStruct((S, D), BF16),
                   jax.ShapeDtypeStruct((1, D), F32)],
        name=name, compiler_params=_cparams("arbitrary"))(*args)


def loss_head(name, h, g, target):
    S, D = h.shape
    T = _pick(S, (256,))

    def body(h_ref, g_ref, t_ref, loss_ref, dg_ref, dh_ref, dhb_ref):
        i = pl.program_id(0)
        xf = h_ref[...]
        gv = g_ref[...]
        r = lax.rsqrt(jnp.mean(xf * xf, axis=-1, keepdims=True) + EPS)
        xhat = xf * r
        err = xhat * gv - t_ref[...]
        part = 0.5 * jnp.sum(jnp.sum(err * err, axis=-1, keepdims=True), axis=0, keepdims=True) / D
        dout = err / D
        dxhat = dout * gv
        dh = r * (dxhat - xhat * jnp.mean(dxhat * xhat, axis=-1, keepdims=True))
        dh_ref[...] = dh
        dhb_ref[...] = dh.astype(dhb_ref.dtype)
        dg = jnp.sum(dout * xhat, axis=0, keepdims=True)
        lrow = jnp.broadcast_to(part, (1, LANES))

        @pl.when(i == 0)
        def _():
            dg_ref[...] = dg
            loss_ref[...] = lrow

        @pl.when(i > 0)
        def _():
            dg_ref[...] += dg
            loss_ref[...] += lrow

    row = pl.BlockSpec((T, D), lambda i: (i, 0))
    vec = pl.BlockSpec((1, D), lambda i: (0, 0))
    return pl.pallas_call(
        body, grid=(S // T,), in_specs=[row, vec, row],
        out_specs=[pl.BlockSpec((1, LANES), lambda i: (0, 0)), vec, row, row],
        out_shape=[jax.ShapeDtypeStruct((1, LANES), F32), jax.ShapeDtypeStruct((1, D), F32),
                   jax.ShapeDtypeStruct((S, D), F32), jax.ShapeDtypeStruct((S, D), BF16)],
        name=name, compiler_params=_cparams("arbitrary"))(h, g.reshape(1, D), target)


def _attn_tq(S):
    return _pick(S, (512,))


def band_bias_table(rel_bias, tq):
    H = rel_bias.shape[0]
    w = 2 * tq
    nbits = int(np.log2(tq))
    assert (1 << nbits) == tq and (N_PAST_CHUNKS + 2) * CHUNK - 1 <= w
    c = np.arange(w)
    d0 = np.where(c <= tq + CHUNK - 1, tq - c, tq + w - c)
    base = jnp.take(rel_bias.astype(F32), jnp.asarray(np.clip(d0, -MAX_REL, MAX_REL) + MAX_REL), axis=1)

    def body(b_ref, o_ref):
        x = jnp.broadcast_to(b_ref[...], (tq, w))
        row = lax.broadcasted_iota(jnp.int32, (tq, w), 0)
        col = lax.broadcasted_iota(jnp.int32, (tq, w), 1)
        for b in range(nbits):
            x = jnp.where(((row >> b) & 1) == 1, pltpu.roll(x, 1 << b, 1), x)
        qc = row // CHUNK
        kc = col // CHUNK - tq // CHUNK
        o_ref[...] = jnp.where((kc <= qc) & (kc >= qc - N_PAST_CHUNKS), x, NEG)

    return pl.pallas_call(
        body, grid=(H,), in_specs=[pl.BlockSpec((None, 1, w), lambda h: (h, 0, 0))],
        out_specs=pl.BlockSpec((None, tq, w), lambda h: (h, 0, 0)),
        out_shape=jax.ShapeDtypeStruct((H, tq, w), F32), name="band_bias_table",
        compiler_params=_cparams("parallel"))(base.reshape(H, 1, w))


def _attn_subblocks(tq):
    sub = tq // 2
    assert sub % CHUNK == 0 and N_PAST_CHUNKS * CHUNK == tq
    return sub, 3


def attn_fwd(proj, bm, D, comm=None):
    S = proj.shape[0]
    H = D // HEAD_DIM_A
    tq = _attn_tq(S)
    nb = S // tq
    scale = HEAD_DIM_A ** -0.5

    sub, n_sub = _attn_subblocks(tq)

    def body(q_ref, kp_ref, kc_ref, vp_ref, vc_ref, bm_ref, o_ref, lse_ref):
        i = pl.program_id(1)
        for qh in range(tq // sub):
            rows = slice(qh * sub, (qh + 1) * sub)
            q = q_ref[rows, :]
            ss = []
            for kb in range(qh, qh + n_sub):
                k_ref, krows = (kp_ref, kb) if kb < tq // sub else (kc_ref, kb - tq // sub)
                s = _dot(q, k_ref[krows * sub:(krows + 1) * sub, :], NT) * scale + bm_ref[rows, kb * sub:(kb + 1) * sub]
                if kb < tq // sub:
                    s = jnp.where(i == 0, NEG, s)
                ss.append(s)
            m = functools.reduce(jnp.maximum, [jnp.max(s, axis=-1, keepdims=True) for s in ss])
            ps = [jnp.exp(s - m) for s in ss]
            l = functools.reduce(jnp.add, [jnp.sum(p, axis=-1, keepdims=True) for p in ps])
            o = None
            for p, kb in zip(ps, range(qh, qh + n_sub)):
                v_ref, vrows = (vp_ref, kb) if kb < tq // sub else (vc_ref, kb - tq // sub)
                t = _dot(p.astype(BF16), v_ref[vrows * sub:(vrows + 1) * sub, :], NN)
                o = t if o is None else o + t
            o_ref[rows, :] = (o / l).astype(o_ref.dtype)
            lse_ref[rows, :] = m + jnp.log(l)

    def col(base):
        return (pl.BlockSpec((tq, HEAD_DIM_A), lambda h, i: (jnp.maximum(i - 1, 0), base + h)),
                pl.BlockSpec((tq, HEAD_DIM_A), lambda h, i: (i, base + h)))

    kp, kc = col(H)
    vp, vc = col(2 * H)
    return _call(
        body, name="attn_fwd", grid=(H, nb),
        in_specs=[pl.BlockSpec((tq, HEAD_DIM_A), lambda h, i: (i, h)), kp, kc, vp, vc,
                  pl.BlockSpec((None, tq, 2 * tq), lambda h, i: (h, 0, 0))],
        out_specs=[pl.BlockSpec((tq, HEAD_DIM_A), lambda h, i: (i, h)),
                   pl.BlockSpec((None, tq, 1), lambda h, i: (h, i, 0))],
        out_shape=[jax.ShapeDtypeStruct((S, D), BF16), jax.ShapeDtypeStruct((H, S, 1), F32)],
        args=[proj, proj, proj, proj, proj, bm], sem=("parallel", "arbitrary"), comm=comm)


def attn_bwd(proj, ya, dya, lse, bm, D, comm=None):
    S = proj.shape[0]
    H = D // HEAD_DIM_A
    tq = _attn_tq(S)
    nb = S // tq
    scale = HEAD_DIM_A ** -0.5
    sub, n_sub = _attn_subblocks(tq)

    def body(q_ref, kp_ref, kc_ref, vp_ref, vc_ref, o_ref, do_ref, lse_ref, bm_ref,
             dq_ref, dkc_ref, dkp_ref, dvc_ref, dvp_ref, ds_ref):
        i = pl.program_id(1)
        per = tq // sub

        @pl.when(i == 0)
        def _():
            ds_ref[...] = jnp.zeros_like(ds_ref)

        dk_acc = [None] * (2 * per)
        dv_acc = [None] * (2 * per)
        for qh in range(per):
            rows = slice(qh * sub, (qh + 1) * sub)
            q = q_ref[rows, :]
            do = do_ref[rows, :]
            delta = jnp.sum(do.astype(F32) * o_ref[rows, :].astype(F32), axis=-1, keepdims=True)
            lse_v = lse_ref[rows, :]
            dq = None
            for kb in range(qh, qh + n_sub):
                k_ref, v_ref, kr = (kp_ref, vp_ref, kb) if kb < per else (kc_ref, vc_ref, kb - per)
                k = k_ref[kr * sub:(kr + 1) * sub, :]
                cols = slice(kb * sub, (kb + 1) * sub)
                s = _dot(q, k, NT) * scale + bm_ref[rows, cols]
                if kb < per:
                    s = jnp.where(i == 0, NEG, s)
                p = jnp.exp(s - lse_v)
                dv = _dot(p.astype(BF16), do, TN)
                dp = _dot(do, v_ref[kr * sub:(kr + 1) * sub, :], NT)
                ds = p * (dp - delta)
                dsb = ds.astype(BF16)
                t = _dot(dsb, k, NN)
                dq = t if dq is None else dq + t
                dk = _dot(dsb, q, TN)
                dk_acc[kb] = dk if dk_acc[kb] is None else dk_acc[kb] + dk
                dv_acc[kb] = dv if dv_acc[kb] is None else dv_acc[kb] + dv
                ds_ref[rows, cols] += ds
            dq_ref[rows, :] = (dq * scale).astype(dq_ref.dtype)
        for kb in range(2 * per):
            dk_ref, dv_ref, kr = (dkp_ref, dvp_ref, kb) if kb < per else (dkc_ref, dvc_ref, kb - per)
            dk_ref[kr * sub:(kr + 1) * sub, :] = (dk_acc[kb] * scale).astype(dk_ref.dtype)
            dv_ref[kr * sub:(kr + 1) * sub, :] = dv_acc[kb].astype(dv_ref.dtype)

    def col(base):
        return (pl.BlockSpec((tq, HEAD_DIM_A), lambda h, i: (jnp.maximum(i - 1, 0), base + h)),
                pl.BlockSpec((tq, HEAD_DIM_A), lambda h, i: (i, base + h)))

    kp, kc = col(H)
    vp, vc = col(2 * H)
    blk = pl.BlockSpec((tq, HEAD_DIM_A), lambda h, i: (i, h))
    sd = jax.ShapeDtypeStruct((S, D), BF16)
    return _call(
        body, name="attn_bwd", grid=(H, nb),
        in_specs=[blk, kp, kc, vp, vc, blk, blk,
                  pl.BlockSpec((None, tq, 1), lambda h, i: (h, i, 0)),
                  pl.BlockSpec((None, tq, 2 * tq), lambda h, i: (h, 0, 0))],
        out_specs=[blk, blk, blk, blk, blk, pl.BlockSpec((None, tq, 2 * tq), lambda h, i: (h, 0, 0))],
        out_shape=[sd, sd, sd, sd, sd, jax.ShapeDtypeStruct((H, tq, 2 * tq), F32)],
        args=[proj, proj, proj, proj, proj, ya, dya, lse, bm], sem=("parallel", "arbitrary"), comm=comm)


def rel_bias_grad(ds_sum):
    H, tq, w = ds_sum.shape
    nbin = 2 * MAX_REL + 1
    nbin_pad = 3 * LANES
    d_lo, d_hi = -(CHUNK - 1), (N_PAST_CHUNKS + 1) * CHUNK - 1
    assert d_hi - d_lo + 1 <= w
    onehot = np.zeros((w, nbin_pad), np.float32)
    for d in range(d_lo, d_hi + 1):
        onehot[(tq - d) % w, int(np.clip(d, -MAX_REL, MAX_REL)) + MAX_REL] = 1.0
    nbits = int(np.log2(tq))
    assert (1 << nbits) == tq

    def body(ds_ref, m_ref, o_ref):
        x = ds_ref[...]
        row = lax.broadcasted_iota(jnp.int32, x.shape, 0)
        for b in range(nbits):
            rolled = pltpu.roll(x, w - (1 << b), 1)
            x = jnp.where(((row >> b) & 1) == 1, rolled, x)
        t = jnp.sum(x, axis=0, keepdims=True)
        o_ref[...] = lax.dot_general(t, m_ref[...], (NN, ((), ())), precision=lax.Precision.HIGHEST,
                                     preferred_element_type=F32)

    out = pl.pallas_call(
        body, grid=(H,),
        in_specs=[pl.BlockSpec((None, tq, w), lambda h: (h, 0, 0)),
                  pl.BlockSpec((w, nbin_pad), lambda h: (0, 0))],
        out_specs=pl.BlockSpec((None, 1, nbin_pad), lambda h: (h, 0, 0)),
        out_shape=jax.ShapeDtypeStruct((H, 1, nbin_pad), F32),
        name="rel_bias_grad", compiler_params=_cparams("parallel"))(ds_sum, jnp.asarray(onehot))
    return out[:, 0, :nbin]


def _conv_t(S):
    return _pick(S, (256,))


ROW_CHUNK = 16


def _row_loop(n_rows, step):
    def one(r, carry):
        step(pl.ds(pl.multiple_of(r * ROW_CHUNK, ROW_CHUNK), ROW_CHUNK))
        return carry

    lax.fori_loop(0, n_rows // ROW_CHUNK, one, 0)


def _fill_zbuf(zbuf, ap_ref, bp_ref, a_ref, b_ref, i):
    zp = ap_ref[...].astype(F32) * _sigmoid(bp_ref[...].astype(F32))
    zbuf[0:CONV_HALO, :] = jnp.where(i == 0, 0.0, zp)

    def step(rows):
        below = pl.ds(pl.multiple_of(rows.start + CONV_HALO, ROW_CHUNK), ROW_CHUNK)
        zbuf[below, :] = a_ref[rows, :].astype(F32) * _sigmoid(b_ref[rows, :].astype(F32))

    _row_loop(a_ref.shape[0], step)


def _shifted_windows(buf, shifted, lanes, T):
    rows = T + CONV_HALO - SUBLANES
    for b in range(1, SUBLANES):
        shifted[b - 1] = buf[pl.ds(b, rows), lanes]

    def window(off, r0=0, n=T):
        a, b = divmod(off, SUBLANES)
        if b == 0:
            return buf[pl.ds(r0 + off, n), lanes]
        return shifted[b - 1, pl.ds(r0 + a * SUBLANES, n), :]

    return window


def _shifted_scratch(T):
    return pltpu.VMEM((SUBLANES - 1, T + CONV_HALO - SUBLANES, LANES), F32)


def conv_gate_fwd(proj, ya, cw, cb, lng, lnb, D, comm=None):
    S = proj.shape[0]
    T = _conv_t(S)
    hb = T // CONV_HALO
    nlb = D // LANES

    def body(ap_ref, bp_ref, a_ref, b_ref, ga_ref, gb_ref, ya_ref, cw_ref, cb_ref, lng_ref, lnb_ref,
             y_ref, c_ref, zbuf, zsh):
        i = pl.program_id(0)
        _fill_zbuf(zbuf, ap_ref, bp_ref, a_ref, b_ref, i)

        def lane_block(lb, carry):
            lanes = pl.ds(pl.multiple_of(lb * LANES, LANES), LANES)
            z_at = _shifted_windows(zbuf, zsh, lanes, T)
            acc = jnp.zeros((T, LANES), F32)
            for k in range(CONV_WIDTH):
                acc = acc + cw_ref[k:k + 1, lanes] * z_at(CONV_HALO - CONV_WIDTH + 1 + k)
            c_ref[:, lanes] = acc + cb_ref[:, lanes]
            return carry

        lax.fori_loop(0, nlb, lane_block, 0)

        def norm_and_gate(rows):
            c = c_ref[rows, :]
            mu = jnp.mean(c, axis=-1, keepdims=True)
            xc = c - mu
            rstd = lax.rsqrt(jnp.mean(xc * xc, axis=-1, keepdims=True) + EPS)
            ln = xc * rstd * lng_ref[...] + lnb_ref[...]
            yb = ln * _sigmoid(ln)
            ga = ga_ref[rows, :].astype(F32)
            gb = gb_ref[rows, :].astype(F32)
            y_ref[rows, :D] = (ya_ref[rows, :].astype(F32) * (ga * _sigmoid(ga))).astype(y_ref.dtype)
            y_ref[rows, D:] = (yb * (gb * _sigmoid(gb))).astype(y_ref.dtype)

        _row_loop(T, norm_and_gate)

    def cur(cidx):
        return pl.BlockSpec((T, D), lambda i: (i, cidx))

    def prev(cidx):
        return pl.BlockSpec((CONV_HALO, D), lambda i: (jnp.maximum(i * hb - 1, 0), cidx))

    vec = pl.BlockSpec((1, D), lambda i: (0, 0))
    return _call(
        body, name="conv_gate_fwd", grid=(S // T,),
        in_specs=[prev(3), prev(4), cur(3), cur(4), cur(5), cur(6), pl.BlockSpec((T, D), lambda i: (i, 0)),
                  pl.BlockSpec((CONV_HALO, D), lambda i: (0, 0)), vec, vec, vec],
        out_specs=[pl.BlockSpec((T, 2 * D), lambda i: (i, 0)), pl.BlockSpec((T, D), lambda i: (i, 0))],
        out_shape=[jax.ShapeDtypeStruct((S, 2 * D), BF16), jax.ShapeDtypeStruct((S, D), F32)],
        scratch_shapes=[pltpu.VMEM((T + CONV_HALO, D), F32), _shifted_scratch(T)],
        args=[proj, proj, proj, proj, proj, proj, ya, cw, cb, lng, lnb], sem=("parallel",), comm=comm)


def conv_gate_bwd_a(dy0, proj, ya, cpre, lng, lnb, D):
    S = proj.shape[0]
    T = _conv_t(S)

    def body(dy_ref, ga_ref, gb_ref, ya_ref, c_ref, lng_ref, lnb_ref,
             dya_ref, dg_ref, dc_ref, dlng_ref, dlnb_ref):
        i = pl.program_id(0)

        c = c_ref[...]
        gv = lng_ref[...]
        mu = jnp.mean(c, axis=-1, keepdims=True)
        xc = c - mu
        rstd = lax.rsqrt(jnp.mean(xc * xc, axis=-1, keepdims=True) + EPS)
        xhat = xc * rstd
        ln = xhat * gv + lnb_ref[...]
        sl = _sigmoid(ln)
        yb = ln * sl
        ga = ga_ref[...].astype(F32)
        gb = gb_ref[...].astype(F32)
        sa = _sigmoid(ga)
        sb = _sigmoid(gb)
        dy_a = dy_ref[:, :D].astype(F32)
        dy_b = dy_ref[:, D:].astype(F32)
        dya_ref[...] = (dy_a * (ga * sa)).astype(dya_ref.dtype)
        dg_ref[:, :D] = (dy_a * ya_ref[...].astype(F32) * (sa * (1.0 + ga * (1.0 - sa)))).astype(dg_ref.dtype)
        dg_ref[:, D:] = (dy_b * yb * (sb * (1.0 + gb * (1.0 - sb)))).astype(dg_ref.dtype)
        dln = dy_b * (gb * sb) * (sl * (1.0 + ln * (1.0 - sl)))
        dxhat = dln * gv
        dc_ref[...] = rstd * (dxhat - jnp.mean(dxhat, axis=-1, keepdims=True)
                              - xhat * jnp.mean(dxhat * xhat, axis=-1, keepdims=True))
        dlng = jnp.sum(dln * xhat, axis=0, keepdims=True)
        dlnb = jnp.sum(dln, axis=0, keepdims=True)

        @pl.when(i == 0)
        def _():
            dlng_ref[...] = dlng
            dlnb_ref[...] = dlnb

        @pl.when(i > 0)
        def _():
            dlng_ref[...] += dlng
            dlnb_ref[...] += dlnb

    row = pl.BlockSpec((T, D), lambda i: (i, 0))
    vec = pl.BlockSpec((1, D), lambda i: (0, 0))
    return pl.pallas_call(
        body, grid=(S // T,),
        in_specs=[pl.BlockSpec((T, 2 * D), lambda i: (i, 0)),
                  pl.BlockSpec((T, D), lambda i: (i, 5)), pl.BlockSpec((T, D), lambda i: (i, 6)),
                  row, row, vec, vec],
        out_specs=[row, pl.BlockSpec((T, 2 * D), lambda i: (i, 0)), row, vec, vec],
        out_shape=[jax.ShapeDtypeStruct((S, D), BF16), jax.ShapeDtypeStruct((S, 2 * D), BF16),
                   jax.ShapeDtypeStruct((S, D), F32), jax.ShapeDtypeStruct((1, D), F32),
                   jax.ShapeDtypeStruct((1, D), F32)],
        name="conv_gate_bwd_a", compiler_params=_cparams("arbitrary"))(
            dy0, proj, proj, ya, cpre, lng, lnb)


def conv_gate_bwd_b(dc, proj, cw, D, comm=None):
    S = proj.shape[0]
    T = _conv_t(S)
    hb = T // CONV_HALO
    nt = S // T
    nlb = D // LANES
    half = T // 2

    def body(dc_ref, dn_ref, ap_ref, bp_ref, a_ref, b_ref, cw_ref, da_ref, db_ref, dcw_ref, dcb_ref,
             zbuf, dcbuf, zsh, dcsh, dcw8):
        i = pl.program_id(0)
        _fill_zbuf(zbuf, ap_ref, bp_ref, a_ref, b_ref, i)
        dcv = dc_ref[...]
        dcbuf[0:T, :] = dcv
        dcbuf[T:, :] = jnp.where(i == nt - 1, 0.0, dn_ref[...])

        @pl.when(i == 0)
        def _():
            dcw8[...] = jnp.zeros_like(dcw8)
            dcb_ref[...] = jnp.zeros_like(dcb_ref)

        dcb_ref[...] += jnp.sum(dcv, axis=0, keepdims=True)

        def lane_block(lb, carry):
            lanes = pl.ds(pl.multiple_of(lb * LANES, LANES), LANES)
            z_at = _shifted_windows(zbuf, zsh, lanes, T)
            dc_at = _shifted_windows(dcbuf, dcsh, lanes, T)
            for r0 in range(0, T, half):
                d0 = dcbuf[r0:r0 + half, lanes]
                dz = jnp.zeros((half, LANES), F32)
                for k in range(CONV_WIDTH):
                    dz = dz + cw_ref[k:k + 1, lanes] * dc_at(CONV_WIDTH - 1 - k, r0, half)
                    prod = d0 * z_at(CONV_HALO - CONV_WIDTH + 1 + k, r0, half)
                    dcw8[pl.ds(k * SUBLANES, SUBLANES), lanes] += jnp.sum(
                        prod.reshape(half // SUBLANES, SUBLANES, LANES), axis=0)
                av = a_ref[r0:r0 + half, lanes].astype(F32)
                sg = _sigmoid(b_ref[r0:r0 + half, lanes].astype(F32))
                da_ref[r0:r0 + half, lanes] = (dz * sg).astype(da_ref.dtype)
                db_ref[r0:r0 + half, lanes] = (dz * av * sg * (1.0 - sg)).astype(db_ref.dtype)
            return carry

        lax.fori_loop(0, nlb, lane_block, 0)

        @pl.when(i == nt - 1)
        def _():
            dcw_ref[...] = jnp.sum(dcw8[...].reshape(CONV_HALO, SUBLANES, D), axis=1)

    def cur(cidx):
        return pl.BlockSpec((T, D), lambda i: (i, cidx))

    def prev(cidx):
        return pl.BlockSpec((CONV_HALO, D), lambda i: (jnp.maximum(i * hb - 1, 0), cidx))

    row = pl.BlockSpec((T, D), lambda i: (i, 0))
    nxt = pl.BlockSpec((CONV_HALO, D), lambda i: (jnp.minimum((i + 1) * hb, nt * hb - 1), 0))
    return _call(
        body, name="conv_gate_bwd_b", grid=(nt,),
        in_specs=[row, nxt, prev(3), prev(4), cur(3), cur(4), pl.BlockSpec((CONV_HALO, D), lambda i: (0, 0))],
        out_specs=[row, row, pl.BlockSpec((CONV_HALO, D), lambda i: (0, 0)),
                   pl.BlockSpec((1, D), lambda i: (0, 0))],
        out_shape=[jax.ShapeDtypeStruct((S, D), BF16), jax.ShapeDtypeStruct((S, D), BF16),
                   jax.ShapeDtypeStruct((CONV_HALO, D), F32), jax.ShapeDtypeStruct((1, D), F32)],
        scratch_shapes=[pltpu.VMEM((T + CONV_HALO, D), F32), pltpu.VMEM((T + CONV_HALO, D), F32),
                        _shifted_scratch(T), _shifted_scratch(T), pltpu.VMEM((CONV_HALO * SUBLANES, D), F32)],
        args=[dc, dc, proj, proj, proj, proj, cw], sem=("arbitrary",), comm=comm)


def assemble_dproj0(dq, dkc, dkp, dvc, dvp, da, db, dgate, D):
    S = dq.shape[0]
    tq = _attn_tq(S)
    T = _pick(S, (256,))
    shift = tq // T
    nt = S // T

    def body(dq_ref, dkc_ref, dkp_ref, dvc_ref, dvp_ref, da_ref, db_ref, dg_ref, o_ref):
        i = pl.program_id(0)
        last = i + shift >= nt
        o_ref[:, 0:D] = dq_ref[...]
        dk = dkc_ref[...].astype(F32) + jnp.where(last, 0.0, dkp_ref[...].astype(F32))
        dv = dvc_ref[...].astype(F32) + jnp.where(last, 0.0, dvp_ref[...].astype(F32))
        o_ref[:, D:2 * D] = dk.astype(o_ref.dtype)
        o_ref[:, 2 * D:3 * D] = dv.astype(o_ref.dtype)
        o_ref[:, 3 * D:4 * D] = da_ref[...]
        o_ref[:, 4 * D:5 * D] = db_ref[...]
        o_ref[:, 5 * D:] = dg_ref[...]

    row = pl.BlockSpec((T, D), lambda i: (i, 0))
    nxt = pl.BlockSpec((T, D), lambda i: (jnp.minimum(i + shift, nt - 1), 0))
    return pl.pallas_call(
        body, grid=(nt,),
        in_specs=[row, row, nxt, row, nxt, row, row, pl.BlockSpec((T, 2 * D), lambda i: (i, 0))],
        out_specs=pl.BlockSpec((T, 7 * D), lambda i: (i, 0)),
        out_shape=jax.ShapeDtypeStruct((S, 7 * D), BF16),
        name="assemble_dproj0", compiler_params=_cparams("parallel"))(dq, dkc, dkp, dvc, dvp, da, db, dgate)


def _sgu_t(S):
    return _pick(S, (256, 128))


def _ws_masked(ws_ref, g):
    row = lax.broadcasted_iota(jnp.int32, (GMLP_CHUNK, GMLP_CHUNK), 0) // CHUNK
    col = lax.broadcasted_iota(jnp.int32, (GMLP_CHUNK, GMLP_CHUNK), 1) // CHUNK
    return jnp.where(row >= col, ws_ref[g], 0.0), row >= col


def sgu_fwd(proj, lng, lnb, ws, bst, MIX):
    S = proj.shape[0]
    T = _sgu_t(S)
    gw = MIX // N_GROUPS_C

    def body(u_ref, v_ref, g_ref, lng_ref, lnb_ref, ws_ref, bst_ref, y_ref):
        v = v_ref[...].astype(F32)
        mu = jnp.mean(v, axis=-1, keepdims=True)
        xc = v - mu
        rstd = lax.rsqrt(jnp.mean(xc * xc, axis=-1, keepdims=True) + EPS)
        for g in range(N_GROUPS_C):
            cols = slice(g * gw, (g + 1) * gw)
            wsm = _ws_masked(ws_ref, g)[0].astype(BF16)
            vn = (xc[:, cols] * rstd * lng_ref[:, cols] + lnb_ref[:, cols]).astype(BF16)
            for blk in range(T // GMLP_CHUNK):
                rows = slice(blk * GMLP_CHUNK, (blk + 1) * GMLP_CHUNK)
                sg = _dot(wsm, vn[rows], NN) + bst_ref[:, g:g + 1]
                gate = g_ref[rows, cols].astype(F32)
                y = u_ref[rows, cols].astype(F32) * sg * (gate * _sigmoid(gate))
                y_ref[rows, cols] = y.astype(y_ref.dtype)

    def part(cidx):
        return pl.BlockSpec((T, MIX), lambda i: (i, cidx))

    vec = pl.BlockSpec((1, MIX), lambda i: (0, 0))
    return pl.pallas_call(
        body, grid=(S // T,),
        in_specs=[part(0), part(1), part(2), vec, vec,
                  pl.BlockSpec((N_GROUPS_C, GMLP_CHUNK, GMLP_CHUNK), lambda i: (0, 0, 0)),
                  pl.BlockSpec((GMLP_CHUNK, N_GROUPS_C), lambda i: (0, 0))],
        out_specs=pl.BlockSpec((T, MIX), lambda i: (i, 0)),
        out_shape=jax.ShapeDtypeStruct((S, MIX), BF16),
        name="sgu_fwd", compiler_params=_cparams("parallel"))(proj, proj, proj, lng, lnb, ws, bst)


def sgu_bwd(dy1, proj, lng, lnb, ws, bst, MIX):
    S = proj.shape[0]
    T = _sgu_t(S)
    gw = MIX // N_GROUPS_C

    def body(dy_ref, u_ref, v_ref, g_ref, lng_ref, lnb_ref, ws_ref, bst_ref,
             dp_ref, dws_ref, dbst_ref, dlng_ref, dlnb_ref, dvn_buf):
        i = pl.program_id(0)

        @pl.when(i == 0)
        def _():
            dws_ref[...] = jnp.zeros_like(dws_ref)
            dbst_ref[...] = jnp.zeros_like(dbst_ref)
            dlng_ref[...] = jnp.zeros_like(dlng_ref)
            dlnb_ref[...] = jnp.zeros_like(dlnb_ref)

        v = v_ref[...].astype(F32)
        mu = jnp.mean(v, axis=-1, keepdims=True)
        xc = v - mu
        rstd = lax.rsqrt(jnp.mean(xc * xc, axis=-1, keepdims=True) + EPS)
        for g in range(N_GROUPS_C):
            cols = slice(g * gw, (g + 1) * gw)
            wsf, keep = _ws_masked(ws_ref, g)
            wsm = wsf.astype(BF16)
            vn = (xc[:, cols] * rstd * lng_ref[:, cols] + lnb_ref[:, cols]).astype(BF16)
            for blk in range(T // GMLP_CHUNK):
                rows = slice(blk * GMLP_CHUNK, (blk + 1) * GMLP_CHUNK)
                vnb = vn[rows]
                sg = _dot(wsm, vnb, NN) + bst_ref[:, g:g + 1]
                gate = g_ref[rows, cols].astype(F32)
                sig = _sigmoid(gate)
                sil = gate * sig
                u = u_ref[rows, cols].astype(F32)
                dy = dy_ref[rows, cols].astype(F32)
                dp_ref[rows, g * gw:(g + 1) * gw] = (dy * sg * sil).astype(dp_ref.dtype)
                dp_ref[rows, 2 * MIX + g * gw:2 * MIX + (g + 1) * gw] = (
                    dy * u * sg * (sig * (1.0 + gate * (1.0 - sig)))).astype(dp_ref.dtype)
                dsg = dy * u * sil
                dsgb = dsg.astype(BF16)
                dvn_buf[rows, cols] = _dot(wsm, dsgb, TN)
                dws_ref[g] += jnp.where(keep, _dot(dsgb, vnb, NT), 0.0)
                dbst_ref[:, g:g + 1] += jnp.sum(dsg, axis=-1, keepdims=True)
        dvn = dvn_buf[...]
        xhat = xc * rstd
        dxhat = dvn * lng_ref[...]
        dv = rstd * (dxhat - jnp.mean(dxhat, axis=-1, keepdims=True)
                     - xhat * jnp.mean(dxhat * xhat, axis=-1, keepdims=True))
        dp_ref[:, MIX:2 * MIX] = dv.astype(dp_ref.dtype)
        dlng_ref[...] += jnp.sum(dvn * xhat, axis=0, keepdims=True)
        dlnb_ref[...] += jnp.sum(dvn, axis=0, keepdims=True)

    def part(cidx):
        return pl.BlockSpec((T, MIX), lambda i: (i, cidx))

    vec = pl.BlockSpec((1, MIX), lambda i: (0, 0))
    wspec = pl.BlockSpec((N_GROUPS_C, GMLP_CHUNK, GMLP_CHUNK), lambda i: (0, 0, 0))
    bspec = pl.BlockSpec((GMLP_CHUNK, N_GROUPS_C), lambda i: (0, 0))
    return pl.pallas_call(
        body, grid=(S // T,),
        in_specs=[pl.BlockSpec((T, MIX), lambda i: (i, 0)), part(0), part(1), part(2), vec, vec, wspec, bspec],
        out_specs=[pl.BlockSpec((T, 3 * MIX), lambda i: (i, 0)), wspec, bspec, vec, vec],
        out_shape=[jax.ShapeDtypeStruct((S, 3 * MIX), BF16),
                   jax.ShapeDtypeStruct((N_GROUPS_C, GMLP_CHUNK, GMLP_CHUNK), F32),
                   jax.ShapeDtypeStruct((GMLP_CHUNK, N_GROUPS_C), F32),
                   jax.ShapeDtypeStruct((1, MIX), F32), jax.ShapeDtypeStruct((1, MIX), F32)],
        scratch_shapes=[pltpu.VMEM((T, MIX), F32)],
        name="sgu_bwd", compiler_params=_cparams("arbitrary"))(dy1, proj, proj, proj, lng, lnb, ws, bst)


def xattn_fwd(name, q, k, v):
    S, D = q.shape
    nm = k.shape[0]
    dh = D // N_HEADS_X
    tq = _pick(S, (512, 256))
    scale = dh ** -0.5

    def body(q_ref, k_ref, v_ref, o_ref, lse_ref):
        s = _dot(q_ref[...], k_ref[...], NT) * scale
        m = jnp.max(s, axis=-1, keepdims=True)
        p = jnp.exp(s - m)
        l = jnp.sum(p, axis=-1, keepdims=True)
        o_ref[...] = (_dot(p.astype(BF16), v_ref[...], NN) / l).astype(o_ref.dtype)
        lse_ref[...] = m + jnp.log(l)

    return pl.pallas_call(
        body, grid=(N_HEADS_X, S // tq),
        in_specs=[pl.BlockSpec((tq, dh), lambda h, i: (i, h)),
                  pl.BlockSpec((nm, dh), lambda h, i: (0, h)), pl.BlockSpec((nm, dh), lambda h, i: (0, h))],
        out_specs=[pl.BlockSpec((tq, dh), lambda h, i: (i, h)),
                   pl.BlockSpec((None, tq, 1), lambda h, i: (h, i, 0))],
        out_shape=[jax.ShapeDtypeStruct((S, D), BF16), jax.ShapeDtypeStruct((N_HEADS_X, S, 1), F32)],
        name=name, compiler_params=_cparams("parallel", "parallel"))(q, k, v)


def xattn_bwd(name, q, k, v, o, do, lse):
    S, D = q.shape
    nm = k.shape[0]
    dh = D // N_HEADS_X
    tq = _pick(S, (512, 256))
    scale = dh ** -0.5

    def body(q_ref, k_ref, v_ref, o_ref, do_ref, lse_ref, dq_ref, dk_ref, dv_ref):
        i = pl.program_id(1)
        q_v = q_ref[...]
        k_v = k_ref[...]
        do_v = do_ref[...]
        p = jnp.exp(_dot(q_v, k_v, NT) * scale - lse_ref[...])
        delta = jnp.sum(do_v.astype(F32) * o_ref[...].astype(F32), axis=-1, keepdims=True)
        dv = _dot(p.astype(BF16), do_v, TN)
        ds = (p * (_dot(do_v, v_ref[...], NT) - delta)).astype(BF16)
        dq_ref[...] = (_dot(ds, k_v, NN) * scale).astype(dq_ref.dtype)
        dk = _dot(ds, q_v, TN) * scale

        @pl.when(i == 0)
        def _():
            dk_ref[...] = dk
            dv_ref[...] = dv

        @pl.when(i > 0)
        def _():
            dk_ref[...] += dk
            dv_ref[...] += dv

    qs = pl.BlockSpec((tq, dh), lambda h, i: (i, h))
    ks = pl.BlockSpec((nm, dh), lambda h, i: (0, h))
    return pl.pallas_call(
        body, grid=(N_HEADS_X, S // tq),
        in_specs=[qs, ks, ks, qs, qs, pl.BlockSpec((None, tq, 1), lambda h, i: (h, i, 0))],
        out_specs=[qs, ks, ks],
        out_shape=[jax.ShapeDtypeStruct((S, D), BF16), jax.ShapeDtypeStruct((nm, D), F32),
                   jax.ShapeDtypeStruct((nm, D), F32)],
        name=name, compiler_params=_cparams("parallel", "arbitrary"))(q, k, v, o, do, lse)


def adamw(name, w, g, m, v):
    R, C = w.shape
    tr = _pick(R, tuple(t for t in (512, 256, 128, 64, 32, 16, 8) if t * C * 4 <= (1 << 20)) or (8,))
    c1 = 1.0 - ADAM_B1 ** ADAM_STEP
    c2 = 1.0 - ADAM_B2 ** ADAM_STEP

    def body(w_ref, g_ref, m_ref, v_ref, d_ref, nm_ref, nv_ref):
        gv = g_ref[...]
        nm = ADAM_B1 * m_ref[...] + (1.0 - ADAM_B1) * gv
        nv = ADAM_B2 * v_ref[...] + (1.0 - ADAM_B2) * (gv * gv)
        d_ref[...] = -ADAM_LR * ((nm / c1) / (jnp.sqrt(nv / c2) + ADAM_EPS) + ADAM_WD * w_ref[...])
        nm_ref[...] = nm
        nv_ref[...] = nv

    blk = pl.BlockSpec((tr, C), lambda i: (i, 0))
    sd = jax.ShapeDtypeStruct((R, C), F32)
    return pl.pallas_call(body, grid=(R // tr,), in_specs=[blk] * 4, out_specs=[blk] * 3,
                          out_shape=[sd, sd, sd], name=name, compiler_params=_cparams("parallel"))(w, g, m, v)


def adamw_many(name, tensors):
    n = len(tensors)
    c1 = 1.0 - ADAM_B1 ** ADAM_STEP
    c2 = 1.0 - ADAM_B2 ** ADAM_STEP

    def as2d(a):
        return a.reshape((1, -1) if a.ndim == 1 else (-1, a.shape[-1])).astype(F32)

    flat = [as2d(a) for t in tensors for a in t]

    def body(*refs):
        ins, outs = refs[:4 * n], refs[4 * n:]
        for t in range(n):
            w_ref, g_ref, m_ref, v_ref = ins[4 * t:4 * t + 4]
            d_ref, nm_ref, nv_ref = outs[3 * t:3 * t + 3]
            gv = g_ref[...]
            nm = ADAM_B1 * m_ref[...] + (1.0 - ADAM_B1) * gv
            nv = ADAM_B2 * v_ref[...] + (1.0 - ADAM_B2) * (gv * gv)
            d_ref[...] = -ADAM_LR * ((nm / c1) / (jnp.sqrt(nv / c2) + ADAM_EPS) + ADAM_WD * w_ref[...])
            nm_ref[...] = nm
            nv_ref[...] = nv

    vm = pl.BlockSpec(memory_space=pltpu.VMEM)
    shapes = [jax.ShapeDtypeStruct(flat[4 * t].shape, F32) for t in range(n) for _ in range(3)]
    res = pl.pallas_call(body, in_specs=[vm] * (4 * n), out_specs=[vm] * (3 * n), out_shape=shapes, name=name,
                         compiler_params=pltpu.CompilerParams(vmem_limit_bytes=V7X_VMEM_LIMIT))(*flat)
    return [tuple(r.reshape(tensors[t][0].shape) for r in res[3 * t:3 * t + 3]) for t in range(n)]


def add_halves(name, g4, recv, cidx):
    _, R, C = g4.shape
    rh = R // 2
    tr = _pick(rh, (256, 128, 64, 32, 16))
    nrb = rh // tr

    def body(c_ref, a_ref, b_ref, o_ref):
        o_ref[...] = (a_ref[...].astype(F32) + b_ref[...].astype(F32)).astype(o_ref.dtype)

    grid_spec = pltpu.PrefetchScalarGridSpec(
        num_scalar_prefetch=1, grid=(4, nrb),
        in_specs=[pl.BlockSpec((None, tr, C), lambda j, r, c_ref: (j, c_ref[0] * nrb + r, 0)),
                  pl.BlockSpec((None, tr, C), lambda j, r, c_ref: (j, r, 0))],
        out_specs=pl.BlockSpec((None, tr, C), lambda j, r, c_ref: (j, r, 0)))
    return pl.pallas_call(body, grid_spec=grid_spec, out_shape=jax.ShapeDtypeStruct((4, rh, C), BF16),
                          name=name, compiler_params=_cparams("parallel", "parallel"))(cidx, g4, recv)


def sum_chips(name, own, recv, place):
    _, rh, C = own.shape
    tr = _pick(rh, (256, 128, 64, 32, 16))
    nrb = rh // tr

    def body(s_ref, own_ref, recv_ref, o_ref):
        acc = own_ref[...].astype(F32)
        for k in range(N_CHIPS - 1):
            acc = acc + recv_ref[k].astype(F32)
        o_ref[...] = acc

    grid_spec = pltpu.PrefetchScalarGridSpec(
        num_scalar_prefetch=1, grid=(nrb,),
        in_specs=[pl.BlockSpec((None, tr, C), lambda r, s: (s[0], r, 0)),
                  pl.BlockSpec((N_CHIPS - 1, tr, C), lambda r, s: (0, r, 0))],
        out_specs=pl.BlockSpec((tr, C), lambda r, s: (s[1] * nrb + r, 0)))
    return pl.pallas_call(body, grid_spec=grid_spec, out_shape=jax.ShapeDtypeStruct((2 * rh, C), F32),
                          name=name, compiler_params=_cparams("parallel"))(place, own, recv)


def cast_into_slot(name, w, place):
    R, C = w.shape
    tr = _pick(R, (256, 128, 64, 32, 16))

    def body(s_ref, w_ref, o_ref):
        o_ref[...] = w_ref[...].astype(o_ref.dtype)

    grid_spec = pltpu.PrefetchScalarGridSpec(
        num_scalar_prefetch=1, grid=(R // tr,),
        in_specs=[pl.BlockSpec((tr, C), lambda r, s: (r, 0))],
        out_specs=pl.BlockSpec((None, tr, C), lambda r, s: (s[0], r, 0)))
    return pl.pallas_call(body, grid_spec=grid_spec, out_shape=jax.ShapeDtypeStruct((N_CHIPS, R, C), BF16),
                          name=name, compiler_params=_cparams("parallel"))(place, w)


def _place():
    return lax.axis_index("x"), lax.axis_index("y"), lax.axis_index("c")


_CHIP_FLIPS = ((1, 0), (0, 1), (1, 1))


def _flip(v, bit):
    return 1 - v if bit else v


HBM_SPEC = pl.BlockSpec(memory_space=pl.ANY)


def exchange_small(name, buf, reduce):
    R = buf.shape[0]

    def body(x_ref, *refs):
        if reduce:
            sum_ref, all_ref, send_sems, recv_sems, local_sem = refs
        else:
            all_ref, send_sems, recv_sems, local_sem = refs
        x, y, c = _place()
        me = 4 * x + 2 * y + c
        mine = pltpu.make_async_copy(x_ref, all_ref.at[me], local_sem)
        mine.start()
        sends = []
        for k in range(1, N_DEV):
            peer = (_flip(x, k & 4), _flip(y, k & 2), _flip(c, k & 1))
            cp = pltpu.make_async_remote_copy(src_ref=x_ref, dst_ref=all_ref.at[me], send_sem=send_sems.at[k - 1],
                                              recv_sem=recv_sems.at[k - 1], device_id=peer, device_id_type=MESH)
            cp.start()
            sends.append(cp)
        for k in range(1, N_DEV):
            peer = (_flip(x, k & 4), _flip(y, k & 2), _flip(c, k & 1))
            src = 4 * peer[0] + 2 * peer[1] + peer[2]
            pltpu.make_async_remote_copy(src_ref=x_ref, dst_ref=all_ref.at[src], send_sem=send_sems.at[k - 1],
                                         recv_sem=recv_sems.at[k - 1], device_id=peer,
                                         device_id_type=MESH).wait_recv()
        for cp in sends:
            cp.wait_send()
        mine.wait()
        if reduce:
            acc = all_ref[0]
            for d in range(1, N_DEV):
                acc = acc + all_ref[d]
            sum_ref[...] = acc

    vm = pl.BlockSpec(memory_space=pltpu.VMEM)
    sems = [pltpu.SemaphoreType.DMA((N_DEV - 1,)), pltpu.SemaphoreType.DMA((N_DEV - 1,)), pltpu.SemaphoreType.DMA]
    if reduce:
        return pl.pallas_call(
            body, in_specs=[vm], out_specs=vm, out_shape=jax.ShapeDtypeStruct((R, LANES), F32),
            scratch_shapes=[pltpu.VMEM((N_DEV, R, LANES), F32)] + sems, name=name,
            compiler_params=pltpu.CompilerParams(vmem_limit_bytes=V7X_VMEM_LIMIT))(buf)
    return pl.pallas_call(
        body, in_specs=[vm], out_specs=vm, out_shape=jax.ShapeDtypeStruct((N_DEV, R, LANES), F32),
        scratch_shapes=sems, name=name,
        compiler_params=pltpu.CompilerParams(vmem_limit_bytes=V7X_VMEM_LIMIT))(buf)


def exchange_job(buf):
    R = buf.shape[0]

    def copies(x_ref, all_ref, send_sems, recv_sems):
        x, y, c = _place()
        me = 4 * x + 2 * y + c
        sends, arrivals = [], []
        for k in range(1, N_DEV):
            peer = (_flip(x, k & 4), _flip(y, k & 2), _flip(c, k & 1))
            src = 4 * peer[0] + 2 * peer[1] + peer[2]
            sends.append(pltpu.make_async_remote_copy(
                src_ref=x_ref, dst_ref=all_ref.at[me], send_sem=send_sems.at[k - 1], recv_sem=recv_sems.at[k - 1],
                device_id=peer, device_id_type=MESH))
            arrivals.append(pltpu.make_async_remote_copy(
                src_ref=x_ref, dst_ref=all_ref.at[src], send_sem=send_sems.at[k - 1], recv_sem=recv_sems.at[k - 1],
                device_id=peer, device_id_type=MESH))
        return sends, arrivals

    def start(ins, outs, sems):
        for cp in copies(ins[0], outs[0], *sems)[0]:
            cp.start()

    def finish(ins, outs, sems):
        sends, arrivals = copies(ins[0], outs[0], *sems)
        for cp in arrivals:
            cp.wait_recv()
        for cp in sends:
            cp.wait_send()

    return _Comm([buf], [jax.ShapeDtypeStruct((N_DEV, R, LANES), F32)], {},
                 [pltpu.SemaphoreType.DMA((N_DEV - 1,)), pltpu.SemaphoreType.DMA((N_DEV - 1,))], start, finish)


def sum_devices(name, slots):
    _, R, _ = slots.shape
    tr = _pick(R, (512, 256, 128, 64, 32, 16, 8))

    def body(s_ref, o_ref):
        acc = s_ref[0]
        for d in range(1, N_DEV):
            acc = acc + s_ref[d]
        o_ref[...] = acc

    return pl.pallas_call(body, grid=(R // tr,),
                          in_specs=[pl.BlockSpec((N_DEV, tr, LANES), lambda r: (0, r, 0))],
                          out_specs=pl.BlockSpec((tr, LANES), lambda r: (r, 0)),
                          out_shape=jax.ShapeDtypeStruct((R, LANES), F32), name=name,
                          compiler_params=_cparams("parallel"))(slots)


def gather_job(slots, relay_frac=0.75, flips=None):
    n = len(slots)
    flips = flips or [tuple(range(len(_CHIP_FLIPS)))] * n

    def copies(o_refs, send_sems, recv_sems):
        x, y, c = _place()
        me = 2 * x + y
        sib = (x, y, 1 - c)
        chips = [(_flip(x, fx), _flip(y, fy)) for fx, fy in _CHIP_FLIPS]
        ici, fwd, from_sib = [], [], []
        for t in range(n):
            rh = o_refs[t].shape[1] // 2
            mine, theirs = pl.ds(c * rh, rh), pl.ds((1 - c) * rh, rh)
            for k, (px, py) in enumerate(chips):
                if k not in flips[t]:
                    continue
                own = o_refs[t].at[me, mine]
                ici.append(pltpu.make_async_remote_copy(
                    src_ref=own, dst_ref=own, send_sem=send_sems.at[t, k], recv_sem=recv_sems.at[t, k],
                    device_id=(px, py, c), device_id_type=MESH))
                landed = o_refs[t].at[2 * px + py, mine]
                arrival = pltpu.make_async_remote_copy(
                    src_ref=landed, dst_ref=landed, send_sem=send_sems.at[t, k], recv_sem=recv_sems.at[t, k],
                    device_id=(px, py, c), device_id_type=MESH)
                fwd.append((arrival, pltpu.make_async_remote_copy(
                    src_ref=landed, dst_ref=landed, send_sem=send_sems.at[t, 3 + k],
                    recv_sem=recv_sems.at[t, 3 + k], device_id=sib, device_id_type=MESH)))
                passed = o_refs[t].at[2 * px + py, theirs]
                from_sib.append(pltpu.make_async_remote_copy(
                    src_ref=passed, dst_ref=passed, send_sem=send_sems.at[t, 3 + k],
                    recv_sem=recv_sems.at[t, 3 + k], device_id=sib, device_id_type=MESH))
        return ici, fwd, from_sib

    def start(ins, o_refs, sems):
        for cp in copies(o_refs, *sems)[0]:
            cp.start()

    def relay(ins, o_refs, sems):
        for arrival, forward in copies(o_refs, *sems)[1]:
            arrival.wait_recv()
            forward.start()

    def finish(ins, o_refs, sems):
        ici, fwd, from_sib = copies(o_refs, *sems)
        for cp in from_sib:
            cp.wait_recv()
        for cp in ici:
            cp.wait_send()
        for _, forward in fwd:
            forward.wait_send()

    return _Comm(slots, [jax.ShapeDtypeStruct(s.shape, s.dtype) for s in slots], {t: t for t in range(n)},
                 [pltpu.SemaphoreType.DMA((n, 6)), pltpu.SemaphoreType.DMA((n, 6))], start, finish, relay,
                 relay_frac)


def sibling_halves_job(grads):
    n = len(grads)

    def copies(g_refs, o_refs, send_sems, recv_sems):
        x, y, c = _place()
        out = []
        for t in range(n):
            rh = g_refs[t].shape[1] // 2
            out.append(pltpu.make_async_remote_copy(
                src_ref=g_refs[t].at[:, pl.ds((1 - c) * rh, rh), :], dst_ref=o_refs[t],
                send_sem=send_sems.at[t], recv_sem=recv_sems.at[t], device_id=(x, y, 1 - c),
                device_id_type=MESH))
        return out

    def start(g_refs, o_refs, sems):
        for cp in copies(g_refs, o_refs, *sems):
            cp.start()

    def finish(g_refs, o_refs, sems):
        cps = copies(g_refs, o_refs, *sems)
        for cp in cps:
            cp.wait_recv()
        for cp in cps:
            cp.wait_send()

    return _Comm(grads, [jax.ShapeDtypeStruct((4, g.shape[1] // 2, g.shape[2]), g.dtype) for g in grads], {},
                 [pltpu.SemaphoreType.DMA((n,)), pltpu.SemaphoreType.DMA((n,))], start, finish)


def scatter_job(parts):
    n = len(parts)

    def copies(p_refs, o_refs, send_sems, recv_sems):
        x, y, c = _place()
        out = []
        for t in range(n):
            for k, (fx, fy) in enumerate(_CHIP_FLIPS):
                px, py = _flip(x, fx), _flip(y, fy)
                out.append(pltpu.make_async_remote_copy(
                    src_ref=p_refs[t].at[2 * px + py], dst_ref=o_refs[t].at[k],
                    send_sem=send_sems.at[t, k], recv_sem=recv_sems.at[t, k],
                    device_id=(px, py, c), device_id_type=MESH))
        return out

    def start(p_refs, o_refs, sems):
        for cp in copies(p_refs, o_refs, *sems):
            cp.start()

    def finish(p_refs, o_refs, sems):
        cps = copies(p_refs, o_refs, *sems)
        for cp in cps:
            cp.wait_recv()
        for cp in cps:
            cp.wait_send()

    return _Comm(parts, [jax.ShapeDtypeStruct((N_CHIPS - 1,) + p.shape[1:], p.dtype) for p in parts], {},
                 [pltpu.SemaphoreType.DMA((n, 3)), pltpu.SemaphoreType.DMA((n, 3))], start, finish)


def share_halves_job(halves):
    n = len(halves)

    def copies(o_refs, send_sems, recv_sems):
        x, y, c = _place()
        sends, arrivals = [], []
        for t in range(n):
            rh = o_refs[t].shape[0] // 2
            mine = o_refs[t].at[pl.ds(c * rh, rh)]
            theirs = o_refs[t].at[pl.ds((1 - c) * rh, rh)]
            sends.append(pltpu.make_async_remote_copy(
                src_ref=mine, dst_ref=mine, send_sem=send_sems.at[t], recv_sem=recv_sems.at[t],
                device_id=(x, y, 1 - c), device_id_type=MESH))
            arrivals.append(pltpu.make_async_remote_copy(
                src_ref=theirs, dst_ref=theirs, send_sem=send_sems.at[t], recv_sem=recv_sems.at[t],
                device_id=(x, y, 1 - c), device_id_type=MESH))
        return sends, arrivals

    def start(ins, o_refs, sems):
        for cp in copies(o_refs, *sems)[0]:
            cp.start()

    def finish(ins, o_refs, sems):
        sends, arrivals = copies(o_refs, *sems)
        for cp in arrivals:
            cp.wait_recv()
        for cp in sends:
            cp.wait_send()

    return _Comm(halves, [jax.ShapeDtypeStruct(h.shape, h.dtype) for h in halves], {t: t for t in range(n)},
                 [pltpu.SemaphoreType.DMA((n,)), pltpu.SemaphoreType.DMA((n,))], start, finish)


def _pack(arrs, row_multiple=SUBLANES):
    flat, total = [], 0
    for a in arrs:
        v = a.reshape(-1).astype(F32)
        pad = (-v.shape[0]) % (SUBLANES * LANES)
        flat.append(jnp.pad(v, (0, pad)))
        total += v.shape[0] + pad
    tail = (-total) % (row_multiple * LANES)
    if tail:
        flat.append(jnp.zeros((tail,), F32))
    return jnp.concatenate(flat).reshape(-1, LANES)


def _unpack(buf, shapes):
    out, off = [], 0
    flat = buf.reshape(-1)
    for s in shapes:
        n = int(np.prod(s))
        out.append(flat[off:off + n].reshape(s))
        off += n + ((-n) % (8 * LANES))
    return out


def _xattn_layer_fwd(tag, h, mem, gx, gmem, w):
    hx = rms_fwd(f"rms_x{tag}", h, gx)
    memn = rms_fwd(f"rms_mem{tag}", mem, gmem)
    q = mm_nn(f"xq{tag}", hx, w["q"], BF16)
    k = mm_nn(f"xk{tag}", memn, w["k"], BF16)
    v = mm_nn(f"xv{tag}", memn, w["v"], BF16)
    o, lse = xattn_fwd(f"xattn_fwd{tag}", q, k, v)
    h_out = mm_nn(f"xo{tag}", o, w["o"], F32, res=h)
    return h_out, dict(hx=hx, memn=memn, q=q, k=k, v=v, o=o, lse=lse)


def _xattn_layer_bwd(tag, dh_out, dh_out_b, h_in, mem, gx, gmem, w, sv):
    do = mm_nt(f"d_xo{tag}", dh_out_b, w["o"], BF16)
    dwo = mm_tn(f"dw_xo{tag}", sv["o"], dh_out_b)
    dq, dk, dv = xattn_bwd(f"xattn_bwd{tag}", sv["q"], sv["k"], sv["v"], sv["o"], do, sv["lse"])
    dwq = mm_tn(f"dw_xq{tag}", sv["hx"], dq)
    dhx = mm_nt(f"d_xq{tag}", dq, w["q"], F32)
    dwk = mm_tn(f"dw_xk{tag}", sv["memn"], dk)
    dwv = mm_tn(f"dw_xv{tag}", sv["memn"], dv)
    dmk = mm_nt(f"d_xk{tag}", dk, w["k"], F32)
    dmv = mm_nt(f"d_xv{tag}", dv, w["v"], F32)
    dh_in, dh_in_b, dgx = rms_bwd(f"rms_x_bwd{tag}", h_in, gx, [dhx], dh_out)
    _, _, dgmem = rms_bwd(f"rms_mem_bwd{tag}", mem, gmem, [dmk, dmv], None)
    return dh_in, dh_in_b, dgx, dgmem, dict(q=dwq, k=dwk, v=dwv, o=dwo)


def kernel(x, mem, norm_mix_g, norm_x_g, norm_mem_g, final_norm_g, w_in_ab, rel_bias, conv_w, conv_b, conv_ln_g, conv_ln_b, w_out_ab, w_in_c, sgu_ln_g, sgu_ln_b, w_s, b_s, w_out_c, w_xq, w_xk, w_xv, w_xo, loss_target, m_norm_mix_g, m_norm_x_g, m_norm_mem_g, m_final_norm_g, m_w_in_ab, m_rel_bias, m_conv_w, m_conv_b, m_conv_ln_g, m_conv_ln_b, m_w_out_ab, m_w_in_c, m_sgu_ln_g, m_sgu_ln_b, m_w_s, m_b_s, m_w_out_c, m_w_xq, m_w_xk, m_w_xv, m_w_xo, v_norm_mix_g, v_norm_x_g, v_norm_mem_g, v_final_norm_g, v_w_in_ab, v_rel_bias, v_conv_w, v_conv_b, v_conv_ln_g, v_conv_ln_b, v_w_out_ab, v_w_in_c, v_sgu_ln_g, v_sgu_ln_b, v_w_s, v_b_s, v_w_out_c, v_w_xq, v_w_xk, v_w_xv, v_w_xo):
    S, D = x.shape[1], x.shape[2]
    MIX = 2 * D
    xs, mems, tgt = x[0], mem[0], loss_target[0]
    cx, cy, cc = _place()
    chip = 2 * cx + cy
    cidx = jnp.reshape(cc, (1,)).astype(jnp.int32)
    place = jnp.stack([chip, cc]).astype(jnp.int32)

    ro, rq = MIX // 4, D // 4
    row_sharded = [("out_ab", w_out_ab[0]), ("out_c", w_out_c[0])]
    for layer in range(2):
        for nm_, w in (("q", w_xq), ("k", w_xk), ("v", w_xv), ("o", w_xo)):
            row_sharded.append((f"x{nm_}{layer}", w[layer]))
    slots = {"in_ab": cast_into_slot("cast_in_ab", w_in_ab[0], place),
             "in_c": cast_into_slot("cast_in_c", w_in_c[0], place)}
    slots.update({nm_: cast_into_slot("cast_" + nm_, w, place) for nm_, w in row_sharded})

    small_sh = [conv_w[0], sgu_ln_g[0], sgu_ln_b[0]]
    gathered = exchange_small("gather_small", _pack(small_sh), reduce=False)
    per_chip = [_unpack(gathered[2 * j], [a.shape for a in small_sh]) for j in range(N_CHIPS)]
    conv_w_full = jnp.concatenate([p[0] for p in per_chip], axis=1)
    sgu_g_full = jnp.concatenate([p[1] for p in per_chip], axis=0).reshape(1, MIX)
    sgu_b_full = jnp.concatenate([p[2] for p in per_chip], axis=0).reshape(1, MIX)
    cw_pad = jnp.pad(conv_w_full, ((0, CONV_HALO - CONV_WIDTH), (0, 0)))
    cb = conv_b.reshape(1, D)
    clg, clb = conv_ln_g.reshape(1, D), conv_ln_b.reshape(1, D)
    ws = w_s[0]
    bst = jnp.transpose(b_s[0])
    tq = _attn_tq(S)
    bm = band_bias_table(rel_bias[0], tq)

    hn0 = rms_fwd("rms_mix0", xs, norm_mix_g[0])
    near, far, every = (0, 1), (2,), (0, 1, 2)
    proj0, (wab4,) = proj_cols_own("proj_ab_own", hn0, w_in_ab[0], place,
                                   comm=gather_job([slots["in_ab"]], relay_frac=1.0, flips=[near]))
    proj0, (wab4, w_out_ab4) = proj_cols_rest(
        "proj_ab_near", hn0, wab4, proj0, place, (2, 1),
        comm=gather_job([wab4, slots["out_ab"]], flips=[far, every]))
    proj0, got_qk = proj_cols_rest("proj_ab_far", hn0, wab4, proj0, place, (3,),
                                   comm=gather_job([slots["xq0"], slots["xk0"]]))
    (ya, lse_a), (wc4,) = attn_fwd(proj0, bm, D, comm=gather_job([slots["in_c"]]))
    (y0, cpre), got_b = conv_gate_fwd(
        proj0, ya, cw_pad, cb, clg, clb, D,
        comm=gather_job([slots["xv0"], slots["xo0"], slots["out_c"], slots["xq1"]]))
    h1, got_c = mm_nn("out_ab", y0, w_out_ab4.reshape(-1, D), F32, res=xs,
                      comm=gather_job([slots["xk1"], slots["xv1"], slots["xo1"]]))
    got = dict(zip(["xq0", "xk0", "xv0", "xo0", "out_c", "xq1", "xk1", "xv1", "xo1"], got_qk + got_b + got_c))
    wrow = {n: g.reshape(-1, g.shape[2]) for n, g in got.items()}
    wrow["out_ab"] = w_out_ab4.reshape(-1, D)
    wx = [{k: wrow[f"x{k}{layer}"] for k in "qkvo"} for layer in range(2)]
    h2, sx0 = _xattn_layer_fwd("0", h1, mems, norm_x_g[0], norm_mem_g[0], wx[0])
    hn1 = rms_fwd("rms_mix1", h2, norm_mix_g[1])
    proj1 = mm_nn_cols("proj_c", hn1, wc4, BF16)
    y1 = sgu_fwd(proj1, sgu_g_full, sgu_b_full, ws, bst, MIX)
    h3 = mm_nn("out_c", y1, wrow["out_c"], F32, res=h2)
    h4, sx1 = _xattn_layer_fwd("1", h3, mems, norm_x_g[1], norm_mem_g[1], wx[1])
    loss_row, dg_final, dh4, dh4b = loss_head("loss_head", h4, final_norm_g, tgt)

    dh3, dh3b, dgx1, dgmem1, dwx1 = _xattn_layer_bwd("1", dh4, dh4b, h3, mems, norm_x_g[1], norm_mem_g[1], wx[1], sx1)
    def stack_rows(dw_out, dwx):
        return jnp.concatenate([g.reshape(N_CHIPS, -1, g.shape[1]) for g in [dw_out] + [dwx[k] for k in "qkvo"]],
                               axis=1)

    dy1 = mm_nt("d_out_c", dh3b, wrow["out_c"], BF16)
    dw_out_c = mm_tn("dw_out_c", y1, dh3b)
    dproj1, dws, dbst, dsgu_g, dsgu_b = sgu_bwd(dy1, proj1, sgu_g_full, sgu_b_full, ws, bst, MIX)
    grp1 = stack_rows(dw_out_c, dwx1)
    dw_in_c, (sib1,) = mm_tn_cols("dw_in_c", hn1, dproj1, comm=sibling_halves_job([grp1]))
    part1 = add_halves("add_halves1", grp1, sib1, cidx)
    dhn1, (recv1, sib2) = mm_nt_cols("d_proj_c", dproj1, wc4, F32,
                                     comm=_join(scatter_job([part1]), sibling_halves_job([dw_in_c])))
    part2 = add_halves("add_halves2", dw_in_c, sib2, cidx)
    dh2, dh2b, dgmix1 = rms_bwd("rms_mix1_bwd", h2, norm_mix_g[1], [dhn1], dh3)
    dh1, dh1b, dgx0, dgmem0, dwx0 = _xattn_layer_bwd("0", dh2, dh2b, h1, mems, norm_x_g[0], norm_mem_g[0], wx[0], sx0)
    dy0 = mm_nt("d_out_ab", dh1b, wrow["out_ab"], BF16)
    dw_out_ab = mm_tn("dw_out_ab", y0, dh1b)
    grp3 = stack_rows(dw_out_ab, dwx0)
    dya, dgate, dc, dclg, dclb = conv_gate_bwd_a(dy0, proj0, ya, cpre, clg, clb, D)
    (da, db, dcw, dcb), (recv2, sib3) = conv_gate_bwd_b(
        dc, proj0, cw_pad, D, comm=_join(scatter_job([part2]), sibling_halves_job([grp3])))
    part3 = add_halves("add_halves3", grp3, sib3, cidx)
    (dq, dkc, dkp, dvc, dvp, ds_sum), (recv3,) = attn_bwd(proj0, ya, dya, lse_a, bm, D, comm=scatter_job([part3]))
    drel = rel_bias_grad(ds_sum)
    dproj0 = assemble_dproj0(dq, dkc, dkp, dvc, dvp, da, db, dgate, D)
    dw_in_ab = mm_tn_cols("dw_in_ab", hn0, dproj0)
    (sib4,) = run_comm("sibling_halves4", sibling_halves_job([dw_in_ab]))
    part4 = add_halves("add_halves4", dw_in_ab, sib4, cidx)
    halves = [sum_chips(f"sum_chips{t + 1}", p, r, place)
              for t, (p, r) in enumerate(((part1, recv1), (part2, recv2), (part3, recv3)))]
    small_early = [
        jnp.concatenate([dgx0, dgx1], axis=0), jnp.concatenate([dgmem0, dgmem1], axis=0), dg_final.reshape(D),
        drel[None], dcb, dclg, dclb, dws[None], jnp.transpose(dbst)[None],
        dcw[:CONV_WIDTH][None], dsgu_g, dsgu_b]
    early = _pack(small_early, row_multiple=512)
    dhn0, (recv4, small_slots, g_r1, g_c, g_r0) = mm_nt_cols(
        "d_proj_ab", dproj0, wab4, F32,
        comm=_join(scatter_job([part4]), exchange_job(early), share_halves_job(halves)))
    dx, _, dgmix0 = rms_bwd("rms_mix0_bwd", xs, norm_mix_g[0], [dhn0], dh1)
    (g_ab,) = run_comm("share_reduced_half4", share_halves_job([sum_chips("sum_chips4", part4, recv4, place)]))

    me = 4 * cx + 2 * cy + cc
    small_slots = lax.dynamic_update_slice(small_slots, early[None], (me, 0, 0))
    summed = _unpack(sum_devices("sum_small", small_slots), [a.shape for a in small_early])
    (g_norm_x, g_norm_mem, g_final, g_rel, g_conv_b, g_clg, g_clb, g_ws, g_bs,
     g_conv_w_full, g_sgu_g_full, g_sgu_b_full) = summed
    dgmix = jnp.concatenate([dgmix0, dgmix1], axis=0)
    (g_norm_mix,) = _unpack(exchange_small("reduce_late", _pack([dgmix]), reduce=True), [dgmix.shape])
    cws = conv_w.shape[2]
    g_conv_w = lax.dynamic_slice_in_dim(g_conv_w_full, chip * cws, cws, axis=2)
    sgs = sgu_ln_g.shape[1]
    g_sgu_g = lax.dynamic_slice_in_dim(g_sgu_g_full, chip * sgs, sgs, axis=1)
    g_sgu_b = lax.dynamic_slice_in_dim(g_sgu_b_full, chip * sgs, sgs, axis=1)

    loss = lax.psum(loss_row[0, 0], ("x", "y", "c"))

    g_rows = {"w_out_ab": g_r0[0:ro][None], "w_out_c": g_r1[0:ro][None]}
    for i, nm_ in enumerate("qkvo"):
        lo = ro + i * rq
        g_rows["w_x" + nm_] = jnp.stack([g_r0[lo:lo + rq], g_r1[lo:lo + rq]])
    grads = dict(
        norm_mix_g=g_norm_mix, norm_x_g=g_norm_x, norm_mem_g=g_norm_mem, final_norm_g=g_final,
        w_in_ab=g_ab[None], rel_bias=g_rel, conv_w=g_conv_w, conv_b=g_conv_b, conv_ln_g=g_clg, conv_ln_b=g_clb,
        w_out_ab=g_rows["w_out_ab"], w_in_c=g_c[None], sgu_ln_g=g_sgu_g, sgu_ln_b=g_sgu_b, w_s=g_ws, b_s=g_bs,
        w_out_c=g_rows["w_out_c"], w_xq=g_rows["w_xq"], w_xk=g_rows["w_xk"], w_xv=g_rows["w_xv"],
        w_xo=g_rows["w_xo"])
    weights = dict(
        norm_mix_g=(norm_mix_g, m_norm_mix_g, v_norm_mix_g), norm_x_g=(norm_x_g, m_norm_x_g, v_norm_x_g),
        norm_mem_g=(norm_mem_g, m_norm_mem_g, v_norm_mem_g), final_norm_g=(final_norm_g, m_final_norm_g, v_final_norm_g),
        w_in_ab=(w_in_ab, m_w_in_ab, v_w_in_ab), rel_bias=(rel_bias, m_rel_bias, v_rel_bias),
        conv_w=(conv_w, m_conv_w, v_conv_w), conv_b=(conv_b, m_conv_b, v_conv_b),
        conv_ln_g=(conv_ln_g, m_conv_ln_g, v_conv_ln_g), conv_ln_b=(conv_ln_b, m_conv_ln_b, v_conv_ln_b),
        w_out_ab=(w_out_ab, m_w_out_ab, v_w_out_ab), w_in_c=(w_in_c, m_w_in_c, v_w_in_c),
        sgu_ln_g=(sgu_ln_g, m_sgu_ln_g, v_sgu_ln_g), sgu_ln_b=(sgu_ln_b, m_sgu_ln_b, v_sgu_ln_b),
        w_s=(w_s, m_w_s, v_w_s), b_s=(b_s, m_b_s, v_b_s), w_out_c=(w_out_c, m_w_out_c, v_w_out_c),
        w_xq=(w_xq, m_w_xq, v_w_xq), w_xk=(w_xk, m_w_xk, v_w_xk), w_xv=(w_xv, m_w_xv, v_w_xv),
        w_xo=(w_xo, m_w_xo, v_w_xo))
    names = list(weights)
    big_names = ("w_in_ab", "w_out_ab", "w_in_c", "w_out_c", "w_xq", "w_xk", "w_xv", "w_xo")
    delta, new_m, new_v = {}, {}, {}
    for nm_ in big_names:
        w, m, v = weights[nm_]
        C = w.shape[-1]
        d2, m2, v2 = adamw("adamw_" + nm_, w.reshape(-1, C), grads[nm_].reshape(-1, C), m.reshape(-1, C),
                           v.reshape(-1, C))
        delta[nm_], new_m[nm_], new_v[nm_] = d2.reshape(w.shape), m2.reshape(w.shape), v2.reshape(w.shape)
    small_names = [n for n in names if n not in big_names]
    stepped = adamw_many("adamw_small", [(weights[n][0], grads[n].reshape(weights[n][0].shape), weights[n][1],
                                          weights[n][2]) for n in small_names])
    for n, (d_, m_, v_) in zip(small_names, stepped):
        delta[n], new_m[n], new_v[n] = d_, m_, v_

    return (loss, dx[None], *[grads[n].reshape(weights[n][0].shape) for n in names], *[delta[n] for n in names],
            *[new_m[n] for n in names], *[new_v[n] for n in names])
```

```python
import functools

import numpy as np
import jax
import jax.numpy as jnp
from jax import lax
from jax.experimental import pallas as pl
from jax.experimental.pallas import tpu as pltpu

F32 = jnp.float32
BF16 = jnp.bfloat16
MESH = pl.DeviceIdType.MESH

EPS = 1e-6
CHUNK = 64
N_PAST_CHUNKS = 8
MAX_REL = 128
HEAD_DIM_A = 128
CONV_WIDTH = 31
CONV_HALO = 32
GMLP_CHUNK = 128
N_GROUPS_C = 8
N_HEADS_X = 4
NEG = -1e30

ADAM_LR = 0.001
ADAM_B1 = 0.9
ADAM_B2 = 0.999
ADAM_EPS = 1e-08
ADAM_WD = 0.01
ADAM_STEP = 10

N_CHIPS = 4
N_DEV = 8
V7X_VMEM_LIMIT = 56 * 1024 * 1024
LANES = 128
SUBLANES = 8


def _pick(n, cands):
    for c in cands:
        if c <= n and n % c == 0:
            return c
    return n


def _cparams(*sem):
    return pltpu.CompilerParams(dimension_semantics=sem, vmem_limit_bytes=V7X_VMEM_LIMIT)


def _sigmoid(x):
    return 0.5 * jnp.tanh(0.5 * x) + 0.5


def _dot(a, b, contract):
    return lax.dot_general(a, b, (contract, ((), ())), preferred_element_type=F32)


NN = ((1,), (0,))
NT = ((1,), (1,))
TN = ((0,), (0,))


class _Comm:
    def __init__(self, arrays, out_shapes, aliases, sems, start, finish, relay=None, relay_frac=0.75):
        self.arrays, self.out_shapes, self.aliases, self.sems = list(arrays), list(out_shapes), dict(aliases), list(sems)
        self.start, self.finish, self.relay = start, finish, relay
        self.relay_frac = relay_frac


def _join(*jobs):
    assert all(j.relay is None for j in jobs)
    arrays, outs, sems, aliases, spans = [], [], [], {}, []
    for j in jobs:
        spans.append((len(arrays), len(outs), len(sems)))
        aliases.update({len(arrays) + i: len(outs) + o for i, o in j.aliases.items()})
        arrays += j.arrays
        outs += j.out_shapes
        sems += j.sems

    def part(j, span, ins, os_, ss):
        a0, o0, s0 = span
        return (ins[a0:a0 + len(j.arrays)], os_[o0:o0 + len(j.out_shapes)], ss[s0:s0 + len(j.sems)])

    def start(ins, os_, ss):
        for j, span in zip(jobs, spans):
            j.start(*part(j, span, ins, os_, ss))

    def finish(ins, os_, ss):
        for j, span in zip(jobs, spans):
            j.finish(*part(j, span, ins, os_, ss))

    return _Comm(arrays, outs, aliases, sems, start, finish)


def _call(body, *, name, grid, in_specs, out_specs, out_shape, args, scratch_shapes=(), sem=None, comm=None,
          prefetch=None, io_aliases=None):
    multi = isinstance(out_shape, (list, tuple))
    o_shapes = list(out_shape) if multi else [out_shape]
    o_specs = list(out_specs) if multi else [out_specs]
    if comm is None:
        assert prefetch is None and io_aliases is None
        return pl.pallas_call(body, grid=grid, in_specs=in_specs, out_specs=out_specs, out_shape=out_shape,
                              scratch_shapes=list(scratch_shapes), name=name,
                              compiler_params=_cparams(*sem))(*args)
    n_in, n_out, n_scr = len(in_specs), len(o_shapes), len(scratch_shapes)
    n_ci, n_co = len(comm.arrays), len(comm.out_shapes)
    n_steps = int(np.prod(grid))
    n_pre = 0 if prefetch is None else 1

    def carrier(*refs):
        pre, refs = refs[:n_pre], refs[n_pre:]
        ins, rest = refs[:n_in], refs[n_in:]
        cins, rest = rest[:n_ci], rest[n_ci:]
        outs, rest = rest[:n_out], rest[n_out:]
        couts, rest = rest[:n_co], rest[n_co:]
        scr, csems = rest[:n_scr], rest[n_scr:]
        step = 0
        for a, g in enumerate(grid):
            step = step * g + pl.program_id(a)

        @pl.when(step == 0)
        def _():
            comm.start(cins, couts, csems)

        body(*pre, *ins, *outs, *scr)

        relay_step = min(int(comm.relay_frac * n_steps), n_steps - 1)
        if comm.relay is not None and relay_step < n_steps - 1:
            @pl.when(step == relay_step)
            def _():
                comm.relay(cins, couts, csems)

        @pl.when(step == n_steps - 1)
        def _():
            if comm.relay is not None and relay_step == n_steps - 1:
                comm.relay(cins, couts, csems)
            comm.finish(cins, couts, csems)

    aliases = {n_pre + n_in + i: n_out + o for i, o in comm.aliases.items()}
    aliases.update({n_pre + i: o for i, o in (io_aliases or {}).items()})
    all_in = list(in_specs) + [HBM_SPEC] * n_ci
    all_out = o_specs + [HBM_SPEC] * n_co
    all_scratch = list(scratch_shapes) + comm.sems
    params = _cparams(*(["arbitrary"] * len(grid)))
    if prefetch is None:
        res = pl.pallas_call(
            carrier, grid=grid, in_specs=all_in, out_specs=all_out, out_shape=o_shapes + comm.out_shapes,
            input_output_aliases=aliases, scratch_shapes=all_scratch, name=name,
            compiler_params=params)(*args, *comm.arrays)
    else:
        grid_spec = pltpu.PrefetchScalarGridSpec(num_scalar_prefetch=1, grid=grid, in_specs=all_in,
                                                 out_specs=all_out, scratch_shapes=all_scratch)
        res = pl.pallas_call(
            carrier, grid_spec=grid_spec, out_shape=o_shapes + comm.out_shapes, input_output_aliases=aliases,
            name=name, compiler_params=params)(prefetch, *args, *comm.arrays)
    mine = list(res[:n_out]) if multi else res[0]
    return mine, list(res[n_out:])


def run_comm(name, comm):
    def body(*refs):
        n_ci, n_co = len(comm.arrays), len(comm.out_shapes)
        cins, couts, csems = refs[:n_ci], refs[n_ci:n_ci + n_co], refs[n_ci + n_co:]
        comm.start(cins, couts, csems)
        if comm.relay is not None:
            comm.relay(cins, couts, csems)
        comm.finish(cins, couts, csems)

    return pl.pallas_call(
        body, in_specs=[HBM_SPEC] * len(comm.arrays), out_specs=[HBM_SPEC] * len(comm.out_shapes),
        out_shape=comm.out_shapes, input_output_aliases=comm.aliases, scratch_shapes=comm.sems,
        name=name)(*comm.arrays)


def _mm(name, a, b, *, contract, grid, a_spec, b_spec, o_spec, out_shape, res=None, comm=None):
    nk = grid[2]

    def body(*refs):
        if res is not None:
            a_ref, b_ref, r_ref, o_ref = refs[:4]
        else:
            a_ref, b_ref, o_ref = refs[:3]
            r_ref = None
        p = _dot(a_ref[...].astype(BF16), b_ref[...].astype(BF16), contract)

        def finish(acc):
            if r_ref is not None:
                acc = acc + r_ref[...]
            o_ref[...] = acc.astype(o_ref.dtype)

        if nk == 1:
            finish(p)
        else:
            acc_ref = refs[-1]
            k = pl.program_id(2)

            @pl.when(k == 0)
            def _():
                acc_ref[...] = p

            @pl.when(k > 0)
            def _():
                acc_ref[...] += p

            @pl.when(k == nk - 1)
            def _():
                finish(acc_ref[...])

    in_specs = [a_spec, b_spec]
    args = [a, b]
    if res is not None:
        in_specs.append(o_spec)
        args.append(res)
    blk = tuple(d for d in o_spec.block_shape if d is not None)
    scratch = [] if nk == 1 else [pltpu.VMEM(blk, F32)]
    return _call(body, name=name, grid=grid, in_specs=in_specs, out_specs=o_spec, out_shape=out_shape,
                 args=args, scratch_shapes=scratch, sem=("parallel", "parallel", "arbitrary"), comm=comm)


def mm_nn_cols(name, a, w4, out_dtype, comm=None):
    M, K = a.shape
    _, _, C = w4.shape
    tm = _pick(M, (1024, 512, 256))
    tn = _pick(C, (1024, 512, 256, 128))
    nps = C // tn
    return _mm(name, a, w4, contract=NN, grid=(M // tm, 4 * nps, 1),
               a_spec=pl.BlockSpec((tm, K), lambda i, j, k: (i, 0)),
               b_spec=pl.BlockSpec((None, K, tn), lambda i, j, k: (j // nps, 0, j % nps)),
               o_spec=pl.BlockSpec((tm, tn), lambda i, j, k: (i, j)),
               out_shape=jax.ShapeDtypeStruct((M, 4 * C), out_dtype), comm=comm)


def proj_cols_own(name, a, w_own, place, comm):
    M, K = a.shape
    C = w_own.shape[1]
    tm = _pick(M, (1024, 512, 256))
    tn = _pick(C, (512, 256, 128))
    nps = C // tn

    def body(s_ref, a_ref, b_ref, o_ref):
        o_ref[...] = _dot(a_ref[...], b_ref[...].astype(BF16), NN).astype(o_ref.dtype)

    return _call(body, name=name, grid=(M // tm, nps),
                 in_specs=[pl.BlockSpec((tm, K), lambda i, j, s: (i, 0)),
                           pl.BlockSpec((K, tn), lambda i, j, s: (0, j))],
                 out_specs=pl.BlockSpec((tm, tn), lambda i, j, s: (i, s[0] * nps + j)),
                 out_shape=jax.ShapeDtypeStruct((M, N_CHIPS * C), BF16), args=[a, w_own], comm=comm,
                 prefetch=place)


def proj_cols_rest(name, a, w4, partial, place, masks, comm):
    M, K = a.shape
    C = w4.shape[2]
    tm = _pick(M, (1024, 512, 256))
    tn = _pick(C, (1792, 1536, 1024, 512, 256, 128))
    nps = C // tn
    assert len(masks) in (1, 2)
    step = masks[-1] - masks[0]

    def slot(j, s):
        return jnp.bitwise_xor(s[0], masks[0] + step * (j // nps))

    def body(s_ref, a_ref, b_ref, part_ref, o_ref):
        o_ref[...] = _dot(a_ref[...], b_ref[...], NN).astype(o_ref.dtype)

    return _call(body, name=name, grid=(M // tm, len(masks) * nps),
                 in_specs=[pl.BlockSpec((tm, K), lambda i, j, s: (i, 0)),
                           pl.BlockSpec((None, K, tn), lambda i, j, s: (slot(j, s), 0, j % nps)),
                           HBM_SPEC],
                 out_specs=pl.BlockSpec((tm, tn), lambda i, j, s: (i, slot(j, s) * nps + j % nps)),
                 out_shape=jax.ShapeDtypeStruct(partial.shape, partial.dtype), args=[a, w4, partial],
                 comm=comm, prefetch=place, io_aliases={2: 0})


def mm_nn(name, a, w, out_dtype, res=None, comm=None):
    M, K = a.shape
    N = w.shape[1]
    tm = _pick(M, (1024, 512, 256))
    tn = _pick(N, (1024, 512, 256, 128) if K <= 2048 else (512, 256, 128))
    return _mm(name, a, w, contract=NN, grid=(M // tm, N // tn, 1),
               a_spec=pl.BlockSpec((tm, K), lambda i, j, k: (i, 0)),
               b_spec=pl.BlockSpec((K, tn), lambda i, j, k: (0, j)),
               o_spec=pl.BlockSpec((tm, tn), lambda i, j, k: (i, j)),
               out_shape=jax.ShapeDtypeStruct((M, N), out_dtype), res=res, comm=comm)


def mm_nt_cols(name, a, w4, out_dtype, comm=None):
    M = a.shape[0]
    _, K, C = w4.shape
    tm = _pick(M, (1024, 512, 256))
    tn = _pick(K, (1024, 512, 256, 128))
    tk = _pick(C, (3584, 3072, 1792, 1536, 1024, 512, 256, 128))
    kps = C // tk
    return _mm(name, a, w4, contract=NT, grid=(M // tm, K // tn, 4 * kps),
               a_spec=pl.BlockSpec((tm, tk), lambda i, j, k: (i, k)),
               b_spec=pl.BlockSpec((None, tn, tk), lambda i, j, k: (k // kps, j, k % kps)),
               o_spec=pl.BlockSpec((tm, tn), lambda i, j, k: (i, j)),
               out_shape=jax.ShapeDtypeStruct((M, K), out_dtype), comm=comm)


def mm_nt(name, a, w, out_dtype):
    M, C = a.shape
    N = w.shape[0]
    tm = _pick(M, (1024, 512, 256))
    tn = _pick(N, (1024, 512, 256, 128))
    return _mm(name, a, w, contract=NT, grid=(M // tm, N // tn, 1),
               a_spec=pl.BlockSpec((tm, C), lambda i, j, k: (i, 0)),
               b_spec=pl.BlockSpec((tn, C), lambda i, j, k: (j, 0)),
               o_spec=pl.BlockSpec((tm, tn), lambda i, j, k: (i, j)),
               out_shape=jax.ShapeDtypeStruct((M, N), out_dtype))


def mm_tn_cols(name, a, b, comm=None):
    S, K = a.shape
    C = b.shape[1] // 4
    ts = _pick(S, (2048, 1024, 512, 256))
    tko = _pick(K, (1024, 512, 256, 128))
    tn = _pick(C, (1792, 1536, 1024, 512, 256, 128))
    nps = C // tn
    return _mm(name, a, b, contract=TN, grid=(K // tko, 4 * nps, S // ts),
               a_spec=pl.BlockSpec((ts, tko), lambda i, j, k: (k, i)),
               b_spec=pl.BlockSpec((ts, tn), lambda i, j, k: (k, j)),
               o_spec=pl.BlockSpec((None, tko, tn), lambda i, j, k: (j // nps, i, j % nps)),
               out_shape=jax.ShapeDtypeStruct((4, K, C), BF16), comm=comm)


def mm_tn(name, a, b):
    S, K = a.shape
    N = b.shape[1]
    ts = _pick(S, (1024, 512, 256))
    tko = _pick(K, (2048, 1024, 512, 256, 128))
    tn = _pick(N, (1024, 512, 256, 128))
    return _mm(name, a, b, contract=TN, grid=(K // tko, N // tn, S // ts),
               a_spec=pl.BlockSpec((ts, tko), lambda i, j, k: (k, i)),
               b_spec=pl.BlockSpec((ts, tn), lambda i, j, k: (k, j)),
               o_spec=pl.BlockSpec((tko, tn), lambda i, j, k: (i, j)),
               out_shape=jax.ShapeDtypeStruct((K, N), BF16))


def rms_fwd(name, x, g):
    S, D = x.shape
    T = _pick(S, (512, 256))

    def body(x_ref, g_ref, o_ref):
        xf = x_ref[...]
        r = lax.rsqrt(jnp.mean(xf * xf, axis=-1, keepdims=True) + EPS)
        o_ref[...] = (xf * r * g_ref[...]).astype(o_ref.dtype)

    return pl.pallas_call(
        body, grid=(S // T,),
        in_specs=[pl.BlockSpec((T, D), lambda i: (i, 0)), pl.BlockSpec((1, D), lambda i: (0, 0))],
        out_specs=pl.BlockSpec((T, D), lambda i: (i, 0)),
        out_shape=jax.ShapeDtypeStruct((S, D), BF16), name=name,
        compiler_params=_cparams("parallel"))(x, g.reshape(1, D))


def rms_bwd(name, x, g, dys, dres):
    S, D = x.shape
    T = _pick(S, (256,))
    ndy = len(dys)
    has_res = dres is not None

    def body(*refs):
        x_ref, g_ref = refs[0], refs[1]
        dy_refs = refs[2:2 + ndy]
        r_ref = refs[2 + ndy] if has_res else None
        dx_ref, dxb_ref, dg_ref = refs[-3], refs[-2], refs[-1]
        i = pl.program_id(0)
        xf = x_ref[...]
        r = lax.rsqrt(jnp.mean(xf * xf, axis=-1, keepdims=True) + EPS)
        xhat = xf * r
        dy = dy_refs[0][...].astype(F32)
        for d in dy_refs[1:]:
            dy = dy + d[...].astype(F32)
        dxhat = dy * g_ref[...]
        dx = r * (dxhat - xhat * jnp.mean(dxhat * xhat, axis=-1, keepdims=True))
        if has_res:
            dx = dx + r_ref[...]
        dx_ref[...] = dx
        dxb_ref[...] = dx.astype(dxb_ref.dtype)
        dg = jnp.sum(dy * xhat, axis=0, keepdims=True)

        @pl.when(i == 0)
        def _():
            dg_ref[...] = dg

        @pl.when(i > 0)
        def _():
            dg_ref[...] += dg

    row = pl.BlockSpec((T, D), lambda i: (i, 0))
    vec = pl.BlockSpec((1, D), lambda i: (0, 0))
    args = [x, g.reshape(1, D), *dys] + ([dres] if has_res else [])
    return pl.pallas_call(
        body, grid=(S // T,),
        in_specs=[row, vec] + [row] * (ndy + int(has_res)),
        out_specs=[row, row, vec],
        out_shape=[jax.ShapeDtypeStruct((S, D), F32), jax.ShapeDtypeStruct((S, D), BF16),
                   jax.ShapeDtypeStruct((1, D), F32)],
        name=name, compiler_params=_cparams("arbitrary"))(*args)


def loss_head(name, h, g, target):
    S, D = h.shape
    T = _pick(S, (256,))

    def body(h_ref, g_ref, t_ref, loss_ref, dg_ref, dh_ref, dhb_ref):
        i = pl.program_id(0)
        xf = h_ref[...]
        gv = g_ref[...]
        r = lax.rsqrt(jnp.mean(xf * xf, axis=-1, keepdims=True) + EPS)
        xhat = xf * r
        err = xhat * gv - t_ref[...]
        part = 0.5 * jnp.sum(jnp.sum(err * err, axis=-1, keepdims=True), axis=0, keepdims=True) / D
        dout = err / D
        dxhat = dout * gv
        dh = r * (dxhat - xhat * jnp.mean(dxhat * xhat, axis=-1, keepdims=True))
        dh_ref[...] = dh
        dhb_ref[...] = dh.astype(dhb_ref.dtype)
        dg = jnp.sum(dout * xhat, axis=0, keepdims=True)
        lrow = jnp.broadcast_to(part, (1, LANES))

        @pl.when(i == 0)
        def _():
            dg_ref[...] = dg
            loss_ref[...] = lrow

        @pl.when(i > 0)
        def _():
            dg_ref[...] += dg
            loss_ref[...] += lrow

    row = pl.BlockSpec((T, D), lambda i: (i, 0))
    vec = pl.BlockSpec((1, D), lambda i: (0, 0))
    return pl.pallas_call(
        body, grid=(S // T,), in_specs=[row, vec, row],
        out_specs=[pl.BlockSpec((1, LANES), lambda i: (0, 0)), vec, row, row],
        out_shape=[jax.ShapeDtypeStruct((1, LANES), F32), jax.ShapeDtypeStruct((1, D), F32),
                   jax.ShapeDtypeStruct((S, D), F32), jax.ShapeDtypeStruct((S, D), BF16)],
        name=name, compiler_params=_cparams("arbitrary"))(h, g.reshape(1, D), target)


def _attn_tq(S):
    return _pick(S, (512,))


def band_bias_table(rel_bias, tq):
    H = rel_bias.shape[0]
    w = 2 * tq
    nbits = int(np.log2(tq))
    assert (1 << nbits) == tq and (N_PAST_CHUNKS + 2) * CHUNK - 1 <= w
    c = np.arange(w)
    d0 = np.where(c <= tq + CHUNK - 1, tq - c, tq + w - c)
    base = jnp.take(rel_bias.astype(F32), jnp.asarray(np.clip(d0, -MAX_REL, MAX_REL) + MAX_REL), axis=1)

    def body(b_ref, o_ref):
        x = jnp.broadcast_to(b_ref[...], (tq, w))
        row = lax.broadcasted_iota(jnp.int32, (tq, w), 0)
        col = lax.broadcasted_iota(jnp.int32, (tq, w), 1)
        for b in range(nbits):
            x = jnp.where(((row >> b) & 1) == 1, pltpu.roll(x, 1 << b, 1), x)
        qc = row // CHUNK
        kc = col // CHUNK - tq // CHUNK
        o_ref[...] = jnp.where((kc <= qc) & (kc >= qc - N_PAST_CHUNKS), x, NEG)

    return pl.pallas_call(
        body, grid=(H,), in_specs=[pl.BlockSpec((None, 1, w), lambda h: (h, 0, 0))],
        out_specs=pl.BlockSpec((None, tq, w), lambda h: (h, 0, 0)),
        out_shape=jax.ShapeDtypeStruct((H, tq, w), F32), name="band_bias_table",
        compiler_params=_cparams("parallel"))(base.reshape(H, 1, w))


def _attn_subblocks(tq):
    sub = tq // 2
    assert sub % CHUNK == 0 and N_PAST_CHUNKS * CHUNK == tq
    return sub, 3


def attn_fwd(proj, bm, D, comm=None):
    S = proj.shape[0]
    H = D // HEAD_DIM_A
    tq = _attn_tq(S)
    nb = S // tq
    scale = HEAD_DIM_A ** -0.5

    sub, n_sub = _attn_subblocks(tq)

    def body(q_ref, kp_ref, kc_ref, vp_ref, vc_ref, bm_ref, o_ref, lse_ref):
        i = pl.program_id(1)
        for qh in range(tq // sub):
            rows = slice(qh * sub, (qh + 1) * sub)
            q = q_ref[rows, :]
            ss = []
            for kb in range(qh, qh + n_sub):
                k_ref, krows = (kp_ref, kb) if kb < tq // sub else (kc_ref, kb - tq // sub)
                s = _dot(q, k_ref[krows * sub:(krows + 1) * sub, :], NT) * scale + bm_ref[rows, kb * sub:(kb + 1) * sub]
                if kb < tq // sub:
                    s = jnp.where(i == 0, NEG, s)
                ss.append(s)
            m = functools.reduce(jnp.maximum, [jnp.max(s, axis=-1, keepdims=True) for s in ss])
            ps = [jnp.exp(s - m) for s in ss]
            l = functools.reduce(jnp.add, [jnp.sum(p, axis=-1, keepdims=True) for p in ps])
            o = None
            for p, kb in zip(ps, range(qh, qh + n_sub)):
                v_ref, vrows = (vp_ref, kb) if kb < tq // sub else (vc_ref, kb - tq // sub)
                t = _dot(p.astype(BF16), v_ref[vrows * sub:(vrows + 1) * sub, :], NN)
                o = t if o is None else o + t
            o_ref[rows, :] = (o / l).astype(o_ref.dtype)
            lse_ref[rows, :] = m + jnp.log(l)

    def col(base):
        return (pl.BlockSpec((tq, HEAD_DIM_A), lambda h, i: (jnp.maximum(i - 1, 0), base + h)),
                pl.BlockSpec((tq, HEAD_DIM_A), lambda h, i: (i, base + h)))

    kp, kc = col(H)
    vp, vc = col(2 * H)
    return _call(
        body, name="attn_fwd", grid=(H, nb),
        in_specs=[pl.BlockSpec((tq, HEAD_DIM_A), lambda h, i: (i, h)), kp, kc, vp, vc,
                  pl.BlockSpec((None, tq, 2 * tq), lambda h, i: (h, 0, 0))],
        out_specs=[pl.BlockSpec((tq, HEAD_DIM_A), lambda h, i: (i, h)),
                   pl.BlockSpec((None, tq, 1), lambda h, i: (h, i, 0))],
        out_shape=[jax.ShapeDtypeStruct((S, D), BF16), jax.ShapeDtypeStruct((H, S, 1), F32)],
        args=[proj, proj, proj, proj, proj, bm], sem=("parallel", "arbitrary"), comm=comm)


def attn_bwd(proj, ya, dya, lse, bm, D, comm=None):
    S = proj.shape[0]
    H = D // HEAD_DIM_A
    tq = _attn_tq(S)
    nb = S // tq
    scale = HEAD_DIM_A ** -0.5
    sub, n_sub = _attn_subblocks(tq)

    def body(q_ref, kp_ref, kc_ref, vp_ref, vc_ref, o_ref, do_ref, lse_ref, bm_ref,
             dq_ref, dkc_ref, dkp_ref, dvc_ref, dvp_ref, ds_ref):
        i = pl.program_id(1)
        per = tq // sub

        @pl.when(i == 0)
        def _():
            ds_ref[...] = jnp.zeros_like(ds_ref)

        dk_acc = [None] * (2 * per)
        dv_acc = [None] * (2 * per)
        for qh in range(per):
            rows = slice(qh * sub, (qh + 1) * sub)
            q = q_ref[rows, :]
            do = do_ref[rows, :]
            delta = jnp.sum(do.astype(F32) * o_ref[rows, :].astype(F32), axis=-1, keepdims=True)
            lse_v = lse_ref[rows, :]
            dq = None
            for kb in range(qh, qh + n_sub):
                k_ref, v_ref, kr = (kp_ref, vp_ref, kb) if kb < per else (kc_ref, vc_ref, kb - per)
                k = k_ref[kr * sub:(kr + 1) * sub, :]
                cols = slice(kb * sub, (kb + 1) * sub)
                s = _dot(q, k, NT) * scale + bm_ref[rows, cols]
                if kb < per:
                    s = jnp.where(i == 0, NEG, s)
                p = jnp.exp(s - lse_v)
                dv = _dot(p.astype(BF16), do, TN)
                dp = _dot(do, v_ref[kr * sub:(kr + 1) * sub, :], NT)
                ds = p * (dp - delta)
                dsb = ds.astype(BF16)
                t = _dot(dsb, k, NN)
                dq = t if dq is None else dq + t
                dk = _dot(dsb, q, TN)
                dk_acc[kb] = dk if dk_acc[kb] is None else dk_acc[kb] + dk
                dv_acc[kb] = dv if dv_acc[kb] is None else dv_acc[kb] + dv
                ds_ref[rows, cols] += ds
            dq_ref[rows, :] = (dq * scale).astype(dq_ref.dtype)
        for kb in range(2 * per):
            dk_ref, dv_ref, kr = (dkp_ref, dvp_ref, kb) if kb < per else (dkc_ref, dvc_ref, kb - per)
            dk_ref[kr * sub:(kr + 1) * sub, :] = (dk_acc[kb] * scale).astype(dk_ref.dtype)
            dv_ref[kr * sub:(kr + 1) * sub, :] = dv_acc[kb].astype(dv_ref.dtype)

    def col(base):
        return (pl.BlockSpec((tq, HEAD_DIM_A), lambda h, i: (jnp.maximum(i - 1, 0), base + h)),
                pl.BlockSpec((tq, HEAD_DIM_A), lambda h, i: (i, base + h)))

    kp, kc = col(H)
    vp, vc = col(2 * H)
    blk = pl.BlockSpec((tq, HEAD_DIM_A), lambda h, i: (i, h))
    sd = jax.ShapeDtypeStruct((S, D), BF16)
    return _call(
        body, name="attn_bwd", grid=(H, nb),
        in_specs=[blk, kp, kc, vp, vc, blk, blk,
                  pl.BlockSpec((None, tq, 1), lambda h, i: (h, i, 0)),
                  pl.BlockSpec((None, tq, 2 * tq), lambda h, i: (h, 0, 0))],
        out_specs=[blk, blk, blk, blk, blk, pl.BlockSpec((None, tq, 2 * tq), lambda h, i: (h, 0, 0))],
        out_shape=[sd, sd, sd, sd, sd, jax.ShapeDtypeStruct((H, tq, 2 * tq), F32)],
        args=[proj, proj, proj, proj, proj, ya, dya, lse, bm], sem=("parallel", "arbitrary"), comm=comm)


def rel_bias_grad(ds_sum):
    H, tq, w = ds_sum.shape
    nbin = 2 * MAX_REL + 1
    nbin_pad = 3 * LANES
    d_lo, d_hi = -(CHUNK - 1), (N_PAST_CHUNKS + 1) * CHUNK - 1
    assert d_hi - d_lo + 1 <= w
    onehot = np.zeros((w, nbin_pad), np.float32)
    for d in range(d_lo, d_hi + 1):
        onehot[(tq - d) % w, int(np.clip(d, -MAX_REL, MAX_REL)) + MAX_REL] = 1.0
    nbits = int(np.log2(tq))
    assert (1 << nbits) == tq

    def body(ds_ref, m_ref, o_ref):
        x = ds_ref[...]
        row = lax.broadcasted_iota(jnp.int32, x.shape, 0)
        for b in range(nbits):
            rolled = pltpu.roll(x, w - (1 << b), 1)
            x = jnp.where(((row >> b) & 1) == 1, rolled, x)
        t = jnp.sum(x, axis=0, keepdims=True)
        o_ref[...] = lax.dot_general(t, m_ref[...], (NN, ((), ())), precision=lax.Precision.HIGHEST,
                                     preferred_element_type=F32)

    out = pl.pallas_call(
        body, grid=(H,),
        in_specs=[pl.BlockSpec((None, tq, w), lambda h: (h, 0, 0)),
                  pl.BlockSpec((w, nbin_pad), lambda h: (0, 0))],
        out_specs=pl.BlockSpec((None, 1, nbin_pad), lambda h: (h, 0, 0)),
        out_shape=jax.ShapeDtypeStruct((H, 1, nbin_pad), F32),
        name="rel_bias_grad", compiler_params=_cparams("parallel"))(ds_sum, jnp.asarray(onehot))
    return out[:, 0, :nbin]


def _conv_t(S):
    return _pick(S, (256,))


ROW_CHUNK = 16


def _row_loop(n_rows, step):
    def one(r, carry):
        step(pl.ds(pl.multiple_of(r * ROW_CHUNK, ROW_CHUNK), ROW_CHUNK))
        return carry

    lax.fori_loop(0, n_rows // ROW_CHUNK, one, 0)


def _fill_zbuf(zbuf, ap_ref, bp_ref, a_ref, b_ref, i):
    zp = ap_ref[...].astype(F32) * _sigmoid(bp_ref[...].astype(F32))
    zbuf[0:CONV_HALO, :] = jnp.where(i == 0, 0.0, zp)

    def step(rows):
        below = pl.ds(pl.multiple_of(rows.start + CONV_HALO, ROW_CHUNK), ROW_CHUNK)
        zbuf[below, :] = a_ref[rows, :].astype(F32) * _sigmoid(b_ref[rows, :].astype(F32))

    _row_loop(a_ref.shape[0], step)


def _shifted_windows(buf, shifted, lanes, T):
    rows = T + CONV_HALO - SUBLANES
    for b in range(1, SUBLANES):
        shifted[b - 1] = buf[pl.ds(b, rows), lanes]

    def window(off, r0=0, n=T):
        a, b = divmod(off, SUBLANES)
        if b == 0:
            return buf[pl.ds(r0 + off, n), lanes]
        return shifted[b - 1, pl.ds(r0 + a * SUBLANES, n), :]

    return window


def _shifted_scratch(T):
    return pltpu.VMEM((SUBLANES - 1, T + CONV_HALO - SUBLANES, LANES), F32)


def conv_gate_fwd(proj, ya, cw, cb, lng, lnb, D, comm=None):
    S = proj.shape[0]
    T = _conv_t(S)
    hb = T // CONV_HALO
    nlb = D // LANES

    def body(ap_ref, bp_ref, a_ref, b_ref, ga_ref, gb_ref, ya_ref, cw_ref, cb_ref, lng_ref, lnb_ref,
             y_ref, c_ref, zbuf, zsh):
        i = pl.program_id(0)
        _fill_zbuf(zbuf, ap_ref, bp_ref, a_ref, b_ref, i)

        def lane_block(lb, carry):
            lanes = pl.ds(pl.multiple_of(lb * LANES, LANES), LANES)
            z_at = _shifted_windows(zbuf, zsh, lanes, T)
            acc = jnp.zeros((T, LANES), F32)
            for k in range(CONV_WIDTH):
                acc = acc + cw_ref[k:k + 1, lanes] * z_at(CONV_HALO - CONV_WIDTH + 1 + k)
            c_ref[:, lanes] = acc + cb_ref[:, lanes]
            return carry

        lax.fori_loop(0, nlb, lane_block, 0)

        def norm_and_gate(rows):
            c = c_ref[rows, :]
            mu = jnp.mean(c, axis=-1, keepdims=True)
            xc = c - mu
            rstd = lax.rsqrt(jnp.mean(xc * xc, axis=-1, keepdims=True) + EPS)
            ln = xc * rstd * lng_ref[...] + lnb_ref[...]
            yb = ln * _sigmoid(ln)
            ga = ga_ref[rows, :].astype(F32)
            gb = gb_ref[rows, :].astype(F32)
            y_ref[rows, :D] = (ya_ref[rows, :].astype(F32) * (ga * _sigmoid(ga))).astype(y_ref.dtype)
            y_ref[rows, D:] = (yb * (gb * _sigmoid(gb))).astype(y_ref.dtype)

        _row_loop(T, norm_and_gate)

    def cur(cidx):
        return pl.BlockSpec((T, D), lambda i: (i, cidx))

    def prev(cidx):
        return pl.BlockSpec((CONV_HALO, D), lambda i: (jnp.maximum(i * hb - 1, 0), cidx))

    vec = pl.BlockSpec((1, D), lambda i: (0, 0))
    return _call(
        body, name="conv_gate_fwd", grid=(S // T,),
        in_specs=[prev(3), prev(4), cur(3), cur(4), cur(5), cur(6), pl.BlockSpec((T, D), lambda i: (i, 0)),
                  pl.BlockSpec((CONV_HALO, D), lambda i: (0, 0)), vec, vec, vec],
        out_specs=[pl.BlockSpec((T, 2 * D), lambda i: (i, 0)), pl.BlockSpec((T, D), lambda i: (i, 0))],
        out_shape=[jax.ShapeDtypeStruct((S, 2 * D), BF16), jax.ShapeDtypeStruct((S, D), F32)],
        scratch_shapes=[pltpu.VMEM((T + CONV_HALO, D), F32), _shifted_scratch(T)],
        args=[proj, proj, proj, proj, proj, proj, ya, cw, cb, lng, lnb], sem=("parallel",), comm=comm)


def conv_gate_bwd_a(dy0, proj, ya, cpre, lng, lnb, D):
    S = proj.shape[0]
    T = _conv_t(S)

    def body(dy_ref, ga_ref, gb_ref, ya_ref, c_ref, lng_ref, lnb_ref,
             dya_ref, dg_ref, dc_ref, dlng_ref, dlnb_ref):
        i = pl.program_id(0)

        c = c_ref[...]
        gv = lng_ref[...]
        mu = jnp.mean(c, axis=-1, keepdims=True)
        xc = c - mu
        rstd = lax.rsqrt(jnp.mean(xc * xc, axis=-1, keepdims=True) + EPS)
        xhat = xc * rstd
        ln = xhat * gv + lnb_ref[...]
        sl = _sigmoid(ln)
        yb = ln * sl
        ga = ga_ref[...].astype(F32)
        gb = gb_ref[...].astype(F32)
        sa = _sigmoid(ga)
        sb = _sigmoid(gb)
        dy_a = dy_ref[:, :D].astype(F32)
        dy_b = dy_ref[:, D:].astype(F32)
        dya_ref[...] = (dy_a * (ga * sa)).astype(dya_ref.dtype)
        dg_ref[:, :D] = (dy_a * ya_ref[...].astype(F32) * (sa * (1.0 + ga * (1.0 - sa)))).astype(dg_ref.dtype)
        dg_ref[:, D:] = (dy_b * yb * (sb * (1.0 + gb * (1.0 - sb)))).astype(dg_ref.dtype)
        dln = dy_b * (gb * sb) * (sl * (1.0 + ln * (1.0 - sl)))
        dxhat = dln * gv
        dc_ref[...] = rstd * (dxhat - jnp.mean(dxhat, axis=-1, keepdims=True)
                              - xhat * jnp.mean(dxhat * xhat, axis=-1, keepdims=True))
        dlng = jnp.sum(dln * xhat, axis=0, keepdims=True)
        dlnb = jnp.sum(dln, axis=0, keepdims=True)

        @pl.when(i == 0)
        def _():
            dlng_ref[...] = dlng
            dlnb_ref[...] = dlnb

        @pl.when(i > 0)
        def _():
            dlng_ref[...] += dlng
            dlnb_ref[...] += dlnb

    row = pl.BlockSpec((T, D), lambda i: (i, 0))
    vec = pl.BlockSpec((1, D), lambda i: (0, 0))
    return pl.pallas_call(
        body, grid=(S // T,),
        in_specs=[pl.BlockSpec((T, 2 * D), lambda i: (i, 0)),
                  pl.BlockSpec((T, D), lambda i: (i, 5)), pl.BlockSpec((T, D), lambda i: (i, 6)),
                  row, row, vec, vec],
        out_specs=[row, pl.BlockSpec((T, 2 * D), lambda i: (i, 0)), row, vec, vec],
        out_shape=[jax.ShapeDtypeStruct((S, D), BF16), jax.ShapeDtypeStruct((S, 2 * D), BF16),
                   jax.ShapeDtypeStruct((S, D), F32), jax.ShapeDtypeStruct((1, D), F32),
                   jax.ShapeDtypeStruct((1, D), F32)],
        name="conv_gate_bwd_a", compiler_params=_cparams("arbitrary"))(
            dy0, proj, proj, ya, cpre, lng, lnb)


def conv_gate_bwd_b(dc, proj, cw, D, comm=None):
    S = proj.shape[0]
    T = _conv_t(S)
    hb = T // CONV_HALO
    nt = S // T
    nlb = D // LANES
    half = T // 2

    def body(dc_ref, dn_ref, ap_ref, bp_ref, a_ref, b_ref, cw_ref, da_ref, db_ref, dcw_ref, dcb_ref,
             zbuf, dcbuf, zsh, dcsh, dcw8):
        i = pl.program_id(0)
        _fill_zbuf(zbuf, ap_ref, bp_ref, a_ref, b_ref, i)
        dcv = dc_ref[...]
        dcbuf[0:T, :] = dcv
        dcbuf[T:, :] = jnp.where(i == nt - 1, 0.0, dn_ref[...])

        @pl.when(i == 0)
        def _():
            dcw8[...] = jnp.zeros_like(dcw8)
            dcb_ref[...] = jnp.zeros_like(dcb_ref)

        dcb_ref[...] += jnp.sum(dcv, axis=0, keepdims=True)

        def lane_block(lb, carry):
            lanes = pl.ds(pl.multiple_of(lb * LANES, LANES), LANES)
            z_at = _shifted_windows(zbuf, zsh, lanes, T)
            dc_at = _shifted_windows(dcbuf, dcsh, lanes, T)
            for r0 in range(0, T, half):
                d0 = dcbuf[r0:r0 + half, lanes]
                dz = jnp.zeros((half, LANES), F32)
                for k in range(CONV_WIDTH):
                    dz = dz + cw_ref[k:k + 1, lanes] * dc_at(CONV_WIDTH - 1 - k, r0, half)
                    prod = d0 * z_at(CONV_HALO - CONV_WIDTH + 1 + k, r0, half)
                    dcw8[pl.ds(k * SUBLANES, SUBLANES), lanes] += jnp.sum(
                        prod.reshape(half // SUBLANES, SUBLANES, LANES), axis=0)
                av = a_ref[r0:r0 + half, lanes].astype(F32)
                sg = _sigmoid(b_ref[r0:r0 + half, lanes].astype(F32))
                da_ref[r0:r0 + half, lanes] = (dz * sg).astype(da_ref.dtype)
                db_ref[r0:r0 + half, lanes] = (dz * av * sg * (1.0 - sg)).astype(db_ref.dtype)
            return carry

        lax.fori_loop(0, nlb, lane_block, 0)

        @pl.when(i == nt - 1)
        def _():
            dcw_ref[...] = jnp.sum(dcw8[...].reshape(CONV_HALO, SUBLANES, D), axis=1)

    def cur(cidx):
        return pl.BlockSpec((T, D), lambda i: (i, cidx))

    def prev(cidx):
        return pl.BlockSpec((CONV_HALO, D), lambda i: (jnp.maximum(i * hb - 1, 0), cidx))

    row = pl.BlockSpec((T, D), lambda i: (i, 0))
    nxt = pl.BlockSpec((CONV_HALO, D), lambda i: (jnp.minimum((i + 1) * hb, nt * hb - 1), 0))
    return _call(
        body, name="conv_gate_bwd_b", grid=(nt,),
        in_specs=[row, nxt, prev(3), prev(4), cur(3), cur(4), pl.BlockSpec((CONV_HALO, D), lambda i: (0, 0))],
        out_specs=[row, row, pl.BlockSpec((CONV_HALO, D), lambda i: (0, 0)),
                   pl.BlockSpec((1, D), lambda i: (0, 0))],
        out_shape=[jax.ShapeDtypeStruct((S, D), BF16), jax.ShapeDtypeStruct((S, D), BF16),
                   jax.ShapeDtypeStruct((CONV_HALO, D), F32), jax.ShapeDtypeStruct((1, D), F32)],
        scratch_shapes=[pltpu.VMEM((T + CONV_HALO, D), F32), pltpu.VMEM((T + CONV_HALO, D), F32),
                        _shifted_scratch(T), _shifted_scratch(T), pltpu.VMEM((CONV_HALO * SUBLANES, D), F32)],
        args=[dc, dc, proj, proj, proj, proj, cw], sem=("arbitrary",), comm=comm)


def assemble_dproj0(dq, dkc, dkp, dvc, dvp, da, db, dgate, D):
    S = dq.shape[0]
    tq = _attn_tq(S)
    T = _pick(S, (256,))
    shift = tq // T
    nt = S // T

    def body(dq_ref, dkc_ref, dkp_ref, dvc_ref, dvp_ref, da_ref, db_ref, dg_ref, o_ref):
        i = pl.program_id(0)
        last = i + shift >= nt
        o_ref[:, 0:D] = dq_ref[...]
        dk = dkc_ref[...].astype(F32) + jnp.where(last, 0.0, dkp_ref[...].astype(F32))
        dv = dvc_ref[...].astype(F32) + jnp.where(last, 0.0, dvp_ref[...].astype(F32))
        o_ref[:, D:2 * D] = dk.astype(o_ref.dtype)
        o_ref[:, 2 * D:3 * D] = dv.astype(o_ref.dtype)
        o_ref[:, 3 * D:4 * D] = da_ref[...]
        o_ref[:, 4 * D:5 * D] = db_ref[...]
        o_ref[:, 5 * D:] = dg_ref[...]

    row = pl.BlockSpec((T, D), lambda i: (i, 0))
    nxt = pl.BlockSpec((T, D), lambda i: (jnp.minimum(i + shift, nt - 1), 0))
    return pl.pallas_call(
        body, grid=(nt,),
        in_specs=[row, row, nxt, row, nxt, row, row, pl.BlockSpec((T, 2 * D), lambda i: (i, 0))],
        out_specs=pl.BlockSpec((T, 7 * D), lambda i: (i, 0)),
        out_shape=jax.ShapeDtypeStruct((S, 7 * D), BF16),
        name="assemble_dproj0", compiler_params=_cparams("parallel"))(dq, dkc, dkp, dvc, dvp, da, db, dgate)


def _sgu_t(S):
    return _pick(S, (256, 128))


def _ws_masked(ws_ref, g):
    row = lax.broadcasted_iota(jnp.int32, (GMLP_CHUNK, GMLP_CHUNK), 0) // CHUNK
    col = lax.broadcasted_iota(jnp.int32, (GMLP_CHUNK, GMLP_CHUNK), 1) // CHUNK
    return jnp.where(row >= col, ws_ref[g], 0.0), row >= col


def sgu_fwd(proj, lng, lnb, ws, bst, MIX):
    S = proj.shape[0]
    T = _sgu_t(S)
    gw = MIX // N_GROUPS_C

    def body(u_ref, v_ref, g_ref, lng_ref, lnb_ref, ws_ref, bst_ref, y_ref):
        v = v_ref[...].astype(F32)
        mu = jnp.mean(v, axis=-1, keepdims=True)
        xc = v - mu
        rstd = lax.rsqrt(jnp.mean(xc * xc, axis=-1, keepdims=True) + EPS)
        for g in range(N_GROUPS_C):
            cols = slice(g * gw, (g + 1) * gw)
            wsm = _ws_masked(ws_ref, g)[0].astype(BF16)
            vn = (xc[:, cols] * rstd * lng_ref[:, cols] + lnb_ref[:, cols]).astype(BF16)
            for blk in range(T // GMLP_CHUNK):
                rows = slice(blk * GMLP_CHUNK, (blk + 1) * GMLP_CHUNK)
                sg = _dot(wsm, vn[rows], NN) + bst_ref[:, g:g + 1]
                gate = g_ref[rows, cols].astype(F32)
                y = u_ref[rows, cols].astype(F32) * sg * (gate * _sigmoid(gate))
                y_ref[rows, cols] = y.astype(y_ref.dtype)

    def part(cidx):
        return pl.BlockSpec((T, MIX), lambda i: (i, cidx))

    vec = pl.BlockSpec((1, MIX), lambda i: (0, 0))
    return pl.pallas_call(
        body, grid=(S // T,),
        in_specs=[part(0), part(1), part(2), vec, vec,
                  pl.BlockSpec((N_GROUPS_C, GMLP_CHUNK, GMLP_CHUNK), lambda i: (0, 0, 0)),
                  pl.BlockSpec((GMLP_CHUNK, N_GROUPS_C), lambda i: (0, 0))],
        out_specs=pl.BlockSpec((T, MIX), lambda i: (i, 0)),
        out_shape=jax.ShapeDtypeStruct((S, MIX), BF16),
        name="sgu_fwd", compiler_params=_cparams("parallel"))(proj, proj, proj, lng, lnb, ws, bst)


def sgu_bwd(dy1, proj, lng, lnb, ws, bst, MIX):
    S = proj.shape[0]
    T = _sgu_t(S)
    gw = MIX // N_GROUPS_C

    def body(dy_ref, u_ref, v_ref, g_ref, lng_ref, lnb_ref, ws_ref, bst_ref,
             dp_ref, dws_ref, dbst_ref, dlng_ref, dlnb_ref, dvn_buf):
        i = pl.program_id(0)

        @pl.when(i == 0)
        def _():
            dws_ref[...] = jnp.zeros_like(dws_ref)
            dbst_ref[...] = jnp.zeros_like(dbst_ref)
            dlng_ref[...] = jnp.zeros_like(dlng_ref)
            dlnb_ref[...] = jnp.zeros_like(dlnb_ref)

        v = v_ref[...].astype(F32)
        mu = jnp.mean(v, axis=-1, keepdims=True)
        xc = v - mu
        rstd = lax.rsqrt(jnp.mean(xc * xc, axis=-1, keepdims=True) + EPS)
        for g in range(N_GROUPS_C):
            cols = slice(g * gw, (g + 1) * gw)
            wsf, keep = _ws_masked(ws_ref, g)
            wsm = wsf.astype(BF16)
            vn = (xc[:, cols] * rstd * lng_ref[:, cols] + lnb_ref[:, cols]).astype(BF16)
            for blk in range(T // GMLP_CHUNK):
                rows = slice(blk * GMLP_CHUNK, (blk + 1) * GMLP_CHUNK)
                vnb = vn[rows]
                sg = _dot(wsm, vnb, NN) + bst_ref[:, g:g + 1]
                gate = g_ref[rows, cols].astype(F32)
                sig = _sigmoid(gate)
                sil = gate * sig
                u = u_ref[rows, cols].astype(F32)
                dy = dy_ref[rows, cols].astype(F32)
                dp_ref[rows, g * gw:(g + 1) * gw] = (dy * sg * sil).astype(dp_ref.dtype)
                dp_ref[rows, 2 * MIX + g * gw:2 * MIX + (g + 1) * gw] = (
                    dy * u * sg * (sig * (1.0 + gate * (1.0 - sig)))).astype(dp_ref.dtype)
                dsg = dy * u * sil
                dsgb = dsg.astype(BF16)
                dvn_buf[rows, cols] = _dot(wsm, dsgb, TN)
                dws_ref[g] += jnp.where(keep, _dot(dsgb, vnb, NT), 0.0)
                dbst_ref[:, g:g + 1] += jnp.sum(dsg, axis=-1, keepdims=True)
        dvn = dvn_buf[...]
        xhat = xc * rstd
        dxhat = dvn * lng_ref[...]
        dv = rstd * (dxhat - jnp.mean(dxhat, axis=-1, keepdims=True)
                     - xhat * jnp.mean(dxhat * xhat, axis=-1, keepdims=True))
        dp_ref[:, MIX:2 * MIX] = dv.astype(dp_ref.dtype)
        dlng_ref[...] += jnp.sum(dvn * xhat, axis=0, keepdims=True)
        dlnb_ref[...] += jnp.sum(dvn, axis=0, keepdims=True)

    def part(cidx):
        return pl.BlockSpec((T, MIX), lambda i: (i, cidx))

    vec = pl.BlockSpec((1, MIX), lambda i: (0, 0))
    wspec = pl.BlockSpec((N_GROUPS_C, GMLP_CHUNK, GMLP_CHUNK), lambda i: (0, 0, 0))
    bspec = pl.BlockSpec((GMLP_CHUNK, N_GROUPS_C), lambda i: (0, 0))
    return pl.pallas_call(
        body, grid=(S // T,),
        in_specs=[pl.BlockSpec((T, MIX), lambda i: (i, 0)), part(0), part(1), part(2), vec, vec, wspec, bspec],
        out_specs=[pl.BlockSpec((T, 3 * MIX), lambda i: (i, 0)), wspec, bspec, vec, vec],
        out_shape=[jax.ShapeDtypeStruct((S, 3 * MIX), BF16),
                   jax.ShapeDtypeStruct((N_GROUPS_C, GMLP_CHUNK, GMLP_CHUNK), F32),
                   jax.ShapeDtypeStruct((GMLP_CHUNK, N_GROUPS_C), F32),
                   jax.ShapeDtypeStruct((1, MIX), F32), jax.ShapeDtypeStruct((1, MIX), F32)],
        scratch_shapes=[pltpu.VMEM((T, MIX), F32)],
        name="sgu_bwd", compiler_params=_cparams("arbitrary"))(dy1, proj, proj, proj, lng, lnb, ws, bst)


def xattn_fwd(name, q, k, v):
    S, D = q.shape
    nm = k.shape[0]
    dh = D // N_HEADS_X
    tq = _pick(S, (512, 256))
    scale = dh ** -0.5

    def body(q_ref, k_ref, v_ref, o_ref, lse_ref):
        s = _dot(q_ref[...], k_ref[...], NT) * scale
        m = jnp.max(s, axis=-1, keepdims=True)
        p = jnp.exp(s - m)
        l = jnp.sum(p, axis=-1, keepdims=True)
        o_ref[...] = (_dot(p.astype(BF16), v_ref[...], NN) / l).astype(o_ref.dtype)
        lse_ref[...] = m + jnp.log(l)

    return pl.pallas_call(
        body, grid=(N_HEADS_X, S // tq),
        in_specs=[pl.BlockSpec((tq, dh), lambda h, i: (i, h)),
                  pl.BlockSpec((nm, dh), lambda h, i: (0, h)), pl.BlockSpec((nm, dh), lambda h, i: (0, h))],
        out_specs=[pl.BlockSpec((tq, dh), lambda h, i: (i, h)),
                   pl.BlockSpec((None, tq, 1), lambda h, i: (h, i, 0))],
        out_shape=[jax.ShapeDtypeStruct((S, D), BF16), jax.ShapeDtypeStruct((N_HEADS_X, S, 1), F32)],
        name=name, compiler_params=_cparams("parallel", "parallel"))(q, k, v)


def xattn_bwd(name, q, k, v, o, do, lse):
    S, D = q.shape
    nm = k.shape[0]
    dh = D // N_HEADS_X
    tq = _pick(S, (512, 256))
    scale = dh ** -0.5

    def body(q_ref, k_ref, v_ref, o_ref, do_ref, lse_ref, dq_ref, dk_ref, dv_ref):
        i = pl.program_id(1)
        q_v = q_ref[...]
        k_v = k_ref[...]
        do_v = do_ref[...]
        p = jnp.exp(_dot(q_v, k_v, NT) * scale - lse_ref[...])
        delta = jnp.sum(do_v.astype(F32) * o_ref[...].astype(F32), axis=-1, keepdims=True)
        dv = _dot(p.astype(BF16), do_v, TN)
        ds = (p * (_dot(do_v, v_ref[...], NT) - delta)).astype(BF16)
        dq_ref[...] = (_dot(ds, k_v, NN) * scale).astype(dq_ref.dtype)
        dk = _dot(ds, q_v, TN) * scale

        @pl.when(i == 0)
        def _():
            dk_ref[...] = dk
            dv_ref[...] = dv

        @pl.when(i > 0)
        def _():
            dk_ref[...] += dk
            dv_ref[...] += dv

    qs = pl.BlockSpec((tq, dh), lambda h, i: (i, h))
    ks = pl.BlockSpec((nm, dh), lambda h, i: (0, h))
    return pl.pallas_call(
        body, grid=(N_HEADS_X, S // tq),
        in_specs=[qs, ks, ks, qs, qs, pl.BlockSpec((None, tq, 1), lambda h, i: (h, i, 0))],
        out_specs=[qs, ks, ks],
        out_shape=[jax.ShapeDtypeStruct((S, D), BF16), jax.ShapeDtypeStruct((nm, D), F32),
                   jax.ShapeDtypeStruct((nm, D), F32)],
        name=name, compiler_params=_cparams("parallel", "arbitrary"))(q, k, v, o, do, lse)


def adamw_rows(name, w, m, v, groups, row_off):
    L, R, C = w.shape
    assert len(groups) == L
    tr = _pick(R, tuple(t for t in (512, 256, 128, 64, 32, 16, 8) if t * C * 4 <= (1 << 20)) or (8,))
    assert row_off % tr == 0
    c1 = 1.0 - ADAM_B1 ** ADAM_STEP
    c2 = 1.0 - ADAM_B2 ** ADAM_STEP

    def body(w_ref, m_ref, v_ref, *refs):
        g_refs, (go_ref, d_ref, nm_ref, nv_ref) = refs[:L], refs[L:]
        layer = pl.program_id(0)
        gv = g_refs[0][...]
        for i in range(1, L):
            gv = jnp.where(layer == i, g_refs[i][...], gv)
        nm = ADAM_B1 * m_ref[...] + (1.0 - ADAM_B1) * gv
        nv = ADAM_B2 * v_ref[...] + (1.0 - ADAM_B2) * (gv * gv)
        go_ref[...] = gv
        d_ref[...] = -ADAM_LR * ((nm / c1) / (jnp.sqrt(nv / c2) + ADAM_EPS) + ADAM_WD * w_ref[...])
        nm_ref[...] = nm
        nv_ref[...] = nv

    blk = pl.BlockSpec((None, tr, C), lambda l, r: (l, r, 0))
    gblk = pl.BlockSpec((tr, C), lambda l, r: (row_off // tr + r, 0))
    sd = jax.ShapeDtypeStruct((L, R, C), F32)
    return pl.pallas_call(body, grid=(L, R // tr), in_specs=[blk] * 3 + [gblk] * L, out_specs=[blk] * 4,
                          out_shape=[sd] * 4, name=name,
                          compiler_params=_cparams("parallel", "parallel"))(w, m, v, *groups)


def adamw_many(name, tensors):
    n = len(tensors)
    c1 = 1.0 - ADAM_B1 ** ADAM_STEP
    c2 = 1.0 - ADAM_B2 ** ADAM_STEP

    def as2d(a):
        return a.reshape((1, -1) if a.ndim == 1 else (-1, a.shape[-1])).astype(F32)

    flat = [as2d(a) for t in tensors for a in t]

    def body(*refs):
        ins, outs = refs[:4 * n], refs[4 * n:]
        for t in range(n):
            w_ref, g_ref, m_ref, v_ref = ins[4 * t:4 * t + 4]
            d_ref, nm_ref, nv_ref = outs[3 * t:3 * t + 3]
            gv = g_ref[...]
            nm = ADAM_B1 * m_ref[...] + (1.0 - ADAM_B1) * gv
            nv = ADAM_B2 * v_ref[...] + (1.0 - ADAM_B2) * (gv * gv)
            d_ref[...] = -ADAM_LR * ((nm / c1) / (jnp.sqrt(nv / c2) + ADAM_EPS) + ADAM_WD * w_ref[...])
            nm_ref[...] = nm
            nv_ref[...] = nv

    vm = pl.BlockSpec(memory_space=pltpu.VMEM)
    shapes = [jax.ShapeDtypeStruct(flat[4 * t].shape, F32) for t in range(n) for _ in range(3)]
    res = pl.pallas_call(body, in_specs=[vm] * (4 * n), out_specs=[vm] * (3 * n), out_shape=shapes, name=name,
                         compiler_params=pltpu.CompilerParams(vmem_limit_bytes=V7X_VMEM_LIMIT))(*flat)
    return [tuple(r.reshape(tensors[t][0].shape) for r in res[3 * t:3 * t + 3]) for t in range(n)]


def add_halves(name, g4, recv, cidx):
    _, R, C = g4.shape
    rh = R // 2
    tr = _pick(rh, (256, 128, 64, 32, 16))
    nrb = rh // tr

    def body(c_ref, a_ref, b_ref, o_ref):
        o_ref[...] = (a_ref[...].astype(F32) + b_ref[...].astype(F32)).astype(o_ref.dtype)

    grid_spec = pltpu.PrefetchScalarGridSpec(
        num_scalar_prefetch=1, grid=(4, nrb),
        in_specs=[pl.BlockSpec((None, tr, C), lambda j, r, c_ref: (j, c_ref[0] * nrb + r, 0)),
                  pl.BlockSpec((None, tr, C), lambda j, r, c_ref: (j, r, 0))],
        out_specs=pl.BlockSpec((None, tr, C), lambda j, r, c_ref: (j, r, 0)))
    return pl.pallas_call(body, grid_spec=grid_spec, out_shape=jax.ShapeDtypeStruct((4, rh, C), BF16),
                          name=name, compiler_params=_cparams("parallel", "parallel"))(cidx, g4, recv)


def sum_chips(name, own, recv, place):
    _, rh, C = own.shape
    tr = _pick(rh, (256, 128, 64, 32, 16))
    nrb = rh // tr

    def body(s_ref, own_ref, recv_ref, o_ref):
        acc = own_ref[...].astype(F32)
        for k in range(N_CHIPS - 1):
            acc = acc + recv_ref[k].astype(F32)
        o_ref[...] = acc

    grid_spec = pltpu.PrefetchScalarGridSpec(
        num_scalar_prefetch=1, grid=(nrb,),
        in_specs=[pl.BlockSpec((None, tr, C), lambda r, s: (s[0], r, 0)),
                  pl.BlockSpec((N_CHIPS - 1, tr, C), lambda r, s: (0, r, 0))],
        out_specs=pl.BlockSpec((tr, C), lambda r, s: (s[1] * nrb + r, 0)))
    return pl.pallas_call(body, grid_spec=grid_spec, out_shape=jax.ShapeDtypeStruct((2 * rh, C), F32),
                          name=name, compiler_params=_cparams("parallel"))(place, own, recv)


def cast_into_slot(name, w, place):
    R, C = w.shape
    tr = _pick(R, (256, 128, 64, 32, 16))

    def body(s_ref, w_ref, o_ref):
        o_ref[...] = w_ref[...].astype(o_ref.dtype)

    grid_spec = pltpu.PrefetchScalarGridSpec(
        num_scalar_prefetch=1, grid=(R // tr,),
        in_specs=[pl.BlockSpec((tr, C), lambda r, s: (r, 0))],
        out_specs=pl.BlockSpec((None, tr, C), lambda r, s: (s[0], r, 0)))
    return pl.pallas_call(body, grid_spec=grid_spec, out_shape=jax.ShapeDtypeStruct((N_CHIPS, R, C), BF16),
                          name=name, compiler_params=_cparams("parallel"))(place, w)


def _place():
    return lax.axis_index("x"), lax.axis_index("y"), lax.axis_index("c")


_CHIP_FLIPS = ((1, 0), (0, 1), (1, 1))


def _flip(v, bit):
    return 1 - v if bit else v


HBM_SPEC = pl.BlockSpec(memory_space=pl.ANY)


def exchange_small(name, buf, reduce):
    R = buf.shape[0]

    def body(x_ref, *refs):
        if reduce:
            sum_ref, all_ref, send_sems, recv_sems, local_sem = refs
        else:
            all_ref, send_sems, recv_sems, local_sem = refs
        x, y, c = _place()
        me = 4 * x + 2 * y + c
        mine = pltpu.make_async_copy(x_ref, all_ref.at[me], local_sem)
        mine.start()
        sends = []
        for k in range(1, N_DEV):
            peer = (_flip(x, k & 4), _flip(y, k & 2), _flip(c, k & 1))
            cp = pltpu.make_async_remote_copy(src_ref=x_ref, dst_ref=all_ref.at[me], send_sem=send_sems.at[k - 1],
                                              recv_sem=recv_sems.at[k - 1], device_id=peer, device_id_type=MESH)
            cp.start()
            sends.append(cp)
        for k in range(1, N_DEV):
            peer = (_flip(x, k & 4), _flip(y, k & 2), _flip(c, k & 1))
            src = 4 * peer[0] + 2 * peer[1] + peer[2]
            pltpu.make_async_remote_copy(src_ref=x_ref, dst_ref=all_ref.at[src], send_sem=send_sems.at[k - 1],
                                         recv_sem=recv_sems.at[k - 1], device_id=peer,
                                         device_id_type=MESH).wait_recv()
        for cp in sends:
            cp.wait_send()
        mine.wait()
        if reduce:
            acc = all_ref[0]
            for d in range(1, N_DEV):
                acc = acc + all_ref[d]
            sum_ref[...] = acc

    vm = pl.BlockSpec(memory_space=pltpu.VMEM)
    sems = [pltpu.SemaphoreType.DMA((N_DEV - 1,)), pltpu.SemaphoreType.DMA((N_DEV - 1,)), pltpu.SemaphoreType.DMA]
    if reduce:
        return pl.pallas_call(
            body, in_specs=[vm], out_specs=vm, out_shape=jax.ShapeDtypeStruct((R, LANES), F32),
            scratch_shapes=[pltpu.VMEM((N_DEV, R, LANES), F32)] + sems, name=name,
            compiler_params=pltpu.CompilerParams(vmem_limit_bytes=V7X_VMEM_LIMIT))(buf)
    return pl.pallas_call(
        body, in_specs=[vm], out_specs=vm, out_shape=jax.ShapeDtypeStruct((N_DEV, R, LANES), F32),
        scratch_shapes=sems, name=name,
        compiler_params=pltpu.CompilerParams(vmem_limit_bytes=V7X_VMEM_LIMIT))(buf)


def exchange_job(buf):
    R = buf.shape[0]

    def copies(x_ref, all_ref, send_sems, recv_sems):
        x, y, c = _place()
        me = 4 * x + 2 * y + c
        sends, arrivals = [], []
        for k in range(1, N_DEV):
            peer = (_flip(x, k & 4), _flip(y, k & 2), _flip(c, k & 1))
            src = 4 * peer[0] + 2 * peer[1] + peer[2]
            sends.append(pltpu.make_async_remote_copy(
                src_ref=x_ref, dst_ref=all_ref.at[me], send_sem=send_sems.at[k - 1], recv_sem=recv_sems.at[k - 1],
                device_id=peer, device_id_type=MESH))
            arrivals.append(pltpu.make_async_remote_copy(
                src_ref=x_ref, dst_ref=all_ref.at[src], send_sem=send_sems.at[k - 1], recv_sem=recv_sems.at[k - 1],
                device_id=peer, device_id_type=MESH))
        return sends, arrivals

    def start(ins, outs, sems):
        for cp in copies(ins[0], outs[0], *sems)[0]:
            cp.start()

    def finish(ins, outs, sems):
        sends, arrivals = copies(ins[0], outs[0], *sems)
        for cp in arrivals:
            cp.wait_recv()
        for cp in sends:
            cp.wait_send()

    return _Comm([buf], [jax.ShapeDtypeStruct((N_DEV, R, LANES), F32)], {},
                 [pltpu.SemaphoreType.DMA((N_DEV - 1,)), pltpu.SemaphoreType.DMA((N_DEV - 1,))], start, finish)


def sum_devices(name, slots):
    _, R, _ = slots.shape
    tr = _pick(R, (512, 256, 128, 64, 32, 16, 8))

    def body(s_ref, o_ref):
        acc = s_ref[0]
        for d in range(1, N_DEV):
            acc = acc + s_ref[d]
        o_ref[...] = acc

    return pl.pallas_call(body, grid=(R // tr,),
                          in_specs=[pl.BlockSpec((N_DEV, tr, LANES), lambda r: (0, r, 0))],
                          out_specs=pl.BlockSpec((tr, LANES), lambda r: (r, 0)),
                          out_shape=jax.ShapeDtypeStruct((R, LANES), F32), name=name,
                          compiler_params=_cparams("parallel"))(slots)


def gather_job(slots, relay_frac=0.75, flips=None):
    n = len(slots)
    flips = flips or [tuple(range(len(_CHIP_FLIPS)))] * n

    def copies(o_refs, send_sems, recv_sems):
        x, y, c = _place()
        me = 2 * x + y
        sib = (x, y, 1 - c)
        chips = [(_flip(x, fx), _flip(y, fy)) for fx, fy in _CHIP_FLIPS]
        ici, fwd, from_sib = [], [], []
        for t in range(n):
            rh = o_refs[t].shape[1] // 2
            mine, theirs = pl.ds(c * rh, rh), pl.ds((1 - c) * rh, rh)
            for k, (px, py) in enumerate(chips):
                if k not in flips[t]:
                    continue
                own = o_refs[t].at[me, mine]
                ici.append(pltpu.make_async_remote_copy(
                    src_ref=own, dst_ref=own, send_sem=send_sems.at[t, k], recv_sem=recv_sems.at[t, k],
                    device_id=(px, py, c), device_id_type=MESH))
                landed = o_refs[t].at[2 * px + py, mine]
                arrival = pltpu.make_async_remote_copy(
                    src_ref=landed, dst_ref=landed, send_sem=send_sems.at[t, k], recv_sem=recv_sems.at[t, k],
                    device_id=(px, py, c), device_id_type=MESH)
                fwd.append((arrival, pltpu.make_async_remote_copy(
                    src_ref=landed, dst_ref=landed, send_sem=send_sems.at[t, 3 + k],
                    recv_sem=recv_sems.at[t, 3 + k], device_id=sib, device_id_type=MESH)))
                passed = o_refs[t].at[2 * px + py, theirs]
                from_sib.append(pltpu.make_async_remote_copy(
                    src_ref=passed, dst_ref=passed, send_sem=send_sems.at[t, 3 + k],
                    recv_sem=recv_sems.at[t, 3 + k], device_id=sib, device_id_type=MESH))
        return ici, fwd, from_sib

    def start(ins, o_refs, sems):
        for cp in copies(o_refs, *sems)[0]:
            cp.start()

    def relay(ins, o_refs, sems):
        for arrival, forward in copies(o_refs, *sems)[1]:
            arrival.wait_recv()
            forward.start()

    def finish(ins, o_refs, sems):
        ici, fwd, from_sib = copies(o_refs, *sems)
        for cp in from_sib:
            cp.wait_recv()
        for cp in ici:
            cp.wait_send()
        for _, forward in fwd:
            forward.wait_send()

    return _Comm(slots, [jax.ShapeDtypeStruct(s.shape, s.dtype) for s in slots], {t: t for t in range(n)},
                 [pltpu.SemaphoreType.DMA((n, 6)), pltpu.SemaphoreType.DMA((n, 6))], start, finish, relay,
                 relay_frac)


def sibling_halves_job(grads):
    n = len(grads)

    def copies(g_refs, o_refs, send_sems, recv_sems):
        x, y, c = _place()
        out = []
        for t in range(n):
            rh = g_refs[t].shape[1] // 2
            out.append(pltpu.make_async_remote_copy(
                src_ref=g_refs[t].at[:, pl.ds((1 - c) * rh, rh), :], dst_ref=o_refs[t],
                send_sem=send_sems.at[t], recv_sem=recv_sems.at[t], device_id=(x, y, 1 - c),
                device_id_type=MESH))
        return out

    def start(g_refs, o_refs, sems):
        for cp in copies(g_refs, o_refs, *sems):
            cp.start()

    def finish(g_refs, o_refs, sems):
        cps = copies(g_refs, o_refs, *sems)
        for cp in cps:
            cp.wait_recv()
        for cp in cps:
            cp.wait_send()

    return _Comm(grads, [jax.ShapeDtypeStruct((4, g.shape[1] // 2, g.shape[2]), g.dtype) for g in grads], {},
                 [pltpu.SemaphoreType.DMA((n,)), pltpu.SemaphoreType.DMA((n,))], start, finish)


def scatter_job(parts):
    n = len(parts)

    def copies(p_refs, o_refs, send_sems, recv_sems):
        x, y, c = _place()
        out = []
        for t in range(n):
            for k, (fx, fy) in enumerate(_CHIP_FLIPS):
                px, py = _flip(x, fx), _flip(y, fy)
                out.append(pltpu.make_async_remote_copy(
                    src_ref=p_refs[t].at[2 * px + py], dst_ref=o_refs[t].at[k],
                    send_sem=send_sems.at[t, k], recv_sem=recv_sems.at[t, k],
                    device_id=(px, py, c), device_id_type=MESH))
        return out

    def start(p_refs, o_refs, sems):
        for cp in copies(p_refs, o_refs, *sems):
            cp.start()

    def finish(p_refs, o_refs, sems):
        cps = copies(p_refs, o_refs, *sems)
        for cp in cps:
            cp.wait_recv()
        for cp in cps:
            cp.wait_send()

    return _Comm(parts, [jax.ShapeDtypeStruct((N_CHIPS - 1,) + p.shape[1:], p.dtype) for p in parts], {},
                 [pltpu.SemaphoreType.DMA((n, 3)), pltpu.SemaphoreType.DMA((n, 3))], start, finish)


def share_halves_job(halves):
    n = len(halves)

    def copies(o_refs, send_sems, recv_sems):
        x, y, c = _place()
        sends, arrivals = [], []
        for t in range(n):
            rh = o_refs[t].shape[0] // 2
            mine = o_refs[t].at[pl.ds(c * rh, rh)]
            theirs = o_refs[t].at[pl.ds((1 - c) * rh, rh)]
            sends.append(pltpu.make_async_remote_copy(
                src_ref=mine, dst_ref=mine, send_sem=send_sems.at[t], recv_sem=recv_sems.at[t],
                device_id=(x, y, 1 - c), device_id_type=MESH))
            arrivals.append(pltpu.make_async_remote_copy(
                src_ref=theirs, dst_ref=theirs, send_sem=send_sems.at[t], recv_sem=recv_sems.at[t],
                device_id=(x, y, 1 - c), device_id_type=MESH))
        return sends, arrivals

    def start(ins, o_refs, sems):
        for cp in copies(o_refs, *sems)[0]:
            cp.start()

    def finish(ins, o_refs, sems):
        sends, arrivals = copies(o_refs, *sems)
        for cp in arrivals:
            cp.wait_recv()
        for cp in sends:
            cp.wait_send()

    return _Comm(halves, [jax.ShapeDtypeStruct(h.shape, h.dtype) for h in halves], {t: t for t in range(n)},
                 [pltpu.SemaphoreType.DMA((n,)), pltpu.SemaphoreType.DMA((n,))], start, finish)


def _pack(arrs, row_multiple=SUBLANES):
    flat, total = [], 0
    for a in arrs:
        v = a.reshape(-1).astype(F32)
        pad = (-v.shape[0]) % (SUBLANES * LANES)
        flat.append(jnp.pad(v, (0, pad)))
        total += v.shape[0] + pad
    tail = (-total) % (row_multiple * LANES)
    if tail:
        flat.append(jnp.zeros((tail,), F32))
    return jnp.concatenate(flat).reshape(-1, LANES)


def _unpack(buf, shapes):
    out, off = [], 0
    flat = buf.reshape(-1)
    for s in shapes:
        n = int(np.prod(s))
        out.append(flat[off:off + n].reshape(s))
        off += n + ((-n) % (8 * LANES))
    return out


def _xattn_layer_fwd(tag, h, mem, gx, gmem, w):
    hx = rms_fwd(f"rms_x{tag}", h, gx)
    memn = rms_fwd(f"rms_mem{tag}", mem, gmem)
    q = mm_nn(f"xq{tag}", hx, w["q"], BF16)
    k = mm_nn(f"xk{tag}", memn, w["k"], BF16)
    v = mm_nn(f"xv{tag}", memn, w["v"], BF16)
    o, lse = xattn_fwd(f"xattn_fwd{tag}", q, k, v)
    h_out = mm_nn(f"xo{tag}", o, w["o"], F32, res=h)
    return h_out, dict(hx=hx, memn=memn, q=q, k=k, v=v, o=o, lse=lse)


def _xattn_layer_bwd(tag, dh_out, dh_out_b, h_in, mem, gx, gmem, w, sv):
    do = mm_nt(f"d_xo{tag}", dh_out_b, w["o"], BF16)
    dwo = mm_tn(f"dw_xo{tag}", sv["o"], dh_out_b)
    dq, dk, dv = xattn_bwd(f"xattn_bwd{tag}", sv["q"], sv["k"], sv["v"], sv["o"], do, sv["lse"])
    dwq = mm_tn(f"dw_xq{tag}", sv["hx"], dq)
    dhx = mm_nt(f"d_xq{tag}", dq, w["q"], F32)
    dwk = mm_tn(f"dw_xk{tag}", sv["memn"], dk)
    dwv = mm_tn(f"dw_xv{tag}", sv["memn"], dv)
    dmk = mm_nt(f"d_xk{tag}", dk, w["k"], F32)
    dmv = mm_nt(f"d_xv{tag}", dv, w["v"], F32)
    dh_in, dh_in_b, dgx = rms_bwd(f"rms_x_bwd{tag}", h_in, gx, [dhx], dh_out)
    _, _, dgmem = rms_bwd(f"rms_mem_bwd{tag}", mem, gmem, [dmk, dmv], None)
    return dh_in, dh_in_b, dgx, dgmem, dict(q=dwq, k=dwk, v=dwv, o=dwo)


def kernel(x, mem, norm_mix_g, norm_x_g, norm_mem_g, final_norm_g, w_in_ab, rel_bias, conv_w, conv_b, conv_ln_g, conv_ln_b, w_out_ab, w_in_c, sgu_ln_g, sgu_ln_b, w_s, b_s, w_out_c, w_xq, w_xk, w_xv, w_xo, loss_target, m_norm_mix_g, m_norm_x_g, m_norm_mem_g, m_final_norm_g, m_w_in_ab, m_rel_bias, m_conv_w, m_conv_b, m_conv_ln_g, m_conv_ln_b, m_w_out_ab, m_w_in_c, m_sgu_ln_g, m_sgu_ln_b, m_w_s, m_b_s, m_w_out_c, m_w_xq, m_w_xk, m_w_xv, m_w_xo, v_norm_mix_g, v_norm_x_g, v_norm_mem_g, v_final_norm_g, v_w_in_ab, v_rel_bias, v_conv_w, v_conv_b, v_conv_ln_g, v_conv_ln_b, v_w_out_ab, v_w_in_c, v_sgu_ln_g, v_sgu_ln_b, v_w_s, v_b_s, v_w_out_c, v_w_xq, v_w_xk, v_w_xv, v_w_xo):
    S, D = x.shape[1], x.shape[2]
    MIX = 2 * D
    xs, mems, tgt = x[0], mem[0], loss_target[0]
    cx, cy, cc = _place()
    chip = 2 * cx + cy
    cidx = jnp.reshape(cc, (1,)).astype(jnp.int32)
    place = jnp.stack([chip, cc]).astype(jnp.int32)

    ro, rq = MIX // 4, D // 4
    row_sharded = [("out_ab", w_out_ab[0]), ("out_c", w_out_c[0])]
    for layer in range(2):
        for nm_, w in (("q", w_xq), ("k", w_xk), ("v", w_xv), ("o", w_xo)):
            row_sharded.append((f"x{nm_}{layer}", w[layer]))
    slots = {"in_ab": cast_into_slot("cast_in_ab", w_in_ab[0], place),
             "in_c": cast_into_slot("cast_in_c", w_in_c[0], place)}
    slots.update({nm_: cast_into_slot("cast_" + nm_, w, place) for nm_, w in row_sharded})

    small_sh = [conv_w[0], sgu_ln_g[0], sgu_ln_b[0]]
    gathered = exchange_small("gather_small", _pack(small_sh), reduce=False)
    per_chip = [_unpack(gathered[2 * j], [a.shape for a in small_sh]) for j in range(N_CHIPS)]
    conv_w_full = jnp.concatenate([p[0] for p in per_chip], axis=1)
    sgu_g_full = jnp.concatenate([p[1] for p in per_chip], axis=0).reshape(1, MIX)
    sgu_b_full = jnp.concatenate([p[2] for p in per_chip], axis=0).reshape(1, MIX)
    cw_pad = jnp.pad(conv_w_full, ((0, CONV_HALO - CONV_WIDTH), (0, 0)))
    cb = conv_b.reshape(1, D)
    clg, clb = conv_ln_g.reshape(1, D), conv_ln_b.reshape(1, D)
    ws = w_s[0]
    bst = jnp.transpose(b_s[0])
    tq = _attn_tq(S)
    bm = band_bias_table(rel_bias[0], tq)

    hn0 = rms_fwd("rms_mix0", xs, norm_mix_g[0])
    near, far, every = (0, 1), (2,), (0, 1, 2)
    proj0, (wab4,) = proj_cols_own("proj_ab_own", hn0, w_in_ab[0], place,
                                   comm=gather_job([slots["in_ab"]], relay_frac=1.0, flips=[near]))
    proj0, (wab4, w_out_ab4) = proj_cols_rest(
        "proj_ab_near", hn0, wab4, proj0, place, (2, 1),
        comm=gather_job([wab4, slots["out_ab"]], flips=[far, every]))
    proj0, got_qk = proj_cols_rest("proj_ab_far", hn0, wab4, proj0, place, (3,),
                                   comm=gather_job([slots["xq0"], slots["xk0"]]))
    (ya, lse_a), (wc4,) = attn_fwd(proj0, bm, D, comm=gather_job([slots["in_c"]]))
    (y0, cpre), got_b = conv_gate_fwd(
        proj0, ya, cw_pad, cb, clg, clb, D,
        comm=gather_job([slots["xv0"], slots["xo0"], slots["out_c"], slots["xq1"]]))
    h1, got_c = mm_nn("out_ab", y0, w_out_ab4.reshape(-1, D), F32, res=xs,
                      comm=gather_job([slots["xk1"], slots["xv1"], slots["xo1"]]))
    got = dict(zip(["xq0", "xk0", "xv0", "xo0", "out_c", "xq1", "xk1", "xv1", "xo1"], got_qk + got_b + got_c))
    wrow = {n: g.reshape(-1, g.shape[2]) for n, g in got.items()}
    wrow["out_ab"] = w_out_ab4.reshape(-1, D)
    wx = [{k: wrow[f"x{k}{layer}"] for k in "qkvo"} for layer in range(2)]
    h2, sx0 = _xattn_layer_fwd("0", h1, mems, norm_x_g[0], norm_mem_g[0], wx[0])
    hn1 = rms_fwd("rms_mix1", h2, norm_mix_g[1])
    proj1 = mm_nn_cols("proj_c", hn1, wc4, BF16)
    y1 = sgu_fwd(proj1, sgu_g_full, sgu_b_full, ws, bst, MIX)
    h3 = mm_nn("out_c", y1, wrow["out_c"], F32, res=h2)
    h4, sx1 = _xattn_layer_fwd("1", h3, mems, norm_x_g[1], norm_mem_g[1], wx[1])
    loss_row, dg_final, dh4, dh4b = loss_head("loss_head", h4, final_norm_g, tgt)

    dh3, dh3b, dgx1, dgmem1, dwx1 = _xattn_layer_bwd("1", dh4, dh4b, h3, mems, norm_x_g[1], norm_mem_g[1], wx[1], sx1)
    def stack_rows(dw_out, dwx):
        return jnp.concatenate([g.reshape(N_CHIPS, -1, g.shape[1]) for g in [dw_out] + [dwx[k] for k in "qkvo"]],
                               axis=1)

    dy1 = mm_nt("d_out_c", dh3b, wrow["out_c"], BF16)
    dw_out_c = mm_tn("dw_out_c", y1, dh3b)
    dproj1, dws, dbst, dsgu_g, dsgu_b = sgu_bwd(dy1, proj1, sgu_g_full, sgu_b_full, ws, bst, MIX)
    grp1 = stack_rows(dw_out_c, dwx1)
    dw_in_c, (sib1,) = mm_tn_cols("dw_in_c", hn1, dproj1, comm=sibling_halves_job([grp1]))
    part1 = add_halves("add_halves1", grp1, sib1, cidx)
    dhn1, (recv1, sib2) = mm_nt_cols("d_proj_c", dproj1, wc4, F32,
                                     comm=_join(scatter_job([part1]), sibling_halves_job([dw_in_c])))
    part2 = add_halves("add_halves2", dw_in_c, sib2, cidx)
    dh2, dh2b, dgmix1 = rms_bwd("rms_mix1_bwd", h2, norm_mix_g[1], [dhn1], dh3)
    dh1, dh1b, dgx0, dgmem0, dwx0 = _xattn_layer_bwd("0", dh2, dh2b, h1, mems, norm_x_g[0], norm_mem_g[0], wx[0], sx0)
    dy0 = mm_nt("d_out_ab", dh1b, wrow["out_ab"], BF16)
    dw_out_ab = mm_tn("dw_out_ab", y0, dh1b)
    grp3 = stack_rows(dw_out_ab, dwx0)
    dya, dgate, dc, dclg, dclb = conv_gate_bwd_a(dy0, proj0, ya, cpre, clg, clb, D)
    (da, db, dcw, dcb), (recv2, sib3) = conv_gate_bwd_b(
        dc, proj0, cw_pad, D, comm=_join(scatter_job([part2]), sibling_halves_job([grp3])))
    part3 = add_halves("add_halves3", grp3, sib3, cidx)
    (dq, dkc, dkp, dvc, dvp, ds_sum), (recv3,) = attn_bwd(proj0, ya, dya, lse_a, bm, D, comm=scatter_job([part3]))
    drel = rel_bias_grad(ds_sum)
    dproj0 = assemble_dproj0(dq, dkc, dkp, dvc, dvp, da, db, dgate, D)
    dw_in_ab = mm_tn_cols("dw_in_ab", hn0, dproj0)
    (sib4,) = run_comm("sibling_halves4", sibling_halves_job([dw_in_ab]))
    part4 = add_halves("add_halves4", dw_in_ab, sib4, cidx)
    halves = [sum_chips(f"sum_chips{t + 1}", p, r, place)
              for t, (p, r) in enumerate(((part1, recv1), (part2, recv2), (part3, recv3)))]
    small_early = [
        jnp.concatenate([dgx0, dgx1], axis=0), jnp.concatenate([dgmem0, dgmem1], axis=0), dg_final.reshape(D),
        drel[None], dcb, dclg, dclb, dws[None], jnp.transpose(dbst)[None],
        dcw[:CONV_WIDTH][None], dsgu_g, dsgu_b]
    early = _pack(small_early, row_multiple=512)
    dhn0, (recv4, small_slots, g_r1, g_c, g_r0) = mm_nt_cols(
        "d_proj_ab", dproj0, wab4, F32,
        comm=_join(scatter_job([part4]), exchange_job(early), share_halves_job(halves)))
    dx, _, dgmix0 = rms_bwd("rms_mix0_bwd", xs, norm_mix_g[0], [dhn0], dh1)
    (g_ab,) = run_comm("share_reduced_half4", share_halves_job([sum_chips("sum_chips4", part4, recv4, place)]))

    me = 4 * cx + 2 * cy + cc
    small_slots = lax.dynamic_update_slice(small_slots, early[None], (me, 0, 0))
    summed = _unpack(sum_devices("sum_small", small_slots), [a.shape for a in small_early])
    (g_norm_x, g_norm_mem, g_final, g_rel, g_conv_b, g_clg, g_clb, g_ws, g_bs,
     g_conv_w_full, g_sgu_g_full, g_sgu_b_full) = summed
    dgmix = jnp.concatenate([dgmix0, dgmix1], axis=0)
    (g_norm_mix,) = _unpack(exchange_small("reduce_late", _pack([dgmix]), reduce=True), [dgmix.shape])
    cws = conv_w.shape[2]
    g_conv_w = lax.dynamic_slice_in_dim(g_conv_w_full, chip * cws, cws, axis=2)
    sgs = sgu_ln_g.shape[1]
    g_sgu_g = lax.dynamic_slice_in_dim(g_sgu_g_full, chip * sgs, sgs, axis=1)
    g_sgu_b = lax.dynamic_slice_in_dim(g_sgu_b_full, chip * sgs, sgs, axis=1)

    loss = lax.psum(loss_row[0, 0], ("x", "y", "c"))

    big_grads = {"w_in_ab": ([g_ab], 0), "w_in_c": ([g_c], 0), "w_out_ab": ([g_r0], 0), "w_out_c": ([g_r1], 0)}
    for i, nm_ in enumerate("qkvo"):
        big_grads["w_x" + nm_] = ([g_r0, g_r1], ro + i * rq)
    grads = dict(
        norm_mix_g=g_norm_mix, norm_x_g=g_norm_x, norm_mem_g=g_norm_mem, final_norm_g=g_final,
        rel_bias=g_rel, conv_w=g_conv_w, conv_b=g_conv_b, conv_ln_g=g_clg, conv_ln_b=g_clb,
        sgu_ln_g=g_sgu_g, sgu_ln_b=g_sgu_b, w_s=g_ws, b_s=g_bs)
    weights = dict(
        norm_mix_g=(norm_mix_g, m_norm_mix_g, v_norm_mix_g), norm_x_g=(norm_x_g, m_norm_x_g, v_norm_x_g),
        norm_mem_g=(norm_mem_g, m_norm_mem_g, v_norm_mem_g), final_norm_g=(final_norm_g, m_final_norm_g, v_final_norm_g),
        w_in_ab=(w_in_ab, m_w_in_ab, v_w_in_ab), rel_bias=(rel_bias, m_rel_bias, v_rel_bias),
        conv_w=(conv_w, m_conv_w, v_conv_w), conv_b=(conv_b, m_conv_b, v_conv_b),
        conv_ln_g=(conv_ln_g, m_conv_ln_g, v_conv_ln_g), conv_ln_b=(conv_ln_b, m_conv_ln_b, v_conv_ln_b),
        w_out_ab=(w_out_ab, m_w_out_ab, v_w_out_ab), w_in_c=(w_in_c, m_w_in_c, v_w_in_c),
        sgu_ln_g=(sgu_ln_g, m_sgu_ln_g, v_sgu_ln_g), sgu_ln_b=(sgu_ln_b, m_sgu_ln_b, v_sgu_ln_b),
        w_s=(w_s, m_w_s, v_w_s), b_s=(b_s, m_b_s, v_b_s), w_out_c=(w_out_c, m_w_out_c, v_w_out_c),
        w_xq=(w_xq, m_w_xq, v_w_xq), w_xk=(w_xk, m_w_xk, v_w_xk), w_xv=(w_xv, m_w_xv, v_w_xv),
        w_xo=(w_xo, m_w_xo, v_w_xo))
    names = list(weights)
    delta, new_m, new_v = {}, {}, {}
    for nm_, (groups, row_off) in big_grads.items():
        w, m, v = weights[nm_]
        grads[nm_], delta[nm_], new_m[nm_], new_v[nm_] = adamw_rows("adamw_" + nm_, w, m, v, groups, row_off)
    small_names = [n for n in names if n not in big_grads]
    stepped = adamw_many("adamw_small", [(weights[n][0], grads[n].reshape(weights[n][0].shape), weights[n][1],
                                          weights[n][2]) for n in small_names])
    for n, (d_, m_, v_) in zip(small_names, stepped):
        delta[n], new_m[n], new_v[n] = d_, m_, v_

    return (loss, dx[None], *[grads[n].reshape(weights[n][0].shape) for n in names], *[delta[n] for n in names],
            *[new_m[n] for n in names], *[new_v[n] for n in names])
```

```python
import functools

import numpy as np
import jax
import jax.numpy as jnp
from jax import lax
from jax.experimental import pallas as pl
from jax.experimental.pallas import tpu as pltpu

F32 = jnp.float32
BF16 = jnp.bfloat16
MESH = pl.DeviceIdType.MESH

EPS = 1e-6
CHUNK = 64
N_PAST_CHUNKS = 8
MAX_REL = 128
HEAD_DIM_A = 128
CONV_WIDTH = 31
CONV_HALO = 32
GMLP_CHUNK = 128
N_GROUPS_C = 8
N_HEADS_X = 4
NEG = -1e30

ADAM_LR = 0.001
ADAM_B1 = 0.9
ADAM_B2 = 0.999
ADAM_EPS = 1e-08
ADAM_WD = 0.01
ADAM_STEP = 10

N_CHIPS = 4
N_DEV = 8
V7X_VMEM_LIMIT = 56 * 1024 * 1024
LANES = 128
SUBLANES = 8


def _pick(n, cands):
    for c in cands:
        if c <= n and n % c == 0:
            return c
    return n


def _cparams(*sem):
    return pltpu.CompilerParams(dimension_semantics=sem, vmem_limit_bytes=V7X_VMEM_LIMIT)


def _sigmoid(x):
    return 0.5 * jnp.tanh(0.5 * x) + 0.5


def _dot(a, b, contract):
    return lax.dot_general(a, b, (contract, ((), ())), preferred_element_type=F32)


NN = ((1,), (0,))
NT = ((1,), (1,))
TN = ((0,), (0,))


class _Comm:
    def __init__(self, arrays, out_shapes, aliases, sems, start, finish, relay=None, relay_frac=0.75):
        self.arrays, self.out_shapes, self.aliases, self.sems = list(arrays), list(out_shapes), dict(aliases), list(sems)
        self.start, self.finish, self.relay = start, finish, relay
        self.relay_frac = relay_frac


def _join(*jobs):
    assert all(j.relay is None for j in jobs)
    arrays, outs, sems, aliases, spans = [], [], [], {}, []
    for j in jobs:
        spans.append((len(arrays), len(outs), len(sems)))
        aliases.update({len(arrays) + i: len(outs) + o for i, o in j.aliases.items()})
        arrays += j.arrays
        outs += j.out_shapes
        sems += j.sems

    def part(j, span, ins, os_, ss):
        a0, o0, s0 = span
        return (ins[a0:a0 + len(j.arrays)], os_[o0:o0 + len(j.out_shapes)], ss[s0:s0 + len(j.sems)])

    def start(ins, os_, ss):
        for j, span in zip(jobs, spans):
            j.start(*part(j, span, ins, os_, ss))

    def finish(ins, os_, ss):
        for j, span in zip(jobs, spans):
            j.finish(*part(j, span, ins, os_, ss))

    return _Comm(arrays, outs, aliases, sems, start, finish)


def _call(body, *, name, grid, in_specs, out_specs, out_shape, args, scratch_shapes=(), sem=None, comm=None,
          prefetch=None, io_aliases=None):
    multi = isinstance(out_shape, (list, tuple))
    o_shapes = list(out_shape) if multi else [out_shape]
    o_specs = list(out_specs) if multi else [out_specs]
    if comm is None:
        assert prefetch is None and io_aliases is None
        return pl.pallas_call(body, grid=grid, in_specs=in_specs, out_specs=out_specs, out_shape=out_shape,
                              scratch_shapes=list(scratch_shapes), name=name,
                              compiler_params=_cparams(*sem))(*args)
    n_in, n_out, n_scr = len(in_specs), len(o_shapes), len(scratch_shapes)
    n_ci, n_co = len(comm.arrays), len(comm.out_shapes)
    n_steps = int(np.prod(grid))
    n_pre = 0 if prefetch is None else 1

    def carrier(*refs):
        pre, refs = refs[:n_pre], refs[n_pre:]
        ins, rest = refs[:n_in], refs[n_in:]
        cins, rest = rest[:n_ci], rest[n_ci:]
        outs, rest = rest[:n_out], rest[n_out:]
        couts, rest = rest[:n_co], rest[n_co:]
        scr, csems = rest[:n_scr], rest[n_scr:]
        step = 0
        for a, g in enumerate(grid):
            step = step * g + pl.program_id(a)

        @pl.when(step == 0)
        def _():
            comm.start(cins, couts, csems)

        body(*pre, *ins, *outs, *scr)

        relay_step = min(int(comm.relay_frac * n_steps), n_steps - 1)
        if comm.relay is not None and relay_step < n_steps - 1:
            @pl.when(step == relay_step)
            def _():
                comm.relay(cins, couts, csems)

        @pl.when(step == n_steps - 1)
        def _():
            if comm.relay is not None and relay_step == n_steps - 1:
                comm.relay(cins, couts, csems)
            comm.finish(cins, couts, csems)

    aliases = {n_pre + n_in + i: n_out + o for i, o in comm.aliases.items()}
    aliases.update({n_pre + i: o for i, o in (io_aliases or {}).items()})
    all_in = list(in_specs) + [HBM_SPEC] * n_ci
    all_out = o_specs + [HBM_SPEC] * n_co
    all_scratch = list(scratch_shapes) + comm.sems
    params = _cparams(*(["arbitrary"] * len(grid)))
    if prefetch is None:
        res = pl.pallas_call(
            carrier, grid=grid, in_specs=all_in, out_specs=all_out, out_shape=o_shapes + comm.out_shapes,
            input_output_aliases=aliases, scratch_shapes=all_scratch, name=name,
            compiler_params=params)(*args, *comm.arrays)
    else:
        grid_spec = pltpu.PrefetchScalarGridSpec(num_scalar_prefetch=1, grid=grid, in_specs=all_in,
                                                 out_specs=all_out, scratch_shapes=all_scratch)
        res = pl.pallas_call(
            carrier, grid_spec=grid_spec, out_shape=o_shapes + comm.out_shapes, input_output_aliases=aliases,
            name=name, compiler_params=params)(prefetch, *args, *comm.arrays)
    mine = list(res[:n_out]) if multi else res[0]
    return mine, list(res[n_out:])


def run_comm(name, comm):
    def body(*refs):
        n_ci, n_co = len(comm.arrays), len(comm.out_shapes)
        cins, couts, csems = refs[:n_ci], refs[n_ci:n_ci + n_co], refs[n_ci + n_co:]
        comm.start(cins, couts, csems)
        if comm.relay is not None:
            comm.relay(cins, couts, csems)
        comm.finish(cins, couts, csems)

    return pl.pallas_call(
        body, in_specs=[HBM_SPEC] * len(comm.arrays), out_specs=[HBM_SPEC] * len(comm.out_shapes),
        out_shape=comm.out_shapes, input_output_aliases=comm.aliases, scratch_shapes=comm.sems,
        name=name)(*comm.arrays)


def _mm(name, a, b, *, contract, grid, a_spec, b_spec, o_spec, out_shape, res=None, comm=None):
    nk = grid[2]

    def body(*refs):
        if res is not None:
            a_ref, b_ref, r_ref, o_ref = refs[:4]
        else:
            a_ref, b_ref, o_ref = refs[:3]
            r_ref = None
        p = _dot(a_ref[...].astype(BF16), b_ref[...].astype(BF16), contract)

        def finish(acc):
            if r_ref is not None:
                acc = acc + r_ref[...]
            o_ref[...] = acc.astype(o_ref.dtype)

        if nk == 1:
            finish(p)
        else:
            acc_ref = refs[-1]
            k = pl.program_id(2)

            @pl.when(k == 0)
            def _():
                acc_ref[...] = p

            @pl.when(k > 0)
            def _():
                acc_ref[...] += p

            @pl.when(k == nk - 1)
            def _():
                finish(acc_ref[...])

    in_specs = [a_spec, b_spec]
    args = [a, b]
    if res is not None:
        in_specs.append(o_spec)
        args.append(res)
    blk = tuple(d for d in o_spec.block_shape if d is not None)
    scratch = [] if nk == 1 else [pltpu.VMEM(blk, F32)]
    return _call(body, name=name, grid=grid, in_specs=in_specs, out_specs=o_spec, out_shape=out_shape,
                 args=args, scratch_shapes=scratch, sem=("parallel", "parallel", "arbitrary"), comm=comm)


def mm_nn_cols(name, a, w4, out_dtype, comm=None):
    M, K = a.shape
    _, _, C = w4.shape
    tm = _pick(M, (1024, 512, 256))
    tn = _pick(C, (1024, 512, 256, 128))
    nps = C // tn
    return _mm(name, a, w4, contract=NN, grid=(M // tm, 4 * nps, 1),
               a_spec=pl.BlockSpec((tm, K), lambda i, j, k: (i, 0)),
               b_spec=pl.BlockSpec((None, K, tn), lambda i, j, k: (j // nps, 0, j % nps)),
               o_spec=pl.BlockSpec((tm, tn), lambda i, j, k: (i, j)),
               out_shape=jax.ShapeDtypeStruct((M, 4 * C), out_dtype), comm=comm)


def proj_cols_own(name, a, w_own, place, comm):
    M, K = a.shape
    C = w_own.shape[1]
    tm = _pick(M, (1024, 512, 256))
    tn = _pick(C, (512, 256, 128))
    nps = C // tn

    def body(s_ref, a_ref, b_ref, o_ref):
        o_ref[...] = _dot(a_ref[...], b_ref[...].astype(BF16), NN).astype(o_ref.dtype)

    return _call(body, name=name, grid=(M // tm, nps),
                 in_specs=[pl.BlockSpec((tm, K), lambda i, j, s: (i, 0)),
                           pl.BlockSpec((K, tn), lambda i, j, s: (0, j))],
                 out_specs=pl.BlockSpec((tm, tn), lambda i, j, s: (i, s[0] * nps + j)),
                 out_shape=jax.ShapeDtypeStruct((M, N_CHIPS * C), BF16), args=[a, w_own], comm=comm,
                 prefetch=place)


def proj_cols_rest(name, a, w4, partial, place, masks, comm):
    M, K = a.shape
    C = w4.shape[2]
    tm = _pick(M, (1024, 512, 256))
    tn = _pick(C, (1792, 1536, 1024, 512, 256, 128))
    nps = C // tn
    assert len(masks) in (1, 2)
    step = masks[-1] - masks[0]

    def slot(j, s):
        return jnp.bitwise_xor(s[0], masks[0] + step * (j // nps))

    def body(s_ref, a_ref, b_ref, part_ref, o_ref):
        o_ref[...] = _dot(a_ref[...], b_ref[...], NN).astype(o_ref.dtype)

    return _call(body, name=name, grid=(M // tm, len(masks) * nps),
                 in_specs=[pl.BlockSpec((tm, K), lambda i, j, s: (i, 0)),
                           pl.BlockSpec((None, K, tn), lambda i, j, s: (slot(j, s), 0, j % nps)),
                           HBM_SPEC],
                 out_specs=pl.BlockSpec((tm, tn), lambda i, j, s: (i, slot(j, s) * nps + j % nps)),
                 out_shape=jax.ShapeDtypeStruct(partial.shape, partial.dtype), args=[a, w4, partial],
                 comm=comm, prefetch=place, io_aliases={2: 0})


def mm_nn(name, a, w, out_dtype, res=None, comm=None):
    M, K = a.shape
    N = w.shape[1]
    tm = _pick(M, (1024, 512, 256))
    tn = _pick(N, (1024, 512, 256, 128))
    tk = _pick(K, (2048, 1024, 512, 256, 128))
    return _mm(name, a, w, contract=NN, grid=(M // tm, N // tn, K // tk),
               a_spec=pl.BlockSpec((tm, tk), lambda i, j, k: (i, k)),
               b_spec=pl.BlockSpec((tk, tn), lambda i, j, k: (k, j)),
               o_spec=pl.BlockSpec((tm, tn), lambda i, j, k: (i, j)),
               out_shape=jax.ShapeDtypeStruct((M, N), out_dtype), res=res, comm=comm)


def mm_nt_cols(name, a, w4, out_dtype, comm=None):
    M = a.shape[0]
    _, K, C = w4.shape
    tm = _pick(M, (1024, 512, 256))
    tn = _pick(K, (1024, 512, 256, 128))
    tk = _pick(C, (3584, 3072, 1792, 1536, 1024, 512, 256, 128))
    kps = C // tk
    return _mm(name, a, w4, contract=NT, grid=(M // tm, K // tn, 4 * kps),
               a_spec=pl.BlockSpec((tm, tk), lambda i, j, k: (i, k)),
               b_spec=pl.BlockSpec((None, tn, tk), lambda i, j, k: (k // kps, j, k % kps)),
               o_spec=pl.BlockSpec((tm, tn), lambda i, j, k: (i, j)),
               out_shape=jax.ShapeDtypeStruct((M, K), out_dtype), comm=comm)


def mm_nt(name, a, w, out_dtype):
    M, C = a.shape
    N = w.shape[0]
    tm = _pick(M, (1024, 512, 256))
    tn = _pick(N, (1024, 512, 256, 128))
    return _mm(name, a, w, contract=NT, grid=(M // tm, N // tn, 1),
               a_spec=pl.BlockSpec((tm, C), lambda i, j, k: (i, 0)),
               b_spec=pl.BlockSpec((tn, C), lambda i, j, k: (j, 0)),
               o_spec=pl.BlockSpec((tm, tn), lambda i, j, k: (i, j)),
               out_shape=jax.ShapeDtypeStruct((M, N), out_dtype))


def mm_tn_cols(name, a, b, comm=None):
    S, K = a.shape
    C = b.shape[1] // 4
    ts = _pick(S, (2048, 1024, 512, 256))
    tko = _pick(K, (1024, 512, 256, 128))
    tn = _pick(C, (1792, 1536, 1024, 512, 256, 128))
    nps = C // tn
    return _mm(name, a, b, contract=TN, grid=(K // tko, 4 * nps, S // ts),
               a_spec=pl.BlockSpec((ts, tko), lambda i, j, k: (k, i)),
               b_spec=pl.BlockSpec((ts, tn), lambda i, j, k: (k, j)),
               o_spec=pl.BlockSpec((None, tko, tn), lambda i, j, k: (j // nps, i, j % nps)),
               out_shape=jax.ShapeDtypeStruct((4, K, C), BF16), comm=comm)


def mm_tn(name, a, b):
    S, K = a.shape
    N = b.shape[1]
    ts = _pick(S, (1024, 512, 256))
    tko = _pick(K, (2048, 1024, 512, 256, 128))
    tn = _pick(N, (1024, 512, 256, 128))
    return _mm(name, a, b, contract=TN, grid=(K // tko, N // tn, S // ts),
               a_spec=pl.BlockSpec((ts, tko), lambda i, j, k: (k, i)),
               b_spec=pl.BlockSpec((ts, tn), lambda i, j, k: (k, j)),
               o_spec=pl.BlockSpec((tko, tn), lambda i, j, k: (i, j)),
               out_shape=jax.ShapeDtypeStruct((K, N), BF16))


def rms_fwd(name, x, g):
    S, D = x.shape
    T = _pick(S, (512, 256))

    def body(x_ref, g_ref, o_ref):
        xf = x_ref[...]
        r = lax.rsqrt(jnp.mean(xf * xf, axis=-1, keepdims=True) + EPS)
        o_ref[...] = (xf * r * g_ref[...]).astype(o_ref.dtype)

    return pl.pallas_call(
        body, grid=(S // T,),
        in_specs=[pl.BlockSpec((T, D), lambda i: (i, 0)), pl.BlockSpec((1, D), lambda i: (0, 0))],
        out_specs=pl.BlockSpec((T, D), lambda i: (i, 0)),
        out_shape=jax.ShapeDtypeStruct((S, D), BF16), name=name,
        compiler_params=_cparams("parallel"))(x, g.reshape(1, D))


def rms_bwd(name, x, g, dys, dres):
    S, D = x.shape
    T = _pick(S, (256,))
    ndy = len(dys)
    has_res = dres is not None

    def body(*refs):
        x_ref, g_ref = refs[0], refs[1]
        dy_refs = refs[2:2 + ndy]
        r_ref = refs[2 + ndy] if has_res else None
        dx_ref, dxb_ref, dg_ref = refs[-3], refs[-2], refs[-1]
        i = pl.program_id(0)
        xf = x_ref[...]
        r = lax.rsqrt(jnp.mean(xf * xf, axis=-1, keepdims=True) + EPS)
        xhat = xf * r
        dy = dy_refs[0][...].astype(F32)
        for d in dy_refs[1:]:
            dy = dy + d[...].astype(F32)
        dxhat = dy * g_ref[...]
        dx = r * (dxhat - xhat * jnp.mean(dxhat * xhat, axis=-1, keepdims=True))
        if has_res:
            dx = dx + r_ref[...]
        dx_ref[...] = dx
        dxb_ref[...] = dx.astype(dxb_ref.dtype)
        dg = jnp.sum(dy * xhat, axis=0, keepdims=True)

        @pl.when(i == 0)
        def _():
            dg_ref[...] = dg

        @pl.when(i > 0)
        def _():
            dg_ref[...] += dg

    row = pl.BlockSpec((T, D), lambda i: (i, 0))
    vec = pl.BlockSpec((1, D), lambda i: (0, 0))
    args = [x, g.reshape(1, D), *dys] + ([dres] if has_res else [])
    return pl.pallas_call(
        body, grid=(S // T,),
        in_specs=[row, vec] + [row] * (ndy + int(has_res)),
        out_specs=[row, row, vec],
        out_shape=[jax.ShapeDtypeStruct((S, D), F32), jax.ShapeDtypeStruct((S, D), BF16),
                   jax.ShapeDtypeStruct((1, D), F32)],
        name=name, compiler_params=_cparams("arbitrary"))(*args)


def loss_head(name, h, g, target):
    S, D = h.shape
    T = _pick(S, (256,))

    def body(h_ref, g_ref, t_ref, loss_ref, dg_ref, dh_ref, dhb_ref):
        i = pl.program_id(0)
        xf = h_ref[...]
        gv = g_ref[...]
        r = lax.rsqrt(jnp.mean(xf * xf, axis=-1, keepdims=True) + EPS)
        xhat = xf * r
        err = xhat * gv - t_ref[...]
        part = 0.5 * jnp.sum(jnp.sum(err * err, axis=-1, keepdims=True), axis=0, keepdims=True) / D
        dout = err / D
        dxhat = dout * gv
        dh = r * (dxhat - xhat * jnp.mean(dxhat * xhat, axis=-1, keepdims=True))
        dh_ref[...] = dh
        dhb_ref[...] = dh.astype(dhb_ref.dtype)
        dg = jnp.sum(dout * xhat, axis=0, keepdims=True)
        lrow = jnp.broadcast_to(part, (1, LANES))

        @pl.when(i == 0)
        def _():
            dg_ref[...] = dg
            loss_ref[...] = lrow

        @pl.when(i > 0)
        def _():
            dg_ref[...] += dg
            loss_ref[...] += lrow

    row = pl.BlockSpec((T, D), lambda i: (i, 0))
    vec = pl.BlockSpec((1, D), lambda i: (0, 0))
    return pl.pallas_call(
        body, grid=(S // T,), in_specs=[row, vec, row],
        out_specs=[pl.BlockSpec((1, LANES), lambda i: (0, 0)), vec, row, row],
        out_shape=[jax.ShapeDtypeStruct((1, LANES), F32), jax.ShapeDtypeStruct((1, D), F32),
                   jax.ShapeDtypeStruct((S, D), F32), jax.ShapeDtypeStruct((S, D), BF16)],
        name=name, compiler_params=_cparams("arbitrary"))(h, g.reshape(1, D), target)


def _attn_tq(S):
    return _pick(S, (512,))


def band_bias_table(rel_bias, tq):
    H = rel_bias.shape[0]
    w = 2 * tq
    nbits = int(np.log2(tq))
    assert (1 << nbits) == tq and (N_PAST_CHUNKS + 2) * CHUNK - 1 <= w
    c = np.arange(w)
    d0 = np.where(c <= tq + CHUNK - 1, tq - c, tq + w - c)
    base = jnp.take(rel_bias.astype(F32), jnp.asarray(np.clip(d0, -MAX_REL, MAX_REL) + MAX_REL), axis=1)

    def body(b_ref, o_ref):
        x = jnp.broadcast_to(b_ref[...], (tq, w))
        row = lax.broadcasted_iota(jnp.int32, (tq, w), 0)
        col = lax.broadcasted_iota(jnp.int32, (tq, w), 1)
        for b in range(nbits):
            x = jnp.where(((row >> b) & 1) == 1, pltpu.roll(x, 1 << b, 1), x)
        qc = row // CHUNK
        kc = col // CHUNK - tq // CHUNK
        o_ref[...] = jnp.where((kc <= qc) & (kc >= qc - N_PAST_CHUNKS), x, NEG)

    return pl.pallas_call(
        body, grid=(H,), in_specs=[pl.BlockSpec((None, 1, w), lambda h: (h, 0, 0))],
        out_specs=pl.BlockSpec((None, tq, w), lambda h: (h, 0, 0)),
        out_shape=jax.ShapeDtypeStruct((H, tq, w), F32), name="band_bias_table",
        compiler_params=_cparams("parallel"))(base.reshape(H, 1, w))


def _attn_subblocks(tq):
    sub = tq // 2
    assert sub % CHUNK == 0 and N_PAST_CHUNKS * CHUNK == tq
    return sub, 3


def attn_fwd(proj, bm, D, comm=None):
    S = proj.shape[0]
    H = D // HEAD_DIM_A
    tq = _attn_tq(S)
    nb = S // tq
    scale = HEAD_DIM_A ** -0.5

    sub, n_sub = _attn_subblocks(tq)

    def body(q_ref, kp_ref, kc_ref, vp_ref, vc_ref, bm_ref, o_ref, lse_ref):
        i = pl.program_id(1)
        for qh in range(tq // sub):
            rows = slice(qh * sub, (qh + 1) * sub)
            q = q_ref[rows, :]
            ss = []
            for kb in range(qh, qh + n_sub):
                k_ref, krows = (kp_ref, kb) if kb < tq // sub else (kc_ref, kb - tq // sub)
                s = _dot(q, k_ref[krows * sub:(krows + 1) * sub, :], NT) * scale + bm_ref[rows, kb * sub:(kb + 1) * sub]
                if kb < tq // sub:
                    s = jnp.where(i == 0, NEG, s)
                ss.append(s)
            m = functools.reduce(jnp.maximum, [jnp.max(s, axis=-1, keepdims=True) for s in ss])
            ps = [jnp.exp(s - m) for s in ss]
            l = functools.reduce(jnp.add, [jnp.sum(p, axis=-1, keepdims=True) for p in ps])
            o = None
            for p, kb in zip(ps, range(qh, qh + n_sub)):
                v_ref, vrows = (vp_ref, kb) if kb < tq // sub else (vc_ref, kb - tq // sub)
                t = _dot(p.astype(BF16), v_ref[vrows * sub:(vrows + 1) * sub, :], NN)
                o = t if o is None else o + t
            o_ref[rows, :] = (o / l).astype(o_ref.dtype)
            lse_ref[rows, :] = m + jnp.log(l)

    def col(base):
        return (pl.BlockSpec((tq, HEAD_DIM_A), lambda h, i: (jnp.maximum(i - 1, 0), base + h)),
                pl.BlockSpec((tq, HEAD_DIM_A), lambda h, i: (i, base + h)))

    kp, kc = col(H)
    vp, vc = col(2 * H)
    return _call(
        body, name="attn_fwd", grid=(H, nb),
        in_specs=[pl.BlockSpec((tq, HEAD_DIM_A), lambda h, i: (i, h)), kp, kc, vp, vc,
                  pl.BlockSpec((None, tq, 2 * tq), lambda h, i: (h, 0, 0))],
        out_specs=[pl.BlockSpec((tq, HEAD_DIM_A), lambda h, i: (i, h)),
                   pl.BlockSpec((None, tq, 1), lambda h, i: (h, i, 0))],
        out_shape=[jax.ShapeDtypeStruct((S, D), BF16), jax.ShapeDtypeStruct((H, S, 1), F32)],
        args=[proj, proj, proj, proj, proj, bm], sem=("parallel", "arbitrary"), comm=comm)


def attn_bwd(proj, ya, dya, lse, bm, D, comm=None):
    S = proj.shape[0]
    H = D // HEAD_DIM_A
    tq = _attn_tq(S)
    nb = S // tq
    scale = HEAD_DIM_A ** -0.5
    sub, n_sub = _attn_subblocks(tq)

    def body(q_ref, kp_ref, kc_ref, vp_ref, vc_ref, o_ref, do_ref, lse_ref, bm_ref,
             dq_ref, dkc_ref, dkp_ref, dvc_ref, dvp_ref, ds_ref):
        i = pl.program_id(1)
        per = tq // sub

        @pl.when(i == 0)
        def _():
            ds_ref[...] = jnp.zeros_like(ds_ref)

        dk_acc = [None] * (2 * per)
        dv_acc = [None] * (2 * per)
        for qh in range(per):
            rows = slice(qh * sub, (qh + 1) * sub)
            q = q_ref[rows, :]
            do = do_ref[rows, :]
            delta = jnp.sum(do.astype(F32) * o_ref[rows, :].astype(F32), axis=-1, keepdims=True)
            lse_v = lse_ref[rows, :]
            dq = None
            for kb in range(qh, qh + n_sub):
                k_ref, v_ref, kr = (kp_ref, vp_ref, kb) if kb < per else (kc_ref, vc_ref, kb - per)
                k = k_ref[kr * sub:(kr + 1) * sub, :]
                cols = slice(kb * sub, (kb + 1) * sub)
                s = _dot(q, k, NT) * scale + bm_ref[rows, cols]
                if kb < per:
                    s = jnp.where(i == 0, NEG, s)
                p = jnp.exp(s - lse_v)
                dv = _dot(p.astype(BF16), do, TN)
                dp = _dot(do, v_ref[kr * sub:(kr + 1) * sub, :], NT)
                ds = p * (dp - delta)
                dsb = ds.astype(BF16)
                t = _dot(dsb, k, NN)
                dq = t if dq is None else dq + t
                dk = _dot(dsb, q, TN)
                dk_acc[kb] = dk if dk_acc[kb] is None else dk_acc[kb] + dk
                dv_acc[kb] = dv if dv_acc[kb] is None else dv_acc[kb] + dv
                ds_ref[rows, cols] += ds
            dq_ref[rows, :] = (dq * scale).astype(dq_ref.dtype)
        for kb in range(2 * per):
            dk_ref, dv_ref, kr = (dkp_ref, dvp_ref, kb) if kb < per else (dkc_ref, dvc_ref, kb - per)
            dk_ref[kr * sub:(kr + 1) * sub, :] = (dk_acc[kb] * scale).astype(dk_ref.dtype)
            dv_ref[kr * sub:(kr + 1) * sub, :] = dv_acc[kb].astype(dv_ref.dtype)

    def col(base):
        return (pl.BlockSpec((tq, HEAD_DIM_A), lambda h, i: (jnp.maximum(i - 1, 0), base + h)),
                pl.BlockSpec((tq, HEAD_DIM_A), lambda h, i: (i, base + h)))

    kp, kc = col(H)
    vp, vc = col(2 * H)
    blk = pl.BlockSpec((tq, HEAD_DIM_A), lambda h, i: (i, h))
    sd = jax.ShapeDtypeStruct((S, D), BF16)
    return _call(
        body, name="attn_bwd", grid=(H, nb),
        in_specs=[blk, kp, kc, vp, vc, blk, blk,
                  pl.BlockSpec((None, tq, 1), lambda h, i: (h, i, 0)),
                  pl.BlockSpec((None, tq, 2 * tq), lambda h, i: (h, 0, 0))],
        out_specs=[blk, blk, blk, blk, blk, pl.BlockSpec((None, tq, 2 * tq), lambda h, i: (h, 0, 0))],
        out_shape=[sd, sd, sd, sd, sd, jax.ShapeDtypeStruct((H, tq, 2 * tq), F32)],
        args=[proj, proj, proj, proj, proj, ya, dya, lse, bm], sem=("parallel", "arbitrary"), comm=comm)


def rel_bias_grad(ds_sum):
    H, tq, w = ds_sum.shape
    nbin = 2 * MAX_REL + 1
    nbin_pad = 3 * LANES
    d_lo, d_hi = -(CHUNK - 1), (N_PAST_CHUNKS + 1) * CHUNK - 1
    assert d_hi - d_lo + 1 <= w
    onehot = np.zeros((w, nbin_pad), np.float32)
    for d in range(d_lo, d_hi + 1):
        onehot[(tq - d) % w, int(np.clip(d, -MAX_REL, MAX_REL)) + MAX_REL] = 1.0
    nbits = int(np.log2(tq))
    assert (1 << nbits) == tq

    def body(ds_ref, m_ref, o_ref):
        x = ds_ref[...]
        row = lax.broadcasted_iota(jnp.int32, x.shape, 0)
        for b in range(nbits):
            rolled = pltpu.roll(x, w - (1 << b), 1)
            x = jnp.where(((row >> b) & 1) == 1, rolled, x)
        t = jnp.sum(x, axis=0, keepdims=True)
        o_ref[...] = lax.dot_general(t, m_ref[...], (NN, ((), ())), precision=lax.Precision.HIGHEST,
                                     preferred_element_type=F32)

    out = pl.pallas_call(
        body, grid=(H,),
        in_specs=[pl.BlockSpec((None, tq, w), lambda h: (h, 0, 0)),
                  pl.BlockSpec((w, nbin_pad), lambda h: (0, 0))],
        out_specs=pl.BlockSpec((None, 1, nbin_pad), lambda h: (h, 0, 0)),
        out_shape=jax.ShapeDtypeStruct((H, 1, nbin_pad), F32),
        name="rel_bias_grad", compiler_params=_cparams("parallel"))(ds_sum, jnp.asarray(onehot))
    return out[:, 0, :nbin]


def _conv_t(S):
    return _pick(S, (256,))


ROW_CHUNK = 16


def _row_loop(n_rows, step):
    def one(r, carry):
        step(pl.ds(pl.multiple_of(r * ROW_CHUNK, ROW_CHUNK), ROW_CHUNK))
        return carry

    lax.fori_loop(0, n_rows // ROW_CHUNK, one, 0)


def _fill_zbuf(zbuf, ap_ref, bp_ref, a_ref, b_ref, i):
    zp = ap_ref[...].astype(F32) * _sigmoid(bp_ref[...].astype(F32))
    zbuf[0:CONV_HALO, :] = jnp.where(i == 0, 0.0, zp)

    def step(rows):
        below = pl.ds(pl.multiple_of(rows.start + CONV_HALO, ROW_CHUNK), ROW_CHUNK)
        zbuf[below, :] = a_ref[rows, :].astype(F32) * _sigmoid(b_ref[rows, :].astype(F32))

    _row_loop(a_ref.shape[0], step)


def _shifted_windows(buf, shifted, lanes, T):
    rows = T + CONV_HALO - SUBLANES
    for b in range(1, SUBLANES):
        shifted[b - 1] = buf[pl.ds(b, rows), lanes]

    def window(off, r0=0, n=T):
        a, b = divmod(off, SUBLANES)
        if b == 0:
            return buf[pl.ds(r0 + off, n), lanes]
        return shifted[b - 1, pl.ds(r0 + a * SUBLANES, n), :]

    return window


def _shifted_scratch(T):
    return pltpu.VMEM((SUBLANES - 1, T + CONV_HALO - SUBLANES, LANES), F32)


def conv_gate_fwd(proj, ya, cw, cb, lng, lnb, D, comm=None):
    S = proj.shape[0]
    T = _conv_t(S)
    hb = T // CONV_HALO
    nlb = D // LANES

    def body(ap_ref, bp_ref, a_ref, b_ref, ga_ref, gb_ref, ya_ref, cw_ref, cb_ref, lng_ref, lnb_ref,
             y_ref, c_ref, zbuf, zsh):
        i = pl.program_id(0)
        _fill_zbuf(zbuf, ap_ref, bp_ref, a_ref, b_ref, i)

        def lane_block(lb, carry):
            lanes = pl.ds(pl.multiple_of(lb * LANES, LANES), LANES)
            z_at = _shifted_windows(zbuf, zsh, lanes, T)
            acc = jnp.zeros((T, LANES), F32)
            for k in range(CONV_WIDTH):
                acc = acc + cw_ref[k:k + 1, lanes] * z_at(CONV_HALO - CONV_WIDTH + 1 + k)
            c_ref[:, lanes] = acc + cb_ref[:, lanes]
            return carry

        lax.fori_loop(0, nlb, lane_block, 0)

        def norm_and_gate(rows):
            c = c_ref[rows, :]
            mu = jnp.mean(c, axis=-1, keepdims=True)
            xc = c - mu
            rstd = lax.rsqrt(jnp.mean(xc * xc, axis=-1, keepdims=True) + EPS)
            ln = xc * rstd * lng_ref[...] + lnb_ref[...]
            yb = ln * _sigmoid(ln)
            ga = ga_ref[rows, :].astype(F32)
            gb = gb_ref[rows, :].astype(F32)
            y_ref[rows, :D] = (ya_ref[rows, :].astype(F32) * (ga * _sigmoid(ga))).astype(y_ref.dtype)
            y_ref[rows, D:] = (yb * (gb * _sigmoid(gb))).astype(y_ref.dtype)

        _row_loop(T, norm_and_gate)

    def cur(cidx):
        return pl.BlockSpec((T, D), lambda i: (i, cidx))

    def prev(cidx):
        return pl.BlockSpec((CONV_HALO, D), lambda i: (jnp.maximum(i * hb - 1, 0), cidx))

    vec = pl.BlockSpec((1, D), lambda i: (0, 0))
    return _call(
        body, name="conv_gate_fwd", grid=(S // T,),
        in_specs=[prev(3), prev(4), cur(3), cur(4), cur(5), cur(6), pl.BlockSpec((T, D), lambda i: (i, 0)),
                  pl.BlockSpec((CONV_HALO, D), lambda i: (0, 0)), vec, vec, vec],
        out_specs=[pl.BlockSpec((T, 2 * D), lambda i: (i, 0)), pl.BlockSpec((T, D), lambda i: (i, 0))],
        out_shape=[jax.ShapeDtypeStruct((S, 2 * D), BF16), jax.ShapeDtypeStruct((S, D), F32)],
        scratch_shapes=[pltpu.VMEM((T + CONV_HALO, D), F32), _shifted_scratch(T)],
        args=[proj, proj, proj, proj, proj, proj, ya, cw, cb, lng, lnb], sem=("parallel",), comm=comm)


def conv_gate_bwd_a(dy0, proj, ya, cpre, lng, lnb, D):
    S = proj.shape[0]
    T = _conv_t(S)

    def body(dy_ref, ga_ref, gb_ref, ya_ref, c_ref, lng_ref, lnb_ref,
             dya_ref, dg_ref, dc_ref, dlng_ref, dlnb_ref):
        i = pl.program_id(0)

        c = c_ref[...]
        gv = lng_ref[...]
        mu = jnp.mean(c, axis=-1, keepdims=True)
        xc = c - mu
        rstd = lax.rsqrt(jnp.mean(xc * xc, axis=-1, keepdims=True) + EPS)
        xhat = xc * rstd
        ln = xhat * gv + lnb_ref[...]
        sl = _sigmoid(ln)
        yb = ln * sl
        ga = ga_ref[...].astype(F32)
        gb = gb_ref[...].astype(F32)
        sa = _sigmoid(ga)
        sb = _sigmoid(gb)
        dy_a = dy_ref[:, :D].astype(F32)
        dy_b = dy_ref[:, D:].astype(F32)
        dya_ref[...] = (dy_a * (ga * sa)).astype(dya_ref.dtype)
        dg_ref[:, :D] = (dy_a * ya_ref[...].astype(F32) * (sa * (1.0 + ga * (1.0 - sa)))).astype(dg_ref.dtype)
        dg_ref[:, D:] = (dy_b * yb * (sb * (1.0 + gb * (1.0 - sb)))).astype(dg_ref.dtype)
        dln = dy_b * (gb * sb) * (sl * (1.0 + ln * (1.0 - sl)))
        dxhat = dln * gv
        dc_ref[...] = rstd * (dxhat - jnp.mean(dxhat, axis=-1, keepdims=True)
                              - xhat * jnp.mean(dxhat * xhat, axis=-1, keepdims=True))
        dlng = jnp.sum(dln * xhat, axis=0, keepdims=True)
        dlnb = jnp.sum(dln, axis=0, keepdims=True)

        @pl.when(i == 0)
        def _():
            dlng_ref[...] = dlng
            dlnb_ref[...] = dlnb

        @pl.when(i > 0)
        def _():
            dlng_ref[...] += dlng
            dlnb_ref[...] += dlnb

    row = pl.BlockSpec((T, D), lambda i: (i, 0))
    vec = pl.BlockSpec((1, D), lambda i: (0, 0))
    return pl.pallas_call(
        body, grid=(S // T,),
        in_specs=[pl.BlockSpec((T, 2 * D), lambda i: (i, 0)),
                  pl.BlockSpec((T, D), lambda i: (i, 5)), pl.BlockSpec((T, D), lambda i: (i, 6)),
                  row, row, vec, vec],
        out_specs=[row, pl.BlockSpec((T, 2 * D), lambda i: (i, 0)), row, vec, vec],
        out_shape=[jax.ShapeDtypeStruct((S, D), BF16), jax.ShapeDtypeStruct((S, 2 * D), BF16),
                   jax.ShapeDtypeStruct((S, D), F32), jax.ShapeDtypeStruct((1, D), F32),
                   jax.ShapeDtypeStruct((1, D), F32)],
        name="conv_gate_bwd_a", compiler_params=_cparams("arbitrary"))(
            dy0, proj, proj, ya, cpre, lng, lnb)


def conv_gate_bwd_b(dc, proj, cw, D, comm=None):
    S = proj.shape[0]
    T = _conv_t(S)
    hb = T // CONV_HALO
    nt = S // T
    nlb = D // LANES
    half = T // 2

    def body(dc_ref, dn_ref, ap_ref, bp_ref, a_ref, b_ref, cw_ref, da_ref, db_ref, dcw_ref, dcb_ref,
             zbuf, dcbuf, zsh, dcsh, dcw8):
        i = pl.program_id(0)
        _fill_zbuf(zbuf, ap_ref, bp_ref, a_ref, b_ref, i)
        dcv = dc_ref[...]
        dcbuf[0:T, :] = dcv
        dcbuf[T:, :] = jnp.where(i == nt - 1, 0.0, dn_ref[...])

        @pl.when(i == 0)
        def _():
            dcw8[...] = jnp.zeros_like(dcw8)
            dcb_ref[...] = jnp.zeros_like(dcb_ref)

        dcb_ref[...] += jnp.sum(dcv, axis=0, keepdims=True)

        def lane_block(lb, carry):
            lanes = pl.ds(pl.multiple_of(lb * LANES, LANES), LANES)
            z_at = _shifted_windows(zbuf, zsh, lanes, T)
            dc_at = _shifted_windows(dcbuf, dcsh, lanes, T)
            for r0 in range(0, T, half):
                d0 = dcbuf[r0:r0 + half, lanes]
                dz = jnp.zeros((half, LANES), F32)
                for k in range(CONV_WIDTH):
                    dz = dz + cw_ref[k:k + 1, lanes] * dc_at(CONV_WIDTH - 1 - k, r0, half)
                    prod = d0 * z_at(CONV_HALO - CONV_WIDTH + 1 + k, r0, half)
                    dcw8[pl.ds(k * SUBLANES, SUBLANES), lanes] += jnp.sum(
                        prod.reshape(half // SUBLANES, SUBLANES, LANES), axis=0)
                av = a_ref[r0:r0 + half, lanes].astype(F32)
                sg = _sigmoid(b_ref[r0:r0 + half, lanes].astype(F32))
                da_ref[r0:r0 + half, lanes] = (dz * sg).astype(da_ref.dtype)
                db_ref[r0:r0 + half, lanes] = (dz * av * sg * (1.0 - sg)).astype(db_ref.dtype)
            return carry

        lax.fori_loop(0, nlb, lane_block, 0)

        @pl.when(i == nt - 1)
        def _():
            dcw_ref[...] = jnp.sum(dcw8[...].reshape(CONV_HALO, SUBLANES, D), axis=1)

    def cur(cidx):
        return pl.BlockSpec((T, D), lambda i: (i, cidx))

    def prev(cidx):
        return pl.BlockSpec((CONV_HALO, D), lambda i: (jnp.maximum(i * hb - 1, 0), cidx))

    row = pl.BlockSpec((T, D), lambda i: (i, 0))
    nxt = pl.BlockSpec((CONV_HALO, D), lambda i: (jnp.minimum((i + 1) * hb, nt * hb - 1), 0))
    return _call(
        body, name="conv_gate_bwd_b", grid=(nt,),
        in_specs=[row, nxt, prev(3), prev(4), cur(3), cur(4), pl.BlockSpec((CONV_HALO, D), lambda i: (0, 0))],
        out_specs=[row, row, pl.BlockSpec((CONV_HALO, D), lambda i: (0, 0)),
                   pl.BlockSpec((1, D), lambda i: (0, 0))],
        out_shape=[jax.ShapeDtypeStruct((S, D), BF16), jax.ShapeDtypeStruct((S, D), BF16),
                   jax.ShapeDtypeStruct((CONV_HALO, D), F32), jax.ShapeDtypeStruct((1, D), F32)],
        scratch_shapes=[pltpu.VMEM((T + CONV_HALO, D), F32), pltpu.VMEM((T + CONV_HALO, D), F32),
                        _shifted_scratch(T), _shifted_scratch(T), pltpu.VMEM((CONV_HALO * SUBLANES, D), F32)],
        args=[dc, dc, proj, proj, proj, proj, cw], sem=("arbitrary",), comm=comm)


def assemble_dproj0(dq, dkc, dkp, dvc, dvp, da, db, dgate, D):
    S = dq.shape[0]
    tq = _attn_tq(S)
    T = _pick(S, (256,))
    shift = tq // T
    nt = S // T

    def body(dq_ref, dkc_ref, dkp_ref, dvc_ref, dvp_ref, da_ref, db_ref, dg_ref, o_ref):
        i = pl.program_id(0)
        last = i + shift >= nt
        o_ref[:, 0:D] = dq_ref[...]
        dk = dkc_ref[...].astype(F32) + jnp.where(last, 0.0, dkp_ref[...].astype(F32))
        dv = dvc_ref[...].astype(F32) + jnp.where(last, 0.0, dvp_ref[...].astype(F32))
        o_ref[:, D:2 * D] = dk.astype(o_ref.dtype)
        o_ref[:, 2 * D:3 * D] = dv.astype(o_ref.dtype)
        o_ref[:, 3 * D:4 * D] = da_ref[...]
        o_ref[:, 4 * D:5 * D] = db_ref[...]
        o_ref[:, 5 * D:] = dg_ref[...]

    row = pl.BlockSpec((T, D), lambda i: (i, 0))
    nxt = pl.BlockSpec((T, D), lambda i: (jnp.minimum(i + shift, nt - 1), 0))
    return pl.pallas_call(
        body, grid=(nt,),
        in_specs=[row, row, nxt, row, nxt, row, row, pl.BlockSpec((T, 2 * D), lambda i: (i, 0))],
        out_specs=pl.BlockSpec((T, 7 * D), lambda i: (i, 0)),
        out_shape=jax.ShapeDtypeStruct((S, 7 * D), BF16),
        name="assemble_dproj0", compiler_params=_cparams("parallel"))(dq, dkc, dkp, dvc, dvp, da, db, dgate)


def _sgu_t(S):
    return _pick(S, (256, 128))


def _ws_masked(ws_ref, g):
    row = lax.broadcasted_iota(jnp.int32, (GMLP_CHUNK, GMLP_CHUNK), 0) // CHUNK
    col = lax.broadcasted_iota(jnp.int32, (GMLP_CHUNK, GMLP_CHUNK), 1) // CHUNK
    return jnp.where(row >= col, ws_ref[g], 0.0), row >= col


def sgu_fwd(proj, lng, lnb, ws, bst, MIX):
    S = proj.shape[0]
    T = _sgu_t(S)
    gw = MIX // N_GROUPS_C

    def body(u_ref, v_ref, g_ref, lng_ref, lnb_ref, ws_ref, bst_ref, y_ref):
        v = v_ref[...].astype(F32)
        mu = jnp.mean(v, axis=-1, keepdims=True)
        xc = v - mu
        rstd = lax.rsqrt(jnp.mean(xc * xc, axis=-1, keepdims=True) + EPS)
        for g in range(N_GROUPS_C):
            cols = slice(g * gw, (g + 1) * gw)
            wsm = _ws_masked(ws_ref, g)[0].astype(BF16)
            vn = (xc[:, cols] * rstd * lng_ref[:, cols] + lnb_ref[:, cols]).astype(BF16)
            for blk in range(T // GMLP_CHUNK):
                rows = slice(blk * GMLP_CHUNK, (blk + 1) * GMLP_CHUNK)
                sg = _dot(wsm, vn[rows], NN) + bst_ref[:, g:g + 1]
                gate = g_ref[rows, cols].astype(F32)
                y = u_ref[rows, cols].astype(F32) * sg * (gate * _sigmoid(gate))
                y_ref[rows, cols] = y.astype(y_ref.dtype)

    def part(cidx):
        return pl.BlockSpec((T, MIX), lambda i: (i, cidx))

    vec = pl.BlockSpec((1, MIX), lambda i: (0, 0))
    return pl.pallas_call(
        body, grid=(S // T,),
        in_specs=[part(0), part(1), part(2), vec, vec,
                  pl.BlockSpec((N_GROUPS_C, GMLP_CHUNK, GMLP_CHUNK), lambda i: (0, 0, 0)),
                  pl.BlockSpec((GMLP_CHUNK, N_GROUPS_C), lambda i: (0, 0))],
        out_specs=pl.BlockSpec((T, MIX), lambda i: (i, 0)),
        out_shape=jax.ShapeDtypeStruct((S, MIX), BF16),
        name="sgu_fwd", compiler_params=_cparams("parallel"))(proj, proj, proj, lng, lnb, ws, bst)


def sgu_bwd(dy1, proj, lng, lnb, ws, bst, MIX):
    S = proj.shape[0]
    T = _sgu_t(S)
    gw = MIX // N_GROUPS_C

    def body(dy_ref, u_ref, v_ref, g_ref, lng_ref, lnb_ref, ws_ref, bst_ref,
             dp_ref, dws_ref, dbst_ref, dlng_ref, dlnb_ref, dvn_buf):
        i = pl.program_id(0)

        @pl.when(i == 0)
        def _():
            dws_ref[...] = jnp.zeros_like(dws_ref)
            dbst_ref[...] = jnp.zeros_like(dbst_ref)
            dlng_ref[...] = jnp.zeros_like(dlng_ref)
            dlnb_ref[...] = jnp.zeros_like(dlnb_ref)

        v = v_ref[...].astype(F32)
        mu = jnp.mean(v, axis=-1, keepdims=True)
        xc = v - mu
        rstd = lax.rsqrt(jnp.mean(xc * xc, axis=-1, keepdims=True) + EPS)
        for g in range(N_GROUPS_C):
            cols = slice(g * gw, (g + 1) * gw)
            wsf, keep = _ws_masked(ws_ref, g)
            wsm = wsf.astype(BF16)
            vn = (xc[:, cols] * rstd * lng_ref[:, cols] + lnb_ref[:, cols]).astype(BF16)
            for blk in range(T // GMLP_CHUNK):
                rows = slice(blk * GMLP_CHUNK, (blk + 1) * GMLP_CHUNK)
                vnb = vn[rows]
                sg = _dot(wsm, vnb, NN) + bst_ref[:, g:g + 1]
                gate = g_ref[rows, cols].astype(F32)
                sig = _sigmoid(gate)
                sil = gate * sig
                u = u_ref[rows, cols].astype(F32)
                dy = dy_ref[rows, cols].astype(F32)
                dp_ref[rows, g * gw:(g + 1) * gw] = (dy * sg * sil).astype(dp_ref.dtype)
                dp_ref[rows, 2 * MIX + g * gw:2 * MIX + (g + 1) * gw] = (
                    dy * u * sg * (sig * (1.0 + gate * (1.0 - sig)))).astype(dp_ref.dtype)
                dsg = dy * u * sil
                dsgb = dsg.astype(BF16)
                dvn_buf[rows, cols] = _dot(wsm, dsgb, TN)
                dws_ref[g] += jnp.where(keep, _dot(dsgb, vnb, NT), 0.0)
                dbst_ref[:, g:g + 1] += jnp.sum(dsg, axis=-1, keepdims=True)
        dvn = dvn_buf[...]
        xhat = xc * rstd
        dxhat = dvn * lng_ref[...]
        dv = rstd * (dxhat - jnp.mean(dxhat, axis=-1, keepdims=True)
                     - xhat * jnp.mean(dxhat * xhat, axis=-1, keepdims=True))
        dp_ref[:, MIX:2 * MIX] = dv.astype(dp_ref.dtype)
        dlng_ref[...] += jnp.sum(dvn * xhat, axis=0, keepdims=True)
        dlnb_ref[...] += jnp.sum(dvn, axis=0, keepdims=True)

    def part(cidx):
        return pl.BlockSpec((T, MIX), lambda i: (i, cidx))

    vec = pl.BlockSpec((1, MIX), lambda i: (0, 0))
    wspec = pl.BlockSpec((N_GROUPS_C, GMLP_CHUNK, GMLP_CHUNK), lambda i: (0, 0, 0))
    bspec = pl.BlockSpec((GMLP_CHUNK, N_GROUPS_C), lambda i: (0, 0))
    return pl.pallas_call(
        body, grid=(S // T,),
        in_specs=[pl.BlockSpec((T, MIX), lambda i: (i, 0)), part(0), part(1), part(2), vec, vec, wspec, bspec],
        out_specs=[pl.BlockSpec((T, 3 * MIX), lambda i: (i, 0)), wspec, bspec, vec, vec],
        out_shape=[jax.ShapeDtypeStruct((S, 3 * MIX), BF16),
                   jax.ShapeDtypeStruct((N_GROUPS_C, GMLP_CHUNK, GMLP_CHUNK), F32),
                   jax.ShapeDtypeStruct((GMLP_CHUNK, N_GROUPS_C), F32),
                   jax.ShapeDtypeStruct((1, MIX), F32), jax.ShapeDtypeStruct((1, MIX), F32)],
        scratch_shapes=[pltpu.VMEM((T, MIX), F32)],
        name="sgu_bwd", compiler_params=_cparams("arbitrary"))(dy1, proj, proj, proj, lng, lnb, ws, bst)


def xattn_fwd(name, q, k, v):
    S, D = q.shape
    nm = k.shape[0]
    dh = D // N_HEADS_X
    tq = _pick(S, (512, 256))
    scale = dh ** -0.5

    def body(q_ref, k_ref, v_ref, o_ref, lse_ref):
        s = _dot(q_ref[...], k_ref[...], NT) * scale
        m = jnp.max(s, axis=-1, keepdims=True)
        p = jnp.exp(s - m)
        l = jnp.sum(p, axis=-1, keepdims=True)
        o_ref[...] = (_dot(p.astype(BF16), v_ref[...], NN) / l).astype(o_ref.dtype)
        lse_ref[...] = m + jnp.log(l)

    return pl.pallas_call(
        body, grid=(N_HEADS_X, S // tq),
        in_specs=[pl.BlockSpec((tq, dh), lambda h, i: (i, h)),
                  pl.BlockSpec((nm, dh), lambda h, i: (0, h)), pl.BlockSpec((nm, dh), lambda h, i: (0, h))],
        out_specs=[pl.BlockSpec((tq, dh), lambda h, i: (i, h)),
                   pl.BlockSpec((None, tq, 1), lambda h, i: (h, i, 0))],
        out_shape=[jax.ShapeDtypeStruct((S, D), BF16), jax.ShapeDtypeStruct((N_HEADS_X, S, 1), F32)],
        name=name, compiler_params=_cparams("parallel", "parallel"))(q, k, v)


def xattn_bwd(name, q, k, v, o, do, lse):
    S, D = q.shape
    nm = k.shape[0]
    dh = D // N_HEADS_X
    tq = _pick(S, (512, 256))
    scale = dh ** -0.5

    def body(q_ref, k_ref, v_ref, o_ref, do_ref, lse_ref, dq_ref, dk_ref, dv_ref):
        i = pl.program_id(1)
        q_v = q_ref[...]
        k_v = k_ref[...]
        do_v = do_ref[...]
        p = jnp.exp(_dot(q_v, k_v, NT) * scale - lse_ref[...])
        delta = jnp.sum(do_v.astype(F32) * o_ref[...].astype(F32), axis=-1, keepdims=True)
        dv = _dot(p.astype(BF16), do_v, TN)
        ds = (p * (_dot(do_v, v_ref[...], NT) - delta)).astype(BF16)
        dq_ref[...] = (_dot(ds, k_v, NN) * scale).astype(dq_ref.dtype)
        dk = _dot(ds, q_v, TN) * scale

        @pl.when(i == 0)
        def _():
            dk_ref[...] = dk
            dv_ref[...] = dv

        @pl.when(i > 0)
        def _():
            dk_ref[...] += dk
            dv_ref[...] += dv

    qs = pl.BlockSpec((tq, dh), lambda h, i: (i, h))
    ks = pl.BlockSpec((nm, dh), lambda h, i: (0, h))
    return pl.pallas_call(
        body, grid=(N_HEADS_X, S // tq),
        in_specs=[qs, ks, ks, qs, qs, pl.BlockSpec((None, tq, 1), lambda h, i: (h, i, 0))],
        out_specs=[qs, ks, ks],
        out_shape=[jax.ShapeDtypeStruct((S, D), BF16), jax.ShapeDtypeStruct((nm, D), F32),
                   jax.ShapeDtypeStruct((nm, D), F32)],
        name=name, compiler_params=_cparams("parallel", "arbitrary"))(q, k, v, o, do, lse)


def adamw_rows(name, w, m, v, groups, row_off):
    L, R, C = w.shape
    assert len(groups) == L
    tr = _pick(R, tuple(t for t in (512, 256, 128, 64, 32, 16, 8) if t * C * 4 <= (1 << 20)) or (8,))
    assert row_off % tr == 0
    c1 = 1.0 - ADAM_B1 ** ADAM_STEP
    c2 = 1.0 - ADAM_B2 ** ADAM_STEP

    def body(w_ref, m_ref, v_ref, *refs):
        g_refs, (go_ref, d_ref, nm_ref, nv_ref) = refs[:L], refs[L:]
        layer = pl.program_id(0)
        gv = g_refs[0][...]
        for i in range(1, L):
            gv = jnp.where(layer == i, g_refs[i][...], gv)
        nm = ADAM_B1 * m_ref[...] + (1.0 - ADAM_B1) * gv
        nv = ADAM_B2 * v_ref[...] + (1.0 - ADAM_B2) * (gv * gv)
        go_ref[...] = gv
        d_ref[...] = -ADAM_LR * ((nm / c1) / (jnp.sqrt(nv / c2) + ADAM_EPS) + ADAM_WD * w_ref[...])
        nm_ref[...] = nm
        nv_ref[...] = nv

    blk = pl.BlockSpec((None, tr, C), lambda l, r: (l, r, 0))
    gblk = pl.BlockSpec((tr, C), lambda l, r: (row_off // tr + r, 0))
    sd = jax.ShapeDtypeStruct((L, R, C), F32)
    return pl.pallas_call(body, grid=(L, R // tr), in_specs=[blk] * 3 + [gblk] * L, out_specs=[blk] * 4,
                          out_shape=[sd] * 4, name=name,
                          compiler_params=_cparams("parallel", "parallel"))(w, m, v, *groups)


def adamw_many(name, tensors):
    n = len(tensors)
    c1 = 1.0 - ADAM_B1 ** ADAM_STEP
    c2 = 1.0 - ADAM_B2 ** ADAM_STEP

    def as2d(a):
        return a.reshape((1, -1) if a.ndim == 1 else (-1, a.shape[-1])).astype(F32)

    flat = [as2d(a) for t in tensors for a in t]

    def body(*refs):
        ins, outs = refs[:4 * n], refs[4 * n:]
        for t in range(n):
            w_ref, g_ref, m_ref, v_ref = ins[4 * t:4 * t + 4]
            d_ref, nm_ref, nv_ref = outs[3 * t:3 * t + 3]
            gv = g_ref[...]
            nm = ADAM_B1 * m_ref[...] + (1.0 - ADAM_B1) * gv
            nv = ADAM_B2 * v_ref[...] + (1.0 - ADAM_B2) * (gv * gv)
            d_ref[...] = -ADAM_LR * ((nm / c1) / (jnp.sqrt(nv / c2) + ADAM_EPS) + ADAM_WD * w_ref[...])
            nm_ref[...] = nm
            nv_ref[...] = nv

    vm = pl.BlockSpec(memory_space=pltpu.VMEM)
    shapes = [jax.ShapeDtypeStruct(flat[4 * t].shape, F32) for t in range(n) for _ in range(3)]
    res = pl.pallas_call(body, in_specs=[vm] * (4 * n), out_specs=[vm] * (3 * n), out_shape=shapes, name=name,
                         compiler_params=pltpu.CompilerParams(vmem_limit_bytes=V7X_VMEM_LIMIT))(*flat)
    return [tuple(r.reshape(tensors[t][0].shape) for r in res[3 * t:3 * t + 3]) for t in range(n)]


def add_halves(name, g4, recv, cidx):
    _, R, C = g4.shape
    rh = R // 2
    tr = _pick(rh, (256, 128, 64, 32, 16))
    nrb = rh // tr

    def body(c_ref, a_ref, b_ref, o_ref):
        o_ref[...] = (a_ref[...].astype(F32) + b_ref[...].astype(F32)).astype(o_ref.dtype)

    grid_spec = pltpu.PrefetchScalarGridSpec(
        num_scalar_prefetch=1, grid=(4, nrb),
        in_specs=[pl.BlockSpec((None, tr, C), lambda j, r, c_ref: (j, c_ref[0] * nrb + r, 0)),
                  pl.BlockSpec((None, tr, C), lambda j, r, c_ref: (j, r, 0))],
        out_specs=pl.BlockSpec((None, tr, C), lambda j, r, c_ref: (j, r, 0)))
    return pl.pallas_call(body, grid_spec=grid_spec, out_shape=jax.ShapeDtypeStruct((4, rh, C), BF16),
                          name=name, compiler_params=_cparams("parallel", "parallel"))(cidx, g4, recv)


def sum_chips(name, own, recv, place):
    _, rh, C = own.shape
    tr = _pick(rh, (256, 128, 64, 32, 16))
    nrb = rh // tr

    def body(s_ref, own_ref, recv_ref, o_ref):
        acc = own_ref[...].astype(F32)
        for k in range(N_CHIPS - 1):
            acc = acc + recv_ref[k].astype(F32)
        o_ref[...] = acc

    grid_spec = pltpu.PrefetchScalarGridSpec(
        num_scalar_prefetch=1, grid=(nrb,),
        in_specs=[pl.BlockSpec((None, tr, C), lambda r, s: (s[0], r, 0)),
                  pl.BlockSpec((N_CHIPS - 1, tr, C), lambda r, s: (0, r, 0))],
        out_specs=pl.BlockSpec((tr, C), lambda r, s: (s[1] * nrb + r, 0)))
    return pl.pallas_call(body, grid_spec=grid_spec, out_shape=jax.ShapeDtypeStruct((2 * rh, C), F32),
                          name=name, compiler_params=_cparams("parallel"))(place, own, recv)


def cast_into_slot(name, w, place):
    R, C = w.shape
    tr = _pick(R, (256, 128, 64, 32, 16))

    def body(s_ref, w_ref, o_ref):
        o_ref[...] = w_ref[...].astype(o_ref.dtype)

    grid_spec = pltpu.PrefetchScalarGridSpec(
        num_scalar_prefetch=1, grid=(R // tr,),
        in_specs=[pl.BlockSpec((tr, C), lambda r, s: (r, 0))],
        out_specs=pl.BlockSpec((None, tr, C), lambda r, s: (s[0], r, 0)))
    return pl.pallas_call(body, grid_spec=grid_spec, out_shape=jax.ShapeDtypeStruct((N_CHIPS, R, C), BF16),
                          name=name, compiler_params=_cparams("parallel"))(place, w)


def _place():
    return lax.axis_index("x"), lax.axis_index("y"), lax.axis_index("c")


_CHIP_FLIPS = ((1, 0), (0, 1), (1, 1))


def _flip(v, bit):
    return 1 - v if bit else v


HBM_SPEC = pl.BlockSpec(memory_space=pl.ANY)


def exchange_small(name, buf, reduce):
    R = buf.shape[0]

    def body(x_ref, *refs):
        if reduce:
            sum_ref, all_ref, send_sems, recv_sems, local_sem = refs
        else:
            all_ref, send_sems, recv_sems, local_sem = refs
        x, y, c = _place()
        me = 4 * x + 2 * y + c
        mine = pltpu.make_async_copy(x_ref, all_ref.at[me], local_sem)
        mine.start()
        sends = []
        for k in range(1, N_DEV):
            peer = (_flip(x, k & 4), _flip(y, k & 2), _flip(c, k & 1))
            cp = pltpu.make_async_remote_copy(src_ref=x_ref, dst_ref=all_ref.at[me], send_sem=send_sems.at[k - 1],
                                              recv_sem=recv_sems.at[k - 1], device_id=peer, device_id_type=MESH)
            cp.start()
            sends.append(cp)
        for k in range(1, N_DEV):
            peer = (_flip(x, k & 4), _flip(y, k & 2), _flip(c, k & 1))
            src = 4 * peer[0] + 2 * peer[1] + peer[2]
            pltpu.make_async_remote_copy(src_ref=x_ref, dst_ref=all_ref.at[src], send_sem=send_sems.at[k - 1],
                                         recv_sem=recv_sems.at[k - 1], device_id=peer,
                                         device_id_type=MESH).wait_recv()
        for cp in sends:
            cp.wait_send()
        mine.wait()
        if reduce:
            acc = all_ref[0]
            for d in range(1, N_DEV):
                acc = acc + all_ref[d]
            sum_ref[...] = acc

    vm = pl.BlockSpec(memory_space=pltpu.VMEM)
    sems = [pltpu.SemaphoreType.DMA((N_DEV - 1,)), pltpu.SemaphoreType.DMA((N_DEV - 1,)), pltpu.SemaphoreType.DMA]
    if reduce:
        return pl.pallas_call(
            body, in_specs=[vm], out_specs=vm, out_shape=jax.ShapeDtypeStruct((R, LANES), F32),
            scratch_shapes=[pltpu.VMEM((N_DEV, R, LANES), F32)] + sems, name=name,
            compiler_params=pltpu.CompilerParams(vmem_limit_bytes=V7X_VMEM_LIMIT))(buf)
    return pl.pallas_call(
        body, in_specs=[vm], out_specs=vm, out_shape=jax.ShapeDtypeStruct((N_DEV, R, LANES), F32),
        scratch_shapes=sems, name=name,
        compiler_params=pltpu.CompilerParams(vmem_limit_bytes=V7X_VMEM_LIMIT))(buf)


def exchange_job(buf):
    R = buf.shape[0]

    def copies(x_ref, all_ref, send_sems, recv_sems):
        x, y, c = _place()
        me = 4 * x + 2 * y + c
        sends, arrivals = [], []
        for k in range(1, N_DEV):
            peer = (_flip(x, k & 4), _flip(y, k & 2), _flip(c, k & 1))
            src = 4 * peer[0] + 2 * peer[1] + peer[2]
            sends.append(pltpu.make_async_remote_copy(
                src_ref=x_ref, dst_ref=all_ref.at[me], send_sem=send_sems.at[k - 1], recv_sem=recv_sems.at[k - 1],
                device_id=peer, device_id_type=MESH))
            arrivals.append(pltpu.make_async_remote_copy(
                src_ref=x_ref, dst_ref=all_ref.at[src], send_sem=send_sems.at[k - 1], recv_sem=recv_sems.at[k - 1],
                device_id=peer, device_id_type=MESH))
        return sends, arrivals

    def start(ins, outs, sems):
        for cp in copies(ins[0], outs[0], *sems)[0]:
            cp.start()

    def finish(ins, outs, sems):
        sends, arrivals = copies(ins[0], outs[0], *sems)
        for cp in arrivals:
            cp.wait_recv()
        for cp in sends:
            cp.wait_send()

    return _Comm([buf], [jax.ShapeDtypeStruct((N_DEV, R, LANES), F32)], {},
                 [pltpu.SemaphoreType.DMA((N_DEV - 1,)), pltpu.SemaphoreType.DMA((N_DEV - 1,))], start, finish)


def sum_devices(name, slots):
    _, R, _ = slots.shape
    tr = _pick(R, (512, 256, 128, 64, 32, 16, 8))

    def body(s_ref, o_ref):
        acc = s_ref[0]
        for d in range(1, N_DEV):
            acc = acc + s_ref[d]
        o_ref[...] = acc

    return pl.pallas_call(body, grid=(R // tr,),
                          in_specs=[pl.BlockSpec((N_DEV, tr, LANES), lambda r: (0, r, 0))],
                          out_specs=pl.BlockSpec((tr, LANES), lambda r: (r, 0)),
                          out_shape=jax.ShapeDtypeStruct((R, LANES), F32), name=name,
                          compiler_params=_cparams("parallel"))(slots)


def gather_job(slots, relay_frac=0.75, flips=None):
    n = len(slots)
    flips = flips or [tuple(range(len(_CHIP_FLIPS)))] * n

    def copies(o_refs, send_sems, recv_sems):
        x, y, c = _place()
        me = 2 * x + y
        sib = (x, y, 1 - c)
        chips = [(_flip(x, fx), _flip(y, fy)) for fx, fy in _CHIP_FLIPS]
        ici, fwd, from_sib = [], [], []
        for t in range(n):
            rh = o_refs[t].shape[1] // 2
            mine, theirs = pl.ds(c * rh, rh), pl.ds((1 - c) * rh, rh)
            for k, (px, py) in enumerate(chips):
                if k not in flips[t]:
                    continue
                own = o_refs[t].at[me, mine]
                ici.append(pltpu.make_async_remote_copy(
                    src_ref=own, dst_ref=own, send_sem=send_sems.at[t, k], recv_sem=recv_sems.at[t, k],
                    device_id=(px, py, c), device_id_type=MESH))
                landed = o_refs[t].at[2 * px + py, mine]
                arrival = pltpu.make_async_remote_copy(
                    src_ref=landed, dst_ref=landed, send_sem=send_sems.at[t, k], recv_sem=recv_sems.at[t, k],
                    device_id=(px, py, c), device_id_type=MESH)
                fwd.append((arrival, pltpu.make_async_remote_copy(
                    src_ref=landed, dst_ref=landed, send_sem=send_sems.at[t, 3 + k],
                    recv_sem=recv_sems.at[t, 3 + k], device_id=sib, device_id_type=MESH)))
                passed = o_refs[t].at[2 * px + py, theirs]
                from_sib.append(pltpu.make_async_remote_copy(
                    src_ref=passed, dst_ref=passed, send_sem=send_sems.at[t, 3 + k],
                    recv_sem=recv_sems.at[t, 3 + k], device_id=sib, device_id_type=MESH))
        return ici, fwd, from_sib

    def start(ins, o_refs, sems):
        for cp in copies(o_refs, *sems)[0]:
            cp.start()

    def relay(ins, o_refs, sems):
        for arrival, forward in copies(o_refs, *sems)[1]:
            arrival.wait_recv()
            forward.start()

    def finish(ins, o_refs, sems):
        ici, fwd, from_sib = copies(o_refs, *sems)
        for cp in from_sib:
            cp.wait_recv()
        for cp in ici:
            cp.wait_send()
        for _, forward in fwd:
            forward.wait_send()

    return _Comm(slots, [jax.ShapeDtypeStruct(s.shape, s.dtype) for s in slots], {t: t for t in range(n)},
                 [pltpu.SemaphoreType.DMA((n, 6)), pltpu.SemaphoreType.DMA((n, 6))], start, finish, relay,
                 relay_frac)


def sibling_halves_job(grads):
    n = len(grads)

    def copies(g_refs, o_refs, send_sems, recv_sems):
        x, y, c = _place()
        out = []
        for t in range(n):
            rh = g_refs[t].shape[1] // 2
            out.append(pltpu.make_async_remote_copy(
                src_ref=g_refs[t].at[:, pl.ds((1 - c) * rh, rh), :], dst_ref=o_refs[t],
                send_sem=send_sems.at[t], recv_sem=recv_sems.at[t], device_id=(x, y, 1 - c),
                device_id_type=MESH))
        return out

    def start(g_refs, o_refs, sems):
        for cp in copies(g_refs, o_refs, *sems):
            cp.start()

    def finish(g_refs, o_refs, sems):
        cps = copies(g_refs, o_refs, *sems)
        for cp in cps:
            cp.wait_recv()
        for cp in cps:
            cp.wait_send()

    return _Comm(grads, [jax.ShapeDtypeStruct((4, g.shape[1] // 2, g.shape[2]), g.dtype) for g in grads], {},
                 [pltpu.SemaphoreType.DMA((n,)), pltpu.SemaphoreType.DMA((n,))], start, finish)


def scatter_job(parts):
    n = len(parts)

    def copies(p_refs, o_refs, send_sems, recv_sems):
        x, y, c = _place()
        out = []
        for t in range(n):
            for k, (fx, fy) in enumerate(_CHIP_FLIPS):
                px, py = _flip(x, fx), _flip(y, fy)
                out.append(pltpu.make_async_remote_copy(
                    src_ref=p_refs[t].at[2 * px + py], dst_ref=o_refs[t].at[k],
                    send_sem=send_sems.at[t, k], recv_sem=recv_sems.at[t, k],
                    device_id=(px, py, c), device_id_type=MESH))
        return out

    def start(p_refs, o_refs, sems):
        for cp in copies(p_refs, o_refs, *sems):
            cp.start()

    def finish(p_refs, o_refs, sems):
        cps = copies(p_refs, o_refs, *sems)
        for cp in cps:
            cp.wait_recv()
        for cp in cps:
            cp.wait_send()

    return _Comm(parts, [jax.ShapeDtypeStruct((N_CHIPS - 1,) + p.shape[1:], p.dtype) for p in parts], {},
                 [pltpu.SemaphoreType.DMA((n, 3)), pltpu.SemaphoreType.DMA((n, 3))], start, finish)


def share_halves_job(halves):
    n = len(halves)

    def copies(o_refs, send_sems, recv_sems):
        x, y, c = _place()
        sends, arrivals = [], []
        for t in range(n):
            rh = o_refs[t].shape[0] // 2
            mine = o_refs[t].at[pl.ds(c * rh, rh)]
            theirs = o_refs[t].at[pl.ds((1 - c) * rh, rh)]
            sends.append(pltpu.make_async_remote_copy(
                src_ref=mine, dst_ref=mine, send_sem=send_sems.at[t], recv_sem=recv_sems.at[t],
                device_id=(x, y, 1 - c), device_id_type=MESH))
            arrivals.append(pltpu.make_async_remote_copy(
                src_ref=theirs, dst_ref=theirs, send_sem=send_sems.at[t], recv_sem=recv_sems.at[t],
                device_id=(x, y, 1 - c), device_id_type=MESH))
        return sends, arrivals

    def start(ins, o_refs, sems):
        for cp in copies(o_refs, *sems)[0]:
            cp.start()

    def finish(ins, o_refs, sems):
        sends, arrivals = copies(o_refs, *sems)
        for cp in arrivals:
            cp.wait_recv()
        for cp in sends:
            cp.wait_send()

    return _Comm(halves, [jax.ShapeDtypeStruct(h.shape, h.dtype) for h in halves], {t: t for t in range(n)},
                 [pltpu.SemaphoreType.DMA((n,)), pltpu.SemaphoreType.DMA((n,))], start, finish)


def _pack(arrs, row_multiple=SUBLANES):
    flat, total = [], 0
    for a in arrs:
        v = a.reshape(-1).astype(F32)
        pad = (-v.shape[0]) % (SUBLANES * LANES)
        flat.append(jnp.pad(v, (0, pad)))
        total += v.shape[0] + pad
    tail = (-total) % (row_multiple * LANES)
    if tail:
        flat.append(jnp.zeros((tail,), F32))
    return jnp.concatenate(flat).reshape(-1, LANES)


def _unpack(buf, shapes):
    out, off = [], 0
    flat = buf.reshape(-1)
    for s in shapes:
        n = int(np.prod(s))
        out.append(flat[off:off + n].reshape(s))
        off += n + ((-n) % (8 * LANES))
    return out


def _xattn_layer_fwd(tag, h, mem, gx, gmem, w):
    hx = rms_fwd(f"rms_x{tag}", h, gx)
    memn = rms_fwd(f"rms_mem{tag}", mem, gmem)
    q = mm_nn(f"xq{tag}", hx, w["q"], BF16)
    k = mm_nn(f"xk{tag}", memn, w["k"], BF16)
    v = mm_nn(f"xv{tag}", memn, w["v"], BF16)
    o, lse = xattn_fwd(f"xattn_fwd{tag}", q, k, v)
    h_out = mm_nn(f"xo{tag}", o, w["o"], F32, res=h)
    return h_out, dict(hx=hx, memn=memn, q=q, k=k, v=v, o=o, lse=lse)


def _xattn_layer_bwd(tag, dh_out, dh_out_b, h_in, mem, gx, gmem, w, sv):
    do = mm_nt(f"d_xo{tag}", dh_out_b, w["o"], BF16)
    dwo = mm_tn(f"dw_xo{tag}", sv["o"], dh_out_b)
    dq, dk, dv = xattn_bwd(f"xattn_bwd{tag}", sv["q"], sv["k"], sv["v"], sv["o"], do, sv["lse"])
    dwq = mm_tn(f"dw_xq{tag}", sv["hx"], dq)
    dhx = mm_nt(f"d_xq{tag}", dq, w["q"], BF16)
    dwk = mm_tn(f"dw_xk{tag}", sv["memn"], dk)
    dwv = mm_tn(f"dw_xv{tag}", sv["memn"], dv)
    dmk = mm_nt(f"d_xk{tag}", dk, w["k"], F32)
    dmv = mm_nt(f"d_xv{tag}", dv, w["v"], F32)
    dh_in, dh_in_b, dgx = rms_bwd(f"rms_x_bwd{tag}", h_in, gx, [dhx], dh_out)
    _, _, dgmem = rms_bwd(f"rms_mem_bwd{tag}", mem, gmem, [dmk, dmv], None)
    return dh_in, dh_in_b, dgx, dgmem, dict(q=dwq, k=dwk, v=dwv, o=dwo)


def kernel(x, mem, norm_mix_g, norm_x_g, norm_mem_g, final_norm_g, w_in_ab, rel_bias, conv_w, conv_b, conv_ln_g, conv_ln_b, w_out_ab, w_in_c, sgu_ln_g, sgu_ln_b, w_s, b_s, w_out_c, w_xq, w_xk, w_xv, w_xo, loss_target, m_norm_mix_g, m_norm_x_g, m_norm_mem_g, m_final_norm_g, m_w_in_ab, m_rel_bias, m_conv_w, m_conv_b, m_conv_ln_g, m_conv_ln_b, m_w_out_ab, m_w_in_c, m_sgu_ln_g, m_sgu_ln_b, m_w_s, m_b_s, m_w_out_c, m_w_xq, m_w_xk, m_w_xv, m_w_xo, v_norm_mix_g, v_norm_x_g, v_norm_mem_g, v_final_norm_g, v_w_in_ab, v_rel_bias, v_conv_w, v_conv_b, v_conv_ln_g, v_conv_ln_b, v_w_out_ab, v_w_in_c, v_sgu_ln_g, v_sgu_ln_b, v_w_s, v_b_s, v_w_out_c, v_w_xq, v_w_xk, v_w_xv, v_w_xo):
    S, D = x.shape[1], x.shape[2]
    MIX = 2 * D
    xs, mems, tgt = x[0], mem[0], loss_target[0]
    cx, cy, cc = _place()
    chip = 2 * cx + cy
    cidx = jnp.reshape(cc, (1,)).astype(jnp.int32)
    place = jnp.stack([chip, cc]).astype(jnp.int32)

    ro, rq = MIX // 4, D // 4
    row_sharded = [("out_ab", w_out_ab[0]), ("out_c", w_out_c[0])]
    for layer in range(2):
        for nm_, w in (("q", w_xq), ("k", w_xk), ("v", w_xv), ("o", w_xo)):
            row_sharded.append((f"x{nm_}{layer}", w[layer]))
    slots = {"in_ab": cast_into_slot("cast_in_ab", w_in_ab[0], place),
             "in_c": cast_into_slot("cast_in_c", w_in_c[0], place)}
    slots.update({nm_: cast_into_slot("cast_" + nm_, w, place) for nm_, w in row_sharded})

    small_sh = [conv_w[0], sgu_ln_g[0], sgu_ln_b[0]]
    gathered = exchange_small("gather_small", _pack(small_sh), reduce=False)
    per_chip = [_unpack(gathered[2 * j], [a.shape for a in small_sh]) for j in range(N_CHIPS)]
    conv_w_full = jnp.concatenate([p[0] for p in per_chip], axis=1)
    sgu_g_full = jnp.concatenate([p[1] for p in per_chip], axis=0).reshape(1, MIX)
    sgu_b_full = jnp.concatenate([p[2] for p in per_chip], axis=0).reshape(1, MIX)
    cw_pad = jnp.pad(conv_w_full, ((0, CONV_HALO - CONV_WIDTH), (0, 0)))
    cb = conv_b.reshape(1, D)
    clg, clb = conv_ln_g.reshape(1, D), conv_ln_b.reshape(1, D)
    ws = w_s[0]
    bst = jnp.transpose(b_s[0])
    tq = _attn_tq(S)
    bm = band_bias_table(rel_bias[0], tq)

    hn0 = rms_fwd("rms_mix0", xs, norm_mix_g[0])
    near, far, every = (0, 1), (2,), (0, 1, 2)
    proj0, (wab4,) = proj_cols_own("proj_ab_own", hn0, w_in_ab[0], place,
                                   comm=gather_job([slots["in_ab"]], relay_frac=1.0, flips=[near]))
    proj0, (wab4, w_out_ab4) = proj_cols_rest(
        "proj_ab_near", hn0, wab4, proj0, place, (2, 1),
        comm=gather_job([wab4, slots["out_ab"]], flips=[far, every]))
    proj0, got_qk = proj_cols_rest("proj_ab_far", hn0, wab4, proj0, place, (3,),
                                   comm=gather_job([slots["xq0"], slots["xk0"]]))
    (ya, lse_a), (wc4,) = attn_fwd(proj0, bm, D, comm=gather_job([slots["in_c"]]))
    (y0, cpre), got_b = conv_gate_fwd(
        proj0, ya, cw_pad, cb, clg, clb, D,
        comm=gather_job([slots["xv0"], slots["xo0"], slots["out_c"], slots["xq1"]]))
    h1, got_c = mm_nn("out_ab", y0, w_out_ab4.reshape(-1, D), F32, res=xs,
                      comm=gather_job([slots["xk1"], slots["xv1"], slots["xo1"]]))
    got = dict(zip(["xq0", "xk0", "xv0", "xo0", "out_c", "xq1", "xk1", "xv1", "xo1"], got_qk + got_b + got_c))
    wrow = {n: g.reshape(-1, g.shape[2]) for n, g in got.items()}
    wrow["out_ab"] = w_out_ab4.reshape(-1, D)
    wx = [{k: wrow[f"x{k}{layer}"] for k in "qkvo"} for layer in range(2)]
    h2, sx0 = _xattn_layer_fwd("0", h1, mems, norm_x_g[0], norm_mem_g[0], wx[0])
    hn1 = rms_fwd("rms_mix1", h2, norm_mix_g[1])
    proj1 = mm_nn_cols("proj_c", hn1, wc4, BF16)
    y1 = sgu_fwd(proj1, sgu_g_full, sgu_b_full, ws, bst, MIX)
    h3 = mm_nn("out_c", y1, wrow["out_c"], F32, res=h2)
    h4, sx1 = _xattn_layer_fwd("1", h3, mems, norm_x_g[1], norm_mem_g[1], wx[1])
    loss_row, dg_final, dh4, dh4b = loss_head("loss_head", h4, final_norm_g, tgt)

    dh3, dh3b, dgx1, dgmem1, dwx1 = _xattn_layer_bwd("1", dh4, dh4b, h3, mems, norm_x_g[1], norm_mem_g[1], wx[1], sx1)
    def stack_rows(dw_out, dwx):
        return jnp.concatenate([g.reshape(N_CHIPS, -1, g.shape[1]) for g in [dw_out] + [dwx[k] for k in "qkvo"]],
                               axis=1)

    dy1 = mm_nt("d_out_c", dh3b, wrow["out_c"], BF16)
    dw_out_c = mm_tn("dw_out_c", y1, dh3b)
    dproj1, dws, dbst, dsgu_g, dsgu_b = sgu_bwd(dy1, proj1, sgu_g_full, sgu_b_full, ws, bst, MIX)
    grp1 = stack_rows(dw_out_c, dwx1)
    dw_in_c, (sib1,) = mm_tn_cols("dw_in_c", hn1, dproj1, comm=sibling_halves_job([grp1]))
    part1 = add_halves("add_halves1", grp1, sib1, cidx)
    dhn1, (recv1, sib2) = mm_nt_cols("d_proj_c", dproj1, wc4, BF16,
                                     comm=_join(scatter_job([part1]), sibling_halves_job([dw_in_c])))
    part2 = add_halves("add_halves2", dw_in_c, sib2, cidx)
    dh2, dh2b, dgmix1 = rms_bwd("rms_mix1_bwd", h2, norm_mix_g[1], [dhn1], dh3)
    dh1, dh1b, dgx0, dgmem0, dwx0 = _xattn_layer_bwd("0", dh2, dh2b, h1, mems, norm_x_g[0], norm_mem_g[0], wx[0], sx0)
    dy0 = mm_nt("d_out_ab", dh1b, wrow["out_ab"], BF16)
    dw_out_ab = mm_tn("dw_out_ab", y0, dh1b)
    grp3 = stack_rows(dw_out_ab, dwx0)
    dya, dgate, dc, dclg, dclb = conv_gate_bwd_a(dy0, proj0, ya, cpre, clg, clb, D)
    (da, db, dcw, dcb), (recv2, sib3) = conv_gate_bwd_b(
        dc, proj0, cw_pad, D, comm=_join(scatter_job([part2]), sibling_halves_job([grp3])))
    part3 = add_halves("add_halves3", grp3, sib3, cidx)
    (dq, dkc, dkp, dvc, dvp, ds_sum), (recv3,) = attn_bwd(proj0, ya, dya, lse_a, bm, D, comm=scatter_job([part3]))
    drel = rel_bias_grad(ds_sum)
    dproj0 = assemble_dproj0(dq, dkc, dkp, dvc, dvp, da, db, dgate, D)
    dw_in_ab = mm_tn_cols("dw_in_ab", hn0, dproj0)
    (sib4,) = run_comm("sibling_halves4", sibling_halves_job([dw_in_ab]))
    part4 = add_halves("add_halves4", dw_in_ab, sib4, cidx)
    halves = [sum_chips(f"sum_chips{t + 1}", p, r, place)
              for t, (p, r) in enumerate(((part1, recv1), (part2, recv2), (part3, recv3)))]
    small_early = [
        jnp.concatenate([dgx0, dgx1], axis=0), jnp.concatenate([dgmem0, dgmem1], axis=0), dg_final.reshape(D),
        drel[None], dcb, dclg, dclb, dws[None], jnp.transpose(dbst)[None],
        dcw[:CONV_WIDTH][None], dsgu_g, dsgu_b]
    early = _pack(small_early, row_multiple=512)
    dhn0, (recv4, small_slots, g_r1, g_c, g_r0) = mm_nt_cols(
        "d_proj_ab", dproj0, wab4, BF16,
        comm=_join(scatter_job([part4]), exchange_job(early), share_halves_job(halves)))
    dx, _, dgmix0 = rms_bwd("rms_mix0_bwd", xs, norm_mix_g[0], [dhn0], dh1)
    (g_ab,) = run_comm("share_reduced_half4", share_halves_job([sum_chips("sum_chips4", part4, recv4, place)]))

    me = 4 * cx + 2 * cy + cc
    small_slots = lax.dynamic_update_slice(small_slots, early[None], (me, 0, 0))
    summed = _unpack(sum_devices("sum_small", small_slots), [a.shape for a in small_early])
    (g_norm_x, g_norm_mem, g_final, g_rel, g_conv_b, g_clg, g_clb, g_ws, g_bs,
     g_conv_w_full, g_sgu_g_full, g_sgu_b_full) = summed
    dgmix = jnp.concatenate([dgmix0, dgmix1], axis=0)
    (g_norm_mix,) = _unpack(exchange_small("reduce_late", _pack([dgmix]), reduce=True), [dgmix.shape])
    cws = conv_w.shape[2]
    g_conv_w = lax.dynamic_slice_in_dim(g_conv_w_full, chip * cws, cws, axis=2)
    sgs = sgu_ln_g.shape[1]
    g_sgu_g = lax.dynamic_slice_in_dim(g_sgu_g_full, chip * sgs, sgs, axis=1)
    g_sgu_b = lax.dynamic_slice_in_dim(g_sgu_b_full, chip * sgs, sgs, axis=1)

    loss = lax.psum(loss_row[0, 0], ("x", "y", "c"))

    big_grads = {"w_in_ab": ([g_ab], 0), "w_in_c": ([g_c], 0), "w_out_ab": ([g_r0], 0), "w_out_c": ([g_r1], 0)}
    for i, nm_ in enumerate("qkvo"):
        big_grads["w_x" + nm_] = ([g_r0, g_r1], ro + i * rq)
    grads = dict(
        norm_mix_g=g_norm_mix, norm_x_g=g_norm_x, norm_mem_g=g_norm_mem, final_norm_g=g_final,
        rel_bias=g_rel, conv_w=g_conv_w, conv_b=g_conv_b, conv_ln_g=g_clg, conv_ln_b=g_clb,
        sgu_ln_g=g_sgu_g, sgu_ln_b=g_sgu_b, w_s=g_ws, b_s=g_bs)
    weights = dict(
        norm_mix_g=(norm_mix_g, m_norm_mix_g, v_norm_mix_g), norm_x_g=(norm_x_g, m_norm_x_g, v_norm_x_g),
        norm_mem_g=(norm_mem_g, m_norm_mem_g, v_norm_mem_g), final_norm_g=(final_norm_g, m_final_norm_g, v_final_norm_g),
        w_in_ab=(w_in_ab, m_w_in_ab, v_w_in_ab), rel_bias=(rel_bias, m_rel_bias, v_rel_bias),
        conv_w=(conv_w, m_conv_w, v_conv_w), conv_b=(conv_b, m_conv_b, v_conv_b),
        conv_ln_g=(conv_ln_g, m_conv_ln_g, v_conv_ln_g), conv_ln_b=(conv_ln_b, m_conv_ln_b, v_conv_ln_b),
        w_out_ab=(w_out_ab, m_w_out_ab, v_w_out_ab), w_in_c=(w_in_c, m_w_in_c, v_w_in_c),
        sgu_ln_g=(sgu_ln_g, m_sgu_ln_g, v_sgu_ln_g), sgu_ln_b=(sgu_ln_b, m_sgu_ln_b, v_sgu_ln_b),
        w_s=(w_s, m_w_s, v_w_s), b_s=(b_s, m_b_s, v_b_s), w_out_c=(w_out_c, m_w_out_c, v_w_out_c),
        w_xq=(w_xq, m_w_xq, v_w_xq), w_xk=(w_xk, m_w_xk, v_w_xk), w_xv=(w_xv, m_w_xv, v_w_xv),
        w_xo=(w_xo, m_w_xo, v_w_xo))
    names = list(weights)
    delta, new_m, new_v = {}, {}, {}
    for nm_, (groups, row_off) in big_grads.items():
        w, m, v = weights[nm_]
        grads[nm_], delta[nm_], new_m[nm_], new_v[nm_] = adamw_rows("adamw_" + nm_, w, m, v, groups, row_off)
    small_names = [n for n in names if n not in big_grads]
    stepped = adamw_many("adamw_small", [(weights[n][0], grads[n].reshape(weights[n][0].shape), weights[n][1],
                                          weights[n][2]) for n in small_names])
    for n, (d_, m_, v_) in zip(small_names, stepped):
        delta[n], new_m[n], new_v[n] = d_, m_, v_

    return (loss, dx[None], *[grads[n].reshape(weights[n][0].shape) for n in names], *[delta[n] for n in names],
            *[new_m[n] for n in names], *[new_v[n] for n in names])
```

```python
import functools

import numpy as np
import jax
import jax.numpy as jnp
from jax import lax
from jax.experimental import pallas as pl
from jax.experimental.pallas import tpu as pltpu

F32 = jnp.float32
BF16 = jnp.bfloat16
MESH = pl.DeviceIdType.MESH

EPS = 1e-6
CHUNK = 64
N_PAST_CHUNKS = 8
MAX_REL = 128
HEAD_DIM_A = 128
CONV_WIDTH = 31
CONV_HALO = 32
GMLP_CHUNK = 128
N_GROUPS_C = 8
N_HEADS_X = 4
NEG = -1e30

ADAM_LR = 0.001
ADAM_B1 = 0.9
ADAM_B2 = 0.999
ADAM_EPS = 1e-08
ADAM_WD = 0.01
ADAM_STEP = 10

N_CHIPS = 4
N_DEV = 8
V7X_VMEM_LIMIT = 56 * 1024 * 1024
LANES = 128
SUBLANES = 8


def _pick(n, cands):
    for c in cands:
        if c <= n and n % c == 0:
            return c
    return n


def _cparams(*sem):
    return pltpu.CompilerParams(dimension_semantics=sem, vmem_limit_bytes=V7X_VMEM_LIMIT)


def _sigmoid(x):
    return 0.5 * jnp.tanh(0.5 * x) + 0.5


def _dot(a, b, contract):
    return lax.dot_general(a, b, (contract, ((), ())), preferred_element_type=F32)


NN = ((1,), (0,))
NT = ((1,), (1,))
TN = ((0,), (0,))


class _Comm:
    def __init__(self, arrays, out_shapes, aliases, sems, start, finish, relay=None, relay_frac=0.75):
        self.arrays, self.out_shapes, self.aliases, self.sems = list(arrays), list(out_shapes), dict(aliases), list(sems)
        self.start, self.finish, self.relay = start, finish, relay
        self.relay_frac = relay_frac


def _join(*jobs):
    assert all(j.relay is None for j in jobs)
    arrays, outs, sems, aliases, spans = [], [], [], {}, []
    for j in jobs:
        spans.append((len(arrays), len(outs), len(sems)))
        aliases.update({len(arrays) + i: len(outs) + o for i, o in j.aliases.items()})
        arrays += j.arrays
        outs += j.out_shapes
        sems += j.sems

    def part(j, span, ins, os_, ss):
        a0, o0, s0 = span
        return (ins[a0:a0 + len(j.arrays)], os_[o0:o0 + len(j.out_shapes)], ss[s0:s0 + len(j.sems)])

    def start(ins, os_, ss):
        for j, span in zip(jobs, spans):
            j.start(*part(j, span, ins, os_, ss))

    def finish(ins, os_, ss):
        for j, span in zip(jobs, spans):
            j.finish(*part(j, span, ins, os_, ss))

    return _Comm(arrays, outs, aliases, sems, start, finish)


def _call(body, *, name, grid, in_specs, out_specs, out_shape, args, scratch_shapes=(), sem=None, comm=None,
          prefetch=None, io_aliases=None):
    multi = isinstance(out_shape, (list, tuple))
    o_shapes = list(out_shape) if multi else [out_shape]
    o_specs = list(out_specs) if multi else [out_specs]
    if comm is None:
        assert prefetch is None and io_aliases is None
        return pl.pallas_call(body, grid=grid, in_specs=in_specs, out_specs=out_specs, out_shape=out_shape,
                              scratch_shapes=list(scratch_shapes), name=name,
                              compiler_params=_cparams(*sem))(*args)
    n_in, n_out, n_scr = len(in_specs), len(o_shapes), len(scratch_shapes)
    n_ci, n_co = len(comm.arrays), len(comm.out_shapes)
    n_steps = int(np.prod(grid))
    n_pre = 0 if prefetch is None else 1

    def carrier(*refs):
        pre, refs = refs[:n_pre], refs[n_pre:]
        ins, rest = refs[:n_in], refs[n_in:]
        cins, rest = rest[:n_ci], rest[n_ci:]
        outs, rest = rest[:n_out], rest[n_out:]
        couts, rest = rest[:n_co], rest[n_co:]
        scr, csems = rest[:n_scr], rest[n_scr:]
        step = 0
        for a, g in enumerate(grid):
            step = step * g + pl.program_id(a)

        @pl.when(step == 0)
        def _():
            comm.start(cins, couts, csems)

        body(*pre, *ins, *outs, *scr)

        relay_step = min(int(comm.relay_frac * n_steps), n_steps - 1)
        if comm.relay is not None and relay_step < n_steps - 1:
            @pl.when(step == relay_step)
            def _():
                comm.relay(cins, couts, csems)

        @pl.when(step == n_steps - 1)
        def _():
            if comm.relay is not None and relay_step == n_steps - 1:
                comm.relay(cins, couts, csems)
            comm.finish(cins, couts, csems)

    aliases = {n_pre + n_in + i: n_out + o for i, o in comm.aliases.items()}
    aliases.update({n_pre + i: o for i, o in (io_aliases or {}).items()})
    all_in = list(in_specs) + [HBM_SPEC] * n_ci
    all_out = o_specs + [HBM_SPEC] * n_co
    all_scratch = list(scratch_shapes) + comm.sems
    params = _cparams(*(["arbitrary"] * len(grid)))
    if prefetch is None:
        res = pl.pallas_call(
            carrier, grid=grid, in_specs=all_in, out_specs=all_out, out_shape=o_shapes + comm.out_shapes,
            input_output_aliases=aliases, scratch_shapes=all_scratch, name=name,
            compiler_params=params)(*args, *comm.arrays)
    else:
        grid_spec = pltpu.PrefetchScalarGridSpec(num_scalar_prefetch=1, grid=grid, in_specs=all_in,
                                                 out_specs=all_out, scratch_shapes=all_scratch)
        res = pl.pallas_call(
            carrier, grid_spec=grid_spec, out_shape=o_shapes + comm.out_shapes, input_output_aliases=aliases,
            name=name, compiler_params=params)(prefetch, *args, *comm.arrays)
    mine = list(res[:n_out]) if multi else res[0]
    return mine, list(res[n_out:])


def run_comm(name, comm):
    def body(*refs):
        n_ci, n_co = len(comm.arrays), len(comm.out_shapes)
        cins, couts, csems = refs[:n_ci], refs[n_ci:n_ci + n_co], refs[n_ci + n_co:]
        comm.start(cins, couts, csems)
        if comm.relay is not None:
            comm.relay(cins, couts, csems)
        comm.finish(cins, couts, csems)

    return pl.pallas_call(
        body, in_specs=[HBM_SPEC] * len(comm.arrays), out_specs=[HBM_SPEC] * len(comm.out_shapes),
        out_shape=comm.out_shapes, input_output_aliases=comm.aliases, scratch_shapes=comm.sems,
        name=name)(*comm.arrays)


def _mm(name, a, b, *, contract, grid, a_spec, b_spec, o_spec, out_shape, res=None, comm=None):
    nk = grid[2]

    def body(*refs):
        if res is not None:
            a_ref, b_ref, r_ref, o_ref = refs[:4]
        else:
            a_ref, b_ref, o_ref = refs[:3]
            r_ref = None
        p = _dot(a_ref[...].astype(BF16), b_ref[...].astype(BF16), contract)

        def finish(acc):
            if r_ref is not None:
                acc = acc + r_ref[...]
            o_ref[...] = acc.astype(o_ref.dtype)

        if nk == 1:
            finish(p)
        else:
            acc_ref = refs[-1]
            k = pl.program_id(2)

            @pl.when(k == 0)
            def _():
                acc_ref[...] = p

            @pl.when(k > 0)
            def _():
                acc_ref[...] += p

            @pl.when(k == nk - 1)
            def _():
                finish(acc_ref[...])

    in_specs = [a_spec, b_spec]
    args = [a, b]
    if res is not None:
        in_specs.append(o_spec)
        args.append(res)
    blk = tuple(d for d in o_spec.block_shape if d is not None)
    scratch = [] if nk == 1 else [pltpu.VMEM(blk, F32)]
    return _call(body, name=name, grid=grid, in_specs=in_specs, out_specs=o_spec, out_shape=out_shape,
                 args=args, scratch_shapes=scratch, sem=("parallel", "parallel", "arbitrary"), comm=comm)


def mm_nn_cols(name, a, w4, out_dtype, comm=None):
    M, K = a.shape
    _, _, C = w4.shape
    tm = _pick(M, (1024, 512, 256))
    tn = _pick(C, (1024, 512, 256, 128))
    nps = C // tn
    return _mm(name, a, w4, contract=NN, grid=(M // tm, 4 * nps, 1),
               a_spec=pl.BlockSpec((tm, K), lambda i, j, k: (i, 0)),
               b_spec=pl.BlockSpec((None, K, tn), lambda i, j, k: (j // nps, 0, j % nps)),
               o_spec=pl.BlockSpec((tm, tn), lambda i, j, k: (i, j)),
               out_shape=jax.ShapeDtypeStruct((M, 4 * C), out_dtype), comm=comm)


def proj_cols_own(name, a, w_own, place, comm):
    M, K = a.shape
    C = w_own.shape[1]
    tm = _pick(M, (1024, 512, 256))
    tn = _pick(C, (512, 256, 128))
    nps = C // tn

    def body(s_ref, a_ref, b_ref, o_ref):
        o_ref[...] = _dot(a_ref[...], b_ref[...].astype(BF16), NN).astype(o_ref.dtype)

    return _call(body, name=name, grid=(M // tm, nps),
                 in_specs=[pl.BlockSpec((tm, K), lambda i, j, s: (i, 0)),
                           pl.BlockSpec((K, tn), lambda i, j, s: (0, j))],
                 out_specs=pl.BlockSpec((tm, tn), lambda i, j, s: (i, s[0] * nps + j)),
                 out_shape=jax.ShapeDtypeStruct((M, N_CHIPS * C), BF16), args=[a, w_own], comm=comm,
                 prefetch=place)


def proj_cols_rest(name, a, w4, partial, place, masks, comm):
    M, K = a.shape
    C = w4.shape[2]
    tm = _pick(M, (1024, 512, 256))
    tn = _pick(C, (1792, 1536, 1024, 512, 256, 128))
    nps = C // tn
    assert len(masks) in (1, 2)
    step = masks[-1] - masks[0]

    def slot(j, s):
        return jnp.bitwise_xor(s[0], masks[0] + step * (j // nps))

    def body(s_ref, a_ref, b_ref, part_ref, o_ref):
        o_ref[...] = _dot(a_ref[...], b_ref[...], NN).astype(o_ref.dtype)

    return _call(body, name=name, grid=(M // tm, len(masks) * nps),
                 in_specs=[pl.BlockSpec((tm, K), lambda i, j, s: (i, 0)),
                           pl.BlockSpec((None, K, tn), lambda i, j, s: (slot(j, s), 0, j % nps)),
                           HBM_SPEC],
                 out_specs=pl.BlockSpec((tm, tn), lambda i, j, s: (i, slot(j, s) * nps + j % nps)),
                 out_shape=jax.ShapeDtypeStruct(partial.shape, partial.dtype), args=[a, w4, partial],
                 comm=comm, prefetch=place, io_aliases={2: 0})


def mm_nn(name, a, w, out_dtype, res=None, comm=None):
    M, K = a.shape
    N = w.shape[1]
    tm = _pick(M, (1024, 512, 256))
    tn = _pick(N, (1024, 512, 256, 128) if K <= 2048 else (512, 256, 128))
    return _mm(name, a, w, contract=NN, grid=(M // tm, N // tn, 1),
               a_spec=pl.BlockSpec((tm, K), lambda i, j, k: (i, 0)),
               b_spec=pl.BlockSpec((K, tn), lambda i, j, k: (0, j)),
               o_spec=pl.BlockSpec((tm, tn), lambda i, j, k: (i, j)),
               out_shape=jax.ShapeDtypeStruct((M, N), out_dtype), res=res, comm=comm)


def mm_nt_cols(name, a, w4, out_dtype, comm=None):
    M = a.shape[0]
    _, K, C = w4.shape
    tm = _pick(M, (1024, 512, 256))
    tn = _pick(K, (1024, 512, 256, 128))
    tk = _pick(C, (3584, 3072, 1792, 1536, 1024, 512, 256, 128))
    kps = C // tk
    return _mm(name, a, w4, contract=NT, grid=(M // tm, K // tn, 4 * kps),
               a_spec=pl.BlockSpec((tm, tk), lambda i, j, k: (i, k)),
               b_spec=pl.BlockSpec((None, tn, tk), lambda i, j, k: (k // kps, j, k % kps)),
               o_spec=pl.BlockSpec((tm, tn), lambda i, j, k: (i, j)),
               out_shape=jax.ShapeDtypeStruct((M, K), out_dtype), comm=comm)


def mm_nt(name, a, w, out_dtype):
    M, C = a.shape
    N = w.shape[0]
    tm = _pick(M, (1024, 512, 256))
    tn = _pick(N, (1024, 512, 256, 128))
    return _mm(name, a, w, contract=NT, grid=(M // tm, N // tn, 1),
               a_spec=pl.BlockSpec((tm, C), lambda i, j, k: (i, 0)),
               b_spec=pl.BlockSpec((tn, C), lambda i, j, k: (j, 0)),
               o_spec=pl.BlockSpec((tm, tn), lambda i, j, k: (i, j)),
               out_shape=jax.ShapeDtypeStruct((M, N), out_dtype))


def mm_tn_cols(name, a, b, comm=None):
    S, K = a.shape
    C = b.shape[1] // 4
    ts = _pick(S, (2048, 1024, 512, 256))
    tko = _pick(K, (1024, 512, 256, 128))
    tn = _pick(C, (1792, 1536, 1024, 512, 256, 128))
    nps = C // tn
    return _mm(name, a, b, contract=TN, grid=(K // tko, 4 * nps, S // ts),
               a_spec=pl.BlockSpec((ts, tko), lambda i, j, k: (k, i)),
               b_spec=pl.BlockSpec((ts, tn), lambda i, j, k: (k, j)),
               o_spec=pl.BlockSpec((None, tko, tn), lambda i, j, k: (j // nps, i, j % nps)),
               out_shape=jax.ShapeDtypeStruct((4, K, C), BF16), comm=comm)


def mm_tn(name, a, b):
    S, K = a.shape
    N = b.shape[1]
    ts = _pick(S, (2048, 1024, 512, 256))
    tko = _pick(K, (1024, 512, 256, 128))
    tn = _pick(N, (1024, 512, 256, 128))
    return _mm(name, a, b, contract=TN, grid=(K // tko, N // tn, S // ts),
               a_spec=pl.BlockSpec((ts, tko), lambda i, j, k: (k, i)),
               b_spec=pl.BlockSpec((ts, tn), lambda i, j, k: (k, j)),
               o_spec=pl.BlockSpec((tko, tn), lambda i, j, k: (i, j)),
               out_shape=jax.ShapeDtypeStruct((K, N), BF16))


def rms_fwd(name, x, g):
    S, D = x.shape
    T = _pick(S, (512, 256))

    def body(x_ref, g_ref, o_ref):
        xf = x_ref[...]
        r = lax.rsqrt(jnp.mean(xf * xf, axis=-1, keepdims=True) + EPS)
        o_ref[...] = (xf * r * g_ref[...]).astype(o_ref.dtype)

    return pl.pallas_call(
        body, grid=(S // T,),
        in_specs=[pl.BlockSpec((T, D), lambda i: (i, 0)), pl.BlockSpec((1, D), lambda i: (0, 0))],
        out_specs=pl.BlockSpec((T, D), lambda i: (i, 0)),
        out_shape=jax.ShapeDtypeStruct((S, D), BF16), name=name,
        compiler_params=_cparams("parallel"))(x, g.reshape(1, D))


def rms_bwd(name, x, g, dys, dres):
    S, D = x.shape
    T = _pick(S, (256,))
    ndy = len(dys)
    has_res = dres is not None

    def body(*refs):
        x_ref, g_ref = refs[0], refs[1]
        dy_refs = refs[2:2 + ndy]
        r_ref = refs[2 + ndy] if has_res else None
        dx_ref, dxb_ref, dg_ref = refs[-3], refs[-2], refs[-1]
        i = pl.program_id(0)
        xf = x_ref[...]
        r = lax.rsqrt(jnp.mean(xf * xf, axis=-1, keepdims=True) + EPS)
        xhat = xf * r
        dy = dy_refs[0][...].astype(F32)
        for d in dy_refs[1:]:
            dy = dy + d[...].astype(F32)
        dxhat = dy * g_ref[...]
        dx = r * (dxhat - xhat * jnp.mean(dxhat * xhat, axis=-1, keepdims=True))
        if has_res:
            dx = dx + r_ref[...]
        dx_ref[...] = dx
        dxb_ref[...] = dx.astype(dxb_ref.dtype)
        dg = jnp.sum(dy * xhat, axis=0, keepdims=True)

        @pl.when(i == 0)
        def _():
            dg_ref[...] = dg

        @pl.when(i > 0)
        def _():
            dg_ref[...] += dg

    row = pl.BlockSpec((T, D), lambda i: (i, 0))
    vec = pl.BlockSpec((1, D), lambda i: (0, 0))
    args = [x, g.reshape(1, D), *dys] + ([dres] if has_res else [])
    return pl.pallas_call(
        body, grid=(S // T,),
        in_specs=[row, vec] + [row] * (ndy + int(has_res)),
        out_specs=[row, row, vec],
        out_shape=[jax.ShapeDtypeStruct((S, D), F32), jax.ShapeDtypeStruct((S, D), BF16),
                   jax.ShapeDtypeStruct((1, D), F32)],
        name=name, compiler_params=_cparams("arbitrary"))(*args)


def loss_head(name, h, g, target):
    S, D = h.shape
    T = _pick(S, (256,))

    def body(h_ref, g_ref, t_ref, loss_ref, dg_ref, dh_ref, dhb_ref):
        i = pl.program_id(0)
        xf = h_ref[...]
        gv = g_ref[...]
        r = lax.rsqrt(jnp.mean(xf * xf, axis=-1, keepdims=True) + EPS)
        xhat = xf * r
        err = xhat * gv - t_ref[...]
        part = 0.5 * jnp.sum(jnp.sum(err * err, axis=-1, keepdims=True), axis=0, keepdims=True) / D
        dout = err / D
        dxhat = dout * gv
        dh = r * (dxhat - xhat * jnp.mean(dxhat * xhat, axis=-1, keepdims=True))
        dh_ref[...] = dh
        dhb_ref[...] = dh.astype(dhb_ref.dtype)
        dg = jnp.sum(dout * xhat, axis=0, keepdims=True)
        lrow = jnp.broadcast_to(part, (1, LANES))

        @pl.when(i == 0)
        def _():
            dg_ref[...] = dg
            loss_ref[...] = lrow

        @pl.when(i > 0)
        def _():
            dg_ref[...] += dg
            loss_ref[...] += lrow

    row = pl.BlockSpec((T, D), lambda i: (i, 0))
    vec = pl.BlockSpec((1, D), lambda i: (0, 0))
    return pl.pallas_call(
        body, grid=(S // T,), in_specs=[row, vec, row],
        out_specs=[pl.BlockSpec((1, LANES), lambda i: (0, 0)), vec, row, row],
        out_shape=[jax.ShapeDtypeStruct((1, LANES), F32), jax.ShapeDtypeStruct((1, D), F32),
                   jax.ShapeDtypeStruct((S, D), F32), jax.ShapeDtypeStruct((S, D), BF16)],
        name=name, compiler_params=_cparams("arbitrary"))(h, g.reshape(1, D), target)


def _attn_tq(S):
    return _pick(S, (512,))


def band_bias_table(rel_bias, tq):
    H = rel_bias.shape[0]
    w = 2 * tq
    nbits = int(np.log2(tq))
    assert (1 << nbits) == tq and (N_PAST_CHUNKS + 2) * CHUNK - 1 <= w
    c = np.arange(w)
    d0 = np.where(c <= tq + CHUNK - 1, tq - c, tq + w - c)
    base = jnp.take(rel_bias.astype(F32), jnp.asarray(np.clip(d0, -MAX_REL, MAX_REL) + MAX_REL), axis=1)

    def body(b_ref, o_ref):
        x = jnp.broadcast_to(b_ref[...], (tq, w))
        row = lax.broadcasted_iota(jnp.int32, (tq, w), 0)
        col = lax.broadcasted_iota(jnp.int32, (tq, w), 1)
        for b in range(nbits):
            x = jnp.where(((row >> b) & 1) == 1, pltpu.roll(x, 1 << b, 1), x)
        qc = row // CHUNK
        kc = col // CHUNK - tq // CHUNK
        o_ref[...] = jnp.where((kc <= qc) & (kc >= qc - N_PAST_CHUNKS), x, NEG)

    return pl.pallas_call(
        body, grid=(H,), in_specs=[pl.BlockSpec((None, 1, w), lambda h: (h, 0, 0))],
        out_specs=pl.BlockSpec((None, tq, w), lambda h: (h, 0, 0)),
        out_shape=jax.ShapeDtypeStruct((H, tq, w), F32), name="band_bias_table",
        compiler_params=_cparams("parallel"))(base.reshape(H, 1, w))


def _attn_subblocks(tq):
    sub = tq // 2
    assert sub % CHUNK == 0 and N_PAST_CHUNKS * CHUNK == tq
    return sub, 3


def attn_fwd(proj, bm, D, comm=None):
    S = proj.shape[0]
    H = D // HEAD_DIM_A
    tq = _attn_tq(S)
    nb = S // tq
    scale = HEAD_DIM_A ** -0.5

    sub, n_sub = _attn_subblocks(tq)

    def body(q_ref, kp_ref, kc_ref, vp_ref, vc_ref, bm_ref, o_ref, lse_ref):
        i = pl.program_id(1)
        for qh in range(tq // sub):
            rows = slice(qh * sub, (qh + 1) * sub)
            q = q_ref[rows, :]
            ss = []
            for kb in range(qh, qh + n_sub):
                k_ref, krows = (kp_ref, kb) if kb < tq // sub else (kc_ref, kb - tq // sub)
                s = _dot(q, k_ref[krows * sub:(krows + 1) * sub, :], NT) * scale + bm_ref[rows, kb * sub:(kb + 1) * sub]
                if kb < tq // sub:
                    s = jnp.where(i == 0, NEG, s)
                ss.append(s)
            m = functools.reduce(jnp.maximum, [jnp.max(s, axis=-1, keepdims=True) for s in ss])
            ps = [jnp.exp(s - m) for s in ss]
            l = functools.reduce(jnp.add, [jnp.sum(p, axis=-1, keepdims=True) for p in ps])
            o = None
            for p, kb in zip(ps, range(qh, qh + n_sub)):
                v_ref, vrows = (vp_ref, kb) if kb < tq // sub else (vc_ref, kb - tq // sub)
                t = _dot(p.astype(BF16), v_ref[vrows * sub:(vrows + 1) * sub, :], NN)
                o = t if o is None else o + t
            o_ref[rows, :] = (o / l).astype(o_ref.dtype)
            lse_ref[rows, :] = m + jnp.log(l)

    def col(base):
        return (pl.BlockSpec((tq, HEAD_DIM_A), lambda h, i: (jnp.maximum(i - 1, 0), base + h)),
                pl.BlockSpec((tq, HEAD_DIM_A), lambda h, i: (i, base + h)))

    kp, kc = col(H)
    vp, vc = col(2 * H)
    return _call(
        body, name="attn_fwd", grid=(H, nb),
        in_specs=[pl.BlockSpec((tq, HEAD_DIM_A), lambda h, i: (i, h)), kp, kc, vp, vc,
                  pl.BlockSpec((None, tq, 2 * tq), lambda h, i: (h, 0, 0))],
        out_specs=[pl.BlockSpec((tq, HEAD_DIM_A), lambda h, i: (i, h)),
                   pl.BlockSpec((None, tq, 1), lambda h, i: (h, i, 0))],
        out_shape=[jax.ShapeDtypeStruct((S, D), BF16), jax.ShapeDtypeStruct((H, S, 1), F32)],
        args=[proj, proj, proj, proj, proj, bm], sem=("parallel", "arbitrary"), comm=comm)


def attn_bwd(proj, ya, dya, lse, bm, D, comm=None):
    S = proj.shape[0]
    H = D // HEAD_DIM_A
    tq = _attn_tq(S)
    nb = S // tq
    scale = HEAD_DIM_A ** -0.5
    sub, n_sub = _attn_subblocks(tq)

    def body(q_ref, kp_ref, kc_ref, vp_ref, vc_ref, o_ref, do_ref, lse_ref, bm_ref,
             dq_ref, dkc_ref, dkp_ref, dvc_ref, dvp_ref, ds_ref):
        i = pl.program_id(1)
        per = tq // sub

        @pl.when(i == 0)
        def _():
            ds_ref[...] = jnp.zeros_like(ds_ref)

        dk_acc = [None] * (2 * per)
        dv_acc = [None] * (2 * per)
        for qh in range(per):
            rows = slice(qh * sub, (qh + 1) * sub)
            q = q_ref[rows, :]
            do = do_ref[rows, :]
            delta = jnp.sum(do.astype(F32) * o_ref[rows, :].astype(F32), axis=-1, keepdims=True)
            lse_v = lse_ref[rows, :]
            dq = None
            for kb in range(qh, qh + n_sub):
                k_ref, v_ref, kr = (kp_ref, vp_ref, kb) if kb < per else (kc_ref, vc_ref, kb - per)
                k = k_ref[kr * sub:(kr + 1) * sub, :]
                cols = slice(kb * sub, (kb + 1) * sub)
                s = _dot(q, k, NT) * scale + bm_ref[rows, cols]
                if kb < per:
                    s = jnp.where(i == 0, NEG, s)
                p = jnp.exp(s - lse_v)
                dv = _dot(p.astype(BF16), do, TN)
                dp = _dot(do, v_ref[kr * sub:(kr + 1) * sub, :], NT)
                ds = p * (dp - delta)
                dsb = ds.astype(BF16)
                t = _dot(dsb, k, NN)
                dq = t if dq is None else dq + t
                dk = _dot(dsb, q, TN)
                dk_acc[kb] = dk if dk_acc[kb] is None else dk_acc[kb] + dk
                dv_acc[kb] = dv if dv_acc[kb] is None else dv_acc[kb] + dv
                ds_ref[rows, cols] += ds
            dq_ref[rows, :] = (dq * scale).astype(dq_ref.dtype)
        for kb in range(2 * per):
            dk_ref, dv_ref, kr = (dkp_ref, dvp_ref, kb) if kb < per else (dkc_ref, dvc_ref, kb - per)
            dk_ref[kr * sub:(kr + 1) * sub, :] = (dk_acc[kb] * scale).astype(dk_ref.dtype)
            dv_ref[kr * sub:(kr + 1) * sub, :] = dv_acc[kb].astype(dv_ref.dtype)

    def col(base):
        return (pl.BlockSpec((tq, HEAD_DIM_A), lambda h, i: (jnp.maximum(i - 1, 0), base + h)),
                pl.BlockSpec((tq, HEAD_DIM_A), lambda h, i: (i, base + h)))

    kp, kc = col(H)
    vp, vc = col(2 * H)
    blk = pl.BlockSpec((tq, HEAD_DIM_A), lambda h, i: (i, h))
    sd = jax.ShapeDtypeStruct((S, D), BF16)
    return _call(
        body, name="attn_bwd", grid=(H, nb),
        in_specs=[blk, kp, kc, vp, vc, blk, blk,
                  pl.BlockSpec((None, tq, 1), lambda h, i: (h, i, 0)),
                  pl.BlockSpec((None, tq, 2 * tq), lambda h, i: (h, 0, 0))],
        out_specs=[blk, blk, blk, blk, blk, pl.BlockSpec((None, tq, 2 * tq), lambda h, i: (h, 0, 0))],
        out_shape=[sd, sd, sd, sd, sd, jax.ShapeDtypeStruct((H, tq, 2 * tq), F32)],
        args=[proj, proj, proj, proj, proj, ya, dya, lse, bm], sem=("parallel", "arbitrary"), comm=comm)


def rel_bias_grad(ds_sum):
    H, tq, w = ds_sum.shape
    nbin = 2 * MAX_REL + 1
    nbin_pad = 3 * LANES
    d_lo, d_hi = -(CHUNK - 1), (N_PAST_CHUNKS + 1) * CHUNK - 1
    assert d_hi - d_lo + 1 <= w
    onehot = np.zeros((w, nbin_pad), np.float32)
    for d in range(d_lo, d_hi + 1):
        onehot[(tq - d) % w, int(np.clip(d, -MAX_REL, MAX_REL)) + MAX_REL] = 1.0
    nbits = int(np.log2(tq))
    assert (1 << nbits) == tq

    def body(ds_ref, m_ref, o_ref):
        x = ds_ref[...]
        row = lax.broadcasted_iota(jnp.int32, x.shape, 0)
        for b in range(nbits):
            rolled = pltpu.roll(x, w - (1 << b), 1)
            x = jnp.where(((row >> b) & 1) == 1, rolled, x)
        t = jnp.sum(x, axis=0, keepdims=True)
        o_ref[...] = lax.dot_general(t, m_ref[...], (NN, ((), ())), precision=lax.Precision.HIGHEST,
                                     preferred_element_type=F32)

    out = pl.pallas_call(
        body, grid=(H,),
        in_specs=[pl.BlockSpec((None, tq, w), lambda h: (h, 0, 0)),
                  pl.BlockSpec((w, nbin_pad), lambda h: (0, 0))],
        out_specs=pl.BlockSpec((None, 1, nbin_pad), lambda h: (h, 0, 0)),
        out_shape=jax.ShapeDtypeStruct((H, 1, nbin_pad), F32),
        name="rel_bias_grad", compiler_params=_cparams("parallel"))(ds_sum, jnp.asarray(onehot))
    return out[:, 0, :nbin]


def _conv_t(S):
    return _pick(S, (256,))


ROW_CHUNK = 16


def _row_loop(n_rows, step):
    def one(r, carry):
        step(pl.ds(pl.multiple_of(r * ROW_CHUNK, ROW_CHUNK), ROW_CHUNK))
        return carry

    lax.fori_loop(0, n_rows // ROW_CHUNK, one, 0)


def _fill_zbuf(zbuf, ap_ref, bp_ref, a_ref, b_ref, i):
    zp = ap_ref[...].astype(F32) * _sigmoid(bp_ref[...].astype(F32))
    zbuf[0:CONV_HALO, :] = jnp.where(i == 0, 0.0, zp)

    def step(rows):
        below = pl.ds(pl.multiple_of(rows.start + CONV_HALO, ROW_CHUNK), ROW_CHUNK)
        zbuf[below, :] = a_ref[rows, :].astype(F32) * _sigmoid(b_ref[rows, :].astype(F32))

    _row_loop(a_ref.shape[0], step)


def _shifted_windows(buf, shifted, lanes, T):
    rows = T + CONV_HALO - SUBLANES
    for b in range(1, SUBLANES):
        shifted[b - 1] = buf[pl.ds(b, rows), lanes]

    def window(off, r0=0, n=T):
        a, b = divmod(off, SUBLANES)
        if b == 0:
            return buf[pl.ds(r0 + off, n), lanes]
        return shifted[b - 1, pl.ds(r0 + a * SUBLANES, n), :]

    return window


def _shifted_scratch(T):
    return pltpu.VMEM((SUBLANES - 1, T + CONV_HALO - SUBLANES, LANES), F32)


def conv_gate_fwd(proj, ya, cw, cb, lng, lnb, D, comm=None):
    S = proj.shape[0]
    T = _conv_t(S)
    hb = T // CONV_HALO
    nlb = D // LANES

    def body(ap_ref, bp_ref, a_ref, b_ref, ga_ref, gb_ref, ya_ref, cw_ref, cb_ref, lng_ref, lnb_ref,
             y_ref, c_ref, zbuf, zsh):
        i = pl.program_id(0)
        _fill_zbuf(zbuf, ap_ref, bp_ref, a_ref, b_ref, i)

        def lane_block(lb, carry):
            lanes = pl.ds(pl.multiple_of(lb * LANES, LANES), LANES)
            z_at = _shifted_windows(zbuf, zsh, lanes, T)
            acc = jnp.zeros((T, LANES), F32)
            for k in range(CONV_WIDTH):
                acc = acc + cw_ref[k:k + 1, lanes] * z_at(CONV_HALO - CONV_WIDTH + 1 + k)
            c_ref[:, lanes] = acc + cb_ref[:, lanes]
            return carry

        lax.fori_loop(0, nlb, lane_block, 0)

        def norm_and_gate(rows):
            c = c_ref[rows, :]
            mu = jnp.mean(c, axis=-1, keepdims=True)
            xc = c - mu
            rstd = lax.rsqrt(jnp.mean(xc * xc, axis=-1, keepdims=True) + EPS)
            ln = xc * rstd * lng_ref[...] + lnb_ref[...]
            yb = ln * _sigmoid(ln)
            ga = ga_ref[rows, :].astype(F32)
            gb = gb_ref[rows, :].astype(F32)
            y_ref[rows, :D] = (ya_ref[rows, :].astype(F32) * (ga * _sigmoid(ga))).astype(y_ref.dtype)
            y_ref[rows, D:] = (yb * (gb * _sigmoid(gb))).astype(y_ref.dtype)

        _row_loop(T, norm_and_gate)

    def cur(cidx):
        return pl.BlockSpec((T, D), lambda i: (i, cidx))

    def prev(cidx):
        return pl.BlockSpec((CONV_HALO, D), lambda i: (jnp.maximum(i * hb - 1, 0), cidx))

    vec = pl.BlockSpec((1, D), lambda i: (0, 0))
    return _call(
        body, name="conv_gate_fwd", grid=(S // T,),
        in_specs=[prev(3), prev(4), cur(3), cur(4), cur(5), cur(6), pl.BlockSpec((T, D), lambda i: (i, 0)),
                  pl.BlockSpec((CONV_HALO, D), lambda i: (0, 0)), vec, vec, vec],
        out_specs=[pl.BlockSpec((T, 2 * D), lambda i: (i, 0)), pl.BlockSpec((T, D), lambda i: (i, 0))],
        out_shape=[jax.ShapeDtypeStruct((S, 2 * D), BF16), jax.ShapeDtypeStruct((S, D), F32)],
        scratch_shapes=[pltpu.VMEM((T + CONV_HALO, D), F32), _shifted_scratch(T)],
        args=[proj, proj, proj, proj, proj, proj, ya, cw, cb, lng, lnb], sem=("parallel",), comm=comm)


def conv_gate_bwd_a(dy0, proj, ya, cpre, lng, lnb, D):
    S = proj.shape[0]
    T = _conv_t(S)

    def body(dy_ref, ga_ref, gb_ref, ya_ref, c_ref, lng_ref, lnb_ref,
             dya_ref, dg_ref, dc_ref, dlng_ref, dlnb_ref):
        i = pl.program_id(0)

        c = c_ref[...]
        gv = lng_ref[...]
        mu = jnp.mean(c, axis=-1, keepdims=True)
        xc = c - mu
        rstd = lax.rsqrt(jnp.mean(xc * xc, axis=-1, keepdims=True) + EPS)
        xhat = xc * rstd
        ln = xhat * gv + lnb_ref[...]
        sl = _sigmoid(ln)
        yb = ln * sl
        ga = ga_ref[...].astype(F32)
        gb = gb_ref[...].astype(F32)
        sa = _sigmoid(ga)
        sb = _sigmoid(gb)
        dy_a = dy_ref[:, :D].astype(F32)
        dy_b = dy_ref[:, D:].astype(F32)
        dya_ref[...] = (dy_a * (ga * sa)).astype(dya_ref.dtype)
        dg_ref[:, :D] = (dy_a * ya_ref[...].astype(F32) * (sa * (1.0 + ga * (1.0 - sa)))).astype(dg_ref.dtype)
        dg_ref[:, D:] = (dy_b * yb * (sb * (1.0 + gb * (1.0 - sb)))).astype(dg_ref.dtype)
        dln = dy_b * (gb * sb) * (sl * (1.0 + ln * (1.0 - sl)))
        dxhat = dln * gv
        dc_ref[...] = rstd * (dxhat - jnp.mean(dxhat, axis=-1, keepdims=True)
                              - xhat * jnp.mean(dxhat * xhat, axis=-1, keepdims=True))
        dlng = jnp.sum(dln * xhat, axis=0, keepdims=True)
        dlnb = jnp.sum(dln, axis=0, keepdims=True)

        @pl.when(i == 0)
        def _():
            dlng_ref[...] = dlng
            dlnb_ref[...] = dlnb

        @pl.when(i > 0)
        def _():
            dlng_ref[...] += dlng
            dlnb_ref[...] += dlnb

    row = pl.BlockSpec((T, D), lambda i: (i, 0))
    vec = pl.BlockSpec((1, D), lambda i: (0, 0))
    return pl.pallas_call(
        body, grid=(S // T,),
        in_specs=[pl.BlockSpec((T, 2 * D), lambda i: (i, 0)),
                  pl.BlockSpec((T, D), lambda i: (i, 5)), pl.BlockSpec((T, D), lambda i: (i, 6)),
                  row, row, vec, vec],
        out_specs=[row, pl.BlockSpec((T, 2 * D), lambda i: (i, 0)), row, vec, vec],
        out_shape=[jax.ShapeDtypeStruct((S, D), BF16), jax.ShapeDtypeStruct((S, 2 * D), BF16),
                   jax.ShapeDtypeStruct((S, D), F32), jax.ShapeDtypeStruct((1, D), F32),
                   jax.ShapeDtypeStruct((1, D), F32)],
        name="conv_gate_bwd_a", compiler_params=_cparams("arbitrary"))(
            dy0, proj, proj, ya, cpre, lng, lnb)


def conv_gate_bwd_b(dc, proj, cw, D, comm=None):
    S = proj.shape[0]
    T = _conv_t(S)
    hb = T // CONV_HALO
    nt = S // T
    nlb = D // LANES
    half = T // 2

    def body(dc_ref, dn_ref, ap_ref, bp_ref, a_ref, b_ref, cw_ref, da_ref, db_ref, dcw_ref, dcb_ref,
             zbuf, dcbuf, zsh, dcsh, dcw8):
        i = pl.program_id(0)
        _fill_zbuf(zbuf, ap_ref, bp_ref, a_ref, b_ref, i)
        dcv = dc_ref[...]
        dcbuf[0:T, :] = dcv
        dcbuf[T:, :] = jnp.where(i == nt - 1, 0.0, dn_ref[...])

        @pl.when(i == 0)
        def _():
            dcw8[...] = jnp.zeros_like(dcw8)
            dcb_ref[...] = jnp.zeros_like(dcb_ref)

        dcb_ref[...] += jnp.sum(dcv, axis=0, keepdims=True)

        def lane_block(lb, carry):
            lanes = pl.ds(pl.multiple_of(lb * LANES, LANES), LANES)
            z_at = _shifted_windows(zbuf, zsh, lanes, T)
            dc_at = _shifted_windows(dcbuf, dcsh, lanes, T)
            for r0 in range(0, T, half):
                d0 = dcbuf[r0:r0 + half, lanes]
                dz = jnp.zeros((half, LANES), F32)
                for k in range(CONV_WIDTH):
                    dz = dz + cw_ref[k:k + 1, lanes] * dc_at(CONV_WIDTH - 1 - k, r0, half)
                    prod = d0 * z_at(CONV_HALO - CONV_WIDTH + 1 + k, r0, half)
                    dcw8[pl.ds(k * SUBLANES, SUBLANES), lanes] += jnp.sum(
                        prod.reshape(half // SUBLANES, SUBLANES, LANES), axis=0)
                av = a_ref[r0:r0 + half, lanes].astype(F32)
                sg = _sigmoid(b_ref[r0:r0 + half, lanes].astype(F32))
                da_ref[r0:r0 + half, lanes] = (dz * sg).astype(da_ref.dtype)
                db_ref[r0:r0 + half, lanes] = (dz * av * sg * (1.0 - sg)).astype(db_ref.dtype)
            return carry

        lax.fori_loop(0, nlb, lane_block, 0)

        @pl.when(i == nt - 1)
        def _():
            dcw_ref[...] = jnp.sum(dcw8[...].reshape(CONV_HALO, SUBLANES, D), axis=1)

    def cur(cidx):
        return pl.BlockSpec((T, D), lambda i: (i, cidx))

    def prev(cidx):
        return pl.BlockSpec((CONV_HALO, D), lambda i: (jnp.maximum(i * hb - 1, 0), cidx))

    row = pl.BlockSpec((T, D), lambda i: (i, 0))
    nxt = pl.BlockSpec((CONV_HALO, D), lambda i: (jnp.minimum((i + 1) * hb, nt * hb - 1), 0))
    return _call(
        body, name="conv_gate_bwd_b", grid=(nt,),
        in_specs=[row, nxt, prev(3), prev(4), cur(3), cur(4), pl.BlockSpec((CONV_HALO, D), lambda i: (0, 0))],
        out_specs=[row, row, pl.BlockSpec((CONV_HALO, D), lambda i: (0, 0)),
                   pl.BlockSpec((1, D), lambda i: (0, 0))],
        out_shape=[jax.ShapeDtypeStruct((S, D), BF16), jax.ShapeDtypeStruct((S, D), BF16),
                   jax.ShapeDtypeStruct((CONV_HALO, D), F32), jax.ShapeDtypeStruct((1, D), F32)],
        scratch_shapes=[pltpu.VMEM((T + CONV_HALO, D), F32), pltpu.VMEM((T + CONV_HALO, D), F32),
                        _shifted_scratch(T), _shifted_scratch(T), pltpu.VMEM((CONV_HALO * SUBLANES, D), F32)],
        args=[dc, dc, proj, proj, proj, proj, cw], sem=("arbitrary",), comm=comm)


def assemble_dproj0(dq, dkc, dkp, dvc, dvp, da, db, dgate, D):
    S = dq.shape[0]
    tq = _attn_tq(S)
    T = _pick(S, (256,))
    shift = tq // T
    nt = S // T

    def body(dq_ref, dkc_ref, dkp_ref, dvc_ref, dvp_ref, da_ref, db_ref, dg_ref, o_ref):
        i = pl.program_id(0)
        last = i + shift >= nt
        o_ref[:, 0:D] = dq_ref[...]
        dk = dkc_ref[...].astype(F32) + jnp.where(last, 0.0, dkp_ref[...].astype(F32))
        dv = dvc_ref[...].astype(F32) + jnp.where(last, 0.0, dvp_ref[...].astype(F32))
        o_ref[:, D:2 * D] = dk.astype(o_ref.dtype)
        o_ref[:, 2 * D:3 * D] = dv.astype(o_ref.dtype)
        o_ref[:, 3 * D:4 * D] = da_ref[...]
        o_ref[:, 4 * D:5 * D] = db_ref[...]
        o_ref[:, 5 * D:] = dg_ref[...]

    row = pl.BlockSpec((T, D), lambda i: (i, 0))
    nxt = pl.BlockSpec((T, D), lambda i: (jnp.minimum(i + shift, nt - 1), 0))
    return pl.pallas_call(
        body, grid=(nt,),
        in_specs=[row, row, nxt, row, nxt, row, row, pl.BlockSpec((T, 2 * D), lambda i: (i, 0))],
        out_specs=pl.BlockSpec((T, 7 * D), lambda i: (i, 0)),
        out_shape=jax.ShapeDtypeStruct((S, 7 * D), BF16),
        name="assemble_dproj0", compiler_params=_cparams("parallel"))(dq, dkc, dkp, dvc, dvp, da, db, dgate)


def _sgu_t(S):
    return _pick(S, (256, 128))


def _ws_masked(ws_ref, g):
    row = lax.broadcasted_iota(jnp.int32, (GMLP_CHUNK, GMLP_CHUNK), 0) // CHUNK
    col = lax.broadcasted_iota(jnp.int32, (GMLP_CHUNK, GMLP_CHUNK), 1) // CHUNK
    return jnp.where(row >= col, ws_ref[g], 0.0), row >= col


def sgu_fwd(proj, lng, lnb, ws, bst, MIX):
    S = proj.shape[0]
    T = _sgu_t(S)
    gw = MIX // N_GROUPS_C

    def body(u_ref, v_ref, g_ref, lng_ref, lnb_ref, ws_ref, bst_ref, y_ref):
        v = v_ref[...].astype(F32)
        mu = jnp.mean(v, axis=-1, keepdims=True)
        xc = v - mu
        rstd = lax.rsqrt(jnp.mean(xc * xc, axis=-1, keepdims=True) + EPS)
        for g in range(N_GROUPS_C):
            cols = slice(g * gw, (g + 1) * gw)
            wsm = _ws_masked(ws_ref, g)[0].astype(BF16)
            vn = (xc[:, cols] * rstd * lng_ref[:, cols] + lnb_ref[:, cols]).astype(BF16)
            for blk in range(T // GMLP_CHUNK):
                rows = slice(blk * GMLP_CHUNK, (blk + 1) * GMLP_CHUNK)
                sg = _dot(wsm, vn[rows], NN) + bst_ref[:, g:g + 1]
                gate = g_ref[rows, cols].astype(F32)
                y = u_ref[rows, cols].astype(F32) * sg * (gate * _sigmoid(gate))
                y_ref[rows, cols] = y.astype(y_ref.dtype)

    def part(cidx):
        return pl.BlockSpec((T, MIX), lambda i: (i, cidx))

    vec = pl.BlockSpec((1, MIX), lambda i: (0, 0))
    return pl.pallas_call(
        body, grid=(S // T,),
        in_specs=[part(0), part(1), part(2), vec, vec,
                  pl.BlockSpec((N_GROUPS_C, GMLP_CHUNK, GMLP_CHUNK), lambda i: (0, 0, 0)),
                  pl.BlockSpec((GMLP_CHUNK, N_GROUPS_C), lambda i: (0, 0))],
        out_specs=pl.BlockSpec((T, MIX), lambda i: (i, 0)),
        out_shape=jax.ShapeDtypeStruct((S, MIX), BF16),
        name="sgu_fwd", compiler_params=_cparams("parallel"))(proj, proj, proj, lng, lnb, ws, bst)


def sgu_bwd(dy1, proj, lng, lnb, ws, bst, MIX):
    S = proj.shape[0]
    T = _sgu_t(S)
    gw = MIX // N_GROUPS_C

    def body(dy_ref, u_ref, v_ref, g_ref, lng_ref, lnb_ref, ws_ref, bst_ref,
             dp_ref, dws_ref, dbst_ref, dlng_ref, dlnb_ref, dvn_buf):
        i = pl.program_id(0)

        @pl.when(i == 0)
        def _():
            dws_ref[...] = jnp.zeros_like(dws_ref)
            dbst_ref[...] = jnp.zeros_like(dbst_ref)
            dlng_ref[...] = jnp.zeros_like(dlng_ref)
            dlnb_ref[...] = jnp.zeros_like(dlnb_ref)

        v = v_ref[...].astype(F32)
        mu = jnp.mean(v, axis=-1, keepdims=True)
        xc = v - mu
        rstd = lax.rsqrt(jnp.mean(xc * xc, axis=-1, keepdims=True) + EPS)
        for g in range(N_GROUPS_C):
            cols = slice(g * gw, (g + 1) * gw)
            wsf, keep = _ws_masked(ws_ref, g)
            wsm = wsf.astype(BF16)
            vn = (xc[:, cols] * rstd * lng_ref[:, cols] + lnb_ref[:, cols]).astype(BF16)
            for blk in range(T // GMLP_CHUNK):
                rows = slice(blk * GMLP_CHUNK, (blk + 1) * GMLP_CHUNK)
                vnb = vn[rows]
                sg = _dot(wsm, vnb, NN) + bst_ref[:, g:g + 1]
                gate = g_ref[rows, cols].astype(F32)
                sig = _sigmoid(gate)
                sil = gate * sig
                u = u_ref[rows, cols].astype(F32)
                dy = dy_ref[rows, cols].astype(F32)
                dp_ref[rows, g * gw:(g + 1) * gw] = (dy * sg * sil).astype(dp_ref.dtype)
                dp_ref[rows, 2 * MIX + g * gw:2 * MIX + (g + 1) * gw] = (
                    dy * u * sg * (sig * (1.0 + gate * (1.0 - sig)))).astype(dp_ref.dtype)
                dsg = dy * u * sil
                dsgb = dsg.astype(BF16)
                dvn_buf[rows, cols] = _dot(wsm, dsgb, TN)
                dws_ref[g] += jnp.where(keep, _dot(dsgb, vnb, NT), 0.0)
                dbst_ref[:, g:g + 1] += jnp.sum(dsg, axis=-1, keepdims=True)
        dvn = dvn_buf[...]
        xhat = xc * rstd
        dxhat = dvn * lng_ref[...]
        dv = rstd * (dxhat - jnp.mean(dxhat, axis=-1, keepdims=True)
                     - xhat * jnp.mean(dxhat * xhat, axis=-1, keepdims=True))
        dp_ref[:, MIX:2 * MIX] = dv.astype(dp_ref.dtype)
        dlng_ref[...] += jnp.sum(dvn * xhat, axis=0, keepdims=True)
        dlnb_ref[...] += jnp.sum(dvn, axis=0, keepdims=True)

    def part(cidx):
        return pl.BlockSpec((T, MIX), lambda i: (i, cidx))

    vec = pl.BlockSpec((1, MIX), lambda i: (0, 0))
    wspec = pl.BlockSpec((N_GROUPS_C, GMLP_CHUNK, GMLP_CHUNK), lambda i: (0, 0, 0))
    bspec = pl.BlockSpec((GMLP_CHUNK, N_GROUPS_C), lambda i: (0, 0))
    return pl.pallas_call(
        body, grid=(S // T,),
        in_specs=[pl.BlockSpec((T, MIX), lambda i: (i, 0)), part(0), part(1), part(2), vec, vec, wspec, bspec],
        out_specs=[pl.BlockSpec((T, 3 * MIX), lambda i: (i, 0)), wspec, bspec, vec, vec],
        out_shape=[jax.ShapeDtypeStruct((S, 3 * MIX), BF16),
                   jax.ShapeDtypeStruct((N_GROUPS_C, GMLP_CHUNK, GMLP_CHUNK), F32),
                   jax.ShapeDtypeStruct((GMLP_CHUNK, N_GROUPS_C), F32),
                   jax.ShapeDtypeStruct((1, MIX), F32), jax.ShapeDtypeStruct((1, MIX), F32)],
        scratch_shapes=[pltpu.VMEM((T, MIX), F32)],
        name="sgu_bwd", compiler_params=_cparams("arbitrary"))(dy1, proj, proj, proj, lng, lnb, ws, bst)


def xattn_fwd(name, q, k, v):
    S, D = q.shape
    nm = k.shape[0]
    dh = D // N_HEADS_X
    tq = _pick(S, (512, 256))
    scale = dh ** -0.5

    def body(q_ref, k_ref, v_ref, o_ref, lse_ref):
        s = _dot(q_ref[...], k_ref[...], NT) * scale
        m = jnp.max(s, axis=-1, keepdims=True)
        p = jnp.exp(s - m)
        l = jnp.sum(p, axis=-1, keepdims=True)
        o_ref[...] = (_dot(p.astype(BF16), v_ref[...], NN) / l).astype(o_ref.dtype)
        lse_ref[...] = m + jnp.log(l)

    return pl.pallas_call(
        body, grid=(N_HEADS_X, S // tq),
        in_specs=[pl.BlockSpec((tq, dh), lambda h, i: (i, h)),
                  pl.BlockSpec((nm, dh), lambda h, i: (0, h)), pl.BlockSpec((nm, dh), lambda h, i: (0, h))],
        out_specs=[pl.BlockSpec((tq, dh), lambda h, i: (i, h)),
                   pl.BlockSpec((None, tq, 1), lambda h, i: (h, i, 0))],
        out_shape=[jax.ShapeDtypeStruct((S, D), BF16), jax.ShapeDtypeStruct((N_HEADS_X, S, 1), F32)],
        name=name, compiler_params=_cparams("parallel", "parallel"))(q, k, v)


def xattn_bwd(name, q, k, v, o, do, lse):
    S, D = q.shape
    nm = k.shape[0]
    dh = D // N_HEADS_X
    tq = _pick(S, (512, 256))
    scale = dh ** -0.5

    def body(q_ref, k_ref, v_ref, o_ref, do_ref, lse_ref, dq_ref, dk_ref, dv_ref):
        i = pl.program_id(1)
        q_v = q_ref[...]
        k_v = k_ref[...]
        do_v = do_ref[...]
        p = jnp.exp(_dot(q_v, k_v, NT) * scale - lse_ref[...])
        delta = jnp.sum(do_v.astype(F32) * o_ref[...].astype(F32), axis=-1, keepdims=True)
        dv = _dot(p.astype(BF16), do_v, TN)
        ds = (p * (_dot(do_v, v_ref[...], NT) - delta)).astype(BF16)
        dq_ref[...] = (_dot(ds, k_v, NN) * scale).astype(dq_ref.dtype)
        dk = _dot(ds, q_v, TN) * scale

        @pl.when(i == 0)
        def _():
            dk_ref[...] = dk
            dv_ref[...] = dv

        @pl.when(i > 0)
        def _():
            dk_ref[...] += dk
            dv_ref[...] += dv

    qs = pl.BlockSpec((tq, dh), lambda h, i: (i, h))
    ks = pl.BlockSpec((nm, dh), lambda h, i: (0, h))
    return pl.pallas_call(
        body, grid=(N_HEADS_X, S // tq),
        in_specs=[qs, ks, ks, qs, qs, pl.BlockSpec((None, tq, 1), lambda h, i: (h, i, 0))],
        out_specs=[qs, ks, ks],
        out_shape=[jax.ShapeDtypeStruct((S, D), BF16), jax.ShapeDtypeStruct((nm, D), F32),
                   jax.ShapeDtypeStruct((nm, D), F32)],
        name=name, compiler_params=_cparams("parallel", "arbitrary"))(q, k, v, o, do, lse)


def adamw_rows(name, w, m, v, groups, row_off):
    L, R, C = w.shape
    assert len(groups) == L
    tr = _pick(R, tuple(t for t in (512, 256, 128, 64, 32, 16, 8) if t * C * 4 <= (1 << 20)) or (8,))
    assert row_off % tr == 0
    c1 = 1.0 - ADAM_B1 ** ADAM_STEP
    c2 = 1.0 - ADAM_B2 ** ADAM_STEP

    def body(w_ref, m_ref, v_ref, *refs):
        g_refs, (go_ref, d_ref, nm_ref, nv_ref) = refs[:L], refs[L:]
        layer = pl.program_id(0)
        gv = g_refs[0][...]
        for i in range(1, L):
            gv = jnp.where(layer == i, g_refs[i][...], gv)
        nm = ADAM_B1 * m_ref[...] + (1.0 - ADAM_B1) * gv
        nv = ADAM_B2 * v_ref[...] + (1.0 - ADAM_B2) * (gv * gv)
        go_ref[...] = gv
        d_ref[...] = -ADAM_LR * ((nm / c1) / (jnp.sqrt(nv / c2) + ADAM_EPS) + ADAM_WD * w_ref[...])
        nm_ref[...] = nm
        nv_ref[...] = nv

    blk = pl.BlockSpec((None, tr, C), lambda l, r: (l, r, 0))
    gblk = pl.BlockSpec((tr, C), lambda l, r: (row_off // tr + r, 0))
    sd = jax.ShapeDtypeStruct((L, R, C), F32)
    return pl.pallas_call(body, grid=(L, R // tr), in_specs=[blk] * 3 + [gblk] * L, out_specs=[blk] * 4,
                          out_shape=[sd] * 4, name=name,
                          compiler_params=_cparams("parallel", "parallel"))(w, m, v, *groups)


def adamw_many(name, tensors):
    n = len(tensors)
    c1 = 1.0 - ADAM_B1 ** ADAM_STEP
    c2 = 1.0 - ADAM_B2 ** ADAM_STEP

    def as2d(a):
        return a.reshape((1, -1) if a.ndim == 1 else (-1, a.shape[-1])).astype(F32)

    flat = [as2d(a) for t in tensors for a in t]

    def body(*refs):
        ins, outs = refs[:4 * n], refs[4 * n:]
        for t in range(n):
            w_ref, g_ref, m_ref, v_ref = ins[4 * t:4 * t + 4]
            d_ref, nm_ref, nv_ref = outs[3 * t:3 * t + 3]
            gv = g_ref[...]
            nm = ADAM_B1 * m_ref[...] + (1.0 - ADAM_B1) * gv
            nv = ADAM_B2 * v_ref[...] + (1.0 - ADAM_B2) * (gv * gv)
            d_ref[...] = -ADAM_LR * ((nm / c1) / (jnp.sqrt(nv / c2) + ADAM_EPS) + ADAM_WD * w_ref[...])
            nm_ref[...] = nm
            nv_ref[...] = nv

    vm = pl.BlockSpec(memory_space=pltpu.VMEM)
    shapes = [jax.ShapeDtypeStruct(flat[4 * t].shape, F32) for t in range(n) for _ in range(3)]
    res = pl.pallas_call(body, in_specs=[vm] * (4 * n), out_specs=[vm] * (3 * n), out_shape=shapes, name=name,
                         compiler_params=pltpu.CompilerParams(vmem_limit_bytes=V7X_VMEM_LIMIT))(*flat)
    return [tuple(r.reshape(tensors[t][0].shape) for r in res[3 * t:3 * t + 3]) for t in range(n)]


def add_halves(name, g4, recv, cidx):
    _, R, C = g4.shape
    rh = R // 2
    tr = _pick(rh, (256, 128, 64, 32, 16))
    nrb = rh // tr

    def body(c_ref, a_ref, b_ref, o_ref):
        o_ref[...] = (a_ref[...].astype(F32) + b_ref[...].astype(F32)).astype(o_ref.dtype)

    grid_spec = pltpu.PrefetchScalarGridSpec(
        num_scalar_prefetch=1, grid=(4, nrb),
        in_specs=[pl.BlockSpec((None, tr, C), lambda j, r, c_ref: (j, c_ref[0] * nrb + r, 0)),
                  pl.BlockSpec((None, tr, C), lambda j, r, c_ref: (j, r, 0))],
        out_specs=pl.BlockSpec((None, tr, C), lambda j, r, c_ref: (j, r, 0)))
    return pl.pallas_call(body, grid_spec=grid_spec, out_shape=jax.ShapeDtypeStruct((4, rh, C), BF16),
                          name=name, compiler_params=_cparams("parallel", "parallel"))(cidx, g4, recv)


def sum_chips(name, own, recv, place):
    _, rh, C = own.shape
    tr = _pick(rh, (256, 128, 64, 32, 16))
    nrb = rh // tr

    def body(s_ref, own_ref, recv_ref, o_ref):
        acc = own_ref[...].astype(F32)
        for k in range(N_CHIPS - 1):
            acc = acc + recv_ref[k].astype(F32)
        o_ref[...] = acc

    grid_spec = pltpu.PrefetchScalarGridSpec(
        num_scalar_prefetch=1, grid=(nrb,),
        in_specs=[pl.BlockSpec((None, tr, C), lambda r, s: (s[0], r, 0)),
                  pl.BlockSpec((N_CHIPS - 1, tr, C), lambda r, s: (0, r, 0))],
        out_specs=pl.BlockSpec((tr, C), lambda r, s: (s[1] * nrb + r, 0)))
    return pl.pallas_call(body, grid_spec=grid_spec, out_shape=jax.ShapeDtypeStruct((2 * rh, C), F32),
                          name=name, compiler_params=_cparams("parallel"))(place, own, recv)


def cast_into_slot(name, w, place):
    R, C = w.shape
    tr = _pick(R, (256, 128, 64, 32, 16))

    def body(s_ref, w_ref, o_ref):
        o_ref[...] = w_ref[...].astype(o_ref.dtype)

    grid_spec = pltpu.PrefetchScalarGridSpec(
        num_scalar_prefetch=1, grid=(R // tr,),
        in_specs=[pl.BlockSpec((tr, C), lambda r, s: (r, 0))],
        out_specs=pl.BlockSpec((None, tr, C), lambda r, s: (s[0], r, 0)))
    return pl.pallas_call(body, grid_spec=grid_spec, out_shape=jax.ShapeDtypeStruct((N_CHIPS, R, C), BF16),
                          name=name, compiler_params=_cparams("parallel"))(place, w)


def _place():
    return lax.axis_index("x"), lax.axis_index("y"), lax.axis_index("c")


_CHIP_FLIPS = ((1, 0), (0, 1), (1, 1))


def _flip(v, bit):
    return 1 - v if bit else v


HBM_SPEC = pl.BlockSpec(memory_space=pl.ANY)


def exchange_small(name, buf, reduce):
    R = buf.shape[0]

    def body(x_ref, *refs):
        if reduce:
            sum_ref, all_ref, send_sems, recv_sems, local_sem = refs
        else:
            all_ref, send_sems, recv_sems, local_sem = refs
        x, y, c = _place()
        me = 4 * x + 2 * y + c
        mine = pltpu.make_async_copy(x_ref, all_ref.at[me], local_sem)
        mine.start()
        sends = []
        for k in range(1, N_DEV):
            peer = (_flip(x, k & 4), _flip(y, k & 2), _flip(c, k & 1))
            cp = pltpu.make_async_remote_copy(src_ref=x_ref, dst_ref=all_ref.at[me], send_sem=send_sems.at[k - 1],
                                              recv_sem=recv_sems.at[k - 1], device_id=peer, device_id_type=MESH)
            cp.start()
            sends.append(cp)
        for k in range(1, N_DEV):
            peer = (_flip(x, k & 4), _flip(y, k & 2), _flip(c, k & 1))
            src = 4 * peer[0] + 2 * peer[1] + peer[2]
            pltpu.make_async_remote_copy(src_ref=x_ref, dst_ref=all_ref.at[src], send_sem=send_sems.at[k - 1],
                                         recv_sem=recv_sems.at[k - 1], device_id=peer,
                                         device_id_type=MESH).wait_recv()
        for cp in sends:
            cp.wait_send()
        mine.wait()
        if reduce:
            acc = all_ref[0]
            for d in range(1, N_DEV):
                acc = acc + all_ref[d]
            sum_ref[...] = acc

    vm = pl.BlockSpec(memory_space=pltpu.VMEM)
    sems = [pltpu.SemaphoreType.DMA((N_DEV - 1,)), pltpu.SemaphoreType.DMA((N_DEV - 1,)), pltpu.SemaphoreType.DMA]
    if reduce:
        return pl.pallas_call(
            body, in_specs=[vm], out_specs=vm, out_shape=jax.ShapeDtypeStruct((R, LANES), F32),
            scratch_shapes=[pltpu.VMEM((N_DEV, R, LANES), F32)] + sems, name=name,
            compiler_params=pltpu.CompilerParams(vmem_limit_bytes=V7X_VMEM_LIMIT))(buf)
    return pl.pallas_call(
        body, in_specs=[vm], out_specs=vm, out_shape=jax.ShapeDtypeStruct((N_DEV, R, LANES), F32),
        scratch_shapes=sems, name=name,
        compiler_params=pltpu.CompilerParams(vmem_limit_bytes=V7X_VMEM_LIMIT))(buf)


def exchange_job(buf):
    R = buf.shape[0]

    def copies(x_ref, all_ref, send_sems, recv_sems):
        x, y, c = _place()
        me = 4 * x + 2 * y + c
        sends, arrivals = [], []
        for k in range(1, N_DEV):
            peer = (_flip(x, k & 4), _flip(y, k & 2), _flip(c, k & 1))
            src = 4 * peer[0] + 2 * peer[1] + peer[2]
            sends.append(pltpu.make_async_remote_copy(
                src_ref=x_ref, dst_ref=all_ref.at[me], send_sem=send_sems.at[k - 1], recv_sem=recv_sems.at[k - 1],
                device_id=peer, device_id_type=MESH))
            arrivals.append(pltpu.make_async_remote_copy(
                src_ref=x_ref, dst_ref=all_ref.at[src], send_sem=send_sems.at[k - 1], recv_sem=recv_sems.at[k - 1],
                device_id=peer, device_id_type=MESH))
        return sends, arrivals

    def start(ins, outs, sems):
        for cp in copies(ins[0], outs[0], *sems)[0]:
            cp.start()

    def finish(ins, outs, sems):
        sends, arrivals = copies(ins[0], outs[0], *sems)
        for cp in arrivals:
            cp.wait_recv()
        for cp in sends:
            cp.wait_send()

    return _Comm([buf], [jax.ShapeDtypeStruct((N_DEV, R, LANES), F32)], {},
                 [pltpu.SemaphoreType.DMA((N_DEV - 1,)), pltpu.SemaphoreType.DMA((N_DEV - 1,))], start, finish)


def sum_devices(name, slots):
    _, R, _ = slots.shape
    tr = _pick(R, (512, 256, 128, 64, 32, 16, 8))

    def body(s_ref, o_ref):
        acc = s_ref[0]
        for d in range(1, N_DEV):
            acc = acc + s_ref[d]
        o_ref[...] = acc

    return pl.pallas_call(body, grid=(R // tr,),
                          in_specs=[pl.BlockSpec((N_DEV, tr, LANES), lambda r: (0, r, 0))],
                          out_specs=pl.BlockSpec((tr, LANES), lambda r: (r, 0)),
                          out_shape=jax.ShapeDtypeStruct((R, LANES), F32), name=name,
                          compiler_params=_cparams("parallel"))(slots)


def gather_job(slots, relay_frac=0.75, flips=None):
    n = len(slots)
    flips = flips or [tuple(range(len(_CHIP_FLIPS)))] * n

    def copies(o_refs, send_sems, recv_sems):
        x, y, c = _place()
        me = 2 * x + y
        sib = (x, y, 1 - c)
        chips = [(_flip(x, fx), _flip(y, fy)) for fx, fy in _CHIP_FLIPS]
        ici, fwd, from_sib = [], [], []
        for t in range(n):
            rh = o_refs[t].shape[1] // 2
            mine, theirs = pl.ds(c * rh, rh), pl.ds((1 - c) * rh, rh)
            for k, (px, py) in enumerate(chips):
                if k not in flips[t]:
                    continue
                own = o_refs[t].at[me, mine]
                ici.append(pltpu.make_async_remote_copy(
                    src_ref=own, dst_ref=own, send_sem=send_sems.at[t, k], recv_sem=recv_sems.at[t, k],
                    device_id=(px, py, c), device_id_type=MESH))
                landed = o_refs[t].at[2 * px + py, mine]
                arrival = pltpu.make_async_remote_copy(
                    src_ref=landed, dst_ref=landed, send_sem=send_sems.at[t, k], recv_sem=recv_sems.at[t, k],
                    device_id=(px, py, c), device_id_type=MESH)
                fwd.append((arrival, pltpu.make_async_remote_copy(
                    src_ref=landed, dst_ref=landed, send_sem=send_sems.at[t, 3 + k],
                    recv_sem=recv_sems.at[t, 3 + k], device_id=sib, device_id_type=MESH)))
                passed = o_refs[t].at[2 * px + py, theirs]
                from_sib.append(pltpu.make_async_remote_copy(
                    src_ref=passed, dst_ref=passed, send_sem=send_sems.at[t, 3 + k],
                    recv_sem=recv_sems.at[t, 3 + k], device_id=sib, device_id_type=MESH))
        return ici, fwd, from_sib

    def start(ins, o_refs, sems):
        for cp in copies(o_refs, *sems)[0]:
            cp.start()

    def relay(ins, o_refs, sems):
        for arrival, forward in copies(o_refs, *sems)[1]:
            arrival.wait_recv()
            forward.start()

    def finish(ins, o_refs, sems):
        ici, fwd, from_sib = copies(o_refs, *sems)
        for cp in from_sib:
            cp.wait_recv()
        for cp in ici:
            cp.wait_send()
        for _, forward in fwd:
            forward.wait_send()

    return _Comm(slots, [jax.ShapeDtypeStruct(s.shape, s.dtype) for s in slots], {t: t for t in range(n)},
                 [pltpu.SemaphoreType.DMA((n, 6)), pltpu.SemaphoreType.DMA((n, 6))], start, finish, relay,
                 relay_frac)


def sibling_halves_job(grads):
    n = len(grads)

    def copies(g_refs, o_refs, send_sems, recv_sems):
        x, y, c = _place()
        out = []
        for t in range(n):
            rh = g_refs[t].shape[1] // 2
            out.append(pltpu.make_async_remote_copy(
                src_ref=g_refs[t].at[:, pl.ds((1 - c) * rh, rh), :], dst_ref=o_refs[t],
                send_sem=send_sems.at[t], recv_sem=recv_sems.at[t], device_id=(x, y, 1 - c),
                device_id_type=MESH))
        return out

    def start(g_refs, o_refs, sems):
        for cp in copies(g_refs, o_refs, *sems):
            cp.start()

    def finish(g_refs, o_refs, sems):
        cps = copies(g_refs, o_refs, *sems)
        for cp in cps:
            cp.wait_recv()
        for cp in cps:
            cp.wait_send()

    return _Comm(grads, [jax.ShapeDtypeStruct((4, g.shape[1] // 2, g.shape[2]), g.dtype) for g in grads], {},
                 [pltpu.SemaphoreType.DMA((n,)), pltpu.SemaphoreType.DMA((n,))], start, finish)


def scatter_job(parts):
    n = len(parts)

    def copies(p_refs, o_refs, send_sems, recv_sems):
        x, y, c = _place()
        out = []
        for t in range(n):
            for k, (fx, fy) in enumerate(_CHIP_FLIPS):
                px, py = _flip(x, fx), _flip(y, fy)
                out.append(pltpu.make_async_remote_copy(
                    src_ref=p_refs[t].at[2 * px + py], dst_ref=o_refs[t].at[k],
                    send_sem=send_sems.at[t, k], recv_sem=recv_sems.at[t, k],
                    device_id=(px, py, c), device_id_type=MESH))
        return out

    def start(p_refs, o_refs, sems):
        for cp in copies(p_refs, o_refs, *sems):
            cp.start()

    def finish(p_refs, o_refs, sems):
        cps = copies(p_refs, o_refs, *sems)
        for cp in cps:
            cp.wait_recv()
        for cp in cps:
            cp.wait_send()

    return _Comm(parts, [jax.ShapeDtypeStruct((N_CHIPS - 1,) + p.shape[1:], p.dtype) for p in parts], {},
                 [pltpu.SemaphoreType.DMA((n, 3)), pltpu.SemaphoreType.DMA((n, 3))], start, finish)


def share_halves_job(halves):
    n = len(halves)

    def copies(o_refs, send_sems, recv_sems):
        x, y, c = _place()
        sends, arrivals = [], []
        for t in range(n):
            rh = o_refs[t].shape[0] // 2
            mine = o_refs[t].at[pl.ds(c * rh, rh)]
            theirs = o_refs[t].at[pl.ds((1 - c) * rh, rh)]
            sends.append(pltpu.make_async_remote_copy(
                src_ref=mine, dst_ref=mine, send_sem=send_sems.at[t], recv_sem=recv_sems.at[t],
                device_id=(x, y, 1 - c), device_id_type=MESH))
            arrivals.append(pltpu.make_async_remote_copy(
                src_ref=theirs, dst_ref=theirs, send_sem=send_sems.at[t], recv_sem=recv_sems.at[t],
                device_id=(x, y, 1 - c), device_id_type=MESH))
        return sends, arrivals

    def start(ins, o_refs, sems):
        for cp in copies(o_refs, *sems)[0]:
            cp.start()

    def finish(ins, o_refs, sems):
        sends, arrivals = copies(o_refs, *sems)
        for cp in arrivals:
            cp.wait_recv()
        for cp in sends:
            cp.wait_send()

    return _Comm(halves, [jax.ShapeDtypeStruct(h.shape, h.dtype) for h in halves], {t: t for t in range(n)},
                 [pltpu.SemaphoreType.DMA((n,)), pltpu.SemaphoreType.DMA((n,))], start, finish)


def _pack(arrs, row_multiple=SUBLANES):
    flat, total = [], 0
    for a in arrs:
        v = a.reshape(-1).astype(F32)
        pad = (-v.shape[0]) % (SUBLANES * LANES)
        flat.append(jnp.pad(v, (0, pad)))
        total += v.shape[0] + pad
    tail = (-total) % (row_multiple * LANES)
    if tail:
        flat.append(jnp.zeros((tail,), F32))
    return jnp.concatenate(flat).reshape(-1, LANES)


def _unpack(buf, shapes):
    out, off = [], 0
    flat = buf.reshape(-1)
    for s in shapes:
        n = int(np.prod(s))
        out.append(flat[off:off + n].reshape(s))
        off += n + ((-n) % (8 * LANES))
    return out


def _xattn_layer_fwd(tag, h, mem, gx, gmem, w):
    hx = rms_fwd(f"rms_x{tag}", h, gx)
    memn = rms_fwd(f"rms_mem{tag}", mem, gmem)
    q = mm_nn(f"xq{tag}", hx, w["q"], BF16)
    k = mm_nn(f"xk{tag}", memn, w["k"], BF16)
    v = mm_nn(f"xv{tag}", memn, w["v"], BF16)
    o, lse = xattn_fwd(f"xattn_fwd{tag}", q, k, v)
    h_out = mm_nn(f"xo{tag}", o, w["o"], F32, res=h)
    return h_out, dict(hx=hx, memn=memn, q=q, k=k, v=v, o=o, lse=lse)


def _xattn_layer_bwd(tag, dh_out, dh_out_b, h_in, mem, gx, gmem, w, sv):
    do = mm_nt(f"d_xo{tag}", dh_out_b, w["o"], BF16)
    dwo = mm_tn(f"dw_xo{tag}", sv["o"], dh_out_b)
    dq, dk, dv = xattn_bwd(f"xattn_bwd{tag}", sv["q"], sv["k"], sv["v"], sv["o"], do, sv["lse"])
    dwq = mm_tn(f"dw_xq{tag}", sv["hx"], dq)
    dhx = mm_nt(f"d_xq{tag}", dq, w["q"], BF16)
    dwk = mm_tn(f"dw_xk{tag}", sv["memn"], dk)
    dwv = mm_tn(f"dw_xv{tag}", sv["memn"], dv)
    dmk = mm_nt(f"d_xk{tag}", dk, w["k"], F32)
    dmv = mm_nt(f"d_xv{tag}", dv, w["v"], F32)
    dh_in, dh_in_b, dgx = rms_bwd(f"rms_x_bwd{tag}", h_in, gx, [dhx], dh_out)
    _, _, dgmem = rms_bwd(f"rms_mem_bwd{tag}", mem, gmem, [dmk, dmv], None)
    return dh_in, dh_in_b, dgx, dgmem, dict(q=dwq, k=dwk, v=dwv, o=dwo)


def kernel(x, mem, norm_mix_g, norm_x_g, norm_mem_g, final_norm_g, w_in_ab, rel_bias, conv_w, conv_b, conv_ln_g, conv_ln_b, w_out_ab, w_in_c, sgu_ln_g, sgu_ln_b, w_s, b_s, w_out_c, w_xq, w_xk, w_xv, w_xo, loss_target, m_norm_mix_g, m_norm_x_g, m_norm_mem_g, m_final_norm_g, m_w_in_ab, m_rel_bias, m_conv_w, m_conv_b, m_conv_ln_g, m_conv_ln_b, m_w_out_ab, m_w_in_c, m_sgu_ln_g, m_sgu_ln_b, m_w_s, m_b_s, m_w_out_c, m_w_xq, m_w_xk, m_w_xv, m_w_xo, v_norm_mix_g, v_norm_x_g, v_norm_mem_g, v_final_norm_g, v_w_in_ab, v_rel_bias, v_conv_w, v_conv_b, v_conv_ln_g, v_conv_ln_b, v_w_out_ab, v_w_in_c, v_sgu_ln_g, v_sgu_ln_b, v_w_s, v_b_s, v_w_out_c, v_w_xq, v_w_xk, v_w_xv, v_w_xo):
    S, D = x.shape[1], x.shape[2]
    MIX = 2 * D
    xs, mems, tgt = x[0], mem[0], loss_target[0]
    cx, cy, cc = _place()
    chip = 2 * cx + cy
    cidx = jnp.reshape(cc, (1,)).astype(jnp.int32)
    place = jnp.stack([chip, cc]).astype(jnp.int32)

    ro, rq = MIX // 4, D // 4
    row_sharded = [("out_ab", w_out_ab[0]), ("out_c", w_out_c[0])]
    for layer in range(2):
        for nm_, w in (("q", w_xq), ("k", w_xk), ("v", w_xv), ("o", w_xo)):
            row_sharded.append((f"x{nm_}{layer}", w[layer]))
    slots = {"in_ab": cast_into_slot("cast_in_ab", w_in_ab[0], place),
             "in_c": cast_into_slot("cast_in_c", w_in_c[0], place)}
    slots.update({nm_: cast_into_slot("cast_" + nm_, w, place) for nm_, w in row_sharded})

    small_sh = [conv_w[0], sgu_ln_g[0], sgu_ln_b[0]]
    gathered = exchange_small("gather_small", _pack(small_sh), reduce=False)
    per_chip = [_unpack(gathered[2 * j], [a.shape for a in small_sh]) for j in range(N_CHIPS)]
    conv_w_full = jnp.concatenate([p[0] for p in per_chip], axis=1)
    sgu_g_full = jnp.concatenate([p[1] for p in per_chip], axis=0).reshape(1, MIX)
    sgu_b_full = jnp.concatenate([p[2] for p in per_chip], axis=0).reshape(1, MIX)
    cw_pad = jnp.pad(conv_w_full, ((0, CONV_HALO - CONV_WIDTH), (0, 0)))
    cb = conv_b.reshape(1, D)
    clg, clb = conv_ln_g.reshape(1, D), conv_ln_b.reshape(1, D)
    ws = w_s[0]
    bst = jnp.transpose(b_s[0])
    tq = _attn_tq(S)
    bm = band_bias_table(rel_bias[0], tq)

    hn0 = rms_fwd("rms_mix0", xs, norm_mix_g[0])
    near, far, every = (0, 1), (2,), (0, 1, 2)
    proj0, (wab4,) = proj_cols_own("proj_ab_own", hn0, w_in_ab[0], place,
                                   comm=gather_job([slots["in_ab"]], relay_frac=1.0, flips=[near]))
    proj0, (wab4, w_out_ab4) = proj_cols_rest(
        "proj_ab_near", hn0, wab4, proj0, place, (2, 1),
        comm=gather_job([wab4, slots["out_ab"]], flips=[far, every]))
    proj0, got_qk = proj_cols_rest("proj_ab_far", hn0, wab4, proj0, place, (3,),
                                   comm=gather_job([slots["xq0"], slots["xk0"]]))
    (ya, lse_a), (wc4,) = attn_fwd(proj0, bm, D, comm=gather_job([slots["in_c"]]))
    (y0, cpre), got_b = conv_gate_fwd(
        proj0, ya, cw_pad, cb, clg, clb, D,
        comm=gather_job([slots["xv0"], slots["xo0"], slots["out_c"], slots["xq1"]]))
    h1, got_c = mm_nn("out_ab", y0, w_out_ab4.reshape(-1, D), F32, res=xs,
                      comm=gather_job([slots["xk1"], slots["xv1"], slots["xo1"]]))
    got = dict(zip(["xq0", "xk0", "xv0", "xo0", "out_c", "xq1", "xk1", "xv1", "xo1"], got_qk + got_b + got_c))
    wrow = {n: g.reshape(-1, g.shape[2]) for n, g in got.items()}
    wrow["out_ab"] = w_out_ab4.reshape(-1, D)
    wx = [{k: wrow[f"x{k}{layer}"] for k in "qkvo"} for layer in range(2)]
    h2, sx0 = _xattn_layer_fwd("0", h1, mems, norm_x_g[0], norm_mem_g[0], wx[0])
    hn1 = rms_fwd("rms_mix1", h2, norm_mix_g[1])
    proj1 = mm_nn_cols("proj_c", hn1, wc4, BF16)
    y1 = sgu_fwd(proj1, sgu_g_full, sgu_b_full, ws, bst, MIX)
    h3 = mm_nn("out_c", y1, wrow["out_c"], F32, res=h2)
    h4, sx1 = _xattn_layer_fwd("1", h3, mems, norm_x_g[1], norm_mem_g[1], wx[1])
    loss_row, dg_final, dh4, dh4b = loss_head("loss_head", h4, final_norm_g, tgt)

    dh3, dh3b, dgx1, dgmem1, dwx1 = _xattn_layer_bwd("1", dh4, dh4b, h3, mems, norm_x_g[1], norm_mem_g[1], wx[1], sx1)
    def stack_rows(dw_out, dwx):
        return jnp.concatenate([g.reshape(N_CHIPS, -1, g.shape[1]) for g in [dw_out] + [dwx[k] for k in "qkvo"]],
                               axis=1)

    dy1 = mm_nt("d_out_c", dh3b, wrow["out_c"], BF16)
    dw_out_c = mm_tn("dw_out_c", y1, dh3b)
    dproj1, dws, dbst, dsgu_g, dsgu_b = sgu_bwd(dy1, proj1, sgu_g_full, sgu_b_full, ws, bst, MIX)
    grp1 = stack_rows(dw_out_c, dwx1)
    dw_in_c, (sib1,) = mm_tn_cols("dw_in_c", hn1, dproj1, comm=sibling_halves_job([grp1]))
    part1 = add_halves("add_halves1", grp1, sib1, cidx)
    dhn1, (recv1, sib2) = mm_nt_cols("d_proj_c", dproj1, wc4, BF16,
                                     comm=_join(scatter_job([part1]), sibling_halves_job([dw_in_c])))
    part2 = add_halves("add_halves2", dw_in_c, sib2, cidx)
    dh2, dh2b, dgmix1 = rms_bwd("rms_mix1_bwd", h2, norm_mix_g[1], [dhn1], dh3)
    dh1, dh1b, dgx0, dgmem0, dwx0 = _xattn_layer_bwd("0", dh2, dh2b, h1, mems, norm_x_g[0], norm_mem_g[0], wx[0], sx0)
    dy0 = mm_nt("d_out_ab", dh1b, wrow["out_ab"], BF16)
    dw_out_ab = mm_tn("dw_out_ab", y0, dh1b)
    grp3 = stack_rows(dw_out_ab, dwx0)
    dya, dgate, dc, dclg, dclb = conv_gate_bwd_a(dy0, proj0, ya, cpre, clg, clb, D)
    (da, db, dcw, dcb), (recv2, sib3) = conv_gate_bwd_b(
        dc, proj0, cw_pad, D, comm=_join(scatter_job([part2]), sibling_halves_job([grp3])))
    part3 = add_halves("add_halves3", grp3, sib3, cidx)
    (dq, dkc, dkp, dvc, dvp, ds_sum), (recv3,) = attn_bwd(proj0, ya, dya, lse_a, bm, D, comm=scatter_job([part3]))
    drel = rel_bias_grad(ds_sum)
    dproj0 = assemble_dproj0(dq, dkc, dkp, dvc, dvp, da, db, dgate, D)
    dw_in_ab = mm_tn_cols("dw_in_ab", hn0, dproj0)
    (sib4,) = run_comm("sibling_halves4", sibling_halves_job([dw_in_ab]))
    part4 = add_halves("add_halves4", dw_in_ab, sib4, cidx)
    halves = [sum_chips(f"sum_chips{t + 1}", p, r, place)
              for t, (p, r) in enumerate(((part1, recv1), (part2, recv2), (part3, recv3)))]
    small_early = [
        jnp.concatenate([dgx0, dgx1], axis=0), jnp.concatenate([dgmem0, dgmem1], axis=0), dg_final.reshape(D),
        drel[None], dcb, dclg, dclb, dws[None], jnp.transpose(dbst)[None],
        dcw[:CONV_WIDTH][None], dsgu_g, dsgu_b]
    early = _pack(small_early, row_multiple=512)
    dhn0, (recv4, small_slots, g_r1, g_c, g_r0) = mm_nt_cols(
        "d_proj_ab", dproj0, wab4, BF16,
        comm=_join(scatter_job([part4]), exchange_job(early), share_halves_job(halves)))
    dx, _, dgmix0 = rms_bwd("rms_mix0_bwd", xs, norm_mix_g[0], [dhn0], dh1)
    (g_ab,) = run_comm("share_reduced_half4", share_halves_job([sum_chips("sum_chips4", part4, recv4, place)]))

    me = 4 * cx + 2 * cy + cc
    small_slots = lax.dynamic_update_slice(small_slots, early[None], (me, 0, 0))
    summed = _unpack(sum_devices("sum_small", small_slots), [a.shape for a in small_early])
    (g_norm_x, g_norm_mem, g_final, g_rel, g_conv_b, g_clg, g_clb, g_ws, g_bs,
     g_conv_w_full, g_sgu_g_full, g_sgu_b_full) = summed
    dgmix = jnp.concatenate([dgmix0, dgmix1], axis=0)
    (g_norm_mix,) = _unpack(exchange_small("reduce_late", _pack([dgmix]), reduce=True), [dgmix.shape])
    cws = conv_w.shape[2]
    g_conv_w = lax.dynamic_slice_in_dim(g_conv_w_full, chip * cws, cws, axis=2)
    sgs = sgu_ln_g.shape[1]
    g_sgu_g = lax.dynamic_slice_in_dim(g_sgu_g_full, chip * sgs, sgs, axis=1)
    g_sgu_b = lax.dynamic_slice_in_dim(g_sgu_b_full, chip * sgs, sgs, axis=1)

    loss = lax.psum(loss_row[0, 0], ("x", "y", "c"))

    big_grads = {"w_in_ab": ([g_ab], 0), "w_in_c": ([g_c], 0), "w_out_ab": ([g_r0], 0), "w_out_c": ([g_r1], 0)}
    for i, nm_ in enumerate("qkvo"):
        big_grads["w_x" + nm_] = ([g_r0, g_r1], ro + i * rq)
    grads = dict(
        norm_mix_g=g_norm_mix, norm_x_g=g_norm_x, norm_mem_g=g_norm_mem, final_norm_g=g_final,
        rel_bias=g_rel, conv_w=g_conv_w, conv_b=g_conv_b, conv_ln_g=g_clg, conv_ln_b=g_clb,
        sgu_ln_g=g_sgu_g, sgu_ln_b=g_sgu_b, w_s=g_ws, b_s=g_bs)
    weights = dict(
        norm_mix_g=(norm_mix_g, m_norm_mix_g, v_norm_mix_g), norm_x_g=(norm_x_g, m_norm_x_g, v_norm_x_g),
        norm_mem_g=(norm_mem_g, m_norm_mem_g, v_norm_mem_g), final_norm_g=(final_norm_g, m_final_norm_g, v_final_norm_g),
        w_in_ab=(w_in_ab, m_w_in_ab, v_w_in_ab), rel_bias=(rel_bias, m_rel_bias, v_rel_bias),
        conv_w=(conv_w, m_conv_w, v_conv_w), conv_b=(conv_b, m_conv_b, v_conv_b),
        conv_ln_g=(conv_ln_g, m_conv_ln_g, v_conv_ln_g), conv_ln_b=(conv_ln_b, m_conv_ln_b, v_conv_ln_b),
        w_out_ab=(w_out_ab, m_w_out_ab, v_w_out_ab), w_in_c=(w_in_c, m_w_in_c, v_w_in_c),
        sgu_ln_g=(sgu_ln_g, m_sgu_ln_g, v_sgu_ln_g), sgu_ln_b=(sgu_ln_b, m_sgu_ln_b, v_sgu_ln_b),
        w_s=(w_s, m_w_s, v_w_s), b_s=(b_s, m_b_s, v_b_s), w_out_c=(w_out_c, m_w_out_c, v_w_out_c),
        w_xq=(w_xq, m_w_xq, v_w_xq), w_xk=(w_xk, m_w_xk, v_w_xk), w_xv=(w_xv, m_w_xv, v_w_xv),
        w_xo=(w_xo, m_w_xo, v_w_xo))
    names = list(weights)
    delta, new_m, new_v = {}, {}, {}
    for nm_, (groups, row_off) in big_grads.items():
        w, m, v = weights[nm_]
        grads[nm_], delta[nm_], new_m[nm_], new_v[nm_] = adamw_rows("adamw_" + nm_, w, m, v, groups, row_off)
    small_names = [n for n in names if n not in big_grads]
    stepped = adamw_many("adamw_small", [(weights[n][0], grads[n].reshape(weights[n][0].shape), weights[n][1],
                                          weights[n][2]) for n in small_names])
    for n, (d_, m_, v_) in zip(small_names, stepped):
        delta[n], new_m[n], new_v[n] = d_, m_, v_

    return (loss, dx[None], *[grads[n].reshape(weights[n][0].shape) for n in names], *[delta[n] for n in names],
            *[new_m[n] for n in names], *[new_v[n] for n in names])
```

```python
import functools

import numpy as np
import jax
import jax.numpy as jnp
from jax import lax
from jax.experimental import pallas as pl
from jax.experimental.pallas import tpu as pltpu

F32 = jnp.float32
BF16 = jnp.bfloat16
MESH = pl.DeviceIdType.MESH

EPS = 1e-6
CHUNK = 64
N_PAST_CHUNKS = 8
MAX_REL = 128
HEAD_DIM_A = 128
CONV_WIDTH = 31
CONV_HALO = 32
GMLP_CHUNK = 128
N_GROUPS_C = 8
N_HEADS_X = 4
NEG = -1e30

ADAM_LR = 0.001
ADAM_B1 = 0.9
ADAM_B2 = 0.999
ADAM_EPS = 1e-08
ADAM_WD = 0.01
ADAM_STEP = 10

N_CHIPS = 4
N_DEV = 8
V7X_VMEM_LIMIT = 56 * 1024 * 1024
LANES = 128
SUBLANES = 8


def _pick(n, cands):
    for c in cands:
        if c <= n and n % c == 0:
            return c
    return n


def _cparams(*sem):
    return pltpu.CompilerParams(dimension_semantics=sem, vmem_limit_bytes=V7X_VMEM_LIMIT)


def _sigmoid(x):
    return 0.5 * jnp.tanh(0.5 * x) + 0.5


def _dot(a, b, contract):
    return lax.dot_general(a, b, (contract, ((), ())), preferred_element_type=F32)


NN = ((1,), (0,))
NT = ((1,), (1,))
TN = ((0,), (0,))


class _Comm:
    def __init__(self, arrays, out_shapes, aliases, sems, start, finish, relay=None, relay_frac=0.75):
        self.arrays, self.out_shapes, self.aliases, self.sems = list(arrays), list(out_shapes), dict(aliases), list(sems)
        self.start, self.finish, self.relay = start, finish, relay
        self.relay_frac = relay_frac


def _join(*jobs):
    assert all(j.relay is None for j in jobs)
    arrays, outs, sems, aliases, spans = [], [], [], {}, []
    for j in jobs:
        spans.append((len(arrays), len(outs), len(sems)))
        aliases.update({len(arrays) + i: len(outs) + o for i, o in j.aliases.items()})
        arrays += j.arrays
        outs += j.out_shapes
        sems += j.sems

    def part(j, span, ins, os_, ss):
        a0, o0, s0 = span
        return (ins[a0:a0 + len(j.arrays)], os_[o0:o0 + len(j.out_shapes)], ss[s0:s0 + len(j.sems)])

    def start(ins, os_, ss):
        for j, span in zip(jobs, spans):
            j.start(*part(j, span, ins, os_, ss))

    def finish(ins, os_, ss):
        for j, span in zip(jobs, spans):
            j.finish(*part(j, span, ins, os_, ss))

    return _Comm(arrays, outs, aliases, sems, start, finish)


def _call(body, *, name, grid, in_specs, out_specs, out_shape, args, scratch_shapes=(), sem=None, comm=None,
          prefetch=None, io_aliases=None):
    multi = isinstance(out_shape, (list, tuple))
    o_shapes = list(out_shape) if multi else [out_shape]
    o_specs = list(out_specs) if multi else [out_specs]
    if comm is None:
        assert prefetch is None and io_aliases is None
        return pl.pallas_call(body, grid=grid, in_specs=in_specs, out_specs=out_specs, out_shape=out_shape,
                              scratch_shapes=list(scratch_shapes), name=name,
                              compiler_params=_cparams(*sem))(*args)
    n_in, n_out, n_scr = len(in_specs), len(o_shapes), len(scratch_shapes)
    n_ci, n_co = len(comm.arrays), len(comm.out_shapes)
    n_steps = int(np.prod(grid))
    n_pre = 0 if prefetch is None else 1

    def carrier(*refs):
        pre, refs = refs[:n_pre], refs[n_pre:]
        ins, rest = refs[:n_in], refs[n_in:]
        cins, rest = rest[:n_ci], rest[n_ci:]
        outs, rest = rest[:n_out], rest[n_out:]
        couts, rest = rest[:n_co], rest[n_co:]
        scr, csems = rest[:n_scr], rest[n_scr:]
        step = 0
        for a, g in enumerate(grid):
            step = step * g + pl.program_id(a)

        @pl.when(step == 0)
        def _():
            comm.start(cins, couts, csems)

        body(*pre, *ins, *outs, *scr)

        relay_step = min(int(comm.relay_frac * n_steps), n_steps - 1)
        if comm.relay is not None and relay_step < n_steps - 1:
            @pl.when(step == relay_step)
            def _():
                comm.relay(cins, couts, csems)

        @pl.when(step == n_steps - 1)
        def _():
            if comm.relay is not None and relay_step == n_steps - 1:
                comm.relay(cins, couts, csems)
            comm.finish(cins, couts, csems)

    aliases = {n_pre + n_in + i: n_out + o for i, o in comm.aliases.items()}
    aliases.update({n_pre + i: o for i, o in (io_aliases or {}).items()})
    all_in = list(in_specs) + [HBM_SPEC] * n_ci
    all_out = o_specs + [HBM_SPEC] * n_co
    all_scratch = list(scratch_shapes) + comm.sems
    params = _cparams(*(["arbitrary"] * len(grid)))
    if prefetch is None:
        res = pl.pallas_call(
            carrier, grid=grid, in_specs=all_in, out_specs=all_out, out_shape=o_shapes + comm.out_shapes,
            input_output_aliases=aliases, scratch_shapes=all_scratch, name=name,
            compiler_params=params)(*args, *comm.arrays)
    else:
        grid_spec = pltpu.PrefetchScalarGridSpec(num_scalar_prefetch=1, grid=grid, in_specs=all_in,
                                                 out_specs=all_out, scratch_shapes=all_scratch)
        res = pl.pallas_call(
            carrier, grid_spec=grid_spec, out_shape=o_shapes + comm.out_shapes, input_output_aliases=aliases,
            name=name, compiler_params=params)(prefetch, *args, *comm.arrays)
    mine = list(res[:n_out]) if multi else res[0]
    return mine, list(res[n_out:])


def run_comm(name, comm):
    def body(*refs):
        n_ci, n_co = len(comm.arrays), len(comm.out_shapes)
        cins, couts, csems = refs[:n_ci], refs[n_ci:n_ci + n_co], refs[n_ci + n_co:]
        comm.start(cins, couts, csems)
        if comm.relay is not None:
            comm.relay(cins, couts, csems)
        comm.finish(cins, couts, csems)

    return pl.pallas_call(
        body, in_specs=[HBM_SPEC] * len(comm.arrays), out_specs=[HBM_SPEC] * len(comm.out_shapes),
        out_shape=comm.out_shapes, input_output_aliases=comm.aliases, scratch_shapes=comm.sems,
        name=name)(*comm.arrays)


def _mm(name, a, b, *, contract, grid, a_spec, b_spec, o_spec, out_shape, res=None, comm=None):
    nk = grid[2]

    def body(*refs):
        if res is not None:
            a_ref, b_ref, r_ref, o_ref = refs[:4]
        else:
            a_ref, b_ref, o_ref = refs[:3]
            r_ref = None
        p = _dot(a_ref[...].astype(BF16), b_ref[...].astype(BF16), contract)

        def finish(acc):
            if r_ref is not None:
                acc = acc + r_ref[...]
            o_ref[...] = acc.astype(o_ref.dtype)

        if nk == 1:
            finish(p)
        else:
            acc_ref = refs[-1]
            k = pl.program_id(2)

            @pl.when(k == 0)
            def _():
                acc_ref[...] = p

            @pl.when(k > 0)
            def _():
                acc_ref[...] += p

            @pl.when(k == nk - 1)
            def _():
                finish(acc_ref[...])

    in_specs = [a_spec, b_spec]
    args = [a, b]
    if res is not None:
        in_specs.append(o_spec)
        args.append(res)
    blk = tuple(d for d in o_spec.block_shape if d is not None)
    scratch = [] if nk == 1 else [pltpu.VMEM(blk, F32)]
    return _call(body, name=name, grid=grid, in_specs=in_specs, out_specs=o_spec, out_shape=out_shape,
                 args=args, scratch_shapes=scratch, sem=("parallel", "parallel", "arbitrary"), comm=comm)


def mm_nn_cols(name, a, w4, out_dtype, comm=None):
    M, K = a.shape
    _, _, C = w4.shape
    tm = _pick(M, (1024, 512, 256))
    tn = _pick(C, (1024, 512, 256, 128))
    nps = C // tn
    return _mm(name, a, w4, contract=NN, grid=(M // tm, 4 * nps, 1),
               a_spec=pl.BlockSpec((tm, K), lambda i, j, k: (i, 0)),
               b_spec=pl.BlockSpec((None, K, tn), lambda i, j, k: (j // nps, 0, j % nps)),
               o_spec=pl.BlockSpec((tm, tn), lambda i, j, k: (i, j)),
               out_shape=jax.ShapeDtypeStruct((M, 4 * C), out_dtype), comm=comm)


def proj_cols_own(name, a, w_own, place, comm):
    M, K = a.shape
    C = w_own.shape[1]
    tm = _pick(M, (1024, 512, 256))
    tn = _pick(C, (512, 256, 128))
    nps = C // tn

    def body(s_ref, a_ref, b_ref, o_ref):
        o_ref[...] = _dot(a_ref[...], b_ref[...].astype(BF16), NN).astype(o_ref.dtype)

    return _call(body, name=name, grid=(M // tm, nps),
                 in_specs=[pl.BlockSpec((tm, K), lambda i, j, s: (i, 0)),
                           pl.BlockSpec((K, tn), lambda i, j, s: (0, j))],
                 out_specs=pl.BlockSpec((tm, tn), lambda i, j, s: (i, s[0] * nps + j)),
                 out_shape=jax.ShapeDtypeStruct((M, N_CHIPS * C), BF16), args=[a, w_own], comm=comm,
                 prefetch=place)


def proj_cols_rest(name, a, w4, partial, place, masks, comm):
    M, K = a.shape
    C = w4.shape[2]
    tm = _pick(M, (1024, 512, 256))
    tn = _pick(C, (1792, 1536, 1024, 512, 256, 128))
    nps = C // tn
    assert len(masks) in (1, 2)
    step = masks[-1] - masks[0]

    def slot(j, s):
        return jnp.bitwise_xor(s[0], masks[0] + step * (j // nps))

    def body(s_ref, a_ref, b_ref, part_ref, o_ref):
        o_ref[...] = _dot(a_ref[...], b_ref[...], NN).astype(o_ref.dtype)

    return _call(body, name=name, grid=(M // tm, len(masks) * nps),
                 in_specs=[pl.BlockSpec((tm, K), lambda i, j, s: (i, 0)),
                           pl.BlockSpec((None, K, tn), lambda i, j, s: (slot(j, s), 0, j % nps)),
                           HBM_SPEC],
                 out_specs=pl.BlockSpec((tm, tn), lambda i, j, s: (i, slot(j, s) * nps + j % nps)),
                 out_shape=jax.ShapeDtypeStruct(partial.shape, partial.dtype), args=[a, w4, partial],
                 comm=comm, prefetch=place, io_aliases={2: 0})


def mm_nn(name, a, w, out_dtype, res=None, comm=None):
    M, K = a.shape
    N = w.shape[1]
    tm = _pick(M, (1024, 512, 256))
    tn = _pick(N, (1024, 512, 256, 128) if K <= 2048 else (512, 256, 128))
    return _mm(name, a, w, contract=NN, grid=(M // tm, N // tn, 1),
               a_spec=pl.BlockSpec((tm, K), lambda i, j, k: (i, 0)),
               b_spec=pl.BlockSpec((K, tn), lambda i, j, k: (0, j)),
               o_spec=pl.BlockSpec((tm, tn), lambda i, j, k: (i, j)),
               out_shape=jax.ShapeDtypeStruct((M, N), out_dtype), res=res, comm=comm)


def mm_nt_cols(name, a, w4, out_dtype, comm=None):
    M = a.shape[0]
    _, K, C = w4.shape
    tm = _pick(M, (1024, 512, 256))
    tn = _pick(K, (1024, 512, 256, 128))
    tk = _pick(C, (3584, 3072, 1792, 1536, 1024, 512, 256, 128))
    kps = C // tk
    return _mm(name, a, w4, contract=NT, grid=(M // tm, K // tn, 4 * kps),
               a_spec=pl.BlockSpec((tm, tk), lambda i, j, k: (i, k)),
               b_spec=pl.BlockSpec((None, tn, tk), lambda i, j, k: (k // kps, j, k % kps)),
               o_spec=pl.BlockSpec((tm, tn), lambda i, j, k: (i, j)),
               out_shape=jax.ShapeDtypeStruct((M, K), out_dtype), comm=comm)


def mm_nt(name, a, w, out_dtype):
    M, C = a.shape
    N = w.shape[0]
    tm = _pick(M, (1024, 512, 256))
    tn = _pick(N, (1024, 512, 256, 128))
    return _mm(name, a, w, contract=NT, grid=(M // tm, N // tn, 1),
               a_spec=pl.BlockSpec((tm, C), lambda i, j, k: (i, 0)),
               b_spec=pl.BlockSpec((tn, C), lambda i, j, k: (j, 0)),
               o_spec=pl.BlockSpec((tm, tn), lambda i, j, k: (i, j)),
               out_shape=jax.ShapeDtypeStruct((M, N), out_dtype))


def mm_tn_cols(name, a, b, comm=None):
    S, K = a.shape
    C = b.shape[1] // 4
    ts = _pick(S, (2048, 1024, 512, 256))
    tko = _pick(K, (1024, 512, 256, 128))
    tn = _pick(C, (1792, 1536, 1024, 512, 256, 128))
    nps = C // tn
    return _mm(name, a, b, contract=TN, grid=(K // tko, 4 * nps, S // ts),
               a_spec=pl.BlockSpec((ts, tko), lambda i, j, k: (k, i)),
               b_spec=pl.BlockSpec((ts, tn), lambda i, j, k: (k, j)),
               o_spec=pl.BlockSpec((None, tko, tn), lambda i, j, k: (j // nps, i, j % nps)),
               out_shape=jax.ShapeDtypeStruct((4, K, C), BF16), comm=comm)


def mm_tn(name, a, b):
    S, K = a.shape
    N = b.shape[1]
    ts = _pick(S, (2048, 1024, 512, 256))
    tko = _pick(K, (1024, 512, 256, 128))
    tn = _pick(N, (1024, 512, 256, 128))
    return _mm(name, a, b, contract=TN, grid=(K // tko, N // tn, S // ts),
               a_spec=pl.BlockSpec((ts, tko), lambda i, j, k: (k, i)),
               b_spec=pl.BlockSpec((ts, tn), lambda i, j, k: (k, j)),
               o_spec=pl.BlockSpec((tko, tn), lambda i, j, k: (i, j)),
               out_shape=jax.ShapeDtypeStruct((K, N), BF16))


def rms_fwd(name, x, g):
    S, D = x.shape
    T = _pick(S, (512, 256))

    def body(x_ref, g_ref, o_ref):
        xf = x_ref[...]
        r = lax.rsqrt(jnp.mean(xf * xf, axis=-1, keepdims=True) + EPS)
        o_ref[...] = (xf * r * g_ref[...]).astype(o_ref.dtype)

    return pl.pallas_call(
        body, grid=(S // T,),
        in_specs=[pl.BlockSpec((T, D), lambda i: (i, 0)), pl.BlockSpec((1, D), lambda i: (0, 0))],
        out_specs=pl.BlockSpec((T, D), lambda i: (i, 0)),
        out_shape=jax.ShapeDtypeStruct((S, D), BF16), name=name,
        compiler_params=_cparams("parallel"))(x, g.reshape(1, D))


def rms_bwd(name, x, g, dys, dres):
    S, D = x.shape
    T = _pick(S, (256,))
    ndy = len(dys)
    has_res = dres is not None

    def body(*refs):
        x_ref, g_ref = refs[0], refs[1]
        dy_refs = refs[2:2 + ndy]
        r_ref = refs[2 + ndy] if has_res else None
        dx_ref, dxb_ref, dg_ref = refs[-3], refs[-2], refs[-1]
        i = pl.program_id(0)
        xf = x_ref[...]
        r = lax.rsqrt(jnp.mean(xf * xf, axis=-1, keepdims=True) + EPS)
        xhat = xf * r
        dy = dy_refs[0][...].astype(F32)
        for d in dy_refs[1:]:
            dy = dy + d[...].astype(F32)
        dxhat = dy * g_ref[...]
        dx = r * (dxhat - xhat * jnp.mean(dxhat * xhat, axis=-1, keepdims=True))
        if has_res:
            dx = dx + r_ref[...]
        dx_ref[...] = dx
        dxb_ref[...] = dx.astype(dxb_ref.dtype)
        dg = jnp.sum(dy * xhat, axis=0, keepdims=True)

        @pl.when(i == 0)
        def _():
            dg_ref[...] = dg

        @pl.when(i > 0)
        def _():
            dg_ref[...] += dg

    row = pl.BlockSpec((T, D), lambda i: (i, 0))
    vec = pl.BlockSpec((1, D), lambda i: (0, 0))
    args = [x, g.reshape(1, D), *dys] + ([dres] if has_res else [])
    return pl.pallas_call(
        body, grid=(S // T,),
        in_specs=[row, vec] + [row] * (ndy + int(has_res)),
        out_specs=[row, row, vec],
        out_shape=[jax.ShapeDtypeStruct((S, D), F32), jax.ShapeDtypeStruct((S, D), BF16),
                   jax.ShapeDtypeStruct((1, D), F32)],
        name=name, compiler_params=_cparams("arbitrary"))(*args)


def loss_head(name, h, g, target):
    S, D = h.shape
    T = _pick(S, (256,))

    def body(h_ref, g_ref, t_ref, loss_ref, dg_ref, dh_ref, dhb_ref):
        i = pl.program_id(0)
        xf = h_ref[...]
        gv = g_ref[...]
        r = lax.rsqrt(jnp.mean(xf * xf, axis=-1, keepdims=True) + EPS)
        xhat = xf * r
        err = xhat * gv - t_ref[...]
        part = 0.5 * jnp.sum(jnp.sum(err * err, axis=-1, keepdims=True), axis=0, keepdims=True) / D
        dout = err / D
        dxhat = dout * gv
        dh = r * (dxhat - xhat * jnp.mean(dxhat * xhat, axis=-1, keepdims=True))
        dh_ref[...] = dh
        dhb_ref[...] = dh.astype(dhb_ref.dtype)
        dg = jnp.sum(dout * xhat, axis=0, keepdims=True)
        lrow = jnp.broadcast_to(part, (1, LANES))

        @pl.when(i == 0)
        def _():
            dg_ref[...] = dg
            loss_ref[...] = lrow

        @pl.when(i > 0)
        def _():
            dg_ref[...] += dg
            loss_ref[...] += lrow

    row = pl.BlockSpec((T, D), lambda i: (i, 0))
    vec = pl.BlockSpec((1, D), lambda i: (0, 0))
    return pl.pallas_call(
        body, grid=(S // T,), in_specs=[row, vec, row],
        out_specs=[pl.BlockSpec((1, LANES), lambda i: (0, 0)), vec, row, row],
        out_shape=[jax.ShapeDtypeStruct((1, LANES), F32), jax.ShapeDtypeStruct((1, D), F32),
                   jax.ShapeDtypeStruct((S, D), F32), jax.ShapeDtypeStruct((S, D), BF16)],
        name=name, compiler_params=_cparams("arbitrary"))(h, g.reshape(1, D), target)


def _attn_tq(S):
    return _pick(S, (512,))


def band_bias_table(rel_bias, tq):
    H = rel_bias.shape[0]
    w = 2 * tq
    nbits = int(np.log2(tq))
    assert (1 << nbits) == tq and (N_PAST_CHUNKS + 2) * CHUNK - 1 <= w
    c = np.arange(w)
    d0 = np.where(c <= tq + CHUNK - 1, tq - c, tq + w - c)
    base = jnp.take(rel_bias.astype(F32), jnp.asarray(np.clip(d0, -MAX_REL, MAX_REL) + MAX_REL), axis=1)

    def body(b_ref, o_ref):
        x = jnp.broadcast_to(b_ref[...], (tq, w))
        row = lax.broadcasted_iota(jnp.int32, (tq, w), 0)
        col = lax.broadcasted_iota(jnp.int32, (tq, w), 1)
        for b in range(nbits):
            x = jnp.where(((row >> b) & 1) == 1, pltpu.roll(x, 1 << b, 1), x)
        qc = row // CHUNK
        kc = col // CHUNK - tq // CHUNK
        o_ref[...] = jnp.where((kc <= qc) & (kc >= qc - N_PAST_CHUNKS), x, NEG)

    return pl.pallas_call(
        body, grid=(H,), in_specs=[pl.BlockSpec((None, 1, w), lambda h: (h, 0, 0))],
        out_specs=pl.BlockSpec((None, tq, w), lambda h: (h, 0, 0)),
        out_shape=jax.ShapeDtypeStruct((H, tq, w), F32), name="band_bias_table",
        compiler_params=_cparams("parallel"))(base.reshape(H, 1, w))


def _attn_subblocks(tq):
    sub = tq // 2
    assert sub % CHUNK == 0 and N_PAST_CHUNKS * CHUNK == tq
    return sub, 3


def attn_fwd(proj, bm, D, comm=None):
    S = proj.shape[0]
    H = D // HEAD_DIM_A
    tq = _attn_tq(S)
    nb = S // tq
    scale = HEAD_DIM_A ** -0.5

    sub, n_sub = _attn_subblocks(tq)

    def body(q_ref, kp_ref, kc_ref, vp_ref, vc_ref, bm_ref, o_ref, lse_ref):
        i = pl.program_id(1)
        for qh in range(tq // sub):
            rows = slice(qh * sub, (qh + 1) * sub)
            q = q_ref[rows, :]
            ss = []
            for kb in range(qh, qh + n_sub):
                k_ref, krows = (kp_ref, kb) if kb < tq // sub else (kc_ref, kb - tq // sub)
                s = _dot(q, k_ref[krows * sub:(krows + 1) * sub, :], NT) * scale + bm_ref[rows, kb * sub:(kb + 1) * sub]
                if kb < tq // sub:
                    s = jnp.where(i == 0, NEG, s)
                ss.append(s)
            m = functools.reduce(jnp.maximum, [jnp.max(s, axis=-1, keepdims=True) for s in ss])
            ps = [jnp.exp(s - m) for s in ss]
            l = functools.reduce(jnp.add, [jnp.sum(p, axis=-1, keepdims=True) for p in ps])
            o = None
            for p, kb in zip(ps, range(qh, qh + n_sub)):
                v_ref, vrows = (vp_ref, kb) if kb < tq // sub else (vc_ref, kb - tq // sub)
                t = _dot(p.astype(BF16), v_ref[vrows * sub:(vrows + 1) * sub, :], NN)
                o = t if o is None else o + t
            o_ref[rows, :] = (o / l).astype(o_ref.dtype)
            lse_ref[rows, :] = m + jnp.log(l)

    def col(base):
        return (pl.BlockSpec((tq, HEAD_DIM_A), lambda h, i: (jnp.maximum(i - 1, 0), base + h)),
                pl.BlockSpec((tq, HEAD_DIM_A), lambda h, i: (i, base + h)))

    kp, kc = col(H)
    vp, vc = col(2 * H)
    return _call(
        body, name="attn_fwd", grid=(H, nb),
        in_specs=[pl.BlockSpec((tq, HEAD_DIM_A), lambda h, i: (i, h)), kp, kc, vp, vc,
                  pl.BlockSpec((None, tq, 2 * tq), lambda h, i: (h, 0, 0))],
        out_specs=[pl.BlockSpec((tq, HEAD_DIM_A), lambda h, i: (i, h)),
                   pl.BlockSpec((None, tq, 1), lambda h, i: (h, i, 0))],
        out_shape=[jax.ShapeDtypeStruct((S, D), BF16), jax.ShapeDtypeStruct((H, S, 1), F32)],
        args=[proj, proj, proj, proj, proj, bm], sem=("parallel", "arbitrary"), comm=comm)


def attn_bwd(proj, ya, dya, lse, bm, D, comm=None):
    S = proj.shape[0]
    H = D // HEAD_DIM_A
    tq = _attn_tq(S)
    nb = S // tq
    scale = HEAD_DIM_A ** -0.5
    sub, n_sub = _attn_subblocks(tq)

    def body(q_ref, kp_ref, kc_ref, vp_ref, vc_ref, o_ref, do_ref, lse_ref, bm_ref,
             dq_ref, dkc_ref, dkp_ref, dvc_ref, dvp_ref, ds_ref):
        i = pl.program_id(1)
        per = tq // sub

        @pl.when(i == 0)
        def _():
            ds_ref[...] = jnp.zeros_like(ds_ref)

        dk_acc = [None] * (2 * per)
        dv_acc = [None] * (2 * per)
        for qh in range(per):
            rows = slice(qh * sub, (qh + 1) * sub)
            q = q_ref[rows, :]
            do = do_ref[rows, :]
            delta = jnp.sum(do.astype(F32) * o_ref[rows, :].astype(F32), axis=-1, keepdims=True)
            lse_v = lse_ref[rows, :]
            dq = None
            for kb in range(qh, qh + n_sub):
                k_ref, v_ref, kr = (kp_ref, vp_ref, kb) if kb < per else (kc_ref, vc_ref, kb - per)
                k = k_ref[kr * sub:(kr + 1) * sub, :]
                cols = slice(kb * sub, (kb + 1) * sub)
                s = _dot(q, k, NT) * scale + bm_ref[rows, cols]
                if kb < per:
                    s = jnp.where(i == 0, NEG, s)
                p = jnp.exp(s - lse_v)
                dv = _dot(p.astype(BF16), do, TN)
                dp = _dot(do, v_ref[kr * sub:(kr + 1) * sub, :], NT)
                ds = p * (dp - delta)
                dsb = ds.astype(BF16)
                t = _dot(dsb, k, NN)
                dq = t if dq is None else dq + t
                dk = _dot(dsb, q, TN)
                dk_acc[kb] = dk if dk_acc[kb] is None else dk_acc[kb] + dk
                dv_acc[kb] = dv if dv_acc[kb] is None else dv_acc[kb] + dv
                ds_ref[rows, cols] += ds
            dq_ref[rows, :] = (dq * scale).astype(dq_ref.dtype)
        for kb in range(2 * per):
            dk_ref, dv_ref, kr = (dkp_ref, dvp_ref, kb) if kb < per else (dkc_ref, dvc_ref, kb - per)
            dk_ref[kr * sub:(kr + 1) * sub, :] = (dk_acc[kb] * scale).astype(dk_ref.dtype)
            dv_ref[kr * sub:(kr + 1) * sub, :] = dv_acc[kb].astype(dv_ref.dtype)

    def col(base):
        return (pl.BlockSpec((tq, HEAD_DIM_A), lambda h, i: (jnp.maximum(i - 1, 0), base + h)),
                pl.BlockSpec((tq, HEAD_DIM_A), lambda h, i: (i, base + h)))

    kp, kc = col(H)
    vp, vc = col(2 * H)
    blk = pl.BlockSpec((tq, HEAD_DIM_A), lambda h, i: (i, h))
    sd = jax.ShapeDtypeStruct((S, D), BF16)
    return _call(
        body, name="attn_bwd", grid=(H, nb),
        in_specs=[blk, kp, kc, vp, vc, blk, blk,
                  pl.BlockSpec((None, tq, 1), lambda h, i: (h, i, 0)),
                  pl.BlockSpec((None, tq, 2 * tq), lambda h, i: (h, 0, 0))],
        out_specs=[blk, blk, blk, blk, blk, pl.BlockSpec((None, tq, 2 * tq), lambda h, i: (h, 0, 0))],
        out_shape=[sd, sd, sd, sd, sd, jax.ShapeDtypeStruct((H, tq, 2 * tq), F32)],
        args=[proj, proj, proj, proj, proj, ya, dya, lse, bm], sem=("parallel", "arbitrary"), comm=comm)


def rel_bias_grad(ds_sum):
    H, tq, w = ds_sum.shape
    nbin = 2 * MAX_REL + 1
    nbin_pad = 3 * LANES
    d_lo, d_hi = -(CHUNK - 1), (N_PAST_CHUNKS + 1) * CHUNK - 1
    assert d_hi - d_lo + 1 <= w
    onehot = np.zeros((w, nbin_pad), np.float32)
    for d in range(d_lo, d_hi + 1):
        onehot[(tq - d) % w, int(np.clip(d, -MAX_REL, MAX_REL)) + MAX_REL] = 1.0
    nbits = int(np.log2(tq))
    assert (1 << nbits) == tq

    def body(ds_ref, m_ref, o_ref):
        x = ds_ref[...]
        row = lax.broadcasted_iota(jnp.int32, x.shape, 0)
        for b in range(nbits):
            rolled = pltpu.roll(x, w - (1 << b), 1)
            x = jnp.where(((row >> b) & 1) == 1, rolled, x)
        t = jnp.sum(x, axis=0, keepdims=True)
        o_ref[...] = lax.dot_general(t, m_ref[...], (NN, ((), ())), precision=lax.Precision.HIGHEST,
                                     preferred_element_type=F32)

    out = pl.pallas_call(
        body, grid=(H,),
        in_specs=[pl.BlockSpec((None, tq, w), lambda h: (h, 0, 0)),
                  pl.BlockSpec((w, nbin_pad), lambda h: (0, 0))],
        out_specs=pl.BlockSpec((None, 1, nbin_pad), lambda h: (h, 0, 0)),
        out_shape=jax.ShapeDtypeStruct((H, 1, nbin_pad), F32),
        name="rel_bias_grad", compiler_params=_cparams("parallel"))(ds_sum, jnp.asarray(onehot))
    return out[:, 0, :nbin]


def _conv_t(S):
    return _pick(S, (256,))


ROW_CHUNK = 16


def _row_loop(n_rows, step):
    def one(r, carry):
        step(pl.ds(pl.multiple_of(r * ROW_CHUNK, ROW_CHUNK), ROW_CHUNK))
        return carry

    lax.fori_loop(0, n_rows // ROW_CHUNK, one, 0)


def _fill_zbuf(zbuf, ap_ref, bp_ref, a_ref, b_ref, i):
    zp = ap_ref[...].astype(F32) * _sigmoid(bp_ref[...].astype(F32))
    zbuf[0:CONV_HALO, :] = jnp.where(i == 0, 0.0, zp)

    def step(rows):
        below = pl.ds(pl.multiple_of(rows.start + CONV_HALO, ROW_CHUNK), ROW_CHUNK)
        zbuf[below, :] = a_ref[rows, :].astype(F32) * _sigmoid(b_ref[rows, :].astype(F32))

    _row_loop(a_ref.shape[0], step)


def _shifted_windows(buf, shifted, lanes, T):
    rows = T + CONV_HALO - SUBLANES
    for b in range(1, SUBLANES):
        shifted[b - 1] = buf[pl.ds(b, rows), lanes]

    def window(off, r0=0, n=T):
        a, b = divmod(off, SUBLANES)
        if b == 0:
            return buf[pl.ds(r0 + off, n), lanes]
        return shifted[b - 1, pl.ds(r0 + a * SUBLANES, n), :]

    return window


def _shifted_scratch(T):
    return pltpu.VMEM((SUBLANES - 1, T + CONV_HALO - SUBLANES, LANES), F32)


def conv_gate_fwd(proj, ya, cw, cb, lng, lnb, D, comm=None):
    S = proj.shape[0]
    T = _conv_t(S)
    hb = T // CONV_HALO
    nlb = D // LANES

    def body(ap_ref, bp_ref, a_ref, b_ref, ga_ref, gb_ref, ya_ref, cw_ref, cb_ref, lng_ref, lnb_ref,
             y_ref, c_ref, zbuf, zsh):
        i = pl.program_id(0)
        _fill_zbuf(zbuf, ap_ref, bp_ref, a_ref, b_ref, i)

        def lane_block(lb, carry):
            lanes = pl.ds(pl.multiple_of(lb * LANES, LANES), LANES)
            z_at = _shifted_windows(zbuf, zsh, lanes, T)
            acc = jnp.zeros((T, LANES), F32)
            for k in range(CONV_WIDTH):
                acc = acc + cw_ref[k:k + 1, lanes] * z_at(CONV_HALO - CONV_WIDTH + 1 + k)
            c_ref[:, lanes] = acc + cb_ref[:, lanes]
            return carry

        lax.fori_loop(0, nlb, lane_block, 0)

        def norm_and_gate(rows):
            c = c_ref[rows, :]
            mu = jnp.mean(c, axis=-1, keepdims=True)
            xc = c - mu
            rstd = lax.rsqrt(jnp.mean(xc * xc, axis=-1, keepdims=True) + EPS)
            ln = xc * rstd * lng_ref[...] + lnb_ref[...]
            yb = ln * _sigmoid(ln)
            ga = ga_ref[rows, :].astype(F32)
            gb = gb_ref[rows, :].astype(F32)
            y_ref[rows, :D] = (ya_ref[rows, :].astype(F32) * (ga * _sigmoid(ga))).astype(y_ref.dtype)
            y_ref[rows, D:] = (yb * (gb * _sigmoid(gb))).astype(y_ref.dtype)

        _row_loop(T, norm_and_gate)

    def cur(cidx):
        return pl.BlockSpec((T, D), lambda i: (i, cidx))

    def prev(cidx):
        return pl.BlockSpec((CONV_HALO, D), lambda i: (jnp.maximum(i * hb - 1, 0), cidx))

    vec = pl.BlockSpec((1, D), lambda i: (0, 0))
    return _call(
        body, name="conv_gate_fwd", grid=(S // T,),
        in_specs=[prev(3), prev(4), cur(3), cur(4), cur(5), cur(6), pl.BlockSpec((T, D), lambda i: (i, 0)),
                  pl.BlockSpec((CONV_HALO, D), lambda i: (0, 0)), vec, vec, vec],
        out_specs=[pl.BlockSpec((T, 2 * D), lambda i: (i, 0)), pl.BlockSpec((T, D), lambda i: (i, 0))],
        out_shape=[jax.ShapeDtypeStruct((S, 2 * D), BF16), jax.ShapeDtypeStruct((S, D), F32)],
        scratch_shapes=[pltpu.VMEM((T + CONV_HALO, D), F32), _shifted_scratch(T)],
        args=[proj, proj, proj, proj, proj, proj, ya, cw, cb, lng, lnb], sem=("parallel",), comm=comm)


def conv_gate_bwd_a(dy0, proj, ya, cpre, lng, lnb, D):
    S = proj.shape[0]
    T = _conv_t(S)

    def body(dy_ref, ga_ref, gb_ref, ya_ref, c_ref, lng_ref, lnb_ref,
             dya_ref, dg_ref, dc_ref, dlng_ref, dlnb_ref):
        i = pl.program_id(0)

        c = c_ref[...]
        gv = lng_ref[...]
        mu = jnp.mean(c, axis=-1, keepdims=True)
        xc = c - mu
        rstd = lax.rsqrt(jnp.mean(xc * xc, axis=-1, keepdims=True) + EPS)
        xhat = xc * rstd
        ln = xhat * gv + lnb_ref[...]
        sl = _sigmoid(ln)
        yb = ln * sl
        ga = ga_ref[...].astype(F32)
        gb = gb_ref[...].astype(F32)
        sa = _sigmoid(ga)
        sb = _sigmoid(gb)
        dy_a = dy_ref[:, :D].astype(F32)
        dy_b = dy_ref[:, D:].astype(F32)
        dya_ref[...] = (dy_a * (ga * sa)).astype(dya_ref.dtype)
        dg_ref[:, :D] = (dy_a * ya_ref[...].astype(F32) * (sa * (1.0 + ga * (1.0 - sa)))).astype(dg_ref.dtype)
        dg_ref[:, D:] = (dy_b * yb * (sb * (1.0 + gb * (1.0 - sb)))).astype(dg_ref.dtype)
        dln = dy_b * (gb * sb) * (sl * (1.0 + ln * (1.0 - sl)))
        dxhat = dln * gv
        dc_ref[...] = rstd * (dxhat - jnp.mean(dxhat, axis=-1, keepdims=True)
                              - xhat * jnp.mean(dxhat * xhat, axis=-1, keepdims=True))
        dlng = jnp.sum(dln * xhat, axis=0, keepdims=True)
        dlnb = jnp.sum(dln, axis=0, keepdims=True)

        @pl.when(i == 0)
        def _():
            dlng_ref[...] = dlng
            dlnb_ref[...] = dlnb

        @pl.when(i > 0)
        def _():
            dlng_ref[...] += dlng
            dlnb_ref[...] += dlnb

    row = pl.BlockSpec((T, D), lambda i: (i, 0))
    vec = pl.BlockSpec((1, D), lambda i: (0, 0))
    return pl.pallas_call(
        body, grid=(S // T,),
        in_specs=[pl.BlockSpec((T, 2 * D), lambda i: (i, 0)),
                  pl.BlockSpec((T, D), lambda i: (i, 5)), pl.BlockSpec((T, D), lambda i: (i, 6)),
                  row, row, vec, vec],
        out_specs=[row, pl.BlockSpec((T, 2 * D), lambda i: (i, 0)), row, vec, vec],
        out_shape=[jax.ShapeDtypeStruct((S, D), BF16), jax.ShapeDtypeStruct((S, 2 * D), BF16),
                   jax.ShapeDtypeStruct((S, D), F32), jax.ShapeDtypeStruct((1, D), F32),
                   jax.ShapeDtypeStruct((1, D), F32)],
        name="conv_gate_bwd_a", compiler_params=_cparams("arbitrary"))(
            dy0, proj, proj, ya, cpre, lng, lnb)


def conv_gate_bwd_b(dc, proj, cw, D, comm=None):
    S = proj.shape[0]
    T = _conv_t(S)
    hb = T // CONV_HALO
    nt = S // T
    nlb = D // LANES
    half = T // 2

    def body(dc_ref, dn_ref, ap_ref, bp_ref, a_ref, b_ref, cw_ref, da_ref, db_ref, dcw_ref, dcb_ref,
             zbuf, dcbuf, zsh, dcsh, dcw8):
        i = pl.program_id(0)
        _fill_zbuf(zbuf, ap_ref, bp_ref, a_ref, b_ref, i)
        dcv = dc_ref[...]
        dcbuf[0:T, :] = dcv
        dcbuf[T:, :] = jnp.where(i == nt - 1, 0.0, dn_ref[...])

        @pl.when(i == 0)
        def _():
            dcw8[...] = jnp.zeros_like(dcw8)
            dcb_ref[...] = jnp.zeros_like(dcb_ref)

        dcb_ref[...] += jnp.sum(dcv, axis=0, keepdims=True)

        def lane_block(lb, carry):
            lanes = pl.ds(pl.multiple_of(lb * LANES, LANES), LANES)
            z_at = _shifted_windows(zbuf, zsh, lanes, T)
            dc_at = _shifted_windows(dcbuf, dcsh, lanes, T)
            for r0 in range(0, T, half):
                d0 = dcbuf[r0:r0 + half, lanes]
                dz = jnp.zeros((half, LANES), F32)
                for k in range(CONV_WIDTH):
                    dz = dz + cw_ref[k:k + 1, lanes] * dc_at(CONV_WIDTH - 1 - k, r0, half)
                    prod = d0 * z_at(CONV_HALO - CONV_WIDTH + 1 + k, r0, half)
                    dcw8[pl.ds(k * SUBLANES, SUBLANES), lanes] += jnp.sum(
                        prod.reshape(half // SUBLANES, SUBLANES, LANES), axis=0)
                av = a_ref[r0:r0 + half, lanes].astype(F32)
                sg = _sigmoid(b_ref[r0:r0 + half, lanes].astype(F32))
                da_ref[r0:r0 + half, lanes] = (dz * sg).astype(da_ref.dtype)
                db_ref[r0:r0 + half, lanes] = (dz * av * sg * (1.0 - sg)).astype(db_ref.dtype)
            return carry

        lax.fori_loop(0, nlb, lane_block, 0)

        @pl.when(i == nt - 1)
        def _():
            dcw_ref[...] = jnp.sum(dcw8[...].reshape(CONV_HALO, SUBLANES, D), axis=1)

    def cur(cidx):
        return pl.BlockSpec((T, D), lambda i: (i, cidx))

    def prev(cidx):
        return pl.BlockSpec((CONV_HALO, D), lambda i: (jnp.maximum(i * hb - 1, 0), cidx))

    row = pl.BlockSpec((T, D), lambda i: (i, 0))
    nxt = pl.BlockSpec((CONV_HALO, D), lambda i: (jnp.minimum((i + 1) * hb, nt * hb - 1), 0))
    return _call(
        body, name="conv_gate_bwd_b", grid=(nt,),
        in_specs=[row, nxt, prev(3), prev(4), cur(3), cur(4), pl.BlockSpec((CONV_HALO, D), lambda i: (0, 0))],
        out_specs=[row, row, pl.BlockSpec((CONV_HALO, D), lambda i: (0, 0)),
                   pl.BlockSpec((1, D), lambda i: (0, 0))],
        out_shape=[jax.ShapeDtypeStruct((S, D), BF16), jax.ShapeDtypeStruct((S, D), BF16),
                   jax.ShapeDtypeStruct((CONV_HALO, D), F32), jax.ShapeDtypeStruct((1, D), F32)],
        scratch_shapes=[pltpu.VMEM((T + CONV_HALO, D), F32), pltpu.VMEM((T + CONV_HALO, D), F32),
                        _shifted_scratch(T), _shifted_scratch(T), pltpu.VMEM((CONV_HALO * SUBLANES, D), F32)],
        args=[dc, dc, proj, proj, proj, proj, cw], sem=("arbitrary",), comm=comm)


def assemble_dproj0(dq, dkc, dkp, dvc, dvp, da, db, dgate, D):
    S = dq.shape[0]
    tq = _attn_tq(S)
    T = _pick(S, (256,))
    shift = tq // T
    nt = S // T

    def body(dq_ref, dkc_ref, dkp_ref, dvc_ref, dvp_ref, da_ref, db_ref, dg_ref, o_ref):
        i = pl.program_id(0)
        last = i + shift >= nt
        o_ref[:, 0:D] = dq_ref[...]
        dk = dkc_ref[...].astype(F32) + jnp.where(last, 0.0, dkp_ref[...].astype(F32))
        dv = dvc_ref[...].astype(F32) + jnp.where(last, 0.0, dvp_ref[...].astype(F32))
        o_ref[:, D:2 * D] = dk.astype(o_ref.dtype)
        o_ref[:, 2 * D:3 * D] = dv.astype(o_ref.dtype)
        o_ref[:, 3 * D:4 * D] = da_ref[...]
        o_ref[:, 4 * D:5 * D] = db_ref[...]
        o_ref[:, 5 * D:] = dg_ref[...]

    row = pl.BlockSpec((T, D), lambda i: (i, 0))
    nxt = pl.BlockSpec((T, D), lambda i: (jnp.minimum(i + shift, nt - 1), 0))
    return pl.pallas_call(
        body, grid=(nt,),
        in_specs=[row, row, nxt, row, nxt, row, row, pl.BlockSpec((T, 2 * D), lambda i: (i, 0))],
        out_specs=pl.BlockSpec((T, 7 * D), lambda i: (i, 0)),
        out_shape=jax.ShapeDtypeStruct((S, 7 * D), BF16),
        name="assemble_dproj0", compiler_params=_cparams("parallel"))(dq, dkc, dkp, dvc, dvp, da, db, dgate)


def _sgu_t(S):
    return _pick(S, (256, 128))


def _ws_masked(ws_ref, g):
    row = lax.broadcasted_iota(jnp.int32, (GMLP_CHUNK, GMLP_CHUNK), 0) // CHUNK
    col = lax.broadcasted_iota(jnp.int32, (GMLP_CHUNK, GMLP_CHUNK), 1) // CHUNK
    return jnp.where(row >= col, ws_ref[g], 0.0), row >= col


def sgu_fwd(proj, lng, lnb, ws, bst, MIX):
    S = proj.shape[0]
    T = _sgu_t(S)
    gw = MIX // N_GROUPS_C

    def body(u_ref, v_ref, g_ref, lng_ref, lnb_ref, ws_ref, bst_ref, y_ref):
        v = v_ref[...].astype(F32)
        mu = jnp.mean(v, axis=-1, keepdims=True)
        xc = v - mu
        rstd = lax.rsqrt(jnp.mean(xc * xc, axis=-1, keepdims=True) + EPS)
        for g in range(N_GROUPS_C):
            cols = slice(g * gw, (g + 1) * gw)
            wsm = _ws_masked(ws_ref, g)[0].astype(BF16)
            vn = (xc[:, cols] * rstd * lng_ref[:, cols] + lnb_ref[:, cols]).astype(BF16)
            for blk in range(T // GMLP_CHUNK):
                rows = slice(blk * GMLP_CHUNK, (blk + 1) * GMLP_CHUNK)
                sg = _dot(wsm, vn[rows], NN) + bst_ref[:, g:g + 1]
                gate = g_ref[rows, cols].astype(F32)
                y = u_ref[rows, cols].astype(F32) * sg * (gate * _sigmoid(gate))
                y_ref[rows, cols] = y.astype(y_ref.dtype)

    def part(cidx):
        return pl.BlockSpec((T, MIX), lambda i: (i, cidx))

    vec = pl.BlockSpec((1, MIX), lambda i: (0, 0))
    return pl.pallas_call(
        body, grid=(S // T,),
        in_specs=[part(0), part(1), part(2), vec, vec,
                  pl.BlockSpec((N_GROUPS_C, GMLP_CHUNK, GMLP_CHUNK), lambda i: (0, 0, 0)),
                  pl.BlockSpec((GMLP_CHUNK, N_GROUPS_C), lambda i: (0, 0))],
        out_specs=pl.BlockSpec((T, MIX), lambda i: (i, 0)),
        out_shape=jax.ShapeDtypeStruct((S, MIX), BF16),
        name="sgu_fwd", compiler_params=_cparams("parallel"))(proj, proj, proj, lng, lnb, ws, bst)


def sgu_bwd(dy1, proj, lng, lnb, ws, bst, MIX):
    S = proj.shape[0]
    T = _sgu_t(S)
    gw = MIX // N_GROUPS_C

    def body(dy_ref, u_ref, v_ref, g_ref, lng_ref, lnb_ref, ws_ref, bst_ref,
             dp_ref, dws_ref, dbst_ref, dlng_ref, dlnb_ref, dvn_buf):
        i = pl.program_id(0)

        @pl.when(i == 0)
        def _():
            dws_ref[...] = jnp.zeros_like(dws_ref)
            dbst_ref[...] = jnp.zeros_like(dbst_ref)
            dlng_ref[...] = jnp.zeros_like(dlng_ref)
            dlnb_ref[...] = jnp.zeros_like(dlnb_ref)

        v = v_ref[...].astype(F32)
        mu = jnp.mean(v, axis=-1, keepdims=True)
        xc = v - mu
        rstd = lax.rsqrt(jnp.mean(xc * xc, axis=-1, keepdims=True) + EPS)
        for g in range(N_GROUPS_C):
            cols = slice(g * gw, (g + 1) * gw)
            wsf, keep = _ws_masked(ws_ref, g)
            wsm = wsf.astype(BF16)
            vn = (xc[:, cols] * rstd * lng_ref[:, cols] + lnb_ref[:, cols]).astype(BF16)
            for blk in range(T // GMLP_CHUNK):
                rows = slice(blk * GMLP_CHUNK, (blk + 1) * GMLP_CHUNK)
                vnb = vn[rows]
                sg = _dot(wsm, vnb, NN) + bst_ref[:, g:g + 1]
                gate = g_ref[rows, cols].astype(F32)
                sig = _sigmoid(gate)
                sil = gate * sig
                u = u_ref[rows, cols].astype(F32)
                dy = dy_ref[rows, cols].astype(F32)
                dp_ref[rows, g * gw:(g + 1) * gw] = (dy * sg * sil).astype(dp_ref.dtype)
                dp_ref[rows, 2 * MIX + g * gw:2 * MIX + (g + 1) * gw] = (
                    dy * u * sg * (sig * (1.0 + gate * (1.0 - sig)))).astype(dp_ref.dtype)
                dsg = dy * u * sil
                dsgb = dsg.astype(BF16)
                dvn_buf[rows, cols] = _dot(wsm, dsgb, TN)
                dws_ref[g] += jnp.where(keep, _dot(dsgb, vnb, NT), 0.0)
                dbst_ref[:, g:g + 1] += jnp.sum(dsg, axis=-1, keepdims=True)
        dvn = dvn_buf[...]
        xhat = xc * rstd
        dxhat = dvn * lng_ref[...]
        dv = rstd * (dxhat - jnp.mean(dxhat, axis=-1, keepdims=True)
                     - xhat * jnp.mean(dxhat * xhat, axis=-1, keepdims=True))
        dp_ref[:, MIX:2 * MIX] = dv.astype(dp_ref.dtype)
        dlng_ref[...] += jnp.sum(dvn * xhat, axis=0, keepdims=True)
        dlnb_ref[...] += jnp.sum(dvn, axis=0, keepdims=True)

    def part(cidx):
        return pl.BlockSpec((T, MIX), lambda i: (i, cidx))

    vec = pl.BlockSpec((1, MIX), lambda i: (0, 0))
    wspec = pl.BlockSpec((N_GROUPS_C, GMLP_CHUNK, GMLP_CHUNK), lambda i: (0, 0, 0))
    bspec = pl.BlockSpec((GMLP_CHUNK, N_GROUPS_C), lambda i: (0, 0))
    return pl.pallas_call(
        body, grid=(S // T,),
        in_specs=[pl.BlockSpec((T, MIX), lambda i: (i, 0)), part(0), part(1), part(2), vec, vec, wspec, bspec],
        out_specs=[pl.BlockSpec((T, 3 * MIX), lambda i: (i, 0)), wspec, bspec, vec, vec],
        out_shape=[jax.ShapeDtypeStruct((S, 3 * MIX), BF16),
                   jax.ShapeDtypeStruct((N_GROUPS_C, GMLP_CHUNK, GMLP_CHUNK), F32),
                   jax.ShapeDtypeStruct((GMLP_CHUNK, N_GROUPS_C), F32),
                   jax.ShapeDtypeStruct((1, MIX), F32), jax.ShapeDtypeStruct((1, MIX), F32)],
        scratch_shapes=[pltpu.VMEM((T, MIX), F32)],
        name="sgu_bwd", compiler_params=_cparams("arbitrary"))(dy1, proj, proj, proj, lng, lnb, ws, bst)


def xattn_fwd(name, q, k, v):
    S, D = q.shape
    nm = k.shape[0]
    dh = D // N_HEADS_X
    tq = _pick(S, (512, 256))
    scale = dh ** -0.5

    def body(q_ref, k_ref, v_ref, o_ref, lse_ref):
        s = _dot(q_ref[...], k_ref[...], NT) * scale
        m = jnp.max(s, axis=-1, keepdims=True)
        p = jnp.exp(s - m)
        l = jnp.sum(p, axis=-1, keepdims=True)
        o_ref[...] = (_dot(p.astype(BF16), v_ref[...], NN) / l).astype(o_ref.dtype)
        lse_ref[...] = m + jnp.log(l)

    return pl.pallas_call(
        body, grid=(N_HEADS_X, S // tq),
        in_specs=[pl.BlockSpec((tq, dh), lambda h, i: (i, h)),
                  pl.BlockSpec((nm, dh), lambda h, i: (0, h)), pl.BlockSpec((nm, dh), lambda h, i: (0, h))],
        out_specs=[pl.BlockSpec((tq, dh), lambda h, i: (i, h)),
                   pl.BlockSpec((None, tq, 1), lambda h, i: (h, i, 0))],
        out_shape=[jax.ShapeDtypeStruct((S, D), BF16), jax.ShapeDtypeStruct((N_HEADS_X, S, 1), F32)],
        name=name, compiler_params=_cparams("parallel", "parallel"))(q, k, v)


def xattn_bwd(name, q, k, v, o, do, lse):
    S, D = q.shape
    nm = k.shape[0]
    dh = D // N_HEADS_X
    tq = _pick(S, (512, 256))
    scale = dh ** -0.5

    def body(q_ref, k_ref, v_ref, o_ref, do_ref, lse_ref, dq_ref, dk_ref, dv_ref):
        i = pl.program_id(1)
        q_v = q_ref[...]
        k_v = k_ref[...]
        do_v = do_ref[...]
        p = jnp.exp(_dot(q_v, k_v, NT) * scale - lse_ref[...])
        delta = jnp.sum(do_v.astype(F32) * o_ref[...].astype(F32), axis=-1, keepdims=True)
        dv = _dot(p.astype(BF16), do_v, TN)
        ds = (p * (_dot(do_v, v_ref[...], NT) - delta)).astype(BF16)
        dq_ref[...] = (_dot(ds, k_v, NN) * scale).astype(dq_ref.dtype)
        dk = _dot(ds, q_v, TN) * scale

        @pl.when(i == 0)
        def _():
            dk_ref[...] = dk
            dv_ref[...] = dv

        @pl.when(i > 0)
        def _():
            dk_ref[...] += dk
            dv_ref[...] += dv

    qs = pl.BlockSpec((tq, dh), lambda h, i: (i, h))
    ks = pl.BlockSpec((nm, dh), lambda h, i: (0, h))
    return pl.pallas_call(
        body, grid=(N_HEADS_X, S // tq),
        in_specs=[qs, ks, ks, qs, qs, pl.BlockSpec((None, tq, 1), lambda h, i: (h, i, 0))],
        out_specs=[qs, ks, ks],
        out_shape=[jax.ShapeDtypeStruct((S, D), BF16), jax.ShapeDtypeStruct((nm, D), F32),
                   jax.ShapeDtypeStruct((nm, D), F32)],
        name=name, compiler_params=_cparams("parallel", "arbitrary"))(q, k, v, o, do, lse)


def adamw_rows(name, w, m, v, groups, row_off):
    L, R, C = w.shape
    assert len(groups) == L
    tr = _pick(R, tuple(t for t in (512, 256, 128, 64, 32, 16, 8) if t * C * 4 <= (1 << 20)) or (8,))
    assert row_off % tr == 0
    c1 = 1.0 - ADAM_B1 ** ADAM_STEP
    c2 = 1.0 - ADAM_B2 ** ADAM_STEP

    def body(w_ref, m_ref, v_ref, *refs):
        g_refs, (go_ref, d_ref, nm_ref, nv_ref) = refs[:L], refs[L:]
        layer = pl.program_id(0)
        gv = g_refs[0][...]
        for i in range(1, L):
            gv = jnp.where(layer == i, g_refs[i][...], gv)
        nm = ADAM_B1 * m_ref[...] + (1.0 - ADAM_B1) * gv
        nv = ADAM_B2 * v_ref[...] + (1.0 - ADAM_B2) * (gv * gv)
        go_ref[...] = gv
        d_ref[...] = -ADAM_LR * ((nm / c1) / (jnp.sqrt(nv / c2) + ADAM_EPS) + ADAM_WD * w_ref[...])
        nm_ref[...] = nm
        nv_ref[...] = nv

    blk = pl.BlockSpec((None, tr, C), lambda l, r: (l, r, 0))
    gblk = pl.BlockSpec((tr, C), lambda l, r: (row_off // tr + r, 0))
    sd = jax.ShapeDtypeStruct((L, R, C), F32)
    return pl.pallas_call(body, grid=(L, R // tr), in_specs=[blk] * 3 + [gblk] * L, out_specs=[blk] * 4,
                          out_shape=[sd] * 4, name=name,
                          compiler_params=_cparams("parallel", "parallel"))(w, m, v, *groups)


def adamw_many(name, tensors):
    n = len(tensors)
    c1 = 1.0 - ADAM_B1 ** ADAM_STEP
    c2 = 1.0 - ADAM_B2 ** ADAM_STEP

    def as2d(a):
        return a.reshape((1, -1) if a.ndim == 1 else (-1, a.shape[-1])).astype(F32)

    flat = [as2d(a) for t in tensors for a in t]

    def body(*refs):
        ins, outs = refs[:4 * n], refs[4 * n:]
        for t in range(n):
            w_ref, g_ref, m_ref, v_ref = ins[4 * t:4 * t + 4]
            d_ref, nm_ref, nv_ref = outs[3 * t:3 * t + 3]
            gv = g_ref[...]
            nm = ADAM_B1 * m_ref[...] + (1.0 - ADAM_B1) * gv
            nv = ADAM_B2 * v_ref[...] + (1.0 - ADAM_B2) * (gv * gv)
            d_ref[...] = -ADAM_LR * ((nm / c1) / (jnp.sqrt(nv / c2) + ADAM_EPS) + ADAM_WD * w_ref[...])
            nm_ref[...] = nm
            nv_ref[...] = nv

    vm = pl.BlockSpec(memory_space=pltpu.VMEM)
    shapes = [jax.ShapeDtypeStruct(flat[4 * t].shape, F32) for t in range(n) for _ in range(3)]
    res = pl.pallas_call(body, in_specs=[vm] * (4 * n), out_specs=[vm] * (3 * n), out_shape=shapes, name=name,
                         compiler_params=pltpu.CompilerParams(vmem_limit_bytes=V7X_VMEM_LIMIT))(*flat)
    return [tuple(r.reshape(tensors[t][0].shape) for r in res[3 * t:3 * t + 3]) for t in range(n)]


def add_halves(name, g4, recv, cidx):
    _, R, C = g4.shape
    rh = R // 2
    tr = _pick(rh, (256, 128, 64, 32, 16))
    nrb = rh // tr

    def body(c_ref, a_ref, b_ref, o_ref):
        o_ref[...] = (a_ref[...].astype(F32) + b_ref[...].astype(F32)).astype(o_ref.dtype)

    grid_spec = pltpu.PrefetchScalarGridSpec(
        num_scalar_prefetch=1, grid=(4, nrb),
        in_specs=[pl.BlockSpec((None, tr, C), lambda j, r, c_ref: (j, c_ref[0] * nrb + r, 0)),
                  pl.BlockSpec((None, tr, C), lambda j, r, c_ref: (j, r, 0))],
        out_specs=pl.BlockSpec((None, tr, C), lambda j, r, c_ref: (j, r, 0)))
    return pl.pallas_call(body, grid_spec=grid_spec, out_shape=jax.ShapeDtypeStruct((4, rh, C), BF16),
                          name=name, compiler_params=_cparams("parallel", "parallel"))(cidx, g4, recv)


def sum_chips(name, own, recv, place):
    _, rh, C = own.shape
    tr = _pick(rh, (256, 128, 64, 32, 16))
    nrb = rh // tr

    def body(s_ref, own_ref, recv_ref, o_ref):
        acc = own_ref[...].astype(F32)
        for k in range(N_CHIPS - 1):
            acc = acc + recv_ref[k].astype(F32)
        o_ref[...] = acc

    grid_spec = pltpu.PrefetchScalarGridSpec(
        num_scalar_prefetch=1, grid=(nrb,),
        in_specs=[pl.BlockSpec((None, tr, C), lambda r, s: (s[0], r, 0)),
                  pl.BlockSpec((N_CHIPS - 1, tr, C), lambda r, s: (0, r, 0))],
        out_specs=pl.BlockSpec((tr, C), lambda r, s: (s[1] * nrb + r, 0)))
    return pl.pallas_call(body, grid_spec=grid_spec, out_shape=jax.ShapeDtypeStruct((2 * rh, C), F32),
                          name=name, compiler_params=_cparams("parallel"))(place, own, recv)


def cast_into_slot(name, w, place):
    R, C = w.shape
    tr = _pick(R, (256, 128, 64, 32, 16))

    def body(s_ref, w_ref, o_ref):
        o_ref[...] = w_ref[...].astype(o_ref.dtype)

    grid_spec = pltpu.PrefetchScalarGridSpec(
        num_scalar_prefetch=1, grid=(R // tr,),
        in_specs=[pl.BlockSpec((tr, C), lambda r, s: (r, 0))],
        out_specs=pl.BlockSpec((None, tr, C), lambda r, s: (s[0], r, 0)))
    return pl.pallas_call(body, grid_spec=grid_spec, out_shape=jax.ShapeDtypeStruct((N_CHIPS, R, C), BF16),
                          name=name, compiler_params=_cparams("parallel"))(place, w)


def _place():
    return lax.axis_index("x"), lax.axis_index("y"), lax.axis_index("c")


_CHIP_FLIPS = ((1, 0), (0, 1), (1, 1))


def _flip(v, bit):
    return 1 - v if bit else v


HBM_SPEC = pl.BlockSpec(memory_space=pl.ANY)


def exchange_small(name, buf, reduce):
    R = buf.shape[0]

    def body(x_ref, *refs):
        if reduce:
            sum_ref, all_ref, send_sems, recv_sems, local_sem = refs
        else:
            all_ref, send_sems, recv_sems, local_sem = refs
        x, y, c = _place()
        me = 4 * x + 2 * y + c
        mine = pltpu.make_async_copy(x_ref, all_ref.at[me], local_sem)
        mine.start()
        sends = []
        for k in range(1, N_DEV):
            peer = (_flip(x, k & 4), _flip(y, k & 2), _flip(c, k & 1))
            cp = pltpu.make_async_remote_copy(src_ref=x_ref, dst_ref=all_ref.at[me], send_sem=send_sems.at[k - 1],
                                              recv_sem=recv_sems.at[k - 1], device_id=peer, device_id_type=MESH)
            cp.start()
            sends.append(cp)
        for k in range(1, N_DEV):
            peer = (_flip(x, k & 4), _flip(y, k & 2), _flip(c, k & 1))
            src = 4 * peer[0] + 2 * peer[1] + peer[2]
            pltpu.make_async_remote_copy(src_ref=x_ref, dst_ref=all_ref.at[src], send_sem=send_sems.at[k - 1],
                                         recv_sem=recv_sems.at[k - 1], device_id=peer,
                                         device_id_type=MESH).wait_recv()
        for cp in sends:
            cp.wait_send()
        mine.wait()
        if reduce:
            acc = all_ref[0]
            for d in range(1, N_DEV):
                acc = acc + all_ref[d]
            sum_ref[...] = acc

    vm = pl.BlockSpec(memory_space=pltpu.VMEM)
    sems = [pltpu.SemaphoreType.DMA((N_DEV - 1,)), pltpu.SemaphoreType.DMA((N_DEV - 1,)), pltpu.SemaphoreType.DMA]
    if reduce:
        return pl.pallas_call(
            body, in_specs=[vm], out_specs=vm, out_shape=jax.ShapeDtypeStruct((R, LANES), F32),
            scratch_shapes=[pltpu.VMEM((N_DEV, R, LANES), F32)] + sems, name=name,
            compiler_params=pltpu.CompilerParams(vmem_limit_bytes=V7X_VMEM_LIMIT))(buf)
    return pl.pallas_call(
        body, in_specs=[vm], out_specs=vm, out_shape=jax.ShapeDtypeStruct((N_DEV, R, LANES), F32),
        scratch_shapes=sems, name=name,
        compiler_params=pltpu.CompilerParams(vmem_limit_bytes=V7X_VMEM_LIMIT))(buf)


def exchange_job(buf):
    R = buf.shape[0]

    def copies(x_ref, all_ref, send_sems, recv_sems):
        x, y, c = _place()
        me = 4 * x + 2 * y + c
        sends, arrivals = [], []
        for k in range(1, N_DEV):
            peer = (_flip(x, k & 4), _flip(y, k & 2), _flip(c, k & 1))
            src = 4 * peer[0] + 2 * peer[1] + peer[2]
            sends.append(pltpu.make_async_remote_copy(
                src_ref=x_ref, dst_ref=all_ref.at[me], send_sem=send_sems.at[k - 1], recv_sem=recv_sems.at[k - 1],
                device_id=peer, device_id_type=MESH))
            arrivals.append(pltpu.make_async_remote_copy(
                src_ref=x_ref, dst_ref=all_ref.at[src], send_sem=send_sems.at[k - 1], recv_sem=recv_sems.at[k - 1],
                device_id=peer, device_id_type=MESH))
        return sends, arrivals

    def start(ins, outs, sems):
        for cp in copies(ins[0], outs[0], *sems)[0]:
            cp.start()

    def finish(ins, outs, sems):
        sends, arrivals = copies(ins[0], outs[0], *sems)
        for cp in arrivals:
            cp.wait_recv()
        for cp in sends:
            cp.wait_send()

    return _Comm([buf], [jax.ShapeDtypeStruct((N_DEV, R, LANES), F32)], {},
                 [pltpu.SemaphoreType.DMA((N_DEV - 1,)), pltpu.SemaphoreType.DMA((N_DEV - 1,))], start, finish)


def sum_devices(name, slots):
    _, R, _ = slots.shape
    tr = _pick(R, (512, 256, 128, 64, 32, 16, 8))

    def body(s_ref, o_ref):
        acc = s_ref[0]
        for d in range(1, N_DEV):
            acc = acc + s_ref[d]
        o_ref[...] = acc

    return pl.pallas_call(body, grid=(R // tr,),
                          in_specs=[pl.BlockSpec((N_DEV, tr, LANES), lambda r: (0, r, 0))],
                          out_specs=pl.BlockSpec((tr, LANES), lambda r: (r, 0)),
                          out_shape=jax.ShapeDtypeStruct((R, LANES), F32), name=name,
                          compiler_params=_cparams("parallel"))(slots)


def gather_job(slots, relay_frac=0.75, flips=None):
    n = len(slots)
    flips = flips or [tuple(range(len(_CHIP_FLIPS)))] * n

    def copies(o_refs, send_sems, recv_sems):
        x, y, c = _place()
        me = 2 * x + y
        sib = (x, y, 1 - c)
        chips = [(_flip(x, fx), _flip(y, fy)) for fx, fy in _CHIP_FLIPS]
        ici, fwd, from_sib = [], [], []
        for t in range(n):
            rh = o_refs[t].shape[1] // 2
            mine, theirs = pl.ds(c * rh, rh), pl.ds((1 - c) * rh, rh)
            for k, (px, py) in enumerate(chips):
                if k not in flips[t]:
                    continue
                own = o_refs[t].at[me, mine]
                ici.append(pltpu.make_async_remote_copy(
                    src_ref=own, dst_ref=own, send_sem=send_sems.at[t, k], recv_sem=recv_sems.at[t, k],
                    device_id=(px, py, c), device_id_type=MESH))
                landed = o_refs[t].at[2 * px + py, mine]
                arrival = pltpu.make_async_remote_copy(
                    src_ref=landed, dst_ref=landed, send_sem=send_sems.at[t, k], recv_sem=recv_sems.at[t, k],
                    device_id=(px, py, c), device_id_type=MESH)
                fwd.append((arrival, pltpu.make_async_remote_copy(
                    src_ref=landed, dst_ref=landed, send_sem=send_sems.at[t, 3 + k],
                    recv_sem=recv_sems.at[t, 3 + k], device_id=sib, device_id_type=MESH)))
                passed = o_refs[t].at[2 * px + py, theirs]
                from_sib.append(pltpu.make_async_remote_copy(
                    src_ref=passed, dst_ref=passed, send_sem=send_sems.at[t, 3 + k],
                    recv_sem=recv_sems.at[t, 3 + k], device_id=sib, device_id_type=MESH))
        return ici, fwd, from_sib

    def start(ins, o_refs, sems):
        for cp in copies(o_refs, *sems)[0]:
            cp.start()

    def relay(ins, o_refs, sems):
        for arrival, forward in copies(o_refs, *sems)[1]:
            arrival.wait_recv()
            forward.start()

    def finish(ins, o_refs, sems):
        ici, fwd, from_sib = copies(o_refs, *sems)
        for cp in from_sib:
            cp.wait_recv()
        for cp in ici:
            cp.wait_send()
        for _, forward in fwd:
            forward.wait_send()

    return _Comm(slots, [jax.ShapeDtypeStruct(s.shape, s.dtype) for s in slots], {t: t for t in range(n)},
                 [pltpu.SemaphoreType.DMA((n, 6)), pltpu.SemaphoreType.DMA((n, 6))], start, finish, relay,
                 relay_frac)


def sibling_halves_job(grads):
    n = len(grads)

    def copies(g_refs, o_refs, send_sems, recv_sems):
        x, y, c = _place()
        out = []
        for t in range(n):
            rh = g_refs[t].shape[1] // 2
            out.append(pltpu.make_async_remote_copy(
                src_ref=g_refs[t].at[:, pl.ds((1 - c) * rh, rh), :], dst_ref=o_refs[t],
                send_sem=send_sems.at[t], recv_sem=recv_sems.at[t], device_id=(x, y, 1 - c),
                device_id_type=MESH))
        return out

    def start(g_refs, o_refs, sems):
        for cp in copies(g_refs, o_refs, *sems):
            cp.start()

    def finish(g_refs, o_refs, sems):
        cps = copies(g_refs, o_refs, *sems)
        for cp in cps:
            cp.wait_recv()
        for cp in cps:
            cp.wait_send()

    return _Comm(grads, [jax.ShapeDtypeStruct((4, g.shape[1] // 2, g.shape[2]), g.dtype) for g in grads], {},
                 [pltpu.SemaphoreType.DMA((n,)), pltpu.SemaphoreType.DMA((n,))], start, finish)


def scatter_job(parts):
    n = len(parts)

    def copies(p_refs, o_refs, send_sems, recv_sems):
        x, y, c = _place()
        out = []
        for t in range(n):
            for k, (fx, fy) in enumerate(_CHIP_FLIPS):
                px, py = _flip(x, fx), _flip(y, fy)
                out.append(pltpu.make_async_remote_copy(
                    src_ref=p_refs[t].at[2 * px + py], dst_ref=o_refs[t].at[k],
                    send_sem=send_sems.at[t, k], recv_sem=recv_sems.at[t, k],
                    device_id=(px, py, c), device_id_type=MESH))
        return out

    def start(p_refs, o_refs, sems):
        for cp in copies(p_refs, o_refs, *sems):
            cp.start()

    def finish(p_refs, o_refs, sems):
        cps = copies(p_refs, o_refs, *sems)
        for cp in cps:
            cp.wait_recv()
        for cp in cps:
            cp.wait_send()

    return _Comm(parts, [jax.ShapeDtypeStruct((N_CHIPS - 1,) + p.shape[1:], p.dtype) for p in parts], {},
                 [pltpu.SemaphoreType.DMA((n, 3)), pltpu.SemaphoreType.DMA((n, 3))], start, finish)


def share_halves_job(halves):
    n = len(halves)

    def copies(o_refs, send_sems, recv_sems):
        x, y, c = _place()
        sends, arrivals = [], []
        for t in range(n):
            rh = o_refs[t].shape[0] // 2
            mine = o_refs[t].at[pl.ds(c * rh, rh)]
            theirs = o_refs[t].at[pl.ds((1 - c) * rh, rh)]
            sends.append(pltpu.make_async_remote_copy(
                src_ref=mine, dst_ref=mine, send_sem=send_sems.at[t], recv_sem=recv_sems.at[t],
                device_id=(x, y, 1 - c), device_id_type=MESH))
            arrivals.append(pltpu.make_async_remote_copy(
                src_ref=theirs, dst_ref=theirs, send_sem=send_sems.at[t], recv_sem=recv_sems.at[t],
                device_id=(x, y, 1 - c), device_id_type=MESH))
        return sends, arrivals

    def start(ins, o_refs, sems):
        for cp in copies(o_refs, *sems)[0]:
            cp.start()

    def finish(ins, o_refs, sems):
        sends, arrivals = copies(o_refs, *sems)
        for cp in arrivals:
            cp.wait_recv()
        for cp in sends:
            cp.wait_send()

    return _Comm(halves, [jax.ShapeDtypeStruct(h.shape, h.dtype) for h in halves], {t: t for t in range(n)},
                 [pltpu.SemaphoreType.DMA((n,)), pltpu.SemaphoreType.DMA((n,))], start, finish)


def _pack(arrs, row_multiple=SUBLANES):
    flat, total = [], 0
    for a in arrs:
        v = a.reshape(-1).astype(F32)
        pad = (-v.shape[0]) % (SUBLANES * LANES)
        flat.append(jnp.pad(v, (0, pad)))
        total += v.shape[0] + pad
    tail = (-total) % (row_multiple * LANES)
    if tail:
        flat.append(jnp.zeros((tail,), F32))
    return jnp.concatenate(flat).reshape(-1, LANES)


def _unpack(buf, shapes):
    out, off = [], 0
    flat = buf.reshape(-1)
    for s in shapes:
        n = int(np.prod(s))
        out.append(flat[off:off + n].reshape(s))
        off += n + ((-n) % (8 * LANES))
    return out


def _xattn_layer_fwd(tag, h, mem, gx, gmem, w):
    hx = rms_fwd(f"rms_x{tag}", h, gx)
    memn = rms_fwd(f"rms_mem{tag}", mem, gmem)
    q = mm_nn(f"xq{tag}", hx, w["q"], BF16)
    k = mm_nn(f"xk{tag}", memn, w["k"], BF16)
    v = mm_nn(f"xv{tag}", memn, w["v"], BF16)
    o, lse = xattn_fwd(f"xattn_fwd{tag}", q, k, v)
    h_out = mm_nn(f"xo{tag}", o, w["o"], F32, res=h)
    return h_out, dict(hx=hx, memn=memn, q=q, k=k, v=v, o=o, lse=lse)


def _xattn_layer_bwd(tag, dh_out, dh_out_b, h_in, mem, gx, gmem, w, sv):
    do = mm_nt(f"d_xo{tag}", dh_out_b, w["o"], BF16)
    dwo = mm_tn(f"dw_xo{tag}", sv["o"], dh_out_b)
    dq, dk, dv = xattn_bwd(f"xattn_bwd{tag}", sv["q"], sv["k"], sv["v"], sv["o"], do, sv["lse"])
    dwq = mm_tn(f"dw_xq{tag}", sv["hx"], dq)
    dhx = mm_nt(f"d_xq{tag}", dq, w["q"], BF16)
    dwk = mm_tn(f"dw_xk{tag}", sv["memn"], dk)
    dwv = mm_tn(f"dw_xv{tag}", sv["memn"], dv)
    dmk = mm_nt(f"d_xk{tag}", dk, w["k"], F32)
    dmv = mm_nt(f"d_xv{tag}", dv, w["v"], F32)
    dh_in, dh_in_b, dgx = rms_bwd(f"rms_x_bwd{tag}", h_in, gx, [dhx], dh_out)
    _, _, dgmem = rms_bwd(f"rms_mem_bwd{tag}", mem, gmem, [dmk, dmv], None)
    return dh_in, dh_in_b, dgx, dgmem, dict(q=dwq, k=dwk, v=dwv, o=dwo)


def kernel(x, mem, norm_mix_g, norm_x_g, norm_mem_g, final_norm_g, w_in_ab, rel_bias, conv_w, conv_b, conv_ln_g, conv_ln_b, w_out_ab, w_in_c, sgu_ln_g, sgu_ln_b, w_s, b_s, w_out_c, w_xq, w_xk, w_xv, w_xo, loss_target, m_norm_mix_g, m_norm_x_g, m_norm_mem_g, m_final_norm_g, m_w_in_ab, m_rel_bias, m_conv_w, m_conv_b, m_conv_ln_g, m_conv_ln_b, m_w_out_ab, m_w_in_c, m_sgu_ln_g, m_sgu_ln_b, m_w_s, m_b_s, m_w_out_c, m_w_xq, m_w_xk, m_w_xv, m_w_xo, v_norm_mix_g, v_norm_x_g, v_norm_mem_g, v_final_norm_g, v_w_in_ab, v_rel_bias, v_conv_w, v_conv_b, v_conv_ln_g, v_conv_ln_b, v_w_out_ab, v_w_in_c, v_sgu_ln_g, v_sgu_ln_b, v_w_s, v_b_s, v_w_out_c, v_w_xq, v_w_xk, v_w_xv, v_w_xo):
    S, D = x.shape[1], x.shape[2]
    MIX = 2 * D
    xs, mems, tgt = x[0], mem[0], loss_target[0]
    cx, cy, cc = _place()
    chip = 2 * cx + cy
    cidx = jnp.reshape(cc, (1,)).astype(jnp.int32)
    place = jnp.stack([chip, cc]).astype(jnp.int32)

    ro, rq = MIX // 4, D // 4
    row_sharded = [("out_ab", w_out_ab[0]), ("out_c", w_out_c[0])]
    for layer in range(2):
        for nm_, w in (("q", w_xq), ("k", w_xk), ("v", w_xv), ("o", w_xo)):
            row_sharded.append((f"x{nm_}{layer}", w[layer]))
    slots = {"in_ab": cast_into_slot("cast_in_ab", w_in_ab[0], place),
             "in_c": cast_into_slot("cast_in_c", w_in_c[0], place)}
    slots.update({nm_: cast_into_slot("cast_" + nm_, w, place) for nm_, w in row_sharded})

    small_sh = [conv_w[0], sgu_ln_g[0], sgu_ln_b[0]]
    gathered = exchange_small("gather_small", _pack(small_sh), reduce=False)
    per_chip = [_unpack(gathered[2 * j], [a.shape for a in small_sh]) for j in range(N_CHIPS)]
    conv_w_full = jnp.concatenate([p[0] for p in per_chip], axis=1)
    sgu_g_full = jnp.concatenate([p[1] for p in per_chip], axis=0).reshape(1, MIX)
    sgu_b_full = jnp.concatenate([p[2] for p in per_chip], axis=0).reshape(1, MIX)
    cw_pad = jnp.pad(conv_w_full, ((0, CONV_HALO - CONV_WIDTH), (0, 0)))
    cb = conv_b.reshape(1, D)
    clg, clb = conv_ln_g.reshape(1, D), conv_ln_b.reshape(1, D)
    ws = w_s[0]
    bst = jnp.transpose(b_s[0])
    tq = _attn_tq(S)
    bm = band_bias_table(rel_bias[0], tq)

    hn0 = rms_fwd("rms_mix0", xs, norm_mix_g[0])
    near, far, every = (0, 1), (2,), (0, 1, 2)
    proj0, (wab4,) = proj_cols_own("proj_ab_own", hn0, w_in_ab[0], place,
                                   comm=gather_job([slots["in_ab"]], relay_frac=1.0, flips=[near]))
    proj0, (wab4, w_out_ab4) = proj_cols_rest(
        "proj_ab_near", hn0, wab4, proj0, place, (2, 1),
        comm=gather_job([wab4, slots["out_ab"]], relay_frac=0.85, flips=[far, every]))
    proj0, got_qk = proj_cols_rest("proj_ab_far", hn0, wab4, proj0, place, (3,),
                                   comm=gather_job([slots["xq0"], slots["xk0"]]))
    (ya, lse_a), (wc4,) = attn_fwd(proj0, bm, D, comm=gather_job([slots["in_c"]], relay_frac=0.95))
    (y0, cpre), got_b = conv_gate_fwd(
        proj0, ya, cw_pad, cb, clg, clb, D,
        comm=gather_job([slots["xv0"], slots["xo0"], slots["out_c"], slots["xq1"]]))
    h1, got_c = mm_nn("out_ab", y0, w_out_ab4.reshape(-1, D), F32, res=xs,
                      comm=gather_job([slots["xk1"], slots["xv1"], slots["xo1"]]))
    got = dict(zip(["xq0", "xk0", "xv0", "xo0", "out_c", "xq1", "xk1", "xv1", "xo1"], got_qk + got_b + got_c))
    wrow = {n: g.reshape(-1, g.shape[2]) for n, g in got.items()}
    wrow["out_ab"] = w_out_ab4.reshape(-1, D)
    wx = [{k: wrow[f"x{k}{layer}"] for k in "qkvo"} for layer in range(2)]
    h2, sx0 = _xattn_layer_fwd("0", h1, mems, norm_x_g[0], norm_mem_g[0], wx[0])
    hn1 = rms_fwd("rms_mix1", h2, norm_mix_g[1])
    proj1 = mm_nn_cols("proj_c", hn1, wc4, BF16)
    y1 = sgu_fwd(proj1, sgu_g_full, sgu_b_full, ws, bst, MIX)
    h3 = mm_nn("out_c", y1, wrow["out_c"], F32, res=h2)
    h4, sx1 = _xattn_layer_fwd("1", h3, mems, norm_x_g[1], norm_mem_g[1], wx[1])
    loss_row, dg_final, dh4, dh4b = loss_head("loss_head", h4, final_norm_g, tgt)

    dh3, dh3b, dgx1, dgmem1, dwx1 = _xattn_layer_bwd("1", dh4, dh4b, h3, mems, norm_x_g[1], norm_mem_g[1], wx[1], sx1)
    def stack_rows(dw_out, dwx):
        return jnp.concatenate([g.reshape(N_CHIPS, -1, g.shape[1]) for g in [dw_out] + [dwx[k] for k in "qkvo"]],
                               axis=1)

    dy1 = mm_nt("d_out_c", dh3b, wrow["out_c"], BF16)
    dw_out_c = mm_tn("dw_out_c", y1, dh3b)
    dproj1, dws, dbst, dsgu_g, dsgu_b = sgu_bwd(dy1, proj1, sgu_g_full, sgu_b_full, ws, bst, MIX)
    grp1 = stack_rows(dw_out_c, dwx1)
    dw_in_c, (sib1,) = mm_tn_cols("dw_in_c", hn1, dproj1, comm=sibling_halves_job([grp1]))
    part1 = add_halves("add_halves1", grp1, sib1, cidx)
    dhn1, (recv1, sib2) = mm_nt_cols("d_proj_c", dproj1, wc4, BF16,
                                     comm=_join(scatter_job([part1]), sibling_halves_job([dw_in_c])))
    part2 = add_halves("add_halves2", dw_in_c, sib2, cidx)
    dh2, dh2b, dgmix1 = rms_bwd("rms_mix1_bwd", h2, norm_mix_g[1], [dhn1], dh3)
    dh1, dh1b, dgx0, dgmem0, dwx0 = _xattn_layer_bwd("0", dh2, dh2b, h1, mems, norm_x_g[0], norm_mem_g[0], wx[0], sx0)
    dy0 = mm_nt("d_out_ab", dh1b, wrow["out_ab"], BF16)
    dw_out_ab = mm_tn("dw_out_ab", y0, dh1b)
    grp3 = stack_rows(dw_out_ab, dwx0)
    dya, dgate, dc, dclg, dclb = conv_gate_bwd_a(dy0, proj0, ya, cpre, clg, clb, D)
    (da, db, dcw, dcb), (recv2, sib3) = conv_gate_bwd_b(
        dc, proj0, cw_pad, D, comm=_join(scatter_job([part2]), sibling_halves_job([grp3])))
    part3 = add_halves("add_halves3", grp3, sib3, cidx)
    (dq, dkc, dkp, dvc, dvp, ds_sum), (recv3,) = attn_bwd(proj0, ya, dya, lse_a, bm, D, comm=scatter_job([part3]))
    drel = rel_bias_grad(ds_sum)
    dproj0 = assemble_dproj0(dq, dkc, dkp, dvc, dvp, da, db, dgate, D)
    dw_in_ab = mm_tn_cols("dw_in_ab", hn0, dproj0)
    (sib4,) = run_comm("sibling_halves4", sibling_halves_job([dw_in_ab]))
    part4 = add_halves("add_halves4", dw_in_ab, sib4, cidx)
    halves = [sum_chips(f"sum_chips{t + 1}", p, r, place)
              for t, (p, r) in enumerate(((part1, recv1), (part2, recv2), (part3, recv3)))]
    small_early = [
        jnp.concatenate([dgx0, dgx1], axis=0), jnp.concatenate([dgmem0, dgmem1], axis=0), dg_final.reshape(D),
        drel[None], dcb, dclg, dclb, dws[None], jnp.transpose(dbst)[None],
        dcw[:CONV_WIDTH][None], dsgu_g, dsgu_b]
    early = _pack(small_early, row_multiple=512)
    dhn0, (recv4, small_slots, g_r1, g_c, g_r0) = mm_nt_cols(
        "d_proj_ab", dproj0, wab4, BF16,
        comm=_join(scatter_job([part4]), exchange_job(early), share_halves_job(halves)))
    dx, _, dgmix0 = rms_bwd("rms_mix0_bwd", xs, norm_mix_g[0], [dhn0], dh1)
    (g_ab,) = run_comm("share_reduced_half4", share_halves_job([sum_chips("sum_chips4", part4, recv4, place)]))

    me = 4 * cx + 2 * cy + cc
    small_slots = lax.dynamic_update_slice(small_slots, early[None], (me, 0, 0))
    summed = _unpack(sum_devices("sum_small", small_slots), [a.shape for a in small_early])
    (g_norm_x, g_norm_mem, g_final, g_rel, g_conv_b, g_clg, g_clb, g_ws, g_bs,
     g_conv_w_full, g_sgu_g_full, g_sgu_b_full) = summed
    dgmix = jnp.concatenate([dgmix0, dgmix1], axis=0)
    (g_norm_mix,) = _unpack(exchange_small("reduce_late", _pack([dgmix]), reduce=True), [dgmix.shape])
    cws = conv_w.shape[2]
    g_conv_w = lax.dynamic_slice_in_dim(g_conv_w_full, chip * cws, cws, axis=2)
    sgs = sgu_ln_g.shape[1]
    g_sgu_g = lax.dynamic_slice_in_dim(g_sgu_g_full, chip * sgs, sgs, axis=1)
    g_sgu_b = lax.dynamic_slice_in_dim(g_sgu_b_full, chip * sgs, sgs, axis=1)

    loss = lax.psum(loss_row[0, 0], ("x", "y", "c"))

    big_grads = {"w_in_ab": ([g_ab], 0), "w_in_c": ([g_c], 0), "w_out_ab": ([g_r0], 0), "w_out_c": ([g_r1], 0)}
    for i, nm_ in enumerate("qkvo"):
        big_grads["w_x" + nm_] = ([g_r0, g_r1], ro + i * rq)
    grads = dict(
        norm_mix_g=g_norm_mix, norm_x_g=g_norm_x, norm_mem_g=g_norm_mem, final_norm_g=g_final,
        rel_bias=g_rel, conv_w=g_conv_w, conv_b=g_conv_b, conv_ln_g=g_clg, conv_ln_b=g_clb,
        sgu_ln_g=g_sgu_g, sgu_ln_b=g_sgu_b, w_s=g_ws, b_s=g_bs)
    weights = dict(
        norm_mix_g=(norm_mix_g, m_norm_mix_g, v_norm_mix_g), norm_x_g=(norm_x_g, m_norm_x_g, v_norm_x_g),
        norm_mem_g=(norm_mem_g, m_norm_mem_g, v_norm_mem_g), final_norm_g=(final_norm_g, m_final_norm_g, v_final_norm_g),
        w_in_ab=(w_in_ab, m_w_in_ab, v_w_in_ab), rel_bias=(rel_bias, m_rel_bias, v_rel_bias),
        conv_w=(conv_w, m_conv_w, v_conv_w), conv_b=(conv_b, m_conv_b, v_conv_b),
        conv_ln_g=(conv_ln_g, m_conv_ln_g, v_conv_ln_g), conv_ln_b=(conv_ln_b, m_conv_ln_b, v_conv_ln_b),
        w_out_ab=(w_out_ab, m_w_out_ab, v_w_out_ab), w_in_c=(w_in_c, m_w_in_c, v_w_in_c),
        sgu_ln_g=(sgu_ln_g, m_sgu_ln_g, v_sgu_ln_g), sgu_ln_b=(sgu_ln_b, m_sgu_ln_b, v_sgu_ln_b),
        w_s=(w_s, m_w_s, v_w_s), b_s=(b_s, m_b_s, v_b_s), w_out_c=(w_out_c, m_w_out_c, v_w_out_c),
        w_xq=(w_xq, m_w_xq, v_w_xq), w_xk=(w_xk, m_w_xk, v_w_xk), w_xv=(w_xv, m_w_xv, v_w_xv),
        w_xo=(w_xo, m_w_xo, v_w_xo))
    names = list(weights)
    delta, new_m, new_v = {}, {}, {}
    for nm_, (groups, row_off) in big_grads.items():
        w, m, v = weights[nm_]
        grads[nm_], delta[nm_], new_m[nm_], new_v[nm_] = adamw_rows("adamw_" + nm_, w, m, v, groups, row_off)
    small_names = [n for n in names if n not in big_grads]
    stepped = adamw_many("adamw_small", [(weights[n][0], grads[n].reshape(weights[n][0].shape), weights[n][1],
                                          weights[n][2]) for n in small_names])
    for n, (d_, m_, v_) in zip(small_names, stepped):
        delta[n], new_m[n], new_v[n] = d_, m_, v_

    return (loss, dx[None], *[grads[n].reshape(weights[n][0].shape) for n in names], *[delta[n] for n in names],
            *[new_m[n] for n in names], *[new_v[n] for n in names])
```

```python
import functools

import numpy as np
import jax
import jax.numpy as jnp
from jax import lax
from jax.experimental import pallas as pl
from jax.experimental.pallas import tpu as pltpu

F32 = jnp.float32
BF16 = jnp.bfloat16
MESH = pl.DeviceIdType.MESH

EPS = 1e-6
CHUNK = 64
N_PAST_CHUNKS = 8
MAX_REL = 128
HEAD_DIM_A = 128
CONV_WIDTH = 31
CONV_HALO = 32
GMLP_CHUNK = 128
N_GROUPS_C = 8
N_HEADS_X = 4
NEG = -1e30

ADAM_LR = 0.001
ADAM_B1 = 0.9
ADAM_B2 = 0.999
ADAM_EPS = 1e-08
ADAM_WD = 0.01
ADAM_STEP = 10

N_CHIPS = 4
N_DEV = 8
V7X_VMEM_LIMIT = 56 * 1024 * 1024
LANES = 128
SUBLANES = 8


def _pick(n, cands):
    for c in cands:
        if c <= n and n % c == 0:
            return c
    return n


def _cparams(*sem):
    return pltpu.CompilerParams(dimension_semantics=sem, vmem_limit_bytes=V7X_VMEM_LIMIT)


def _sigmoid(x):
    return 0.5 * jnp.tanh(0.5 * x) + 0.5


def _dot(a, b, contract):
    return lax.dot_general(a, b, (contract, ((), ())), preferred_element_type=F32)


NN = ((1,), (0,))
NT = ((1,), (1,))
TN = ((0,), (0,))


class _Comm:
    def __init__(self, arrays, out_shapes, aliases, sems, start, finish, relay=None, relay_frac=0.75):
        self.arrays, self.out_shapes, self.aliases, self.sems = list(arrays), list(out_shapes), dict(aliases), list(sems)
        self.start, self.finish, self.relay = start, finish, relay
        self.relay_frac = relay_frac


def _join(*jobs):
    assert all(j.relay is None for j in jobs)
    arrays, outs, sems, aliases, spans = [], [], [], {}, []
    for j in jobs:
        spans.append((len(arrays), len(outs), len(sems)))
        aliases.update({len(arrays) + i: len(outs) + o for i, o in j.aliases.items()})
        arrays += j.arrays
        outs += j.out_shapes
        sems += j.sems

    def part(j, span, ins, os_, ss):
        a0, o0, s0 = span
        return (ins[a0:a0 + len(j.arrays)], os_[o0:o0 + len(j.out_shapes)], ss[s0:s0 + len(j.sems)])

    def start(ins, os_, ss):
        for j, span in zip(jobs, spans):
            j.start(*part(j, span, ins, os_, ss))

    def finish(ins, os_, ss):
        for j, span in zip(jobs, spans):
            j.finish(*part(j, span, ins, os_, ss))

    return _Comm(arrays, outs, aliases, sems, start, finish)


def _call(body, *, name, grid, in_specs, out_specs, out_shape, args, scratch_shapes=(), sem=None, comm=None,
          prefetch=None, io_aliases=None):
    multi = isinstance(out_shape, (list, tuple))
    o_shapes = list(out_shape) if multi else [out_shape]
    o_specs = list(out_specs) if multi else [out_specs]
    if comm is None:
        assert prefetch is None and io_aliases is None
        return pl.pallas_call(body, grid=grid, in_specs=in_specs, out_specs=out_specs, out_shape=out_shape,
                              scratch_shapes=list(scratch_shapes), name=name,
                              compiler_params=_cparams(*sem))(*args)
    n_in, n_out, n_scr = len(in_specs), len(o_shapes), len(scratch_shapes)
    n_ci, n_co = len(comm.arrays), len(comm.out_shapes)
    n_steps = int(np.prod(grid))
    n_pre = 0 if prefetch is None else 1

    def carrier(*refs):
        pre, refs = refs[:n_pre], refs[n_pre:]
        ins, rest = refs[:n_in], refs[n_in:]
        cins, rest = rest[:n_ci], rest[n_ci:]
        outs, rest = rest[:n_out], rest[n_out:]
        couts, rest = rest[:n_co], rest[n_co:]
        scr, csems = rest[:n_scr], rest[n_scr:]
        step = 0
        for a, g in enumerate(grid):
            step = step * g + pl.program_id(a)

        @pl.when(step == 0)
        def _():
            comm.start(cins, couts, csems)

        body(*pre, *ins, *outs, *scr)

        relay_step = min(int(comm.relay_frac * n_steps), n_steps - 1)
        if comm.relay is not None and relay_step < n_steps - 1:
            @pl.when(step == relay_step)
            def _():
                comm.relay(cins, couts, csems)

        @pl.when(step == n_steps - 1)
        def _():
            if comm.relay is not None and relay_step == n_steps - 1:
                comm.relay(cins, couts, csems)
            comm.finish(cins, couts, csems)

    aliases = {n_pre + n_in + i: n_out + o for i, o in comm.aliases.items()}
    aliases.update({n_pre + i: o for i, o in (io_aliases or {}).items()})
    all_in = list(in_specs) + [HBM_SPEC] * n_ci
    all_out = o_specs + [HBM_SPEC] * n_co
    all_scratch = list(scratch_shapes) + comm.sems
    params = _cparams(*(["arbitrary"] * len(grid)))
    if prefetch is None:
        res = pl.pallas_call(
            carrier, grid=grid, in_specs=all_in, out_specs=all_out, out_shape=o_shapes + comm.out_shapes,
            input_output_aliases=aliases, scratch_shapes=all_scratch, name=name,
            compiler_params=params)(*args, *comm.arrays)
    else:
        grid_spec = pltpu.PrefetchScalarGridSpec(num_scalar_prefetch=1, grid=grid, in_specs=all_in,
                                                 out_specs=all_out, scratch_shapes=all_scratch)
        res = pl.pallas_call(
            carrier, grid_spec=grid_spec, out_shape=o_shapes + comm.out_shapes, input_output_aliases=aliases,
            name=name, compiler_params=params)(prefetch, *args, *comm.arrays)
    mine = list(res[:n_out]) if multi else res[0]
    return mine, list(res[n_out:])


def run_comm(name, comm):
    def body(*refs):
        n_ci, n_co = len(comm.arrays), len(comm.out_shapes)
        cins, couts, csems = refs[:n_ci], refs[n_ci:n_ci + n_co], refs[n_ci + n_co:]
        comm.start(cins, couts, csems)
        if comm.relay is not None:
            comm.relay(cins, couts, csems)
        comm.finish(cins, couts, csems)

    return pl.pallas_call(
        body, in_specs=[HBM_SPEC] * len(comm.arrays), out_specs=[HBM_SPEC] * len(comm.out_shapes),
        out_shape=comm.out_shapes, input_output_aliases=comm.aliases, scratch_shapes=comm.sems,
        name=name)(*comm.arrays)


def _mm(name, a, b, *, contract, grid, a_spec, b_spec, o_spec, out_shape, res=None, comm=None):
    nk = grid[2]

    def body(*refs):
        if res is not None:
            a_ref, b_ref, r_ref, o_ref = refs[:4]
        else:
            a_ref, b_ref, o_ref = refs[:3]
            r_ref = None
        p = _dot(a_ref[...].astype(BF16), b_ref[...].astype(BF16), contract)

        def finish(acc):
            if r_ref is not None:
                acc = acc + r_ref[...]
            o_ref[...] = acc.astype(o_ref.dtype)

        if nk == 1:
            finish(p)
        else:
            acc_ref = refs[-1]
            k = pl.program_id(2)

            @pl.when(k == 0)
            def _():
                acc_ref[...] = p

            @pl.when(k > 0)
            def _():
                acc_ref[...] += p

            @pl.when(k == nk - 1)
            def _():
                finish(acc_ref[...])

    in_specs = [a_spec, b_spec]
    args = [a, b]
    if res is not None:
        in_specs.append(o_spec)
        args.append(res)
    blk = tuple(d for d in o_spec.block_shape if d is not None)
    scratch = [] if nk == 1 else [pltpu.VMEM(blk, F32)]
    return _call(body, name=name, grid=grid, in_specs=in_specs, out_specs=o_spec, out_shape=out_shape,
                 args=args, scratch_shapes=scratch, sem=("parallel", "parallel", "arbitrary"), comm=comm)


def mm_nn_cols(name, a, w4, out_dtype, comm=None):
    M, K = a.shape
    _, _, C = w4.shape
    tm = _pick(M, (1024, 512, 256))
    tn = _pick(C, (1024, 512, 256, 128))
    nps = C // tn
    return _mm(name, a, w4, contract=NN, grid=(M // tm, 4 * nps, 1),
               a_spec=pl.BlockSpec((tm, K), lambda i, j, k: (i, 0)),
               b_spec=pl.BlockSpec((None, K, tn), lambda i, j, k: (j // nps, 0, j % nps)),
               o_spec=pl.BlockSpec((tm, tn), lambda i, j, k: (i, j)),
               out_shape=jax.ShapeDtypeStruct((M, 4 * C), out_dtype), comm=comm)


def proj_cols_own(name, a, w_own, place, comm):
    M, K = a.shape
    C = w_own.shape[1]
    tm = _pick(M, (1024, 512, 256))
    tn = _pick(C, (512, 256, 128))
    nps = C // tn

    def body(s_ref, a_ref, b_ref, o_ref):
        o_ref[...] = _dot(a_ref[...], b_ref[...].astype(BF16), NN).astype(o_ref.dtype)

    return _call(body, name=name, grid=(M // tm, nps),
                 in_specs=[pl.BlockSpec((tm, K), lambda i, j, s: (i, 0)),
                           pl.BlockSpec((K, tn), lambda i, j, s: (0, j))],
                 out_specs=pl.BlockSpec((tm, tn), lambda i, j, s: (i, s[0] * nps + j)),
                 out_shape=jax.ShapeDtypeStruct((M, N_CHIPS * C), BF16), args=[a, w_own], comm=comm,
                 prefetch=place)


def proj_cols_rest(name, a, w4, partial, place, masks, comm):
    M, K = a.shape
    C = w4.shape[2]
    tm = _pick(M, (1024, 512, 256))
    tn = _pick(C, (1792, 1536, 1024, 512, 256, 128))
    nps = C // tn
    assert len(masks) in (1, 2)
    step = masks[-1] - masks[0]

    def slot(j, s):
        return jnp.bitwise_xor(s[0], masks[0] + step * (j // nps))

    def body(s_ref, a_ref, b_ref, part_ref, o_ref):
        o_ref[...] = _dot(a_ref[...], b_ref[...], NN).astype(o_ref.dtype)

    return _call(body, name=name, grid=(M // tm, len(masks) * nps),
                 in_specs=[pl.BlockSpec((tm, K), lambda i, j, s: (i, 0)),
                           pl.BlockSpec((None, K, tn), lambda i, j, s: (slot(j, s), 0, j % nps)),
                           HBM_SPEC],
                 out_specs=pl.BlockSpec((tm, tn), lambda i, j, s: (i, slot(j, s) * nps + j % nps)),
                 out_shape=jax.ShapeDtypeStruct(partial.shape, partial.dtype), args=[a, w4, partial],
                 comm=comm, prefetch=place, io_aliases={2: 0})


def mm_nn(name, a, w, out_dtype, res=None, comm=None):
    M, K = a.shape
    N = w.shape[1]
    tm = _pick(M, (1024, 512, 256))
    tn = _pick(N, (1024, 512, 256, 128) if K <= 2048 else (512, 256, 128))
    return _mm(name, a, w, contract=NN, grid=(M // tm, N // tn, 1),
               a_spec=pl.BlockSpec((tm, K), lambda i, j, k: (i, 0)),
               b_spec=pl.BlockSpec((K, tn), lambda i, j, k: (0, j)),
               o_spec=pl.BlockSpec((tm, tn), lambda i, j, k: (i, j)),
               out_shape=jax.ShapeDtypeStruct((M, N), out_dtype), res=res, comm=comm)


def mm_nt_cols(name, a, w4, out_dtype, comm=None):
    M = a.shape[0]
    _, K, C = w4.shape
    tm = _pick(M, (1024, 512, 256))
    tn = _pick(K, (1024, 512, 256, 128))
    tk = _pick(C, (3584, 3072, 1792, 1536, 1024, 512, 256, 128))
    kps = C // tk
    return _mm(name, a, w4, contract=NT, grid=(M // tm, K // tn, 4 * kps),
               a_spec=pl.BlockSpec((tm, tk), lambda i, j, k: (i, k)),
               b_spec=pl.BlockSpec((None, tn, tk), lambda i, j, k: (k // kps, j, k % kps)),
               o_spec=pl.BlockSpec((tm, tn), lambda i, j, k: (i, j)),
               out_shape=jax.ShapeDtypeStruct((M, K), out_dtype), comm=comm)


def mm_nt(name, a, w, out_dtype):
    M, C = a.shape
    N = w.shape[0]
    tm = _pick(M, (1024, 512, 256))
    tn = _pick(N, (1024, 512, 256, 128))
    return _mm(name, a, w, contract=NT, grid=(M // tm, N // tn, 1),
               a_spec=pl.BlockSpec((tm, C), lambda i, j, k: (i, 0)),
               b_spec=pl.BlockSpec((tn, C), lambda i, j, k: (j, 0)),
               o_spec=pl.BlockSpec((tm, tn), lambda i, j, k: (i, j)),
               out_shape=jax.ShapeDtypeStruct((M, N), out_dtype))


def mm_tn_cols(name, a, b, comm=None):
    S, K = a.shape
    C = b.shape[1] // 4
    ts = _pick(S, (2048, 1024, 512, 256))
    tko = _pick(K, (1024, 512, 256, 128))
    tn = _pick(C, (1792, 1536, 1024, 512, 256, 128))
    nps = C // tn
    return _mm(name, a, b, contract=TN, grid=(K // tko, 4 * nps, S // ts),
               a_spec=pl.BlockSpec((ts, tko), lambda i, j, k: (k, i)),
               b_spec=pl.BlockSpec((ts, tn), lambda i, j, k: (k, j)),
               o_spec=pl.BlockSpec((None, tko, tn), lambda i, j, k: (j // nps, i, j % nps)),
               out_shape=jax.ShapeDtypeStruct((4, K, C), BF16), comm=comm)


def mm_tn(name, a, b):
    S, K = a.shape
    N = b.shape[1]
    ts = _pick(S, (2048, 1024, 512, 256))
    tko = _pick(K, (1024, 512, 256, 128))
    tn = _pick(N, (1024, 512, 256, 128))
    return _mm(name, a, b, contract=TN, grid=(K // tko, N // tn, S // ts),
               a_spec=pl.BlockSpec((ts, tko), lambda i, j, k: (k, i)),
               b_spec=pl.BlockSpec((ts, tn), lambda i, j, k: (k, j)),
               o_spec=pl.BlockSpec((tko, tn), lambda i, j, k: (i, j)),
               out_shape=jax.ShapeDtypeStruct((K, N), BF16))


def rms_fwd(name, x, g):
    S, D = x.shape
    T = _pick(S, (512, 256))

    def body(x_ref, g_ref, o_ref):
        xf = x_ref[...]
        r = lax.rsqrt(jnp.mean(xf * xf, axis=-1, keepdims=True) + EPS)
        o_ref[...] = (xf * r * g_ref[...]).astype(o_ref.dtype)

    return pl.pallas_call(
        body, grid=(S // T,),
        in_specs=[pl.BlockSpec((T, D), lambda i: (i, 0)), pl.BlockSpec((1, D), lambda i: (0, 0))],
        out_specs=pl.BlockSpec((T, D), lambda i: (i, 0)),
        out_shape=jax.ShapeDtypeStruct((S, D), BF16), name=name,
        compiler_params=_cparams("parallel"))(x, g.reshape(1, D))


def rms_bwd(name, x, g, dys, dres):
    S, D = x.shape
    T = _pick(S, (256,))
    ndy = len(dys)
    has_res = dres is not None

    def body(*refs):
        x_ref, g_ref = refs[0], refs[1]
        dy_refs = refs[2:2 + ndy]
        r_ref = refs[2 + ndy] if has_res else None
        dx_ref, dxb_ref, dg_ref = refs[-3], refs[-2], refs[-1]
        i = pl.program_id(0)
        xf = x_ref[...]
        r = lax.rsqrt(jnp.mean(xf * xf, axis=-1, keepdims=True) + EPS)
        xhat = xf * r
        dy = dy_refs[0][...].astype(F32)
        for d in dy_refs[1:]:
            dy = dy + d[...].astype(F32)
        dxhat = dy * g_ref[...]
        dx = r * (dxhat - xhat * jnp.mean(dxhat * xhat, axis=-1, keepdims=True))
        if has_res:
            dx = dx + r_ref[...]
        dx_ref[...] = dx
        dxb_ref[...] = dx.astype(dxb_ref.dtype)
        dg = jnp.sum(dy * xhat, axis=0, keepdims=True)

        @pl.when(i == 0)
        def _():
            dg_ref[...] = dg

        @pl.when(i > 0)
        def _():
            dg_ref[...] += dg

    row = pl.BlockSpec((T, D), lambda i: (i, 0))
    vec = pl.BlockSpec((1, D), lambda i: (0, 0))
    args = [x, g.reshape(1, D), *dys] + ([dres] if has_res else [])
    return pl.pallas_call(
        body, grid=(S // T,),
        in_specs=[row, vec] + [row] * (ndy + int(has_res)),
        out_specs=[row, row, vec],
        out_shape=[jax.ShapeDtypeStruct((S, D), F32), jax.ShapeDtypeStruct((S, D), BF16),
                   jax.ShapeDtypeStruct((1, D), F32)],
        name=name, compiler_params=_cparams("arbitrary"))(*args)


def loss_head(name, h, g, target):
    S, D = h.shape
    T = _pick(S, (256,))

    def body(h_ref, g_ref, t_ref, loss_ref, dg_ref, dh_ref, dhb_ref):
        i = pl.program_id(0)
        xf = h_ref[...]
        gv = g_ref[...]
        r = lax.rsqrt(jnp.mean(xf * xf, axis=-1, keepdims=True) + EPS)
        xhat = xf * r
        err = xhat * gv - t_ref[...]
        part = 0.5 * jnp.sum(jnp.sum(err * err, axis=-1, keepdims=True), axis=0, keepdims=True) / D
        dout = err / D
        dxhat = dout * gv
        dh = r * (dxhat - xhat * jnp.mean(dxhat * xhat, axis=-1, keepdims=True))
        dh_ref[...] = dh
        dhb_ref[...] = dh.astype(dhb_ref.dtype)
        dg = jnp.sum(dout * xhat, axis=0, keepdims=True)
        lrow = jnp.broadcast_to(part, (1, LANES))

        @pl.when(i == 0)
        def _():
            dg_ref[...] = dg
            loss_ref[...] = lrow

        @pl.when(i > 0)
        def _():
            dg_ref[...] += dg
            loss_ref[...] += lrow

    row = pl.BlockSpec((T, D), lambda i: (i, 0))
    vec = pl.BlockSpec((1, D), lambda i: (0, 0))
    return pl.pallas_call(
        body, grid=(S // T,), in_specs=[row, vec, row],
        out_specs=[pl.BlockSpec((1, LANES), lambda i: (0, 0)), vec, row, row],
        out_shape=[jax.ShapeDtypeStruct((1, LANES), F32), jax.ShapeDtypeStruct((1, D), F32),
                   jax.ShapeDtypeStruct((S, D), F32), jax.ShapeDtypeStruct((S, D), BF16)],
        name=name, compiler_params=_cparams("arbitrary"))(h, g.reshape(1, D), target)


def _attn_tq(S):
    return _pick(S, (512,))


def band_bias_table(rel_bias, tq):
    H = rel_bias.shape[0]
    w = 2 * tq
    nbits = int(np.log2(tq))
    assert (1 << nbits) == tq and (N_PAST_CHUNKS + 2) * CHUNK - 1 <= w
    c = np.arange(w)
    d0 = np.where(c <= tq + CHUNK - 1, tq - c, tq + w - c)
    base = jnp.take(rel_bias.astype(F32), jnp.asarray(np.clip(d0, -MAX_REL, MAX_REL) + MAX_REL), axis=1)

    def body(b_ref, o_ref):
        x = jnp.broadcast_to(b_ref[...], (tq, w))
        row = lax.broadcasted_iota(jnp.int32, (tq, w), 0)
        col = lax.broadcasted_iota(jnp.int32, (tq, w), 1)
        for b in range(nbits):
            x = jnp.where(((row >> b) & 1) == 1, pltpu.roll(x, 1 << b, 1), x)
        qc = row // CHUNK
        kc = col // CHUNK - tq // CHUNK
        o_ref[...] = jnp.where((kc <= qc) & (kc >= qc - N_PAST_CHUNKS), x, NEG)

    return pl.pallas_call(
        body, grid=(H,), in_specs=[pl.BlockSpec((None, 1, w), lambda h: (h, 0, 0))],
        out_specs=pl.BlockSpec((None, tq, w), lambda h: (h, 0, 0)),
        out_shape=jax.ShapeDtypeStruct((H, tq, w), F32), name="band_bias_table",
        compiler_params=_cparams("parallel"))(base.reshape(H, 1, w))


def _attn_subblocks(tq):
    sub = tq // 2
    assert sub % CHUNK == 0 and N_PAST_CHUNKS * CHUNK == tq
    return sub, 3


def attn_fwd(proj, bm, D, comm=None):
    S = proj.shape[0]
    H = D // HEAD_DIM_A
    tq = _attn_tq(S)
    nb = S // tq
    scale = HEAD_DIM_A ** -0.5

    sub, n_sub = _attn_subblocks(tq)

    def body(q_ref, kp_ref, kc_ref, vp_ref, vc_ref, bm_ref, o_ref, lse_ref):
        i = pl.program_id(1)
        for qh in range(tq // sub):
            rows = slice(qh * sub, (qh + 1) * sub)
            q = q_ref[rows, :]
            ss = []
            for kb in range(qh, qh + n_sub):
                k_ref, krows = (kp_ref, kb) if kb < tq // sub else (kc_ref, kb - tq // sub)
                s = _dot(q, k_ref[krows * sub:(krows + 1) * sub, :], NT) * scale + bm_ref[rows, kb * sub:(kb + 1) * sub]
                if kb < tq // sub:
                    s = jnp.where(i == 0, NEG, s)
                ss.append(s)
            m = functools.reduce(jnp.maximum, [jnp.max(s, axis=-1, keepdims=True) for s in ss])
            ps = [jnp.exp(s - m) for s in ss]
            l = functools.reduce(jnp.add, [jnp.sum(p, axis=-1, keepdims=True) for p in ps])
            o = None
            for p, kb in zip(ps, range(qh, qh + n_sub)):
                v_ref, vrows = (vp_ref, kb) if kb < tq // sub else (vc_ref, kb - tq // sub)
                t = _dot(p.astype(BF16), v_ref[vrows * sub:(vrows + 1) * sub, :], NN)
                o = t if o is None else o + t
            o_ref[rows, :] = (o / l).astype(o_ref.dtype)
            lse_ref[rows, :] = m + jnp.log(l)

    def col(base):
        return (pl.BlockSpec((tq, HEAD_DIM_A), lambda h, i: (jnp.maximum(i - 1, 0), base + h)),
                pl.BlockSpec((tq, HEAD_DIM_A), lambda h, i: (i, base + h)))

    kp, kc = col(H)
    vp, vc = col(2 * H)
    return _call(
        body, name="attn_fwd", grid=(H, nb),
        in_specs=[pl.BlockSpec((tq, HEAD_DIM_A), lambda h, i: (i, h)), kp, kc, vp, vc,
                  pl.BlockSpec((None, tq, 2 * tq), lambda h, i: (h, 0, 0))],
        out_specs=[pl.BlockSpec((tq, HEAD_DIM_A), lambda h, i: (i, h)),
                   pl.BlockSpec((None, tq, 1), lambda h, i: (h, i, 0))],
        out_shape=[jax.ShapeDtypeStruct((S, D), BF16), jax.ShapeDtypeStruct((H, S, 1), F32)],
        args=[proj, proj, proj, proj, proj, bm], sem=("parallel", "arbitrary"), comm=comm)


def attn_bwd(proj, ya, dya, lse, bm, D, comm=None):
    S = proj.shape[0]
    H = D // HEAD_DIM_A
    tq = _attn_tq(S)
    nb = S // tq
    scale = HEAD_DIM_A ** -0.5
    sub, n_sub = _attn_subblocks(tq)

    def body(q_ref, kp_ref, kc_ref, vp_ref, vc_ref, o_ref, do_ref, lse_ref, bm_ref,
             dq_ref, dkc_ref, dkp_ref, dvc_ref, dvp_ref, ds_ref):
        i = pl.program_id(1)
        per = tq // sub

        @pl.when(i == 0)
        def _():
            ds_ref[...] = jnp.zeros_like(ds_ref)

        dk_acc = [None] * (2 * per)
        dv_acc = [None] * (2 * per)
        for qh in range(per):
            rows = slice(qh * sub, (qh + 1) * sub)
            q = q_ref[rows, :]
            do = do_ref[rows, :]
            delta = jnp.sum(do.astype(F32) * o_ref[rows, :].astype(F32), axis=-1, keepdims=True)
            lse_v = lse_ref[rows, :]
            dq = None
            for kb in range(qh, qh + n_sub):
                k_ref, v_ref, kr = (kp_ref, vp_ref, kb) if kb < per else (kc_ref, vc_ref, kb - per)
                k = k_ref[kr * sub:(kr + 1) * sub, :]
                cols = slice(kb * sub, (kb + 1) * sub)
                s = _dot(q, k, NT) * scale + bm_ref[rows, cols]
                if kb < per:
                    s = jnp.where(i == 0, NEG, s)
                p = jnp.exp(s - lse_v)
                dv = _dot(p.astype(BF16), do, TN)
                dp = _dot(do, v_ref[kr * sub:(kr + 1) * sub, :], NT)
                ds = p * (dp - delta)
                dsb = ds.astype(BF16)
                t = _dot(dsb, k, NN)
                dq = t if dq is None else dq + t
                dk = _dot(dsb, q, TN)
                dk_acc[kb] = dk if dk_acc[kb] is None else dk_acc[kb] + dk
                dv_acc[kb] = dv if dv_acc[kb] is None else dv_acc[kb] + dv
                ds_ref[rows, cols] += ds
            dq_ref[rows, :] = (dq * scale).astype(dq_ref.dtype)
        for kb in range(2 * per):
            dk_ref, dv_ref, kr = (dkp_ref, dvp_ref, kb) if kb < per else (dkc_ref, dvc_ref, kb - per)
            dk_ref[kr * sub:(kr + 1) * sub, :] = (dk_acc[kb] * scale).astype(dk_ref.dtype)
            dv_ref[kr * sub:(kr + 1) * sub, :] = dv_acc[kb].astype(dv_ref.dtype)

    def col(base):
        return (pl.BlockSpec((tq, HEAD_DIM_A), lambda h, i: (jnp.maximum(i - 1, 0), base + h)),
                pl.BlockSpec((tq, HEAD_DIM_A), lambda h, i: (i, base + h)))

    kp, kc = col(H)
    vp, vc = col(2 * H)
    blk = pl.BlockSpec((tq, HEAD_DIM_A), lambda h, i: (i, h))
    sd = jax.ShapeDtypeStruct((S, D), BF16)
    return _call(
        body, name="attn_bwd", grid=(H, nb),
        in_specs=[blk, kp, kc, vp, vc, blk, blk,
                  pl.BlockSpec((None, tq, 1), lambda h, i: (h, i, 0)),
                  pl.BlockSpec((None, tq, 2 * tq), lambda h, i: (h, 0, 0))],
        out_specs=[blk, blk, blk, blk, blk, pl.BlockSpec((None, tq, 2 * tq), lambda h, i: (h, 0, 0))],
        out_shape=[sd, sd, sd, sd, sd, jax.ShapeDtypeStruct((H, tq, 2 * tq), F32)],
        args=[proj, proj, proj, proj, proj, ya, dya, lse, bm], sem=("parallel", "arbitrary"), comm=comm)


def rel_bias_grad(ds_sum):
    H, tq, w = ds_sum.shape
    nbin = 2 * MAX_REL + 1
    nbin_pad = 3 * LANES
    d_lo, d_hi = -(CHUNK - 1), (N_PAST_CHUNKS + 1) * CHUNK - 1
    assert d_hi - d_lo + 1 <= w
    onehot = np.zeros((w, nbin_pad), np.float32)
    for d in range(d_lo, d_hi + 1):
        onehot[(tq - d) % w, int(np.clip(d, -MAX_REL, MAX_REL)) + MAX_REL] = 1.0
    nbits = int(np.log2(tq))
    assert (1 << nbits) == tq

    def body(ds_ref, m_ref, o_ref):
        x = ds_ref[...]
        row = lax.broadcasted_iota(jnp.int32, x.shape, 0)
        for b in range(nbits):
            rolled = pltpu.roll(x, w - (1 << b), 1)
            x = jnp.where(((row >> b) & 1) == 1, rolled, x)
        t = jnp.sum(x, axis=0, keepdims=True)
        o_ref[...] = lax.dot_general(t, m_ref[...], (NN, ((), ())), precision=lax.Precision.HIGHEST,
                                     preferred_element_type=F32)

    out = pl.pallas_call(
        body, grid=(H,),
        in_specs=[pl.BlockSpec((None, tq, w), lambda h: (h, 0, 0)),
                  pl.BlockSpec((w, nbin_pad), lambda h: (0, 0))],
        out_specs=pl.BlockSpec((None, 1, nbin_pad), lambda h: (h, 0, 0)),
        out_shape=jax.ShapeDtypeStruct((H, 1, nbin_pad), F32),
        name="rel_bias_grad", compiler_params=_cparams("parallel"))(ds_sum, jnp.asarray(onehot))
    return out[:, 0, :nbin]


def _conv_t(S):
    return _pick(S, (256,))


ROW_CHUNK = 16


def _row_loop(n_rows, step):
    def one(r, carry):
        step(pl.ds(pl.multiple_of(r * ROW_CHUNK, ROW_CHUNK), ROW_CHUNK))
        return carry

    lax.fori_loop(0, n_rows // ROW_CHUNK, one, 0)


def _fill_zbuf(zbuf, ap_ref, bp_ref, a_ref, b_ref, i):
    zp = ap_ref[...].astype(F32) * _sigmoid(bp_ref[...].astype(F32))
    zbuf[0:CONV_HALO, :] = jnp.where(i == 0, 0.0, zp)

    def step(rows):
        below = pl.ds(pl.multiple_of(rows.start + CONV_HALO, ROW_CHUNK), ROW_CHUNK)
        zbuf[below, :] = a_ref[rows, :].astype(F32) * _sigmoid(b_ref[rows, :].astype(F32))

    _row_loop(a_ref.shape[0], step)


def _shifted_windows(buf, shifted, lanes, T):
    rows = T + CONV_HALO - SUBLANES
    for b in range(1, SUBLANES):
        shifted[b - 1] = buf[pl.ds(b, rows), lanes]

    def window(off, r0=0, n=T):
        a, b = divmod(off, SUBLANES)
        if b == 0:
            return buf[pl.ds(r0 + off, n), lanes]
        return shifted[b - 1, pl.ds(r0 + a * SUBLANES, n), :]

    return window


def _shifted_scratch(T):
    return pltpu.VMEM((SUBLANES - 1, T + CONV_HALO - SUBLANES, LANES), F32)


def conv_gate_fwd(proj, ya, cw, cb, lng, lnb, D, comm=None):
    S = proj.shape[0]
    T = _conv_t(S)
    hb = T // CONV_HALO
    nlb = D // LANES

    def body(ap_ref, bp_ref, a_ref, b_ref, ga_ref, gb_ref, ya_ref, cw_ref, cb_ref, lng_ref, lnb_ref,
             y_ref, c_ref, zbuf, zsh):
        i = pl.program_id(0)
        _fill_zbuf(zbuf, ap_ref, bp_ref, a_ref, b_ref, i)

        def lane_block(lb, carry):
            lanes = pl.ds(pl.multiple_of(lb * LANES, LANES), LANES)
            z_at = _shifted_windows(zbuf, zsh, lanes, T)
            acc = jnp.zeros((T, LANES), F32)
            for k in range(CONV_WIDTH):
                acc = acc + cw_ref[k:k + 1, lanes] * z_at(CONV_HALO - CONV_WIDTH + 1 + k)
            c_ref[:, lanes] = acc + cb_ref[:, lanes]
            return carry

        lax.fori_loop(0, nlb, lane_block, 0)

        def norm_and_gate(rows):
            c = c_ref[rows, :]
            mu = jnp.mean(c, axis=-1, keepdims=True)
            xc = c - mu
            rstd = lax.rsqrt(jnp.mean(xc * xc, axis=-1, keepdims=True) + EPS)
            ln = xc * rstd * lng_ref[...] + lnb_ref[...]
            yb = ln * _sigmoid(ln)
            ga = ga_ref[rows, :].astype(F32)
            gb = gb_ref[rows, :].astype(F32)
            y_ref[rows, :D] = (ya_ref[rows, :].astype(F32) * (ga * _sigmoid(ga))).astype(y_ref.dtype)
            y_ref[rows, D:] = (yb * (gb * _sigmoid(gb))).astype(y_ref.dtype)

        _row_loop(T, norm_and_gate)

    def cur(cidx):
        return pl.BlockSpec((T, D), lambda i: (i, cidx))

    def prev(cidx):
        return pl.BlockSpec((CONV_HALO, D), lambda i: (jnp.maximum(i * hb - 1, 0), cidx))

    vec = pl.BlockSpec((1, D), lambda i: (0, 0))
    return _call(
        body, name="conv_gate_fwd", grid=(S // T,),
        in_specs=[prev(3), prev(4), cur(3), cur(4), cur(5), cur(6), pl.BlockSpec((T, D), lambda i: (i, 0)),
                  pl.BlockSpec((CONV_HALO, D), lambda i: (0, 0)), vec, vec, vec],
        out_specs=[pl.BlockSpec((T, 2 * D), lambda i: (i, 0)), pl.BlockSpec((T, D), lambda i: (i, 0))],
        out_shape=[jax.ShapeDtypeStruct((S, 2 * D), BF16), jax.ShapeDtypeStruct((S, D), F32)],
        scratch_shapes=[pltpu.VMEM((T + CONV_HALO, D), F32), _shifted_scratch(T)],
        args=[proj, proj, proj, proj, proj, proj, ya, cw, cb, lng, lnb], sem=("parallel",), comm=comm)


def conv_gate_bwd_a(dy0, proj, ya, cpre, lng, lnb, D):
    S = proj.shape[0]
    T = _conv_t(S)

    def body(dy_ref, ga_ref, gb_ref, ya_ref, c_ref, lng_ref, lnb_ref,
             dya_ref, dg_ref, dc_ref, dlng_ref, dlnb_ref):
        i = pl.program_id(0)

        c = c_ref[...]
        gv = lng_ref[...]
        mu = jnp.mean(c, axis=-1, keepdims=True)
        xc = c - mu
        rstd = lax.rsqrt(jnp.mean(xc * xc, axis=-1, keepdims=True) + EPS)
        xhat = xc * rstd
        ln = xhat * gv + lnb_ref[...]
        sl = _sigmoid(ln)
        yb = ln * sl
        ga = ga_ref[...].astype(F32)
        gb = gb_ref[...].astype(F32)
        sa = _sigmoid(ga)
        sb = _sigmoid(gb)
        dy_a = dy_ref[:, :D].astype(F32)
        dy_b = dy_ref[:, D:].astype(F32)
        dya_ref[...] = (dy_a * (ga * sa)).astype(dya_ref.dtype)
        dg_ref[:, :D] = (dy_a * ya_ref[...].astype(F32) * (sa * (1.0 + ga * (1.0 - sa)))).astype(dg_ref.dtype)
        dg_ref[:, D:] = (dy_b * yb * (sb * (1.0 + gb * (1.0 - sb)))).astype(dg_ref.dtype)
        dln = dy_b * (gb * sb) * (sl * (1.0 + ln * (1.0 - sl)))
        dxhat = dln * gv
        dc_ref[...] = rstd * (dxhat - jnp.mean(dxhat, axis=-1, keepdims=True)
                              - xhat * jnp.mean(dxhat * xhat, axis=-1, keepdims=True))
        dlng = jnp.sum(dln * xhat, axis=0, keepdims=True)
        dlnb = jnp.sum(dln, axis=0, keepdims=True)

        @pl.when(i == 0)
        def _():
            dlng_ref[...] = dlng
            dlnb_ref[...] = dlnb

        @pl.when(i > 0)
        def _():
            dlng_ref[...] += dlng
            dlnb_ref[...] += dlnb

    row = pl.BlockSpec((T, D), lambda i: (i, 0))
    vec = pl.BlockSpec((1, D), lambda i: (0, 0))
    return pl.pallas_call(
        body, grid=(S // T,),
        in_specs=[pl.BlockSpec((T, 2 * D), lambda i: (i, 0)),
                  pl.BlockSpec((T, D), lambda i: (i, 5)), pl.BlockSpec((T, D), lambda i: (i, 6)),
                  row, row, vec, vec],
        out_specs=[row, pl.BlockSpec((T, 2 * D), lambda i: (i, 0)), row, vec, vec],
        out_shape=[jax.ShapeDtypeStruct((S, D), BF16), jax.ShapeDtypeStruct((S, 2 * D), BF16),
                   jax.ShapeDtypeStruct((S, D), F32), jax.ShapeDtypeStruct((1, D), F32),
                   jax.ShapeDtypeStruct((1, D), F32)],
        name="conv_gate_bwd_a", compiler_params=_cparams("arbitrary"))(
            dy0, proj, proj, ya, cpre, lng, lnb)


def conv_gate_bwd_b(dc, proj, cw, D, comm=None):
    S = proj.shape[0]
    T = _conv_t(S)
    hb = T // CONV_HALO
    nt = S // T
    nlb = D // LANES
    half = T // 2

    def body(dc_ref, dn_ref, ap_ref, bp_ref, a_ref, b_ref, cw_ref, da_ref, db_ref, dcw_ref, dcb_ref,
             zbuf, dcbuf, zsh, dcsh, dcw8):
        i = pl.program_id(0)
        _fill_zbuf(zbuf, ap_ref, bp_ref, a_ref, b_ref, i)
        dcv = dc_ref[...]
        dcbuf[0:T, :] = dcv
        dcbuf[T:, :] = jnp.where(i == nt - 1, 0.0, dn_ref[...])

        @pl.when(i == 0)
        def _():
            dcw8[...] = jnp.zeros_like(dcw8)
            dcb_ref[...] = jnp.zeros_like(dcb_ref)

        dcb_ref[...] += jnp.sum(dcv, axis=0, keepdims=True)

        def lane_block(lb, carry):
            lanes = pl.ds(pl.multiple_of(lb * LANES, LANES), LANES)
            z_at = _shifted_windows(zbuf, zsh, lanes, T)
            dc_at = _shifted_windows(dcbuf, dcsh, lanes, T)
            for r0 in range(0, T, half):
                d0 = dcbuf[r0:r0 + half, lanes]
                dz = jnp.zeros((half, LANES), F32)
                for k in range(CONV_WIDTH):
                    dz = dz + cw_ref[k:k + 1, lanes] * dc_at(CONV_WIDTH - 1 - k, r0, half)
                    prod = d0 * z_at(CONV_HALO - CONV_WIDTH + 1 + k, r0, half)
                    dcw8[pl.ds(k * SUBLANES, SUBLANES), lanes] += jnp.sum(
                        prod.reshape(half // SUBLANES, SUBLANES, LANES), axis=0)
                av = a_ref[r0:r0 + half, lanes].astype(F32)
                sg = _sigmoid(b_ref[r0:r0 + half, lanes].astype(F32))
                da_ref[r0:r0 + half, lanes] = (dz * sg).astype(da_ref.dtype)
                db_ref[r0:r0 + half, lanes] = (dz * av * sg * (1.0 - sg)).astype(db_ref.dtype)
            return carry

        lax.fori_loop(0, nlb, lane_block, 0)

        @pl.when(i == nt - 1)
        def _():
            dcw_ref[...] = jnp.sum(dcw8[...].reshape(CONV_HALO, SUBLANES, D), axis=1)

    def cur(cidx):
        return pl.BlockSpec((T, D), lambda i: (i, cidx))

    def prev(cidx):
        return pl.BlockSpec((CONV_HALO, D), lambda i: (jnp.maximum(i * hb - 1, 0), cidx))

    row = pl.BlockSpec((T, D), lambda i: (i, 0))
    nxt = pl.BlockSpec((CONV_HALO, D), lambda i: (jnp.minimum((i + 1) * hb, nt * hb - 1), 0))
    return _call(
        body, name="conv_gate_bwd_b", grid=(nt,),
        in_specs=[row, nxt, prev(3), prev(4), cur(3), cur(4), pl.BlockSpec((CONV_HALO, D), lambda i: (0, 0))],
        out_specs=[row, row, pl.BlockSpec((CONV_HALO, D), lambda i: (0, 0)),
                   pl.BlockSpec((1, D), lambda i: (0, 0))],
        out_shape=[jax.ShapeDtypeStruct((S, D), BF16), jax.ShapeDtypeStruct((S, D), BF16),
                   jax.ShapeDtypeStruct((CONV_HALO, D), F32), jax.ShapeDtypeStruct((1, D), F32)],
        scratch_shapes=[pltpu.VMEM((T + CONV_HALO, D), F32), pltpu.VMEM((T + CONV_HALO, D), F32),
                        _shifted_scratch(T), _shifted_scratch(T), pltpu.VMEM((CONV_HALO * SUBLANES, D), F32)],
        args=[dc, dc, proj, proj, proj, proj, cw], sem=("arbitrary",), comm=comm)


def assemble_dproj0(dq, dkc, dkp, dvc, dvp, da, db, dgate, D):
    S = dq.shape[0]
    tq = _attn_tq(S)
    T = _pick(S, (256,))
    shift = tq // T
    nt = S // T

    def body(dq_ref, dkc_ref, dkp_ref, dvc_ref, dvp_ref, da_ref, db_ref, dg_ref, o_ref):
        i = pl.program_id(0)
        last = i + shift >= nt
        o_ref[:, 0:D] = dq_ref[...]
        dk = dkc_ref[...].astype(F32) + jnp.where(last, 0.0, dkp_ref[...].astype(F32))
        dv = dvc_ref[...].astype(F32) + jnp.where(last, 0.0, dvp_ref[...].astype(F32))
        o_ref[:, D:2 * D] = dk.astype(o_ref.dtype)
        o_ref[:, 2 * D:3 * D] = dv.astype(o_ref.dtype)
        o_ref[:, 3 * D:4 * D] = da_ref[...]
        o_ref[:, 4 * D:5 * D] = db_ref[...]
        o_ref[:, 5 * D:] = dg_ref[...]

    row = pl.BlockSpec((T, D), lambda i: (i, 0))
    nxt = pl.BlockSpec((T, D), lambda i: (jnp.minimum(i + shift, nt - 1), 0))
    return pl.pallas_call(
        body, grid=(nt,),
        in_specs=[row, row, nxt, row, nxt, row, row, pl.BlockSpec((T, 2 * D), lambda i: (i, 0))],
        out_specs=pl.BlockSpec((T, 7 * D), lambda i: (i, 0)),
        out_shape=jax.ShapeDtypeStruct((S, 7 * D), BF16),
        name="assemble_dproj0", compiler_params=_cparams("parallel"))(dq, dkc, dkp, dvc, dvp, da, db, dgate)


def _sgu_t(S):
    return _pick(S, (256, 128))


def _ws_masked(ws_ref, g):
    row = lax.broadcasted_iota(jnp.int32, (GMLP_CHUNK, GMLP_CHUNK), 0) // CHUNK
    col = lax.broadcasted_iota(jnp.int32, (GMLP_CHUNK, GMLP_CHUNK), 1) // CHUNK
    return jnp.where(row >= col, ws_ref[g], 0.0), row >= col


def sgu_fwd(proj, lng, lnb, ws, bst, MIX):
    S = proj.shape[0]
    T = _sgu_t(S)
    gw = MIX // N_GROUPS_C

    def body(u_ref, v_ref, g_ref, lng_ref, lnb_ref, ws_ref, bst_ref, y_ref):
        v = v_ref[...].astype(F32)
        mu = jnp.mean(v, axis=-1, keepdims=True)
        xc = v - mu
        rstd = lax.rsqrt(jnp.mean(xc * xc, axis=-1, keepdims=True) + EPS)
        for g in range(N_GROUPS_C):
            cols = slice(g * gw, (g + 1) * gw)
            wsm = _ws_masked(ws_ref, g)[0].astype(BF16)
            vn = (xc[:, cols] * rstd * lng_ref[:, cols] + lnb_ref[:, cols]).astype(BF16)
            for blk in range(T // GMLP_CHUNK):
                rows = slice(blk * GMLP_CHUNK, (blk + 1) * GMLP_CHUNK)
                sg = _dot(wsm, vn[rows], NN) + bst_ref[:, g:g + 1]
                gate = g_ref[rows, cols].astype(F32)
                y = u_ref[rows, cols].astype(F32) * sg * (gate * _sigmoid(gate))
                y_ref[rows, cols] = y.astype(y_ref.dtype)

    def part(cidx):
        return pl.BlockSpec((T, MIX), lambda i: (i, cidx))

    vec = pl.BlockSpec((1, MIX), lambda i: (0, 0))
    return pl.pallas_call(
        body, grid=(S // T,),
        in_specs=[part(0), part(1), part(2), vec, vec,
                  pl.BlockSpec((N_GROUPS_C, GMLP_CHUNK, GMLP_CHUNK), lambda i: (0, 0, 0)),
                  pl.BlockSpec((GMLP_CHUNK, N_GROUPS_C), lambda i: (0, 0))],
        out_specs=pl.BlockSpec((T, MIX), lambda i: (i, 0)),
        out_shape=jax.ShapeDtypeStruct((S, MIX), BF16),
        name="sgu_fwd", compiler_params=_cparams("parallel"))(proj, proj, proj, lng, lnb, ws, bst)


def sgu_bwd(dy1, proj, lng, lnb, ws, bst, MIX):
    S = proj.shape[0]
    T = _sgu_t(S)
    gw = MIX // N_GROUPS_C

    def body(dy_ref, u_ref, v_ref, g_ref, lng_ref, lnb_ref, ws_ref, bst_ref,
             dp_ref, dws_ref, dbst_ref, dlng_ref, dlnb_ref, dvn_buf):
        i = pl.program_id(0)

        @pl.when(i == 0)
        def _():
            dws_ref[...] = jnp.zeros_like(dws_ref)
            dbst_ref[...] = jnp.zeros_like(dbst_ref)
            dlng_ref[...] = jnp.zeros_like(dlng_ref)
            dlnb_ref[...] = jnp.zeros_like(dlnb_ref)

        v = v_ref[...].astype(F32)
        mu = jnp.mean(v, axis=-1, keepdims=True)
        xc = v - mu
        rstd = lax.rsqrt(jnp.mean(xc * xc, axis=-1, keepdims=True) + EPS)
        for g in range(N_GROUPS_C):
            cols = slice(g * gw, (g + 1) * gw)
            wsf, keep = _ws_masked(ws_ref, g)
            wsm = wsf.astype(BF16)
            vn = (xc[:, cols] * rstd * lng_ref[:, cols] + lnb_ref[:, cols]).astype(BF16)
            for blk in range(T // GMLP_CHUNK):
                rows = slice(blk * GMLP_CHUNK, (blk + 1) * GMLP_CHUNK)
                vnb = vn[rows]
                sg = _dot(wsm, vnb, NN) + bst_ref[:, g:g + 1]
                gate = g_ref[rows, cols].astype(F32)
                sig = _sigmoid(gate)
                sil = gate * sig
                u = u_ref[rows, cols].astype(F32)
                dy = dy_ref[rows, cols].astype(F32)
                dp_ref[rows, g * gw:(g + 1) * gw] = (dy * sg * sil).astype(dp_ref.dtype)
                dp_ref[rows, 2 * MIX + g * gw:2 * MIX + (g + 1) * gw] = (
                    dy * u * sg * (sig * (1.0 + gate * (1.0 - sig)))).astype(dp_ref.dtype)
                dsg = dy * u * sil
                dsgb = dsg.astype(BF16)
                dvn_buf[rows, cols] = _dot(wsm, dsgb, TN)
                dws_ref[g] += jnp.where(keep, _dot(dsgb, vnb, NT), 0.0)
                dbst_ref[:, g:g + 1] += jnp.sum(dsg, axis=-1, keepdims=True)
        dvn = dvn_buf[...]
        xhat = xc * rstd
        dxhat = dvn * lng_ref[...]
        dv = rstd * (dxhat - jnp.mean(dxhat, axis=-1, keepdims=True)
                     - xhat * jnp.mean(dxhat * xhat, axis=-1, keepdims=True))
        dp_ref[:, MIX:2 * MIX] = dv.astype(dp_ref.dtype)
        dlng_ref[...] += jnp.sum(dvn * xhat, axis=0, keepdims=True)
        dlnb_ref[...] += jnp.sum(dvn, axis=0, keepdims=True)

    def part(cidx):
        return pl.BlockSpec((T, MIX), lambda i: (i, cidx))

    vec = pl.BlockSpec((1, MIX), lambda i: (0, 0))
    wspec = pl.BlockSpec((N_GROUPS_C, GMLP_CHUNK, GMLP_CHUNK), lambda i: (0, 0, 0))
    bspec = pl.BlockSpec((GMLP_CHUNK, N_GROUPS_C), lambda i: (0, 0))
    return pl.pallas_call(
        body, grid=(S // T,),
        in_specs=[pl.BlockSpec((T, MIX), lambda i: (i, 0)), part(0), part(1), part(2), vec, vec, wspec, bspec],
        out_specs=[pl.BlockSpec((T, 3 * MIX), lambda i: (i, 0)), wspec, bspec, vec, vec],
        out_shape=[jax.ShapeDtypeStruct((S, 3 * MIX), BF16),
                   jax.ShapeDtypeStruct((N_GROUPS_C, GMLP_CHUNK, GMLP_CHUNK), F32),
                   jax.ShapeDtypeStruct((GMLP_CHUNK, N_GROUPS_C), F32),
                   jax.ShapeDtypeStruct((1, MIX), F32), jax.ShapeDtypeStruct((1, MIX), F32)],
        scratch_shapes=[pltpu.VMEM((T, MIX), F32)],
        name="sgu_bwd", compiler_params=_cparams("arbitrary"))(dy1, proj, proj, proj, lng, lnb, ws, bst)


def xattn_fwd(name, q, k, v):
    S, D = q.shape
    nm = k.shape[0]
    dh = D // N_HEADS_X
    tq = _pick(S, (512, 256))
    scale = dh ** -0.5

    def body(q_ref, k_ref, v_ref, o_ref, lse_ref):
        s = _dot(q_ref[...], k_ref[...], NT) * scale
        m = jnp.max(s, axis=-1, keepdims=True)
        p = jnp.exp(s - m)
        l = jnp.sum(p, axis=-1, keepdims=True)
        o_ref[...] = (_dot(p.astype(BF16), v_ref[...], NN) / l).astype(o_ref.dtype)
        lse_ref[...] = m + jnp.log(l)

    return pl.pallas_call(
        body, grid=(N_HEADS_X, S // tq),
        in_specs=[pl.BlockSpec((tq, dh), lambda h, i: (i, h)),
                  pl.BlockSpec((nm, dh), lambda h, i: (0, h)), pl.BlockSpec((nm, dh), lambda h, i: (0, h))],
        out_specs=[pl.BlockSpec((tq, dh), lambda h, i: (i, h)),
                   pl.BlockSpec((None, tq, 1), lambda h, i: (h, i, 0))],
        out_shape=[jax.ShapeDtypeStruct((S, D), BF16), jax.ShapeDtypeStruct((N_HEADS_X, S, 1), F32)],
        name=name, compiler_params=_cparams("parallel", "parallel"))(q, k, v)


def xattn_bwd(name, q, k, v, o, do, lse):
    S, D = q.shape
    nm = k.shape[0]
    dh = D // N_HEADS_X
    tq = _pick(S, (512, 256))
    scale = dh ** -0.5

    def body(q_ref, k_ref, v_ref, o_ref, do_ref, lse_ref, dq_ref, dk_ref, dv_ref):
        i = pl.program_id(1)
        q_v = q_ref[...]
        k_v = k_ref[...]
        do_v = do_ref[...]
        p = jnp.exp(_dot(q_v, k_v, NT) * scale - lse_ref[...])
        delta = jnp.sum(do_v.astype(F32) * o_ref[...].astype(F32), axis=-1, keepdims=True)
        dv = _dot(p.astype(BF16), do_v, TN)
        ds = (p * (_dot(do_v, v_ref[...], NT) - delta)).astype(BF16)
        dq_ref[...] = (_dot(ds, k_v, NN) * scale).astype(dq_ref.dtype)
        dk = _dot(ds, q_v, TN) * scale

        @pl.when(i == 0)
        def _():
            dk_ref[...] = dk
            dv_ref[...] = dv

        @pl.when(i > 0)
        def _():
            dk_ref[...] += dk
            dv_ref[...] += dv

    qs = pl.BlockSpec((tq, dh), lambda h, i: (i, h))
    ks = pl.BlockSpec((nm, dh), lambda h, i: (0, h))
    return pl.pallas_call(
        body, grid=(N_HEADS_X, S // tq),
        in_specs=[qs, ks, ks, qs, qs, pl.BlockSpec((None, tq, 1), lambda h, i: (h, i, 0))],
        out_specs=[qs, ks, ks],
        out_shape=[jax.ShapeDtypeStruct((S, D), BF16), jax.ShapeDtypeStruct((nm, D), F32),
                   jax.ShapeDtypeStruct((nm, D), F32)],
        name=name, compiler_params=_cparams("parallel", "arbitrary"))(q, k, v, o, do, lse)


def adamw_rows(name, w, m, v, groups, row_off):
    L, R, C = w.shape
    assert len(groups) == L
    tr = _pick(R, tuple(t for t in (512, 256, 128, 64, 32, 16, 8) if t * C * 4 <= (1 << 20)) or (8,))
    assert row_off % tr == 0
    c1 = 1.0 - ADAM_B1 ** ADAM_STEP
    c2 = 1.0 - ADAM_B2 ** ADAM_STEP

    def body(w_ref, m_ref, v_ref, *refs):
        g_refs, (go_ref, d_ref, nm_ref, nv_ref) = refs[:L], refs[L:]
        layer = pl.program_id(0)
        gv = g_refs[0][...]
        for i in range(1, L):
            gv = jnp.where(layer == i, g_refs[i][...], gv)
        nm = ADAM_B1 * m_ref[...] + (1.0 - ADAM_B1) * gv
        nv = ADAM_B2 * v_ref[...] + (1.0 - ADAM_B2) * (gv * gv)
        go_ref[...] = gv
        d_ref[...] = -ADAM_LR * ((nm / c1) / (jnp.sqrt(nv / c2) + ADAM_EPS) + ADAM_WD * w_ref[...])
        nm_ref[...] = nm
        nv_ref[...] = nv

    blk = pl.BlockSpec((None, tr, C), lambda l, r: (l, r, 0))
    gblk = pl.BlockSpec((tr, C), lambda l, r: (row_off // tr + r, 0))
    sd = jax.ShapeDtypeStruct((L, R, C), F32)
    return pl.pallas_call(body, grid=(L, R // tr), in_specs=[blk] * 3 + [gblk] * L, out_specs=[blk] * 4,
                          out_shape=[sd] * 4, name=name,
                          compiler_params=_cparams("parallel", "parallel"))(w, m, v, *groups)


def adamw_many(name, tensors):
    n = len(tensors)
    c1 = 1.0 - ADAM_B1 ** ADAM_STEP
    c2 = 1.0 - ADAM_B2 ** ADAM_STEP

    def as2d(a):
        return a.reshape((1, -1) if a.ndim == 1 else (-1, a.shape[-1])).astype(F32)

    flat = [as2d(a) for t in tensors for a in t]

    def body(*refs):
        ins, outs = refs[:4 * n], refs[4 * n:]
        for t in range(n):
            w_ref, g_ref, m_ref, v_ref = ins[4 * t:4 * t + 4]
            d_ref, nm_ref, nv_ref = outs[3 * t:3 * t + 3]
            gv = g_ref[...]
            nm = ADAM_B1 * m_ref[...] + (1.0 - ADAM_B1) * gv
            nv = ADAM_B2 * v_ref[...] + (1.0 - ADAM_B2) * (gv * gv)
            d_ref[...] = -ADAM_LR * ((nm / c1) / (jnp.sqrt(nv / c2) + ADAM_EPS) + ADAM_WD * w_ref[...])
            nm_ref[...] = nm
            nv_ref[...] = nv

    vm = pl.BlockSpec(memory_space=pltpu.VMEM)
    shapes = [jax.ShapeDtypeStruct(flat[4 * t].shape, F32) for t in range(n) for _ in range(3)]
    res = pl.pallas_call(body, in_specs=[vm] * (4 * n), out_specs=[vm] * (3 * n), out_shape=shapes, name=name,
                         compiler_params=pltpu.CompilerParams(vmem_limit_bytes=V7X_VMEM_LIMIT))(*flat)
    return [tuple(r.reshape(tensors[t][0].shape) for r in res[3 * t:3 * t + 3]) for t in range(n)]


def add_halves(name, g4, recv, cidx):
    _, R, C = g4.shape
    rh = R // 2
    tr = _pick(rh, (256, 128, 64, 32, 16))
    nrb = rh // tr

    def body(c_ref, a_ref, b_ref, o_ref):
        o_ref[...] = (a_ref[...].astype(F32) + b_ref[...].astype(F32)).astype(o_ref.dtype)

    grid_spec = pltpu.PrefetchScalarGridSpec(
        num_scalar_prefetch=1, grid=(4, nrb),
        in_specs=[pl.BlockSpec((None, tr, C), lambda j, r, c_ref: (j, c_ref[0] * nrb + r, 0)),
                  pl.BlockSpec((None, tr, C), lambda j, r, c_ref: (j, r, 0))],
        out_specs=pl.BlockSpec((None, tr, C), lambda j, r, c_ref: (j, r, 0)))
    return pl.pallas_call(body, grid_spec=grid_spec, out_shape=jax.ShapeDtypeStruct((4, rh, C), BF16),
                          name=name, compiler_params=_cparams("parallel", "parallel"))(cidx, g4, recv)


def sum_chips(name, own, recv, place):
    _, rh, C = own.shape
    tr = _pick(rh, (256, 128, 64, 32, 16))
    nrb = rh // tr

    def body(s_ref, own_ref, recv_ref, o_ref):
        acc = own_ref[...].astype(F32)
        for k in range(N_CHIPS - 1):
            acc = acc + recv_ref[k].astype(F32)
        o_ref[...] = acc

    grid_spec = pltpu.PrefetchScalarGridSpec(
        num_scalar_prefetch=1, grid=(nrb,),
        in_specs=[pl.BlockSpec((None, tr, C), lambda r, s: (s[0], r, 0)),
                  pl.BlockSpec((N_CHIPS - 1, tr, C), lambda r, s: (0, r, 0))],
        out_specs=pl.BlockSpec((tr, C), lambda r, s: (s[1] * nrb + r, 0)))
    return pl.pallas_call(body, grid_spec=grid_spec, out_shape=jax.ShapeDtypeStruct((2 * rh, C), F32),
                          name=name, compiler_params=_cparams("parallel"))(place, own, recv)


def cast_into_slot(name, w, place):
    R, C = w.shape
    tr = _pick(R, (256, 128, 64, 32, 16))

    def body(s_ref, w_ref, o_ref):
        o_ref[...] = w_ref[...].astype(o_ref.dtype)

    grid_spec = pltpu.PrefetchScalarGridSpec(
        num_scalar_prefetch=1, grid=(R // tr,),
        in_specs=[pl.BlockSpec((tr, C), lambda r, s: (r, 0))],
        out_specs=pl.BlockSpec((None, tr, C), lambda r, s: (s[0], r, 0)))
    return pl.pallas_call(body, grid_spec=grid_spec, out_shape=jax.ShapeDtypeStruct((N_CHIPS, R, C), BF16),
                          name=name, compiler_params=_cparams("parallel"))(place, w)


def _place():
    return lax.axis_index("x"), lax.axis_index("y"), lax.axis_index("c")


_CHIP_FLIPS = ((1, 0), (0, 1), (1, 1))


def _flip(v, bit):
    return 1 - v if bit else v


HBM_SPEC = pl.BlockSpec(memory_space=pl.ANY)


def exchange_small(name, buf, reduce):
    R = buf.shape[0]

    def body(x_ref, *refs):
        if reduce:
            sum_ref, all_ref, send_sems, recv_sems, local_sem = refs
        else:
            all_ref, send_sems, recv_sems, local_sem = refs
        x, y, c = _place()
        me = 4 * x + 2 * y + c
        mine = pltpu.make_async_copy(x_ref, all_ref.at[me], local_sem)
        mine.start()
        sends = []
        for k in range(1, N_DEV):
            peer = (_flip(x, k & 4), _flip(y, k & 2), _flip(c, k & 1))
            cp = pltpu.make_async_remote_copy(src_ref=x_ref, dst_ref=all_ref.at[me], send_sem=send_sems.at[k - 1],
                                              recv_sem=recv_sems.at[k - 1], device_id=peer, device_id_type=MESH)
            cp.start()
            sends.append(cp)
        for k in range(1, N_DEV):
            peer = (_flip(x, k & 4), _flip(y, k & 2), _flip(c, k & 1))
            src = 4 * peer[0] + 2 * peer[1] + peer[2]
            pltpu.make_async_remote_copy(src_ref=x_ref, dst_ref=all_ref.at[src], send_sem=send_sems.at[k - 1],
                                         recv_sem=recv_sems.at[k - 1], device_id=peer,
                                         device_id_type=MESH).wait_recv()
        for cp in sends:
            cp.wait_send()
        mine.wait()
        if reduce:
            acc = all_ref[0]
            for d in range(1, N_DEV):
                acc = acc + all_ref[d]
            sum_ref[...] = acc

    vm = pl.BlockSpec(memory_space=pltpu.VMEM)
    sems = [pltpu.SemaphoreType.DMA((N_DEV - 1,)), pltpu.SemaphoreType.DMA((N_DEV - 1,)), pltpu.SemaphoreType.DMA]
    if reduce:
        return pl.pallas_call(
            body, in_specs=[vm], out_specs=vm, out_shape=jax.ShapeDtypeStruct((R, LANES), F32),
            scratch_shapes=[pltpu.VMEM((N_DEV, R, LANES), F32)] + sems, name=name,
            compiler_params=pltpu.CompilerParams(vmem_limit_bytes=V7X_VMEM_LIMIT))(buf)
    return pl.pallas_call(
        body, in_specs=[vm], out_specs=vm, out_shape=jax.ShapeDtypeStruct((N_DEV, R, LANES), F32),
        scratch_shapes=sems, name=name,
        compiler_params=pltpu.CompilerParams(vmem_limit_bytes=V7X_VMEM_LIMIT))(buf)


def exchange_job(buf):
    R = buf.shape[0]

    def copies(x_ref, all_ref, send_sems, recv_sems):
        x, y, c = _place()
        me = 4 * x + 2 * y + c
        sends, arrivals = [], []
        for k in range(1, N_DEV):
            peer = (_flip(x, k & 4), _flip(y, k & 2), _flip(c, k & 1))
            src = 4 * peer[0] + 2 * peer[1] + peer[2]
            sends.append(pltpu.make_async_remote_copy(
                src_ref=x_ref, dst_ref=all_ref.at[me], send_sem=send_sems.at[k - 1], recv_sem=recv_sems.at[k - 1],
                device_id=peer, device_id_type=MESH))
            arrivals.append(pltpu.make_async_remote_copy(
                src_ref=x_ref, dst_ref=all_ref.at[src], send_sem=send_sems.at[k - 1], recv_sem=recv_sems.at[k - 1],
                device_id=peer, device_id_type=MESH))
        return sends, arrivals

    def start(ins, outs, sems):
        for cp in copies(ins[0], outs[0], *sems)[0]:
            cp.start()

    def finish(ins, outs, sems):
        sends, arrivals = copies(ins[0], outs[0], *sems)
        for cp in arrivals:
            cp.wait_recv()
        for cp in sends:
            cp.wait_send()

    return _Comm([buf], [jax.ShapeDtypeStruct((N_DEV, R, LANES), F32)], {},
                 [pltpu.SemaphoreType.DMA((N_DEV - 1,)), pltpu.SemaphoreType.DMA((N_DEV - 1,))], start, finish)


def sum_devices(name, slots):
    _, R, _ = slots.shape
    tr = _pick(R, (512, 256, 128, 64, 32, 16, 8))

    def body(s_ref, o_ref):
        acc = s_ref[0]
        for d in range(1, N_DEV):
            acc = acc + s_ref[d]
        o_ref[...] = acc

    return pl.pallas_call(body, grid=(R // tr,),
                          in_specs=[pl.BlockSpec((N_DEV, tr, LANES), lambda r: (0, r, 0))],
                          out_specs=pl.BlockSpec((tr, LANES), lambda r: (r, 0)),
                          out_shape=jax.ShapeDtypeStruct((R, LANES), F32), name=name,
                          compiler_params=_cparams("parallel"))(slots)


def gather_job(slots, relay_frac=0.75, flips=None):
    n = len(slots)
    flips = flips or [tuple(range(len(_CHIP_FLIPS)))] * n

    def copies(o_refs, send_sems, recv_sems):
        x, y, c = _place()
        me = 2 * x + y
        sib = (x, y, 1 - c)
        chips = [(_flip(x, fx), _flip(y, fy)) for fx, fy in _CHIP_FLIPS]
        ici, fwd, from_sib = [], [], []
        for t in range(n):
            rh = o_refs[t].shape[1] // 2
            mine, theirs = pl.ds(c * rh, rh), pl.ds((1 - c) * rh, rh)
            for k, (px, py) in enumerate(chips):
                if k not in flips[t]:
                    continue
                own = o_refs[t].at[me, mine]
                ici.append(pltpu.make_async_remote_copy(
                    src_ref=own, dst_ref=own, send_sem=send_sems.at[t, k], recv_sem=recv_sems.at[t, k],
                    device_id=(px, py, c), device_id_type=MESH))
                landed = o_refs[t].at[2 * px + py, mine]
                arrival = pltpu.make_async_remote_copy(
                    src_ref=landed, dst_ref=landed, send_sem=send_sems.at[t, k], recv_sem=recv_sems.at[t, k],
                    device_id=(px, py, c), device_id_type=MESH)
                fwd.append((arrival, pltpu.make_async_remote_copy(
                    src_ref=landed, dst_ref=landed, send_sem=send_sems.at[t, 3 + k],
                    recv_sem=recv_sems.at[t, 3 + k], device_id=sib, device_id_type=MESH)))
                passed = o_refs[t].at[2 * px + py, theirs]
                from_sib.append(pltpu.make_async_remote_copy(
                    src_ref=passed, dst_ref=passed, send_sem=send_sems.at[t, 3 + k],
                    recv_sem=recv_sems.at[t, 3 + k], device_id=sib, device_id_type=MESH))
        return ici, fwd, from_sib

    def start(ins, o_refs, sems):
        for cp in copies(o_refs, *sems)[0]:
            cp.start()

    def relay(ins, o_refs, sems):
        for arrival, forward in copies(o_refs, *sems)[1]:
            arrival.wait_recv()
            forward.start()

    def finish(ins, o_refs, sems):
        ici, fwd, from_sib = copies(o_refs, *sems)
        for cp in from_sib:
            cp.wait_recv()
        for cp in ici:
            cp.wait_send()
        for _, forward in fwd:
            forward.wait_send()

    return _Comm(slots, [jax.ShapeDtypeStruct(s.shape, s.dtype) for s in slots], {t: t for t in range(n)},
                 [pltpu.SemaphoreType.DMA((n, 6)), pltpu.SemaphoreType.DMA((n, 6))], start, finish, relay,
                 relay_frac)


def sibling_halves_job(grads):
    n = len(grads)

    def copies(g_refs, o_refs, send_sems, recv_sems):
        x, y, c = _place()
        out = []
        for t in range(n):
            rh = g_refs[t].shape[1] // 2
            out.append(pltpu.make_async_remote_copy(
                src_ref=g_refs[t].at[:, pl.ds((1 - c) * rh, rh), :], dst_ref=o_refs[t],
                send_sem=send_sems.at[t], recv_sem=recv_sems.at[t], device_id=(x, y, 1 - c),
                device_id_type=MESH))
        return out

    def start(g_refs, o_refs, sems):
        for cp in copies(g_refs, o_refs, *sems):
            cp.start()

    def finish(g_refs, o_refs, sems):
        cps = copies(g_refs, o_refs, *sems)
        for cp in cps:
            cp.wait_recv()
        for cp in cps:
            cp.wait_send()

    return _Comm(grads, [jax.ShapeDtypeStruct((4, g.shape[1] // 2, g.shape[2]), g.dtype) for g in grads], {},
                 [pltpu.SemaphoreType.DMA((n,)), pltpu.SemaphoreType.DMA((n,))], start, finish)


def scatter_job(parts):
    n = len(parts)

    def copies(p_refs, o_refs, send_sems, recv_sems):
        x, y, c = _place()
        out = []
        for t in range(n):
            for k, (fx, fy) in enumerate(_CHIP_FLIPS):
                px, py = _flip(x, fx), _flip(y, fy)
                out.append(pltpu.make_async_remote_copy(
                    src_ref=p_refs[t].at[2 * px + py], dst_ref=o_refs[t].at[k],
                    send_sem=send_sems.at[t, k], recv_sem=recv_sems.at[t, k],
                    device_id=(px, py, c), device_id_type=MESH))
        return out

    def start(p_refs, o_refs, sems):
        for cp in copies(p_refs, o_refs, *sems):
            cp.start()

    def finish(p_refs, o_refs, sems):
        cps = copies(p_refs, o_refs, *sems)
        for cp in cps:
            cp.wait_recv()
        for cp in cps:
            cp.wait_send()

    return _Comm(parts, [jax.ShapeDtypeStruct((N_CHIPS - 1,) + p.shape[1:], p.dtype) for p in parts], {},
                 [pltpu.SemaphoreType.DMA((n, 3)), pltpu.SemaphoreType.DMA((n, 3))], start, finish)


def share_halves_job(halves):
    n = len(halves)

    def copies(o_refs, send_sems, recv_sems):
        x, y, c = _place()
        sends, arrivals = [], []
        for t in range(n):
            rh = o_refs[t].shape[0] // 2
            mine = o_refs[t].at[pl.ds(c * rh, rh)]
            theirs = o_refs[t].at[pl.ds((1 - c) * rh, rh)]
            sends.append(pltpu.make_async_remote_copy(
                src_ref=mine, dst_ref=mine, send_sem=send_sems.at[t], recv_sem=recv_sems.at[t],
                device_id=(x, y, 1 - c), device_id_type=MESH))
            arrivals.append(pltpu.make_async_remote_copy(
                src_ref=theirs, dst_ref=theirs, send_sem=send_sems.at[t], recv_sem=recv_sems.at[t],
                device_id=(x, y, 1 - c), device_id_type=MESH))
        return sends, arrivals

    def start(ins, o_refs, sems):
        for cp in copies(o_refs, *sems)[0]:
            cp.start()

    def finish(ins, o_refs, sems):
        sends, arrivals = copies(o_refs, *sems)
        for cp in arrivals:
            cp.wait_recv()
        for cp in sends:
            cp.wait_send()

    return _Comm(halves, [jax.ShapeDtypeStruct(h.shape, h.dtype) for h in halves], {t: t for t in range(n)},
                 [pltpu.SemaphoreType.DMA((n,)), pltpu.SemaphoreType.DMA((n,))], start, finish)


def _pack(arrs, row_multiple=SUBLANES):
    flat, total = [], 0
    for a in arrs:
        v = a.reshape(-1).astype(F32)
        pad = (-v.shape[0]) % (SUBLANES * LANES)
        flat.append(jnp.pad(v, (0, pad)))
        total += v.shape[0] + pad
    tail = (-total) % (row_multiple * LANES)
    if tail:
        flat.append(jnp.zeros((tail,), F32))
    return jnp.concatenate(flat).reshape(-1, LANES)


def _unpack(buf, shapes):
    out, off = [], 0
    flat = buf.reshape(-1)
    for s in shapes:
        n = int(np.prod(s))
        out.append(flat[off:off + n].reshape(s))
        off += n + ((-n) % (8 * LANES))
    return out


def _xattn_layer_fwd(tag, h, mem, gx, gmem, w):
    hx = rms_fwd(f"rms_x{tag}", h, gx)
    memn = rms_fwd(f"rms_mem{tag}", mem, gmem)
    q = mm_nn(f"xq{tag}", hx, w["q"], BF16)
    k = mm_nn(f"xk{tag}", memn, w["k"], BF16)
    v = mm_nn(f"xv{tag}", memn, w["v"], BF16)
    o, lse = xattn_fwd(f"xattn_fwd{tag}", q, k, v)
    h_out = mm_nn(f"xo{tag}", o, w["o"], F32, res=h)
    return h_out, dict(hx=hx, memn=memn, q=q, k=k, v=v, o=o, lse=lse)


def _xattn_layer_bwd(tag, dh_out, dh_out_b, h_in, mem, gx, gmem, w, sv):
    do = mm_nt(f"d_xo{tag}", dh_out_b, w["o"], BF16)
    dwo = mm_tn(f"dw_xo{tag}", sv["o"], dh_out_b)
    dq, dk, dv = xattn_bwd(f"xattn_bwd{tag}", sv["q"], sv["k"], sv["v"], sv["o"], do, sv["lse"])
    dwq = mm_tn(f"dw_xq{tag}", sv["hx"], dq)
    dhx = mm_nt(f"d_xq{tag}", dq, w["q"], BF16)
    dwk = mm_tn(f"dw_xk{tag}", sv["memn"], dk)
    dwv = mm_tn(f"dw_xv{tag}", sv["memn"], dv)
    dmk = mm_nt(f"d_xk{tag}", dk, w["k"], F32)
    dmv = mm_nt(f"d_xv{tag}", dv, w["v"], F32)
    dh_in, dh_in_b, dgx = rms_bwd(f"rms_x_bwd{tag}", h_in, gx, [dhx], dh_out)
    _, _, dgmem = rms_bwd(f"rms_mem_bwd{tag}", mem, gmem, [dmk, dmv], None)
    return dh_in, dh_in_b, dgx, dgmem, dict(q=dwq, k=dwk, v=dwv, o=dwo)


def kernel(x, mem, norm_mix_g, norm_x_g, norm_mem_g, final_norm_g, w_in_ab, rel_bias, conv_w, conv_b, conv_ln_g, conv_ln_b, w_out_ab, w_in_c, sgu_ln_g, sgu_ln_b, w_s, b_s, w_out_c, w_xq, w_xk, w_xv, w_xo, loss_target, m_norm_mix_g, m_norm_x_g, m_norm_mem_g, m_final_norm_g, m_w_in_ab, m_rel_bias, m_conv_w, m_conv_b, m_conv_ln_g, m_conv_ln_b, m_w_out_ab, m_w_in_c, m_sgu_ln_g, m_sgu_ln_b, m_w_s, m_b_s, m_w_out_c, m_w_xq, m_w_xk, m_w_xv, m_w_xo, v_norm_mix_g, v_norm_x_g, v_norm_mem_g, v_final_norm_g, v_w_in_ab, v_rel_bias, v_conv_w, v_conv_b, v_conv_ln_g, v_conv_ln_b, v_w_out_ab, v_w_in_c, v_sgu_ln_g, v_sgu_ln_b, v_w_s, v_b_s, v_w_out_c, v_w_xq, v_w_xk, v_w_xv, v_w_xo):
    S, D = x.shape[1], x.shape[2]
    MIX = 2 * D
    xs, mems, tgt = x[0], mem[0], loss_target[0]
    cx, cy, cc = _place()
    chip = 2 * cx + cy
    cidx = jnp.reshape(cc, (1,)).astype(jnp.int32)
    place = jnp.stack([chip, cc]).astype(jnp.int32)

    ro, rq = MIX // 4, D // 4
    row_sharded = [("out_ab", w_out_ab[0]), ("out_c", w_out_c[0])]
    for layer in range(2):
        for nm_, w in (("q", w_xq), ("k", w_xk), ("v", w_xv), ("o", w_xo)):
            row_sharded.append((f"x{nm_}{layer}", w[layer]))
    slots = {"in_ab": cast_into_slot("cast_in_ab", w_in_ab[0], place),
             "in_c": cast_into_slot("cast_in_c", w_in_c[0], place)}
    slots.update({nm_: cast_into_slot("cast_" + nm_, w, place) for nm_, w in row_sharded})

    small_sh = [conv_w[0], sgu_ln_g[0], sgu_ln_b[0]]
    gathered = exchange_small("gather_small", _pack(small_sh), reduce=False)
    per_chip = [_unpack(gathered[2 * j], [a.shape for a in small_sh]) for j in range(N_CHIPS)]
    conv_w_full = jnp.concatenate([p[0] for p in per_chip], axis=1)
    sgu_g_full = jnp.concatenate([p[1] for p in per_chip], axis=0).reshape(1, MIX)
    sgu_b_full = jnp.concatenate([p[2] for p in per_chip], axis=0).reshape(1, MIX)
    cw_pad = jnp.pad(conv_w_full, ((0, CONV_HALO - CONV_WIDTH), (0, 0)))
    cb = conv_b.reshape(1, D)
    clg, clb = conv_ln_g.reshape(1, D), conv_ln_b.reshape(1, D)
    ws = w_s[0]
    bst = jnp.transpose(b_s[0])
    tq = _attn_tq(S)
    bm = band_bias_table(rel_bias[0], tq)

    hn0 = rms_fwd("rms_mix0", xs, norm_mix_g[0])
    near, far, every = (0, 1), (2,), (0, 1, 2)
    proj0, (wab4,) = proj_cols_own("proj_ab_own", hn0, w_in_ab[0], place,
                                   comm=gather_job([slots["in_ab"]], relay_frac=1.0, flips=[near]))
    proj0, (wab4, w_out_ab4) = proj_cols_rest(
        "proj_ab_near", hn0, wab4, proj0, place, (2, 1),
        comm=gather_job([wab4, slots["out_ab"]], relay_frac=0.85, flips=[far, every]))
    proj0, got_qk = proj_cols_rest("proj_ab_far", hn0, wab4, proj0, place, (3,),
                                   comm=gather_job([slots["xq0"], slots["xk0"]]))
    (ya, lse_a), (wc4,) = attn_fwd(proj0, bm, D, comm=gather_job([slots["in_c"]], flips=[near]))
    (y0, cpre), (wc4, *got_b) = conv_gate_fwd(
        proj0, ya, cw_pad, cb, clg, clb, D,
        comm=gather_job([wc4, slots["xv0"], slots["xo0"], slots["out_c"], slots["xq1"]],
                        flips=[far, every, every, every, every]))
    h1, got_c = mm_nn("out_ab", y0, w_out_ab4.reshape(-1, D), F32, res=xs,
                      comm=gather_job([slots["xk1"], slots["xv1"], slots["xo1"]]))
    got = dict(zip(["xq0", "xk0", "xv0", "xo0", "out_c", "xq1", "xk1", "xv1", "xo1"], got_qk + got_b + got_c))
    wrow = {n: g.reshape(-1, g.shape[2]) for n, g in got.items()}
    wrow["out_ab"] = w_out_ab4.reshape(-1, D)
    wx = [{k: wrow[f"x{k}{layer}"] for k in "qkvo"} for layer in range(2)]
    h2, sx0 = _xattn_layer_fwd("0", h1, mems, norm_x_g[0], norm_mem_g[0], wx[0])
    hn1 = rms_fwd("rms_mix1", h2, norm_mix_g[1])
    proj1 = mm_nn_cols("proj_c", hn1, wc4, BF16)
    y1 = sgu_fwd(proj1, sgu_g_full, sgu_b_full, ws, bst, MIX)
    h3 = mm_nn("out_c", y1, wrow["out_c"], F32, res=h2)
    h4, sx1 = _xattn_layer_fwd("1", h3, mems, norm_x_g[1], norm_mem_g[1], wx[1])
    loss_row, dg_final, dh4, dh4b = loss_head("loss_head", h4, final_norm_g, tgt)

    dh3, dh3b, dgx1, dgmem1, dwx1 = _xattn_layer_bwd("1", dh4, dh4b, h3, mems, norm_x_g[1], norm_mem_g[1], wx[1], sx1)
    def stack_rows(dw_out, dwx):
        return jnp.concatenate([g.reshape(N_CHIPS, -1, g.shape[1]) for g in [dw_out] + [dwx[k] for k in "qkvo"]],
                               axis=1)

    dy1 = mm_nt("d_out_c", dh3b, wrow["out_c"], BF16)
    dw_out_c = mm_tn("dw_out_c", y1, dh3b)
    dproj1, dws, dbst, dsgu_g, dsgu_b = sgu_bwd(dy1, proj1, sgu_g_full, sgu_b_full, ws, bst, MIX)
    grp1 = stack_rows(dw_out_c, dwx1)
    dw_in_c, (sib1,) = mm_tn_cols("dw_in_c", hn1, dproj1, comm=sibling_halves_job([grp1]))
    part1 = add_halves("add_halves1", grp1, sib1, cidx)
    dhn1, (recv1, sib2) = mm_nt_cols("d_proj_c", dproj1, wc4, BF16,
                                     comm=_join(scatter_job([part1]), sibling_halves_job([dw_in_c])))
    part2 = add_halves("add_halves2", dw_in_c, sib2, cidx)
    dh2, dh2b, dgmix1 = rms_bwd("rms_mix1_bwd", h2, norm_mix_g[1], [dhn1], dh3)
    dh1, dh1b, dgx0, dgmem0, dwx0 = _xattn_layer_bwd("0", dh2, dh2b, h1, mems, norm_x_g[0], norm_mem_g[0], wx[0], sx0)
    dy0 = mm_nt("d_out_ab", dh1b, wrow["out_ab"], BF16)
    dw_out_ab = mm_tn("dw_out_ab", y0, dh1b)
    grp3 = stack_rows(dw_out_ab, dwx0)
    dya, dgate, dc, dclg, dclb = conv_gate_bwd_a(dy0, proj0, ya, cpre, clg, clb, D)
    (da, db, dcw, dcb), (recv2, sib3) = conv_gate_bwd_b(
        dc, proj0, cw_pad, D, comm=_join(scatter_job([part2]), sibling_halves_job([grp3])))
    part3 = add_halves("add_halves3", grp3, sib3, cidx)
    (dq, dkc, dkp, dvc, dvp, ds_sum), (recv3,) = attn_bwd(proj0, ya, dya, lse_a, bm, D, comm=scatter_job([part3]))
    drel = rel_bias_grad(ds_sum)
    dproj0 = assemble_dproj0(dq, dkc, dkp, dvc, dvp, da, db, dgate, D)
    dw_in_ab = mm_tn_cols("dw_in_ab", hn0, dproj0)
    (sib4,) = run_comm("sibling_halves4", sibling_halves_job([dw_in_ab]))
    part4 = add_halves("add_halves4", dw_in_ab, sib4, cidx)
    halves = [sum_chips(f"sum_chips{t + 1}", p, r, place)
              for t, (p, r) in enumerate(((part1, recv1), (part2, recv2), (part3, recv3)))]
    small_early = [
        jnp.concatenate([dgx0, dgx1], axis=0), jnp.concatenate([dgmem0, dgmem1], axis=0), dg_final.reshape(D),
        drel[None], dcb, dclg, dclb, dws[None], jnp.transpose(dbst)[None],
        dcw[:CONV_WIDTH][None], dsgu_g, dsgu_b]
    early = _pack(small_early, row_multiple=512)
    dhn0, (recv4, small_slots, g_r1, g_c, g_r0) = mm_nt_cols(
        "d_proj_ab", dproj0, wab4, BF16,
        comm=_join(scatter_job([part4]), exchange_job(early), share_halves_job(halves)))
    dx, _, dgmix0 = rms_bwd("rms_mix0_bwd", xs, norm_mix_g[0], [dhn0], dh1)
    (g_ab,) = run_comm("share_reduced_half4", share_halves_job([sum_chips("sum_chips4", part4, recv4, place)]))

    me = 4 * cx + 2 * cy + cc
    small_slots = lax.dynamic_update_slice(small_slots, early[None], (me, 0, 0))
    summed = _unpack(sum_devices("sum_small", small_slots), [a.shape for a in small_early])
    (g_norm_x, g_norm_mem, g_final, g_rel, g_conv_b, g_clg, g_clb, g_ws, g_bs,
     g_conv_w_full, g_sgu_g_full, g_sgu_b_full) = summed
    dgmix = jnp.concatenate([dgmix0, dgmix1], axis=0)
    (g_norm_mix,) = _unpack(exchange_small("reduce_late", _pack([dgmix]), reduce=True), [dgmix.shape])
    cws = conv_w.shape[2]
    g_conv_w = lax.dynamic_slice_in_dim(g_conv_w_full, chip * cws, cws, axis=2)
    sgs = sgu_ln_g.shape[1]
    g_sgu_g = lax.dynamic_slice_in_dim(g_sgu_g_full, chip * sgs, sgs, axis=1)
    g_sgu_b = lax.dynamic_slice_in_dim(g_sgu_b_full, chip * sgs, sgs, axis=1)

    loss = lax.psum(loss_row[0, 0], ("x", "y", "c"))

    big_grads = {"w_in_ab": ([g_ab], 0), "w_in_c": ([g_c], 0), "w_out_ab": ([g_r0], 0), "w_out_c": ([g_r1], 0)}
    for i, nm_ in enumerate("qkvo"):
        big_grads["w_x" + nm_] = ([g_r0, g_r1], ro + i * rq)
    grads = dict(
        norm_mix_g=g_norm_mix, norm_x_g=g_norm_x, norm_mem_g=g_norm_mem, final_norm_g=g_final,
        rel_bias=g_rel, conv_w=g_conv_w, conv_b=g_conv_b, conv_ln_g=g_clg, conv_ln_b=g_clb,
        sgu_ln_g=g_sgu_g, sgu_ln_b=g_sgu_b, w_s=g_ws, b_s=g_bs)
    weights = dict(
        norm_mix_g=(norm_mix_g, m_norm_mix_g, v_norm_mix_g), norm_x_g=(norm_x_g, m_norm_x_g, v_norm_x_g),
        norm_mem_g=(norm_mem_g, m_norm_mem_g, v_norm_mem_g), final_norm_g=(final_norm_g, m_final_norm_g, v_final_norm_g),
        w_in_ab=(w_in_ab, m_w_in_ab, v_w_in_ab), rel_bias=(rel_bias, m_rel_bias, v_rel_bias),
        conv_w=(conv_w, m_conv_w, v_conv_w), conv_b=(conv_b, m_conv_b, v_conv_b),
        conv_ln_g=(conv_ln_g, m_conv_ln_g, v_conv_ln_g), conv_ln_b=(conv_ln_b, m_conv_ln_b, v_conv_ln_b),
        w_out_ab=(w_out_ab, m_w_out_ab, v_w_out_ab), w_in_c=(w_in_c, m_w_in_c, v_w_in_c),
        sgu_ln_g=(sgu_ln_g, m_sgu_ln_g, v_sgu_ln_g), sgu_ln_b=(sgu_ln_b, m_sgu_ln_b, v_sgu_ln_b),
        w_s=(w_s, m_w_s, v_w_s), b_s=(b_s, m_b_s, v_b_s), w_out_c=(w_out_c, m_w_out_c, v_w_out_c),
        w_xq=(w_xq, m_w_xq, v_w_xq), w_xk=(w_xk, m_w_xk, v_w_xk), w_xv=(w_xv, m_w_xv, v_w_xv),
        w_xo=(w_xo, m_w_xo, v_w_xo))
    names = list(weights)
    delta, new_m, new_v = {}, {}, {}
    for nm_, (groups, row_off) in big_grads.items():
        w, m, v = weights[nm_]
        grads[nm_], delta[nm_], new_m[nm_], new_v[nm_] = adamw_rows("adamw_" + nm_, w, m, v, groups, row_off)
    small_names = [n for n in names if n not in big_grads]
    stepped = adamw_many("adamw_small", [(weights[n][0], grads[n].reshape(weights[n][0].shape), weights[n][1],
                                          weights[n][2]) for n in small_names])
    for n, (d_, m_, v_) in zip(small_names, stepped):
        delta[n], new_m[n], new_v[n] = d_, m_, v_

    return (loss, dx[None], *[grads[n].reshape(weights[n][0].shape) for n in names], *[delta[n] for n in names],
            *[new_m[n] for n in names], *[new_v[n] for n in names])
```
